```python
import jax, jax.numpy as jnp
from jax import lax
import numpy as np

D_MODEL = 1024
BATCH = 8
SEQ = 8192
DEPTH = 1

CHUNK = 64
N_META = 16
Q_BLOCK = 128
HEAD_DIM = 64
N_HEADS_FOX = 8
N_HEADS_SB = 8
WIDTH_FOX = N_HEADS_FOX * HEAD_DIM
WIDTH_SB = N_HEADS_SB * HEAD_DIM
D_FF = 2816
CONV_WIDTH = 3
EPS = 1e-6

SPLIT_SIZES = (WIDTH_FOX, WIDTH_FOX, WIDTH_FOX, N_HEADS_FOX,
               WIDTH_SB, WIDTH_SB, WIDTH_SB, D_MODEL, D_MODEL)
PROJ_WIDTH = sum(SPLIT_SIZES)
SPLIT_POINTS = tuple(sum(SPLIT_SIZES[:i + 1]) for i in range(len(SPLIT_SIZES) - 1))

kernel_name = "hybrid_fox_stickbreak_convffn_block"


def rms_norm(x, g):
    xf = x.astype(jnp.float32)
    y = xf * lax.rsqrt(jnp.mean(xf * xf, axis=-1, keepdims=True) + EPS)
    return (y * g.astype(jnp.float32)).astype(x.dtype)


def _split_heads(t, n_heads):
    b, l, _ = t.shape
    return t.reshape(b, l, n_heads, HEAD_DIM).transpose(0, 2, 1, 3)


def _merge_heads(t):
    b, h, l, d = t.shape
    return t.transpose(0, 2, 1, 3).reshape(b, l, h * d)


def forgetting_attention(q, k, v, log_f):
    seq_len = q.shape[2]
    scale = HEAD_DIM ** -0.5
    c = jnp.cumsum(log_f, axis=-1)
    outs = []
    for start in range(0, seq_len, Q_BLOCK):
        end = min(start + Q_BLOCK, seq_len)
        s = jnp.einsum('bhqd,bhkd->bhqk', q[:, :, start:end], k[:, :, :end],
                       preferred_element_type=jnp.float32) * scale
        s = s + c[:, :, start:end, None] - c[:, :, None, :end]
        t_pos = jnp.arange(start, end)[:, None]
        s_pos = jnp.arange(end)[None, :]
        s = jnp.where(s_pos <= t_pos, s, -jnp.inf)
        p = jax.nn.softmax(s, axis=-1)
        outs.append(jnp.einsum('bhqk,bhkd->bhqd', p.astype(v.dtype), v[:, :, :end]))
    return jnp.concatenate(outs, axis=2)


def stick_breaking_attention(q, k, v):
    seq_len = q.shape[2]
    scale = HEAD_DIM ** -0.5
    outs = []
    for start in range(0, seq_len, Q_BLOCK):
        end = min(start + Q_BLOCK, seq_len)
        z = jnp.einsum('bhqd,bhkd->bhqk', q[:, :, start:end], k[:, :, :end],
                       preferred_element_type=jnp.float32) * scale
        t_pos = jnp.arange(start, end)[:, None]
        s_pos = jnp.arange(end)[None, :]
        causal = s_pos < t_pos
        log_keep = jnp.where(causal, jax.nn.log_sigmoid(-z), 0.0)
        later = lax.cumsum(log_keep, axis=3, reverse=True) - log_keep
        a = jnp.where(causal, jnp.exp(jax.nn.log_sigmoid(z) + later), 0.0)
        outs.append(jnp.einsum('bhqk,bhkd->bhqd', a.astype(v.dtype), v[:, :, :end]))
    return jnp.concatenate(outs, axis=2)


def causal_depthwise_conv(u, w, b):
    seq_len = u.shape[1]
    up = jnp.pad(u, ((0, 0), (CONV_WIDTH - 1, 0), (0, 0)))
    out = b.astype(u.dtype)
    for i in range(CONV_WIDTH):
        out = out + w[i] * up[:, i:i + seq_len]
    return out


def _fwd_setup_inputs(seed: int = 0) -> dict:
    key = jax.random.key(seed)
    ks = jax.random.split(key, 13)
    x = jax.random.normal(ks[0], (BATCH, SEQ, D_MODEL), jnp.float32)
    meta_tokens = jax.random.normal(ks[1], (N_META, D_MODEL), jnp.float32)
    norm_gains = 1.0 + 0.05 * jax.random.normal(ks[2], (DEPTH, 4, D_MODEL), jnp.float32)
    w_in = jax.random.normal(ks[3], (DEPTH, D_MODEL, PROJ_WIDTH), jnp.float32) * D_MODEL ** -0.5
    b_forget = jax.random.uniform(ks[4], (DEPTH, N_HEADS_FOX), jnp.float32, minval=1.0, maxval=6.0)
    w_o_fox = jax.random.normal(ks[5], (DEPTH, WIDTH_FOX, D_MODEL), jnp.float32) * WIDTH_FOX ** -0.5
    w_o_sb = jax.random.normal(ks[6], (DEPTH, WIDTH_SB, D_MODEL), jnp.float32) * WIDTH_SB ** -0.5
    w_out = jax.random.normal(ks[7], (DEPTH, D_MODEL, D_MODEL), jnp.float32) * D_MODEL ** -0.5
    w_up = jax.random.normal(ks[8], (DEPTH, D_MODEL, 2 * D_FF), jnp.float32) * D_MODEL ** -0.5
    conv_w = jax.random.normal(ks[9], (DEPTH, CONV_WIDTH, 2 * D_FF), jnp.float32) * CONV_WIDTH ** -0.5
    conv_b = 0.02 * jax.random.normal(ks[10], (DEPTH, 2 * D_FF), jnp.float32)
    w_down = jax.random.normal(ks[11], (DEPTH, D_FF, D_MODEL), jnp.float32) * D_FF ** -0.5
    return {"x": x, "meta_tokens": meta_tokens, "norm_gains": norm_gains, "w_in": w_in,
            "b_forget": b_forget, "w_o_fox": w_o_fox, "w_o_sb": w_o_sb, "w_out": w_out,
            "w_up": w_up, "conv_w": conv_w, "conv_b": conv_b, "w_down": w_down}


def _fwd_reference(x, meta_tokens, norm_gains, w_in, b_forget, w_o_fox, w_o_sb, w_out,
              w_up, conv_w, conv_b, w_down):
    batch = x.shape[0]
    meta = jnp.broadcast_to(meta_tokens[None].astype(x.dtype), (batch, N_META, D_MODEL))
    h = jnp.concatenate([meta, x], axis=1)
    for layer in range(DEPTH):
        xn = rms_norm(h, norm_gains[layer, 0])
        proj = xn @ w_in[layer]
        q_a, k_a, v_a, f_a, q_b, k_b, v_b, g_a, g_b = jnp.split(proj, SPLIT_POINTS, axis=-1)
        log_f = jax.nn.log_sigmoid((f_a + b_forget[layer]).astype(jnp.float32))
        o_a = forgetting_attention(_split_heads(q_a, N_HEADS_FOX), _split_heads(k_a, N_HEADS_FOX),
                                   _split_heads(v_a, N_HEADS_FOX), log_f.transpose(0, 2, 1))
        o_b = stick_breaking_attention(_split_heads(q_b, N_HEADS_SB), _split_heads(k_b, N_HEADS_SB),
                                       _split_heads(v_b, N_HEADS_SB))
        y_a = _merge_heads(o_a) @ w_o_fox[layer]
        y_b = _merge_heads(o_b) @ w_o_sb[layer]
        mixed = (jax.nn.sigmoid(g_a) * y_a + jax.nn.sigmoid(g_b) * y_b) @ w_out[layer]
        h = h + rms_norm(mixed, norm_gains[layer, 1])
        xn = rms_norm(h, norm_gains[layer, 2])
        u = causal_depthwise_conv(xn @ w_up[layer], conv_w[layer], conv_b[layer])
        u_gate, u_val = jnp.split(u, 2, axis=-1)
        ffn = (jax.nn.gelu(u_gate, approximate=True) * u_val) @ w_down[layer]
        h = h + rms_norm(ffn, norm_gains[layer, 3])
    return h[:, N_META:]


import jax as _jax
import jax.numpy as _jnp

TWIN_FORMAT = 'train_step'
FWD_PARAMS = ['x', 'meta_tokens', 'norm_gains', 'w_in', 'b_forget', 'w_o_fox', 'w_o_sb', 'w_out', 'w_up', 'conv_w', 'conv_b', 'w_down']
TWIN_WEIGHTS = ['meta_tokens', 'norm_gains', 'w_in', 'b_forget', 'w_o_fox', 'w_o_sb', 'w_out', 'w_up', 'conv_w', 'conv_b', 'w_down']
TWIN_DIFF_INPUT = 'x'
TWIN_INPUTS = ['x', 'meta_tokens', 'norm_gains', 'w_in', 'b_forget', 'w_o_fox', 'w_o_sb', 'w_out', 'w_up', 'conv_w', 'conv_b', 'w_down', 'loss_target', 'm_meta_tokens', 'm_norm_gains', 'm_w_in', 'm_b_forget', 'm_w_o_fox', 'm_w_o_sb', 'm_w_out', 'm_w_up', 'm_conv_w', 'm_conv_b', 'm_w_down', 'v_meta_tokens', 'v_norm_gains', 'v_w_in', 'v_b_forget', 'v_w_o_fox', 'v_w_o_sb', 'v_w_out', 'v_w_up', 'v_conv_w', 'v_conv_b', 'v_w_down']
TWIN_OUTPUTS = ['loss', 'grad_x', 'grad_meta_tokens', 'grad_norm_gains', 'grad_w_in', 'grad_b_forget', 'grad_w_o_fox', 'grad_w_o_sb', 'grad_w_out', 'grad_w_up', 'grad_conv_w', 'grad_conv_b', 'grad_w_down', 'delta_meta_tokens', 'delta_norm_gains', 'delta_w_in', 'delta_b_forget', 'delta_w_o_fox', 'delta_w_o_sb', 'delta_w_out', 'delta_w_up', 'delta_conv_w', 'delta_conv_b', 'delta_w_down', 'new_m_meta_tokens', 'new_m_norm_gains', 'new_m_w_in', 'new_m_b_forget', 'new_m_w_o_fox', 'new_m_w_o_sb', 'new_m_w_out', 'new_m_w_up', 'new_m_conv_w', 'new_m_conv_b', 'new_m_w_down', 'new_v_meta_tokens', 'new_v_norm_gains', 'new_v_w_in', 'new_v_b_forget', 'new_v_w_o_fox', 'new_v_w_o_sb', 'new_v_w_out', 'new_v_w_up', 'new_v_conv_w', 'new_v_conv_b', 'new_v_w_down']
TWIN_LEAF_KINDS = {'loss': 'loss', 'grad_x': 'grad_x', 'grad_meta_tokens': 'grad_w', 'grad_norm_gains': 'grad_w', 'grad_w_in': 'grad_w', 'grad_b_forget': 'grad_w', 'grad_w_o_fox': 'grad_w', 'grad_w_o_sb': 'grad_w', 'grad_w_out': 'grad_w', 'grad_w_up': 'grad_w', 'grad_conv_w': 'grad_w', 'grad_conv_b': 'grad_w', 'grad_w_down': 'grad_w', 'delta_meta_tokens': 'delta_w', 'delta_norm_gains': 'delta_w', 'delta_w_in': 'delta_w', 'delta_b_forget': 'delta_w', 'delta_w_o_fox': 'delta_w', 'delta_w_o_sb': 'delta_w', 'delta_w_out': 'delta_w', 'delta_w_up': 'delta_w', 'delta_conv_w': 'delta_w', 'delta_conv_b': 'delta_w', 'delta_w_down': 'delta_w', 'new_m_meta_tokens': 'new_m', 'new_m_norm_gains': 'new_m', 'new_m_w_in': 'new_m', 'new_m_b_forget': 'new_m', 'new_m_w_o_fox': 'new_m', 'new_m_w_o_sb': 'new_m', 'new_m_w_out': 'new_m', 'new_m_w_up': 'new_m', 'new_m_conv_w': 'new_m', 'new_m_conv_b': 'new_m', 'new_m_w_down': 'new_m', 'new_v_meta_tokens': 'new_v', 'new_v_norm_gains': 'new_v', 'new_v_w_in': 'new_v', 'new_v_b_forget': 'new_v', 'new_v_w_o_fox': 'new_v', 'new_v_w_o_sb': 'new_v', 'new_v_w_out': 'new_v', 'new_v_w_up': 'new_v', 'new_v_conv_w': 'new_v', 'new_v_conv_b': 'new_v', 'new_v_w_down': 'new_v'}


def _forward(args):
    return _fwd_reference(*[args[k] for k in FWD_PARAMS])


def _output_shape():
    def fwd():
        inp = _fwd_setup_inputs(0)
        return _fwd_reference(*[inp[k] for k in FWD_PARAMS])
    out = _jax.eval_shape(fwd)
    return out.shape, out.dtype

N_MICROBATCH = 1
ADAM_LR = 0.001
ADAM_B1 = 0.9
ADAM_B2 = 0.999
ADAM_EPS = 1e-08
ADAM_WD = 0.01
ADAM_STEP = 10
PER_EXAMPLE_BATCH_AXIS = {'x': 0, 'loss_target': 0}
SHARED_INPUTS = []
_WEIGHT_DTYPES = {'meta_tokens': _jnp.float32, 'norm_gains': _jnp.float32, 'w_in': _jnp.float32, 'b_forget': _jnp.float32, 'w_o_fox': _jnp.float32, 'w_o_sb': _jnp.float32, 'w_out': _jnp.float32, 'w_up': _jnp.float32, 'conv_w': _jnp.float32, 'conv_b': _jnp.float32, 'w_down': _jnp.float32}
MOMENT_SCALE = {'meta_tokens': 2.923970e-02, 'norm_gains': 4.519806e+01, 'w_in': 3.983480e-01, 'b_forget': 4.929045e+00, 'w_o_fox': 3.792884e-01, 'w_o_sb': 6.873987e-01, 'w_out': 8.602961e-01, 'w_up': 3.650062e-01, 'conv_w': 4.345758e-01, 'conv_b': 7.541447e-01, 'w_down': 7.720383e-01}


def _to_microbatches(a, axis):
    t = _jnp.moveaxis(a, axis, 0)
    t = t.reshape((N_MICROBATCH, t.shape[0] // N_MICROBATCH) + t.shape[1:])
    return _jnp.moveaxis(t, 1, axis + 1)


def setup_inputs(seed: int = 0) -> dict:
    inp = _fwd_setup_inputs(seed)
    key = _jax.random.fold_in(_jax.random.key(seed), 7919)
    shape, _ = _output_shape()
    out = dict(inp)
    out["loss_target"] = _jax.random.normal(_jax.random.fold_in(key, 0), shape, _jnp.float32)
    for i, name in enumerate(TWIN_WEIGHTS):
        w = inp[name].astype(_jnp.float32)
        if MOMENT_SCALE is None:
            s = _jnp.sqrt(_jnp.mean(_jnp.square(w)) + 1e-30)
        else:
            s = MOMENT_SCALE[name]
        km, kv = _jax.random.split(_jax.random.fold_in(key, i + 1))
        out[name] = w
        out["m_" + name] = s * _jax.random.normal(km, w.shape, _jnp.float32)
        out["v_" + name] = (s * s) * _jax.random.uniform(kv, w.shape, _jnp.float32, 0.5, 1.5)
    if N_MICROBATCH > 1:
        for name, axis in PER_EXAMPLE_BATCH_AXIS.items():
            out[name] = _to_microbatches(out[name], axis)
    return {'x': out['x'], 'meta_tokens': out['meta_tokens'], 'norm_gains': out['norm_gains'], 'w_in': out['w_in'], 'b_forget': out['b_forget'], 'w_o_fox': out['w_o_fox'], 'w_o_sb': out['w_o_sb'], 'w_out': out['w_out'], 'w_up': out['w_up'], 'conv_w': out['conv_w'], 'conv_b': out['conv_b'], 'w_down': out['w_down'], 'loss_target': out['loss_target'], 'm_meta_tokens': out['m_meta_tokens'], 'm_norm_gains': out['m_norm_gains'], 'm_w_in': out['m_w_in'], 'm_b_forget': out['m_b_forget'], 'm_w_o_fox': out['m_w_o_fox'], 'm_w_o_sb': out['m_w_o_sb'], 'm_w_out': out['m_w_out'], 'm_w_up': out['m_w_up'], 'm_conv_w': out['m_conv_w'], 'm_conv_b': out['m_conv_b'], 'm_w_down': out['m_w_down'], 'v_meta_tokens': out['v_meta_tokens'], 'v_norm_gains': out['v_norm_gains'], 'v_w_in': out['v_w_in'], 'v_b_forget': out['v_b_forget'], 'v_w_o_fox': out['v_w_o_fox'], 'v_w_o_sb': out['v_w_o_sb'], 'v_w_out': out['v_w_out'], 'v_w_up': out['v_w_up'], 'v_conv_w': out['v_conv_w'], 'v_conv_b': out['v_conv_b'], 'v_w_down': out['v_w_down']}


def _loss(weights, diff, rest, loss_target):
    with _jax.named_scope("forward"):
        args = {**rest, TWIN_DIFF_INPUT: diff, **{k: w.astype(_WEIGHT_DTYPES[k]) for k, w in weights.items()}}
        y = _forward(args)
    with _jax.named_scope("loss_head"):
        err = _jnp.square(y.astype(_jnp.float32) - loss_target)
        return 0.5 * _jnp.sum(_jnp.mean(err, axis=-1)) if err.ndim else 0.5 * err


def _adamw(w, g, m, v):
    m = ADAM_B1 * m + (1.0 - ADAM_B1) * g
    v = ADAM_B2 * v + (1.0 - ADAM_B2) * _jnp.square(g)
    m_hat = m / (1.0 - ADAM_B1 ** ADAM_STEP)
    v_hat = v / (1.0 - ADAM_B2 ** ADAM_STEP)
    delta = -ADAM_LR * (m_hat / (_jnp.sqrt(v_hat) + ADAM_EPS) + ADAM_WD * w)
    return delta, m, v


def reference(x, meta_tokens, norm_gains, w_in, b_forget, w_o_fox, w_o_sb, w_out, w_up, conv_w, conv_b, w_down, loss_target, m_meta_tokens, m_norm_gains, m_w_in, m_b_forget, m_w_o_fox, m_w_o_sb, m_w_out, m_w_up, m_conv_w, m_conv_b, m_w_down, v_meta_tokens, v_norm_gains, v_w_in, v_b_forget, v_w_o_fox, v_w_o_sb, v_w_out, v_w_up, v_conv_w, v_conv_b, v_w_down):
    given = dict(x=x, meta_tokens=meta_tokens, norm_gains=norm_gains, w_in=w_in, b_forget=b_forget, w_o_fox=w_o_fox, w_o_sb=w_o_sb, w_out=w_out, w_up=w_up, conv_w=conv_w, conv_b=conv_b, w_down=w_down, loss_target=loss_target, m_meta_tokens=m_meta_tokens, m_norm_gains=m_norm_gains, m_w_in=m_w_in, m_b_forget=m_b_forget, m_w_o_fox=m_w_o_fox, m_w_o_sb=m_w_o_sb, m_w_out=m_w_out, m_w_up=m_w_up, m_conv_w=m_conv_w, m_conv_b=m_conv_b, m_w_down=m_w_down, v_meta_tokens=v_meta_tokens, v_norm_gains=v_norm_gains, v_w_in=v_w_in, v_b_forget=v_b_forget, v_w_o_fox=v_w_o_fox, v_w_o_sb=v_w_o_sb, v_w_out=v_w_out, v_w_up=v_w_up, v_conv_w=v_conv_w, v_conv_b=v_conv_b, v_w_down=v_w_down)
    weights = {n: given[n] for n in TWIN_WEIGHTS}
    shared = {n: given[n] for n in SHARED_INPUTS}
    per_example = {n: given[n] for n in ['x']}
    grad_fn = _jax.value_and_grad(_loss, argnums=(0, 1))

    def one_microbatch(ex, loss_target):
        ex = dict(ex)
        diff = ex.pop(TWIN_DIFF_INPUT)
        return grad_fn(weights, diff, {**shared, **ex}, loss_target)

    if N_MICROBATCH == 1:
        loss, (grad_w, grad_x) = one_microbatch(per_example, given["loss_target"])
    else:
        def body(carry, xs):
            loss_sum, grad_sum = carry
            l_k, (gw_k, gx_k) = one_microbatch(xs[0], xs[1])
            with _jax.named_scope("update"):
                return (loss_sum + l_k, _jax.tree.map(_jnp.add, grad_sum, gw_k)), gx_k

        init = (_jnp.zeros((), _jnp.float32), _jax.tree.map(_jnp.zeros_like, weights))
        (loss, grad_w), grad_x = _jax.lax.scan(body, init, (per_example, given["loss_target"]))
    with _jax.named_scope("update"):
        delta_w, new_m, new_v = {}, {}, {}
        for n in TWIN_WEIGHTS:
            delta_w[n], new_m[n], new_v[n] = _adamw(weights[n], grad_w[n], given["m_" + n], given["v_" + n])
    return (loss, grad_x, *[grad_w[n] for n in TWIN_WEIGHTS], *[delta_w[n] for n in TWIN_WEIGHTS],
            *[new_m[n] for n in TWIN_WEIGHTS], *[new_v[n] for n in TWIN_WEIGHTS])
```

```python
import functools
import math

import jax
import jax.numpy as jnp
from jax import lax
from jax.experimental import pallas as pl
from jax.experimental.pallas import tpu as pltpu

F32 = jnp.float32
BF16 = jnp.bfloat16

N_DEV = 8
N_META = 16
HEAD_DIM = 64
N_HEADS = 8
WIDTH = N_HEADS * HEAD_DIM
N_PAIRS = N_HEADS // 2
LANES = 128
SUBLANES = 8
EPS = 1e-6
ATT_SCALE = HEAD_DIM ** -0.5
BLK = 256
F_PAD = 256
VMEM_LIMIT = 48 << 20

ADAM_LR = 0.001
ADAM_B1 = 0.9
ADAM_B2 = 0.999
ADAM_EPS = 1e-08
ADAM_WD = 0.01
ADAM_STEP = 10

GELU_C = math.sqrt(2.0 / math.pi)
GELU_A = 0.044715


def _params(sem, vmem=VMEM_LIMIT):
    return pltpu.CompilerParams(dimension_semantics=sem, vmem_limit_bytes=vmem)


def _tile(dim, cap, align=LANES):
    t = (min(cap, dim) // align) * align
    while t >= align:
        if dim % t == 0:
            return t
        t -= align
    return dim


def _log_sigmoid_parts(z):
    lp = jnp.log1p(jnp.exp(-jnp.abs(z)))
    return jnp.minimum(z, 0.0) - lp, jnp.minimum(-z, 0.0) - lp


def _sigmoid(x):
    return 1.0 / (1.0 + jnp.exp(-x))


def _split_bf16(x):
    hi = x.astype(BF16)
    lo = (x - hi.astype(F32)).astype(BF16)
    return hi, lo


def _dot(a, b, dims):
    return lax.dot_general(a, b, (dims, ((), ())), preferred_element_type=F32)


NN = ((1,), (0,))
NT = ((1,), (1,))
TN = ((0,), (0,))


def _mm(a, b, mode, out_dtype, name):
    if mode == "nn":
        (m, kc), (_, n) = a.shape, b.shape
    elif mode == "nt":
        (m, kc), (n, _) = a.shape, b.shape
    else:
        (kc, m), (_, n) = a.shape, b.shape
    tm = _tile(m, 768)
    tn = _tile(n, 1024)
    tk = _tile(kc, 1536 if mode != "tn" else 768)
    nk = kc // tk
    dims = {"nn": NN, "nt": NT, "tn": TN}[mode]

    def body(a_ref, b_ref, o_ref, acc_ref):
        k = pl.program_id(2)

        @pl.when(k == 0)
        def _():
            acc_ref[...] = jnp.zeros_like(acc_ref)

        acc_ref[...] += _dot(a_ref[...].astype(BF16), b_ref[...].astype(BF16), dims)

        @pl.when(k == nk - 1)
        def _():
            o_ref[...] = acc_ref[...].astype(out_dtype)

    if mode == "tn":
        a_spec = pl.BlockSpec((tk, tm), lambda j, i, k: (k, i))
    else:
        a_spec = pl.BlockSpec((tm, tk), lambda j, i, k: (i, k))
    if mode == "nt":
        b_spec = pl.BlockSpec((tn, tk), lambda j, i, k: (j, k))
    else:
        b_spec = pl.BlockSpec((tk, tn), lambda j, i, k: (k, j))
    return pl.pallas_call(
        body, name=name,
        out_shape=jax.ShapeDtypeStruct((m, n), out_dtype),
        grid=(n // tn, m // tm, nk),
        in_specs=[a_spec, b_spec],
        out_specs=pl.BlockSpec((tm, tn), lambda j, i, k: (i, j)),
        scratch_shapes=[pltpu.VMEM((tm, tn), F32)],
        compiler_params=_params(("parallel", "parallel", "arbitrary")),
    )(a, b)


def _rmsnorm_fwd(x, g, name):
    lp, d = x.shape

    def body(x_ref, g_ref, o_ref):
        xv = x_ref[...]
        r = lax.rsqrt(jnp.mean(xv * xv, axis=-1, keepdims=True) + EPS)
        o_ref[...] = ((xv * r) * g_ref[...]).astype(BF16)

    return pl.pallas_call(
        body, name=name,
        out_shape=jax.ShapeDtypeStruct((lp, d), BF16),
        grid=(lp // BLK,),
        in_specs=[pl.BlockSpec((BLK, d), lambda i: (i, 0)), pl.BlockSpec((1, d), lambda i: (0, 0))],
        out_specs=pl.BlockSpec((BLK, d), lambda i: (i, 0)),
        compiler_params=_params(("parallel",)),
    )(x, g)


def _rmsnorm_bwd(x, g, dy, resid, out_dtype, name):
    lp, d = x.shape
    has_resid = resid is not None

    def body(*refs):
        if has_resid:
            x_ref, g_ref, dy_ref, r_ref, dx_ref, dg_ref = refs
        else:
            x_ref, g_ref, dy_ref, dx_ref, dg_ref = refs
        i = pl.program_id(0)
        xv = x_ref[...]
        dyv = dy_ref[...].astype(F32)
        r = lax.rsqrt(jnp.mean(xv * xv, axis=-1, keepdims=True) + EPS)
        xh = xv * r
        dyg = dyv * g_ref[...]
        dx = r * (dyg - xh * jnp.mean(dyg * xh, axis=-1, keepdims=True))
        if has_resid:
            dx = dx + r_ref[...]
        dx_ref[...] = dx.astype(out_dtype)

        @pl.when(i == 0)
        def _():
            dg_ref[...] = jnp.zeros_like(dg_ref)

        dg_ref[...] += jnp.sum(dyv * xh, axis=0, keepdims=True)

    row = pl.BlockSpec((BLK, d), lambda i: (i, 0))
    vec = pl.BlockSpec((1, d), lambda i: (0, 0))
    ins = [x, g, dy] + ([resid] if has_resid else [])
    in_specs = [row, vec, row] + ([row] if has_resid else [])
    return pl.pallas_call(
        body, name=name,
        out_shape=(jax.ShapeDtypeStruct((lp, d), out_dtype), jax.ShapeDtypeStruct((1, d), F32)),
        grid=(lp // BLK,),
        in_specs=in_specs,
        out_specs=(row, vec),
        compiler_params=_params(("arbitrary",)),
    )(*ins)


def _forget_fwd(fpre, b_pad):
    lp = fpre.shape[0]

    def body(f_ref, b_ref, c_ref, carry_ref):
        i = pl.program_id(0)

        @pl.when(i == 0)
        def _():
            carry_ref[...] = jnp.zeros_like(carry_ref)

        logf, _ = _log_sigmoid_parts(f_ref[...] + b_ref[...])
        row = lax.broadcasted_iota(jnp.int32, (BLK, BLK), 0)
        col = lax.broadcasted_iota(jnp.int32, (BLK, BLK), 1)
        tri = (col <= row).astype(BF16)
        p0 = logf.astype(BF16)
        r1 = logf - p0.astype(F32)
        p1 = r1.astype(BF16)
        p2 = (r1 - p1.astype(F32)).astype(BF16)
        c = _dot(tri, p0, NN) + _dot(tri, p1, NN) + _dot(tri, p2, NN) + carry_ref[0:1, :]
        c_ref[...] = c
        carry_ref[...] = jnp.broadcast_to(c[BLK - 1:BLK, :], carry_ref.shape)

    return pl.pallas_call(
        body, name="forget_fwd",
        out_shape=jax.ShapeDtypeStruct((lp, LANES), F32),
        grid=(lp // BLK,),
        in_specs=[pl.BlockSpec((BLK, LANES), lambda i: (i, 0)), pl.BlockSpec((1, LANES), lambda i: (0, 0))],
        out_specs=pl.BlockSpec((BLK, LANES), lambda i: (i, 0)),
        scratch_shapes=[pltpu.VMEM((SUBLANES, LANES), F32)],
        compiler_params=_params(("arbitrary",)),
    )(fpre, b_pad)


def _forget_bwd(dc, fpre, b_pad):
    lp = fpre.shape[0]
    nb = lp // BLK

    def body(dc_ref, f_ref, b_ref, df_ref, db_ref, carry_ref):
        i = pl.program_id(0)

        @pl.when(i == 0)
        def _():
            carry_ref[...] = jnp.zeros_like(carry_ref)
            db_ref[...] = jnp.zeros_like(db_ref)

        dcv = dc_ref[...]
        row = lax.broadcasted_iota(jnp.int32, (BLK, BLK), 0)
        col = lax.broadcasted_iota(jnp.int32, (BLK, BLK), 1)
        tri = (col >= row).astype(BF16)
        p0 = dcv.astype(BF16)
        r1 = dcv - p0.astype(F32)
        p1 = r1.astype(BF16)
        p2 = (r1 - p1.astype(F32)).astype(BF16)
        dlogf = _dot(tri, p0, NN) + _dot(tri, p1, NN) + _dot(tri, p2, NN) + carry_ref[0:1, :]
        carry_ref[...] = jnp.broadcast_to(dlogf[0:1, :], carry_ref.shape)
        _, ls_neg = _log_sigmoid_parts(f_ref[...] + b_ref[...])
        df = dlogf * jnp.exp(ls_neg)
        df_ref[...] = df
        db_ref[...] += jnp.sum(df, axis=0, keepdims=True)

    rev = pl.BlockSpec((BLK, LANES), lambda i: (nb - 1 - i, 0))
    vec = pl.BlockSpec((1, LANES), lambda i: (0, 0))
    return pl.pallas_call(
        body, name="forget_bwd",
        out_shape=(jax.ShapeDtypeStruct((lp, LANES), F32), jax.ShapeDtypeStruct((1, LANES), F32)),
        grid=(nb,),
        in_specs=[rev, rev, vec],
        out_specs=(rev, vec),
        scratch_shapes=[pltpu.VMEM((SUBLANES, LANES), F32)],
        compiler_params=_params(("arbitrary",)),
    )(dc, fpre, b_pad)


def _mix_fwd(o_a, o_b, gates, h0, w_fox, w_sb, w_out, g1):
    lp, d = h0.shape

    def body(oa_ref, ob_ref, ga_ref, gb_ref, h_ref, wf_ref, ws_ref, wo_ref, g_ref,
             h1_ref, ya_ref, yb_ref, gated_ref, mixed_ref):
        ya = _dot(oa_ref[...].astype(BF16), wf_ref[...], NN)
        yb = _dot(ob_ref[...].astype(BF16), ws_ref[...], NN)
        gated = _sigmoid(ga_ref[...]) * ya + _sigmoid(gb_ref[...]) * yb
        gb16 = gated.astype(BF16)
        mixed = _dot(gb16, wo_ref[...], NN)
        r = lax.rsqrt(jnp.mean(mixed * mixed, axis=-1, keepdims=True) + EPS)
        h1_ref[...] = h_ref[...] + (mixed * r) * g_ref[...]
        ya_ref[...] = ya
        yb_ref[...] = yb
        gated_ref[...] = gb16
        mixed_ref[...] = mixed

    row_w = pl.BlockSpec((BLK, WIDTH), lambda i: (i, 0))
    row_d = pl.BlockSpec((BLK, d), lambda i: (i, 0))
    full = lambda s: pl.BlockSpec(s, lambda i: (0, 0))
    return pl.pallas_call(
        body, name="mix_fwd",
        out_shape=(jax.ShapeDtypeStruct((lp, d), F32), jax.ShapeDtypeStruct((lp, d), F32),
                   jax.ShapeDtypeStruct((lp, d), F32), jax.ShapeDtypeStruct((lp, d), BF16),
                   jax.ShapeDtypeStruct((lp, d), F32)),
        grid=(lp // BLK,),
        in_specs=[row_w, row_w, row_d, pl.BlockSpec((BLK, d), lambda i: (i, 1)), row_d,
                  full((WIDTH, d)), full((WIDTH, d)), full((d, d)), full((1, d))],
        out_specs=(row_d, row_d, row_d, row_d, row_d),
        compiler_params=_params(("parallel",)),
    )(o_a, o_b, gates, gates, h0, w_fox, w_sb, w_out, g1)


def _gate_bwd(d_gated, gates, ya, yb):
    lp, d = d_gated.shape

    def body(dg_ref, ga_ref, gb_ref, ya_ref, yb_ref, dya_ref, dyb_ref, dga_ref, dgb_ref):
        dg = dg_ref[...]
        sa = _sigmoid(ga_ref[...])
        sb = _sigmoid(gb_ref[...])
        dya_ref[...] = (dg * sa).astype(BF16)
        dyb_ref[...] = (dg * sb).astype(BF16)
        dga_ref[...] = (dg * ya_ref[...] * (sa * (1.0 - sa))).astype(BF16)
        dgb_ref[...] = (dg * yb_ref[...] * (sb * (1.0 - sb))).astype(BF16)

    row = pl.BlockSpec((BLK, d), lambda i: (i, 0))
    out = jax.ShapeDtypeStruct((lp, d), BF16)
    return pl.pallas_call(
        body, name="gate_bwd",
        out_shape=(out, out, out, out),
        grid=(lp // BLK,),
        in_specs=[row, row, pl.BlockSpec((BLK, d), lambda i: (i, 1)), row, row],
        out_specs=(row, row, row, row),
        compiler_params=_params(("parallel",)),
    )(d_gated, gates, gates, ya, yb)


def _shift_down(cur, prev, n):
    rolled = pltpu.roll(cur, n, 0)
    row = lax.broadcasted_iota(jnp.int32, cur.shape, 0)
    for t in range(n):
        rolled = jnp.where(row == t, prev[SUBLANES - n + t:SUBLANES - n + t + 1, :], rolled)
    return rolled


def _shift_up(cur, nxt, n):
    rows = cur.shape[0]
    rolled = pltpu.roll(cur, rows - n, 0)
    row = lax.broadcasted_iota(jnp.int32, cur.shape, 0)
    for t in range(n):
        rolled = jnp.where(row == rows - n + t, nxt[t:t + 1, :], rolled)
    return rolled


def _gelu(x):
    return 0.5 * x * (1.0 + jnp.tanh(GELU_C * (x + GELU_A * (x * x * x))))


def _gelu_grad(x):
    t = jnp.tanh(GELU_C * (x + GELU_A * (x * x * x)))
    return 0.5 * (1.0 + t) + 0.5 * x * (1.0 - t * t) * (GELU_C * (1.0 + 3.0 * GELU_A * (x * x)))


def _conv_taps(cur, prev, w_ref, b_ref):
    s1 = _shift_down(cur, prev, 1)
    s2 = _shift_down(cur, prev, 2)
    u = b_ref[...] + w_ref[0:1, :] * s2
    u = u + w_ref[1:2, :] * s1
    u = u + w_ref[2:3, :] * cur
    return u, s1, s2


def _conv_gelu_fwd(up, conv_w, conv_b):
    lp, f2 = up.shape
    f = f2 // 2
    tc = _tile(f, 512)
    nf = f // tc
    rb = BLK // SUBLANES

    def body(ug_ref, uv_ref, pg_ref, pv_ref, wg_ref, wv_ref, bg_ref, bv_ref, act_ref):
        i = pl.program_id(0)
        keep = (i > 0).astype(F32)
        ugate, _, _ = _conv_taps(ug_ref[...], pg_ref[...] * keep, wg_ref, bg_ref)
        uval, _, _ = _conv_taps(uv_ref[...], pv_ref[...] * keep, wv_ref, bv_ref)
        act_ref[...] = (_gelu(ugate) * uval).astype(BF16)

    prev_row = lambda i: jnp.maximum(i * rb - 1, 0)
    return pl.pallas_call(
        body, name="conv_gelu_fwd",
        out_shape=jax.ShapeDtypeStruct((lp, f), BF16),
        grid=(lp // BLK, nf),
        in_specs=[pl.BlockSpec((BLK, tc), lambda i, j: (i, j)),
                  pl.BlockSpec((BLK, tc), lambda i, j: (i, j + nf)),
                  pl.BlockSpec((SUBLANES, tc), lambda i, j: (prev_row(i), j)),
                  pl.BlockSpec((SUBLANES, tc), lambda i, j: (prev_row(i), j + nf)),
                  pl.BlockSpec((3, tc), lambda i, j: (0, j)),
                  pl.BlockSpec((3, tc), lambda i, j: (0, j + nf)),
                  pl.BlockSpec((1, tc), lambda i, j: (0, j)),
                  pl.BlockSpec((1, tc), lambda i, j: (0, j + nf))],
        out_specs=pl.BlockSpec((BLK, tc), lambda i, j: (i, j)),
        compiler_params=_params(("parallel", "parallel")),
    )(up, up, up, up, conv_w, conv_w, conv_b, conv_b)


def _conv_gelu_bwd(up, d_act, conv_w, conv_b):
    lp, f2 = up.shape
    f = f2 // 2
    tc = _tile(f, 512)
    nf = f // tc
    rb = BLK // SUBLANES

    def body(uo_ref, up_ref, po_ref, pp_ref, wo_ref, wp_ref, bo_ref, bp_ref, da_ref,
             du_ref, dcw_ref, dcb_ref):
        j = pl.program_id(0)
        i = pl.program_id(1)
        keep = (i > 0).astype(F32)
        cur = uo_ref[...]
        own, s1, s2 = _conv_taps(cur, po_ref[...] * keep, wo_ref, bo_ref)
        other, _, _ = _conv_taps(up_ref[...], pp_ref[...] * keep, wp_ref, bp_ref)
        da = da_ref[...]
        du = jnp.where(j < nf, da * other * _gelu_grad(own), da * _gelu(other))
        du_ref[...] = du

        @pl.when(i == 0)
        def _():
            dcw_ref[...] = jnp.zeros_like(dcw_ref)
            dcb_ref[...] = jnp.zeros_like(dcb_ref)

        dcw_ref[0:1, :] += jnp.sum(du * s2, axis=0, keepdims=True)
        dcw_ref[1:2, :] += jnp.sum(du * s1, axis=0, keepdims=True)
        dcw_ref[2:3, :] += jnp.sum(du * cur, axis=0, keepdims=True)
        dcb_ref[...] += jnp.sum(du, axis=0, keepdims=True)

    prev_row = lambda i: jnp.maximum(i * rb - 1, 0)
    other = lambda j: (j + nf) % (2 * nf)
    return pl.pallas_call(
        body, name="conv_gelu_bwd",
        out_shape=(jax.ShapeDtypeStruct((lp, f2), F32), jax.ShapeDtypeStruct((3, f2), F32),
                   jax.ShapeDtypeStruct((1, f2), F32)),
        grid=(2 * nf, lp // BLK),
        in_specs=[pl.BlockSpec((BLK, tc), lambda j, i: (i, j)),
                  pl.BlockSpec((BLK, tc), lambda j, i: (i, other(j))),
                  pl.BlockSpec((SUBLANES, tc), lambda j, i: (prev_row(i), j)),
                  pl.BlockSpec((SUBLANES, tc), lambda j, i: (prev_row(i), other(j))),
                  pl.BlockSpec((3, tc), lambda j, i: (0, j)),
                  pl.BlockSpec((3, tc), lambda j, i: (0, other(j))),
                  pl.BlockSpec((1, tc), lambda j, i: (0, j)),
                  pl.BlockSpec((1, tc), lambda j, i: (0, other(j))),
                  pl.BlockSpec((BLK, tc), lambda j, i: (i, j % nf))],
        out_specs=(pl.BlockSpec((BLK, tc), lambda j, i: (i, j)),
                   pl.BlockSpec((3, tc), lambda j, i: (0, j)),
                   pl.BlockSpec((1, tc), lambda j, i: (0, j))),
        compiler_params=_params(("parallel", "arbitrary")),
    )(up, up, up, up, conv_w, conv_w, conv_b, conv_b, d_act)


def _conv_bwd_input(du, conv_w):
    lp, f2 = du.shape
    tc = _tile(f2, 512)
    nb = lp // BLK
    rb = BLK // SUBLANES

    def body(du_ref, nx_ref, w_ref, o_ref):
        i = pl.program_id(0)
        keep = (i < nb - 1).astype(F32)
        cur = du_ref[...]
        nxt = nx_ref[...] * keep
        n1 = _shift_up(cur, nxt, 1)
        n2 = _shift_up(cur, nxt, 2)
        o_ref[...] = (w_ref[2:3, :] * cur + w_ref[1:2, :] * n1 + w_ref[0:1, :] * n2).astype(BF16)

    next_row = lambda i: jnp.minimum((i + 1) * rb, nb * rb - 1)
    return pl.pallas_call(
        body, name="conv_bwd_input",
        out_shape=jax.ShapeDtypeStruct((lp, f2), BF16),
        grid=(nb, f2 // tc),
        in_specs=[pl.BlockSpec((BLK, tc), lambda i, j: (i, j)),
                  pl.BlockSpec((SUBLANES, tc), lambda i, j: (next_row(i), j)),
                  pl.BlockSpec((3, tc), lambda i, j: (0, j))],
        out_specs=pl.BlockSpec((BLK, tc), lambda i, j: (i, j)),
        compiler_params=_params(("parallel", "parallel")),
    )(du, du, conv_w)


def _out_loss(h1, ffn, g3, target, n_valid):
    lp, d = h1.shape

    def body(h_ref, f_ref, g_ref, t_ref, dy_ref, loss_ref):
        i = pl.program_id(0)

        @pl.when(i == 0)
        def _():
            loss_ref[...] = jnp.zeros_like(loss_ref)

        fv = f_ref[...]
        r = lax.rsqrt(jnp.mean(fv * fv, axis=-1, keepdims=True) + EPS)
        y = h_ref[...] + (fv * r) * g_ref[...]
        row = i * BLK + lax.broadcasted_iota(jnp.int32, (BLK, 1), 0)
        valid = (row >= N_META) & (row < n_valid)
        diff = jnp.where(valid, y - t_ref[...], 0.0)
        dy_ref[...] = diff * (1.0 / d)
        per_row = jnp.mean(diff * diff, axis=-1, keepdims=True)
        loss_ref[...] += 0.5 * jnp.sum(per_row, axis=0, keepdims=True)

    row_d = pl.BlockSpec((BLK, d), lambda i: (i, 0))
    return pl.pallas_call(
        body, name="out_loss",
        out_shape=(jax.ShapeDtypeStruct((lp, d), F32), jax.ShapeDtypeStruct((SUBLANES, LANES), F32)),
        grid=(lp // BLK,),
        in_specs=[row_d, row_d, pl.BlockSpec((1, d), lambda i: (0, 0)), row_d],
        out_specs=(row_d, pl.BlockSpec((SUBLANES, LANES), lambda i: (0, 0))),
        compiler_params=_params(("arbitrary",)),
    )(h1, ffn, g3, target)


def _head_masks():
    lane = lax.broadcasted_iota(jnp.int32, (BLK, LANES), 1)
    return [lane < HEAD_DIM, lane >= HEAD_DIM]


def _att_specs(base, lp):
    q_spec = pl.BlockSpec((BLK, LANES), lambda p, i: (i, base + p))
    k_spec = pl.BlockSpec((lp, LANES), lambda p, i: (0, base + N_PAIRS + p))
    v_spec = pl.BlockSpec((lp, LANES), lambda p, i: (0, base + 2 * N_PAIRS + p))
    return q_spec, k_spec, v_spec


def _kv_rows(j):
    return pl.ds(pl.multiple_of(j * BLK, BLK), BLK)


def _fox_fwd(qkv, ccw, crow):
    lp = qkv.shape[0]
    nq = lp // BLK

    def body(q_ref, k_ref, v_ref, cc_ref, cr_ref, o_ref, lse_ref, acc_ref, m_ref, l_ref):
        i = pl.program_id(1)
        masks = _head_masks()
        qs = q_ref[...] * ATT_SCALE
        qh = [jnp.where(mk, qs, 0).astype(BF16) for mk in masks]
        cc = cc_ref[...]
        ct = [cc[:, 0:1], cc[:, HEAD_DIM:HEAD_DIM + 1]]
        acc_ref[...] = jnp.zeros_like(acc_ref)
        m_ref[...] = jnp.full_like(m_ref, -1e30)
        l_ref[...] = jnp.zeros_like(l_ref)
        row = lax.broadcasted_iota(jnp.int32, (BLK, BLK), 0)
        col = lax.broadcasted_iota(jnp.int32, (BLK, BLK), 1)
        causal = col <= row

        def tile(j, diag):
            k = k_ref[_kv_rows(j), :]
            v = v_ref[_kv_rows(j), :]
            for hh in range(2):
                s = _dot(qh[hh], k, NT) + ct[hh] - cr_ref[0, j, hh:hh + 1, :]
                if diag:
                    s = jnp.where(causal, s, -1e30)
                m_prev = m_ref[hh]
                m_new = jnp.maximum(m_prev, jnp.max(s, axis=-1, keepdims=True))
                alpha = jnp.exp(m_prev - m_new)
                p = jnp.exp(s - m_new)
                l_ref[hh] = alpha * l_ref[hh] + jnp.sum(p, axis=-1, keepdims=True)
                acc_ref[hh] = alpha * acc_ref[hh] + _dot(p.astype(BF16), v, NN)
                m_ref[hh] = m_new

        def loop(j, carry):
            tile(j, False)
            return carry

        lax.fori_loop(0, i, loop, 0)
        tile(i, True)
        o0 = acc_ref[0] / l_ref[0]
        o1 = acc_ref[1] / l_ref[1]
        o_ref[...] = jnp.where(masks[0], o0, o1)
        lse0 = m_ref[0] + jnp.log(l_ref[0])
        lse1 = m_ref[1] + jnp.log(l_ref[1])
        lse_ref[...] = jnp.where(masks[0], lse0, lse1)

    q_spec, k_spec, v_spec = _att_specs(0, lp)
    blk = pl.BlockSpec((BLK, LANES), lambda p, i: (i, p))
    out = jax.ShapeDtypeStruct((lp, WIDTH), F32)
    return pl.pallas_call(
        body, name="fox_fwd",
        out_shape=(out, out),
        grid=(N_PAIRS, nq),
        in_specs=[q_spec, k_spec, v_spec, blk,
                  pl.BlockSpec((1, nq, 2, BLK), lambda p, i: (p, 0, 0, 0))],
        out_specs=(blk, blk),
        scratch_shapes=[pltpu.VMEM((2, BLK, LANES), F32), pltpu.VMEM((2, BLK, 1), F32),
                        pltpu.VMEM((2, BLK, 1), F32)],
        compiler_params=_params(("parallel", "parallel")),
    )(qkv, qkv, qkv, ccw, crow)


def _fox_bwd(qkv, ccw, crow, o, lse, d_o):
    lp = qkv.shape[0]
    nq = lp // BLK

    def body(q_ref, k_ref, v_ref, cc_ref, cr_ref, o_ref, lse_ref, do_ref,
             dq_ref, dk_ref, dv_ref, dc_ref, dr_ref, acc_ref, rs_ref):
        i = pl.program_id(1)
        masks = _head_masks()
        qs = q_ref[...] * ATT_SCALE
        qh = [jnp.where(mk, qs, 0).astype(BF16) for mk in masks]
        dov = do_ref[...]
        doh = [jnp.where(mk, dov, 0.0).astype(BF16) for mk in masks]
        prod = dov * o_ref[...]
        dsum = [jnp.sum(jnp.where(mk, prod, 0.0), axis=-1, keepdims=True) for mk in masks]
        cc = cc_ref[...]
        lsev = lse_ref[...]
        ct = [cc[:, 0:1], cc[:, HEAD_DIM:HEAD_DIM + 1]]
        lse = [lsev[:, 0:1], lsev[:, HEAD_DIM:HEAD_DIM + 1]]
        row = lax.broadcasted_iota(jnp.int32, (BLK, BLK), 0)
        col = lax.broadcasted_iota(jnp.int32, (BLK, BLK), 1)
        causal = col <= row

        @pl.when(i == 0)
        def _():
            dk_ref[...] = jnp.zeros_like(dk_ref)
            dv_ref[...] = jnp.zeros_like(dv_ref)
            dc_ref[...] = jnp.zeros_like(dc_ref)

        acc_ref[...] = jnp.zeros_like(acc_ref)
        rs_ref[...] = jnp.zeros_like(rs_ref)

        def tile(j, diag):
            rows = _kv_rows(j)
            k = k_ref[rows, :]
            v = v_ref[rows, :]
            dk_t = jnp.zeros((BLK, LANES), F32)
            dv_t = jnp.zeros((BLK, LANES), F32)
            for hh in range(2):
                s = _dot(qh[hh], k, NT) + ct[hh] - cr_ref[0, j, hh:hh + 1, :]
                p = jnp.exp(s - lse[hh])
                if diag:
                    p = jnp.where(causal, p, 0.0)
                dp = _dot(doh[hh], v, NT)
                ds = p * (dp - dsum[hh])
                dsb = ds.astype(BF16)
                dv_t = dv_t + _dot(p.astype(BF16), doh[hh], TN)
                dk_t = dk_t + _dot(dsb, qh[hh], TN)
                acc_ref[hh] += _dot(dsb, k, NN)
                dc_ref[0, j, hh:hh + 1, :] -= jnp.sum(ds, axis=0, keepdims=True)
                rs_ref[hh] += jnp.sum(ds, axis=-1, keepdims=True)
            dk_ref[rows, :] += dk_t
            dv_ref[rows, :] += dv_t

        def loop(j, carry):
            tile(j, False)
            return carry

        lax.fori_loop(0, i, loop, 0)
        tile(i, True)
        dq_ref[...] = (jnp.where(masks[0], acc_ref[0], acc_ref[1]) * ATT_SCALE).astype(BF16)
        dr_ref[...] = jnp.where(masks[0], rs_ref[0], rs_ref[1])

    q_spec, k_spec, v_spec = _att_specs(0, lp)
    blk = pl.BlockSpec((BLK, LANES), lambda p, i: (i, p))
    col_full = pl.BlockSpec((lp, LANES), lambda p, i: (0, p))
    crow_spec = pl.BlockSpec((1, nq, 2, BLK), lambda p, i: (p, 0, 0, 0))
    return pl.pallas_call(
        body, name="fox_bwd",
        out_shape=(jax.ShapeDtypeStruct((lp, WIDTH), BF16), jax.ShapeDtypeStruct((lp, WIDTH), F32),
                   jax.ShapeDtypeStruct((lp, WIDTH), F32), jax.ShapeDtypeStruct((N_PAIRS, nq, 2, BLK), F32),
                   jax.ShapeDtypeStruct((lp, WIDTH), F32)),
        grid=(N_PAIRS, nq),
        in_specs=[q_spec, k_spec, v_spec, blk, crow_spec, blk, blk, blk],
        out_specs=(blk, col_full, col_full, crow_spec, blk),
        scratch_shapes=[pltpu.VMEM((2, BLK, LANES), F32), pltpu.VMEM((2, BLK, 1), F32)],
        compiler_params=_params(("parallel", "arbitrary")),
    )(qkv, qkv, qkv, ccw, crow, o, lse, d_o)


def _sb_tile_fwd(qh, k, later_carry, strict, causal, diag):
    z = _dot(qh, k, NT)
    ell, kap = _log_sigmoid_parts(z)
    if diag:
        kap = jnp.where(causal, kap, 0.0)
    khi, klo = _split_bf16(kap)
    later = _dot(khi, strict, NN) + _dot(klo, strict, NN) + later_carry
    a = jnp.exp(ell + later)
    if diag:
        a = jnp.where(causal, a, 0.0)
    return a, ell, kap


def _sb_fwd(qkv):
    lp = qkv.shape[0]
    nq = lp // BLK
    assert nq <= HEAD_DIM

    def body(q_ref, k_ref, v_ref, o_ref, lc_ref, acc_ref, car_ref):
        i = pl.program_id(1)
        masks = _head_masks()
        qs = q_ref[...] * ATT_SCALE
        qh = [jnp.where(mk, qs, 0).astype(BF16) for mk in masks]
        row = lax.broadcasted_iota(jnp.int32, (BLK, BLK), 0)
        col = lax.broadcasted_iota(jnp.int32, (BLK, BLK), 1)
        lane = lax.broadcasted_iota(jnp.int32, (BLK, LANES), 1)
        causal = col < row
        strict = (row > col).astype(BF16)
        acc_ref[...] = jnp.zeros_like(acc_ref)
        car_ref[...] = jnp.zeros_like(car_ref)
        lc_ref[...] = jnp.zeros_like(lc_ref)

        def tile(j, diag):
            k = k_ref[_kv_rows(j), :]
            v = v_ref[_kv_rows(j), :]
            for hh in range(2):
                if not diag:
                    lc_ref[...] = jnp.where(lane == j + HEAD_DIM * hh, car_ref[hh], lc_ref[...])
                a, _, kap = _sb_tile_fwd(qh[hh], k, car_ref[hh], strict, causal, diag)
                acc_ref[hh] += _dot(a.astype(BF16), v, NN)
                car_ref[hh] += jnp.sum(kap, axis=-1, keepdims=True)

        tile(i, True)

        def loop(t, carry):
            tile(i - 1 - t, False)
            return carry

        lax.fori_loop(0, i, loop, 0)
        o_ref[...] = jnp.where(masks[0], acc_ref[0], acc_ref[1])

    q_spec, k_spec, v_spec = _att_specs(3 * N_PAIRS, lp)
    blk = pl.BlockSpec((BLK, LANES), lambda p, i: (i, p))
    out = jax.ShapeDtypeStruct((lp, WIDTH), F32)
    return pl.pallas_call(
        body, name="sb_fwd",
        out_shape=(out, out),
        grid=(N_PAIRS, nq),
        in_specs=[q_spec, k_spec, v_spec],
        out_specs=(blk, blk),
        scratch_shapes=[pltpu.VMEM((2, BLK, LANES), F32), pltpu.VMEM((2, BLK, 1), F32)],
        compiler_params=_params(("parallel", "parallel")),
    )(qkv, qkv, qkv)


def _sb_bwd(qkv, lcar, d_o):
    lp = qkv.shape[0]
    nq = lp // BLK

    def body(q_ref, k_ref, v_ref, lc_ref, do_ref, dq_ref, dk_ref, dv_ref, acc_ref, cg_ref):
        i = pl.program_id(1)
        masks = _head_masks()
        qs = q_ref[...] * ATT_SCALE
        qh = [jnp.where(mk, qs, 0).astype(BF16) for mk in masks]
        dov = do_ref[...]
        doh = [jnp.where(mk, dov, 0.0).astype(BF16) for mk in masks]
        lcv = lc_ref[...]
        row = lax.broadcasted_iota(jnp.int32, (BLK, BLK), 0)
        col = lax.broadcasted_iota(jnp.int32, (BLK, BLK), 1)
        lane = lax.broadcasted_iota(jnp.int32, (BLK, LANES), 1)
        causal = col < row
        strict = (row > col).astype(BF16)
        before = (row < col).astype(BF16)

        @pl.when(i == 0)
        def _():
            dk_ref[...] = jnp.zeros_like(dk_ref)
            dv_ref[...] = jnp.zeros_like(dv_ref)

        acc_ref[...] = jnp.zeros_like(acc_ref)
        cg_ref[...] = jnp.zeros_like(cg_ref)

        def tile(j, diag):
            rows = _kv_rows(j)
            k = k_ref[rows, :]
            v = v_ref[rows, :]
            dk_t = jnp.zeros((BLK, LANES), F32)
            dv_t = jnp.zeros((BLK, LANES), F32)
            for hh in range(2):
                later = jnp.sum(jnp.where(lane == j + HEAD_DIM * hh, lcv, 0.0), axis=-1, keepdims=True)
                a, ell, _ = _sb_tile_fwd(qh[hh], k, later, strict, causal, diag)
                g = _dot(doh[hh], v, NT) * a
                ghi, glo = _split_bf16(g)
                cg = _dot(ghi, before, NN) + _dot(glo, before, NN) + cg_ref[hh]
                beta = jnp.exp(ell)
                dz = g * (1.0 - beta) - beta * cg
                if diag:
                    dz = jnp.where(causal, dz, 0.0)
                dzb = dz.astype(BF16)
                acc_ref[hh] += _dot(dzb, k, NN)
                dk_t = dk_t + _dot(dzb, qh[hh], TN)
                dv_t = dv_t + _dot(a.astype(BF16), doh[hh], TN)
                cg_ref[hh] += jnp.sum(g, axis=-1, keepdims=True)
            dk_ref[rows, :] += dk_t
            dv_ref[rows, :] += dv_t

        def loop(j, carry):
            tile(j, False)
            return carry

        lax.fori_loop(0, i, loop, 0)
        tile(i, True)
        dq_ref[...] = (jnp.where(masks[0], acc_ref[0], acc_ref[1]) * ATT_SCALE).astype(BF16)

    q_spec, k_spec, v_spec = _att_specs(3 * N_PAIRS, lp)
    blk = pl.BlockSpec((BLK, LANES), lambda p, i: (i, p))
    col_full = pl.BlockSpec((lp, LANES), lambda p, i: (0, p))
    return pl.pallas_call(
        body, name="sb_bwd",
        out_shape=(jax.ShapeDtypeStruct((lp, WIDTH), BF16), jax.ShapeDtypeStruct((lp, WIDTH), F32),
                   jax.ShapeDtypeStruct((lp, WIDTH), F32)),
        grid=(N_PAIRS, nq),
        in_specs=[q_spec, k_spec, v_spec, blk, blk],
        out_specs=(blk, col_full, col_full),
        scratch_shapes=[pltpu.VMEM((2, BLK, LANES), F32), pltpu.VMEM((2, BLK, 1), F32)],
        compiler_params=_params(("parallel", "arbitrary")),
    )(qkv, qkv, qkv, lcar, d_o)


def _local_step(x, target, meta, gains, w_in, b_forget, w_fox, w_sb, w_out, w_up, conv_w, conv_b, w_down):
    s, d = x.shape
    n_valid = N_META + s
    lp = -(-n_valid // BLK) * BLK
    pad = lp - n_valid
    nq = lp // BLK

    h0 = jnp.concatenate([meta, x, jnp.zeros((pad, d), F32)], axis=0)
    tgt = jnp.concatenate([jnp.zeros((N_META, d), F32), target, jnp.zeros((pad, d), F32)], axis=0)

    q_a, k_a, v_a, f_a, q_b, k_b, v_b, g_a, g_b = jnp.split(
        w_in, [512, 1024, 1536, 1544, 2056, 2568, 3080, 4104], axis=1)
    w_qkv = jnp.concatenate([q_a, k_a, v_a, q_b, k_b, v_b], axis=1)
    w_gf = jnp.concatenate([g_a, g_b, f_a, jnp.zeros((d, F_PAD - N_HEADS), BF16)], axis=1)
    b_pad = jnp.concatenate([b_forget.reshape(1, N_HEADS), jnp.zeros((1, LANES - N_HEADS), F32)], axis=1)
    g0, g1, g2, g3 = (gains[i:i + 1] for i in range(4))

    xn1 = _rmsnorm_fwd(h0, g0, "norm1_fwd")
    qkv = _mm(xn1, w_qkv, "nn", BF16, "proj_qkv")
    gf = _mm(xn1, w_gf, "nn", F32, "proj_gates")
    fpre = gf[:, 2 * d:2 * d + LANES]
    c = _forget_fwd(fpre, b_pad)[:, :N_HEADS]
    ccw = jnp.repeat(c, HEAD_DIM, axis=1)
    crow = c.T.reshape(N_PAIRS, 2, nq, BLK).transpose(0, 2, 1, 3)
    o_a, lse = _fox_fwd(qkv, ccw, crow)
    o_b, lcar = _sb_fwd(qkv)
    h1, ya, yb, gated, mixed = _mix_fwd(o_a, o_b, gf, h0, w_fox, w_sb, w_out, g1)
    xn3 = _rmsnorm_fwd(h1, g2, "norm3_fwd")
    up = _mm(xn3, w_up, "nn", F32, "ffn_up")
    act = _conv_gelu_fwd(up, conv_w, conv_b)
    ffn = _mm(act, w_down, "nn", F32, "ffn_down")
    dy, loss_acc = _out_loss(h1, ffn, g3, tgt, n_valid)
    loss = loss_acc[0, 0]

    d_ffn, dg3 = _rmsnorm_bwd(ffn, g3, dy, None, BF16, "norm4_bwd")
    d_act = _mm(d_ffn, w_down, "nt", F32, "ffn_down_dx")
    gw_down = _mm(act, d_ffn, "tn", F32, "ffn_down_dw")
    du, g_conv_w, g_conv_b = _conv_gelu_bwd(up, d_act, conv_w, conv_b)
    d_up = _conv_bwd_input(du, conv_w)
    d_xn3 = _mm(d_up, w_up, "nt", F32, "ffn_up_dx")
    gw_up = _mm(xn3, d_up, "tn", F32, "ffn_up_dw")
    dh1, dg2 = _rmsnorm_bwd(h1, g2, d_xn3, dy, F32, "norm3_bwd")

    d_mixed, dg1 = _rmsnorm_bwd(mixed, g1, dh1, None, BF16, "norm2_bwd")
    d_gated = _mm(d_mixed, w_out, "nt", F32, "out_dx")
    gw_out = _mm(gated, d_mixed, "tn", F32, "out_dw")
    d_ya, d_yb, d_ga, d_gb = _gate_bwd(d_gated, gf, ya, yb)
    d_oa = _mm(d_ya, w_fox, "nt", F32, "fox_o_dx")
    gw_fox = _mm(o_a, d_ya, "tn", F32, "fox_o_dw")
    d_ob = _mm(d_yb, w_sb, "nt", F32, "sb_o_dx")
    gw_sb = _mm(o_b, d_yb, "tn", F32, "sb_o_dw")
    dq_a, dk_a, dv_a, dcrow, drow = _fox_bwd(qkv, ccw, crow, o_a, lse, d_oa)
    dq_b, dk_b, dv_b = _sb_bwd(qkv, lcar, d_ob)
    dc = dcrow.transpose(0, 2, 1, 3).reshape(N_HEADS, lp).T + drow[:, ::HEAD_DIM]
    dc = jnp.concatenate([dc, jnp.zeros((lp, LANES - N_HEADS), F32)], axis=1)
    df, db = _forget_bwd(dc, fpre, b_pad)
    lane = jnp.arange(LANES) < N_HEADS
    df = jnp.where(lane[None, :], df, 0.0)
    d_proj = jnp.concatenate(
        [dq_a, dk_a.astype(BF16), dv_a.astype(BF16), dq_b, dk_b.astype(BF16), dv_b.astype(BF16),
         d_ga, d_gb, df.astype(BF16), jnp.zeros((lp, F_PAD - LANES), BF16)], axis=1)
    w_in_p = jnp.concatenate([w_qkv, w_gf], axis=1)
    d_xn1 = _mm(d_proj, w_in_p, "nt", F32, "proj_dx")
    gw_in_p = _mm(xn1, d_proj, "tn", F32, "proj_dw")
    dh0, dg0 = _rmsnorm_bwd(h0, g0, d_xn1, dh1, F32, "norm1_bwd")

    qkv_parts = jnp.split(gw_in_p[:, :6 * WIDTH], 6, axis=1)
    gw_in = jnp.concatenate(
        qkv_parts[:3] + [gw_in_p[:, 6 * WIDTH + 2 * d:6 * WIDTH + 2 * d + N_HEADS]] + qkv_parts[3:]
        + [gw_in_p[:, 6 * WIDTH:6 * WIDTH + 2 * d]], axis=1)
    grads = {
        "meta_tokens": dh0[:N_META],
        "norm_gains": jnp.concatenate([dg0, dg1, dg2, dg3], axis=0),
        "w_in": gw_in,
        "b_forget": db[:, :N_HEADS],
        "w_o_fox": gw_fox,
        "w_o_sb": gw_sb,
        "w_out": gw_out,
        "w_up": gw_up,
        "conv_w": g_conv_w,
        "conv_b": g_conv_b,
        "w_down": gw_down,
    }
    return loss, dh0[N_META:n_valid], grads


MESH_IDS = pl.DeviceIdType.MESH


def _all_gather(block):
    r, w = block.shape

    def body(x_ref, out_ref, send_sems, recv_sems, local_sem):
        x, y, c = lax.axis_index("x"), lax.axis_index("y"), lax.axis_index("c")
        me, sibling = (x, y, c), (x, y, 1 - c)
        chips = [(1 - x, y), (x, 1 - y), (1 - x, 1 - y)]

        def slot(px, py, pc):
            return out_ref.at[4 * px + 2 * py + pc]

        def copy(k, blk, to, src=None):
            return pltpu.make_async_remote_copy(
                src_ref=slot(*blk) if src is None else src, dst_ref=slot(*blk),
                send_sem=send_sems.at[k], recv_sem=recv_sems.at[k],
                device_id=to, device_id_type=MESH_IDS)

        mine = pltpu.make_async_copy(x_ref, slot(*me), local_sem)
        mine.start()
        first = [copy(0, me, sibling, src=x_ref)]
        first += [copy(1 + j, me, (*chip, c), src=x_ref) for j, chip in enumerate(chips)]
        for cp in first:
            cp.start()
        passed = [copy(4 + j, (*chip, c), sibling) for j, chip in enumerate(chips)]
        for j, chip in enumerate(chips):
            copy(1 + j, (*chip, c), me).wait_recv()
            passed[j].start()
        copy(0, sibling, me).wait_recv()
        for j, chip in enumerate(chips):
            copy(4 + j, (*chip, 1 - c), me).wait_recv()
        for cp in first + passed:
            cp.wait_send()
        mine.wait()

    return pl.pallas_call(
        body, name="all_gather",
        out_shape=jax.ShapeDtypeStruct((N_DEV, r, w), block.dtype),
        in_specs=[pl.BlockSpec(memory_space=pl.ANY)],
        out_specs=pl.BlockSpec(memory_space=pl.ANY),
        scratch_shapes=[pltpu.SemaphoreType.DMA((7,)), pltpu.SemaphoreType.DMA((7,)), pltpu.SemaphoreType.DMA],
    )(block)


def _exchange(parts):
    _, r, w = parts.shape

    def body(p_ref, out_ref, send_sems, recv_sems, local_sem):
        x, y, c = lax.axis_index("x"), lax.axis_index("y"), lax.axis_index("c")
        my = 4 * x + 2 * y + c
        mine = pltpu.make_async_copy(p_ref.at[my], out_ref.at[my], local_sem)
        mine.start()
        copies = []
        for rel in range(1, N_DEV):
            px, py, pc = x ^ (rel >> 2), y ^ ((rel >> 1) & 1), c ^ (rel & 1)
            peer = 4 * px + 2 * py + pc
            send = pltpu.make_async_remote_copy(
                src_ref=p_ref.at[peer], dst_ref=out_ref.at[my],
                send_sem=send_sems.at[rel - 1], recv_sem=recv_sems.at[rel - 1],
                device_id=(px, py, pc), device_id_type=MESH_IDS)
            send.start()
            recv = pltpu.make_async_remote_copy(
                src_ref=p_ref.at[my], dst_ref=out_ref.at[peer],
                send_sem=send_sems.at[rel - 1], recv_sem=recv_sems.at[rel - 1],
                device_id=(px, py, pc), device_id_type=MESH_IDS)
            copies.append((send, recv))
        for send, recv in copies:
            recv.wait_recv()
        for send, recv in copies:
            send.wait_send()
        mine.wait()

    return pl.pallas_call(
        body, name="grad_exchange",
        out_shape=jax.ShapeDtypeStruct(parts.shape, parts.dtype),
        in_specs=[pl.BlockSpec(memory_space=pl.ANY)],
        out_specs=pl.BlockSpec(memory_space=pl.ANY),
        scratch_shapes=[pltpu.SemaphoreType.DMA((7,)), pltpu.SemaphoreType.DMA((7,)), pltpu.SemaphoreType.DMA],
    )(parts)


def _sum_parts(parts):
    n, r, w = parts.shape
    tr = _tile(r, 2048, SUBLANES)

    def body(p_ref, o_ref):
        acc = p_ref[0]
        for s in range(1, n):
            acc = acc + p_ref[s]
        o_ref[...] = acc

    return pl.pallas_call(
        body, name="grad_sum",
        out_shape=jax.ShapeDtypeStruct((r, w), parts.dtype),
        grid=(r // tr,),
        in_specs=[pl.BlockSpec((n, tr, w), lambda i: (0, i, 0))],
        out_specs=pl.BlockSpec((tr, w), lambda i: (i, 0)),
        compiler_params=_params(("parallel",)),
    )(parts)


def _adamw(w, g, m, v, name):
    rows, cols = w.shape
    tr = BLK if rows % BLK == 0 else rows
    c1 = 1.0 - ADAM_B1 ** ADAM_STEP
    c2 = 1.0 - ADAM_B2 ** ADAM_STEP

    def body(w_ref, g_ref, m_ref, v_ref, d_ref, nm_ref, nv_ref):
        gv = g_ref[...]
        nm = ADAM_B1 * m_ref[...] + (1.0 - ADAM_B1) * gv
        nv = ADAM_B2 * v_ref[...] + (1.0 - ADAM_B2) * (gv * gv)
        m_hat = nm / c1
        v_hat = nv / c2
        d_ref[...] = -ADAM_LR * (m_hat / (jnp.sqrt(v_hat) + ADAM_EPS) + ADAM_WD * w_ref[...])
        nm_ref[...] = nm
        nv_ref[...] = nv

    spec = pl.BlockSpec((tr, cols), lambda i: (i, 0))
    out = jax.ShapeDtypeStruct((rows, cols), F32)
    return pl.pallas_call(
        body, name=name,
        out_shape=(out, out, out),
        grid=(rows // tr,),
        in_specs=[spec, spec, spec, spec],
        out_specs=(spec, spec, spec),
        compiler_params=_params(("parallel",)),
    )(w, g, m, v)


def _pack_rows(flat, dtype, align):
    n = flat.shape[-1]
    padded = -(-n // align) * align
    if padded != n:
        flat = jnp.concatenate([flat, jnp.zeros(flat.shape[:-1] + (padded - n,), dtype)], axis=-1)
    return flat.reshape(flat.shape[:-1] + (padded // LANES, LANES))


def _f32_as_bf16(a):
    return lax.bitcast_convert_type(a, BF16).reshape(-1)


def _bf16_as_f32(flat, shape):
    return lax.bitcast_convert_type(flat.reshape(shape + (2,)), F32)


WEIGHTS = ["meta_tokens", "norm_gains", "w_in", "b_forget", "w_o_fox", "w_o_sb", "w_out", "w_up", "conv_w",
           "conv_b", "w_down"]
ROW_SHARDED = ("w_out", "w_down")
REPLICATED = ("b_forget", "conv_b")
MATMUL_WEIGHTS = ("w_in", "w_o_fox", "w_o_sb", "w_out", "w_up", "w_down")


def kernel(x, meta_tokens, norm_gains, w_in, b_forget, w_o_fox, w_o_sb, w_out, w_up, conv_w, conv_b, w_down, loss_target, m_meta_tokens, m_norm_gains, m_w_in, m_b_forget, m_w_o_fox, m_w_o_sb, m_w_out, m_w_up, m_conv_w, m_conv_b, m_w_down, v_meta_tokens, v_norm_gains, v_w_in, v_b_forget, v_w_o_fox, v_w_o_sb, v_w_out, v_w_up, v_conv_w, v_conv_b, v_w_down):
    w = dict(meta_tokens=meta_tokens, norm_gains=norm_gains, w_in=w_in, b_forget=b_forget, w_o_fox=w_o_fox,
             w_o_sb=w_o_sb, w_out=w_out, w_up=w_up, conv_w=conv_w, conv_b=conv_b, w_down=w_down)
    mom = dict(meta_tokens=m_meta_tokens, norm_gains=m_norm_gains, w_in=m_w_in, b_forget=m_b_forget,
               w_o_fox=m_w_o_fox, w_o_sb=m_w_o_sb, w_out=m_w_out, w_up=m_w_up, conv_w=m_conv_w, conv_b=m_conv_b,
               w_down=m_w_down)
    vel = dict(meta_tokens=v_meta_tokens, norm_gains=v_norm_gains, w_in=v_w_in, b_forget=v_b_forget,
               w_o_fox=v_w_o_fox, w_o_sb=v_w_o_sb, w_out=v_w_out, w_up=v_w_up, conv_w=v_conv_w, conv_b=v_conv_b,
               w_down=v_w_down)
    w2 = {n: a.reshape(a.shape[-2:]) for n, a in w.items()}
    shard_shape = {n: a.shape for n, a in w2.items()}

    gathered = [n for n in WEIGHTS if n not in REPLICATED]
    flat = []
    for n in gathered:
        if n in MATMUL_WEIGHTS:
            flat.append(w2[n].astype(BF16).reshape(-1))
        else:
            flat.append(_f32_as_bf16(w2[n]))
    sizes = [f.shape[0] for f in flat]
    block = _pack_rows(jnp.concatenate(flat), BF16, 16 * LANES)
    everyone = _all_gather(block).reshape(N_DEV, -1)
    full = {}
    off = 0
    for n, size in zip(gathered, sizes):
        rows, cols = shard_shape[n]
        piece = everyone[:, off:off + size]
        off += size
        if n in MATMUL_WEIGHTS:
            piece = piece.reshape(N_DEV, rows, cols)
        else:
            piece = _bf16_as_f32(piece, (N_DEV, rows, cols))
        if n in ROW_SHARDED:
            full[n] = piece.reshape(N_DEV * rows, cols)
        else:
            full[n] = piece.transpose(1, 0, 2).reshape(rows, N_DEV * cols)
    full["b_forget"] = w2["b_forget"]
    full["conv_b"] = w2["conv_b"]

    loss, grad_x, grads = _local_step(
        x[0], loss_target[0], full["meta_tokens"], full["norm_gains"], full["w_in"], full["b_forget"],
        full["w_o_fox"], full["w_o_sb"], full["w_out"], full["w_up"], full["conv_w"], full["conv_b"],
        full["w_down"])
    loss = lax.psum(loss, ("x", "y", "c"))

    flat = []
    for n in WEIGHTS:
        g = grads[n]
        rows, cols = shard_shape[n]
        if n in REPLICATED:
            flat.append(jnp.broadcast_to(g.reshape(1, -1), (N_DEV, rows * cols)))
        elif n in ROW_SHARDED:
            flat.append(g.reshape(N_DEV, rows * cols))
        else:
            flat.append(g.reshape(rows, N_DEV, cols).transpose(1, 0, 2).reshape(N_DEV, rows * cols))
    sizes = [f.shape[1] for f in flat]
    parts = _pack_rows(jnp.concatenate(flat, axis=1), F32, 1024 * LANES)
    total = _sum_parts(_exchange(parts)).reshape(-1)
    grad = {}
    off = 0
    for n, size in zip(WEIGHTS, sizes):
        grad[n] = total[off:off + size].reshape(shard_shape[n])
        off += size

    delta, new_m, new_v = {}, {}, {}
    for n in WEIGHTS:
        shape = w[n].shape
        d_, m_, v_ = _adamw(w2[n], grad[n], mom[n].reshape(shard_shape[n]), vel[n].reshape(shard_shape[n]),
                            "adamw_" + n)
        delta[n], new_m[n], new_v[n] = d_.reshape(shape), m_.reshape(shape), v_.reshape(shape)

    return (loss, grad_x[None], *[grad[n].reshape(w[n].shape) for n in WEIGHTS], *[delta[n] for n in WEIGHTS],
            *[new_m[n] for n in WEIGHTS], *[new_v[n] for n in WEIGHTS])
```

```python
import functools
import math

import jax
import jax.numpy as jnp
from jax import lax
from jax.experimental import pallas as pl
from jax.experimental.pallas import tpu as pltpu

F32 = jnp.float32
BF16 = jnp.bfloat16

N_DEV = 8
N_META = 16
HEAD_DIM = 64
N_HEADS = 8
WIDTH = N_HEADS * HEAD_DIM
N_PAIRS = N_HEADS // 2
LANES = 128
SUBLANES = 8
EPS = 1e-6
ATT_SCALE = HEAD_DIM ** -0.5
BLK = 256
F_PAD = 256
VMEM_LIMIT = 48 << 20

ADAM_LR = 0.001
ADAM_B1 = 0.9
ADAM_B2 = 0.999
ADAM_EPS = 1e-08
ADAM_WD = 0.01
ADAM_STEP = 10

GELU_C = math.sqrt(2.0 / math.pi)
GELU_A = 0.044715


def _params(sem, vmem=VMEM_LIMIT):
    return pltpu.CompilerParams(dimension_semantics=sem, vmem_limit_bytes=vmem)


def _tile(dim, cap, align=LANES):
    t = (min(cap, dim) // align) * align
    while t >= align:
        if dim % t == 0:
            return t
        t -= align
    return dim


def _log_sigmoid_parts(z):
    lp = jnp.log1p(jnp.exp(-jnp.abs(z)))
    return jnp.minimum(z, 0.0) - lp, jnp.minimum(-z, 0.0) - lp


def _sigmoid(x):
    return 1.0 / (1.0 + jnp.exp(-x))


def _split_bf16(x):
    hi = x.astype(BF16)
    lo = (x - hi.astype(F32)).astype(BF16)
    return hi, lo


def _dot(a, b, dims):
    return lax.dot_general(a, b, (dims, ((), ())), preferred_element_type=F32)


NN = ((1,), (0,))
NT = ((1,), (1,))
TN = ((0,), (0,))


def _mm(a, b, mode, out_dtype, name):
    if mode == "nn":
        (m, kc), (_, n) = a.shape, b.shape
    elif mode == "nt":
        (m, kc), (n, _) = a.shape, b.shape
    else:
        (kc, m), (_, n) = a.shape, b.shape
    tm = _tile(m, 768)
    tn = _tile(n, 1024)
    tk = _tile(kc, 1536 if mode != "tn" else 768)
    nk = kc // tk
    dims = {"nn": NN, "nt": NT, "tn": TN}[mode]

    def body(a_ref, b_ref, o_ref, acc_ref):
        k = pl.program_id(2)

        @pl.when(k == 0)
        def _():
            acc_ref[...] = jnp.zeros_like(acc_ref)

        acc_ref[...] += _dot(a_ref[...].astype(BF16), b_ref[...].astype(BF16), dims)

        @pl.when(k == nk - 1)
        def _():
            o_ref[...] = acc_ref[...].astype(out_dtype)

    if mode == "tn":
        a_spec = pl.BlockSpec((tk, tm), lambda j, i, k: (k, i))
    else:
        a_spec = pl.BlockSpec((tm, tk), lambda j, i, k: (i, k))
    if mode == "nt":
        b_spec = pl.BlockSpec((tn, tk), lambda j, i, k: (j, k))
    else:
        b_spec = pl.BlockSpec((tk, tn), lambda j, i, k: (k, j))
    return pl.pallas_call(
        body, name=name,
        out_shape=jax.ShapeDtypeStruct((m, n), out_dtype),
        grid=(n // tn, m // tm, nk),
        in_specs=[a_spec, b_spec],
        out_specs=pl.BlockSpec((tm, tn), lambda j, i, k: (i, j)),
        scratch_shapes=[pltpu.VMEM((tm, tn), F32)],
        compiler_params=_params(("parallel", "parallel", "arbitrary")),
    )(a, b)


def _rmsnorm_fwd(x, g, name):
    lp, d = x.shape

    def body(x_ref, g_ref, o_ref):
        xv = x_ref[...]
        r = lax.rsqrt(jnp.mean(xv * xv, axis=-1, keepdims=True) + EPS)
        o_ref[...] = ((xv * r) * g_ref[...]).astype(BF16)

    return pl.pallas_call(
        body, name=name,
        out_shape=jax.ShapeDtypeStruct((lp, d), BF16),
        grid=(lp // BLK,),
        in_specs=[pl.BlockSpec((BLK, d), lambda i: (i, 0)), pl.BlockSpec((1, d), lambda i: (0, 0))],
        out_specs=pl.BlockSpec((BLK, d), lambda i: (i, 0)),
        compiler_params=_params(("parallel",)),
    )(x, g)


def _rmsnorm_bwd(x, g, dy, resid, out_dtype, name):
    lp, d = x.shape
    has_resid = resid is not None

    def body(*refs):
        if has_resid:
            x_ref, g_ref, dy_ref, r_ref, dx_ref, dg_ref = refs
        else:
            x_ref, g_ref, dy_ref, dx_ref, dg_ref = refs
        i = pl.program_id(0)
        xv = x_ref[...]
        dyv = dy_ref[...].astype(F32)
        r = lax.rsqrt(jnp.mean(xv * xv, axis=-1, keepdims=True) + EPS)
        xh = xv * r
        dyg = dyv * g_ref[...]
        dx = r * (dyg - xh * jnp.mean(dyg * xh, axis=-1, keepdims=True))
        if has_resid:
            dx = dx + r_ref[...]
        dx_ref[...] = dx.astype(out_dtype)

        @pl.when(i == 0)
        def _():
            dg_ref[...] = jnp.zeros_like(dg_ref)

        dg_ref[...] += jnp.sum(dyv * xh, axis=0, keepdims=True)

    row = pl.BlockSpec((BLK, d), lambda i: (i, 0))
    vec = pl.BlockSpec((1, d), lambda i: (0, 0))
    ins = [x, g, dy] + ([resid] if has_resid else [])
    in_specs = [row, vec, row] + ([row] if has_resid else [])
    return pl.pallas_call(
        body, name=name,
        out_shape=(jax.ShapeDtypeStruct((lp, d), out_dtype), jax.ShapeDtypeStruct((1, d), F32)),
        grid=(lp // BLK,),
        in_specs=in_specs,
        out_specs=(row, vec),
        compiler_params=_params(("arbitrary",)),
    )(*ins)


def _forget_fwd(fpre, b_pad):
    lp = fpre.shape[0]

    def body(f_ref, b_ref, c_ref, carry_ref):
        i = pl.program_id(0)

        @pl.when(i == 0)
        def _():
            carry_ref[...] = jnp.zeros_like(carry_ref)

        logf, _ = _log_sigmoid_parts(f_ref[...] + b_ref[...])
        row = lax.broadcasted_iota(jnp.int32, (BLK, BLK), 0)
        col = lax.broadcasted_iota(jnp.int32, (BLK, BLK), 1)
        tri = (col <= row).astype(BF16)
        p0 = logf.astype(BF16)
        r1 = logf - p0.astype(F32)
        p1 = r1.astype(BF16)
        p2 = (r1 - p1.astype(F32)).astype(BF16)
        c = _dot(tri, p0, NN) + _dot(tri, p1, NN) + _dot(tri, p2, NN) + carry_ref[0:1, :]
        c_ref[...] = c
        carry_ref[...] = jnp.broadcast_to(c[BLK - 1:BLK, :], carry_ref.shape)

    return pl.pallas_call(
        body, name="forget_fwd",
        out_shape=jax.ShapeDtypeStruct((lp, LANES), F32),
        grid=(lp // BLK,),
        in_specs=[pl.BlockSpec((BLK, LANES), lambda i: (i, 0)), pl.BlockSpec((1, LANES), lambda i: (0, 0))],
        out_specs=pl.BlockSpec((BLK, LANES), lambda i: (i, 0)),
        scratch_shapes=[pltpu.VMEM((SUBLANES, LANES), F32)],
        compiler_params=_params(("arbitrary",)),
    )(fpre, b_pad)


def _forget_bwd(dc, fpre, b_pad):
    lp = fpre.shape[0]
    nb = lp // BLK

    def body(dc_ref, f_ref, b_ref, df_ref, db_ref, carry_ref):
        i = pl.program_id(0)

        @pl.when(i == 0)
        def _():
            carry_ref[...] = jnp.zeros_like(carry_ref)
            db_ref[...] = jnp.zeros_like(db_ref)

        dcv = dc_ref[...]
        row = lax.broadcasted_iota(jnp.int32, (BLK, BLK), 0)
        col = lax.broadcasted_iota(jnp.int32, (BLK, BLK), 1)
        tri = (col >= row).astype(BF16)
        p0 = dcv.astype(BF16)
        r1 = dcv - p0.astype(F32)
        p1 = r1.astype(BF16)
        p2 = (r1 - p1.astype(F32)).astype(BF16)
        dlogf = _dot(tri, p0, NN) + _dot(tri, p1, NN) + _dot(tri, p2, NN) + carry_ref[0:1, :]
        carry_ref[...] = jnp.broadcast_to(dlogf[0:1, :], carry_ref.shape)
        _, ls_neg = _log_sigmoid_parts(f_ref[...] + b_ref[...])
        df = dlogf * jnp.exp(ls_neg)
        df_ref[...] = df
        db_ref[...] += jnp.sum(df, axis=0, keepdims=True)

    rev = pl.BlockSpec((BLK, LANES), lambda i: (nb - 1 - i, 0))
    vec = pl.BlockSpec((1, LANES), lambda i: (0, 0))
    return pl.pallas_call(
        body, name="forget_bwd",
        out_shape=(jax.ShapeDtypeStruct((lp, LANES), F32), jax.ShapeDtypeStruct((1, LANES), F32)),
        grid=(nb,),
        in_specs=[rev, rev, vec],
        out_specs=(rev, vec),
        scratch_shapes=[pltpu.VMEM((SUBLANES, LANES), F32)],
        compiler_params=_params(("arbitrary",)),
    )(dc, fpre, b_pad)


def _mix_fwd(o_a, o_b, gates, h0, w_fox, w_sb, w_out, g1):
    lp, d = h0.shape

    def body(oa_ref, ob_ref, ga_ref, gb_ref, h_ref, wf_ref, ws_ref, wo_ref, g_ref,
             h1_ref, ya_ref, yb_ref, gated_ref, mixed_ref):
        ya = _dot(oa_ref[...].astype(BF16), wf_ref[...], NN)
        yb = _dot(ob_ref[...].astype(BF16), ws_ref[...], NN)
        gated = _sigmoid(ga_ref[...]) * ya + _sigmoid(gb_ref[...]) * yb
        gb16 = gated.astype(BF16)
        mixed = _dot(gb16, wo_ref[...], NN)
        r = lax.rsqrt(jnp.mean(mixed * mixed, axis=-1, keepdims=True) + EPS)
        h1_ref[...] = h_ref[...] + (mixed * r) * g_ref[...]
        ya_ref[...] = ya
        yb_ref[...] = yb
        gated_ref[...] = gb16
        mixed_ref[...] = mixed

    row_w = pl.BlockSpec((BLK, WIDTH), lambda i: (i, 0))
    row_d = pl.BlockSpec((BLK, d), lambda i: (i, 0))
    full = lambda s: pl.BlockSpec(s, lambda i: (0, 0))
    return pl.pallas_call(
        body, name="mix_fwd",
        out_shape=(jax.ShapeDtypeStruct((lp, d), F32), jax.ShapeDtypeStruct((lp, d), F32),
                   jax.ShapeDtypeStruct((lp, d), F32), jax.ShapeDtypeStruct((lp, d), BF16),
                   jax.ShapeDtypeStruct((lp, d), F32)),
        grid=(lp // BLK,),
        in_specs=[row_w, row_w, row_d, pl.BlockSpec((BLK, d), lambda i: (i, 1)), row_d,
                  full((WIDTH, d)), full((WIDTH, d)), full((d, d)), full((1, d))],
        out_specs=(row_d, row_d, row_d, row_d, row_d),
        compiler_params=_params(("parallel",)),
    )(o_a, o_b, gates, gates, h0, w_fox, w_sb, w_out, g1)


def _gate_bwd(d_gated, gates, ya, yb):
    lp, d = d_gated.shape

    def body(dg_ref, ga_ref, gb_ref, ya_ref, yb_ref, dya_ref, dyb_ref, dga_ref, dgb_ref):
        dg = dg_ref[...]
        sa = _sigmoid(ga_ref[...])
        sb = _sigmoid(gb_ref[...])
        dya_ref[...] = (dg * sa).astype(BF16)
        dyb_ref[...] = (dg * sb).astype(BF16)
        dga_ref[...] = (dg * ya_ref[...] * (sa * (1.0 - sa))).astype(BF16)
        dgb_ref[...] = (dg * yb_ref[...] * (sb * (1.0 - sb))).astype(BF16)

    row = pl.BlockSpec((BLK, d), lambda i: (i, 0))
    out = jax.ShapeDtypeStruct((lp, d), BF16)
    return pl.pallas_call(
        body, name="gate_bwd",
        out_shape=(out, out, out, out),
        grid=(lp // BLK,),
        in_specs=[row, row, pl.BlockSpec((BLK, d), lambda i: (i, 1)), row, row],
        out_specs=(row, row, row, row),
        compiler_params=_params(("parallel",)),
    )(d_gated, gates, gates, ya, yb)


def _shift_down(cur, prev, n):
    rolled = pltpu.roll(cur, n, 0)
    row = lax.broadcasted_iota(jnp.int32, cur.shape, 0)
    for t in range(n):
        rolled = jnp.where(row == t, prev[SUBLANES - n + t:SUBLANES - n + t + 1, :], rolled)
    return rolled


def _shift_up(cur, nxt, n):
    rows = cur.shape[0]
    rolled = pltpu.roll(cur, rows - n, 0)
    row = lax.broadcasted_iota(jnp.int32, cur.shape, 0)
    for t in range(n):
        rolled = jnp.where(row == rows - n + t, nxt[t:t + 1, :], rolled)
    return rolled


def _gelu(x):
    return 0.5 * x * (1.0 + jnp.tanh(GELU_C * (x + GELU_A * (x * x * x))))


def _gelu_grad(x):
    t = jnp.tanh(GELU_C * (x + GELU_A * (x * x * x)))
    return 0.5 * (1.0 + t) + 0.5 * x * (1.0 - t * t) * (GELU_C * (1.0 + 3.0 * GELU_A * (x * x)))


def _conv_taps(cur, prev, w_ref, b_ref):
    s1 = _shift_down(cur, prev, 1)
    s2 = _shift_down(cur, prev, 2)
    u = b_ref[...] + w_ref[0:1, :] * s2
    u = u + w_ref[1:2, :] * s1
    u = u + w_ref[2:3, :] * cur
    return u, s1, s2


def _conv_gelu_fwd(up, conv_w, conv_b):
    lp, f2 = up.shape
    f = f2 // 2
    tc = _tile(f, 512)
    nf = f // tc
    rb = BLK // SUBLANES

    def body(ug_ref, uv_ref, pg_ref, pv_ref, wg_ref, wv_ref, bg_ref, bv_ref, act_ref):
        i = pl.program_id(0)
        keep = (i > 0).astype(F32)
        ugate, _, _ = _conv_taps(ug_ref[...], pg_ref[...] * keep, wg_ref, bg_ref)
        uval, _, _ = _conv_taps(uv_ref[...], pv_ref[...] * keep, wv_ref, bv_ref)
        act_ref[...] = (_gelu(ugate) * uval).astype(BF16)

    prev_row = lambda i: jnp.maximum(i * rb - 1, 0)
    return pl.pallas_call(
        body, name="conv_gelu_fwd",
        out_shape=jax.ShapeDtypeStruct((lp, f), BF16),
        grid=(lp // BLK, nf),
        in_specs=[pl.BlockSpec((BLK, tc), lambda i, j: (i, j)),
                  pl.BlockSpec((BLK, tc), lambda i, j: (i, j + nf)),
                  pl.BlockSpec((SUBLANES, tc), lambda i, j: (prev_row(i), j)),
                  pl.BlockSpec((SUBLANES, tc), lambda i, j: (prev_row(i), j + nf)),
                  pl.BlockSpec((3, tc), lambda i, j: (0, j)),
                  pl.BlockSpec((3, tc), lambda i, j: (0, j + nf)),
                  pl.BlockSpec((1, tc), lambda i, j: (0, j)),
                  pl.BlockSpec((1, tc), lambda i, j: (0, j + nf))],
        out_specs=pl.BlockSpec((BLK, tc), lambda i, j: (i, j)),
        compiler_params=_params(("parallel", "parallel")),
    )(up, up, up, up, conv_w, conv_w, conv_b, conv_b)


def _conv_gelu_bwd(up, d_act, conv_w, conv_b):
    lp, f2 = up.shape
    f = f2 // 2
    tc = _tile(f, 512)
    nf = f // tc
    rb = BLK // SUBLANES

    def body(uo_ref, up_ref, po_ref, pp_ref, wo_ref, wp_ref, bo_ref, bp_ref, da_ref,
             du_ref, dcw_ref, dcb_ref):
        j = pl.program_id(0)
        i = pl.program_id(1)
        keep = (i > 0).astype(F32)
        cur = uo_ref[...]
        own, s1, s2 = _conv_taps(cur, po_ref[...] * keep, wo_ref, bo_ref)
        other, _, _ = _conv_taps(up_ref[...], pp_ref[...] * keep, wp_ref, bp_ref)
        da = da_ref[...]
        du = jnp.where(j < nf, da * other * _gelu_grad(own), da * _gelu(other))
        du_ref[...] = du

        @pl.when(i == 0)
        def _():
            dcw_ref[...] = jnp.zeros_like(dcw_ref)
            dcb_ref[...] = jnp.zeros_like(dcb_ref)

        dcw_ref[0:1, :] += jnp.sum(du * s2, axis=0, keepdims=True)
        dcw_ref[1:2, :] += jnp.sum(du * s1, axis=0, keepdims=True)
        dcw_ref[2:3, :] += jnp.sum(du * cur, axis=0, keepdims=True)
        dcb_ref[...] += jnp.sum(du, axis=0, keepdims=True)

    prev_row = lambda i: jnp.maximum(i * rb - 1, 0)
    other = lambda j: (j + nf) % (2 * nf)
    return pl.pallas_call(
        body, name="conv_gelu_bwd",
        out_shape=(jax.ShapeDtypeStruct((lp, f2), F32), jax.ShapeDtypeStruct((3, f2), F32),
                   jax.ShapeDtypeStruct((1, f2), F32)),
        grid=(2 * nf, lp // BLK),
        in_specs=[pl.BlockSpec((BLK, tc), lambda j, i: (i, j)),
                  pl.BlockSpec((BLK, tc), lambda j, i: (i, other(j))),
                  pl.BlockSpec((SUBLANES, tc), lambda j, i: (prev_row(i), j)),
                  pl.BlockSpec((SUBLANES, tc), lambda j, i: (prev_row(i), other(j))),
                  pl.BlockSpec((3, tc), lambda j, i: (0, j)),
                  pl.BlockSpec((3, tc), lambda j, i: (0, other(j))),
                  pl.BlockSpec((1, tc), lambda j, i: (0, j)),
                  pl.BlockSpec((1, tc), lambda j, i: (0, other(j))),
                  pl.BlockSpec((BLK, tc), lambda j, i: (i, j % nf))],
        out_specs=(pl.BlockSpec((BLK, tc), lambda j, i: (i, j)),
                   pl.BlockSpec((3, tc), lambda j, i: (0, j)),
                   pl.BlockSpec((1, tc), lambda j, i: (0, j))),
        compiler_params=_params(("parallel", "arbitrary")),
    )(up, up, up, up, conv_w, conv_w, conv_b, conv_b, d_act)


def _conv_bwd_input(du, conv_w):
    lp, f2 = du.shape
    tc = _tile(f2, 512)
    nb = lp // BLK
    rb = BLK // SUBLANES

    def body(du_ref, nx_ref, w_ref, o_ref):
        i = pl.program_id(0)
        keep = (i < nb - 1).astype(F32)
        cur = du_ref[...]
        nxt = nx_ref[...] * keep
        n1 = _shift_up(cur, nxt, 1)
        n2 = _shift_up(cur, nxt, 2)
        o_ref[...] = (w_ref[2:3, :] * cur + w_ref[1:2, :] * n1 + w_ref[0:1, :] * n2).astype(BF16)

    next_row = lambda i: jnp.minimum((i + 1) * rb, nb * rb - 1)
    return pl.pallas_call(
        body, name="conv_bwd_input",
        out_shape=jax.ShapeDtypeStruct((lp, f2), BF16),
        grid=(nb, f2 // tc),
        in_specs=[pl.BlockSpec((BLK, tc), lambda i, j: (i, j)),
                  pl.BlockSpec((SUBLANES, tc), lambda i, j: (next_row(i), j)),
                  pl.BlockSpec((3, tc), lambda i, j: (0, j))],
        out_specs=pl.BlockSpec((BLK, tc), lambda i, j: (i, j)),
        compiler_params=_params(("parallel", "parallel")),
    )(du, du, conv_w)


def _out_loss(h1, ffn, g3, target, n_valid):
    lp, d = h1.shape

    def body(h_ref, f_ref, g_ref, t_ref, dy_ref, loss_ref):
        i = pl.program_id(0)

        @pl.when(i == 0)
        def _():
            loss_ref[...] = jnp.zeros_like(loss_ref)

        fv = f_ref[...]
        r = lax.rsqrt(jnp.mean(fv * fv, axis=-1, keepdims=True) + EPS)
        y = h_ref[...] + (fv * r) * g_ref[...]
        row = i * BLK + lax.broadcasted_iota(jnp.int32, (BLK, 1), 0)
        valid = (row >= N_META) & (row < n_valid)
        diff = jnp.where(valid, y - t_ref[...], 0.0)
        dy_ref[...] = diff * (1.0 / d)
        per_row = jnp.mean(diff * diff, axis=-1, keepdims=True)
        loss_ref[...] += 0.5 * jnp.sum(per_row, axis=0, keepdims=True)

    row_d = pl.BlockSpec((BLK, d), lambda i: (i, 0))
    return pl.pallas_call(
        body, name="out_loss",
        out_shape=(jax.ShapeDtypeStruct((lp, d), F32), jax.ShapeDtypeStruct((SUBLANES, LANES), F32)),
        grid=(lp // BLK,),
        in_specs=[row_d, row_d, pl.BlockSpec((1, d), lambda i: (0, 0)), row_d],
        out_specs=(row_d, pl.BlockSpec((SUBLANES, LANES), lambda i: (0, 0))),
        compiler_params=_params(("arbitrary",)),
    )(h1, ffn, g3, target)


def _head_masks():
    lane = lax.broadcasted_iota(jnp.int32, (BLK, LANES), 1)
    return [lane < HEAD_DIM, lane >= HEAD_DIM]


def _att_specs(base, lp):
    q_spec = pl.BlockSpec((BLK, LANES), lambda p, i: (i, base + p))
    k_spec = pl.BlockSpec((lp, LANES), lambda p, i: (0, base + N_PAIRS + p))
    v_spec = pl.BlockSpec((lp, LANES), lambda p, i: (0, base + 2 * N_PAIRS + p))
    return q_spec, k_spec, v_spec


def _kv_rows(j):
    return pl.ds(pl.multiple_of(j * BLK, BLK), BLK)


def _bf16_pieces(x):
    rnd = lambda a: lax.reduce_precision(a, exponent_bits=8, mantissa_bits=7)
    p0 = rnd(x)
    p1 = rnd(x - p0)
    p2 = rnd(x - p0 - p1)
    return [p0.astype(BF16), p1.astype(BF16), p2.astype(BF16)]


def _aug_lanes(cols):
    lp = cols[0].shape[0]
    vals = jnp.stack([c.astype(BF16) for c in cols], axis=-1)
    vals = vals.reshape(lp, N_PAIRS, 2, len(cols))[:, :, ::-1, :]
    vals = jnp.pad(vals, ((0, 0), (0, 0), (0, 0), (0, HEAD_DIM - len(cols))))
    return vals.reshape(lp, WIDTH)


N_AUG = 3


def _fox_fwd(qkv, qaug, kaug):
    lp = qkv.shape[0]
    nq = lp // BLK

    def body(q_ref, k_ref, v_ref, qa_ref, ka_ref, o_ref, lse_ref, acc_ref, m_ref):
        i = pl.program_id(1)
        hs = range(2)
        masks = _head_masks()
        qs = q_ref[...] * ATT_SCALE
        qa = qa_ref[...]
        qh = [jnp.where(masks[hh], qs, qa) for hh in hs]
        acc_ref[...] = jnp.zeros_like(acc_ref)
        m_ref[...] = jnp.full_like(m_ref, -1e30)
        row = lax.broadcasted_iota(jnp.int32, (BLK, BLK), 0)
        col = lax.broadcasted_iota(jnp.int32, (BLK, BLK), 1)
        causal = col <= row
        one = jnp.ones((BLK, LANES), BF16)

        def tile(j, diag):
            k = k_ref[_kv_rows(j), :]
            ka = ka_ref[_kv_rows(j), :]
            v = v_ref[_kv_rows(j), :]
            kh = [jnp.where(masks[hh], k, ka) for hh in hs]
            vh = [jnp.where(masks[hh], v, one) for hh in hs]
            s = [_dot(qh[hh], kh[hh], NT) for hh in hs]
            if diag:
                s = [jnp.where(causal, s[hh], -1e30) for hh in hs]
            m_prev = [m_ref[hh] for hh in hs]
            m_new = [jnp.maximum(m_prev[hh], jnp.max(s[hh], axis=-1, keepdims=True)) for hh in hs]
            p = [jnp.exp(s[hh] - m_new[hh]).astype(BF16) for hh in hs]
            for hh in hs:
                acc_ref[hh] = jnp.exp(m_prev[hh] - m_new[hh]) * acc_ref[hh] + _dot(p[hh], vh[hh], NN)
                m_ref[hh] = m_new[hh]

        def loop(j, carry):
            tile(j, False)
            return carry

        lax.fori_loop(0, i, loop, 0)
        tile(i, True)
        acc = [acc_ref[hh] for hh in hs]
        denom = [acc[0][:, HEAD_DIM:HEAD_DIM + 1], acc[1][:, 0:1]]
        o_ref[...] = jnp.where(masks[0], acc[0] / denom[0], acc[1] / denom[1])
        lse_ref[...] = jnp.where(masks[0], m_ref[0] + jnp.log(denom[0]), m_ref[1] + jnp.log(denom[1]))

    q_spec, k_spec, v_spec = _att_specs(0, lp)
    blk = pl.BlockSpec((BLK, LANES), lambda p, i: (i, p))
    col_full = pl.BlockSpec((lp, LANES), lambda p, i: (0, p))
    out = jax.ShapeDtypeStruct((lp, WIDTH), F32)
    return pl.pallas_call(
        body, name="fox_fwd",
        out_shape=(out, out),
        grid=(N_PAIRS, nq),
        in_specs=[q_spec, k_spec, v_spec, blk, col_full],
        out_specs=(blk, blk),
        scratch_shapes=[pltpu.VMEM((2, BLK, LANES), F32), pltpu.VMEM((2, BLK, 1), F32)],
        compiler_params=_params(("parallel", "parallel")),
    )(qkv, qkv, qkv, qaug, kaug)


def _head_dots(a, b):
    lp = a.shape[0]

    def body(a_ref, b_ref, o_ref):
        lane = lax.broadcasted_iota(jnp.int32, (BLK, LANES), 1)
        out = jnp.zeros((BLK, LANES), F32)
        for p in range(N_PAIRS):
            cols = slice(p * LANES, (p + 1) * LANES)
            prod = a_ref[:, cols] * b_ref[:, cols]
            for hh in range(2):
                part = jnp.where((lane >= HEAD_DIM) == (hh == 1), prod, 0.0)
                out = jnp.where(lane == 2 * p + hh, jnp.sum(part, axis=-1, keepdims=True), out)
        o_ref[...] = out

    row = pl.BlockSpec((BLK, WIDTH), lambda i: (i, 0))
    return pl.pallas_call(
        body, name="head_dots",
        out_shape=jax.ShapeDtypeStruct((lp, LANES), F32),
        grid=(lp // BLK,),
        in_specs=[row, row],
        out_specs=pl.BlockSpec((BLK, LANES), lambda i: (i, 0)),
        compiler_params=_params(("parallel",)),
    )(a, b)


def _fox_bwd(qkv, qaug, kaug, doaug, d_o):
    lp = qkv.shape[0]
    nq = lp // BLK

    def body(q_ref, k_ref, v_ref, qa_ref, ka_ref, da_ref, do_ref,
             dq_ref, dk_ref, dv_ref, dc_ref, dr_ref, acc_ref, rs_ref):
        i = pl.program_id(1)
        hs = range(2)
        masks = _head_masks()
        qs = q_ref[...] * ATT_SCALE
        qa = qa_ref[...]
        qm = [jnp.where(masks[hh], qs, 0) for hh in hs]
        qh = [jnp.where(masks[hh], qs, qa) for hh in hs]
        dov = do_ref[...].astype(BF16)
        doa = da_ref[...]
        dom = [jnp.where(masks[hh], dov, 0) for hh in hs]
        doh = [jnp.where(masks[hh], dov, doa) for hh in hs]
        lane = lax.broadcasted_iota(jnp.int32, (BLK, LANES), 1)
        ones = (lane % HEAD_DIM < N_AUG).astype(BF16)
        row = lax.broadcasted_iota(jnp.int32, (BLK, BLK), 0)
        col = lax.broadcasted_iota(jnp.int32, (BLK, BLK), 1)
        causal = col <= row

        @pl.when(i == 0)
        def _():
            dk_ref[...] = jnp.zeros_like(dk_ref)
            dv_ref[...] = jnp.zeros_like(dv_ref)
            dc_ref[...] = jnp.zeros_like(dc_ref)

        acc_ref[...] = jnp.zeros_like(acc_ref)
        rs_ref[...] = jnp.zeros_like(rs_ref)

        def tile(j, diag):
            rows = _kv_rows(j)
            k = k_ref[rows, :]
            ka = ka_ref[rows, :]
            v = v_ref[rows, :]
            kh = [jnp.where(masks[hh], k, ka) for hh in hs]
            vh = [jnp.where(masks[hh], v, ones) for hh in hs]
            logp = [_dot(qh[hh], kh[hh], NT) for hh in hs]
            dp = [_dot(doh[hh], vh[hh], NT) for hh in hs]
            p = [jnp.exp(logp[hh]) for hh in hs]
            if diag:
                p = [jnp.where(causal, p[hh], 0.0) for hh in hs]
            ds = [p[hh] * dp[hh] for hh in hs]
            dsb = [ds[hh].astype(BF16) for hh in hs]
            for hh in hs:
                acc_ref[hh] += _dot(dsb[hh], k, NN)
                dc_ref[0, j, hh:hh + 1, :] -= jnp.sum(ds[hh], axis=0, keepdims=True)
                rs_ref[hh] += jnp.sum(ds[hh], axis=-1, keepdims=True)
            dk_ref[rows, :] += _dot(dsb[0], qm[0], TN) + _dot(dsb[1], qm[1], TN)
            dv_ref[rows, :] += _dot(p[0].astype(BF16), dom[0], TN) + _dot(p[1].astype(BF16), dom[1], TN)

        def loop(j, carry):
            tile(j, False)
            return carry

        lax.fori_loop(0, i, loop, 0)
        tile(i, True)
        dq_ref[...] = (jnp.where(masks[0], acc_ref[0], acc_ref[1]) * ATT_SCALE).astype(BF16)
        dr_ref[...] = jnp.where(masks[0], rs_ref[0], rs_ref[1])

    q_spec, k_spec, v_spec = _att_specs(0, lp)
    blk = pl.BlockSpec((BLK, LANES), lambda p, i: (i, p))
    col_full = pl.BlockSpec((lp, LANES), lambda p, i: (0, p))
    crow_spec = pl.BlockSpec((1, nq, 2, BLK), lambda p, i: (p, 0, 0, 0))
    return pl.pallas_call(
        body, name="fox_bwd",
        out_shape=(jax.ShapeDtypeStruct((lp, WIDTH), BF16), jax.ShapeDtypeStruct((lp, WIDTH), F32),
                   jax.ShapeDtypeStruct((lp, WIDTH), F32), jax.ShapeDtypeStruct((N_PAIRS, nq, 2, BLK), F32),
                   jax.ShapeDtypeStruct((lp, WIDTH), F32)),
        grid=(N_PAIRS, nq),
        in_specs=[q_spec, k_spec, v_spec, blk, col_full, blk, blk],
        out_specs=(blk, col_full, col_full, crow_spec, blk),
        scratch_shapes=[pltpu.VMEM((2, BLK, LANES), F32), pltpu.VMEM((2, BLK, 1), F32)],
        compiler_params=_params(("parallel", "arbitrary")),
    )(qkv, qkv, qkv, qaug, kaug, doaug, d_o)


def _sb_scores(z):
    ell = jnp.minimum(z, 0.0) - jnp.log(1.0 + jnp.exp(-jnp.abs(z)))
    return ell, ell - z


def _stacked(tri):
    return jnp.concatenate([tri, tri], axis=0)


def _cumsum_dot(x, tri2):
    hi, lo = _split_bf16(x)
    return _dot(jnp.concatenate([hi, lo], axis=1), tri2, NN)


def _sb_tile_fwd(qh, k, car, strict2, causal, diag):
    z = [_dot(qh[hh], k, NT) for hh in range(2)]
    sc = [_sb_scores(z[hh]) for hh in range(2)]
    ell = [sc[hh][0] for hh in range(2)]
    kap = [jnp.where(causal, sc[hh][1], 0.0) if diag else sc[hh][1] for hh in range(2)]
    later = [_cumsum_dot(kap[hh], strict2) for hh in range(2)]
    a = [jnp.exp(ell[hh] + later[hh] + car[hh]) for hh in range(2)]
    if diag:
        a = [jnp.where(causal, a[hh], 0.0) for hh in range(2)]
    return a, ell, kap


def _sb_fwd(qkv):
    lp = qkv.shape[0]
    nq = lp // BLK
    assert nq <= HEAD_DIM

    def body(q_ref, k_ref, v_ref, o_ref, lc_ref, acc_ref, car_ref):
        i = pl.program_id(1)
        masks = _head_masks()
        qs = q_ref[...] * ATT_SCALE
        qh = [jnp.where(mk, qs, 0).astype(BF16) for mk in masks]
        row = lax.broadcasted_iota(jnp.int32, (BLK, BLK), 0)
        col = lax.broadcasted_iota(jnp.int32, (BLK, BLK), 1)
        lane = lax.broadcasted_iota(jnp.int32, (BLK, LANES), 1)
        causal = col < row
        strict2 = _stacked((row > col).astype(BF16))
        acc_ref[...] = jnp.zeros_like(acc_ref)
        car_ref[...] = jnp.zeros_like(car_ref)
        lc_ref[...] = jnp.zeros_like(lc_ref)

        def tile(j, diag):
            k = k_ref[_kv_rows(j), :]
            v = v_ref[_kv_rows(j), :]
            car = [car_ref[0], car_ref[1]]
            if not diag:
                lc_ref[...] = jnp.where(lane == j, car[0], jnp.where(lane == j + HEAD_DIM, car[1], lc_ref[...]))
            a, _, kap = _sb_tile_fwd(qh, k, car, strict2, causal, diag)
            for hh in range(2):
                acc_ref[hh] += _dot(a[hh].astype(BF16), v, NN)
                car_ref[hh] = car[hh] + jnp.sum(kap[hh], axis=-1, keepdims=True)

        tile(i, True)

        def loop(t, carry):
            tile(i - 1 - t, False)
            return carry

        lax.fori_loop(0, i, loop, 0)
        o_ref[...] = jnp.where(masks[0], acc_ref[0], acc_ref[1])

    q_spec, k_spec, v_spec = _att_specs(3 * N_PAIRS, lp)
    blk = pl.BlockSpec((BLK, LANES), lambda p, i: (i, p))
    out = jax.ShapeDtypeStruct((lp, WIDTH), F32)
    return pl.pallas_call(
        body, name="sb_fwd",
        out_shape=(out, out),
        grid=(N_PAIRS, nq),
        in_specs=[q_spec, k_spec, v_spec],
        out_specs=(blk, blk),
        scratch_shapes=[pltpu.VMEM((2, BLK, LANES), F32), pltpu.VMEM((2, BLK, 1), F32)],
        compiler_params=_params(("parallel", "parallel")),
    )(qkv, qkv, qkv)


def _sb_bwd(qkv, lcar, d_o):
    lp = qkv.shape[0]
    nq = lp // BLK

    def body(q_ref, k_ref, v_ref, lc_ref, do_ref, dq_ref, dk_ref, dv_ref, acc_ref, cg_ref):
        i = pl.program_id(1)
        masks = _head_masks()
        qs = q_ref[...] * ATT_SCALE
        qh = [jnp.where(mk, qs, 0).astype(BF16) for mk in masks]
        dov = do_ref[...]
        doh = [jnp.where(mk, dov, 0.0).astype(BF16) for mk in masks]
        lcv = lc_ref[...]
        row = lax.broadcasted_iota(jnp.int32, (BLK, BLK), 0)
        col = lax.broadcasted_iota(jnp.int32, (BLK, BLK), 1)
        lane = lax.broadcasted_iota(jnp.int32, (BLK, LANES), 1)
        causal = col < row
        strict2 = _stacked((row > col).astype(BF16))
        before2 = _stacked((row < col).astype(BF16))

        @pl.when(i == 0)
        def _():
            dk_ref[...] = jnp.zeros_like(dk_ref)
            dv_ref[...] = jnp.zeros_like(dv_ref)

        acc_ref[...] = jnp.zeros_like(acc_ref)
        cg_ref[...] = jnp.zeros_like(cg_ref)

        def tile(j, diag):
            rows = _kv_rows(j)
            k = k_ref[rows, :]
            v = v_ref[rows, :]
            hs = range(2)
            car = [jnp.sum(jnp.where(lane == j + HEAD_DIM * hh, lcv, 0.0), axis=-1, keepdims=True) for hh in hs]
            cgc = [cg_ref[hh] for hh in hs]
            da = [_dot(doh[hh], v, NT) for hh in hs]
            a, ell, _ = _sb_tile_fwd(qh, k, car, strict2, causal, diag)
            g = [da[hh] * a[hh] for hh in hs]
            cg = [_cumsum_dot(g[hh], before2) for hh in hs]
            dz = [g[hh] - jnp.exp(ell[hh]) * (g[hh] + cg[hh] + cgc[hh]) for hh in hs]
            if diag:
                dz = [jnp.where(causal, dz[hh], 0.0) for hh in hs]
            dzb = [dz[hh].astype(BF16) for hh in hs]
            for hh in hs:
                acc_ref[hh] += _dot(dzb[hh], k, NN)
                cg_ref[hh] = cgc[hh] + jnp.sum(g[hh], axis=-1, keepdims=True)
            dk_ref[rows, :] += _dot(dzb[0], qh[0], TN) + _dot(dzb[1], qh[1], TN)
            dv_ref[rows, :] += _dot(a[0].astype(BF16), doh[0], TN) + _dot(a[1].astype(BF16), doh[1], TN)

        def loop(j, carry):
            tile(j, False)
            return carry

        lax.fori_loop(0, i, loop, 0)
        tile(i, True)
        dq_ref[...] = (jnp.where(masks[0], acc_ref[0], acc_ref[1]) * ATT_SCALE).astype(BF16)

    q_spec, k_spec, v_spec = _att_specs(3 * N_PAIRS, lp)
    blk = pl.BlockSpec((BLK, LANES), lambda p, i: (i, p))
    col_full = pl.BlockSpec((lp, LANES), lambda p, i: (0, p))
    return pl.pallas_call(
        body, name="sb_bwd",
        out_shape=(jax.ShapeDtypeStruct((lp, WIDTH), BF16), jax.ShapeDtypeStruct((lp, WIDTH), F32),
                   jax.ShapeDtypeStruct((lp, WIDTH), F32)),
        grid=(N_PAIRS, nq),
        in_specs=[q_spec, k_spec, v_spec, blk, blk],
        out_specs=(blk, col_full, col_full),
        scratch_shapes=[pltpu.VMEM((2, BLK, LANES), F32), pltpu.VMEM((2, BLK, 1), F32)],
        compiler_params=_params(("parallel", "arbitrary")),
    )(qkv, qkv, qkv, lcar, d_o)


def _local_step(x, target, meta, gains, w_in, b_forget, w_fox, w_sb, w_out, w_up, conv_w, conv_b, w_down):
    s, d = x.shape
    n_valid = N_META + s
    lp = -(-n_valid // BLK) * BLK
    pad = lp - n_valid
    nq = lp // BLK

    h0 = jnp.concatenate([meta, x, jnp.zeros((pad, d), F32)], axis=0)
    tgt = jnp.concatenate([jnp.zeros((N_META, d), F32), target, jnp.zeros((pad, d), F32)], axis=0)

    q_a, k_a, v_a, f_a, q_b, k_b, v_b, g_a, g_b = jnp.split(
        w_in, [512, 1024, 1536, 1544, 2056, 2568, 3080, 4104], axis=1)
    w_qkv = jnp.concatenate([q_a, k_a, v_a, q_b, k_b, v_b], axis=1)
    w_gf = jnp.concatenate([g_a, g_b, f_a, jnp.zeros((d, F_PAD - N_HEADS), BF16)], axis=1)
    b_pad = jnp.concatenate([b_forget.reshape(1, N_HEADS), jnp.zeros((1, LANES - N_HEADS), F32)], axis=1)
    g0, g1, g2, g3 = (gains[i:i + 1] for i in range(4))

    xn1 = _rmsnorm_fwd(h0, g0, "norm1_fwd")
    qkv = _mm(xn1, w_qkv, "nn", BF16, "proj_qkv")
    gf = _mm(xn1, w_gf, "nn", F32, "proj_gates")
    fpre = gf[:, 2 * d:2 * d + LANES]
    c = _forget_fwd(fpre, b_pad)[:, :N_HEADS]
    c_pieces = _bf16_pieces(c)
    one = jnp.ones((lp, N_HEADS), BF16)
    kaug = _aug_lanes(3 * [one] + [-x for x in c_pieces] + 3 * [one])
    o_a, lse = _fox_fwd(qkv, _aug_lanes(c_pieces + 3 * [one]), kaug)
    o_b, lcar = _sb_fwd(qkv)
    h1, ya, yb, gated, mixed = _mix_fwd(o_a, o_b, gf, h0, w_fox, w_sb, w_out, g1)
    xn3 = _rmsnorm_fwd(h1, g2, "norm3_fwd")
    up = _mm(xn3, w_up, "nn", F32, "ffn_up")
    act = _conv_gelu_fwd(up, conv_w, conv_b)
    ffn = _mm(act, w_down, "nn", F32, "ffn_down")
    dy, loss_acc = _out_loss(h1, ffn, g3, tgt, n_valid)
    loss = loss_acc[0, 0]

    d_ffn, dg3 = _rmsnorm_bwd(ffn, g3, dy, None, BF16, "norm4_bwd")
    d_act = _mm(d_ffn, w_down, "nt", F32, "ffn_down_dx")
    gw_down = _mm(act, d_ffn, "tn", F32, "ffn_down_dw")
    du, g_conv_w, g_conv_b = _conv_gelu_bwd(up, d_act, conv_w, conv_b)
    d_up = _conv_bwd_input(du, conv_w)
    d_xn3 = _mm(d_up, w_up, "nt", F32, "ffn_up_dx")
    gw_up = _mm(xn3, d_up, "tn", F32, "ffn_up_dw")
    dh1, dg2 = _rmsnorm_bwd(h1, g2, d_xn3, dy, F32, "norm3_bwd")

    d_mixed, dg1 = _rmsnorm_bwd(mixed, g1, dh1, None, BF16, "norm2_bwd")
    d_gated = _mm(d_mixed, w_out, "nt", F32, "out_dx")
    gw_out = _mm(gated, d_mixed, "tn", F32, "out_dw")
    d_ya, d_yb, d_ga, d_gb = _gate_bwd(d_gated, gf, ya, yb)
    d_oa = _mm(d_ya, w_fox, "nt", F32, "fox_o_dx")
    gw_fox = _mm(o_a, d_ya, "tn", F32, "fox_o_dw")
    d_ob = _mm(d_yb, w_sb, "nt", F32, "sb_o_dx")
    gw_sb = _mm(o_b, d_yb, "tn", F32, "sb_o_dw")
    neg_lse = [-x for x in _bf16_pieces(lse[:, ::HEAD_DIM])]
    neg_dsum = [-x for x in _bf16_pieces(_head_dots(d_oa, o_a)[:, :N_HEADS])]
    qaug = _aug_lanes(c_pieces + 3 * [one] + neg_lse)
    dq_a, dk_a, dv_a, dcrow, drow = _fox_bwd(qkv, qaug, kaug, _aug_lanes(neg_dsum), d_oa)
    dq_b, dk_b, dv_b = _sb_bwd(qkv, lcar, d_ob)
    dc = dcrow.transpose(0, 2, 1, 3).reshape(N_HEADS, lp).T + drow[:, ::HEAD_DIM]
    dc = jnp.concatenate([dc, jnp.zeros((lp, LANES - N_HEADS), F32)], axis=1)
    df, db = _forget_bwd(dc, fpre, b_pad)
    lane = jnp.arange(LANES) < N_HEADS
    df = jnp.where(lane[None, :], df, 0.0)
    d_proj = jnp.concatenate(
        [dq_a, dk_a.astype(BF16), dv_a.astype(BF16), dq_b, dk_b.astype(BF16), dv_b.astype(BF16),
         d_ga, d_gb, df.astype(BF16), jnp.zeros((lp, F_PAD - LANES), BF16)], axis=1)
    w_in_p = jnp.concatenate([w_qkv, w_gf], axis=1)
    d_xn1 = _mm(d_proj, w_in_p, "nt", F32, "proj_dx")
    gw_in_p = _mm(xn1, d_proj, "tn", F32, "proj_dw")
    dh0, dg0 = _rmsnorm_bwd(h0, g0, d_xn1, dh1, F32, "norm1_bwd")

    qkv_parts = jnp.split(gw_in_p[:, :6 * WIDTH], 6, axis=1)
    gw_in = jnp.concatenate(
        qkv_parts[:3] + [gw_in_p[:, 6 * WIDTH + 2 * d:6 * WIDTH + 2 * d + N_HEADS]] + qkv_parts[3:]
        + [gw_in_p[:, 6 * WIDTH:6 * WIDTH + 2 * d]], axis=1)
    grads = {
        "meta_tokens": dh0[:N_META],
        "norm_gains": jnp.concatenate([dg0, dg1, dg2, dg3], axis=0),
        "w_in": gw_in,
        "b_forget": db[:, :N_HEADS],
        "w_o_fox": gw_fox,
        "w_o_sb": gw_sb,
        "w_out": gw_out,
        "w_up": gw_up,
        "conv_w": g_conv_w,
        "conv_b": g_conv_b,
        "w_down": gw_down,
    }
    return loss, dh0[N_META:n_valid], grads


MESH_IDS = pl.DeviceIdType.MESH


def _window(ref, kind, idx, rows, cols):
    if kind == "slots":
        return ref.at[idx]
    if kind == "cols":
        return ref.at[:, pl.ds(pl.multiple_of(idx * cols, cols & -cols), cols)]
    return ref.at[pl.ds(pl.multiple_of(idx * rows, rows & -rows), rows), :]


def _gathered_shape(shape, kind):
    rows, cols = shape
    return {"slots": (N_DEV, rows, cols), "cols": (rows, N_DEV * cols), "rows": (N_DEV * rows, cols)}[kind]


def _all_gather(shards, kinds):
    n = len(shards)

    def body(*refs):
        ins, outs = refs[:n], refs[n:2 * n]
        send_sems, recv_sems, local_sems = refs[2 * n:]
        x, y, c = lax.axis_index("x"), lax.axis_index("y"), lax.axis_index("c")
        me, sibling = (x, y, c), (x, y, 1 - c)
        chips = [(1 - x, y), (x, 1 - y), (1 - x, 1 - y)]

        def part(t, px, py, pc):
            return _window(outs[t], kinds[t], 4 * px + 2 * py + pc, *shards[t].shape)

        def copy(k, t, blk, to, src=None):
            return pltpu.make_async_remote_copy(
                src_ref=part(t, *blk) if src is None else src, dst_ref=part(t, *blk),
                send_sem=send_sems.at[k, t], recv_sem=recv_sems.at[k, t],
                device_id=to, device_id_type=MESH_IDS)

        mine = [pltpu.make_async_copy(ins[t], part(t, *me), local_sems.at[t]) for t in range(n)]
        for cp in mine:
            cp.start()
        first = [copy(0, t, me, sibling, src=ins[t]) for t in range(n)]
        first += [copy(1 + j, t, me, (*chip, c), src=ins[t]) for j, chip in enumerate(chips) for t in range(n)]
        for cp in first:
            cp.start()
        passed = []
        for j, chip in enumerate(chips):
            for t in range(n):
                copy(1 + j, t, (*chip, c), me).wait_recv()
                passed.append(copy(4 + j, t, (*chip, c), sibling))
                passed[-1].start()
        for t in range(n):
            copy(0, t, sibling, me).wait_recv()
        for j, chip in enumerate(chips):
            for t in range(n):
                copy(4 + j, t, (*chip, 1 - c), me).wait_recv()
        for cp in first + passed:
            cp.wait_send()
        for cp in mine:
            cp.wait()

    any_space = pl.BlockSpec(memory_space=pl.ANY)
    return pl.pallas_call(
        body, name="all_gather",
        out_shape=tuple(jax.ShapeDtypeStruct(_gathered_shape(a.shape, k), a.dtype) for a, k in zip(shards, kinds)),
        in_specs=[any_space] * n,
        out_specs=tuple([any_space] * n),
        scratch_shapes=[pltpu.SemaphoreType.DMA((7, n)), pltpu.SemaphoreType.DMA((7, n)),
                        pltpu.SemaphoreType.DMA((n,))],
    )(*shards)


def _exchange(grads, kinds, shard_shapes):
    n = len(grads)

    def body(*refs):
        ins, outs = refs[:n], refs[n:2 * n]
        send_sems, recv_sems, local_sems = refs[2 * n:]
        x, y, c = lax.axis_index("x"), lax.axis_index("y"), lax.axis_index("c")
        my = 4 * x + 2 * y + c

        def part(t, idx):
            return ins[t] if kinds[t] == "all" else _window(ins[t], kinds[t], idx, *shard_shapes[t])

        mine = [pltpu.make_async_copy(part(t, my), outs[t].at[my], local_sems.at[t]) for t in range(n)]
        for cp in mine:
            cp.start()
        copies = []
        for rel in range(1, N_DEV):
            px, py, pc = x ^ (rel >> 2), y ^ ((rel >> 1) & 1), c ^ (rel & 1)
            peer = 4 * px + 2 * py + pc
            for t in range(n):
                cp = pltpu.make_async_remote_copy(
                    src_ref=part(t, peer), dst_ref=outs[t].at[my],
                    send_sem=send_sems.at[rel - 1, t], recv_sem=recv_sems.at[rel - 1, t],
                    device_id=(px, py, pc), device_id_type=MESH_IDS)
                cp.start()
                arrival = pltpu.make_async_remote_copy(
                    src_ref=part(t, my), dst_ref=outs[t].at[peer],
                    send_sem=send_sems.at[rel - 1, t], recv_sem=recv_sems.at[rel - 1, t],
                    device_id=(px, py, pc), device_id_type=MESH_IDS)
                copies.append((cp, arrival))
        for _, arrival in copies:
            arrival.wait_recv()
        for cp, _ in copies:
            cp.wait_send()
        for cp in mine:
            cp.wait()

    any_space = pl.BlockSpec(memory_space=pl.ANY)
    return pl.pallas_call(
        body, name="grad_exchange",
        out_shape=tuple(jax.ShapeDtypeStruct((N_DEV,) + tuple(s), g.dtype) for g, s in zip(grads, shard_shapes)),
        in_specs=[any_space] * n,
        out_specs=tuple([any_space] * n),
        scratch_shapes=[pltpu.SemaphoreType.DMA((7, n)), pltpu.SemaphoreType.DMA((7, n)),
                        pltpu.SemaphoreType.DMA((n,))],
    )(*grads)


def _sum_adamw(parts, w, m, v, name):
    rows, cols = w.shape
    n, rows_p, cols_p = parts.shape
    tr = _tile(rows, BLK, SUBLANES) if rows > BLK else rows
    tp = tr if rows_p == rows else rows_p
    c1 = 1.0 - ADAM_B1 ** ADAM_STEP
    c2 = 1.0 - ADAM_B2 ** ADAM_STEP

    def body(p_ref, w_ref, m_ref, v_ref, g_ref, d_ref, nm_ref, nv_ref):
        gv = p_ref[0, 0:tr, 0:cols]
        for s in range(1, n):
            gv = gv + p_ref[s, 0:tr, 0:cols]
        g_ref[...] = gv
        nm = ADAM_B1 * m_ref[...] + (1.0 - ADAM_B1) * gv
        nv = ADAM_B2 * v_ref[...] + (1.0 - ADAM_B2) * (gv * gv)
        m_hat = nm / c1
        v_hat = nv / c2
        d_ref[...] = -ADAM_LR * (m_hat / (jnp.sqrt(v_hat) + ADAM_EPS) + ADAM_WD * w_ref[...])
        nm_ref[...] = nm
        nv_ref[...] = nv

    spec = pl.BlockSpec((tr, cols), lambda i: (i, 0))
    out = jax.ShapeDtypeStruct((rows, cols), F32)
    return pl.pallas_call(
        body, name=name,
        out_shape=(out, out, out, out),
        grid=(rows // tr,),
        in_specs=[pl.BlockSpec((n, tp, cols_p), lambda i: (0, i, 0)), spec, spec, spec],
        out_specs=(spec, spec, spec, spec),
        compiler_params=_params(("parallel",)),
    )(parts, w, m, v)


def _pad2(a, rows, cols):
    return jnp.pad(a, ((0, rows - a.shape[0]), (0, cols - a.shape[1])))


WEIGHTS = ["meta_tokens", "norm_gains", "w_in", "b_forget", "w_o_fox", "w_o_sb", "w_out", "w_up", "conv_w",
           "conv_b", "w_down"]


def kernel(x, meta_tokens, norm_gains, w_in, b_forget, w_o_fox, w_o_sb, w_out, w_up, conv_w, conv_b, w_down, loss_target, m_meta_tokens, m_norm_gains, m_w_in, m_b_forget, m_w_o_fox, m_w_o_sb, m_w_out, m_w_up, m_conv_w, m_conv_b, m_w_down, v_meta_tokens, v_norm_gains, v_w_in, v_b_forget, v_w_o_fox, v_w_o_sb, v_w_out, v_w_up, v_conv_w, v_conv_b, v_w_down):
    w = dict(meta_tokens=meta_tokens, norm_gains=norm_gains, w_in=w_in, b_forget=b_forget, w_o_fox=w_o_fox,
             w_o_sb=w_o_sb, w_out=w_out, w_up=w_up, conv_w=conv_w, conv_b=conv_b, w_down=w_down)
    mom = dict(meta_tokens=m_meta_tokens, norm_gains=m_norm_gains, w_in=m_w_in, b_forget=m_b_forget,
               w_o_fox=m_w_o_fox, w_o_sb=m_w_o_sb, w_out=m_w_out, w_up=m_w_up, conv_w=m_conv_w, conv_b=m_conv_b,
               w_down=m_w_down)
    vel = dict(meta_tokens=v_meta_tokens, norm_gains=v_norm_gains, w_in=v_w_in, b_forget=v_b_forget,
               w_o_fox=v_w_o_fox, w_o_sb=v_w_o_sb, w_out=v_w_out, w_up=v_w_up, conv_w=v_conv_w, conv_b=v_conv_b,
               w_down=v_w_down)
    w2 = {n: a.reshape(a.shape[-2:]) for n, a in w.items()}
    shard_shape = {n: a.shape for n, a in w2.items()}

    d = x.shape[-1]
    up_cols = shard_shape["w_up"][1]
    up_pad = -(-up_cols // LANES) * LANES
    half = N_DEV // 2
    pad_rows = lambda a: _pad2(a, SUBLANES, a.shape[1])

    shards = [
        ("w_in", "slots", w2["w_in"].astype(BF16)),
        ("w_up", "cols", _pad2(w2["w_up"], d, up_pad).astype(BF16)),
        ("w_o_fox", "cols", w2["w_o_fox"].astype(BF16)),
        ("w_o_sb", "cols", w2["w_o_sb"].astype(BF16)),
        ("w_out", "rows", w2["w_out"].astype(BF16)),
        ("w_down", "rows", w2["w_down"].astype(BF16)),
        ("meta_tokens", "cols", w2["meta_tokens"]),
        ("norm_gains", "cols", pad_rows(w2["norm_gains"])),
        ("conv_w", "cols", _pad2(w2["conv_w"], SUBLANES, up_pad)),
    ]
    full = dict(zip([s[0] for s in shards], _all_gather([s[2] for s in shards], [s[1] for s in shards])))
    w_in_full = jnp.concatenate([full["w_in"][i] for i in range(N_DEV)], axis=1)
    w_down_p = jnp.pad(full["w_down"].reshape(half, up_cols, d), ((0, 0), (0, up_pad - up_cols), (0, 0)))
    conv_b_p = jnp.pad(w2["conv_b"].reshape(N_DEV, up_cols), ((0, 0), (0, up_pad - up_cols)))

    loss, grad_x, grads = _local_step(
        x[0], loss_target[0], full["meta_tokens"], full["norm_gains"][:4], w_in_full, w2["b_forget"],
        full["w_o_fox"], full["w_o_sb"], full["w_out"], full["w_up"], full["conv_w"][:3],
        conv_b_p.reshape(1, N_DEV * up_pad), w_down_p.reshape(half * up_pad, d))
    loss = lax.psum(loss, ("x", "y", "c"))

    in_cols = shard_shape["w_in"][1]
    sends = {
        "meta_tokens": ("cols", grads["meta_tokens"], (N_META, LANES)),
        "norm_gains": ("cols", pad_rows(grads["norm_gains"]), (SUBLANES, LANES)),
        "w_in": ("slots", jnp.stack([grads["w_in"][:, i * in_cols:(i + 1) * in_cols] for i in range(N_DEV)]),
                 shard_shape["w_in"]),
        "b_forget": ("all", _pad2(grads["b_forget"], SUBLANES, LANES), (SUBLANES, LANES)),
        "w_o_fox": ("cols", grads["w_o_fox"], shard_shape["w_o_fox"]),
        "w_o_sb": ("cols", grads["w_o_sb"], shard_shape["w_o_sb"]),
        "w_out": ("rows", grads["w_out"], shard_shape["w_out"]),
        "w_up": ("cols", grads["w_up"], (d, up_pad)),
        "conv_w": ("cols", pad_rows(grads["conv_w"]), (SUBLANES, up_pad)),
        "conv_b": ("all", pad_rows(grads["conv_b"].reshape(N_DEV, up_pad)[:, :up_cols].reshape(1, -1)),
                   (SUBLANES, N_DEV * up_cols)),
        "w_down": ("rows", grads["w_down"].reshape(half, up_pad, d)[:, :up_cols].reshape(half * up_cols, d),
                   shard_shape["w_down"]),
    }
    parts = dict(zip(WEIGHTS, _exchange([sends[n][1] for n in WEIGHTS], [sends[n][0] for n in WEIGHTS],
                                        [sends[n][2] for n in WEIGHTS])))

    grad, delta, new_m, new_v = {}, {}, {}, {}
    for n in WEIGHTS:
        shape = w[n].shape
        outs = _sum_adamw(parts[n], w2[n], mom[n].reshape(shard_shape[n]), vel[n].reshape(shard_shape[n]),
                          "adamw_" + n)
        grad[n], delta[n], new_m[n], new_v[n] = (o.reshape(shape) for o in outs)

    return (loss, grad_x[None], *[grad[n] for n in WEIGHTS], *[delta[n] for n in WEIGHTS],
            *[new_m[n] for n in WEIGHTS], *[new_v[n] for n in WEIGHTS])
```

```python
import functools
import math

import jax
import jax.numpy as jnp
from jax import lax
from jax.experimental import pallas as pl
from jax.experimental.pallas import tpu as pltpu

F32 = jnp.float32
BF16 = jnp.bfloat16

N_DEV = 8
N_META = 16
HEAD_DIM = 64
N_HEADS = 8
WIDTH = N_HEADS * HEAD_DIM
N_PAIRS = N_HEADS // 2
LANES = 128
SUBLANES = 8
EPS = 1e-6
ATT_SCALE = HEAD_DIM ** -0.5
BLK = 256
F_PAD = 256
VMEM_LIMIT = 48 << 20

ADAM_LR = 0.001
ADAM_B1 = 0.9
ADAM_B2 = 0.999
ADAM_EPS = 1e-08
ADAM_WD = 0.01
ADAM_STEP = 10

GELU_C = math.sqrt(2.0 / math.pi)
GELU_A = 0.044715


def _params(sem, vmem=VMEM_LIMIT):
    return pltpu.CompilerParams(dimension_semantics=sem, vmem_limit_bytes=vmem)


def _tile(dim, cap, align=LANES):
    t = (min(cap, dim) // align) * align
    while t >= align:
        if dim % t == 0:
            return t
        t -= align
    return dim


def _log_sigmoid_parts(z):
    lp = jnp.log1p(jnp.exp(-jnp.abs(z)))
    return jnp.minimum(z, 0.0) - lp, jnp.minimum(-z, 0.0) - lp


def _sigmoid(x):
    return 1.0 / (1.0 + jnp.exp(-x))


def _split_bf16(x):
    hi = x.astype(BF16)
    lo = (x - hi.astype(F32)).astype(BF16)
    return hi, lo


def _dot(a, b, dims):
    return lax.dot_general(a, b, (dims, ((), ())), preferred_element_type=F32)


NN = ((1,), (0,))
NT = ((1,), (1,))
TN = ((0,), (0,))


def _mm(a, b, mode, out_dtype, name):
    if mode == "nn":
        (m, kc), (_, n) = a.shape, b.shape
    elif mode == "nt":
        (m, kc), (n, _) = a.shape, b.shape
    else:
        (kc, m), (_, n) = a.shape, b.shape
    tm = _tile(m, 768)
    tn = _tile(n, 1024)
    tk = _tile(kc, 1536 if mode != "tn" else 768)
    nk = kc // tk
    dims = {"nn": NN, "nt": NT, "tn": TN}[mode]

    def body(a_ref, b_ref, o_ref, acc_ref):
        k = pl.program_id(2)

        @pl.when(k == 0)
        def _():
            acc_ref[...] = jnp.zeros_like(acc_ref)

        acc_ref[...] += _dot(a_ref[...].astype(BF16), b_ref[...].astype(BF16), dims)

        @pl.when(k == nk - 1)
        def _():
            o_ref[...] = acc_ref[...].astype(out_dtype)

    if mode == "tn":
        a_spec = pl.BlockSpec((tk, tm), lambda j, i, k: (k, i))
    else:
        a_spec = pl.BlockSpec((tm, tk), lambda j, i, k: (i, k))
    if mode == "nt":
        b_spec = pl.BlockSpec((tn, tk), lambda j, i, k: (j, k))
    else:
        b_spec = pl.BlockSpec((tk, tn), lambda j, i, k: (k, j))
    return pl.pallas_call(
        body, name=name,
        out_shape=jax.ShapeDtypeStruct((m, n), out_dtype),
        grid=(n // tn, m // tm, nk),
        in_specs=[a_spec, b_spec],
        out_specs=pl.BlockSpec((tm, tn), lambda j, i, k: (i, j)),
        scratch_shapes=[pltpu.VMEM((tm, tn), F32)],
        compiler_params=_params(("parallel", "parallel", "arbitrary")),
    )(a, b)


def _rmsnorm_fwd(x, g, name):
    lp, d = x.shape

    def body(x_ref, g_ref, o_ref):
        xv = x_ref[...]
        r = lax.rsqrt(jnp.mean(xv * xv, axis=-1, keepdims=True) + EPS)
        o_ref[...] = ((xv * r) * g_ref[...]).astype(BF16)

    return pl.pallas_call(
        body, name=name,
        out_shape=jax.ShapeDtypeStruct((lp, d), BF16),
        grid=(lp // BLK,),
        in_specs=[pl.BlockSpec((BLK, d), lambda i: (i, 0)), pl.BlockSpec((1, d), lambda i: (0, 0))],
        out_specs=pl.BlockSpec((BLK, d), lambda i: (i, 0)),
        compiler_params=_params(("parallel",)),
    )(x, g)


def _rmsnorm_bwd(x, g, dy, resid, out_dtype, name):
    lp, d = x.shape
    has_resid = resid is not None

    def body(*refs):
        if has_resid:
            x_ref, g_ref, dy_ref, r_ref, dx_ref, dg_ref = refs
        else:
            x_ref, g_ref, dy_ref, dx_ref, dg_ref = refs
        i = pl.program_id(0)
        xv = x_ref[...]
        dyv = dy_ref[...].astype(F32)
        r = lax.rsqrt(jnp.mean(xv * xv, axis=-1, keepdims=True) + EPS)
        xh = xv * r
        dyg = dyv * g_ref[...]
        dx = r * (dyg - xh * jnp.mean(dyg * xh, axis=-1, keepdims=True))
        if has_resid:
            dx = dx + r_ref[...]
        dx_ref[...] = dx.astype(out_dtype)

        @pl.when(i == 0)
        def _():
            dg_ref[...] = jnp.zeros_like(dg_ref)

        dg_ref[...] += jnp.sum(dyv * xh, axis=0, keepdims=True)

    row = pl.BlockSpec((BLK, d), lambda i: (i, 0))
    vec = pl.BlockSpec((1, d), lambda i: (0, 0))
    ins = [x, g, dy] + ([resid] if has_resid else [])
    in_specs = [row, vec, row] + ([row] if has_resid else [])
    return pl.pallas_call(
        body, name=name,
        out_shape=(jax.ShapeDtypeStruct((lp, d), out_dtype), jax.ShapeDtypeStruct((1, d), F32)),
        grid=(lp // BLK,),
        in_specs=in_specs,
        out_specs=(row, vec),
        compiler_params=_params(("arbitrary",)),
    )(*ins)


def _forget_fwd(fpre, b_pad):
    lp = fpre.shape[0]

    def body(f_ref, b_ref, c_ref, carry_ref):
        i = pl.program_id(0)

        @pl.when(i == 0)
        def _():
            carry_ref[...] = jnp.zeros_like(carry_ref)

        logf, _ = _log_sigmoid_parts(f_ref[...] + b_ref[...])
        row = lax.broadcasted_iota(jnp.int32, (BLK, BLK), 0)
        col = lax.broadcasted_iota(jnp.int32, (BLK, BLK), 1)
        tri = (col <= row).astype(BF16)
        p0 = logf.astype(BF16)
        r1 = logf - p0.astype(F32)
        p1 = r1.astype(BF16)
        p2 = (r1 - p1.astype(F32)).astype(BF16)
        c = _dot(tri, p0, NN) + _dot(tri, p1, NN) + _dot(tri, p2, NN) + carry_ref[0:1, :]
        c_ref[...] = c
        carry_ref[...] = jnp.broadcast_to(c[BLK - 1:BLK, :], carry_ref.shape)

    return pl.pallas_call(
        body, name="forget_fwd",
        out_shape=jax.ShapeDtypeStruct((lp, LANES), F32),
        grid=(lp // BLK,),
        in_specs=[pl.BlockSpec((BLK, LANES), lambda i: (i, 0)), pl.BlockSpec((1, LANES), lambda i: (0, 0))],
        out_specs=pl.BlockSpec((BLK, LANES), lambda i: (i, 0)),
        scratch_shapes=[pltpu.VMEM((SUBLANES, LANES), F32)],
        compiler_params=_params(("arbitrary",)),
    )(fpre, b_pad)


def _forget_bwd(dc, fpre, b_pad):
    lp = fpre.shape[0]
    nb = lp // BLK

    def body(dc_ref, f_ref, b_ref, df_ref, db_ref, carry_ref):
        i = pl.program_id(0)

        @pl.when(i == 0)
        def _():
            carry_ref[...] = jnp.zeros_like(carry_ref)
            db_ref[...] = jnp.zeros_like(db_ref)

        dcv = dc_ref[...]
        row = lax.broadcasted_iota(jnp.int32, (BLK, BLK), 0)
        col = lax.broadcasted_iota(jnp.int32, (BLK, BLK), 1)
        tri = (col >= row).astype(BF16)
        p0 = dcv.astype(BF16)
        r1 = dcv - p0.astype(F32)
        p1 = r1.astype(BF16)
        p2 = (r1 - p1.astype(F32)).astype(BF16)
        dlogf = _dot(tri, p0, NN) + _dot(tri, p1, NN) + _dot(tri, p2, NN) + carry_ref[0:1, :]
        carry_ref[...] = jnp.broadcast_to(dlogf[0:1, :], carry_ref.shape)
        _, ls_neg = _log_sigmoid_parts(f_ref[...] + b_ref[...])
        df = dlogf * jnp.exp(ls_neg)
        df_ref[...] = df
        db_ref[...] += jnp.sum(df, axis=0, keepdims=True)

    rev = pl.BlockSpec((BLK, LANES), lambda i: (nb - 1 - i, 0))
    vec = pl.BlockSpec((1, LANES), lambda i: (0, 0))
    return pl.pallas_call(
        body, name="forget_bwd",
        out_shape=(jax.ShapeDtypeStruct((lp, LANES), F32), jax.ShapeDtypeStruct((1, LANES), F32)),
        grid=(nb,),
        in_specs=[rev, rev, vec],
        out_specs=(rev, vec),
        scratch_shapes=[pltpu.VMEM((SUBLANES, LANES), F32)],
        compiler_params=_params(("arbitrary",)),
    )(dc, fpre, b_pad)


def _mix_fwd(o_a, o_b, gates, h0, w_fox, w_sb, w_out, g1):
    lp, d = h0.shape

    def body(oa_ref, ob_ref, ga_ref, gb_ref, h_ref, wf_ref, ws_ref, wo_ref, g_ref,
             h1_ref, ya_ref, yb_ref, gated_ref, mixed_ref):
        ya = _dot(oa_ref[...].astype(BF16), wf_ref[...], NN)
        yb = _dot(ob_ref[...].astype(BF16), ws_ref[...], NN)
        gated = _sigmoid(ga_ref[...]) * ya + _sigmoid(gb_ref[...]) * yb
        gb16 = gated.astype(BF16)
        mixed = _dot(gb16, wo_ref[...], NN)
        r = lax.rsqrt(jnp.mean(mixed * mixed, axis=-1, keepdims=True) + EPS)
        h1_ref[...] = h_ref[...] + (mixed * r) * g_ref[...]
        ya_ref[...] = ya
        yb_ref[...] = yb
        gated_ref[...] = gb16
        mixed_ref[...] = mixed

    row_w = pl.BlockSpec((BLK, WIDTH), lambda i: (i, 0))
    row_d = pl.BlockSpec((BLK, d), lambda i: (i, 0))
    full = lambda s: pl.BlockSpec(s, lambda i: (0, 0))
    return pl.pallas_call(
        body, name="mix_fwd",
        out_shape=(jax.ShapeDtypeStruct((lp, d), F32), jax.ShapeDtypeStruct((lp, d), F32),
                   jax.ShapeDtypeStruct((lp, d), F32), jax.ShapeDtypeStruct((lp, d), BF16),
                   jax.ShapeDtypeStruct((lp, d), F32)),
        grid=(lp // BLK,),
        in_specs=[row_w, row_w, row_d, pl.BlockSpec((BLK, d), lambda i: (i, 1)), row_d,
                  full((WIDTH, d)), full((WIDTH, d)), full((d, d)), full((1, d))],
        out_specs=(row_d, row_d, row_d, row_d, row_d),
        compiler_params=_params(("parallel",)),
    )(o_a, o_b, gates, gates, h0, w_fox, w_sb, w_out, g1)


def _gate_bwd(d_gated, gates, ya, yb):
    lp, d = d_gated.shape

    def body(dg_ref, ga_ref, gb_ref, ya_ref, yb_ref, dya_ref, dyb_ref, dga_ref, dgb_ref):
        dg = dg_ref[...]
        sa = _sigmoid(ga_ref[...])
        sb = _sigmoid(gb_ref[...])
        dya_ref[...] = (dg * sa).astype(BF16)
        dyb_ref[...] = (dg * sb).astype(BF16)
        dga_ref[...] = (dg * ya_ref[...] * (sa * (1.0 - sa))).astype(BF16)
        dgb_ref[...] = (dg * yb_ref[...] * (sb * (1.0 - sb))).astype(BF16)

    row = pl.BlockSpec((BLK, d), lambda i: (i, 0))
    out = jax.ShapeDtypeStruct((lp, d), BF16)
    return pl.pallas_call(
        body, name="gate_bwd",
        out_shape=(out, out, out, out),
        grid=(lp // BLK,),
        in_specs=[row, row, pl.BlockSpec((BLK, d), lambda i: (i, 1)), row, row],
        out_specs=(row, row, row, row),
        compiler_params=_params(("parallel",)),
    )(d_gated, gates, gates, ya, yb)


def _shift_down(cur, prev, n):
    rolled = pltpu.roll(cur, n, 0)
    row = lax.broadcasted_iota(jnp.int32, cur.shape, 0)
    for t in range(n):
        rolled = jnp.where(row == t, prev[SUBLANES - n + t:SUBLANES - n + t + 1, :], rolled)
    return rolled


def _shift_up(cur, nxt, n):
    rows = cur.shape[0]
    rolled = pltpu.roll(cur, rows - n, 0)
    row = lax.broadcasted_iota(jnp.int32, cur.shape, 0)
    for t in range(n):
        rolled = jnp.where(row == rows - n + t, nxt[t:t + 1, :], rolled)
    return rolled


def _gelu(x):
    return 0.5 * x * (1.0 + jnp.tanh(GELU_C * (x + GELU_A * (x * x * x))))


def _gelu_and_grad(x):
    t = jnp.tanh(GELU_C * (x + GELU_A * (x * x * x)))
    half = 0.5 * (1.0 + t)
    return x * half, half + 0.5 * x * (1.0 - t * t) * (GELU_C * (1.0 + 3.0 * GELU_A * (x * x)))


def _conv_taps(cur, prev, w_ref, b_ref):
    s1 = _shift_down(cur, prev, 1)
    s2 = _shift_down(cur, prev, 2)
    u = b_ref[...] + w_ref[0:1, :] * s2
    u = u + w_ref[1:2, :] * s1
    u = u + w_ref[2:3, :] * cur
    return u, s1, s2


def _conv_gelu_fwd(up, conv_w, conv_b, tc):
    lp, f2 = up.shape
    rb = BLK // SUBLANES

    def body(u_ref, p_ref, w_ref, b_ref, act_ref):
        i = pl.program_id(0)
        keep = (i > 0).astype(F32)
        u, _, _ = _conv_taps(u_ref[...], p_ref[...] * keep, w_ref, b_ref)
        act_ref[...] = (_gelu(u[:, :tc]) * u[:, tc:]).astype(BF16)

    prev_row = lambda i: jnp.maximum(i * rb - 1, 0)
    return pl.pallas_call(
        body, name="conv_gelu_fwd",
        out_shape=jax.ShapeDtypeStruct((lp, f2 // 2), BF16),
        grid=(lp // BLK, f2 // (2 * tc)),
        in_specs=[pl.BlockSpec((BLK, 2 * tc), lambda i, j: (i, j)),
                  pl.BlockSpec((SUBLANES, 2 * tc), lambda i, j: (prev_row(i), j)),
                  pl.BlockSpec((3, 2 * tc), lambda i, j: (0, j)),
                  pl.BlockSpec((1, 2 * tc), lambda i, j: (0, j))],
        out_specs=pl.BlockSpec((BLK, tc), lambda i, j: (i, j)),
        compiler_params=_params(("parallel", "parallel")),
    )(up, up, conv_w, conv_b)


def _conv_gelu_bwd(up, d_act, conv_w, conv_b, tc):
    lp, f2 = up.shape
    nb = lp // BLK
    rb = BLK // SUBLANES

    def du_of(u, da):
        gel, grad = _gelu_and_grad(u[:, :tc])
        return jnp.concatenate([da * u[:, tc:] * grad, da * gel], axis=1)

    def body(u_ref, p_ref, n_ref, da_ref, dan_ref, w_ref, b_ref, dup_ref, dcw_ref, dcb_ref):
        i = pl.program_id(1)
        cur = u_ref[...]
        u, s1, s2 = _conv_taps(cur, p_ref[...] * (i > 0).astype(F32), w_ref, b_ref)
        du = du_of(u, da_ref[...])
        u_next, _, _ = _conv_taps(n_ref[...], cur[BLK - SUBLANES:BLK, :], w_ref, b_ref)
        du_next = du_of(u_next, dan_ref[...]) * (i < nb - 1).astype(F32)
        n1 = _shift_up(du, du_next, 1)
        n2 = _shift_up(du, du_next, 2)
        dup_ref[...] = (w_ref[2:3, :] * du + w_ref[1:2, :] * n1 + w_ref[0:1, :] * n2).astype(BF16)

        @pl.when(i == 0)
        def _():
            dcw_ref[...] = jnp.zeros_like(dcw_ref)
            dcb_ref[...] = jnp.zeros_like(dcb_ref)

        dcw_ref[0:1, :] += jnp.sum(du * s2, axis=0, keepdims=True)
        dcw_ref[1:2, :] += jnp.sum(du * s1, axis=0, keepdims=True)
        dcw_ref[2:3, :] += jnp.sum(du * cur, axis=0, keepdims=True)
        dcb_ref[...] += jnp.sum(du, axis=0, keepdims=True)

    prev_row = lambda i: jnp.maximum(i * rb - 1, 0)
    next_row = lambda i: jnp.minimum((i + 1) * rb, nb * rb - 1)
    return pl.pallas_call(
        body, name="conv_gelu_bwd",
        out_shape=(jax.ShapeDtypeStruct((lp, f2), BF16), jax.ShapeDtypeStruct((3, f2), F32),
                   jax.ShapeDtypeStruct((1, f2), F32)),
        grid=(f2 // (2 * tc), nb),
        in_specs=[pl.BlockSpec((BLK, 2 * tc), lambda j, i: (i, j)),
                  pl.BlockSpec((SUBLANES, 2 * tc), lambda j, i: (prev_row(i), j)),
                  pl.BlockSpec((SUBLANES, 2 * tc), lambda j, i: (next_row(i), j)),
                  pl.BlockSpec((BLK, tc), lambda j, i: (i, j)),
                  pl.BlockSpec((SUBLANES, tc), lambda j, i: (next_row(i), j)),
                  pl.BlockSpec((3, 2 * tc), lambda j, i: (0, j)),
                  pl.BlockSpec((1, 2 * tc), lambda j, i: (0, j))],
        out_specs=(pl.BlockSpec((BLK, 2 * tc), lambda j, i: (i, j)),
                   pl.BlockSpec((3, 2 * tc), lambda j, i: (0, j)),
                   pl.BlockSpec((1, 2 * tc), lambda j, i: (0, j))),
        compiler_params=_params(("parallel", "arbitrary")),
    )(up, up, up, d_act, d_act, conv_w, conv_b)


def _out_loss(h1, ffn, g3, target, n_valid):
    lp, d = h1.shape

    def body(h_ref, f_ref, g_ref, t_ref, dy_ref, loss_ref):
        i = pl.program_id(0)

        @pl.when(i == 0)
        def _():
            loss_ref[...] = jnp.zeros_like(loss_ref)

        fv = f_ref[...]
        r = lax.rsqrt(jnp.mean(fv * fv, axis=-1, keepdims=True) + EPS)
        y = h_ref[...] + (fv * r) * g_ref[...]
        row = i * BLK + lax.broadcasted_iota(jnp.int32, (BLK, 1), 0)
        valid = (row >= N_META) & (row < n_valid)
        diff = jnp.where(valid, y - t_ref[...], 0.0)
        dy_ref[...] = diff * (1.0 / d)
        per_row = jnp.mean(diff * diff, axis=-1, keepdims=True)
        loss_ref[...] += 0.5 * jnp.sum(per_row, axis=0, keepdims=True)

    row_d = pl.BlockSpec((BLK, d), lambda i: (i, 0))
    return pl.pallas_call(
        body, name="out_loss",
        out_shape=(jax.ShapeDtypeStruct((lp, d), F32), jax.ShapeDtypeStruct((SUBLANES, LANES), F32)),
        grid=(lp // BLK,),
        in_specs=[row_d, row_d, pl.BlockSpec((1, d), lambda i: (0, 0)), row_d],
        out_specs=(row_d, pl.BlockSpec((SUBLANES, LANES), lambda i: (0, 0))),
        compiler_params=_params(("arbitrary",)),
    )(h1, ffn, g3, target)


def _head_masks(rows=BLK):
    lane = lax.broadcasted_iota(jnp.int32, (rows, LANES), 1)
    return [lane < HEAD_DIM, lane >= HEAD_DIM]


def _att_specs(base, lp):
    q_spec = pl.BlockSpec((BLK, LANES), lambda p, i: (i, base + p))
    k_spec = pl.BlockSpec((lp, LANES), lambda p, i: (0, base + N_PAIRS + p))
    v_spec = pl.BlockSpec((lp, LANES), lambda p, i: (0, base + 2 * N_PAIRS + p))
    return q_spec, k_spec, v_spec


def _kv_rows(j, nb=1):
    return pl.ds(pl.multiple_of(j * BLK, nb * BLK), nb * BLK)


def _walk_kv(i, tile, reverse=False):
    pairs, odd = i // 2, i % 2

    def wide(t, carry):
        tile(2 * (pairs - 1 - t) if reverse else 2 * t, 2, False)
        return carry

    def single():
        @pl.when(odd == 1)
        def _():
            tile(i - 1, 1, False)

    if reverse:
        tile(i, 1, True)
        single()
        lax.fori_loop(0, pairs, wide, 0)
    else:
        lax.fori_loop(0, pairs, wide, 0)
        single()
        tile(i, 1, True)


def _bf16_pieces(x):
    rnd = lambda a: lax.reduce_precision(a, exponent_bits=8, mantissa_bits=7)
    p0 = rnd(x)
    p1 = rnd(x - p0)
    p2 = rnd(x - p0 - p1)
    return [p0.astype(BF16), p1.astype(BF16), p2.astype(BF16)]


def _aug_lanes(cols):
    lp = cols[0].shape[0]
    vals = jnp.stack([c.astype(BF16) for c in cols], axis=-1)
    vals = vals.reshape(lp, N_PAIRS, 2, len(cols))[:, :, ::-1, :]
    vals = jnp.pad(vals, ((0, 0), (0, 0), (0, 0), (0, HEAD_DIM - len(cols))))
    return vals.reshape(lp, WIDTH)


N_AUG = 3


def _fox_fwd(qkv, qaug, kaug):
    lp = qkv.shape[0]
    nq = lp // BLK

    def body(q_ref, k_ref, v_ref, qa_ref, ka_ref, o_ref, lse_ref, acc_ref, m_ref):
        i = pl.program_id(1)
        hs = range(2)
        masks = _head_masks()
        qs = q_ref[...] * ATT_SCALE
        qa = qa_ref[...]
        qh = [jnp.where(masks[hh], qs, qa) for hh in hs]
        acc_ref[...] = jnp.zeros_like(acc_ref)
        m_ref[...] = jnp.full_like(m_ref, -1e30)
        row = lax.broadcasted_iota(jnp.int32, (BLK, BLK), 0)
        col = lax.broadcasted_iota(jnp.int32, (BLK, BLK), 1)
        causal = col <= row

        def tile(j, nb, diag):
            rows = _kv_rows(j, nb)
            kmasks = _head_masks(nb * BLK)
            k, ka, v = k_ref[rows, :], ka_ref[rows, :], v_ref[rows, :]
            kh = [jnp.where(kmasks[hh], k, ka) for hh in hs]
            vh = [jnp.where(kmasks[hh], v, jnp.ones_like(v)) for hh in hs]
            s = [_dot(qh[hh], kh[hh], NT) for hh in hs]
            if diag:
                s = [jnp.where(causal, s[hh], -1e30) for hh in hs]
            m_prev = [m_ref[hh] for hh in hs]
            m_new = [jnp.maximum(m_prev[hh], jnp.max(s[hh], axis=-1, keepdims=True)) for hh in hs]
            p = [jnp.exp(s[hh] - m_new[hh]).astype(BF16) for hh in hs]
            for hh in hs:
                acc_ref[hh] = jnp.exp(m_prev[hh] - m_new[hh]) * acc_ref[hh] + _dot(p[hh], vh[hh], NN)
                m_ref[hh] = m_new[hh]

        _walk_kv(i, tile)
        acc = [acc_ref[hh] for hh in hs]
        denom = [acc[0][:, HEAD_DIM:HEAD_DIM + 1], acc[1][:, 0:1]]
        o_ref[...] = jnp.where(masks[0], acc[0] / denom[0], acc[1] / denom[1])
        lse_ref[...] = jnp.where(masks[0], m_ref[0] + jnp.log(denom[0]), m_ref[1] + jnp.log(denom[1]))

    q_spec, k_spec, v_spec = _att_specs(0, lp)
    blk = pl.BlockSpec((BLK, LANES), lambda p, i: (i, p))
    col_full = pl.BlockSpec((lp, LANES), lambda p, i: (0, p))
    out = jax.ShapeDtypeStruct((lp, WIDTH), F32)
    return pl.pallas_call(
        body, name="fox_fwd",
        out_shape=(out, out),
        grid=(N_PAIRS, nq),
        in_specs=[q_spec, k_spec, v_spec, blk, col_full],
        out_specs=(blk, blk),
        scratch_shapes=[pltpu.VMEM((2, BLK, LANES), F32), pltpu.VMEM((2, BLK, 1), F32)],
        compiler_params=_params(("parallel", "parallel")),
    )(qkv, qkv, qkv, qaug, kaug)


def _head_dots(a, b):
    lp = a.shape[0]

    def body(a_ref, b_ref, o_ref):
        lane = lax.broadcasted_iota(jnp.int32, (BLK, LANES), 1)
        out = jnp.zeros((BLK, LANES), F32)
        for p in range(N_PAIRS):
            cols = slice(p * LANES, (p + 1) * LANES)
            prod = a_ref[:, cols] * b_ref[:, cols]
            for hh in range(2):
                part = jnp.where((lane >= HEAD_DIM) == (hh == 1), prod, 0.0)
                out = jnp.where(lane == 2 * p + hh, jnp.sum(part, axis=-1, keepdims=True), out)
        o_ref[...] = out

    row = pl.BlockSpec((BLK, WIDTH), lambda i: (i, 0))
    return pl.pallas_call(
        body, name="head_dots",
        out_shape=jax.ShapeDtypeStruct((lp, LANES), F32),
        grid=(lp // BLK,),
        in_specs=[row, row],
        out_specs=pl.BlockSpec((BLK, LANES), lambda i: (i, 0)),
        compiler_params=_params(("parallel",)),
    )(a, b)


def _fox_bwd(qkv, qaug, kaug, doaug, d_o):
    lp = qkv.shape[0]
    nq = lp // BLK

    def body(q_ref, k_ref, v_ref, qa_ref, ka_ref, da_ref, do_ref,
             dq_ref, dk_ref, dv_ref, dc_ref, dr_ref, acc_ref, rs_ref):
        i = pl.program_id(1)
        hs = range(2)
        masks = _head_masks()
        qs = q_ref[...] * ATT_SCALE
        qa = qa_ref[...]
        qm = [jnp.where(masks[hh], qs, 0) for hh in hs]
        qh = [jnp.where(masks[hh], qs, qa) for hh in hs]
        dov = do_ref[...].astype(BF16)
        doa = da_ref[...]
        dom = [jnp.where(masks[hh], dov, 0) for hh in hs]
        doh = [jnp.where(masks[hh], dov, doa) for hh in hs]
        row = lax.broadcasted_iota(jnp.int32, (BLK, BLK), 0)
        col = lax.broadcasted_iota(jnp.int32, (BLK, BLK), 1)
        causal = col <= row

        @pl.when(i == 0)
        def _():
            dk_ref[...] = jnp.zeros_like(dk_ref)
            dv_ref[...] = jnp.zeros_like(dv_ref)
            dc_ref[...] = jnp.zeros_like(dc_ref)

        acc_ref[...] = jnp.zeros_like(acc_ref)
        rs_ref[...] = jnp.zeros_like(rs_ref)

        def tile(j, nb, diag):
            rows = _kv_rows(j, nb)
            kmasks = _head_masks(nb * BLK)
            k, ka, v = k_ref[rows, :], ka_ref[rows, :], v_ref[rows, :]
            ones = (lax.broadcasted_iota(jnp.int32, v.shape, 1) % HEAD_DIM < N_AUG).astype(BF16)
            kh = [jnp.where(kmasks[hh], k, ka) for hh in hs]
            vh = [jnp.where(kmasks[hh], v, ones) for hh in hs]
            logp = [_dot(qh[hh], kh[hh], NT) for hh in hs]
            dp = [_dot(doh[hh], vh[hh], NT) for hh in hs]
            p = [jnp.exp(logp[hh]) for hh in hs]
            if diag:
                p = [jnp.where(causal, p[hh], 0.0) for hh in hs]
            ds = [p[hh] * dp[hh] for hh in hs]
            dsb = [ds[hh].astype(BF16) for hh in hs]
            for hh in hs:
                acc_ref[hh] += _dot(dsb[hh], k, NN)
                col_sums = jnp.sum(ds[hh], axis=0, keepdims=True)
                for b in range(nb):
                    dc_ref[0, j + b, hh:hh + 1, :] -= col_sums[:, b * BLK:(b + 1) * BLK]
                rs_ref[hh] += jnp.sum(ds[hh], axis=-1, keepdims=True)
            dk_ref[rows, :] += _dot(dsb[0], qm[0], TN) + _dot(dsb[1], qm[1], TN)
            dv_ref[rows, :] += _dot(p[0].astype(BF16), dom[0], TN) + _dot(p[1].astype(BF16), dom[1], TN)

        _walk_kv(i, tile)
        dq_ref[...] = (jnp.where(masks[0], acc_ref[0], acc_ref[1]) * ATT_SCALE).astype(BF16)
        dr_ref[...] = jnp.where(masks[0], rs_ref[0], rs_ref[1])

    q_spec, k_spec, v_spec = _att_specs(0, lp)
    blk = pl.BlockSpec((BLK, LANES), lambda p, i: (i, p))
    col_full = pl.BlockSpec((lp, LANES), lambda p, i: (0, p))
    crow_spec = pl.BlockSpec((1, nq, 2, BLK), lambda p, i: (p, 0, 0, 0))
    return pl.pallas_call(
        body, name="fox_bwd",
        out_shape=(jax.ShapeDtypeStruct((lp, WIDTH), BF16), jax.ShapeDtypeStruct((lp, WIDTH), F32),
                   jax.ShapeDtypeStruct((lp, WIDTH), F32), jax.ShapeDtypeStruct((N_PAIRS, nq, 2, BLK), F32),
                   jax.ShapeDtypeStruct((lp, WIDTH), F32)),
        grid=(N_PAIRS, nq),
        in_specs=[q_spec, k_spec, v_spec, blk, col_full, blk, blk],
        out_specs=(blk, col_full, col_full, crow_spec, blk),
        scratch_shapes=[pltpu.VMEM((2, BLK, LANES), F32), pltpu.VMEM((2, BLK, 1), F32)],
        compiler_params=_params(("parallel", "arbitrary")),
    )(qkv, qkv, qkv, qaug, kaug, doaug, d_o)


def _sb_scores(z):
    ell = jnp.minimum(z, 0.0) - jnp.log(1.0 + jnp.exp(-jnp.abs(z)))
    return ell, ell - z


def _stacked(tri):
    return jnp.concatenate([tri, tri], axis=0)


def _cumsum_dot(x, tri2):
    hi, lo = _split_bf16(x)
    return _dot(jnp.concatenate([hi, lo], axis=1), tri2, NN)


def _sb_units(qh, k, strict2, causal, nb, diag):
    hs, bs = range(2), range(nb)
    z = [_dot(qh[hh], k, NT) for hh in hs]
    sc = [[_sb_scores(z[hh][:, b * BLK:(b + 1) * BLK]) for b in bs] for hh in hs]
    ell = [[sc[hh][b][0] for b in bs] for hh in hs]
    kap = [[jnp.where(causal, sc[hh][b][1], 0.0) if diag else sc[hh][b][1] for b in bs] for hh in hs]
    later = [[_cumsum_dot(kap[hh][b], strict2) for b in bs] for hh in hs]
    return ell, kap, later


def _row_sum(x):
    return jnp.sum(x, axis=-1, keepdims=True)


def _join(blocks):
    joined = blocks[0] if len(blocks) == 1 else jnp.concatenate(blocks, axis=1)
    return joined.astype(BF16)


def _sb_fwd(qkv):
    lp = qkv.shape[0]
    nq = lp // BLK
    assert nq <= HEAD_DIM

    def body(q_ref, k_ref, v_ref, o_ref, lc_ref, acc_ref, car_ref):
        i = pl.program_id(1)
        masks = _head_masks()
        qs = q_ref[...] * ATT_SCALE
        qh = [jnp.where(mk, qs, 0).astype(BF16) for mk in masks]
        row = lax.broadcasted_iota(jnp.int32, (BLK, BLK), 0)
        col = lax.broadcasted_iota(jnp.int32, (BLK, BLK), 1)
        lane = lax.broadcasted_iota(jnp.int32, (BLK, LANES), 1)
        causal = col < row
        strict2 = _stacked((row > col).astype(BF16))
        acc_ref[...] = jnp.zeros_like(acc_ref)
        car_ref[...] = jnp.zeros_like(car_ref)
        lc_ref[...] = jnp.zeros_like(lc_ref)

        def tile(j, nb, diag):
            hs, bs = range(2), range(nb)
            rows = _kv_rows(j, nb)
            k, v = k_ref[rows, :], v_ref[rows, :]
            ell, kap, later = _sb_units(qh, k, strict2, causal, nb, diag)
            car = [[None] * nb for _ in hs]
            for hh in hs:
                run = car_ref[hh]
                for b in reversed(bs):
                    car[hh][b] = run
                    run = run + _row_sum(kap[hh][b])
                car_ref[hh] = run
            if not diag:
                kept = lc_ref[...]
                for hh in hs:
                    for b in bs:
                        kept = jnp.where(lane == j + b + HEAD_DIM * hh, car[hh][b], kept)
                lc_ref[...] = kept
            a = [[jnp.exp(ell[hh][b] + later[hh][b] + car[hh][b]) for b in bs] for hh in hs]
            if diag:
                a = [[jnp.where(causal, a[hh][b], 0.0) for b in bs] for hh in hs]
            for hh in hs:
                acc_ref[hh] += _dot(_join(a[hh]), v, NN)

        _walk_kv(i, tile, reverse=True)
        o_ref[...] = jnp.where(masks[0], acc_ref[0], acc_ref[1])

    q_spec, k_spec, v_spec = _att_specs(3 * N_PAIRS, lp)
    blk = pl.BlockSpec((BLK, LANES), lambda p, i: (i, p))
    out = jax.ShapeDtypeStruct((lp, WIDTH), F32)
    return pl.pallas_call(
        body, name="sb_fwd",
        out_shape=(out, out),
        grid=(N_PAIRS, nq),
        in_specs=[q_spec, k_spec, v_spec],
        out_specs=(blk, blk),
        scratch_shapes=[pltpu.VMEM((2, BLK, LANES), F32), pltpu.VMEM((2, BLK, 1), F32)],
        compiler_params=_params(("parallel", "parallel")),
    )(qkv, qkv, qkv)


def _sb_bwd(qkv, lcar, d_o):
    lp = qkv.shape[0]
    nq = lp // BLK

    def body(q_ref, k_ref, v_ref, lc_ref, do_ref, dq_ref, dk_ref, dv_ref, acc_ref, cg_ref):
        i = pl.program_id(1)
        masks = _head_masks()
        qs = q_ref[...] * ATT_SCALE
        qh = [jnp.where(mk, qs, 0).astype(BF16) for mk in masks]
        dov = do_ref[...]
        doh = [jnp.where(mk, dov, 0.0).astype(BF16) for mk in masks]
        lcv = lc_ref[...]
        row = lax.broadcasted_iota(jnp.int32, (BLK, BLK), 0)
        col = lax.broadcasted_iota(jnp.int32, (BLK, BLK), 1)
        lane = lax.broadcasted_iota(jnp.int32, (BLK, LANES), 1)
        causal = col < row
        strict2 = _stacked((row > col).astype(BF16))
        before2 = _stacked((row < col).astype(BF16))

        @pl.when(i == 0)
        def _():
            dk_ref[...] = jnp.zeros_like(dk_ref)
            dv_ref[...] = jnp.zeros_like(dv_ref)

        acc_ref[...] = jnp.zeros_like(acc_ref)
        cg_ref[...] = jnp.zeros_like(cg_ref)

        def tile(j, nb, diag):
            hs, bs = range(2), range(nb)
            rows = _kv_rows(j, nb)
            k, v = k_ref[rows, :], v_ref[rows, :]
            car = [[_row_sum(jnp.where(lane == j + b + HEAD_DIM * hh, lcv, 0.0)) for b in bs] for hh in hs]
            da = [_dot(doh[hh], v, NT) for hh in hs]
            ell, _, later = _sb_units(qh, k, strict2, causal, nb, diag)
            a = [[jnp.exp(ell[hh][b] + later[hh][b] + car[hh][b]) for b in bs] for hh in hs]
            if diag:
                a = [[jnp.where(causal, a[hh][b], 0.0) for b in bs] for hh in hs]
            g = [[da[hh][:, b * BLK:(b + 1) * BLK] * a[hh][b] for b in bs] for hh in hs]
            cg = [[_cumsum_dot(g[hh][b], before2) for b in bs] for hh in hs]
            before = [[None] * nb for _ in hs]
            for hh in hs:
                run = cg_ref[hh]
                for b in bs:
                    before[hh][b] = run
                    run = run + _row_sum(g[hh][b])
                cg_ref[hh] = run
            dz = [[g[hh][b] - jnp.exp(ell[hh][b]) * (g[hh][b] + cg[hh][b] + before[hh][b]) for b in bs] for hh in hs]
            if diag:
                dz = [[jnp.where(causal, dz[hh][b], 0.0) for b in bs] for hh in hs]
            dzb = [_join(dz[hh]) for hh in hs]
            ab = [_join(a[hh]) for hh in hs]
            for hh in hs:
                acc_ref[hh] += _dot(dzb[hh], k, NN)
            dk_ref[rows, :] += _dot(dzb[0], qh[0], TN) + _dot(dzb[1], qh[1], TN)
            dv_ref[rows, :] += _dot(ab[0], doh[0], TN) + _dot(ab[1], doh[1], TN)

        _walk_kv(i, tile)
        dq_ref[...] = (jnp.where(masks[0], acc_ref[0], acc_ref[1]) * ATT_SCALE).astype(BF16)

    q_spec, k_spec, v_spec = _att_specs(3 * N_PAIRS, lp)
    blk = pl.BlockSpec((BLK, LANES), lambda p, i: (i, p))
    col_full = pl.BlockSpec((lp, LANES), lambda p, i: (0, p))
    return pl.pallas_call(
        body, name="sb_bwd",
        out_shape=(jax.ShapeDtypeStruct((lp, WIDTH), BF16), jax.ShapeDtypeStruct((lp, WIDTH), F32),
                   jax.ShapeDtypeStruct((lp, WIDTH), F32)),
        grid=(N_PAIRS, nq),
        in_specs=[q_spec, k_spec, v_spec, blk, blk],
        out_specs=(blk, col_full, col_full),
        scratch_shapes=[pltpu.VMEM((2, BLK, LANES), F32), pltpu.VMEM((2, BLK, 1), F32)],
        compiler_params=_params(("parallel", "arbitrary")),
    )(qkv, qkv, qkv, lcar, d_o)


def _local_step(x, target, meta, gains, w_in, b_forget, w_fox, w_sb, w_out, w_up, conv_w, conv_b, w_down,
                ffn_block=None):
    s, d = x.shape
    ffn_block = ffn_block or w_up.shape[1] // 2
    n_valid = N_META + s
    lp = -(-n_valid // BLK) * BLK
    pad = lp - n_valid
    nq = lp // BLK

    h0 = jnp.concatenate([meta, x, jnp.zeros((pad, d), F32)], axis=0)
    tgt = jnp.concatenate([jnp.zeros((N_META, d), F32), target, jnp.zeros((pad, d), F32)], axis=0)

    q_a, k_a, v_a, f_a, q_b, k_b, v_b, g_a, g_b = jnp.split(
        w_in, [512, 1024, 1536, 1544, 2056, 2568, 3080, 4104], axis=1)
    w_qkv = jnp.concatenate([q_a, k_a, v_a, q_b, k_b, v_b], axis=1)
    w_gf = jnp.concatenate([g_a, g_b, f_a, jnp.zeros((d, F_PAD - N_HEADS), BF16)], axis=1)
    b_pad = jnp.concatenate([b_forget.reshape(1, N_HEADS), jnp.zeros((1, LANES - N_HEADS), F32)], axis=1)
    g0, g1, g2, g3 = (gains[i:i + 1] for i in range(4))

    xn1 = _rmsnorm_fwd(h0, g0, "norm1_fwd")
    qkv = _mm(xn1, w_qkv, "nn", BF16, "proj_qkv")
    gf = _mm(xn1, w_gf, "nn", F32, "proj_gates")
    fpre = gf[:, 2 * d:2 * d + LANES]
    c = _forget_fwd(fpre, b_pad)[:, :N_HEADS]
    c_pieces = _bf16_pieces(c)
    one = jnp.ones((lp, N_HEADS), BF16)
    kaug = _aug_lanes(3 * [one] + [-x for x in c_pieces] + 3 * [one])
    o_a, lse = _fox_fwd(qkv, _aug_lanes(c_pieces + 3 * [one]), kaug)
    o_b, lcar = _sb_fwd(qkv)
    h1, ya, yb, gated, mixed = _mix_fwd(o_a, o_b, gf, h0, w_fox, w_sb, w_out, g1)
    xn3 = _rmsnorm_fwd(h1, g2, "norm3_fwd")
    up = _mm(xn3, w_up, "nn", F32, "ffn_up")
    act = _conv_gelu_fwd(up, conv_w, conv_b, ffn_block)
    ffn = _mm(act, w_down, "nn", F32, "ffn_down")
    dy, loss_acc = _out_loss(h1, ffn, g3, tgt, n_valid)
    loss = loss_acc[0, 0]

    d_ffn, dg3 = _rmsnorm_bwd(ffn, g3, dy, None, BF16, "norm4_bwd")
    d_act = _mm(d_ffn, w_down, "nt", F32, "ffn_down_dx")
    gw_down = _mm(act, d_ffn, "tn", BF16, "ffn_down_dw")
    d_up, g_conv_w, g_conv_b = _conv_gelu_bwd(up, d_act, conv_w, conv_b, ffn_block)
    d_xn3 = _mm(d_up, w_up, "nt", F32, "ffn_up_dx")
    gw_up = _mm(xn3, d_up, "tn", BF16, "ffn_up_dw")
    dh1, dg2 = _rmsnorm_bwd(h1, g2, d_xn3, dy, F32, "norm3_bwd")

    d_mixed, dg1 = _rmsnorm_bwd(mixed, g1, dh1, None, BF16, "norm2_bwd")
    d_gated = _mm(d_mixed, w_out, "nt", F32, "out_dx")
    gw_out = _mm(gated, d_mixed, "tn", BF16, "out_dw")
    d_ya, d_yb, d_ga, d_gb = _gate_bwd(d_gated, gf, ya, yb)
    d_oa = _mm(d_ya, w_fox, "nt", F32, "fox_o_dx")
    gw_fox = _mm(o_a, d_ya, "tn", BF16, "fox_o_dw")
    d_ob = _mm(d_yb, w_sb, "nt", F32, "sb_o_dx")
    gw_sb = _mm(o_b, d_yb, "tn", BF16, "sb_o_dw")
    neg_lse = [-x for x in _bf16_pieces(lse[:, ::HEAD_DIM])]
    neg_dsum = [-x for x in _bf16_pieces(_head_dots(d_oa, o_a)[:, :N_HEADS])]
    qaug = _aug_lanes(c_pieces + 3 * [one] + neg_lse)
    dq_a, dk_a, dv_a, dcrow, drow = _fox_bwd(qkv, qaug, kaug, _aug_lanes(neg_dsum), d_oa)
    dq_b, dk_b, dv_b = _sb_bwd(qkv, lcar, d_ob)
    dc = dcrow.transpose(0, 2, 1, 3).reshape(N_HEADS, lp).T + drow[:, ::HEAD_DIM]
    dc = jnp.concatenate([dc, jnp.zeros((lp, LANES - N_HEADS), F32)], axis=1)
    df, db = _forget_bwd(dc, fpre, b_pad)
    lane = jnp.arange(LANES) < N_HEADS
    df = jnp.where(lane[None, :], df, 0.0)
    d_proj = jnp.concatenate(
        [dq_a, dk_a.astype(BF16), dv_a.astype(BF16), dq_b, dk_b.astype(BF16), dv_b.astype(BF16),
         d_ga, d_gb, df.astype(BF16), jnp.zeros((lp, F_PAD - LANES), BF16)], axis=1)
    w_in_p = jnp.concatenate([w_qkv, w_gf], axis=1)
    d_xn1 = _mm(d_proj, w_in_p, "nt", F32, "proj_dx")
    gw_in_p = _mm(xn1, d_proj, "tn", BF16, "proj_dw")
    dh0, dg0 = _rmsnorm_bwd(h0, g0, d_xn1, dh1, F32, "norm1_bwd")

    qkv_parts = jnp.split(gw_in_p[:, :6 * WIDTH], 6, axis=1)
    gw_in = jnp.concatenate(
        qkv_parts[:3] + [gw_in_p[:, 6 * WIDTH + 2 * d:6 * WIDTH + 2 * d + N_HEADS]] + qkv_parts[3:]
        + [gw_in_p[:, 6 * WIDTH:6 * WIDTH + 2 * d]], axis=1)
    grads = {
        "meta_tokens": dh0[:N_META],
        "norm_gains": jnp.concatenate([dg0, dg1, dg2, dg3], axis=0),
        "w_in": gw_in,
        "b_forget": db[:, :N_HEADS],
        "w_o_fox": gw_fox,
        "w_o_sb": gw_sb,
        "w_out": gw_out,
        "w_up": gw_up,
        "conv_w": g_conv_w,
        "conv_b": g_conv_b,
        "w_down": gw_down,
    }
    return loss, dh0[N_META:n_valid], grads


MESH_IDS = pl.DeviceIdType.MESH


def _window(ref, kind, idx, rows, cols):
    if kind == "slots":
        return ref.at[idx]
    if kind == "gate_value":
        half = N_DEV // 2
        idx = jnp.where(idx < half, 2 * idx, 2 * (idx - half) + 1)
        kind = "cols"
    if kind == "cols":
        return ref.at[:, pl.ds(pl.multiple_of(idx * cols, cols & -cols), cols)]
    return ref.at[pl.ds(pl.multiple_of(idx * rows, rows & -rows), rows), :]


def _gathered_shape(shape, kind):
    rows, cols = shape
    return {"slots": (N_DEV, rows, cols), "cols": (rows, N_DEV * cols), "gate_value": (rows, N_DEV * cols),
            "rows": (N_DEV * rows, cols)}[kind]


def _all_gather(shards, kinds):
    n = len(shards)

    def body(*refs):
        ins, outs = refs[:n], refs[n:2 * n]
        send_sems, recv_sems, local_sems = refs[2 * n:]
        x, y, c = lax.axis_index("x"), lax.axis_index("y"), lax.axis_index("c")
        me, sibling = (x, y, c), (x, y, 1 - c)
        chips = [(1 - x, y), (x, 1 - y), (1 - x, 1 - y)]

        def part(t, px, py, pc):
            return _window(outs[t], kinds[t], 4 * px + 2 * py + pc, *shards[t].shape)

        def copy(k, t, blk, to, src=None):
            return pltpu.make_async_remote_copy(
                src_ref=part(t, *blk) if src is None else src, dst_ref=part(t, *blk),
                send_sem=send_sems.at[k, t], recv_sem=recv_sems.at[k, t],
                device_id=to, device_id_type=MESH_IDS)

        mine = [pltpu.make_async_copy(ins[t], part(t, *me), local_sems.at[t]) for t in range(n)]
        for cp in mine:
            cp.start()
        first = [copy(0, t, me, sibling, src=ins[t]) for t in range(n)]
        first += [copy(1 + j, t, me, (*chip, c), src=ins[t]) for j, chip in enumerate(chips) for t in range(n)]
        for cp in first:
            cp.start()
        passed = []
        for j, chip in enumerate(chips):
            for t in range(n):
                copy(1 + j, t, (*chip, c), me).wait_recv()
                passed.append(copy(4 + j, t, (*chip, c), sibling))
                passed[-1].start()
        for t in range(n):
            copy(0, t, sibling, me).wait_recv()
        for j, chip in enumerate(chips):
            for t in range(n):
                copy(4 + j, t, (*chip, 1 - c), me).wait_recv()
        for cp in first + passed:
            cp.wait_send()
        for cp in mine:
            cp.wait()

    any_space = pl.BlockSpec(memory_space=pl.ANY)
    return pl.pallas_call(
        body, name="all_gather",
        out_shape=tuple(jax.ShapeDtypeStruct(_gathered_shape(a.shape, k), a.dtype) for a, k in zip(shards, kinds)),
        in_specs=[any_space] * n,
        out_specs=tuple([any_space] * n),
        scratch_shapes=[pltpu.SemaphoreType.DMA((7, n)), pltpu.SemaphoreType.DMA((7, n)),
                        pltpu.SemaphoreType.DMA((n,))],
    )(*shards)


def _exchange(grads, kinds, shard_shapes):
    n = len(grads)

    def body(*refs):
        ins, outs = refs[:n], refs[n:2 * n]
        send_sems, recv_sems, local_sems = refs[2 * n:]
        x, y, c = lax.axis_index("x"), lax.axis_index("y"), lax.axis_index("c")
        my = 4 * x + 2 * y + c

        def part(t, idx):
            return ins[t] if kinds[t] == "all" else _window(ins[t], kinds[t], idx, *shard_shapes[t])

        mine = [pltpu.make_async_copy(part(t, my), outs[t].at[my], local_sems.at[t]) for t in range(n)]
        for cp in mine:
            cp.start()
        copies = []
        for rel in range(1, N_DEV):
            px, py, pc = x ^ (rel >> 2), y ^ ((rel >> 1) & 1), c ^ (rel & 1)
            peer = 4 * px + 2 * py + pc
            for t in range(n):
                cp = pltpu.make_async_remote_copy(
                    src_ref=part(t, peer), dst_ref=outs[t].at[my],
                    send_sem=send_sems.at[rel - 1, t], recv_sem=recv_sems.at[rel - 1, t],
                    device_id=(px, py, pc), device_id_type=MESH_IDS)
                cp.start()
                arrival = pltpu.make_async_remote_copy(
                    src_ref=part(t, my), dst_ref=outs[t].at[peer],
                    send_sem=send_sems.at[rel - 1, t], recv_sem=recv_sems.at[rel - 1, t],
                    device_id=(px, py, pc), device_id_type=MESH_IDS)
                copies.append((cp, arrival))
        for _, arrival in copies:
            arrival.wait_recv()
        for cp, _ in copies:
            cp.wait_send()
        for cp in mine:
            cp.wait()

    any_space = pl.BlockSpec(memory_space=pl.ANY)
    return pl.pallas_call(
        body, name="grad_exchange",
        out_shape=tuple(jax.ShapeDtypeStruct((N_DEV,) + tuple(s), g.dtype) for g, s in zip(grads, shard_shapes)),
        in_specs=[any_space] * n,
        out_specs=tuple([any_space] * n),
        scratch_shapes=[pltpu.SemaphoreType.DMA((7, n)), pltpu.SemaphoreType.DMA((7, n)),
                        pltpu.SemaphoreType.DMA((n,))],
    )(*grads)


def _sum_adamw(parts, w, m, v, name):
    rows, cols = w.shape
    n, rows_p, cols_p = parts.shape
    tr = _tile(rows, BLK, SUBLANES) if rows > BLK else rows
    tp = tr if rows_p == rows else rows_p
    c1 = 1.0 - ADAM_B1 ** ADAM_STEP
    c2 = 1.0 - ADAM_B2 ** ADAM_STEP

    def body(p_ref, w_ref, m_ref, v_ref, g_ref, d_ref, nm_ref, nv_ref):
        gv = p_ref[0, 0:tr, 0:cols].astype(F32)
        for s in range(1, n):
            gv = gv + p_ref[s, 0:tr, 0:cols].astype(F32)
        g_ref[...] = gv
        nm = ADAM_B1 * m_ref[...] + (1.0 - ADAM_B1) * gv
        nv = ADAM_B2 * v_ref[...] + (1.0 - ADAM_B2) * (gv * gv)
        m_hat = nm / c1
        v_hat = nv / c2
        d_ref[...] = -ADAM_LR * (m_hat / (jnp.sqrt(v_hat) + ADAM_EPS) + ADAM_WD * w_ref[...])
        nm_ref[...] = nm
        nv_ref[...] = nv

    spec = pl.BlockSpec((tr, cols), lambda i: (i, 0))
    out = jax.ShapeDtypeStruct((rows, cols), F32)
    return pl.pallas_call(
        body, name=name,
        out_shape=(out, out, out, out),
        grid=(rows // tr,),
        in_specs=[pl.BlockSpec((n, tp, cols_p), lambda i: (0, i, 0)), spec, spec, spec],
        out_specs=(spec, spec, spec, spec),
        compiler_params=_params(("parallel",)),
    )(parts, w, m, v)


def _pad2(a, rows, cols):
    return jnp.pad(a, ((0, rows - a.shape[0]), (0, cols - a.shape[1])))


WEIGHTS = ["meta_tokens", "norm_gains", "w_in", "b_forget", "w_o_fox", "w_o_sb", "w_out", "w_up", "conv_w",
           "conv_b", "w_down"]


def kernel(x, meta_tokens, norm_gains, w_in, b_forget, w_o_fox, w_o_sb, w_out, w_up, conv_w, conv_b, w_down, loss_target, m_meta_tokens, m_norm_gains, m_w_in, m_b_forget, m_w_o_fox, m_w_o_sb, m_w_out, m_w_up, m_conv_w, m_conv_b, m_w_down, v_meta_tokens, v_norm_gains, v_w_in, v_b_forget, v_w_o_fox, v_w_o_sb, v_w_out, v_w_up, v_conv_w, v_conv_b, v_w_down):
    w = dict(meta_tokens=meta_tokens, norm_gains=norm_gains, w_in=w_in, b_forget=b_forget, w_o_fox=w_o_fox,
             w_o_sb=w_o_sb, w_out=w_out, w_up=w_up, conv_w=conv_w, conv_b=conv_b, w_down=w_down)
    mom = dict(meta_tokens=m_meta_tokens, norm_gains=m_norm_gains, w_in=m_w_in, b_forget=m_b_forget,
               w_o_fox=m_w_o_fox, w_o_sb=m_w_o_sb, w_out=m_w_out, w_up=m_w_up, conv_w=m_conv_w, conv_b=m_conv_b,
               w_down=m_w_down)
    vel = dict(meta_tokens=v_meta_tokens, norm_gains=v_norm_gains, w_in=v_w_in, b_forget=v_b_forget,
               w_o_fox=v_w_o_fox, w_o_sb=v_w_o_sb, w_out=v_w_out, w_up=v_w_up, conv_w=v_conv_w, conv_b=v_conv_b,
               w_down=v_w_down)
    w2 = {n: a.reshape(a.shape[-2:]) for n, a in w.items()}
    shard_shape = {n: a.shape for n, a in w2.items()}

    d = x.shape[-1]
    up_cols = shard_shape["w_up"][1]
    up_pad = -(-up_cols // LANES) * LANES
    half = N_DEV // 2
    pad_rows = lambda a: _pad2(a, SUBLANES, a.shape[1])

    shards = [
        ("w_in", "slots", w2["w_in"].astype(BF16)),
        ("w_up", "gate_value", _pad2(w2["w_up"], d, up_pad).astype(BF16)),
        ("w_o_fox", "cols", w2["w_o_fox"].astype(BF16)),
        ("w_o_sb", "cols", w2["w_o_sb"].astype(BF16)),
        ("w_out", "rows", w2["w_out"].astype(BF16)),
        ("w_down", "rows", w2["w_down"].astype(BF16)),
        ("meta_tokens", "cols", w2["meta_tokens"]),
        ("norm_gains", "cols", pad_rows(w2["norm_gains"])),
        ("conv_w", "gate_value", _pad2(w2["conv_w"], SUBLANES, up_pad)),
    ]
    full = dict(zip([s[0] for s in shards], _all_gather([s[2] for s in shards], [s[1] for s in shards])))
    w_in_full = jnp.concatenate([full["w_in"][i] for i in range(N_DEV)], axis=1)
    w_down_p = jnp.pad(full["w_down"].reshape(half, up_cols, d), ((0, 0), (0, up_pad - up_cols), (0, 0)))
    conv_b_p = jnp.pad(w2["conv_b"].reshape(2, half, up_cols), ((0, 0), (0, 0), (0, up_pad - up_cols)))
    conv_b_p = conv_b_p.transpose(1, 0, 2)

    loss, grad_x, grads = _local_step(
        x[0], loss_target[0], full["meta_tokens"], full["norm_gains"][:4], w_in_full, w2["b_forget"],
        full["w_o_fox"], full["w_o_sb"], full["w_out"], full["w_up"], full["conv_w"][:3],
        conv_b_p.reshape(1, N_DEV * up_pad), w_down_p.reshape(half * up_pad, d), ffn_block=up_pad)
    loss = lax.psum(loss, ("x", "y", "c"))

    in_cols = shard_shape["w_in"][1]
    sends = {
        "meta_tokens": ("cols", grads["meta_tokens"], (N_META, LANES)),
        "norm_gains": ("cols", pad_rows(grads["norm_gains"]), (SUBLANES, LANES)),
        "w_in": ("slots", jnp.stack([grads["w_in"][:, i * in_cols:(i + 1) * in_cols] for i in range(N_DEV)]),
                 shard_shape["w_in"]),
        "b_forget": ("all", _pad2(grads["b_forget"], SUBLANES, LANES), (SUBLANES, LANES)),
        "w_o_fox": ("cols", grads["w_o_fox"], shard_shape["w_o_fox"]),
        "w_o_sb": ("cols", grads["w_o_sb"], shard_shape["w_o_sb"]),
        "w_out": ("rows", grads["w_out"], shard_shape["w_out"]),
        "w_up": ("gate_value", grads["w_up"], (d, up_pad)),
        "conv_w": ("gate_value", pad_rows(grads["conv_w"]), (SUBLANES, up_pad)),
        "conv_b": ("all", pad_rows(grads["conv_b"].reshape(half, 2, up_pad).transpose(1, 0, 2)[:, :, :up_cols]
                                   .reshape(1, -1)),
                   (SUBLANES, N_DEV * up_cols)),
        "w_down": ("rows", grads["w_down"].reshape(half, up_pad, d)[:, :up_cols].reshape(half * up_cols, d),
                   shard_shape["w_down"]),
    }
    parts = dict(zip(WEIGHTS, _exchange([sends[n][1] for n in WEIGHTS], [sends[n][0] for n in WEIGHTS],
                                        [sends[n][2] for n in WEIGHTS])))

    grad, delta, new_m, new_v = {}, {}, {}, {}
    for n in WEIGHTS:
        shape = w[n].shape
        outs = _sum_adamw(parts[n], w2[n], mom[n].reshape(shard_shape[n]), vel[n].reshape(shard_shape[n]),
                          "adamw_" + n)
        grad[n], delta[n], new_m[n], new_v[n] = (o.reshape(shape) for o in outs)

    return (loss, grad_x[None], *[grad[n] for n in WEIGHTS], *[delta[n] for n in WEIGHTS],
            *[new_m[n] for n in WEIGHTS], *[new_v[n] for n in WEIGHTS])
```

```python
import functools
import math

import jax
import jax.numpy as jnp
from jax import lax
from jax.experimental import pallas as pl
from jax.experimental.pallas import tpu as pltpu

F32 = jnp.float32
BF16 = jnp.bfloat16

N_DEV = 8
N_META = 16
HEAD_DIM = 64
N_HEADS = 8
WIDTH = N_HEADS * HEAD_DIM
N_PAIRS = N_HEADS // 2
LANES = 128
SUBLANES = 8
EPS = 1e-6
ATT_SCALE = HEAD_DIM ** -0.5
BLK = 256
F_PAD = 256
VMEM_LIMIT = 48 << 20

ADAM_LR = 0.001
ADAM_B1 = 0.9
ADAM_B2 = 0.999
ADAM_EPS = 1e-08
ADAM_WD = 0.01
ADAM_STEP = 10

GELU_C = math.sqrt(2.0 / math.pi)
GELU_A = 0.044715
EXP_IS_ZERO = -110.0
NOT_VISITED = -1e30


def _params(sem, vmem=VMEM_LIMIT):
    return pltpu.CompilerParams(dimension_semantics=sem, vmem_limit_bytes=vmem)


def _tile(dim, cap, align=LANES):
    t = (min(cap, dim) // align) * align
    while t >= align:
        if dim % t == 0:
            return t
        t -= align
    return dim


def _log_sigmoid_parts(z):
    lp = jnp.log1p(jnp.exp(-jnp.abs(z)))
    return jnp.minimum(z, 0.0) - lp, jnp.minimum(-z, 0.0) - lp


def _sigmoid(x):
    return 1.0 / (1.0 + jnp.exp(-x))


def _split_bf16(x):
    hi = x.astype(BF16)
    lo = (x - hi.astype(F32)).astype(BF16)
    return hi, lo


def _dot(a, b, dims):
    return lax.dot_general(a, b, (dims, ((), ())), preferred_element_type=F32)


NN = ((1,), (0,))
NT = ((1,), (1,))
TN = ((0,), (0,))


def _mm(a, b, mode, out_dtype, name):
    if mode == "nn":
        (m, kc), (_, n) = a.shape, b.shape
    elif mode == "nt":
        (m, kc), (n, _) = a.shape, b.shape
    else:
        (kc, m), (_, n) = a.shape, b.shape
    tm = _tile(m, 768)
    tn = _tile(n, 1024)
    tk = _tile(kc, 1536 if mode != "tn" else 768)
    nk = kc // tk
    dims = {"nn": NN, "nt": NT, "tn": TN}[mode]

    def body(a_ref, b_ref, o_ref, acc_ref):
        k = pl.program_id(2)

        @pl.when(k == 0)
        def _():
            acc_ref[...] = jnp.zeros_like(acc_ref)

        acc_ref[...] += _dot(a_ref[...].astype(BF16), b_ref[...].astype(BF16), dims)

        @pl.when(k == nk - 1)
        def _():
            o_ref[...] = acc_ref[...].astype(out_dtype)

    if mode == "tn":
        a_spec = pl.BlockSpec((tk, tm), lambda j, i, k: (k, i))
    else:
        a_spec = pl.BlockSpec((tm, tk), lambda j, i, k: (i, k))
    if mode == "nt":
        b_spec = pl.BlockSpec((tn, tk), lambda j, i, k: (j, k))
    else:
        b_spec = pl.BlockSpec((tk, tn), lambda j, i, k: (k, j))
    return pl.pallas_call(
        body, name=name,
        out_shape=jax.ShapeDtypeStruct((m, n), out_dtype),
        grid=(n // tn, m // tm, nk),
        in_specs=[a_spec, b_spec],
        out_specs=pl.BlockSpec((tm, tn), lambda j, i, k: (i, j)),
        scratch_shapes=[pltpu.VMEM((tm, tn), F32)],
        compiler_params=_params(("parallel", "parallel", "arbitrary")),
    )(a, b)


def _rmsnorm_fwd(x, g, name):
    lp, d = x.shape

    def body(x_ref, g_ref, o_ref):
        xv = x_ref[...]
        r = lax.rsqrt(jnp.mean(xv * xv, axis=-1, keepdims=True) + EPS)
        o_ref[...] = ((xv * r) * g_ref[...]).astype(BF16)

    return pl.pallas_call(
        body, name=name,
        out_shape=jax.ShapeDtypeStruct((lp, d), BF16),
        grid=(lp // BLK,),
        in_specs=[pl.BlockSpec((BLK, d), lambda i: (i, 0)), pl.BlockSpec((1, d), lambda i: (0, 0))],
        out_specs=pl.BlockSpec((BLK, d), lambda i: (i, 0)),
        compiler_params=_params(("parallel",)),
    )(x, g)


def _rmsnorm_bwd(x, g, dy, resid, out_dtype, name):
    lp, d = x.shape
    has_resid = resid is not None

    def body(*refs):
        if has_resid:
            x_ref, g_ref, dy_ref, r_ref, dx_ref, dg_ref = refs
        else:
            x_ref, g_ref, dy_ref, dx_ref, dg_ref = refs
        i = pl.program_id(0)
        xv = x_ref[...]
        dyv = dy_ref[...].astype(F32)
        r = lax.rsqrt(jnp.mean(xv * xv, axis=-1, keepdims=True) + EPS)
        xh = xv * r
        dyg = dyv * g_ref[...]
        dx = r * (dyg - xh * jnp.mean(dyg * xh, axis=-1, keepdims=True))
        if has_resid:
            dx = dx + r_ref[...]
        dx_ref[...] = dx.astype(out_dtype)

        @pl.when(i == 0)
        def _():
            dg_ref[...] = jnp.zeros_like(dg_ref)

        dg_ref[...] += jnp.sum(dyv * xh, axis=0, keepdims=True)

    row = pl.BlockSpec((BLK, d), lambda i: (i, 0))
    vec = pl.BlockSpec((1, d), lambda i: (0, 0))
    ins = [x, g, dy] + ([resid] if has_resid else [])
    in_specs = [row, vec, row] + ([row] if has_resid else [])
    return pl.pallas_call(
        body, name=name,
        out_shape=(jax.ShapeDtypeStruct((lp, d), out_dtype), jax.ShapeDtypeStruct((1, d), F32)),
        grid=(lp // BLK,),
        in_specs=in_specs,
        out_specs=(row, vec),
        compiler_params=_params(("arbitrary",)),
    )(*ins)


def _forget_fwd(fpre, b_pad):
    lp = fpre.shape[0]

    def body(f_ref, b_ref, c_ref, carry_ref):
        i = pl.program_id(0)

        @pl.when(i == 0)
        def _():
            carry_ref[...] = jnp.zeros_like(carry_ref)

        logf, _ = _log_sigmoid_parts(f_ref[...] + b_ref[...])
        row = lax.broadcasted_iota(jnp.int32, (BLK, BLK), 0)
        col = lax.broadcasted_iota(jnp.int32, (BLK, BLK), 1)
        tri = (col <= row).astype(BF16)
        p0 = logf.astype(BF16)
        r1 = logf - p0.astype(F32)
        p1 = r1.astype(BF16)
        p2 = (r1 - p1.astype(F32)).astype(BF16)
        c = _dot(tri, p0, NN) + _dot(tri, p1, NN) + _dot(tri, p2, NN) + carry_ref[0:1, :]
        c_ref[...] = c
        carry_ref[...] = jnp.broadcast_to(c[BLK - 1:BLK, :], carry_ref.shape)

    return pl.pallas_call(
        body, name="forget_fwd",
        out_shape=jax.ShapeDtypeStruct((lp, LANES), F32),
        grid=(lp // BLK,),
        in_specs=[pl.BlockSpec((BLK, LANES), lambda i: (i, 0)), pl.BlockSpec((1, LANES), lambda i: (0, 0))],
        out_specs=pl.BlockSpec((BLK, LANES), lambda i: (i, 0)),
        scratch_shapes=[pltpu.VMEM((SUBLANES, LANES), F32)],
        compiler_params=_params(("arbitrary",)),
    )(fpre, b_pad)


def _forget_bwd(dc, fpre, b_pad):
    lp = fpre.shape[0]
    nb = lp // BLK

    def body(dc_ref, f_ref, b_ref, df_ref, db_ref, carry_ref):
        i = pl.program_id(0)

        @pl.when(i == 0)
        def _():
            carry_ref[...] = jnp.zeros_like(carry_ref)
            db_ref[...] = jnp.zeros_like(db_ref)

        dcv = dc_ref[...]
        row = lax.broadcasted_iota(jnp.int32, (BLK, BLK), 0)
        col = lax.broadcasted_iota(jnp.int32, (BLK, BLK), 1)
        tri = (col >= row).astype(BF16)
        p0 = dcv.astype(BF16)
        r1 = dcv - p0.astype(F32)
        p1 = r1.astype(BF16)
        p2 = (r1 - p1.astype(F32)).astype(BF16)
        dlogf = _dot(tri, p0, NN) + _dot(tri, p1, NN) + _dot(tri, p2, NN) + carry_ref[0:1, :]
        carry_ref[...] = jnp.broadcast_to(dlogf[0:1, :], carry_ref.shape)
        _, ls_neg = _log_sigmoid_parts(f_ref[...] + b_ref[...])
        df = dlogf * jnp.exp(ls_neg)
        df_ref[...] = df
        db_ref[...] += jnp.sum(df, axis=0, keepdims=True)

    rev = pl.BlockSpec((BLK, LANES), lambda i: (nb - 1 - i, 0))
    vec = pl.BlockSpec((1, LANES), lambda i: (0, 0))
    return pl.pallas_call(
        body, name="forget_bwd",
        out_shape=(jax.ShapeDtypeStruct((lp, LANES), F32), jax.ShapeDtypeStruct((1, LANES), F32)),
        grid=(nb,),
        in_specs=[rev, rev, vec],
        out_specs=(rev, vec),
        scratch_shapes=[pltpu.VMEM((SUBLANES, LANES), F32)],
        compiler_params=_params(("arbitrary",)),
    )(dc, fpre, b_pad)


def _mix_fwd(o_a, o_b, gates, h0, w_fox, w_sb, w_out, g1):
    lp, d = h0.shape

    def body(oa_ref, ob_ref, ga_ref, gb_ref, h_ref, wf_ref, ws_ref, wo_ref, g_ref,
             h1_ref, ya_ref, yb_ref, gated_ref, mixed_ref):
        ya = _dot(oa_ref[...].astype(BF16), wf_ref[...], NN)
        yb = _dot(ob_ref[...].astype(BF16), ws_ref[...], NN)
        gated = _sigmoid(ga_ref[...]) * ya + _sigmoid(gb_ref[...]) * yb
        gb16 = gated.astype(BF16)
        mixed = _dot(gb16, wo_ref[...], NN)
        r = lax.rsqrt(jnp.mean(mixed * mixed, axis=-1, keepdims=True) + EPS)
        h1_ref[...] = h_ref[...] + (mixed * r) * g_ref[...]
        ya_ref[...] = ya
        yb_ref[...] = yb
        gated_ref[...] = gb16
        mixed_ref[...] = mixed

    row_w = pl.BlockSpec((BLK, WIDTH), lambda i: (i, 0))
    row_d = pl.BlockSpec((BLK, d), lambda i: (i, 0))
    full = lambda s: pl.BlockSpec(s, lambda i: (0, 0))
    return pl.pallas_call(
        body, name="mix_fwd",
        out_shape=(jax.ShapeDtypeStruct((lp, d), F32), jax.ShapeDtypeStruct((lp, d), F32),
                   jax.ShapeDtypeStruct((lp, d), F32), jax.ShapeDtypeStruct((lp, d), BF16),
                   jax.ShapeDtypeStruct((lp, d), F32)),
        grid=(lp // BLK,),
        in_specs=[row_w, row_w, row_d, pl.BlockSpec((BLK, d), lambda i: (i, 1)), row_d,
                  full((WIDTH, d)), full((WIDTH, d)), full((d, d)), full((1, d))],
        out_specs=(row_d, row_d, row_d, row_d, row_d),
        compiler_params=_params(("parallel",)),
    )(o_a, o_b, gates, gates, h0, w_fox, w_sb, w_out, g1)


def _gate_bwd(d_gated, gates, ya, yb):
    lp, d = d_gated.shape

    def body(dg_ref, ga_ref, gb_ref, ya_ref, yb_ref, dya_ref, dyb_ref, dga_ref, dgb_ref):
        dg = dg_ref[...]
        sa = _sigmoid(ga_ref[...])
        sb = _sigmoid(gb_ref[...])
        dya_ref[...] = (dg * sa).astype(BF16)
        dyb_ref[...] = (dg * sb).astype(BF16)
        dga_ref[...] = (dg * ya_ref[...] * (sa * (1.0 - sa))).astype(BF16)
        dgb_ref[...] = (dg * yb_ref[...] * (sb * (1.0 - sb))).astype(BF16)

    row = pl.BlockSpec((BLK, d), lambda i: (i, 0))
    out = jax.ShapeDtypeStruct((lp, d), BF16)
    return pl.pallas_call(
        body, name="gate_bwd",
        out_shape=(out, out, out, out),
        grid=(lp // BLK,),
        in_specs=[row, row, pl.BlockSpec((BLK, d), lambda i: (i, 1)), row, row],
        out_specs=(row, row, row, row),
        compiler_params=_params(("parallel",)),
    )(d_gated, gates, gates, ya, yb)


def _shift_down(cur, prev, n):
    rolled = pltpu.roll(cur, n, 0)
    row = lax.broadcasted_iota(jnp.int32, cur.shape, 0)
    for t in range(n):
        rolled = jnp.where(row == t, prev[SUBLANES - n + t:SUBLANES - n + t + 1, :], rolled)
    return rolled


def _shift_up(cur, nxt, n):
    rows = cur.shape[0]
    rolled = pltpu.roll(cur, rows - n, 0)
    row = lax.broadcasted_iota(jnp.int32, cur.shape, 0)
    for t in range(n):
        rolled = jnp.where(row == rows - n + t, nxt[t:t + 1, :], rolled)
    return rolled


def _gelu(x):
    return 0.5 * x * (1.0 + jnp.tanh(GELU_C * (x + GELU_A * (x * x * x))))


def _gelu_and_grad(x):
    t = jnp.tanh(GELU_C * (x + GELU_A * (x * x * x)))
    half = 0.5 * (1.0 + t)
    return x * half, half + 0.5 * x * (1.0 - t * t) * (GELU_C * (1.0 + 3.0 * GELU_A * (x * x)))


def _conv_taps(cur, prev, w_ref, b_ref):
    s1 = _shift_down(cur, prev, 1)
    s2 = _shift_down(cur, prev, 2)
    u = b_ref[...] + w_ref[0:1, :] * s2
    u = u + w_ref[1:2, :] * s1
    u = u + w_ref[2:3, :] * cur
    return u, s1, s2


def _conv_gelu_fwd(up, conv_w, conv_b, tc):
    lp, f2 = up.shape
    rb = BLK // SUBLANES

    def body(u_ref, p_ref, w_ref, b_ref, act_ref):
        i = pl.program_id(0)
        keep = (i > 0).astype(F32)
        u, _, _ = _conv_taps(u_ref[...], p_ref[...] * keep, w_ref, b_ref)
        act_ref[...] = (_gelu(u[:, :tc]) * u[:, tc:]).astype(BF16)

    prev_row = lambda i: jnp.maximum(i * rb - 1, 0)
    return pl.pallas_call(
        body, name="conv_gelu_fwd",
        out_shape=jax.ShapeDtypeStruct((lp, f2 // 2), BF16),
        grid=(lp // BLK, f2 // (2 * tc)),
        in_specs=[pl.BlockSpec((BLK, 2 * tc), lambda i, j: (i, j)),
                  pl.BlockSpec((SUBLANES, 2 * tc), lambda i, j: (prev_row(i), j)),
                  pl.BlockSpec((3, 2 * tc), lambda i, j: (0, j)),
                  pl.BlockSpec((1, 2 * tc), lambda i, j: (0, j))],
        out_specs=pl.BlockSpec((BLK, tc), lambda i, j: (i, j)),
        compiler_params=_params(("parallel", "parallel")),
    )(up, up, conv_w, conv_b)


def _conv_gelu_bwd(up, d_act, conv_w, conv_b, tc):
    lp, f2 = up.shape
    nb = lp // BLK
    rb = BLK // SUBLANES

    def du_of(u, da):
        gel, grad = _gelu_and_grad(u[:, :tc])
        return jnp.concatenate([da * u[:, tc:] * grad, da * gel], axis=1)

    def body(u_ref, p_ref, n_ref, da_ref, dan_ref, w_ref, b_ref, dup_ref, dcw_ref, dcb_ref):
        i = pl.program_id(1)
        cur = u_ref[...]
        u, s1, s2 = _conv_taps(cur, p_ref[...] * (i > 0).astype(F32), w_ref, b_ref)
        du = du_of(u, da_ref[...])
        u_next, _, _ = _conv_taps(n_ref[...], cur[BLK - SUBLANES:BLK, :], w_ref, b_ref)
        du_next = du_of(u_next, dan_ref[...]) * (i < nb - 1).astype(F32)
        n1 = _shift_up(du, du_next, 1)
        n2 = _shift_up(du, du_next, 2)
        dup_ref[...] = (w_ref[2:3, :] * du + w_ref[1:2, :] * n1 + w_ref[0:1, :] * n2).astype(BF16)

        @pl.when(i == 0)
        def _():
            dcw_ref[...] = jnp.zeros_like(dcw_ref)
            dcb_ref[...] = jnp.zeros_like(dcb_ref)

        dcw_ref[0:1, :] += jnp.sum(du * s2, axis=0, keepdims=True)
        dcw_ref[1:2, :] += jnp.sum(du * s1, axis=0, keepdims=True)
        dcw_ref[2:3, :] += jnp.sum(du * cur, axis=0, keepdims=True)
        dcb_ref[...] += jnp.sum(du, axis=0, keepdims=True)

    prev_row = lambda i: jnp.maximum(i * rb - 1, 0)
    next_row = lambda i: jnp.minimum((i + 1) * rb, nb * rb - 1)
    return pl.pallas_call(
        body, name="conv_gelu_bwd",
        out_shape=(jax.ShapeDtypeStruct((lp, f2), BF16), jax.ShapeDtypeStruct((3, f2), F32),
                   jax.ShapeDtypeStruct((1, f2), F32)),
        grid=(f2 // (2 * tc), nb),
        in_specs=[pl.BlockSpec((BLK, 2 * tc), lambda j, i: (i, j)),
                  pl.BlockSpec((SUBLANES, 2 * tc), lambda j, i: (prev_row(i), j)),
                  pl.BlockSpec((SUBLANES, 2 * tc), lambda j, i: (next_row(i), j)),
                  pl.BlockSpec((BLK, tc), lambda j, i: (i, j)),
                  pl.BlockSpec((SUBLANES, tc), lambda j, i: (next_row(i), j)),
                  pl.BlockSpec((3, 2 * tc), lambda j, i: (0, j)),
                  pl.BlockSpec((1, 2 * tc), lambda j, i: (0, j))],
        out_specs=(pl.BlockSpec((BLK, 2 * tc), lambda j, i: (i, j)),
                   pl.BlockSpec((3, 2 * tc), lambda j, i: (0, j)),
                   pl.BlockSpec((1, 2 * tc), lambda j, i: (0, j))),
        compiler_params=_params(("parallel", "arbitrary")),
    )(up, up, up, d_act, d_act, conv_w, conv_b)


def _out_loss(h1, ffn, g3, target, n_valid):
    lp, d = h1.shape

    def body(h_ref, f_ref, g_ref, t_ref, dy_ref, loss_ref):
        i = pl.program_id(0)

        @pl.when(i == 0)
        def _():
            loss_ref[...] = jnp.zeros_like(loss_ref)

        fv = f_ref[...]
        r = lax.rsqrt(jnp.mean(fv * fv, axis=-1, keepdims=True) + EPS)
        y = h_ref[...] + (fv * r) * g_ref[...]
        row = i * BLK + lax.broadcasted_iota(jnp.int32, (BLK, 1), 0)
        valid = (row >= N_META) & (row < n_valid)
        diff = jnp.where(valid, y - t_ref[...], 0.0)
        dy_ref[...] = diff * (1.0 / d)
        per_row = jnp.mean(diff * diff, axis=-1, keepdims=True)
        loss_ref[...] += 0.5 * jnp.sum(per_row, axis=0, keepdims=True)

    row_d = pl.BlockSpec((BLK, d), lambda i: (i, 0))
    return pl.pallas_call(
        body, name="out_loss",
        out_shape=(jax.ShapeDtypeStruct((lp, d), F32), jax.ShapeDtypeStruct((SUBLANES, LANES), F32)),
        grid=(lp // BLK,),
        in_specs=[row_d, row_d, pl.BlockSpec((1, d), lambda i: (0, 0)), row_d],
        out_specs=(row_d, pl.BlockSpec((SUBLANES, LANES), lambda i: (0, 0))),
        compiler_params=_params(("arbitrary",)),
    )(h1, ffn, g3, target)


def _head_masks(rows=BLK):
    lane = lax.broadcasted_iota(jnp.int32, (rows, LANES), 1)
    return [lane < HEAD_DIM, lane >= HEAD_DIM]


def _att_specs(base, lp):
    q_spec = pl.BlockSpec((BLK, LANES), lambda p, i: (i, base + p))
    k_spec = pl.BlockSpec((lp, LANES), lambda p, i: (0, base + N_PAIRS + p))
    v_spec = pl.BlockSpec((lp, LANES), lambda p, i: (0, base + 2 * N_PAIRS + p))
    return q_spec, k_spec, v_spec


def _kv_rows(j, nb=1):
    return pl.ds(pl.multiple_of(j * BLK, nb * BLK), nb * BLK)


def _walk_kv(i, tile, reverse=False, first_pair=0, more=None):
    pairs, odd = i // 2, i % 2

    def wide(t, carry):
        tile(2 * (pairs - 1 - t) if reverse else 2 * t, 2, False)
        return carry

    def single():
        @pl.when(odd == 1)
        def _():
            tile(i - 1, 1, False)

    if reverse:
        tile(i, 1, True)
        single()
        if more is None:
            lax.fori_loop(0, pairs, wide, 0)
        else:
            lax.while_loop(lambda c: (c[0] < pairs) & c[1], lambda c: (wide(c[0], c[0]) + 1, more()), (0, more()))
    else:
        lax.fori_loop(first_pair, pairs, wide, 0)
        single()
        tile(i, 1, True)


def _bf16_pieces(x):
    rnd = lambda a: lax.reduce_precision(a, exponent_bits=8, mantissa_bits=7)
    p0 = rnd(x)
    p1 = rnd(x - p0)
    p2 = rnd(x - p0 - p1)
    return [p0.astype(BF16), p1.astype(BF16), p2.astype(BF16)]


def _aug_lanes(cols):
    lp = cols[0].shape[0]
    vals = jnp.stack([c.astype(BF16) for c in cols], axis=-1)
    vals = vals.reshape(lp, N_PAIRS, 2, len(cols))[:, :, ::-1, :]
    vals = jnp.pad(vals, ((0, 0), (0, 0), (0, 0), (0, HEAD_DIM - len(cols))))
    return vals.reshape(lp, WIDTH)


N_AUG = 3


def _fox_fwd(qkv, qaug, kaug):
    lp = qkv.shape[0]
    nq = lp // BLK

    def body(q_ref, k_ref, v_ref, qa_ref, ka_ref, o_ref, lse_ref, acc_ref, m_ref):
        i = pl.program_id(1)
        hs = range(2)
        masks = _head_masks()
        qs = q_ref[...] * ATT_SCALE
        qa = qa_ref[...]
        qh = [jnp.where(masks[hh], qs, qa) for hh in hs]
        acc_ref[...] = jnp.zeros_like(acc_ref)
        m_ref[...] = jnp.full_like(m_ref, -1e30)
        row = lax.broadcasted_iota(jnp.int32, (BLK, BLK), 0)
        col = lax.broadcasted_iota(jnp.int32, (BLK, BLK), 1)
        causal = col <= row

        def tile(j, nb, diag):
            rows = _kv_rows(j, nb)
            kmasks = _head_masks(nb * BLK)
            k, ka, v = k_ref[rows, :], ka_ref[rows, :], v_ref[rows, :]
            kh = [jnp.where(kmasks[hh], k, ka) for hh in hs]
            vh = [jnp.where(kmasks[hh], v, jnp.ones_like(v)) for hh in hs]
            s = [_dot(qh[hh], kh[hh], NT) for hh in hs]
            if diag:
                s = [jnp.where(causal, s[hh], -1e30) for hh in hs]
            m_prev = [m_ref[hh] for hh in hs]
            m_new = [jnp.maximum(m_prev[hh], jnp.max(s[hh], axis=-1, keepdims=True)) for hh in hs]
            p = [jnp.exp(s[hh] - m_new[hh]).astype(BF16) for hh in hs]
            for hh in hs:
                acc_ref[hh] = jnp.exp(m_prev[hh] - m_new[hh]) * acc_ref[hh] + _dot(p[hh], vh[hh], NN)
                m_ref[hh] = m_new[hh]

        _walk_kv(i, tile)
        acc = [acc_ref[hh] for hh in hs]
        denom = [acc[0][:, HEAD_DIM:HEAD_DIM + 1], acc[1][:, 0:1]]
        o_ref[...] = jnp.where(masks[0], acc[0] / denom[0], acc[1] / denom[1])
        lse_ref[...] = jnp.where(masks[0], m_ref[0] + jnp.log(denom[0]), m_ref[1] + jnp.log(denom[1]))

    q_spec, k_spec, v_spec = _att_specs(0, lp)
    blk = pl.BlockSpec((BLK, LANES), lambda p, i: (i, p))
    col_full = pl.BlockSpec((lp, LANES), lambda p, i: (0, p))
    out = jax.ShapeDtypeStruct((lp, WIDTH), F32)
    return pl.pallas_call(
        body, name="fox_fwd",
        out_shape=(out, out),
        grid=(N_PAIRS, nq),
        in_specs=[q_spec, k_spec, v_spec, blk, col_full],
        out_specs=(blk, blk),
        scratch_shapes=[pltpu.VMEM((2, BLK, LANES), F32), pltpu.VMEM((2, BLK, 1), F32)],
        compiler_params=_params(("parallel", "parallel")),
    )(qkv, qkv, qkv, qaug, kaug)


def _head_dots(a, b):
    lp = a.shape[0]

    def body(a_ref, b_ref, o_ref):
        lane = lax.broadcasted_iota(jnp.int32, (BLK, LANES), 1)
        out = jnp.zeros((BLK, LANES), F32)
        for p in range(N_PAIRS):
            cols = slice(p * LANES, (p + 1) * LANES)
            prod = a_ref[:, cols] * b_ref[:, cols]
            for hh in range(2):
                part = jnp.where((lane >= HEAD_DIM) == (hh == 1), prod, 0.0)
                out = jnp.where(lane == 2 * p + hh, jnp.sum(part, axis=-1, keepdims=True), out)
        o_ref[...] = out

    row = pl.BlockSpec((BLK, WIDTH), lambda i: (i, 0))
    return pl.pallas_call(
        body, name="head_dots",
        out_shape=jax.ShapeDtypeStruct((lp, LANES), F32),
        grid=(lp // BLK,),
        in_specs=[row, row],
        out_specs=pl.BlockSpec((BLK, LANES), lambda i: (i, 0)),
        compiler_params=_params(("parallel",)),
    )(a, b)


def _fox_bwd(qkv, qaug, kaug, doaug, d_o):
    lp = qkv.shape[0]
    nq = lp // BLK

    def body(q_ref, k_ref, v_ref, qa_ref, ka_ref, da_ref, do_ref,
             dq_ref, dk_ref, dv_ref, dc_ref, dr_ref, acc_ref, rs_ref):
        i = pl.program_id(1)
        hs = range(2)
        masks = _head_masks()
        qs = q_ref[...] * ATT_SCALE
        qa = qa_ref[...]
        qm = [jnp.where(masks[hh], qs, 0) for hh in hs]
        qh = [jnp.where(masks[hh], qs, qa) for hh in hs]
        dov = do_ref[...].astype(BF16)
        doa = da_ref[...]
        dom = [jnp.where(masks[hh], dov, 0) for hh in hs]
        doh = [jnp.where(masks[hh], dov, doa) for hh in hs]
        row = lax.broadcasted_iota(jnp.int32, (BLK, BLK), 0)
        col = lax.broadcasted_iota(jnp.int32, (BLK, BLK), 1)
        causal = col <= row

        @pl.when(i == 0)
        def _():
            dk_ref[...] = jnp.zeros_like(dk_ref)
            dv_ref[...] = jnp.zeros_like(dv_ref)
            dc_ref[...] = jnp.zeros_like(dc_ref)

        acc_ref[...] = jnp.zeros_like(acc_ref)
        rs_ref[...] = jnp.zeros_like(rs_ref)

        def tile(j, nb, diag):
            rows = _kv_rows(j, nb)
            kmasks = _head_masks(nb * BLK)
            k, ka, v = k_ref[rows, :], ka_ref[rows, :], v_ref[rows, :]
            ones = (lax.broadcasted_iota(jnp.int32, v.shape, 1) % HEAD_DIM < N_AUG).astype(BF16)
            kh = [jnp.where(kmasks[hh], k, ka) for hh in hs]
            vh = [jnp.where(kmasks[hh], v, ones) for hh in hs]
            logp = [_dot(qh[hh], kh[hh], NT) for hh in hs]
            dp = [_dot(doh[hh], vh[hh], NT) for hh in hs]
            p = [jnp.exp(logp[hh]) for hh in hs]
            if diag:
                p = [jnp.where(causal, p[hh], 0.0) for hh in hs]
            ds = [p[hh] * dp[hh] for hh in hs]
            dsb = [ds[hh].astype(BF16) for hh in hs]
            for hh in hs:
                acc_ref[hh] += _dot(dsb[hh], k, NN)
                col_sums = jnp.sum(ds[hh], axis=0, keepdims=True)
                for b in range(nb):
                    dc_ref[0, j + b, hh:hh + 1, :] -= col_sums[:, b * BLK:(b + 1) * BLK]
                rs_ref[hh] += jnp.sum(ds[hh], axis=-1, keepdims=True)
            dk_ref[rows, :] += _dot(dsb[0], qm[0], TN) + _dot(dsb[1], qm[1], TN)
            dv_ref[rows, :] += _dot(p[0].astype(BF16), dom[0], TN) + _dot(p[1].astype(BF16), dom[1], TN)

        _walk_kv(i, tile)
        dq_ref[...] = (jnp.where(masks[0], acc_ref[0], acc_ref[1]) * ATT_SCALE).astype(BF16)
        dr_ref[...] = jnp.where(masks[0], rs_ref[0], rs_ref[1])

    q_spec, k_spec, v_spec = _att_specs(0, lp)
    blk = pl.BlockSpec((BLK, LANES), lambda p, i: (i, p))
    col_full = pl.BlockSpec((lp, LANES), lambda p, i: (0, p))
    crow_spec = pl.BlockSpec((1, nq, 2, BLK), lambda p, i: (p, 0, 0, 0))
    return pl.pallas_call(
        body, name="fox_bwd",
        out_shape=(jax.ShapeDtypeStruct((lp, WIDTH), BF16), jax.ShapeDtypeStruct((lp, WIDTH), F32),
                   jax.ShapeDtypeStruct((lp, WIDTH), F32), jax.ShapeDtypeStruct((N_PAIRS, nq, 2, BLK), F32),
                   jax.ShapeDtypeStruct((lp, WIDTH), F32)),
        grid=(N_PAIRS, nq),
        in_specs=[q_spec, k_spec, v_spec, blk, col_full, blk, blk],
        out_specs=(blk, col_full, col_full, crow_spec, blk),
        scratch_shapes=[pltpu.VMEM((2, BLK, LANES), F32), pltpu.VMEM((2, BLK, 1), F32)],
        compiler_params=_params(("parallel", "arbitrary")),
    )(qkv, qkv, qkv, qaug, kaug, doaug, d_o)


def _sb_scores(z):
    ell = jnp.minimum(z, 0.0) - jnp.log(1.0 + jnp.exp(-jnp.abs(z)))
    return ell, ell - z


def _stacked(tri):
    return jnp.concatenate([tri, tri], axis=0)


def _cumsum_dot(x, tri2):
    hi, lo = _split_bf16(x)
    return _dot(jnp.concatenate([hi, lo], axis=1), tri2, NN)


def _sb_units(qh, k, strict2, causal, nb, diag):
    hs, bs = range(2), range(nb)
    z = [_dot(qh[hh], k, NT) for hh in hs]
    sc = [[_sb_scores(z[hh][:, b * BLK:(b + 1) * BLK]) for b in bs] for hh in hs]
    ell = [[sc[hh][b][0] for b in bs] for hh in hs]
    kap = [[jnp.where(causal, sc[hh][b][1], 0.0) if diag else sc[hh][b][1] for b in bs] for hh in hs]
    later = [[_cumsum_dot(kap[hh][b], strict2) for b in bs] for hh in hs]
    return ell, kap, later


def _row_sum(x):
    return jnp.sum(x, axis=-1, keepdims=True)


def _join(blocks):
    joined = blocks[0] if len(blocks) == 1 else jnp.concatenate(blocks, axis=1)
    return joined.astype(BF16)


def _sb_fwd(qkv):
    lp = qkv.shape[0]
    nq = lp // BLK
    assert nq <= HEAD_DIM

    def body(q_ref, k_ref, v_ref, o_ref, lc_ref, acc_ref, car_ref):
        i = pl.program_id(1)
        masks = _head_masks()
        qs = q_ref[...] * ATT_SCALE
        qh = [jnp.where(mk, qs, 0).astype(BF16) for mk in masks]
        row = lax.broadcasted_iota(jnp.int32, (BLK, BLK), 0)
        col = lax.broadcasted_iota(jnp.int32, (BLK, BLK), 1)
        lane = lax.broadcasted_iota(jnp.int32, (BLK, LANES), 1)
        causal = col < row
        strict2 = _stacked((row > col).astype(BF16))
        acc_ref[...] = jnp.zeros_like(acc_ref)
        car_ref[...] = jnp.zeros_like(car_ref)
        lc_ref[...] = jnp.full_like(lc_ref, NOT_VISITED)

        def tile(j, nb, diag):
            hs, bs = range(2), range(nb)
            rows = _kv_rows(j, nb)
            k, v = k_ref[rows, :], v_ref[rows, :]
            ell, kap, later = _sb_units(qh, k, strict2, causal, nb, diag)
            car = [[None] * nb for _ in hs]
            for hh in hs:
                run = car_ref[hh]
                for b in reversed(bs):
                    car[hh][b] = run
                    run = run + _row_sum(kap[hh][b])
                car_ref[hh] = run
            if not diag:
                kept = lc_ref[...]
                for hh in hs:
                    for b in bs:
                        kept = jnp.where(lane == j + b + HEAD_DIM * hh, car[hh][b], kept)
                lc_ref[...] = kept
            a = [[jnp.exp(ell[hh][b] + later[hh][b] + car[hh][b]) for b in bs] for hh in hs]
            if diag:
                a = [[jnp.where(causal, a[hh][b], 0.0) for b in bs] for hh in hs]
            for hh in hs:
                acc_ref[hh] += _dot(_join(a[hh]), v, NN)

        _walk_kv(i, tile, reverse=True, more=lambda: jnp.max(car_ref[...]) > EXP_IS_ZERO)
        o_ref[...] = jnp.where(masks[0], acc_ref[0], acc_ref[1])

    q_spec, k_spec, v_spec = _att_specs(3 * N_PAIRS, lp)
    blk = pl.BlockSpec((BLK, LANES), lambda p, i: (i, p))
    out = jax.ShapeDtypeStruct((lp, WIDTH), F32)
    return pl.pallas_call(
        body, name="sb_fwd",
        out_shape=(out, out),
        grid=(N_PAIRS, nq),
        in_specs=[q_spec, k_spec, v_spec],
        out_specs=(blk, blk),
        scratch_shapes=[pltpu.VMEM((2, BLK, LANES), F32), pltpu.VMEM((2, BLK, 1), F32)],
        compiler_params=_params(("parallel", "parallel")),
    )(qkv, qkv, qkv)


def _sb_bwd(qkv, lcar, d_o):
    lp = qkv.shape[0]
    nq = lp // BLK

    def body(q_ref, k_ref, v_ref, lc_ref, do_ref, dq_ref, dk_ref, dv_ref, acc_ref, cg_ref):
        i = pl.program_id(1)
        masks = _head_masks()
        qs = q_ref[...] * ATT_SCALE
        qh = [jnp.where(mk, qs, 0).astype(BF16) for mk in masks]
        dov = do_ref[...]
        doh = [jnp.where(mk, dov, 0.0).astype(BF16) for mk in masks]
        lcv = lc_ref[...]
        lane_row = lax.broadcasted_iota(jnp.int32, (1, LANES), 1)
        alive = jnp.where(jnp.max(lcv, axis=0, keepdims=True) > EXP_IS_ZERO, 1.0, 0.0)
        n_alive = jnp.maximum(jnp.sum(jnp.where(lane_row < HEAD_DIM, alive, 0.0)),
                              jnp.sum(jnp.where(lane_row >= HEAD_DIM, alive, 0.0))).astype(jnp.int32)
        first_pair = jnp.maximum(i - n_alive, 0) // 2
        row = lax.broadcasted_iota(jnp.int32, (BLK, BLK), 0)
        col = lax.broadcasted_iota(jnp.int32, (BLK, BLK), 1)
        lane = lax.broadcasted_iota(jnp.int32, (BLK, LANES), 1)
        causal = col < row
        strict2 = _stacked((row > col).astype(BF16))
        before2 = _stacked((row < col).astype(BF16))

        @pl.when(i == 0)
        def _():
            dk_ref[...] = jnp.zeros_like(dk_ref)
            dv_ref[...] = jnp.zeros_like(dv_ref)

        acc_ref[...] = jnp.zeros_like(acc_ref)
        cg_ref[...] = jnp.zeros_like(cg_ref)

        def tile(j, nb, diag):
            hs, bs = range(2), range(nb)
            rows = _kv_rows(j, nb)
            k, v = k_ref[rows, :], v_ref[rows, :]
            if diag:
                car = [[0.0] for _ in hs]
            else:
                car = [[_row_sum(jnp.where(lane == j + b + HEAD_DIM * hh, lcv, 0.0)) for b in bs] for hh in hs]
            da = [_dot(doh[hh], v, NT) for hh in hs]
            ell, _, later = _sb_units(qh, k, strict2, causal, nb, diag)
            a = [[jnp.exp(ell[hh][b] + later[hh][b] + car[hh][b]) for b in bs] for hh in hs]
            if diag:
                a = [[jnp.where(causal, a[hh][b], 0.0) for b in bs] for hh in hs]
            g = [[da[hh][:, b * BLK:(b + 1) * BLK] * a[hh][b] for b in bs] for hh in hs]
            cg = [[_cumsum_dot(g[hh][b], before2) for b in bs] for hh in hs]
            before = [[None] * nb for _ in hs]
            for hh in hs:
                run = cg_ref[hh]
                for b in bs:
                    before[hh][b] = run
                    run = run + _row_sum(g[hh][b])
                cg_ref[hh] = run
            dz = [[g[hh][b] - jnp.exp(ell[hh][b]) * (g[hh][b] + cg[hh][b] + before[hh][b]) for b in bs] for hh in hs]
            if diag:
                dz = [[jnp.where(causal, dz[hh][b], 0.0) for b in bs] for hh in hs]
            dzb = [_join(dz[hh]) for hh in hs]
            ab = [_join(a[hh]) for hh in hs]
            for hh in hs:
                acc_ref[hh] += _dot(dzb[hh], k, NN)
            dk_ref[rows, :] += _dot(dzb[0], qh[0], TN) + _dot(dzb[1], qh[1], TN)
            dv_ref[rows, :] += _dot(ab[0], doh[0], TN) + _dot(ab[1], doh[1], TN)

        _walk_kv(i, tile, first_pair=first_pair)
        dq_ref[...] = (jnp.where(masks[0], acc_ref[0], acc_ref[1]) * ATT_SCALE).astype(BF16)

    q_spec, k_spec, v_spec = _att_specs(3 * N_PAIRS, lp)
    blk = pl.BlockSpec((BLK, LANES), lambda p, i: (i, p))
    col_full = pl.BlockSpec((lp, LANES), lambda p, i: (0, p))
    return pl.pallas_call(
        body, name="sb_bwd",
        out_shape=(jax.ShapeDtypeStruct((lp, WIDTH), BF16), jax.ShapeDtypeStruct((lp, WIDTH), F32),
                   jax.ShapeDtypeStruct((lp, WIDTH), F32)),
        grid=(N_PAIRS, nq),
        in_specs=[q_spec, k_spec, v_spec, blk, blk],
        out_specs=(blk, col_full, col_full),
        scratch_shapes=[pltpu.VMEM((2, BLK, LANES), F32), pltpu.VMEM((2, BLK, 1), F32)],
        compiler_params=_params(("parallel", "arbitrary")),
    )(qkv, qkv, qkv, lcar, d_o)


def _local_step(x, target, meta, gains, w_in, b_forget, w_fox, w_sb, w_out, w_up, conv_w, conv_b, w_down,
                ffn_block=None):
    s, d = x.shape
    ffn_block = ffn_block or w_up.shape[1] // 2
    n_valid = N_META + s
    lp = -(-n_valid // BLK) * BLK
    pad = lp - n_valid
    nq = lp // BLK

    h0 = jnp.concatenate([meta, x, jnp.zeros((pad, d), F32)], axis=0)
    tgt = jnp.concatenate([jnp.zeros((N_META, d), F32), target, jnp.zeros((pad, d), F32)], axis=0)

    q_a, k_a, v_a, f_a, q_b, k_b, v_b, g_a, g_b = jnp.split(
        w_in, [512, 1024, 1536, 1544, 2056, 2568, 3080, 4104], axis=1)
    w_qkv = jnp.concatenate([q_a, k_a, v_a, q_b, k_b, v_b], axis=1)
    w_gf = jnp.concatenate([g_a, g_b, f_a, jnp.zeros((d, F_PAD - N_HEADS), BF16)], axis=1)
    b_pad = jnp.concatenate([b_forget.reshape(1, N_HEADS), jnp.zeros((1, LANES - N_HEADS), F32)], axis=1)
    g0, g1, g2, g3 = (gains[i:i + 1] for i in range(4))

    xn1 = _rmsnorm_fwd(h0, g0, "norm1_fwd")
    qkv = _mm(xn1, w_qkv, "nn", BF16, "proj_qkv")
    gf = _mm(xn1, w_gf, "nn", F32, "proj_gates")
    fpre = gf[:, 2 * d:2 * d + LANES]
    c = _forget_fwd(fpre, b_pad)[:, :N_HEADS]
    c_pieces = _bf16_pieces(c)
    one = jnp.ones((lp, N_HEADS), BF16)
    kaug = _aug_lanes(3 * [one] + [-x for x in c_pieces] + 3 * [one])
    o_a, lse = _fox_fwd(qkv, _aug_lanes(c_pieces + 3 * [one]), kaug)
    o_b, lcar = _sb_fwd(qkv)
    h1, ya, yb, gated, mixed = _mix_fwd(o_a, o_b, gf, h0, w_fox, w_sb, w_out, g1)
    xn3 = _rmsnorm_fwd(h1, g2, "norm3_fwd")
    up = _mm(xn3, w_up, "nn", F32, "ffn_up")
    act = _conv_gelu_fwd(up, conv_w, conv_b, ffn_block)
    ffn = _mm(act, w_down, "nn", F32, "ffn_down")
    dy, loss_acc = _out_loss(h1, ffn, g3, tgt, n_valid)
    loss = loss_acc[0, 0]

    d_ffn, dg3 = _rmsnorm_bwd(ffn, g3, dy, None, BF16, "norm4_bwd")
    d_act = _mm(d_ffn, w_down, "nt", F32, "ffn_down_dx")
    gw_down = _mm(act, d_ffn, "tn", BF16, "ffn_down_dw")
    d_up, g_conv_w, g_conv_b = _conv_gelu_bwd(up, d_act, conv_w, conv_b, ffn_block)
    d_xn3 = _mm(d_up, w_up, "nt", F32, "ffn_up_dx")
    gw_up = _mm(xn3, d_up, "tn", BF16, "ffn_up_dw")
    dh1, dg2 = _rmsnorm_bwd(h1, g2, d_xn3, dy, F32, "norm3_bwd")

    d_mixed, dg1 = _rmsnorm_bwd(mixed, g1, dh1, None, BF16, "norm2_bwd")
    d_gated = _mm(d_mixed, w_out, "nt", F32, "out_dx")
    gw_out = _mm(gated, d_mixed, "tn", BF16, "out_dw")
    d_ya, d_yb, d_ga, d_gb = _gate_bwd(d_gated, gf, ya, yb)
    d_oa = _mm(d_ya, w_fox, "nt", F32, "fox_o_dx")
    gw_fox = _mm(o_a, d_ya, "tn", BF16, "fox_o_dw")
    d_ob = _mm(d_yb, w_sb, "nt", F32, "sb_o_dx")
    gw_sb = _mm(o_b, d_yb, "tn", BF16, "sb_o_dw")
    neg_lse = [-x for x in _bf16_pieces(lse[:, ::HEAD_DIM])]
    neg_dsum = [-x for x in _bf16_pieces(_head_dots(d_oa, o_a)[:, :N_HEADS])]
    qaug = _aug_lanes(c_pieces + 3 * [one] + neg_lse)
    dq_a, dk_a, dv_a, dcrow, drow = _fox_bwd(qkv, qaug, kaug, _aug_lanes(neg_dsum), d_oa)
    dq_b, dk_b, dv_b = _sb_bwd(qkv, lcar, d_ob)
    dc = dcrow.transpose(0, 2, 1, 3).reshape(N_HEADS, lp).T + drow[:, ::HEAD_DIM]
    dc = jnp.concatenate([dc, jnp.zeros((lp, LANES - N_HEADS), F32)], axis=1)
    df, db = _forget_bwd(dc, fpre, b_pad)
    lane = jnp.arange(LANES) < N_HEADS
    df = jnp.where(lane[None, :], df, 0.0)
    d_proj = jnp.concatenate(
        [dq_a, dk_a.astype(BF16), dv_a.astype(BF16), dq_b, dk_b.astype(BF16), dv_b.astype(BF16),
         d_ga, d_gb, df.astype(BF16), jnp.zeros((lp, F_PAD - LANES), BF16)], axis=1)
    w_in_p = jnp.concatenate([w_qkv, w_gf], axis=1)
    d_xn1 = _mm(d_proj, w_in_p, "nt", F32, "proj_dx")
    gw_in_p = _mm(xn1, d_proj, "tn", BF16, "proj_dw")
    dh0, dg0 = _rmsnorm_bwd(h0, g0, d_xn1, dh1, F32, "norm1_bwd")

    qkv_parts = jnp.split(gw_in_p[:, :6 * WIDTH], 6, axis=1)
    gw_in = jnp.concatenate(
        qkv_parts[:3] + [gw_in_p[:, 6 * WIDTH + 2 * d:6 * WIDTH + 2 * d + N_HEADS]] + qkv_parts[3:]
        + [gw_in_p[:, 6 * WIDTH:6 * WIDTH + 2 * d]], axis=1)
    grads = {
        "meta_tokens": dh0[:N_META],
        "norm_gains": jnp.concatenate([dg0, dg1, dg2, dg3], axis=0),
        "w_in": gw_in,
        "b_forget": db[:, :N_HEADS],
        "w_o_fox": gw_fox,
        "w_o_sb": gw_sb,
        "w_out": gw_out,
        "w_up": gw_up,
        "conv_w": g_conv_w,
        "conv_b": g_conv_b,
        "w_down": gw_down,
    }
    return loss, dh0[N_META:n_valid], grads


MESH_IDS = pl.DeviceIdType.MESH


def _window(ref, kind, idx, rows, cols):
    if kind == "slots":
        return ref.at[idx]
    if kind == "gate_value":
        half = N_DEV // 2
        idx = jnp.where(idx < half, 2 * idx, 2 * (idx - half) + 1)
        kind = "cols"
    if kind == "cols":
        return ref.at[:, pl.ds(pl.multiple_of(idx * cols, cols & -cols), cols)]
    return ref.at[pl.ds(pl.multiple_of(idx * rows, rows & -rows), rows), :]


def _gathered_shape(shape, kind):
    rows, cols = shape
    return {"slots": (N_DEV, rows, cols), "cols": (rows, N_DEV * cols), "gate_value": (rows, N_DEV * cols),
            "rows": (N_DEV * rows, cols)}[kind]


def _all_gather(shards, kinds):
    n = len(shards)

    def body(*refs):
        ins, outs = refs[:n], refs[n:2 * n]
        send_sems, recv_sems, local_sems = refs[2 * n:]
        x, y, c = lax.axis_index("x"), lax.axis_index("y"), lax.axis_index("c")
        me, sibling = (x, y, c), (x, y, 1 - c)
        chips = [(1 - x, y), (x, 1 - y), (1 - x, 1 - y)]

        def part(t, px, py, pc):
            return _window(outs[t], kinds[t], 4 * px + 2 * py + pc, *shards[t].shape)

        def copy(k, t, blk, to, src=None):
            return pltpu.make_async_remote_copy(
                src_ref=part(t, *blk) if src is None else src, dst_ref=part(t, *blk),
                send_sem=send_sems.at[k, t], recv_sem=recv_sems.at[k, t],
                device_id=to, device_id_type=MESH_IDS)

        mine = [pltpu.make_async_copy(ins[t], part(t, *me), local_sems.at[t]) for t in range(n)]
        for cp in mine:
            cp.start()
        first = [copy(0, t, me, sibling, src=ins[t]) for t in range(n)]
        first += [copy(1 + j, t, me, (*chip, c), src=ins[t]) for j, chip in enumerate(chips) for t in range(n)]
        for cp in first:
            cp.start()
        passed = []
        for j, chip in enumerate(chips):
            for t in range(n):
                copy(1 + j, t, (*chip, c), me).wait_recv()
                passed.append(copy(4 + j, t, (*chip, c), sibling))
                passed[-1].start()
        for t in range(n):
            copy(0, t, sibling, me).wait_recv()
        for j, chip in enumerate(chips):
            for t in range(n):
                copy(4 + j, t, (*chip, 1 - c), me).wait_recv()
        for cp in first + passed:
            cp.wait_send()
        for cp in mine:
            cp.wait()

    any_space = pl.BlockSpec(memory_space=pl.ANY)
    return pl.pallas_call(
        body, name="all_gather",
        out_shape=tuple(jax.ShapeDtypeStruct(_gathered_shape(a.shape, k), a.dtype) for a, k in zip(shards, kinds)),
        in_specs=[any_space] * n,
        out_specs=tuple([any_space] * n),
        scratch_shapes=[pltpu.SemaphoreType.DMA((7, n)), pltpu.SemaphoreType.DMA((7, n)),
                        pltpu.SemaphoreType.DMA((n,))],
    )(*shards)


def _exchange(grads, kinds, shard_shapes):
    n = len(grads)

    def body(*refs):
        ins, outs = refs[:n], refs[n:2 * n]
        send_sems, recv_sems, local_sems = refs[2 * n:]
        x, y, c = lax.axis_index("x"), lax.axis_index("y"), lax.axis_index("c")
        my = 4 * x + 2 * y + c

        def part(t, idx):
            return ins[t] if kinds[t] == "all" else _window(ins[t], kinds[t], idx, *shard_shapes[t])

        mine = [pltpu.make_async_copy(part(t, my), outs[t].at[my], local_sems.at[t]) for t in range(n)]
        for cp in mine:
            cp.start()
        copies = []
        for rel in range(1, N_DEV):
            px, py, pc = x ^ (rel >> 2), y ^ ((rel >> 1) & 1), c ^ (rel & 1)
            peer = 4 * px + 2 * py + pc
            for t in range(n):
                cp = pltpu.make_async_remote_copy(
                    src_ref=part(t, peer), dst_ref=outs[t].at[my],
                    send_sem=send_sems.at[rel - 1, t], recv_sem=recv_sems.at[rel - 1, t],
                    device_id=(px, py, pc), device_id_type=MESH_IDS)
                cp.start()
                arrival = pltpu.make_async_remote_copy(
                    src_ref=part(t, my), dst_ref=outs[t].at[peer],
                    send_sem=send_sems.at[rel - 1, t], recv_sem=recv_sems.at[rel - 1, t],
                    device_id=(px, py, pc), device_id_type=MESH_IDS)
                copies.append((cp, arrival))
        for _, arrival in copies:
            arrival.wait_recv()
        for cp, _ in copies:
            cp.wait_send()
        for cp in mine:
            cp.wait()

    any_space = pl.BlockSpec(memory_space=pl.ANY)
    return pl.pallas_call(
        body, name="grad_exchange",
        out_shape=tuple(jax.ShapeDtypeStruct((N_DEV,) + tuple(s), g.dtype) for g, s in zip(grads, shard_shapes)),
        in_specs=[any_space] * n,
        out_specs=tuple([any_space] * n),
        scratch_shapes=[pltpu.SemaphoreType.DMA((7, n)), pltpu.SemaphoreType.DMA((7, n)),
                        pltpu.SemaphoreType.DMA((n,))],
    )(*grads)


def _sum_adamw(parts, w, m, v, name):
    rows, cols = w.shape
    n, rows_p, cols_p = parts.shape
    tr = _tile(rows, BLK, SUBLANES) if rows > BLK else rows
    tp = tr if rows_p == rows else rows_p
    c1 = 1.0 - ADAM_B1 ** ADAM_STEP
    c2 = 1.0 - ADAM_B2 ** ADAM_STEP

    def body(p_ref, w_ref, m_ref, v_ref, g_ref, d_ref, nm_ref, nv_ref):
        gv = p_ref[0, 0:tr, 0:cols].astype(F32)
        for s in range(1, n):
            gv = gv + p_ref[s, 0:tr, 0:cols].astype(F32)
        g_ref[...] = gv
        nm = ADAM_B1 * m_ref[...] + (1.0 - ADAM_B1) * gv
        nv = ADAM_B2 * v_ref[...] + (1.0 - ADAM_B2) * (gv * gv)
        m_hat = nm / c1
        v_hat = nv / c2
        d_ref[...] = -ADAM_LR * (m_hat / (jnp.sqrt(v_hat) + ADAM_EPS) + ADAM_WD * w_ref[...])
        nm_ref[...] = nm
        nv_ref[...] = nv

    spec = pl.BlockSpec((tr, cols), lambda i: (i, 0))
    out = jax.ShapeDtypeStruct((rows, cols), F32)
    return pl.pallas_call(
        body, name=name,
        out_shape=(out, out, out, out),
        grid=(rows // tr,),
        in_specs=[pl.BlockSpec((n, tp, cols_p), lambda i: (0, i, 0)), spec, spec, spec],
        out_specs=(spec, spec, spec, spec),
        compiler_params=_params(("parallel",)),
    )(parts, w, m, v)


def _pad2(a, rows, cols):
    return jnp.pad(a, ((0, rows - a.shape[0]), (0, cols - a.shape[1])))


WEIGHTS = ["meta_tokens", "norm_gains", "w_in", "b_forget", "w_o_fox", "w_o_sb", "w_out", "w_up", "conv_w",
           "conv_b", "w_down"]


def kernel(x, meta_tokens, norm_gains, w_in, b_forget, w_o_fox, w_o_sb, w_out, w_up, conv_w, conv_b, w_down, loss_target, m_meta_tokens, m_norm_gains, m_w_in, m_b_forget, m_w_o_fox, m_w_o_sb, m_w_out, m_w_up, m_conv_w, m_conv_b, m_w_down, v_meta_tokens, v_norm_gains, v_w_in, v_b_forget, v_w_o_fox, v_w_o_sb, v_w_out, v_w_up, v_conv_w, v_conv_b, v_w_down):
    w = dict(meta_tokens=meta_tokens, norm_gains=norm_gains, w_in=w_in, b_forget=b_forget, w_o_fox=w_o_fox,
             w_o_sb=w_o_sb, w_out=w_out, w_up=w_up, conv_w=conv_w, conv_b=conv_b, w_down=w_down)
    mom = dict(meta_tokens=m_meta_tokens, norm_gains=m_norm_gains, w_in=m_w_in, b_forget=m_b_forget,
               w_o_fox=m_w_o_fox, w_o_sb=m_w_o_sb, w_out=m_w_out, w_up=m_w_up, conv_w=m_conv_w, conv_b=m_conv_b,
               w_down=m_w_down)
    vel = dict(meta_tokens=v_meta_tokens, norm_gains=v_norm_gains, w_in=v_w_in, b_forget=v_b_forget,
               w_o_fox=v_w_o_fox, w_o_sb=v_w_o_sb, w_out=v_w_out, w_up=v_w_up, conv_w=v_conv_w, conv_b=v_conv_b,
               w_down=v_w_down)
    w2 = {n: a.reshape(a.shape[-2:]) for n, a in w.items()}
    shard_shape = {n: a.shape for n, a in w2.items()}

    d = x.shape[-1]
    up_cols = shard_shape["w_up"][1]
    up_pad = -(-up_cols // LANES) * LANES
    half = N_DEV // 2
    pad_rows = lambda a: _pad2(a, SUBLANES, a.shape[1])

    shards = [
        ("w_in", "slots", w2["w_in"].astype(BF16)),
        ("w_up", "gate_value", _pad2(w2["w_up"], d, up_pad).astype(BF16)),
        ("w_o_fox", "cols", w2["w_o_fox"].astype(BF16)),
        ("w_o_sb", "cols", w2["w_o_sb"].astype(BF16)),
        ("w_out", "rows", w2["w_out"].astype(BF16)),
        ("w_down", "rows", w2["w_down"].astype(BF16)),
        ("meta_tokens", "cols", w2["meta_tokens"]),
        ("norm_gains", "cols", pad_rows(w2["norm_gains"])),
        ("conv_w", "gate_value", _pad2(w2["conv_w"], SUBLANES, up_pad)),
    ]
    full = dict(zip([s[0] for s in shards], _all_gather([s[2] for s in shards], [s[1] for s in shards])))
    w_in_full = jnp.concatenate([full["w_in"][i] for i in range(N_DEV)], axis=1)
    w_down_p = jnp.pad(full["w_down"].reshape(half, up_cols, d), ((0, 0), (0, up_pad - up_cols), (0, 0)))
    conv_b_p = jnp.pad(w2["conv_b"].reshape(2, half, up_cols), ((0, 0), (0, 0), (0, up_pad - up_cols)))
    conv_b_p = conv_b_p.transpose(1, 0, 2)

    loss, grad_x, grads = _local_step(
        x[0], loss_target[0], full["meta_tokens"], full["norm_gains"][:4], w_in_full, w2["b_forget"],
        full["w_o_fox"], full["w_o_sb"], full["w_out"], full["w_up"], full["conv_w"][:3],
        conv_b_p.reshape(1, N_DEV * up_pad), w_down_p.reshape(half * up_pad, d), ffn_block=up_pad)
    loss = lax.psum(loss, ("x", "y", "c"))

    in_cols = shard_shape["w_in"][1]
    sends = {
        "meta_tokens": ("cols", grads["meta_tokens"], (N_META, LANES)),
        "norm_gains": ("cols", pad_rows(grads["norm_gains"]), (SUBLANES, LANES)),
        "w_in": ("slots", jnp.stack([grads["w_in"][:, i * in_cols:(i + 1) * in_cols] for i in range(N_DEV)]),
                 shard_shape["w_in"]),
        "b_forget": ("all", _pad2(grads["b_forget"], SUBLANES, LANES), (SUBLANES, LANES)),
        "w_o_fox": ("cols", grads["w_o_fox"], shard_shape["w_o_fox"]),
        "w_o_sb": ("cols", grads["w_o_sb"], shard_shape["w_o_sb"]),
        "w_out": ("rows", grads["w_out"], shard_shape["w_out"]),
        "w_up": ("gate_value", grads["w_up"], (d, up_pad)),
        "conv_w": ("gate_value", pad_rows(grads["conv_w"]), (SUBLANES, up_pad)),
        "conv_b": ("all", pad_rows(grads["conv_b"].reshape(half, 2, up_pad).transpose(1, 0, 2)[:, :, :up_cols]
                                   .reshape(1, -1)),
                   (SUBLANES, N_DEV * up_cols)),
        "w_down": ("rows", grads["w_down"].reshape(half, up_pad, d)[:, :up_cols].reshape(half * up_cols, d),
                   shard_shape["w_down"]),
    }
    parts = dict(zip(WEIGHTS, _exchange([sends[n][1] for n in WEIGHTS], [sends[n][0] for n in WEIGHTS],
                                        [sends[n][2] for n in WEIGHTS])))

    grad, delta, new_m, new_v = {}, {}, {}, {}
    for n in WEIGHTS:
        shape = w[n].shape
        outs = _sum_adamw(parts[n], w2[n], mom[n].reshape(shard_shape[n]), vel[n].reshape(shard_shape[n]),
                          "adamw_" + n)
        grad[n], delta[n], new_m[n], new_v[n] = (o.reshape(shape) for o in outs)

    return (loss, grad_x[None], *[grad[n] for n in WEIGHTS], *[delta[n] for n in WEIGHTS],
            *[new_m[n] for n in WEIGHTS], *[new_v[n] for n in WEIGHTS])
```

```python
import functools
import math

import jax
import jax.numpy as jnp
from jax import lax
from jax.experimental import pallas as pl
from jax.experimental.pallas import tpu as pltpu

F32 = jnp.float32
BF16 = jnp.bfloat16

N_DEV = 8
N_META = 16
HEAD_DIM = 64
N_HEADS = 8
WIDTH = N_HEADS * HEAD_DIM
N_PAIRS = N_HEADS // 2
LANES = 128
SUBLANES = 8
EPS = 1e-6
ATT_SCALE = HEAD_DIM ** -0.5
BLK = 256
F_PAD = 256
VMEM_LIMIT = 48 << 20

ADAM_LR = 0.001
ADAM_B1 = 0.9
ADAM_B2 = 0.999
ADAM_EPS = 1e-08
ADAM_WD = 0.01
ADAM_STEP = 10

GELU_C = math.sqrt(2.0 / math.pi)
GELU_A = 0.044715
EXP_IS_ZERO = -110.0
NOT_VISITED = -1e30


def _params(sem, vmem=VMEM_LIMIT):
    return pltpu.CompilerParams(dimension_semantics=sem, vmem_limit_bytes=vmem)


def _tile(dim, cap, align=LANES):
    t = (min(cap, dim) // align) * align
    while t >= align:
        if dim % t == 0:
            return t
        t -= align
    return dim


def _log_sigmoid_parts(z):
    lp = jnp.log1p(jnp.exp(-jnp.abs(z)))
    return jnp.minimum(z, 0.0) - lp, jnp.minimum(-z, 0.0) - lp


def _sigmoid(x):
    return 1.0 / (1.0 + jnp.exp(-x))


def _split_bf16(x):
    hi = x.astype(BF16)
    lo = (x - hi.astype(F32)).astype(BF16)
    return hi, lo


def _dot(a, b, dims):
    return lax.dot_general(a, b, (dims, ((), ())), preferred_element_type=F32)


NN = ((1,), (0,))
NT = ((1,), (1,))
TN = ((0,), (0,))


def _mm(a, b, mode, out_dtype, name):
    if mode == "nn":
        (m, kc), (_, n) = a.shape, b.shape
    elif mode == "nt":
        (m, kc), (n, _) = a.shape, b.shape
    else:
        (kc, m), (_, n) = a.shape, b.shape
    tm = _tile(m, 768)
    tn = _tile(n, 1024)
    tk = _tile(kc, 1536 if mode != "tn" else 768)
    nk = kc // tk
    dims = {"nn": NN, "nt": NT, "tn": TN}[mode]

    def body(a_ref, b_ref, o_ref, acc_ref):
        k = pl.program_id(2)

        @pl.when(k == 0)
        def _():
            acc_ref[...] = jnp.zeros_like(acc_ref)

        acc_ref[...] += _dot(a_ref[...].astype(BF16), b_ref[...].astype(BF16), dims)

        @pl.when(k == nk - 1)
        def _():
            o_ref[...] = acc_ref[...].astype(out_dtype)

    if mode == "tn":
        a_spec = pl.BlockSpec((tk, tm), lambda j, i, k: (k, i))
    else:
        a_spec = pl.BlockSpec((tm, tk), lambda j, i, k: (i, k))
    if mode == "nt":
        b_spec = pl.BlockSpec((tn, tk), lambda j, i, k: (j, k))
    else:
        b_spec = pl.BlockSpec((tk, tn), lambda j, i, k: (k, j))
    return pl.pallas_call(
        body, name=name,
        out_shape=jax.ShapeDtypeStruct((m, n), out_dtype),
        grid=(n // tn, m // tm, nk),
        in_specs=[a_spec, b_spec],
        out_specs=pl.BlockSpec((tm, tn), lambda j, i, k: (i, j)),
        scratch_shapes=[pltpu.VMEM((tm, tn), F32)],
        compiler_params=_params(("parallel", "parallel", "arbitrary")),
    )(a, b)


def _rmsnorm_fwd(x, g, name):
    lp, d = x.shape

    def body(x_ref, g_ref, o_ref):
        xv = x_ref[...]
        r = lax.rsqrt(jnp.mean(xv * xv, axis=-1, keepdims=True) + EPS)
        o_ref[...] = ((xv * r) * g_ref[...]).astype(BF16)

    return pl.pallas_call(
        body, name=name,
        out_shape=jax.ShapeDtypeStruct((lp, d), BF16),
        grid=(lp // BLK,),
        in_specs=[pl.BlockSpec((BLK, d), lambda i: (i, 0)), pl.BlockSpec((1, d), lambda i: (0, 0))],
        out_specs=pl.BlockSpec((BLK, d), lambda i: (i, 0)),
        compiler_params=_params(("parallel",)),
    )(x, g)


def _rmsnorm_bwd(x, g, dy, resid, out_dtype, name):
    lp, d = x.shape
    has_resid = resid is not None

    def body(*refs):
        if has_resid:
            x_ref, g_ref, dy_ref, r_ref, dx_ref, dg_ref = refs
        else:
            x_ref, g_ref, dy_ref, dx_ref, dg_ref = refs
        i = pl.program_id(0)
        xv = x_ref[...]
        dyv = dy_ref[...].astype(F32)
        r = lax.rsqrt(jnp.mean(xv * xv, axis=-1, keepdims=True) + EPS)
        xh = xv * r
        dyg = dyv * g_ref[...]
        dx = r * (dyg - xh * jnp.mean(dyg * xh, axis=-1, keepdims=True))
        if has_resid:
            dx = dx + r_ref[...]
        dx_ref[...] = dx.astype(out_dtype)

        @pl.when(i == 0)
        def _():
            dg_ref[...] = jnp.zeros_like(dg_ref)

        dg_ref[...] += jnp.sum(dyv * xh, axis=0, keepdims=True)

    row = pl.BlockSpec((BLK, d), lambda i: (i, 0))
    vec = pl.BlockSpec((1, d), lambda i: (0, 0))
    ins = [x, g, dy] + ([resid] if has_resid else [])
    in_specs = [row, vec, row] + ([row] if has_resid else [])
    return pl.pallas_call(
        body, name=name,
        out_shape=(jax.ShapeDtypeStruct((lp, d), out_dtype), jax.ShapeDtypeStruct((1, d), F32)),
        grid=(lp // BLK,),
        in_specs=in_specs,
        out_specs=(row, vec),
        compiler_params=_params(("arbitrary",)),
    )(*ins)


def _forget_fwd(fpre, b_pad):
    lp = fpre.shape[0]

    def body(f_ref, b_ref, c_ref, carry_ref):
        i = pl.program_id(0)

        @pl.when(i == 0)
        def _():
            carry_ref[...] = jnp.zeros_like(carry_ref)

        logf, _ = _log_sigmoid_parts(f_ref[...] + b_ref[...])
        row = lax.broadcasted_iota(jnp.int32, (BLK, BLK), 0)
        col = lax.broadcasted_iota(jnp.int32, (BLK, BLK), 1)
        tri = (col <= row).astype(BF16)
        p0 = logf.astype(BF16)
        r1 = logf - p0.astype(F32)
        p1 = r1.astype(BF16)
        p2 = (r1 - p1.astype(F32)).astype(BF16)
        c = _dot(tri, p0, NN) + _dot(tri, p1, NN) + _dot(tri, p2, NN) + carry_ref[0:1, :]
        c_ref[...] = c
        carry_ref[...] = jnp.broadcast_to(c[BLK - 1:BLK, :], carry_ref.shape)

    return pl.pallas_call(
        body, name="forget_fwd",
        out_shape=jax.ShapeDtypeStruct((lp, LANES), F32),
        grid=(lp // BLK,),
        in_specs=[pl.BlockSpec((BLK, LANES), lambda i: (i, 0)), pl.BlockSpec((1, LANES), lambda i: (0, 0))],
        out_specs=pl.BlockSpec((BLK, LANES), lambda i: (i, 0)),
        scratch_shapes=[pltpu.VMEM((SUBLANES, LANES), F32)],
        compiler_params=_params(("arbitrary",)),
    )(fpre, b_pad)


def _forget_bwd(dc, fpre, b_pad):
    lp = fpre.shape[0]
    nb = lp // BLK

    def body(dc_ref, f_ref, b_ref, df_ref, db_ref, carry_ref):
        i = pl.program_id(0)

        @pl.when(i == 0)
        def _():
            carry_ref[...] = jnp.zeros_like(carry_ref)
            db_ref[...] = jnp.zeros_like(db_ref)

        dcv = dc_ref[...]
        row = lax.broadcasted_iota(jnp.int32, (BLK, BLK), 0)
        col = lax.broadcasted_iota(jnp.int32, (BLK, BLK), 1)
        tri = (col >= row).astype(BF16)
        p0 = dcv.astype(BF16)
        r1 = dcv - p0.astype(F32)
        p1 = r1.astype(BF16)
        p2 = (r1 - p1.astype(F32)).astype(BF16)
        dlogf = _dot(tri, p0, NN) + _dot(tri, p1, NN) + _dot(tri, p2, NN) + carry_ref[0:1, :]
        carry_ref[...] = jnp.broadcast_to(dlogf[0:1, :], carry_ref.shape)
        _, ls_neg = _log_sigmoid_parts(f_ref[...] + b_ref[...])
        df = dlogf * jnp.exp(ls_neg)
        df_ref[...] = df
        db_ref[...] += jnp.sum(df, axis=0, keepdims=True)

    rev = pl.BlockSpec((BLK, LANES), lambda i: (nb - 1 - i, 0))
    vec = pl.BlockSpec((1, LANES), lambda i: (0, 0))
    return pl.pallas_call(
        body, name="forget_bwd",
        out_shape=(jax.ShapeDtypeStruct((lp, LANES), F32), jax.ShapeDtypeStruct((1, LANES), F32)),
        grid=(nb,),
        in_specs=[rev, rev, vec],
        out_specs=(rev, vec),
        scratch_shapes=[pltpu.VMEM((SUBLANES, LANES), F32)],
        compiler_params=_params(("arbitrary",)),
    )(dc, fpre, b_pad)


def _mix_fwd(o_a, o_b, gates, h0, w_fox, w_sb, w_out, g1):
    lp, d = h0.shape

    def body(oa_ref, ob_ref, ga_ref, gb_ref, h_ref, wf_ref, ws_ref, wo_ref, g_ref,
             h1_ref, ya_ref, yb_ref, gated_ref, mixed_ref):
        ya = _dot(oa_ref[...].astype(BF16), wf_ref[...], NN)
        yb = _dot(ob_ref[...].astype(BF16), ws_ref[...], NN)
        gated = _sigmoid(ga_ref[...]) * ya + _sigmoid(gb_ref[...]) * yb
        gb16 = gated.astype(BF16)
        mixed = _dot(gb16, wo_ref[...], NN)
        r = lax.rsqrt(jnp.mean(mixed * mixed, axis=-1, keepdims=True) + EPS)
        h1_ref[...] = h_ref[...] + (mixed * r) * g_ref[...]
        ya_ref[...] = ya
        yb_ref[...] = yb
        gated_ref[...] = gb16
        mixed_ref[...] = mixed

    row_w = pl.BlockSpec((BLK, WIDTH), lambda i: (i, 0))
    row_d = pl.BlockSpec((BLK, d), lambda i: (i, 0))
    full = lambda s: pl.BlockSpec(s, lambda i: (0, 0))
    return pl.pallas_call(
        body, name="mix_fwd",
        out_shape=(jax.ShapeDtypeStruct((lp, d), F32), jax.ShapeDtypeStruct((lp, d), F32),
                   jax.ShapeDtypeStruct((lp, d), F32), jax.ShapeDtypeStruct((lp, d), BF16),
                   jax.ShapeDtypeStruct((lp, d), F32)),
        grid=(lp // BLK,),
        in_specs=[row_w, row_w, row_d, pl.BlockSpec((BLK, d), lambda i: (i, 1)), row_d,
                  full((WIDTH, d)), full((WIDTH, d)), full((d, d)), full((1, d))],
        out_specs=(row_d, row_d, row_d, row_d, row_d),
        compiler_params=_params(("parallel",)),
    )(o_a, o_b, gates, gates, h0, w_fox, w_sb, w_out, g1)


def _gate_bwd(d_gated, gates, ya, yb):
    lp, d = d_gated.shape

    def body(dg_ref, ga_ref, gb_ref, ya_ref, yb_ref, dya_ref, dyb_ref, dga_ref, dgb_ref):
        dg = dg_ref[...]
        sa = _sigmoid(ga_ref[...])
        sb = _sigmoid(gb_ref[...])
        dya_ref[...] = (dg * sa).astype(BF16)
        dyb_ref[...] = (dg * sb).astype(BF16)
        dga_ref[...] = (dg * ya_ref[...] * (sa * (1.0 - sa))).astype(BF16)
        dgb_ref[...] = (dg * yb_ref[...] * (sb * (1.0 - sb))).astype(BF16)

    row = pl.BlockSpec((BLK, d), lambda i: (i, 0))
    out = jax.ShapeDtypeStruct((lp, d), BF16)
    return pl.pallas_call(
        body, name="gate_bwd",
        out_shape=(out, out, out, out),
        grid=(lp // BLK,),
        in_specs=[row, row, pl.BlockSpec((BLK, d), lambda i: (i, 1)), row, row],
        out_specs=(row, row, row, row),
        compiler_params=_params(("parallel",)),
    )(d_gated, gates, gates, ya, yb)


def _shift_down(cur, prev, n):
    rolled = pltpu.roll(cur, n, 0)
    row = lax.broadcasted_iota(jnp.int32, cur.shape, 0)
    for t in range(n):
        rolled = jnp.where(row == t, prev[SUBLANES - n + t:SUBLANES - n + t + 1, :], rolled)
    return rolled


def _shift_up(cur, nxt, n):
    rows = cur.shape[0]
    rolled = pltpu.roll(cur, rows - n, 0)
    row = lax.broadcasted_iota(jnp.int32, cur.shape, 0)
    for t in range(n):
        rolled = jnp.where(row == rows - n + t, nxt[t:t + 1, :], rolled)
    return rolled


def _gelu(x):
    return 0.5 * x * (1.0 + jnp.tanh(GELU_C * (x + GELU_A * (x * x * x))))


def _gelu_and_grad(x):
    t = jnp.tanh(GELU_C * (x + GELU_A * (x * x * x)))
    half = 0.5 * (1.0 + t)
    return x * half, half + 0.5 * x * (1.0 - t * t) * (GELU_C * (1.0 + 3.0 * GELU_A * (x * x)))


def _conv_taps(cur, prev, w_ref, b_ref):
    s1 = _shift_down(cur, prev, 1)
    s2 = _shift_down(cur, prev, 2)
    u = b_ref[...] + w_ref[0:1, :] * s2
    u = u + w_ref[1:2, :] * s1
    u = u + w_ref[2:3, :] * cur
    return u, s1, s2


def _conv_gelu_fwd(up, conv_w, conv_b, tc):
    lp, f2 = up.shape
    rb = BLK // SUBLANES

    def body(u_ref, p_ref, w_ref, b_ref, act_ref):
        i = pl.program_id(0)
        keep = (i > 0).astype(F32)
        u, _, _ = _conv_taps(u_ref[...], p_ref[...] * keep, w_ref, b_ref)
        act_ref[...] = (_gelu(u[:, :tc]) * u[:, tc:]).astype(BF16)

    prev_row = lambda i: jnp.maximum(i * rb - 1, 0)
    return pl.pallas_call(
        body, name="conv_gelu_fwd",
        out_shape=jax.ShapeDtypeStruct((lp, f2 // 2), BF16),
        grid=(lp // BLK, f2 // (2 * tc)),
        in_specs=[pl.BlockSpec((BLK, 2 * tc), lambda i, j: (i, j)),
                  pl.BlockSpec((SUBLANES, 2 * tc), lambda i, j: (prev_row(i), j)),
                  pl.BlockSpec((3, 2 * tc), lambda i, j: (0, j)),
                  pl.BlockSpec((1, 2 * tc), lambda i, j: (0, j))],
        out_specs=pl.BlockSpec((BLK, tc), lambda i, j: (i, j)),
        compiler_params=_params(("parallel", "parallel")),
    )(up, up, conv_w, conv_b)


def _conv_gelu_bwd(up, d_act, conv_w, conv_b, tc):
    lp, f2 = up.shape
    nb = lp // BLK
    rb = BLK // SUBLANES

    def du_of(u, da):
        gel, grad = _gelu_and_grad(u[:, :tc])
        return jnp.concatenate([da * u[:, tc:] * grad, da * gel], axis=1)

    def body(u_ref, p_ref, n_ref, da_ref, dan_ref, w_ref, b_ref, dup_ref, dcw_ref, dcb_ref):
        i = pl.program_id(1)
        cur = u_ref[...]
        u, s1, s2 = _conv_taps(cur, p_ref[...] * (i > 0).astype(F32), w_ref, b_ref)
        du = du_of(u, da_ref[...])
        u_next, _, _ = _conv_taps(n_ref[...], cur[BLK - SUBLANES:BLK, :], w_ref, b_ref)
        du_next = du_of(u_next, dan_ref[...]) * (i < nb - 1).astype(F32)
        n1 = _shift_up(du, du_next, 1)
        n2 = _shift_up(du, du_next, 2)
        dup_ref[...] = (w_ref[2:3, :] * du + w_ref[1:2, :] * n1 + w_ref[0:1, :] * n2).astype(BF16)

        @pl.when(i == 0)
        def _():
            dcw_ref[...] = jnp.zeros_like(dcw_ref)
            dcb_ref[...] = jnp.zeros_like(dcb_ref)

        dcw_ref[0:1, :] += jnp.sum(du * s2, axis=0, keepdims=True)
        dcw_ref[1:2, :] += jnp.sum(du * s1, axis=0, keepdims=True)
        dcw_ref[2:3, :] += jnp.sum(du * cur, axis=0, keepdims=True)
        dcb_ref[...] += jnp.sum(du, axis=0, keepdims=True)

    prev_row = lambda i: jnp.maximum(i * rb - 1, 0)
    next_row = lambda i: jnp.minimum((i + 1) * rb, nb * rb - 1)
    return pl.pallas_call(
        body, name="conv_gelu_bwd",
        out_shape=(jax.ShapeDtypeStruct((lp, f2), BF16), jax.ShapeDtypeStruct((3, f2), F32),
                   jax.ShapeDtypeStruct((1, f2), F32)),
        grid=(f2 // (2 * tc), nb),
        in_specs=[pl.BlockSpec((BLK, 2 * tc), lambda j, i: (i, j)),
                  pl.BlockSpec((SUBLANES, 2 * tc), lambda j, i: (prev_row(i), j)),
                  pl.BlockSpec((SUBLANES, 2 * tc), lambda j, i: (next_row(i), j)),
                  pl.BlockSpec((BLK, tc), lambda j, i: (i, j)),
                  pl.BlockSpec((SUBLANES, tc), lambda j, i: (next_row(i), j)),
                  pl.BlockSpec((3, 2 * tc), lambda j, i: (0, j)),
                  pl.BlockSpec((1, 2 * tc), lambda j, i: (0, j))],
        out_specs=(pl.BlockSpec((BLK, 2 * tc), lambda j, i: (i, j)),
                   pl.BlockSpec((3, 2 * tc), lambda j, i: (0, j)),
                   pl.BlockSpec((1, 2 * tc), lambda j, i: (0, j))),
        compiler_params=_params(("parallel", "arbitrary")),
    )(up, up, up, d_act, d_act, conv_w, conv_b)


def _out_loss(h1, ffn, g3, target, n_valid):
    lp, d = h1.shape

    def body(h_ref, f_ref, g_ref, t_ref, dy_ref, loss_ref):
        i = pl.program_id(0)

        @pl.when(i == 0)
        def _():
            loss_ref[...] = jnp.zeros_like(loss_ref)

        fv = f_ref[...]
        r = lax.rsqrt(jnp.mean(fv * fv, axis=-1, keepdims=True) + EPS)
        y = h_ref[...] + (fv * r) * g_ref[...]
        row = i * BLK + lax.broadcasted_iota(jnp.int32, (BLK, 1), 0)
        valid = (row >= N_META) & (row < n_valid)
        diff = jnp.where(valid, y - t_ref[...], 0.0)
        dy_ref[...] = diff * (1.0 / d)
        per_row = jnp.mean(diff * diff, axis=-1, keepdims=True)
        loss_ref[...] += 0.5 * jnp.sum(per_row, axis=0, keepdims=True)

    row_d = pl.BlockSpec((BLK, d), lambda i: (i, 0))
    return pl.pallas_call(
        body, name="out_loss",
        out_shape=(jax.ShapeDtypeStruct((lp, d), F32), jax.ShapeDtypeStruct((SUBLANES, LANES), F32)),
        grid=(lp // BLK,),
        in_specs=[row_d, row_d, pl.BlockSpec((1, d), lambda i: (0, 0)), row_d],
        out_specs=(row_d, pl.BlockSpec((SUBLANES, LANES), lambda i: (0, 0))),
        compiler_params=_params(("arbitrary",)),
    )(h1, ffn, g3, target)


def _head_masks(rows=BLK):
    lane = lax.broadcasted_iota(jnp.int32, (rows, LANES), 1)
    return [lane < HEAD_DIM, lane >= HEAD_DIM]


def _att_specs(base, lp):
    q_spec = pl.BlockSpec((BLK, LANES), lambda p, i: (i, base + p))
    k_spec = pl.BlockSpec((lp, LANES), lambda p, i: (0, base + N_PAIRS + p))
    v_spec = pl.BlockSpec((lp, LANES), lambda p, i: (0, base + 2 * N_PAIRS + p))
    return q_spec, k_spec, v_spec


def _kv_rows(j, nb=1):
    return pl.ds(pl.multiple_of(j * BLK, nb * BLK), nb * BLK)


def _walk_kv(i, tile, reverse=False, first_pair=0, more=None):
    pairs, odd = i // 2, i % 2

    def wide(t, carry):
        tile(2 * (pairs - 1 - t) if reverse else 2 * t, 2, False)
        return carry

    def single():
        @pl.when(odd == 1)
        def _():
            tile(i - 1, 1, False)

    if reverse:
        tile(i, 1, True)
        single()
        if more is None:
            lax.fori_loop(0, pairs, wide, 0)
        else:
            lax.while_loop(lambda c: (c[0] < pairs) & c[1], lambda c: (wide(c[0], c[0]) + 1, more()), (0, more()))
    else:
        lax.fori_loop(first_pair, pairs, wide, 0)
        single()
        tile(i, 1, True)


def _walk_kv_ahead(i, lead, follow, kept):
    pairs, odd = i // 2, i % 2

    def keep(values):
        for ref, value in zip(kept, values):
            ref[...] = value

    @pl.when(pairs > 0)
    def _():
        keep(lead(0, 2))

    def wide(t, carry):
        ahead = lead(2 * jnp.minimum(t + 1, pairs - 1), 2)
        follow([ref[...] for ref in kept], 2 * t, 2, False)
        keep(ahead)
        return carry

    lax.fori_loop(0, pairs, wide, 0)

    @pl.when(odd == 1)
    def _():
        follow(lead(i - 1, 1), i - 1, 1, False)

    follow(lead(i, 1), i, 1, True)


def _bf16_pieces(x):
    rnd = lambda a: lax.reduce_precision(a, exponent_bits=8, mantissa_bits=7)
    p0 = rnd(x)
    p1 = rnd(x - p0)
    p2 = rnd(x - p0 - p1)
    return [p0.astype(BF16), p1.astype(BF16), p2.astype(BF16)]


def _aug_lanes(cols):
    lp = cols[0].shape[0]
    vals = jnp.stack([c.astype(BF16) for c in cols], axis=-1)
    vals = vals.reshape(lp, N_PAIRS, 2, len(cols))[:, :, ::-1, :]
    vals = jnp.pad(vals, ((0, 0), (0, 0), (0, 0), (0, HEAD_DIM - len(cols))))
    return vals.reshape(lp, WIDTH)


N_AUG = 3


def _fox_fwd(qkv, qaug, kaug):
    lp = qkv.shape[0]
    nq = lp // BLK

    def body(q_ref, k_ref, v_ref, qa_ref, ka_ref, o_ref, lse_ref, acc_ref, m_ref, s_ref):
        i = pl.program_id(1)
        hs = range(2)
        masks = _head_masks()
        qs = q_ref[...] * ATT_SCALE
        qa = qa_ref[...]
        qh = [jnp.where(masks[hh], qs, qa) for hh in hs]
        acc_ref[...] = jnp.zeros_like(acc_ref)
        m_ref[...] = jnp.full_like(m_ref, -1e30)
        row = lax.broadcasted_iota(jnp.int32, (BLK, BLK), 0)
        col = lax.broadcasted_iota(jnp.int32, (BLK, BLK), 1)
        causal = col <= row

        def scores(j, nb):
            rows = _kv_rows(j, nb)
            kmasks = _head_masks(nb * BLK)
            k, ka = k_ref[rows, :], ka_ref[rows, :]
            return [_dot(qh[hh], jnp.where(kmasks[hh], k, ka), NT) for hh in hs]

        def absorb(s, j, nb, diag):
            v = v_ref[_kv_rows(j, nb), :]
            kmasks = _head_masks(nb * BLK)
            vh = [jnp.where(kmasks[hh], v, jnp.ones_like(v)) for hh in hs]
            if diag:
                s = [jnp.where(causal, s[hh], -1e30) for hh in hs]
            m_prev = [m_ref[hh] for hh in hs]
            m_new = [jnp.maximum(m_prev[hh], jnp.max(s[hh], axis=-1, keepdims=True)) for hh in hs]
            p = [jnp.exp(s[hh] - m_new[hh]).astype(BF16) for hh in hs]
            for hh in hs:
                acc_ref[hh] = jnp.exp(m_prev[hh] - m_new[hh]) * acc_ref[hh] + _dot(p[hh], vh[hh], NN)
                m_ref[hh] = m_new[hh]

        _walk_kv_ahead(i, scores, absorb, [s_ref.at[hh] for hh in hs])
        acc = [acc_ref[hh] for hh in hs]
        denom = [acc[0][:, HEAD_DIM:HEAD_DIM + 1], acc[1][:, 0:1]]
        o_ref[...] = jnp.where(masks[0], acc[0] / denom[0], acc[1] / denom[1])
        lse_ref[...] = jnp.where(masks[0], m_ref[0] + jnp.log(denom[0]), m_ref[1] + jnp.log(denom[1]))

    q_spec, k_spec, v_spec = _att_specs(0, lp)
    blk = pl.BlockSpec((BLK, LANES), lambda p, i: (i, p))
    col_full = pl.BlockSpec((lp, LANES), lambda p, i: (0, p))
    out = jax.ShapeDtypeStruct((lp, WIDTH), F32)
    return pl.pallas_call(
        body, name="fox_fwd",
        out_shape=(out, out),
        grid=(N_PAIRS, nq),
        in_specs=[q_spec, k_spec, v_spec, blk, col_full],
        out_specs=(blk, blk),
        scratch_shapes=[pltpu.VMEM((2, BLK, LANES), F32), pltpu.VMEM((2, BLK, 1), F32),
                        pltpu.VMEM((2, BLK, 2 * BLK), F32)],
        compiler_params=_params(("parallel", "parallel")),
    )(qkv, qkv, qkv, qaug, kaug)


def _head_dots(a, b):
    lp = a.shape[0]

    def body(a_ref, b_ref, o_ref):
        lane = lax.broadcasted_iota(jnp.int32, (BLK, LANES), 1)
        out = jnp.zeros((BLK, LANES), F32)
        for p in range(N_PAIRS):
            cols = slice(p * LANES, (p + 1) * LANES)
            prod = a_ref[:, cols] * b_ref[:, cols]
            for hh in range(2):
                part = jnp.where((lane >= HEAD_DIM) == (hh == 1), prod, 0.0)
                out = jnp.where(lane == 2 * p + hh, jnp.sum(part, axis=-1, keepdims=True), out)
        o_ref[...] = out

    row = pl.BlockSpec((BLK, WIDTH), lambda i: (i, 0))
    return pl.pallas_call(
        body, name="head_dots",
        out_shape=jax.ShapeDtypeStruct((lp, LANES), F32),
        grid=(lp // BLK,),
        in_specs=[row, row],
        out_specs=pl.BlockSpec((BLK, LANES), lambda i: (i, 0)),
        compiler_params=_params(("parallel",)),
    )(a, b)


def _fox_bwd(qkv, qaug, kaug, doaug, d_o):
    lp = qkv.shape[0]
    nq = lp // BLK

    def body(q_ref, k_ref, v_ref, qa_ref, ka_ref, da_ref, do_ref,
             dq_ref, dk_ref, dv_ref, dc_ref, dr_ref, acc_ref, rs_ref):
        i = pl.program_id(1)
        hs = range(2)
        masks = _head_masks()
        qs = q_ref[...] * ATT_SCALE
        qa = qa_ref[...]
        qm = [jnp.where(masks[hh], qs, 0) for hh in hs]
        qh = [jnp.where(masks[hh], qs, qa) for hh in hs]
        dov = do_ref[...].astype(BF16)
        doa = da_ref[...]
        dom = [jnp.where(masks[hh], dov, 0) for hh in hs]
        doh = [jnp.where(masks[hh], dov, doa) for hh in hs]
        row = lax.broadcasted_iota(jnp.int32, (BLK, BLK), 0)
        col = lax.broadcasted_iota(jnp.int32, (BLK, BLK), 1)
        causal = col <= row

        @pl.when(i == 0)
        def _():
            dk_ref[...] = jnp.zeros_like(dk_ref)
            dv_ref[...] = jnp.zeros_like(dv_ref)
            dc_ref[...] = jnp.zeros_like(dc_ref)

        acc_ref[...] = jnp.zeros_like(acc_ref)
        rs_ref[...] = jnp.zeros_like(rs_ref)

        def lead(j, nb):
            rows = _kv_rows(j, nb)
            kmasks = _head_masks(nb * BLK)
            k, ka, v = k_ref[rows, :], ka_ref[rows, :], v_ref[rows, :]
            ones = (lax.broadcasted_iota(jnp.int32, v.shape, 1) % HEAD_DIM < N_AUG).astype(BF16)
            logp = [_dot(qh[hh], jnp.where(kmasks[hh], k, ka), NT) for hh in hs]
            dp = [_dot(doh[hh], jnp.where(kmasks[hh], v, ones), NT) for hh in hs]
            return logp + dp

        def follow(lead_out, j, nb, diag):
            rows = _kv_rows(j, nb)
            k = k_ref[rows, :]
            logp, dp = lead_out[:2], lead_out[2:]
            p = [jnp.exp(logp[hh]) for hh in hs]
            if diag:
                p = [jnp.where(causal, p[hh], 0.0) for hh in hs]
            ds = [p[hh] * dp[hh] for hh in hs]
            dsb = [ds[hh].astype(BF16) for hh in hs]
            for hh in hs:
                acc_ref[hh] += _dot(dsb[hh], k, NN)
                col_sums = jnp.sum(ds[hh], axis=0, keepdims=True)
                for b in range(nb):
                    dc_ref[0, j + b, hh:hh + 1, :] -= col_sums[:, b * BLK:(b + 1) * BLK]
                rs_ref[hh] += jnp.sum(ds[hh], axis=-1, keepdims=True)
            dk_ref[rows, :] += _dot(dsb[0], qm[0], TN) + _dot(dsb[1], qm[1], TN)
            dv_ref[rows, :] += _dot(p[0].astype(BF16), dom[0], TN) + _dot(p[1].astype(BF16), dom[1], TN)

        _walk_kv(i, lambda j, nb, diag: follow(lead(j, nb), j, nb, diag))
        dq_ref[...] = (jnp.where(masks[0], acc_ref[0], acc_ref[1]) * ATT_SCALE).astype(BF16)
        dr_ref[...] = jnp.where(masks[0], rs_ref[0], rs_ref[1])

    q_spec, k_spec, v_spec = _att_specs(0, lp)
    blk = pl.BlockSpec((BLK, LANES), lambda p, i: (i, p))
    col_full = pl.BlockSpec((lp, LANES), lambda p, i: (0, p))
    crow_spec = pl.BlockSpec((1, nq, 2, BLK), lambda p, i: (p, 0, 0, 0))
    return pl.pallas_call(
        body, name="fox_bwd",
        out_shape=(jax.ShapeDtypeStruct((lp, WIDTH), BF16), jax.ShapeDtypeStruct((lp, WIDTH), F32),
                   jax.ShapeDtypeStruct((lp, WIDTH), F32), jax.ShapeDtypeStruct((N_PAIRS, nq, 2, BLK), F32),
                   jax.ShapeDtypeStruct((lp, WIDTH), F32)),
        grid=(N_PAIRS, nq),
        in_specs=[q_spec, k_spec, v_spec, blk, col_full, blk, blk],
        out_specs=(blk, col_full, col_full, crow_spec, blk),
        scratch_shapes=[pltpu.VMEM((2, BLK, LANES), F32), pltpu.VMEM((2, BLK, 1), F32)],
        compiler_params=_params(("parallel", "arbitrary")),
    )(qkv, qkv, qkv, qaug, kaug, doaug, d_o)


def _sb_scores(z):
    ell = jnp.minimum(z, 0.0) - jnp.log(1.0 + jnp.exp(-jnp.abs(z)))
    return ell, ell - z


def _stacked(tri):
    return jnp.concatenate([tri, tri], axis=0)


def _cumsum_dot(x, tri2):
    hi, lo = _split_bf16(x)
    return _dot(jnp.concatenate([hi, lo], axis=1), tri2, NN)


def _sb_units(qh, k, strict2, causal, nb, diag):
    hs, bs = range(2), range(nb)
    z = [_dot(qh[hh], k, NT) for hh in hs]
    sc = [[_sb_scores(z[hh][:, b * BLK:(b + 1) * BLK]) for b in bs] for hh in hs]
    ell = [[sc[hh][b][0] for b in bs] for hh in hs]
    kap = [[jnp.where(causal, sc[hh][b][1], 0.0) if diag else sc[hh][b][1] for b in bs] for hh in hs]
    later = [[_cumsum_dot(kap[hh][b], strict2) for b in bs] for hh in hs]
    return ell, kap, later


def _row_sum(x):
    return jnp.sum(x, axis=-1, keepdims=True)


def _join(blocks):
    joined = blocks[0] if len(blocks) == 1 else jnp.concatenate(blocks, axis=1)
    return joined.astype(BF16)


def _sb_fwd(qkv):
    lp = qkv.shape[0]
    nq = lp // BLK
    assert nq <= HEAD_DIM

    def body(q_ref, k_ref, v_ref, o_ref, lc_ref, acc_ref, car_ref):
        i = pl.program_id(1)
        masks = _head_masks()
        qs = q_ref[...] * ATT_SCALE
        qh = [jnp.where(mk, qs, 0).astype(BF16) for mk in masks]
        row = lax.broadcasted_iota(jnp.int32, (BLK, BLK), 0)
        col = lax.broadcasted_iota(jnp.int32, (BLK, BLK), 1)
        lane = lax.broadcasted_iota(jnp.int32, (BLK, LANES), 1)
        causal = col < row
        strict2 = _stacked((row > col).astype(BF16))
        acc_ref[...] = jnp.zeros_like(acc_ref)
        car_ref[...] = jnp.zeros_like(car_ref)
        lc_ref[...] = jnp.full_like(lc_ref, NOT_VISITED)

        def tile(j, nb, diag):
            hs, bs = range(2), range(nb)
            rows = _kv_rows(j, nb)
            k, v = k_ref[rows, :], v_ref[rows, :]
            ell, kap, later = _sb_units(qh, k, strict2, causal, nb, diag)
            car = [[None] * nb for _ in hs]
            for hh in hs:
                run = car_ref[hh]
                for b in reversed(bs):
                    car[hh][b] = run
                    run = run + _row_sum(kap[hh][b])
                car_ref[hh] = run
            if not diag:
                kept = lc_ref[...]
                for hh in hs:
                    for b in bs:
                        kept = jnp.where(lane == j + b + HEAD_DIM * hh, car[hh][b], kept)
                lc_ref[...] = kept
            a = [[jnp.exp(ell[hh][b] + later[hh][b] + car[hh][b]) for b in bs] for hh in hs]
            if diag:
                a = [[jnp.where(causal, a[hh][b], 0.0) for b in bs] for hh in hs]
            for hh in hs:
                acc_ref[hh] += _dot(_join(a[hh]), v, NN)

        _walk_kv(i, tile, reverse=True, more=lambda: jnp.max(car_ref[...]) > EXP_IS_ZERO)
        o_ref[...] = jnp.where(masks[0], acc_ref[0], acc_ref[1])

    q_spec, k_spec, v_spec = _att_specs(3 * N_PAIRS, lp)
    blk = pl.BlockSpec((BLK, LANES), lambda p, i: (i, p))
    out = jax.ShapeDtypeStruct((lp, WIDTH), F32)
    return pl.pallas_call(
        body, name="sb_fwd",
        out_shape=(out, out),
        grid=(N_PAIRS, nq),
        in_specs=[q_spec, k_spec, v_spec],
        out_specs=(blk, blk),
        scratch_shapes=[pltpu.VMEM((2, BLK, LANES), F32), pltpu.VMEM((2, BLK, 1), F32)],
        compiler_params=_params(("parallel", "parallel")),
    )(qkv, qkv, qkv)


def _sb_bwd(qkv, lcar, d_o):
    lp = qkv.shape[0]
    nq = lp // BLK

    def body(q_ref, k_ref, v_ref, lc_ref, do_ref, dq_ref, dk_ref, dv_ref, acc_ref, cg_ref):
        i = pl.program_id(1)
        masks = _head_masks()
        qs = q_ref[...] * ATT_SCALE
        qh = [jnp.where(mk, qs, 0).astype(BF16) for mk in masks]
        dov = do_ref[...]
        doh = [jnp.where(mk, dov, 0.0).astype(BF16) for mk in masks]
        lcv = lc_ref[...]
        lane_row = lax.broadcasted_iota(jnp.int32, (1, LANES), 1)
        alive = jnp.where(jnp.max(lcv, axis=0, keepdims=True) > EXP_IS_ZERO, 1.0, 0.0)
        n_alive = jnp.maximum(jnp.sum(jnp.where(lane_row < HEAD_DIM, alive, 0.0)),
                              jnp.sum(jnp.where(lane_row >= HEAD_DIM, alive, 0.0))).astype(jnp.int32)
        first_pair = jnp.maximum(i - n_alive, 0) // 2
        row = lax.broadcasted_iota(jnp.int32, (BLK, BLK), 0)
        col = lax.broadcasted_iota(jnp.int32, (BLK, BLK), 1)
        lane = lax.broadcasted_iota(jnp.int32, (BLK, LANES), 1)
        causal = col < row
        strict2 = _stacked((row > col).astype(BF16))
        before2 = _stacked((row < col).astype(BF16))

        @pl.when(i == 0)
        def _():
            dk_ref[...] = jnp.zeros_like(dk_ref)
            dv_ref[...] = jnp.zeros_like(dv_ref)

        acc_ref[...] = jnp.zeros_like(acc_ref)
        cg_ref[...] = jnp.zeros_like(cg_ref)

        def tile(j, nb, diag):
            hs, bs = range(2), range(nb)
            rows = _kv_rows(j, nb)
            k, v = k_ref[rows, :], v_ref[rows, :]
            if diag:
                car = [[0.0] for _ in hs]
            else:
                car = [[_row_sum(jnp.where(lane == j + b + HEAD_DIM * hh, lcv, 0.0)) for b in bs] for hh in hs]
            da = [_dot(doh[hh], v, NT) for hh in hs]
            ell, _, later = _sb_units(qh, k, strict2, causal, nb, diag)
            a = [[jnp.exp(ell[hh][b] + later[hh][b] + car[hh][b]) for b in bs] for hh in hs]
            if diag:
                a = [[jnp.where(causal, a[hh][b], 0.0) for b in bs] for hh in hs]
            g = [[da[hh][:, b * BLK:(b + 1) * BLK] * a[hh][b] for b in bs] for hh in hs]
            cg = [[_cumsum_dot(g[hh][b], before2) for b in bs] for hh in hs]
            before = [[None] * nb for _ in hs]
            for hh in hs:
                run = cg_ref[hh]
                for b in bs:
                    before[hh][b] = run
                    run = run + _row_sum(g[hh][b])
                cg_ref[hh] = run
            dz = [[g[hh][b] - jnp.exp(ell[hh][b]) * (g[hh][b] + cg[hh][b] + before[hh][b]) for b in bs] for hh in hs]
            if diag:
                dz = [[jnp.where(causal, dz[hh][b], 0.0) for b in bs] for hh in hs]
            dzb = [_join(dz[hh]) for hh in hs]
            ab = [_join(a[hh]) for hh in hs]
            for hh in hs:
                acc_ref[hh] += _dot(dzb[hh], k, NN)
            dk_ref[rows, :] += _dot(dzb[0], qh[0], TN) + _dot(dzb[1], qh[1], TN)
            dv_ref[rows, :] += _dot(ab[0], doh[0], TN) + _dot(ab[1], doh[1], TN)

        _walk_kv(i, tile, first_pair=first_pair)
        dq_ref[...] = (jnp.where(masks[0], acc_ref[0], acc_ref[1]) * ATT_SCALE).astype(BF16)

    q_spec, k_spec, v_spec = _att_specs(3 * N_PAIRS, lp)
    blk = pl.BlockSpec((BLK, LANES), lambda p, i: (i, p))
    col_full = pl.BlockSpec((lp, LANES), lambda p, i: (0, p))
    return pl.pallas_call(
        body, name="sb_bwd",
        out_shape=(jax.ShapeDtypeStruct((lp, WIDTH), BF16), jax.ShapeDtypeStruct((lp, WIDTH), F32),
                   jax.ShapeDtypeStruct((lp, WIDTH), F32)),
        grid=(N_PAIRS, nq),
        in_specs=[q_spec, k_spec, v_spec, blk, blk],
        out_specs=(blk, col_full, col_full),
        scratch_shapes=[pltpu.VMEM((2, BLK, LANES), F32), pltpu.VMEM((2, BLK, 1), F32)],
        compiler_params=_params(("parallel", "arbitrary")),
    )(qkv, qkv, qkv, lcar, d_o)


def _local_step(x, target, meta, gains, w_in, b_forget, w_fox, w_sb, w_out, w_up, conv_w, conv_b, w_down,
                ffn_block=None):
    s, d = x.shape
    ffn_block = ffn_block or w_up.shape[1] // 2
    n_valid = N_META + s
    lp = -(-n_valid // BLK) * BLK
    pad = lp - n_valid
    nq = lp // BLK

    h0 = jnp.concatenate([meta, x, jnp.zeros((pad, d), F32)], axis=0)
    tgt = jnp.concatenate([jnp.zeros((N_META, d), F32), target, jnp.zeros((pad, d), F32)], axis=0)

    q_a, k_a, v_a, f_a, q_b, k_b, v_b, g_a, g_b = jnp.split(
        w_in, [512, 1024, 1536, 1544, 2056, 2568, 3080, 4104], axis=1)
    w_qkv = jnp.concatenate([q_a, k_a, v_a, q_b, k_b, v_b], axis=1)
    w_gf = jnp.concatenate([g_a, g_b, f_a, jnp.zeros((d, F_PAD - N_HEADS), BF16)], axis=1)
    b_pad = jnp.concatenate([b_forget.reshape(1, N_HEADS), jnp.zeros((1, LANES - N_HEADS), F32)], axis=1)
    g0, g1, g2, g3 = (gains[i:i + 1] for i in range(4))

    xn1 = _rmsnorm_fwd(h0, g0, "norm1_fwd")
    qkv = _mm(xn1, w_qkv, "nn", BF16, "proj_qkv")
    gf = _mm(xn1, w_gf, "nn", F32, "proj_gates")
    fpre = gf[:, 2 * d:2 * d + LANES]
    c = _forget_fwd(fpre, b_pad)[:, :N_HEADS]
    c_pieces = _bf16_pieces(c)
    one = jnp.ones((lp, N_HEADS), BF16)
    kaug = _aug_lanes(3 * [one] + [-x for x in c_pieces] + 3 * [one])
    o_a, lse = _fox_fwd(qkv, _aug_lanes(c_pieces + 3 * [one]), kaug)
    o_b, lcar = _sb_fwd(qkv)
    h1, ya, yb, gated, mixed = _mix_fwd(o_a, o_b, gf, h0, w_fox, w_sb, w_out, g1)
    xn3 = _rmsnorm_fwd(h1, g2, "norm3_fwd")
    up = _mm(xn3, w_up, "nn", F32, "ffn_up")
    act = _conv_gelu_fwd(up, conv_w, conv_b, ffn_block)
    ffn = _mm(act, w_down, "nn", F32, "ffn_down")
    dy, loss_acc = _out_loss(h1, ffn, g3, tgt, n_valid)
    loss = loss_acc[0, 0]

    d_ffn, dg3 = _rmsnorm_bwd(ffn, g3, dy, None, BF16, "norm4_bwd")
    d_act = _mm(d_ffn, w_down, "nt", F32, "ffn_down_dx")
    gw_down = _mm(act, d_ffn, "tn", BF16, "ffn_down_dw")
    d_up, g_conv_w, g_conv_b = _conv_gelu_bwd(up, d_act, conv_w, conv_b, ffn_block)
    d_xn3 = _mm(d_up, w_up, "nt", F32, "ffn_up_dx")
    gw_up = _mm(xn3, d_up, "tn", BF16, "ffn_up_dw")
    dh1, dg2 = _rmsnorm_bwd(h1, g2, d_xn3, dy, F32, "norm3_bwd")

    d_mixed, dg1 = _rmsnorm_bwd(mixed, g1, dh1, None, BF16, "norm2_bwd")
    d_gated = _mm(d_mixed, w_out, "nt", F32, "out_dx")
    gw_out = _mm(gated, d_mixed, "tn", BF16, "out_dw")
    d_ya, d_yb, d_ga, d_gb = _gate_bwd(d_gated, gf, ya, yb)
    d_oa = _mm(d_ya, w_fox, "nt", F32, "fox_o_dx")
    gw_fox = _mm(o_a, d_ya, "tn", BF16, "fox_o_dw")
    d_ob = _mm(d_yb, w_sb, "nt", F32, "sb_o_dx")
    gw_sb = _mm(o_b, d_yb, "tn", BF16, "sb_o_dw")
    neg_lse = [-x for x in _bf16_pieces(lse[:, ::HEAD_DIM])]
    neg_dsum = [-x for x in _bf16_pieces(_head_dots(d_oa, o_a)[:, :N_HEADS])]
    qaug = _aug_lanes(c_pieces + 3 * [one] + neg_lse)
    dq_a, dk_a, dv_a, dcrow, drow = _fox_bwd(qkv, qaug, kaug, _aug_lanes(neg_dsum), d_oa)
    dq_b, dk_b, dv_b = _sb_bwd(qkv, lcar, d_ob)
    dc = dcrow.transpose(0, 2, 1, 3).reshape(N_HEADS, lp).T + drow[:, ::HEAD_DIM]
    dc = jnp.concatenate([dc, jnp.zeros((lp, LANES - N_HEADS), F32)], axis=1)
    df, db = _forget_bwd(dc, fpre, b_pad)
    lane = jnp.arange(LANES) < N_HEADS
    df = jnp.where(lane[None, :], df, 0.0)
    d_proj = jnp.concatenate(
        [dq_a, dk_a.astype(BF16), dv_a.astype(BF16), dq_b, dk_b.astype(BF16), dv_b.astype(BF16),
         d_ga, d_gb, df.astype(BF16), jnp.zeros((lp, F_PAD - LANES), BF16)], axis=1)
    w_in_p = jnp.concatenate([w_qkv, w_gf], axis=1)
    d_xn1 = _mm(d_proj, w_in_p, "nt", F32, "proj_dx")
    gw_in_p = _mm(xn1, d_proj, "tn", BF16, "proj_dw")
    dh0, dg0 = _rmsnorm_bwd(h0, g0, d_xn1, dh1, F32, "norm1_bwd")

    qkv_parts = jnp.split(gw_in_p[:, :6 * WIDTH], 6, axis=1)
    gw_in = jnp.concatenate(
        qkv_parts[:3] + [gw_in_p[:, 6 * WIDTH + 2 * d:6 * WIDTH + 2 * d + N_HEADS]] + qkv_parts[3:]
        + [gw_in_p[:, 6 * WIDTH:6 * WIDTH + 2 * d]], axis=1)
    grads = {
        "meta_tokens": dh0[:N_META],
        "norm_gains": jnp.concatenate([dg0, dg1, dg2, dg3], axis=0),
        "w_in": gw_in,
        "b_forget": db[:, :N_HEADS],
        "w_o_fox": gw_fox,
        "w_o_sb": gw_sb,
        "w_out": gw_out,
        "w_up": gw_up,
        "conv_w": g_conv_w,
        "conv_b": g_conv_b,
        "w_down": gw_down,
    }
    return loss, dh0[N_META:n_valid], grads


MESH_IDS = pl.DeviceIdType.MESH


def _window(ref, kind, idx, rows, cols):
    if kind == "slots":
        return ref.at[idx]
    if kind == "gate_value":
        half = N_DEV // 2
        idx = jnp.where(idx < half, 2 * idx, 2 * (idx - half) + 1)
        kind = "cols"
    if kind == "cols":
        return ref.at[:, pl.ds(pl.multiple_of(idx * cols, cols & -cols), cols)]
    return ref.at[pl.ds(pl.multiple_of(idx * rows, rows & -rows), rows), :]


def _gathered_shape(shape, kind):
    rows, cols = shape
    return {"slots": (N_DEV, rows, cols), "cols": (rows, N_DEV * cols), "gate_value": (rows, N_DEV * cols),
            "rows": (N_DEV * rows, cols)}[kind]


def _all_gather(shards, kinds):
    n = len(shards)

    def body(*refs):
        ins, outs = refs[:n], refs[n:2 * n]
        send_sems, recv_sems, local_sems = refs[2 * n:]
        x, y, c = lax.axis_index("x"), lax.axis_index("y"), lax.axis_index("c")
        me, sibling = (x, y, c), (x, y, 1 - c)
        chips = [(1 - x, y), (x, 1 - y), (1 - x, 1 - y)]

        def part(t, px, py, pc):
            return _window(outs[t], kinds[t], 4 * px + 2 * py + pc, *shards[t].shape)

        def copy(k, t, blk, to, src=None):
            return pltpu.make_async_remote_copy(
                src_ref=part(t, *blk) if src is None else src, dst_ref=part(t, *blk),
                send_sem=send_sems.at[k, t], recv_sem=recv_sems.at[k, t],
                device_id=to, device_id_type=MESH_IDS)

        mine = [pltpu.make_async_copy(ins[t], part(t, *me), local_sems.at[t]) for t in range(n)]
        for cp in mine:
            cp.start()
        first = [copy(0, t, me, sibling, src=ins[t]) for t in range(n)]
        first += [copy(1 + j, t, me, (*chip, c), src=ins[t]) for j, chip in enumerate(chips) for t in range(n)]
        for cp in first:
            cp.start()
        passed = []
        for j, chip in enumerate(chips):
            for t in range(n):
                copy(1 + j, t, (*chip, c), me).wait_recv()
                passed.append(copy(4 + j, t, (*chip, c), sibling))
                passed[-1].start()
        for t in range(n):
            copy(0, t, sibling, me).wait_recv()
        for j, chip in enumerate(chips):
            for t in range(n):
                copy(4 + j, t, (*chip, 1 - c), me).wait_recv()
        for cp in first + passed:
            cp.wait_send()
        for cp in mine:
            cp.wait()

    any_space = pl.BlockSpec(memory_space=pl.ANY)
    return pl.pallas_call(
        body, name="all_gather",
        out_shape=tuple(jax.ShapeDtypeStruct(_gathered_shape(a.shape, k), a.dtype) for a, k in zip(shards, kinds)),
        in_specs=[any_space] * n,
        out_specs=tuple([any_space] * n),
        scratch_shapes=[pltpu.SemaphoreType.DMA((7, n)), pltpu.SemaphoreType.DMA((7, n)),
                        pltpu.SemaphoreType.DMA((n,))],
    )(*shards)


def _exchange(grads, kinds, shard_shapes):
    n = len(grads)

    def body(*refs):
        ins, outs = refs[:n], refs[n:2 * n]
        send_sems, recv_sems, local_sems = refs[2 * n:]
        x, y, c = lax.axis_index("x"), lax.axis_index("y"), lax.axis_index("c")
        my = 4 * x + 2 * y + c

        def part(t, idx):
            return ins[t] if kinds[t] == "all" else _window(ins[t], kinds[t], idx, *shard_shapes[t])

        mine = [pltpu.make_async_copy(part(t, my), outs[t].at[my], local_sems.at[t]) for t in range(n)]
        for cp in mine:
            cp.start()
        copies = []
        for rel in range(1, N_DEV):
            px, py, pc = x ^ (rel >> 2), y ^ ((rel >> 1) & 1), c ^ (rel & 1)
            peer = 4 * px + 2 * py + pc
            for t in range(n):
                cp = pltpu.make_async_remote_copy(
                    src_ref=part(t, peer), dst_ref=outs[t].at[my],
                    send_sem=send_sems.at[rel - 1, t], recv_sem=recv_sems.at[rel - 1, t],
                    device_id=(px, py, pc), device_id_type=MESH_IDS)
                cp.start()
                arrival = pltpu.make_async_remote_copy(
                    src_ref=part(t, my), dst_ref=outs[t].at[peer],
                    send_sem=send_sems.at[rel - 1, t], recv_sem=recv_sems.at[rel - 1, t],
                    device_id=(px, py, pc), device_id_type=MESH_IDS)
                copies.append((cp, arrival))
        for _, arrival in copies:
            arrival.wait_recv()
        for cp, _ in copies:
            cp.wait_send()
        for cp in mine:
            cp.wait()

    any_space = pl.BlockSpec(memory_space=pl.ANY)
    return pl.pallas_call(
        body, name="grad_exchange",
        out_shape=tuple(jax.ShapeDtypeStruct((N_DEV,) + tuple(s), g.dtype) for g, s in zip(grads, shard_shapes)),
        in_specs=[any_space] * n,
        out_specs=tuple([any_space] * n),
        scratch_shapes=[pltpu.SemaphoreType.DMA((7, n)), pltpu.SemaphoreType.DMA((7, n)),
                        pltpu.SemaphoreType.DMA((n,))],
    )(*grads)


def _sum_adamw(parts, w, m, v, name):
    rows, cols = w.shape
    n, rows_p, cols_p = parts.shape
    tr = _tile(rows, BLK, SUBLANES) if rows > BLK else rows
    tp = tr if rows_p == rows else rows_p
    c1 = 1.0 - ADAM_B1 ** ADAM_STEP
    c2 = 1.0 - ADAM_B2 ** ADAM_STEP

    def body(p_ref, w_ref, m_ref, v_ref, g_ref, d_ref, nm_ref, nv_ref):
        gv = p_ref[0, 0:tr, 0:cols].astype(F32)
        for s in range(1, n):
            gv = gv + p_ref[s, 0:tr, 0:cols].astype(F32)
        g_ref[...] = gv
        nm = ADAM_B1 * m_ref[...] + (1.0 - ADAM_B1) * gv
        nv = ADAM_B2 * v_ref[...] + (1.0 - ADAM_B2) * (gv * gv)
        m_hat = nm / c1
        v_hat = nv / c2
        d_ref[...] = -ADAM_LR * (m_hat / (jnp.sqrt(v_hat) + ADAM_EPS) + ADAM_WD * w_ref[...])
        nm_ref[...] = nm
        nv_ref[...] = nv

    spec = pl.BlockSpec((tr, cols), lambda i: (i, 0))
    out = jax.ShapeDtypeStruct((rows, cols), F32)
    return pl.pallas_call(
        body, name=name,
        out_shape=(out, out, out, out),
        grid=(rows // tr,),
        in_specs=[pl.BlockSpec((n, tp, cols_p), lambda i: (0, i, 0)), spec, spec, spec],
        out_specs=(spec, spec, spec, spec),
        compiler_params=_params(("parallel",)),
    )(parts, w, m, v)


def _pad2(a, rows, cols):
    return jnp.pad(a, ((0, rows - a.shape[0]), (0, cols - a.shape[1])))


WEIGHTS = ["meta_tokens", "norm_gains", "w_in", "b_forget", "w_o_fox", "w_o_sb", "w_out", "w_up", "conv_w",
           "conv_b", "w_down"]


def kernel(x, meta_tokens, norm_gains, w_in, b_forget, w_o_fox, w_o_sb, w_out, w_up, conv_w, conv_b, w_down, loss_target, m_meta_tokens, m_norm_gains, m_w_in, m_b_forget, m_w_o_fox, m_w_o_sb, m_w_out, m_w_up, m_conv_w, m_conv_b, m_w_down, v_meta_tokens, v_norm_gains, v_w_in, v_b_forget, v_w_o_fox, v_w_o_sb, v_w_out, v_w_up, v_conv_w, v_conv_b, v_w_down):
    w = dict(meta_tokens=meta_tokens, norm_gains=norm_gains, w_in=w_in, b_forget=b_forget, w_o_fox=w_o_fox,
             w_o_sb=w_o_sb, w_out=w_out, w_up=w_up, conv_w=conv_w, conv_b=conv_b, w_down=w_down)
    mom = dict(meta_tokens=m_meta_tokens, norm_gains=m_norm_gains, w_in=m_w_in, b_forget=m_b_forget,
               w_o_fox=m_w_o_fox, w_o_sb=m_w_o_sb, w_out=m_w_out, w_up=m_w_up, conv_w=m_conv_w, conv_b=m_conv_b,
               w_down=m_w_down)
    vel = dict(meta_tokens=v_meta_tokens, norm_gains=v_norm_gains, w_in=v_w_in, b_forget=v_b_forget,
               w_o_fox=v_w_o_fox, w_o_sb=v_w_o_sb, w_out=v_w_out, w_up=v_w_up, conv_w=v_conv_w, conv_b=v_conv_b,
               w_down=v_w_down)
    w2 = {n: a.reshape(a.shape[-2:]) for n, a in w.items()}
    shard_shape = {n: a.shape for n, a in w2.items()}

    d = x.shape[-1]
    up_cols = shard_shape["w_up"][1]
    up_pad = -(-up_cols // LANES) * LANES
    half = N_DEV // 2
    pad_rows = lambda a: _pad2(a, SUBLANES, a.shape[1])

    shards = [
        ("w_in", "slots", w2["w_in"].astype(BF16)),
        ("w_up", "gate_value", _pad2(w2["w_up"], d, up_pad).astype(BF16)),
        ("w_o_fox", "cols", w2["w_o_fox"].astype(BF16)),
        ("w_o_sb", "cols", w2["w_o_sb"].astype(BF16)),
        ("w_out", "rows", w2["w_out"].astype(BF16)),
        ("w_down", "rows", w2["w_down"].astype(BF16)),
        ("meta_tokens", "cols", w2["meta_tokens"]),
        ("norm_gains", "cols", pad_rows(w2["norm_gains"])),
        ("conv_w", "gate_value", _pad2(w2["conv_w"], SUBLANES, up_pad)),
    ]
    full = dict(zip([s[0] for s in shards], _all_gather([s[2] for s in shards], [s[1] for s in shards])))
    w_in_full = jnp.concatenate([full["w_in"][i] for i in range(N_DEV)], axis=1)
    w_down_p = jnp.pad(full["w_down"].reshape(half, up_cols, d), ((0, 0), (0, up_pad - up_cols), (0, 0)))
    conv_b_p = jnp.pad(w2["conv_b"].reshape(2, half, up_cols), ((0, 0), (0, 0), (0, up_pad - up_cols)))
    conv_b_p = conv_b_p.transpose(1, 0, 2)

    loss, grad_x, grads = _local_step(
        x[0], loss_target[0], full["meta_tokens"], full["norm_gains"][:4], w_in_full, w2["b_forget"],
        full["w_o_fox"], full["w_o_sb"], full["w_out"], full["w_up"], full["conv_w"][:3],
        conv_b_p.reshape(1, N_DEV * up_pad), w_down_p.reshape(half * up_pad, d), ffn_block=up_pad)
    loss = lax.psum(loss, ("x", "y", "c"))

    in_cols = shard_shape["w_in"][1]
    sends = {
        "meta_tokens": ("cols", grads["meta_tokens"], (N_META, LANES)),
        "norm_gains": ("cols", pad_rows(grads["norm_gains"]), (SUBLANES, LANES)),
        "w_in": ("slots", jnp.stack([grads["w_in"][:, i * in_cols:(i + 1) * in_cols] for i in range(N_DEV)]),
                 shard_shape["w_in"]),
        "b_forget": ("all", _pad2(grads["b_forget"], SUBLANES, LANES), (SUBLANES, LANES)),
        "w_o_fox": ("cols", grads["w_o_fox"], shard_shape["w_o_fox"]),
        "w_o_sb": ("cols", grads["w_o_sb"], shard_shape["w_o_sb"]),
        "w_out": ("rows", grads["w_out"], shard_shape["w_out"]),
        "w_up": ("gate_value", grads["w_up"], (d, up_pad)),
        "conv_w": ("gate_value", pad_rows(grads["conv_w"]), (SUBLANES, up_pad)),
        "conv_b": ("all", pad_rows(grads["conv_b"].reshape(half, 2, up_pad).transpose(1, 0, 2)[:, :, :up_cols]
                                   .reshape(1, -1)),
                   (SUBLANES, N_DEV * up_cols)),
        "w_down": ("rows", grads["w_down"].reshape(half, up_pad, d)[:, :up_cols].reshape(half * up_cols, d),
                   shard_shape["w_down"]),
    }
    parts = dict(zip(WEIGHTS, _exchange([sends[n][1] for n in WEIGHTS], [sends[n][0] for n in WEIGHTS],
                                        [sends[n][2] for n in WEIGHTS])))

    grad, delta, new_m, new_v = {}, {}, {}, {}
    for n in WEIGHTS:
        shape = w[n].shape
        outs = _sum_adamw(parts[n], w2[n], mom[n].reshape(shard_shape[n]), vel[n].reshape(shard_shape[n]),
                          "adamw_" + n)
        grad[n], delta[n], new_m[n], new_v[n] = (o.reshape(shape) for o in outs)

    return (loss, grad_x[None], *[grad[n] for n in WEIGHTS], *[delta[n] for n in WEIGHTS],
            *[new_m[n] for n in WEIGHTS], *[new_v[n] for n in WEIGHTS])
```

```python
import functools
import math

import jax
import jax.numpy as jnp
from jax import lax
from jax.experimental import pallas as pl
from jax.experimental.pallas import tpu as pltpu

F32 = jnp.float32
BF16 = jnp.bfloat16

N_DEV = 8
N_META = 16
HEAD_DIM = 64
N_HEADS = 8
WIDTH = N_HEADS * HEAD_DIM
N_PAIRS = N_HEADS // 2
LANES = 128
SUBLANES = 8
EPS = 1e-6
ATT_SCALE = HEAD_DIM ** -0.5
BLK = 256
F_PAD = 256
VMEM_LIMIT = 48 << 20

ADAM_LR = 0.001
ADAM_B1 = 0.9
ADAM_B2 = 0.999
ADAM_EPS = 1e-08
ADAM_WD = 0.01
ADAM_STEP = 10

GELU_C = math.sqrt(2.0 / math.pi)
GELU_A = 0.044715
EXP_IS_ZERO = -110.0
NOT_VISITED = -1e30


def _params(sem, vmem=VMEM_LIMIT):
    return pltpu.CompilerParams(dimension_semantics=sem, vmem_limit_bytes=vmem)


def _tile(dim, cap, align=LANES):
    t = (min(cap, dim) // align) * align
    while t >= align:
        if dim % t == 0:
            return t
        t -= align
    return dim


def _log_sigmoid_parts(z):
    lp = jnp.log1p(jnp.exp(-jnp.abs(z)))
    return jnp.minimum(z, 0.0) - lp, jnp.minimum(-z, 0.0) - lp


def _sigmoid(x):
    return 1.0 / (1.0 + jnp.exp(-x))


def _split_bf16(x):
    hi = x.astype(BF16)
    lo = (x - hi.astype(F32)).astype(BF16)
    return hi, lo


def _dot(a, b, dims):
    return lax.dot_general(a, b, (dims, ((), ())), preferred_element_type=F32)


NN = ((1,), (0,))
NT = ((1,), (1,))
TN = ((0,), (0,))


def _mm(a, b, mode, out_dtype, name):
    if mode == "nn":
        (m, kc), (_, n) = a.shape, b.shape
    elif mode == "nt":
        (m, kc), (n, _) = a.shape, b.shape
    else:
        (kc, m), (_, n) = a.shape, b.shape
    tm = _tile(m, 768)
    tn = _tile(n, 1024)
    tk = _tile(kc, 1536 if mode != "tn" else 768)
    nk = kc // tk
    dims = {"nn": NN, "nt": NT, "tn": TN}[mode]

    def body(a_ref, b_ref, o_ref, acc_ref):
        k = pl.program_id(2)

        @pl.when(k == 0)
        def _():
            acc_ref[...] = jnp.zeros_like(acc_ref)

        acc_ref[...] += _dot(a_ref[...].astype(BF16), b_ref[...].astype(BF16), dims)

        @pl.when(k == nk - 1)
        def _():
            o_ref[...] = acc_ref[...].astype(out_dtype)

    if mode == "tn":
        a_spec = pl.BlockSpec((tk, tm), lambda j, i, k: (k, i))
    else:
        a_spec = pl.BlockSpec((tm, tk), lambda j, i, k: (i, k))
    if mode == "nt":
        b_spec = pl.BlockSpec((tn, tk), lambda j, i, k: (j, k))
    else:
        b_spec = pl.BlockSpec((tk, tn), lambda j, i, k: (k, j))
    return pl.pallas_call(
        body, name=name,
        out_shape=jax.ShapeDtypeStruct((m, n), out_dtype),
        grid=(n // tn, m // tm, nk),
        in_specs=[a_spec, b_spec],
        out_specs=pl.BlockSpec((tm, tn), lambda j, i, k: (i, j)),
        scratch_shapes=[pltpu.VMEM((tm, tn), F32)],
        compiler_params=_params(("parallel", "parallel", "arbitrary")),
    )(a, b)


def _rmsnorm_fwd(x, g, name):
    lp, d = x.shape

    def body(x_ref, g_ref, o_ref):
        xv = x_ref[...]
        r = lax.rsqrt(jnp.mean(xv * xv, axis=-1, keepdims=True) + EPS)
        o_ref[...] = ((xv * r) * g_ref[...]).astype(BF16)

    return pl.pallas_call(
        body, name=name,
        out_shape=jax.ShapeDtypeStruct((lp, d), BF16),
        grid=(lp // BLK,),
        in_specs=[pl.BlockSpec((BLK, d), lambda i: (i, 0)), pl.BlockSpec((1, d), lambda i: (0, 0))],
        out_specs=pl.BlockSpec((BLK, d), lambda i: (i, 0)),
        compiler_params=_params(("parallel",)),
    )(x, g)


def _rmsnorm_bwd(x, g, dy, resid, out_dtype, name):
    lp, d = x.shape
    has_resid = resid is not None

    def body(*refs):
        if has_resid:
            x_ref, g_ref, dy_ref, r_ref, dx_ref, dg_ref = refs
        else:
            x_ref, g_ref, dy_ref, dx_ref, dg_ref = refs
        i = pl.program_id(0)
        xv = x_ref[...]
        dyv = dy_ref[...].astype(F32)
        r = lax.rsqrt(jnp.mean(xv * xv, axis=-1, keepdims=True) + EPS)
        xh = xv * r
        dyg = dyv * g_ref[...]
        dx = r * (dyg - xh * jnp.mean(dyg * xh, axis=-1, keepdims=True))
        if has_resid:
            dx = dx + r_ref[...]
        dx_ref[...] = dx.astype(out_dtype)

        @pl.when(i == 0)
        def _():
            dg_ref[...] = jnp.zeros_like(dg_ref)

        dg_ref[...] += jnp.sum(dyv * xh, axis=0, keepdims=True)

    row = pl.BlockSpec((BLK, d), lambda i: (i, 0))
    vec = pl.BlockSpec((1, d), lambda i: (0, 0))
    ins = [x, g, dy] + ([resid] if has_resid else [])
    in_specs = [row, vec, row] + ([row] if has_resid else [])
    return pl.pallas_call(
        body, name=name,
        out_shape=(jax.ShapeDtypeStruct((lp, d), out_dtype), jax.ShapeDtypeStruct((1, d), F32)),
        grid=(lp // BLK,),
        in_specs=in_specs,
        out_specs=(row, vec),
        compiler_params=_params(("arbitrary",)),
    )(*ins)


def _forget_fwd(fpre, b_pad):
    lp = fpre.shape[0]

    def body(f_ref, b_ref, c_ref, carry_ref):
        i = pl.program_id(0)

        @pl.when(i == 0)
        def _():
            carry_ref[...] = jnp.zeros_like(carry_ref)

        logf, _ = _log_sigmoid_parts(f_ref[...] + b_ref[...])
        row = lax.broadcasted_iota(jnp.int32, (BLK, BLK), 0)
        col = lax.broadcasted_iota(jnp.int32, (BLK, BLK), 1)
        tri = (col <= row).astype(BF16)
        p0 = logf.astype(BF16)
        r1 = logf - p0.astype(F32)
        p1 = r1.astype(BF16)
        p2 = (r1 - p1.astype(F32)).astype(BF16)
        c = _dot(tri, p0, NN) + _dot(tri, p1, NN) + _dot(tri, p2, NN) + carry_ref[0:1, :]
        c_ref[...] = c
        carry_ref[...] = jnp.broadcast_to(c[BLK - 1:BLK, :], carry_ref.shape)

    return pl.pallas_call(
        body, name="forget_fwd",
        out_shape=jax.ShapeDtypeStruct((lp, LANES), F32),
        grid=(lp // BLK,),
        in_specs=[pl.BlockSpec((BLK, LANES), lambda i: (i, 0)), pl.BlockSpec((1, LANES), lambda i: (0, 0))],
        out_specs=pl.BlockSpec((BLK, LANES), lambda i: (i, 0)),
        scratch_shapes=[pltpu.VMEM((SUBLANES, LANES), F32)],
        compiler_params=_params(("arbitrary",)),
    )(fpre, b_pad)


def _forget_bwd(dc, fpre, b_pad):
    lp = fpre.shape[0]
    nb = lp // BLK

    def body(dc_ref, f_ref, b_ref, df_ref, db_ref, carry_ref):
        i = pl.program_id(0)

        @pl.when(i == 0)
        def _():
            carry_ref[...] = jnp.zeros_like(carry_ref)
            db_ref[...] = jnp.zeros_like(db_ref)

        dcv = dc_ref[...]
        row = lax.broadcasted_iota(jnp.int32, (BLK, BLK), 0)
        col = lax.broadcasted_iota(jnp.int32, (BLK, BLK), 1)
        tri = (col >= row).astype(BF16)
        p0 = dcv.astype(BF16)
        r1 = dcv - p0.astype(F32)
        p1 = r1.astype(BF16)
        p2 = (r1 - p1.astype(F32)).astype(BF16)
        dlogf = _dot(tri, p0, NN) + _dot(tri, p1, NN) + _dot(tri, p2, NN) + carry_ref[0:1, :]
        carry_ref[...] = jnp.broadcast_to(dlogf[0:1, :], carry_ref.shape)
        _, ls_neg = _log_sigmoid_parts(f_ref[...] + b_ref[...])
        df = dlogf * jnp.exp(ls_neg)
        df_ref[...] = df
        db_ref[...] += jnp.sum(df, axis=0, keepdims=True)

    rev = pl.BlockSpec((BLK, LANES), lambda i: (nb - 1 - i, 0))
    vec = pl.BlockSpec((1, LANES), lambda i: (0, 0))
    return pl.pallas_call(
        body, name="forget_bwd",
        out_shape=(jax.ShapeDtypeStruct((lp, LANES), F32), jax.ShapeDtypeStruct((1, LANES), F32)),
        grid=(nb,),
        in_specs=[rev, rev, vec],
        out_specs=(rev, vec),
        scratch_shapes=[pltpu.VMEM((SUBLANES, LANES), F32)],
        compiler_params=_params(("arbitrary",)),
    )(dc, fpre, b_pad)


def _mix_fwd(o_a, o_b, gates, h0, w_fox, w_sb, w_out, g1):
    lp, d = h0.shape

    def body(oa_ref, ob_ref, ga_ref, gb_ref, h_ref, wf_ref, ws_ref, wo_ref, g_ref,
             h1_ref, ya_ref, yb_ref, gated_ref, mixed_ref):
        ya = _dot(oa_ref[...].astype(BF16), wf_ref[...], NN)
        yb = _dot(ob_ref[...].astype(BF16), ws_ref[...], NN)
        gated = _sigmoid(ga_ref[...]) * ya + _sigmoid(gb_ref[...]) * yb
        gb16 = gated.astype(BF16)
        mixed = _dot(gb16, wo_ref[...], NN)
        r = lax.rsqrt(jnp.mean(mixed * mixed, axis=-1, keepdims=True) + EPS)
        h1_ref[...] = h_ref[...] + (mixed * r) * g_ref[...]
        ya_ref[...] = ya
        yb_ref[...] = yb
        gated_ref[...] = gb16
        mixed_ref[...] = mixed

    row_w = pl.BlockSpec((BLK, WIDTH), lambda i: (i, 0))
    row_d = pl.BlockSpec((BLK, d), lambda i: (i, 0))
    full = lambda s: pl.BlockSpec(s, lambda i: (0, 0))
    return pl.pallas_call(
        body, name="mix_fwd",
        out_shape=(jax.ShapeDtypeStruct((lp, d), F32), jax.ShapeDtypeStruct((lp, d), F32),
                   jax.ShapeDtypeStruct((lp, d), F32), jax.ShapeDtypeStruct((lp, d), BF16),
                   jax.ShapeDtypeStruct((lp, d), F32)),
        grid=(lp // BLK,),
        in_specs=[row_w, row_w, row_d, pl.BlockSpec((BLK, d), lambda i: (i, 1)), row_d,
                  full((WIDTH, d)), full((WIDTH, d)), full((d, d)), full((1, d))],
        out_specs=(row_d, row_d, row_d, row_d, row_d),
        compiler_params=_params(("parallel",)),
    )(o_a, o_b, gates, gates, h0, w_fox, w_sb, w_out, g1)


def _gate_bwd(d_gated, gates, ya, yb):
    lp, d = d_gated.shape

    def body(dg_ref, ga_ref, gb_ref, ya_ref, yb_ref, dya_ref, dyb_ref, dga_ref, dgb_ref):
        dg = dg_ref[...]
        sa = _sigmoid(ga_ref[...])
        sb = _sigmoid(gb_ref[...])
        dya_ref[...] = (dg * sa).astype(BF16)
        dyb_ref[...] = (dg * sb).astype(BF16)
        dga_ref[...] = (dg * ya_ref[...] * (sa * (1.0 - sa))).astype(BF16)
        dgb_ref[...] = (dg * yb_ref[...] * (sb * (1.0 - sb))).astype(BF16)

    row = pl.BlockSpec((BLK, d), lambda i: (i, 0))
    out = jax.ShapeDtypeStruct((lp, d), BF16)
    return pl.pallas_call(
        body, name="gate_bwd",
        out_shape=(out, out, out, out),
        grid=(lp // BLK,),
        in_specs=[row, row, pl.BlockSpec((BLK, d), lambda i: (i, 1)), row, row],
        out_specs=(row, row, row, row),
        compiler_params=_params(("parallel",)),
    )(d_gated, gates, gates, ya, yb)


def _shift_down(cur, prev, n):
    rolled = pltpu.roll(cur, n, 0)
    row = lax.broadcasted_iota(jnp.int32, cur.shape, 0)
    for t in range(n):
        rolled = jnp.where(row == t, prev[SUBLANES - n + t:SUBLANES - n + t + 1, :], rolled)
    return rolled


def _shift_up(cur, nxt, n):
    rows = cur.shape[0]
    rolled = pltpu.roll(cur, rows - n, 0)
    row = lax.broadcasted_iota(jnp.int32, cur.shape, 0)
    for t in range(n):
        rolled = jnp.where(row == rows - n + t, nxt[t:t + 1, :], rolled)
    return rolled


def _gelu(x):
    return 0.5 * x * (1.0 + jnp.tanh(GELU_C * (x + GELU_A * (x * x * x))))


def _gelu_and_grad(x):
    t = jnp.tanh(GELU_C * (x + GELU_A * (x * x * x)))
    half = 0.5 * (1.0 + t)
    return x * half, half + 0.5 * x * (1.0 - t * t) * (GELU_C * (1.0 + 3.0 * GELU_A * (x * x)))


def _conv_taps(cur, prev, w_ref, b_ref):
    s1 = _shift_down(cur, prev, 1)
    s2 = _shift_down(cur, prev, 2)
    u = b_ref[...] + w_ref[0:1, :] * s2
    u = u + w_ref[1:2, :] * s1
    u = u + w_ref[2:3, :] * cur
    return u, s1, s2


def _conv_gelu_fwd(up, conv_w, conv_b, tc):
    lp, f2 = up.shape
    rb = BLK // SUBLANES

    def body(u_ref, p_ref, w_ref, b_ref, act_ref):
        i = pl.program_id(0)
        keep = (i > 0).astype(F32)
        u, _, _ = _conv_taps(u_ref[...], p_ref[...] * keep, w_ref, b_ref)
        act_ref[...] = (_gelu(u[:, :tc]) * u[:, tc:]).astype(BF16)

    prev_row = lambda i: jnp.maximum(i * rb - 1, 0)
    return pl.pallas_call(
        body, name="conv_gelu_fwd",
        out_shape=jax.ShapeDtypeStruct((lp, f2 // 2), BF16),
        grid=(lp // BLK, f2 // (2 * tc)),
        in_specs=[pl.BlockSpec((BLK, 2 * tc), lambda i, j: (i, j)),
                  pl.BlockSpec((SUBLANES, 2 * tc), lambda i, j: (prev_row(i), j)),
                  pl.BlockSpec((3, 2 * tc), lambda i, j: (0, j)),
                  pl.BlockSpec((1, 2 * tc), lambda i, j: (0, j))],
        out_specs=pl.BlockSpec((BLK, tc), lambda i, j: (i, j)),
        compiler_params=_params(("parallel", "parallel")),
    )(up, up, conv_w, conv_b)


def _conv_gelu_bwd(up, d_act, conv_w, conv_b, tc):
    lp, f2 = up.shape
    nb = lp // BLK
    rb = BLK // SUBLANES

    def du_of(u, da):
        gel, grad = _gelu_and_grad(u[:, :tc])
        return jnp.concatenate([da * u[:, tc:] * grad, da * gel], axis=1)

    def body(u_ref, p_ref, n_ref, da_ref, dan_ref, w_ref, b_ref, dup_ref, dcw_ref, dcb_ref):
        i = pl.program_id(1)
        cur = u_ref[...]
        u, s1, s2 = _conv_taps(cur, p_ref[...] * (i > 0).astype(F32), w_ref, b_ref)
        du = du_of(u, da_ref[...])
        u_next, _, _ = _conv_taps(n_ref[...], cur[BLK - SUBLANES:BLK, :], w_ref, b_ref)
        du_next = du_of(u_next, dan_ref[...]) * (i < nb - 1).astype(F32)
        n1 = _shift_up(du, du_next, 1)
        n2 = _shift_up(du, du_next, 2)
        dup_ref[...] = (w_ref[2:3, :] * du + w_ref[1:2, :] * n1 + w_ref[0:1, :] * n2).astype(BF16)

        @pl.when(i == 0)
        def _():
            dcw_ref[...] = jnp.zeros_like(dcw_ref)
            dcb_ref[...] = jnp.zeros_like(dcb_ref)

        dcw_ref[0:1, :] += jnp.sum(du * s2, axis=0, keepdims=True)
        dcw_ref[1:2, :] += jnp.sum(du * s1, axis=0, keepdims=True)
        dcw_ref[2:3, :] += jnp.sum(du * cur, axis=0, keepdims=True)
        dcb_ref[...] += jnp.sum(du, axis=0, keepdims=True)

    prev_row = lambda i: jnp.maximum(i * rb - 1, 0)
    next_row = lambda i: jnp.minimum((i + 1) * rb, nb * rb - 1)
    return pl.pallas_call(
        body, name="conv_gelu_bwd",
        out_shape=(jax.ShapeDtypeStruct((lp, f2), BF16), jax.ShapeDtypeStruct((3, f2), F32),
                   jax.ShapeDtypeStruct((1, f2), F32)),
        grid=(f2 // (2 * tc), nb),
        in_specs=[pl.BlockSpec((BLK, 2 * tc), lambda j, i: (i, j)),
                  pl.BlockSpec((SUBLANES, 2 * tc), lambda j, i: (prev_row(i), j)),
                  pl.BlockSpec((SUBLANES, 2 * tc), lambda j, i: (next_row(i), j)),
                  pl.BlockSpec((BLK, tc), lambda j, i: (i, j)),
                  pl.BlockSpec((SUBLANES, tc), lambda j, i: (next_row(i), j)),
                  pl.BlockSpec((3, 2 * tc), lambda j, i: (0, j)),
                  pl.BlockSpec((1, 2 * tc), lambda j, i: (0, j))],
        out_specs=(pl.BlockSpec((BLK, 2 * tc), lambda j, i: (i, j)),
                   pl.BlockSpec((3, 2 * tc), lambda j, i: (0, j)),
                   pl.BlockSpec((1, 2 * tc), lambda j, i: (0, j))),
        compiler_params=_params(("parallel", "arbitrary")),
    )(up, up, up, d_act, d_act, conv_w, conv_b)


def _out_loss(h1, ffn, g3, target, n_valid):
    lp, d = h1.shape

    def body(h_ref, f_ref, g_ref, t_ref, dy_ref, loss_ref):
        i = pl.program_id(0)

        @pl.when(i == 0)
        def _():
            loss_ref[...] = jnp.zeros_like(loss_ref)

        fv = f_ref[...]
        r = lax.rsqrt(jnp.mean(fv * fv, axis=-1, keepdims=True) + EPS)
        y = h_ref[...] + (fv * r) * g_ref[...]
        row = i * BLK + lax.broadcasted_iota(jnp.int32, (BLK, 1), 0)
        valid = (row >= N_META) & (row < n_valid)
        diff = jnp.where(valid, y - t_ref[...], 0.0)
        dy_ref[...] = diff * (1.0 / d)
        per_row = jnp.mean(diff * diff, axis=-1, keepdims=True)
        loss_ref[...] += 0.5 * jnp.sum(per_row, axis=0, keepdims=True)

    row_d = pl.BlockSpec((BLK, d), lambda i: (i, 0))
    return pl.pallas_call(
        body, name="out_loss",
        out_shape=(jax.ShapeDtypeStruct((lp, d), F32), jax.ShapeDtypeStruct((SUBLANES, LANES), F32)),
        grid=(lp // BLK,),
        in_specs=[row_d, row_d, pl.BlockSpec((1, d), lambda i: (0, 0)), row_d],
        out_specs=(row_d, pl.BlockSpec((SUBLANES, LANES), lambda i: (0, 0))),
        compiler_params=_params(("arbitrary",)),
    )(h1, ffn, g3, target)


def _head_masks(rows=BLK):
    lane = lax.broadcasted_iota(jnp.int32, (rows, LANES), 1)
    return [lane < HEAD_DIM, lane >= HEAD_DIM]


def _att_specs(base, lp):
    q_spec = pl.BlockSpec((BLK, LANES), lambda p, i: (i, base + p))
    k_spec = pl.BlockSpec((lp, LANES), lambda p, i: (0, base + N_PAIRS + p))
    v_spec = pl.BlockSpec((lp, LANES), lambda p, i: (0, base + 2 * N_PAIRS + p))
    return q_spec, k_spec, v_spec


def _kv_rows(j, nb=1):
    return pl.ds(pl.multiple_of(j * BLK, nb * BLK), nb * BLK)


def _walk_kv(i, tile, reverse=False, first_pair=0, more=None):
    pairs, odd = i // 2, i % 2

    def wide(t, carry):
        tile(2 * (pairs - 1 - t) if reverse else 2 * t, 2, False)
        return carry

    def single():
        @pl.when(odd == 1)
        def _():
            tile(i - 1, 1, False)

    if reverse:
        tile(i, 1, True)
        single()
        if more is None:
            lax.fori_loop(0, pairs, wide, 0)
        else:
            lax.while_loop(lambda c: (c[0] < pairs) & c[1], lambda c: (wide(c[0], c[0]) + 1, more()), (0, more()))
    else:
        lax.fori_loop(first_pair, pairs, wide, 0)
        single()
        tile(i, 1, True)


def _walk_kv_ahead(i, lead, follow, kept):
    pairs, odd = i // 2, i % 2

    def keep(values):
        for ref, value in zip(kept, values):
            ref[...] = value

    @pl.when(pairs > 0)
    def _():
        keep(lead(0, 2))

    def wide(t, carry):
        ahead = lead(2 * jnp.minimum(t + 1, pairs - 1), 2)
        follow([ref[...] for ref in kept], 2 * t, 2, False)
        keep(ahead)
        return carry

    lax.fori_loop(0, pairs, wide, 0)

    @pl.when(odd == 1)
    def _():
        follow(lead(i - 1, 1), i - 1, 1, False)

    follow(lead(i, 1), i, 1, True)


def _bf16_pieces(x):
    rnd = lambda a: lax.reduce_precision(a, exponent_bits=8, mantissa_bits=7)
    p0 = rnd(x)
    p1 = rnd(x - p0)
    p2 = rnd(x - p0 - p1)
    return [p0.astype(BF16), p1.astype(BF16), p2.astype(BF16)]


def _aug_lanes(cols):
    lp = cols[0].shape[0]
    vals = jnp.stack([c.astype(BF16) for c in cols], axis=-1)
    vals = vals.reshape(lp, N_PAIRS, 2, len(cols))[:, :, ::-1, :]
    vals = jnp.pad(vals, ((0, 0), (0, 0), (0, 0), (0, HEAD_DIM - len(cols))))
    return vals.reshape(lp, WIDTH)


N_AUG = 3


def _riding(exchange, n_in, n_out, n_scratch, grid):
    n = exchange.n if exchange else 0

    def split(refs):
        own_in, ex_in = refs[:n_in], refs[n_in:n_in + n]
        own_out, ex_out = refs[n_in + n:n_in + n + n_out], refs[n_in + n + n_out:n_in + 2 * n + n_out]
        rest = refs[n_in + 2 * n + n_out:]
        return own_in + own_out + rest[:n_scratch], (ex_in, ex_out, rest[n_scratch:])

    def first_step(ex_refs):
        if exchange:
            @pl.when((pl.program_id(0) == 0) & (pl.program_id(1) == 0))
            def _():
                exchange.start(*ex_refs)

    def last_step(ex_refs):
        if exchange:
            @pl.when((pl.program_id(0) == grid[0] - 1) & (pl.program_id(1) == grid[1] - 1))
            def _():
                exchange.wait(*ex_refs)

    return split, first_step, last_step


def _fox_fwd(qkv, qaug, kaug, exchange=None):
    lp = qkv.shape[0]
    nq = lp // BLK
    split, first_step, last_step = _riding(exchange, 5, 2, 3, (N_PAIRS, nq))

    def body(*refs):
        (q_ref, k_ref, v_ref, qa_ref, ka_ref, o_ref, lse_ref, acc_ref, m_ref, s_ref), ex_refs = split(refs)
        first_step(ex_refs)
        i = pl.program_id(1)
        hs = range(2)
        masks = _head_masks()
        qs = q_ref[...] * ATT_SCALE
        qa = qa_ref[...]
        qh = [jnp.where(masks[hh], qs, qa) for hh in hs]
        acc_ref[...] = jnp.zeros_like(acc_ref)
        m_ref[...] = jnp.full_like(m_ref, -1e30)
        row = lax.broadcasted_iota(jnp.int32, (BLK, BLK), 0)
        col = lax.broadcasted_iota(jnp.int32, (BLK, BLK), 1)
        causal = col <= row

        def scores(j, nb):
            rows = _kv_rows(j, nb)
            kmasks = _head_masks(nb * BLK)
            k, ka = k_ref[rows, :], ka_ref[rows, :]
            return [_dot(qh[hh], jnp.where(kmasks[hh], k, ka), NT) for hh in hs]

        def absorb(s, j, nb, diag):
            v = v_ref[_kv_rows(j, nb), :]
            kmasks = _head_masks(nb * BLK)
            vh = [jnp.where(kmasks[hh], v, jnp.ones_like(v)) for hh in hs]
            if diag:
                s = [jnp.where(causal, s[hh], -1e30) for hh in hs]
            m_prev = [m_ref[hh] for hh in hs]
            m_new = [jnp.maximum(m_prev[hh], jnp.max(s[hh], axis=-1, keepdims=True)) for hh in hs]
            p = [jnp.exp(s[hh] - m_new[hh]).astype(BF16) for hh in hs]
            for hh in hs:
                acc_ref[hh] = jnp.exp(m_prev[hh] - m_new[hh]) * acc_ref[hh] + _dot(p[hh], vh[hh], NN)
                m_ref[hh] = m_new[hh]

        _walk_kv_ahead(i, scores, absorb, [s_ref.at[hh] for hh in hs])
        acc = [acc_ref[hh] for hh in hs]
        denom = [acc[0][:, HEAD_DIM:HEAD_DIM + 1], acc[1][:, 0:1]]
        o_ref[...] = jnp.where(masks[0], acc[0] / denom[0], acc[1] / denom[1])
        lse_ref[...] = jnp.where(masks[0], m_ref[0] + jnp.log(denom[0]), m_ref[1] + jnp.log(denom[1]))
        last_step(ex_refs)

    q_spec, k_spec, v_spec = _att_specs(0, lp)
    blk = pl.BlockSpec((BLK, LANES), lambda p, i: (i, p))
    col_full = pl.BlockSpec((lp, LANES), lambda p, i: (0, p))
    out = jax.ShapeDtypeStruct((lp, WIDTH), F32)
    ex = exchange
    return pl.pallas_call(
        body, name="fox_fwd",
        out_shape=(out, out) + tuple(ex.out_shapes if ex else ()),
        grid=(N_PAIRS, nq),
        in_specs=[q_spec, k_spec, v_spec, blk, col_full] + (ex.any_specs if ex else []),
        out_specs=(blk, blk) + tuple(ex.any_specs if ex else ()),
        scratch_shapes=[pltpu.VMEM((2, BLK, LANES), F32), pltpu.VMEM((2, BLK, 1), F32),
                        pltpu.VMEM((2, BLK, 2 * BLK), F32)] + (ex.scratch if ex else []),
        compiler_params=_params(("arbitrary", "arbitrary") if ex else ("parallel", "parallel")),
    )(qkv, qkv, qkv, qaug, kaug, *(ex.arrays if ex else []))


def _head_dots(a, b):
    lp = a.shape[0]

    def body(a_ref, b_ref, o_ref):
        lane = lax.broadcasted_iota(jnp.int32, (BLK, LANES), 1)
        out = jnp.zeros((BLK, LANES), F32)
        for p in range(N_PAIRS):
            cols = slice(p * LANES, (p + 1) * LANES)
            prod = a_ref[:, cols] * b_ref[:, cols]
            for hh in range(2):
                part = jnp.where((lane >= HEAD_DIM) == (hh == 1), prod, 0.0)
                out = jnp.where(lane == 2 * p + hh, jnp.sum(part, axis=-1, keepdims=True), out)
        o_ref[...] = out

    row = pl.BlockSpec((BLK, WIDTH), lambda i: (i, 0))
    return pl.pallas_call(
        body, name="head_dots",
        out_shape=jax.ShapeDtypeStruct((lp, LANES), F32),
        grid=(lp // BLK,),
        in_specs=[row, row],
        out_specs=pl.BlockSpec((BLK, LANES), lambda i: (i, 0)),
        compiler_params=_params(("parallel",)),
    )(a, b)


def _fox_bwd(qkv, qaug, kaug, doaug, d_o, exchange=None):
    lp = qkv.shape[0]
    nq = lp // BLK
    split, first_step, last_step = _riding(exchange, 7, 5, 2, (N_PAIRS, nq))

    def body(*refs):
        (q_ref, k_ref, v_ref, qa_ref, ka_ref, da_ref, do_ref,
         dq_ref, dk_ref, dv_ref, dc_ref, dr_ref, acc_ref, rs_ref), ex_refs = split(refs)
        first_step(ex_refs)
        i = pl.program_id(1)
        hs = range(2)
        masks = _head_masks()
        qs = q_ref[...] * ATT_SCALE
        qa = qa_ref[...]
        qm = [jnp.where(masks[hh], qs, 0) for hh in hs]
        qh = [jnp.where(masks[hh], qs, qa) for hh in hs]
        dov = do_ref[...].astype(BF16)
        doa = da_ref[...]
        dom = [jnp.where(masks[hh], dov, 0) for hh in hs]
        doh = [jnp.where(masks[hh], dov, doa) for hh in hs]
        row = lax.broadcasted_iota(jnp.int32, (BLK, BLK), 0)
        col = lax.broadcasted_iota(jnp.int32, (BLK, BLK), 1)
        causal = col <= row

        @pl.when(i == 0)
        def _():
            dk_ref[...] = jnp.zeros_like(dk_ref)
            dv_ref[...] = jnp.zeros_like(dv_ref)
            dc_ref[...] = jnp.zeros_like(dc_ref)

        acc_ref[...] = jnp.zeros_like(acc_ref)
        rs_ref[...] = jnp.zeros_like(rs_ref)

        def lead(j, nb):
            rows = _kv_rows(j, nb)
            kmasks = _head_masks(nb * BLK)
            k, ka, v = k_ref[rows, :], ka_ref[rows, :], v_ref[rows, :]
            ones = (lax.broadcasted_iota(jnp.int32, v.shape, 1) % HEAD_DIM < N_AUG).astype(BF16)
            logp = [_dot(qh[hh], jnp.where(kmasks[hh], k, ka), NT) for hh in hs]
            dp = [_dot(doh[hh], jnp.where(kmasks[hh], v, ones), NT) for hh in hs]
            return logp + dp

        def follow(lead_out, j, nb, diag):
            rows = _kv_rows(j, nb)
            k = k_ref[rows, :]
            logp, dp = lead_out[:2], lead_out[2:]
            p = [jnp.exp(logp[hh]) for hh in hs]
            if diag:
                p = [jnp.where(causal, p[hh], 0.0) for hh in hs]
            ds = [p[hh] * dp[hh] for hh in hs]
            dsb = [ds[hh].astype(BF16) for hh in hs]
            for hh in hs:
                acc_ref[hh] += _dot(dsb[hh], k, NN)
                col_sums = jnp.sum(ds[hh], axis=0, keepdims=True)
                for b in range(nb):
                    dc_ref[0, j + b, hh:hh + 1, :] -= col_sums[:, b * BLK:(b + 1) * BLK]
                rs_ref[hh] += jnp.sum(ds[hh], axis=-1, keepdims=True)
            dk_ref[rows, :] += _dot(dsb[0], qm[0], TN) + _dot(dsb[1], qm[1], TN)
            dv_ref[rows, :] += _dot(p[0].astype(BF16), dom[0], TN) + _dot(p[1].astype(BF16), dom[1], TN)

        _walk_kv(i, lambda j, nb, diag: follow(lead(j, nb), j, nb, diag))
        dq_ref[...] = (jnp.where(masks[0], acc_ref[0], acc_ref[1]) * ATT_SCALE).astype(BF16)
        dr_ref[...] = jnp.where(masks[0], rs_ref[0], rs_ref[1])
        last_step(ex_refs)

    q_spec, k_spec, v_spec = _att_specs(0, lp)
    blk = pl.BlockSpec((BLK, LANES), lambda p, i: (i, p))
    col_full = pl.BlockSpec((lp, LANES), lambda p, i: (0, p))
    crow_spec = pl.BlockSpec((1, nq, 2, BLK), lambda p, i: (p, 0, 0, 0))
    ex = exchange
    return pl.pallas_call(
        body, name="fox_bwd",
        out_shape=(jax.ShapeDtypeStruct((lp, WIDTH), BF16), jax.ShapeDtypeStruct((lp, WIDTH), F32),
                   jax.ShapeDtypeStruct((lp, WIDTH), F32), jax.ShapeDtypeStruct((N_PAIRS, nq, 2, BLK), F32),
                   jax.ShapeDtypeStruct((lp, WIDTH), F32)) + tuple(ex.out_shapes if ex else ()),
        grid=(N_PAIRS, nq),
        in_specs=[q_spec, k_spec, v_spec, blk, col_full, blk, blk] + (ex.any_specs if ex else []),
        out_specs=(blk, col_full, col_full, crow_spec, blk) + tuple(ex.any_specs if ex else ()),
        scratch_shapes=[pltpu.VMEM((2, BLK, LANES), F32), pltpu.VMEM((2, BLK, 1), F32)] + (ex.scratch if ex else []),
        compiler_params=_params(("arbitrary", "arbitrary") if ex else ("parallel", "arbitrary")),
    )(qkv, qkv, qkv, qaug, kaug, doaug, d_o, *(ex.arrays if ex else []))


def _sb_scores(z):
    ell = jnp.minimum(z, 0.0) - jnp.log(1.0 + jnp.exp(-jnp.abs(z)))
    return ell, ell - z


def _stacked(tri):
    return jnp.concatenate([tri, tri], axis=0)


def _cumsum_dot(x, tri2):
    hi, lo = _split_bf16(x)
    return _dot(jnp.concatenate([hi, lo], axis=1), tri2, NN)


def _sb_units(qh, k, strict2, causal, nb, diag):
    hs, bs = range(2), range(nb)
    z = [_dot(qh[hh], k, NT) for hh in hs]
    sc = [[_sb_scores(z[hh][:, b * BLK:(b + 1) * BLK]) for b in bs] for hh in hs]
    ell = [[sc[hh][b][0] for b in bs] for hh in hs]
    kap = [[jnp.where(causal, sc[hh][b][1], 0.0) if diag else sc[hh][b][1] for b in bs] for hh in hs]
    later = [[_cumsum_dot(kap[hh][b], strict2) for b in bs] for hh in hs]
    return ell, kap, later


def _row_sum(x):
    return jnp.sum(x, axis=-1, keepdims=True)


def _join(blocks):
    joined = blocks[0] if len(blocks) == 1 else jnp.concatenate(blocks, axis=1)
    return joined.astype(BF16)


def _sb_fwd(qkv):
    lp = qkv.shape[0]
    nq = lp // BLK
    assert nq <= HEAD_DIM

    def body(q_ref, k_ref, v_ref, o_ref, lc_ref, acc_ref, car_ref):
        i = pl.program_id(1)
        masks = _head_masks()
        qs = q_ref[...] * ATT_SCALE
        qh = [jnp.where(mk, qs, 0).astype(BF16) for mk in masks]
        row = lax.broadcasted_iota(jnp.int32, (BLK, BLK), 0)
        col = lax.broadcasted_iota(jnp.int32, (BLK, BLK), 1)
        lane = lax.broadcasted_iota(jnp.int32, (BLK, LANES), 1)
        causal = col < row
        strict2 = _stacked((row > col).astype(BF16))
        acc_ref[...] = jnp.zeros_like(acc_ref)
        car_ref[...] = jnp.zeros_like(car_ref)
        lc_ref[...] = jnp.full_like(lc_ref, NOT_VISITED)

        def tile(j, nb, diag):
            hs, bs = range(2), range(nb)
            rows = _kv_rows(j, nb)
            k, v = k_ref[rows, :], v_ref[rows, :]
            ell, kap, later = _sb_units(qh, k, strict2, causal, nb, diag)
            car = [[None] * nb for _ in hs]
            for hh in hs:
                run = car_ref[hh]
                for b in reversed(bs):
                    car[hh][b] = run
                    run = run + _row_sum(kap[hh][b])
                car_ref[hh] = run
            if not diag:
                kept = lc_ref[...]
                for hh in hs:
                    for b in bs:
                        kept = jnp.where(lane == j + b + HEAD_DIM * hh, car[hh][b], kept)
                lc_ref[...] = kept
            a = [[jnp.exp(ell[hh][b] + later[hh][b] + car[hh][b]) for b in bs] for hh in hs]
            if diag:
                a = [[jnp.where(causal, a[hh][b], 0.0) for b in bs] for hh in hs]
            for hh in hs:
                acc_ref[hh] += _dot(_join(a[hh]), v, NN)

        _walk_kv(i, tile, reverse=True, more=lambda: jnp.max(car_ref[...]) > EXP_IS_ZERO)
        o_ref[...] = jnp.where(masks[0], acc_ref[0], acc_ref[1])

    q_spec, k_spec, v_spec = _att_specs(3 * N_PAIRS, lp)
    blk = pl.BlockSpec((BLK, LANES), lambda p, i: (i, p))
    out = jax.ShapeDtypeStruct((lp, WIDTH), F32)
    return pl.pallas_call(
        body, name="sb_fwd",
        out_shape=(out, out),
        grid=(N_PAIRS, nq),
        in_specs=[q_spec, k_spec, v_spec],
        out_specs=(blk, blk),
        scratch_shapes=[pltpu.VMEM((2, BLK, LANES), F32), pltpu.VMEM((2, BLK, 1), F32)],
        compiler_params=_params(("parallel", "parallel")),
    )(qkv, qkv, qkv)


def _sb_bwd(qkv, lcar, d_o):
    lp = qkv.shape[0]
    nq = lp // BLK

    def body(q_ref, k_ref, v_ref, lc_ref, do_ref, dq_ref, dk_ref, dv_ref, acc_ref, cg_ref):
        i = pl.program_id(1)
        masks = _head_masks()
        qs = q_ref[...] * ATT_SCALE
        qh = [jnp.where(mk, qs, 0).astype(BF16) for mk in masks]
        dov = do_ref[...]
        doh = [jnp.where(mk, dov, 0.0).astype(BF16) for mk in masks]
        lcv = lc_ref[...]
        lane_row = lax.broadcasted_iota(jnp.int32, (1, LANES), 1)
        alive = jnp.where(jnp.max(lcv, axis=0, keepdims=True) > EXP_IS_ZERO, 1.0, 0.0)
        n_alive = jnp.maximum(jnp.sum(jnp.where(lane_row < HEAD_DIM, alive, 0.0)),
                              jnp.sum(jnp.where(lane_row >= HEAD_DIM, alive, 0.0))).astype(jnp.int32)
        first_pair = jnp.maximum(i - n_alive, 0) // 2
        row = lax.broadcasted_iota(jnp.int32, (BLK, BLK), 0)
        col = lax.broadcasted_iota(jnp.int32, (BLK, BLK), 1)
        lane = lax.broadcasted_iota(jnp.int32, (BLK, LANES), 1)
        causal = col < row
        strict2 = _stacked((row > col).astype(BF16))
        before2 = _stacked((row < col).astype(BF16))

        @pl.when(i == 0)
        def _():
            dk_ref[...] = jnp.zeros_like(dk_ref)
            dv_ref[...] = jnp.zeros_like(dv_ref)

        acc_ref[...] = jnp.zeros_like(acc_ref)
        cg_ref[...] = jnp.zeros_like(cg_ref)

        def tile(j, nb, diag):
            hs, bs = range(2), range(nb)
            rows = _kv_rows(j, nb)
            k, v = k_ref[rows, :], v_ref[rows, :]
            if diag:
                car = [[0.0] for _ in hs]
            else:
                car = [[_row_sum(jnp.where(lane == j + b + HEAD_DIM * hh, lcv, 0.0)) for b in bs] for hh in hs]
            da = [_dot(doh[hh], v, NT) for hh in hs]
            ell, _, later = _sb_units(qh, k, strict2, causal, nb, diag)
            a = [[jnp.exp(ell[hh][b] + later[hh][b] + car[hh][b]) for b in bs] for hh in hs]
            if diag:
                a = [[jnp.where(causal, a[hh][b], 0.0) for b in bs] for hh in hs]
            g = [[da[hh][:, b * BLK:(b + 1) * BLK] * a[hh][b] for b in bs] for hh in hs]
            cg = [[_cumsum_dot(g[hh][b], before2) for b in bs] for hh in hs]
            before = [[None] * nb for _ in hs]
            for hh in hs:
                run = cg_ref[hh]
                for b in bs:
                    before[hh][b] = run
                    run = run + _row_sum(g[hh][b])
                cg_ref[hh] = run
            dz = [[g[hh][b] - jnp.exp(ell[hh][b]) * (g[hh][b] + cg[hh][b] + before[hh][b]) for b in bs] for hh in hs]
            if diag:
                dz = [[jnp.where(causal, dz[hh][b], 0.0) for b in bs] for hh in hs]
            dzb = [_join(dz[hh]) for hh in hs]
            ab = [_join(a[hh]) for hh in hs]
            for hh in hs:
                acc_ref[hh] += _dot(dzb[hh], k, NN)
            dk_ref[rows, :] += _dot(dzb[0], qh[0], TN) + _dot(dzb[1], qh[1], TN)
            dv_ref[rows, :] += _dot(ab[0], doh[0], TN) + _dot(ab[1], doh[1], TN)

        _walk_kv(i, tile, first_pair=first_pair)
        dq_ref[...] = (jnp.where(masks[0], acc_ref[0], acc_ref[1]) * ATT_SCALE).astype(BF16)

    q_spec, k_spec, v_spec = _att_specs(3 * N_PAIRS, lp)
    blk = pl.BlockSpec((BLK, LANES), lambda p, i: (i, p))
    col_full = pl.BlockSpec((lp, LANES), lambda p, i: (0, p))
    return pl.pallas_call(
        body, name="sb_bwd",
        out_shape=(jax.ShapeDtypeStruct((lp, WIDTH), BF16), jax.ShapeDtypeStruct((lp, WIDTH), F32),
                   jax.ShapeDtypeStruct((lp, WIDTH), F32)),
        grid=(N_PAIRS, nq),
        in_specs=[q_spec, k_spec, v_spec, blk, blk],
        out_specs=(blk, col_full, col_full),
        scratch_shapes=[pltpu.VMEM((2, BLK, LANES), F32), pltpu.VMEM((2, BLK, 1), F32)],
        compiler_params=_params(("parallel", "arbitrary")),
    )(qkv, qkv, qkv, lcar, d_o)


def _local_step(x, target, meta, gains, w_in, b_forget, w_fox, w_sb, w_out, w_up, conv_w, conv_b, w_down,
                ffn_block=None, late_weights=None, early_grads=None):
    s, d = x.shape
    n_valid = N_META + s
    lp = -(-n_valid // BLK) * BLK
    pad = lp - n_valid
    nq = lp // BLK

    h0 = jnp.concatenate([meta, x, jnp.zeros((pad, d), F32)], axis=0)
    tgt = jnp.concatenate([jnp.zeros((N_META, d), F32), target, jnp.zeros((pad, d), F32)], axis=0)

    q_a, k_a, v_a, f_a, q_b, k_b, v_b, g_a, g_b = jnp.split(
        w_in, [512, 1024, 1536, 1544, 2056, 2568, 3080, 4104], axis=1)
    w_qkv = jnp.concatenate([q_a, k_a, v_a, q_b, k_b, v_b], axis=1)
    w_gf = jnp.concatenate([g_a, g_b, f_a, jnp.zeros((d, F_PAD - N_HEADS), BF16)], axis=1)
    b_pad = jnp.concatenate([b_forget.reshape(1, N_HEADS), jnp.zeros((1, LANES - N_HEADS), F32)], axis=1)
    g0, g1, g2, g3 = (gains[i:i + 1] for i in range(4))

    xn1 = _rmsnorm_fwd(h0, g0, "norm1_fwd")
    qkv = _mm(xn1, w_qkv, "nn", BF16, "proj_qkv")
    gf = _mm(xn1, w_gf, "nn", F32, "proj_gates")
    fpre = gf[:, 2 * d:2 * d + LANES]
    c = _forget_fwd(fpre, b_pad)[:, :N_HEADS]
    c_pieces = _bf16_pieces(c)
    one = jnp.ones((lp, N_HEADS), BF16)
    kaug = _aug_lanes(3 * [one] + [-x for x in c_pieces] + 3 * [one])
    o_a, lse, *gathered = _fox_fwd(qkv, _aug_lanes(c_pieces + 3 * [one]), kaug,
                                   late_weights[0] if late_weights else None)
    if late_weights:
        w_fox, w_sb, w_out, w_up, conv_w, w_down = late_weights[1](gathered)
    ffn_block = ffn_block or w_up.shape[1] // 2
    o_b, lcar = _sb_fwd(qkv)
    h1, ya, yb, gated, mixed = _mix_fwd(o_a, o_b, gf, h0, w_fox, w_sb, w_out, g1)
    xn3 = _rmsnorm_fwd(h1, g2, "norm3_fwd")
    up = _mm(xn3, w_up, "nn", F32, "ffn_up")
    act = _conv_gelu_fwd(up, conv_w, conv_b, ffn_block)
    ffn = _mm(act, w_down, "nn", F32, "ffn_down")
    dy, loss_acc = _out_loss(h1, ffn, g3, tgt, n_valid)
    loss = loss_acc[0, 0]

    d_ffn, dg3 = _rmsnorm_bwd(ffn, g3, dy, None, BF16, "norm4_bwd")
    d_act = _mm(d_ffn, w_down, "nt", F32, "ffn_down_dx")
    gw_down = _mm(act, d_ffn, "tn", BF16, "ffn_down_dw")
    d_up, g_conv_w, g_conv_b = _conv_gelu_bwd(up, d_act, conv_w, conv_b, ffn_block)
    d_xn3 = _mm(d_up, w_up, "nt", F32, "ffn_up_dx")
    gw_up = _mm(xn3, d_up, "tn", BF16, "ffn_up_dw")
    dh1, dg2 = _rmsnorm_bwd(h1, g2, d_xn3, dy, F32, "norm3_bwd")

    d_mixed, dg1 = _rmsnorm_bwd(mixed, g1, dh1, None, BF16, "norm2_bwd")
    d_gated = _mm(d_mixed, w_out, "nt", F32, "out_dx")
    gw_out = _mm(gated, d_mixed, "tn", BF16, "out_dw")
    d_ya, d_yb, d_ga, d_gb = _gate_bwd(d_gated, gf, ya, yb)
    d_oa = _mm(d_ya, w_fox, "nt", F32, "fox_o_dx")
    gw_fox = _mm(o_a, d_ya, "tn", BF16, "fox_o_dw")
    d_ob = _mm(d_yb, w_sb, "nt", F32, "sb_o_dx")
    gw_sb = _mm(o_b, d_yb, "tn", BF16, "sb_o_dw")
    neg_lse = [-x for x in _bf16_pieces(lse[:, ::HEAD_DIM])]
    neg_dsum = [-x for x in _bf16_pieces(_head_dots(d_oa, o_a)[:, :N_HEADS])]
    qaug = _aug_lanes(c_pieces + 3 * [one] + neg_lse)
    early = {"w_o_fox": gw_fox, "w_o_sb": gw_sb, "w_out": gw_out, "w_up": gw_up, "conv_w": g_conv_w,
             "conv_b": g_conv_b, "w_down": gw_down}
    dq_a, dk_a, dv_a, dcrow, drow, *early_parts = _fox_bwd(
        qkv, qaug, kaug, _aug_lanes(neg_dsum), d_oa, early_grads(early) if early_grads else None)
    dq_b, dk_b, dv_b = _sb_bwd(qkv, lcar, d_ob)
    dc = dcrow.transpose(0, 2, 1, 3).reshape(N_HEADS, lp).T + drow[:, ::HEAD_DIM]
    dc = jnp.concatenate([dc, jnp.zeros((lp, LANES - N_HEADS), F32)], axis=1)
    df, db = _forget_bwd(dc, fpre, b_pad)
    lane = jnp.arange(LANES) < N_HEADS
    df = jnp.where(lane[None, :], df, 0.0)
    d_proj = jnp.concatenate(
        [dq_a, dk_a.astype(BF16), dv_a.astype(BF16), dq_b, dk_b.astype(BF16), dv_b.astype(BF16),
         d_ga, d_gb, df.astype(BF16), jnp.zeros((lp, F_PAD - LANES), BF16)], axis=1)
    w_in_p = jnp.concatenate([w_qkv, w_gf], axis=1)
    d_xn1 = _mm(d_proj, w_in_p, "nt", F32, "proj_dx")
    gw_in_p = _mm(xn1, d_proj, "tn", BF16, "proj_dw")
    dh0, dg0 = _rmsnorm_bwd(h0, g0, d_xn1, dh1, F32, "norm1_bwd")

    qkv_parts = jnp.split(gw_in_p[:, :6 * WIDTH], 6, axis=1)
    gw_in = jnp.concatenate(
        qkv_parts[:3] + [gw_in_p[:, 6 * WIDTH + 2 * d:6 * WIDTH + 2 * d + N_HEADS]] + qkv_parts[3:]
        + [gw_in_p[:, 6 * WIDTH:6 * WIDTH + 2 * d]], axis=1)
    grads = {
        "meta_tokens": dh0[:N_META],
        "norm_gains": jnp.concatenate([dg0, dg1, dg2, dg3], axis=0),
        "w_in": gw_in,
        "b_forget": db[:, :N_HEADS],
        "early_parts": early_parts,
        **early,
    }
    return loss, dh0[N_META:n_valid], grads


MESH_IDS = pl.DeviceIdType.MESH


def _window(ref, kind, idx, rows, cols):
    if kind == "slots":
        return ref.at[idx]
    if kind == "gate_value":
        half = N_DEV // 2
        idx = jnp.where(idx < half, 2 * idx, 2 * (idx - half) + 1)
        kind = "cols"
    if kind == "cols":
        return ref.at[:, pl.ds(pl.multiple_of(idx * cols, cols & -cols), cols)]
    return ref.at[pl.ds(pl.multiple_of(idx * rows, rows & -rows), rows), :]


def _gathered_shape(shape, kind):
    rows, cols = shape
    return {"slots": (N_DEV, rows, cols), "cols": (rows, N_DEV * cols), "gate_value": (rows, N_DEV * cols),
            "rows": (N_DEV * rows, cols)}[kind]


def _all_gather(shards, kinds):
    n = len(shards)

    def body(*refs):
        ins, outs = refs[:n], refs[n:2 * n]
        send_sems, recv_sems, local_sems = refs[2 * n:]
        x, y, c = lax.axis_index("x"), lax.axis_index("y"), lax.axis_index("c")
        me, sibling = (x, y, c), (x, y, 1 - c)
        chips = [(1 - x, y), (x, 1 - y), (1 - x, 1 - y)]

        def part(t, px, py, pc):
            return _window(outs[t], kinds[t], 4 * px + 2 * py + pc, *shards[t].shape)

        def copy(k, t, blk, to, src=None):
            return pltpu.make_async_remote_copy(
                src_ref=part(t, *blk) if src is None else src, dst_ref=part(t, *blk),
                send_sem=send_sems.at[k, t], recv_sem=recv_sems.at[k, t],
                device_id=to, device_id_type=MESH_IDS)

        mine = [pltpu.make_async_copy(ins[t], part(t, *me), local_sems.at[t]) for t in range(n)]
        for cp in mine:
            cp.start()
        first = [copy(0, t, me, sibling, src=ins[t]) for t in range(n)]
        first += [copy(1 + j, t, me, (*chip, c), src=ins[t]) for j, chip in enumerate(chips) for t in range(n)]
        for cp in first:
            cp.start()
        passed = []
        for j, chip in enumerate(chips):
            for t in range(n):
                copy(1 + j, t, (*chip, c), me).wait_recv()
                passed.append(copy(4 + j, t, (*chip, c), sibling))
                passed[-1].start()
        for t in range(n):
            copy(0, t, sibling, me).wait_recv()
        for j, chip in enumerate(chips):
            for t in range(n):
                copy(4 + j, t, (*chip, 1 - c), me).wait_recv()
        for cp in first + passed:
            cp.wait_send()
        for cp in mine:
            cp.wait()

    any_space = pl.BlockSpec(memory_space=pl.ANY)
    return pl.pallas_call(
        body, name="all_gather",
        out_shape=tuple(jax.ShapeDtypeStruct(_gathered_shape(a.shape, k), a.dtype) for a, k in zip(shards, kinds)),
        in_specs=[any_space] * n,
        out_specs=tuple([any_space] * n),
        scratch_shapes=[pltpu.SemaphoreType.DMA((7, n)), pltpu.SemaphoreType.DMA((7, n)),
                        pltpu.SemaphoreType.DMA((n,))],
    )(*shards)


class _Exchange:
    def __init__(self, arrays, kinds, shard_shapes, gather):
        self.arrays, self.kinds, self.shard_shapes, self.gather = list(arrays), list(kinds), list(shard_shapes), gather
        self.n = n = len(self.arrays)
        self.any_specs = [pl.BlockSpec(memory_space=pl.ANY)] * n
        if gather:
            shapes = [_gathered_shape(a.shape, k) for a, k in zip(self.arrays, kinds)]
        else:
            shapes = [(N_DEV,) + tuple(s) for s in shard_shapes]
        self.out_shapes = [jax.ShapeDtypeStruct(s, a.dtype) for s, a in zip(shapes, self.arrays)]
        self.scratch = [pltpu.SemaphoreType.DMA((N_DEV - 1, n)), pltpu.SemaphoreType.DMA((N_DEV - 1, n)),
                        pltpu.SemaphoreType.DMA((n,))]

    def copies(self, ins, outs, sems):
        send_sems, recv_sems, local_sems = sems
        x, y, c = lax.axis_index("x"), lax.axis_index("y"), lax.axis_index("c")
        my = 4 * x + 2 * y + c

        def src(t, receiver):
            if self.gather or self.kinds[t] == "all":
                return ins[t]
            return _window(ins[t], self.kinds[t], receiver, *self.shard_shapes[t])

        def dst(t, sender):
            if self.gather:
                return _window(outs[t], self.kinds[t], sender, *self.shard_shapes[t])
            return outs[t].at[sender]

        local = [pltpu.make_async_copy(src(t, my), dst(t, my), local_sems.at[t]) for t in range(self.n)]
        sends, arrivals = [], []
        for rel in range(1, N_DEV):
            px, py, pc = x ^ (rel >> 2), y ^ ((rel >> 1) & 1), c ^ (rel & 1)
            peer = 4 * px + 2 * py + pc
            for t in range(self.n):
                common = dict(send_sem=send_sems.at[rel - 1, t], recv_sem=recv_sems.at[rel - 1, t],
                              device_id=(px, py, pc), device_id_type=MESH_IDS)
                sends.append(pltpu.make_async_remote_copy(src_ref=src(t, peer), dst_ref=dst(t, my), **common))
                arrivals.append(pltpu.make_async_remote_copy(src_ref=src(t, my), dst_ref=dst(t, peer), **common))
        return local, sends, arrivals

    def start(self, ins, outs, sems):
        local, sends, _ = self.copies(ins, outs, sems)
        for cp in local + sends:
            cp.start()

    def wait(self, ins, outs, sems):
        local, sends, arrivals = self.copies(ins, outs, sems)
        for cp in arrivals:
            cp.wait_recv()
        for cp in sends:
            cp.wait_send()
        for cp in local:
            cp.wait()


def _exchange(grads, kinds, shard_shapes):
    ex = _Exchange(grads, kinds, shard_shapes, gather=False)
    n = ex.n

    def body(*refs):
        ins, outs, sems = refs[:n], refs[n:2 * n], refs[2 * n:]
        ex.start(ins, outs, sems)
        ex.wait(ins, outs, sems)

    return pl.pallas_call(
        body, name="grad_exchange",
        out_shape=tuple(ex.out_shapes),
        in_specs=ex.any_specs,
        out_specs=tuple(ex.any_specs),
        scratch_shapes=ex.scratch,
    )(*grads)


def _sum_adamw(parts, w, m, v, name):
    rows, cols = w.shape
    n, rows_p, cols_p = parts.shape
    tr = _tile(rows, BLK, SUBLANES) if rows > BLK else rows
    tp = tr if rows_p == rows else rows_p
    c1 = 1.0 - ADAM_B1 ** ADAM_STEP
    c2 = 1.0 - ADAM_B2 ** ADAM_STEP

    def body(p_ref, w_ref, m_ref, v_ref, g_ref, d_ref, nm_ref, nv_ref):
        gv = p_ref[0, 0:tr, 0:cols].astype(F32)
        for s in range(1, n):
            gv = gv + p_ref[s, 0:tr, 0:cols].astype(F32)
        g_ref[...] = gv
        nm = ADAM_B1 * m_ref[...] + (1.0 - ADAM_B1) * gv
        nv = ADAM_B2 * v_ref[...] + (1.0 - ADAM_B2) * (gv * gv)
        m_hat = nm / c1
        v_hat = nv / c2
        d_ref[...] = -ADAM_LR * (m_hat / (jnp.sqrt(v_hat) + ADAM_EPS) + ADAM_WD * w_ref[...])
        nm_ref[...] = nm
        nv_ref[...] = nv

    spec = pl.BlockSpec((tr, cols), lambda i: (i, 0))
    out = jax.ShapeDtypeStruct((rows, cols), F32)
    return pl.pallas_call(
        body, name=name,
        out_shape=(out, out, out, out),
        grid=(rows // tr,),
        in_specs=[pl.BlockSpec((n, tp, cols_p), lambda i: (0, i, 0)), spec, spec, spec],
        out_specs=(spec, spec, spec, spec),
        compiler_params=_params(("parallel",)),
    )(parts, w, m, v)


def _pad2(a, rows, cols):
    return jnp.pad(a, ((0, rows - a.shape[0]), (0, cols - a.shape[1])))


WEIGHTS = ["meta_tokens", "norm_gains", "w_in", "b_forget", "w_o_fox", "w_o_sb", "w_out", "w_up", "conv_w",
           "conv_b", "w_down"]


def kernel(x, meta_tokens, norm_gains, w_in, b_forget, w_o_fox, w_o_sb, w_out, w_up, conv_w, conv_b, w_down, loss_target, m_meta_tokens, m_norm_gains, m_w_in, m_b_forget, m_w_o_fox, m_w_o_sb, m_w_out, m_w_up, m_conv_w, m_conv_b, m_w_down, v_meta_tokens, v_norm_gains, v_w_in, v_b_forget, v_w_o_fox, v_w_o_sb, v_w_out, v_w_up, v_conv_w, v_conv_b, v_w_down):
    w = dict(meta_tokens=meta_tokens, norm_gains=norm_gains, w_in=w_in, b_forget=b_forget, w_o_fox=w_o_fox,
             w_o_sb=w_o_sb, w_out=w_out, w_up=w_up, conv_w=conv_w, conv_b=conv_b, w_down=w_down)
    mom = dict(meta_tokens=m_meta_tokens, norm_gains=m_norm_gains, w_in=m_w_in, b_forget=m_b_forget,
               w_o_fox=m_w_o_fox, w_o_sb=m_w_o_sb, w_out=m_w_out, w_up=m_w_up, conv_w=m_conv_w, conv_b=m_conv_b,
               w_down=m_w_down)
    vel = dict(meta_tokens=v_meta_tokens, norm_gains=v_norm_gains, w_in=v_w_in, b_forget=v_b_forget,
               w_o_fox=v_w_o_fox, w_o_sb=v_w_o_sb, w_out=v_w_out, w_up=v_w_up, conv_w=v_conv_w, conv_b=v_conv_b,
               w_down=v_w_down)
    w2 = {n: a.reshape(a.shape[-2:]) for n, a in w.items()}
    shard_shape = {n: a.shape for n, a in w2.items()}

    d = x.shape[-1]
    up_cols = shard_shape["w_up"][1]
    up_pad = -(-up_cols // LANES) * LANES
    half = N_DEV // 2
    pad_rows = lambda a: _pad2(a, SUBLANES, a.shape[1])

    shards = [
        ("w_in", "slots", w2["w_in"].astype(BF16)),
        ("meta_tokens", "cols", w2["meta_tokens"]),
        ("norm_gains", "cols", pad_rows(w2["norm_gains"])),
    ]
    later = [
        ("w_o_fox", "cols", w2["w_o_fox"].astype(BF16)),
        ("w_o_sb", "cols", w2["w_o_sb"].astype(BF16)),
        ("w_out", "rows", w2["w_out"].astype(BF16)),
        ("w_up", "gate_value", _pad2(w2["w_up"], d, up_pad).astype(BF16)),
        ("conv_w", "gate_value", _pad2(w2["conv_w"], SUBLANES, up_pad)),
        ("w_down", "rows", w2["w_down"].astype(BF16)),
    ]
    full = dict(zip([s[0] for s in shards], _all_gather([s[2] for s in shards], [s[1] for s in shards])))
    w_in_full = jnp.concatenate([full["w_in"][i] for i in range(N_DEV)], axis=1)
    conv_b_p = jnp.pad(w2["conv_b"].reshape(2, half, up_cols), ((0, 0), (0, 0), (0, up_pad - up_cols)))
    conv_b_p = conv_b_p.transpose(1, 0, 2)

    def finish_gather(gathered):
        w_fox, w_sb, w_out_full, w_up_p, conv_w_p, w_down_full = gathered
        w_down_p = jnp.pad(w_down_full.reshape(half, up_cols, d), ((0, 0), (0, up_pad - up_cols), (0, 0)))
        return w_fox, w_sb, w_out_full, w_up_p, conv_w_p[:3], w_down_p.reshape(half * up_pad, d)

    early_names = ["w_o_fox", "w_o_sb", "w_out", "w_up", "conv_w", "conv_b", "w_down"]

    def early_exchange(g):
        sends = {
            "w_o_fox": ("cols", g["w_o_fox"], shard_shape["w_o_fox"]),
            "w_o_sb": ("cols", g["w_o_sb"], shard_shape["w_o_sb"]),
            "w_out": ("rows", g["w_out"], shard_shape["w_out"]),
            "w_up": ("gate_value", g["w_up"], (d, up_pad)),
            "conv_w": ("gate_value", pad_rows(g["conv_w"]), (SUBLANES, up_pad)),
            "conv_b": ("all", pad_rows(g["conv_b"].reshape(half, 2, up_pad).transpose(1, 0, 2)[:, :, :up_cols]
                                       .reshape(1, -1)),
                       (SUBLANES, N_DEV * up_cols)),
            "w_down": ("rows", g["w_down"].reshape(half, up_pad, d)[:, :up_cols].reshape(half * up_cols, d),
                       shard_shape["w_down"]),
        }
        return _Exchange([sends[n][1] for n in early_names], [sends[n][0] for n in early_names],
                         [sends[n][2] for n in early_names], gather=False)

    late_weights = (_Exchange([s[2] for s in later], [s[1] for s in later], [s[2].shape for s in later], gather=True),
                    finish_gather)
    loss, grad_x, grads = _local_step(
        x[0], loss_target[0], full["meta_tokens"], full["norm_gains"][:4], w_in_full, w2["b_forget"],
        None, None, None, None, None, conv_b_p.reshape(1, N_DEV * up_pad), None, ffn_block=up_pad,
        late_weights=late_weights, early_grads=early_exchange)
    loss = lax.psum(loss, ("x", "y", "c"))

    in_cols = shard_shape["w_in"][1]
    late_names = ["meta_tokens", "norm_gains", "w_in", "b_forget"]
    sends = {
        "meta_tokens": ("cols", grads["meta_tokens"], (N_META, LANES)),
        "norm_gains": ("cols", pad_rows(grads["norm_gains"]), (SUBLANES, LANES)),
        "w_in": ("slots", jnp.stack([grads["w_in"][:, i * in_cols:(i + 1) * in_cols] for i in range(N_DEV)]),
                 shard_shape["w_in"]),
        "b_forget": ("all", _pad2(grads["b_forget"], SUBLANES, LANES), (SUBLANES, LANES)),
    }
    parts = dict(zip(late_names, _exchange([sends[n][1] for n in late_names], [sends[n][0] for n in late_names],
                                           [sends[n][2] for n in late_names])))
    parts.update(zip(early_names, grads["early_parts"]))

    grad, delta, new_m, new_v = {}, {}, {}, {}
    for n in WEIGHTS:
        shape = w[n].shape
        outs = _sum_adamw(parts[n], w2[n], mom[n].reshape(shard_shape[n]), vel[n].reshape(shard_shape[n]),
                          "adamw_" + n)
        grad[n], delta[n], new_m[n], new_v[n] = (o.reshape(shape) for o in outs)

    return (loss, grad_x[None], *[grad[n] for n in WEIGHTS], *[delta[n] for n in WEIGHTS],
            *[new_m[n] for n in WEIGHTS], *[new_v[n] for n in WEIGHTS])
```

```python
import functools
import math

import jax
import jax.numpy as jnp
from jax import lax
from jax.experimental import pallas as pl
from jax.experimental.pallas import tpu as pltpu

F32 = jnp.float32
BF16 = jnp.bfloat16

N_DEV = 8
N_META = 16
HEAD_DIM = 64
N_HEADS = 8
WIDTH = N_HEADS * HEAD_DIM
N_PAIRS = N_HEADS // 2
LANES = 128
SUBLANES = 8
EPS = 1e-6
ATT_SCALE = HEAD_DIM ** -0.5
BLK = 256
F_PAD = 256
VMEM_LIMIT = 48 << 20

ADAM_LR = 0.001
ADAM_B1 = 0.9
ADAM_B2 = 0.999
ADAM_EPS = 1e-08
ADAM_WD = 0.01
ADAM_STEP = 10

GELU_C = math.sqrt(2.0 / math.pi)
GELU_A = 0.044715
EXP_IS_ZERO = -110.0
NOT_VISITED = -1e30


def _params(sem, vmem=VMEM_LIMIT):
    return pltpu.CompilerParams(dimension_semantics=sem, vmem_limit_bytes=vmem)


def _tile(dim, cap, align=LANES):
    t = (min(cap, dim) // align) * align
    while t >= align:
        if dim % t == 0:
            return t
        t -= align
    return dim


def _log_sigmoid_parts(z):
    lp = jnp.log1p(jnp.exp(-jnp.abs(z)))
    return jnp.minimum(z, 0.0) - lp, jnp.minimum(-z, 0.0) - lp


def _sigmoid(x):
    return 1.0 / (1.0 + jnp.exp(-x))


def _split_bf16(x):
    hi = x.astype(BF16)
    lo = (x - hi.astype(F32)).astype(BF16)
    return hi, lo


def _dot(a, b, dims):
    return lax.dot_general(a, b, (dims, ((), ())), preferred_element_type=F32)


NN = ((1,), (0,))
NT = ((1,), (1,))
TN = ((0,), (0,))


def _mm(a, b, mode, out_dtype, name):
    if mode == "nn":
        (m, kc), (_, n) = a.shape, b.shape
    elif mode == "nt":
        (m, kc), (n, _) = a.shape, b.shape
    else:
        (kc, m), (_, n) = a.shape, b.shape
    tm = _tile(m, 768)
    tn = _tile(n, 1024)
    tk = _tile(kc, 1536 if mode != "tn" else 768)
    nk = kc // tk
    dims = {"nn": NN, "nt": NT, "tn": TN}[mode]

    def body(a_ref, b_ref, o_ref, acc_ref):
        k = pl.program_id(2)

        @pl.when(k == 0)
        def _():
            acc_ref[...] = jnp.zeros_like(acc_ref)

        acc_ref[...] += _dot(a_ref[...].astype(BF16), b_ref[...].astype(BF16), dims)

        @pl.when(k == nk - 1)
        def _():
            o_ref[...] = acc_ref[...].astype(out_dtype)

    if mode == "tn":
        a_spec = pl.BlockSpec((tk, tm), lambda j, i, k: (k, i))
    else:
        a_spec = pl.BlockSpec((tm, tk), lambda j, i, k: (i, k))
    if mode == "nt":
        b_spec = pl.BlockSpec((tn, tk), lambda j, i, k: (j, k))
    else:
        b_spec = pl.BlockSpec((tk, tn), lambda j, i, k: (k, j))
    return pl.pallas_call(
        body, name=name,
        out_shape=jax.ShapeDtypeStruct((m, n), out_dtype),
        grid=(n // tn, m // tm, nk),
        in_specs=[a_spec, b_spec],
        out_specs=pl.BlockSpec((tm, tn), lambda j, i, k: (i, j)),
        scratch_shapes=[pltpu.VMEM((tm, tn), F32)],
        compiler_params=_params(("parallel", "parallel", "arbitrary")),
    )(a, b)


def _rmsnorm_fwd(x, g, name):
    lp, d = x.shape

    def body(x_ref, g_ref, o_ref):
        xv = x_ref[...]
        r = lax.rsqrt(jnp.mean(xv * xv, axis=-1, keepdims=True) + EPS)
        o_ref[...] = ((xv * r) * g_ref[...]).astype(BF16)

    return pl.pallas_call(
        body, name=name,
        out_shape=jax.ShapeDtypeStruct((lp, d), BF16),
        grid=(lp // BLK,),
        in_specs=[pl.BlockSpec((BLK, d), lambda i: (i, 0)), pl.BlockSpec((1, d), lambda i: (0, 0))],
        out_specs=pl.BlockSpec((BLK, d), lambda i: (i, 0)),
        compiler_params=_params(("parallel",)),
    )(x, g)


def _rmsnorm_bwd(x, g, dy, resid, out_dtype, name):
    lp, d = x.shape
    has_resid = resid is not None

    def body(*refs):
        if has_resid:
            x_ref, g_ref, dy_ref, r_ref, dx_ref, dg_ref = refs
        else:
            x_ref, g_ref, dy_ref, dx_ref, dg_ref = refs
        i = pl.program_id(0)
        xv = x_ref[...]
        dyv = dy_ref[...].astype(F32)
        r = lax.rsqrt(jnp.mean(xv * xv, axis=-1, keepdims=True) + EPS)
        xh = xv * r
        dyg = dyv * g_ref[...]
        dx = r * (dyg - xh * jnp.mean(dyg * xh, axis=-1, keepdims=True))
        if has_resid:
            dx = dx + r_ref[...]
        dx_ref[...] = dx.astype(out_dtype)

        @pl.when(i == 0)
        def _():
            dg_ref[...] = jnp.zeros_like(dg_ref)

        dg_ref[...] += jnp.sum(dyv * xh, axis=0, keepdims=True)

    row = pl.BlockSpec((BLK, d), lambda i: (i, 0))
    vec = pl.BlockSpec((1, d), lambda i: (0, 0))
    ins = [x, g, dy] + ([resid] if has_resid else [])
    in_specs = [row, vec, row] + ([row] if has_resid else [])
    return pl.pallas_call(
        body, name=name,
        out_shape=(jax.ShapeDtypeStruct((lp, d), out_dtype), jax.ShapeDtypeStruct((1, d), F32)),
        grid=(lp // BLK,),
        in_specs=in_specs,
        out_specs=(row, vec),
        compiler_params=_params(("arbitrary",)),
    )(*ins)


def _forget_fwd(fpre, b_pad):
    lp = fpre.shape[0]

    def body(f_ref, b_ref, c_ref, carry_ref):
        i = pl.program_id(0)

        @pl.when(i == 0)
        def _():
            carry_ref[...] = jnp.zeros_like(carry_ref)

        logf, _ = _log_sigmoid_parts(f_ref[...] + b_ref[...])
        row = lax.broadcasted_iota(jnp.int32, (BLK, BLK), 0)
        col = lax.broadcasted_iota(jnp.int32, (BLK, BLK), 1)
        tri = (col <= row).astype(BF16)
        p0 = logf.astype(BF16)
        r1 = logf - p0.astype(F32)
        p1 = r1.astype(BF16)
        p2 = (r1 - p1.astype(F32)).astype(BF16)
        c = _dot(tri, p0, NN) + _dot(tri, p1, NN) + _dot(tri, p2, NN) + carry_ref[0:1, :]
        c_ref[...] = c
        carry_ref[...] = jnp.broadcast_to(c[BLK - 1:BLK, :], carry_ref.shape)

    return pl.pallas_call(
        body, name="forget_fwd",
        out_shape=jax.ShapeDtypeStruct((lp, LANES), F32),
        grid=(lp // BLK,),
        in_specs=[pl.BlockSpec((BLK, LANES), lambda i: (i, 0)), pl.BlockSpec((1, LANES), lambda i: (0, 0))],
        out_specs=pl.BlockSpec((BLK, LANES), lambda i: (i, 0)),
        scratch_shapes=[pltpu.VMEM((SUBLANES, LANES), F32)],
        compiler_params=_params(("arbitrary",)),
    )(fpre, b_pad)


def _forget_bwd(dc, fpre, b_pad):
    lp = fpre.shape[0]
    nb = lp // BLK

    def body(dc_ref, f_ref, b_ref, df_ref, db_ref, carry_ref):
        i = pl.program_id(0)

        @pl.when(i == 0)
        def _():
            carry_ref[...] = jnp.zeros_like(carry_ref)
            db_ref[...] = jnp.zeros_like(db_ref)

        dcv = dc_ref[...]
        row = lax.broadcasted_iota(jnp.int32, (BLK, BLK), 0)
        col = lax.broadcasted_iota(jnp.int32, (BLK, BLK), 1)
        tri = (col >= row).astype(BF16)
        p0 = dcv.astype(BF16)
        r1 = dcv - p0.astype(F32)
        p1 = r1.astype(BF16)
        p2 = (r1 - p1.astype(F32)).astype(BF16)
        dlogf = _dot(tri, p0, NN) + _dot(tri, p1, NN) + _dot(tri, p2, NN) + carry_ref[0:1, :]
        carry_ref[...] = jnp.broadcast_to(dlogf[0:1, :], carry_ref.shape)
        _, ls_neg = _log_sigmoid_parts(f_ref[...] + b_ref[...])
        df = dlogf * jnp.exp(ls_neg)
        df_ref[...] = df
        db_ref[...] += jnp.sum(df, axis=0, keepdims=True)

    rev = pl.BlockSpec((BLK, LANES), lambda i: (nb - 1 - i, 0))
    vec = pl.BlockSpec((1, LANES), lambda i: (0, 0))
    return pl.pallas_call(
        body, name="forget_bwd",
        out_shape=(jax.ShapeDtypeStruct((lp, LANES), F32), jax.ShapeDtypeStruct((1, LANES), F32)),
        grid=(nb,),
        in_specs=[rev, rev, vec],
        out_specs=(rev, vec),
        scratch_shapes=[pltpu.VMEM((SUBLANES, LANES), F32)],
        compiler_params=_params(("arbitrary",)),
    )(dc, fpre, b_pad)


def _mix_fwd(o_a, o_b, gates, h0, w_fox, w_sb, w_out, g1):
    lp, d = h0.shape

    def body(oa_ref, ob_ref, ga_ref, gb_ref, h_ref, wf_ref, ws_ref, wo_ref, g_ref,
             h1_ref, ya_ref, yb_ref, gated_ref, mixed_ref):
        ya = _dot(oa_ref[...].astype(BF16), wf_ref[...], NN)
        yb = _dot(ob_ref[...].astype(BF16), ws_ref[...], NN)
        gated = _sigmoid(ga_ref[...]) * ya + _sigmoid(gb_ref[...]) * yb
        gb16 = gated.astype(BF16)
        mixed = _dot(gb16, wo_ref[...], NN)
        r = lax.rsqrt(jnp.mean(mixed * mixed, axis=-1, keepdims=True) + EPS)
        h1_ref[...] = h_ref[...] + (mixed * r) * g_ref[...]
        ya_ref[...] = ya
        yb_ref[...] = yb
        gated_ref[...] = gb16
        mixed_ref[...] = mixed

    row_w = pl.BlockSpec((BLK, WIDTH), lambda i: (i, 0))
    row_d = pl.BlockSpec((BLK, d), lambda i: (i, 0))
    full = lambda s: pl.BlockSpec(s, lambda i: (0, 0))
    return pl.pallas_call(
        body, name="mix_fwd",
        out_shape=(jax.ShapeDtypeStruct((lp, d), F32), jax.ShapeDtypeStruct((lp, d), F32),
                   jax.ShapeDtypeStruct((lp, d), F32), jax.ShapeDtypeStruct((lp, d), BF16),
                   jax.ShapeDtypeStruct((lp, d), F32)),
        grid=(lp // BLK,),
        in_specs=[row_w, row_w, row_d, pl.BlockSpec((BLK, d), lambda i: (i, 1)), row_d,
                  full((WIDTH, d)), full((WIDTH, d)), full((d, d)), full((1, d))],
        out_specs=(row_d, row_d, row_d, row_d, row_d),
        compiler_params=_params(("parallel",)),
    )(o_a, o_b, gates, gates, h0, w_fox, w_sb, w_out, g1)


def _gate_bwd(d_gated, gates, ya, yb):
    lp, d = d_gated.shape

    def body(dg_ref, ga_ref, gb_ref, ya_ref, yb_ref, dya_ref, dyb_ref, dga_ref, dgb_ref):
        dg = dg_ref[...]
        sa = _sigmoid(ga_ref[...])
        sb = _sigmoid(gb_ref[...])
        dya_ref[...] = (dg * sa).astype(BF16)
        dyb_ref[...] = (dg * sb).astype(BF16)
        dga_ref[...] = (dg * ya_ref[...] * (sa * (1.0 - sa))).astype(BF16)
        dgb_ref[...] = (dg * yb_ref[...] * (sb * (1.0 - sb))).astype(BF16)

    row = pl.BlockSpec((BLK, d), lambda i: (i, 0))
    out = jax.ShapeDtypeStruct((lp, d), BF16)
    return pl.pallas_call(
        body, name="gate_bwd",
        out_shape=(out, out, out, out),
        grid=(lp // BLK,),
        in_specs=[row, row, pl.BlockSpec((BLK, d), lambda i: (i, 1)), row, row],
        out_specs=(row, row, row, row),
        compiler_params=_params(("parallel",)),
    )(d_gated, gates, gates, ya, yb)


def _shift_down(cur, prev, n):
    rolled = pltpu.roll(cur, n, 0)
    row = lax.broadcasted_iota(jnp.int32, cur.shape, 0)
    for t in range(n):
        rolled = jnp.where(row == t, prev[SUBLANES - n + t:SUBLANES - n + t + 1, :], rolled)
    return rolled


def _shift_up(cur, nxt, n):
    rows = cur.shape[0]
    rolled = pltpu.roll(cur, rows - n, 0)
    row = lax.broadcasted_iota(jnp.int32, cur.shape, 0)
    for t in range(n):
        rolled = jnp.where(row == rows - n + t, nxt[t:t + 1, :], rolled)
    return rolled


def _gelu(x):
    return 0.5 * x * (1.0 + jnp.tanh(GELU_C * (x + GELU_A * (x * x * x))))


def _gelu_and_grad(x):
    t = jnp.tanh(GELU_C * (x + GELU_A * (x * x * x)))
    half = 0.5 * (1.0 + t)
    return x * half, half + 0.5 * x * (1.0 - t * t) * (GELU_C * (1.0 + 3.0 * GELU_A * (x * x)))


def _conv_taps(cur, prev, w_ref, b_ref):
    s1 = _shift_down(cur, prev, 1)
    s2 = _shift_down(cur, prev, 2)
    u = b_ref[...] + w_ref[0:1, :] * s2
    u = u + w_ref[1:2, :] * s1
    u = u + w_ref[2:3, :] * cur
    return u, s1, s2


def _conv_gelu_fwd(up, conv_w, conv_b, tc):
    lp, f2 = up.shape
    rb = BLK // SUBLANES

    def body(u_ref, p_ref, w_ref, b_ref, act_ref):
        i = pl.program_id(0)
        keep = (i > 0).astype(F32)
        u, _, _ = _conv_taps(u_ref[...], p_ref[...] * keep, w_ref, b_ref)
        act_ref[...] = (_gelu(u[:, :tc]) * u[:, tc:]).astype(BF16)

    prev_row = lambda i: jnp.maximum(i * rb - 1, 0)
    return pl.pallas_call(
        body, name="conv_gelu_fwd",
        out_shape=jax.ShapeDtypeStruct((lp, f2 // 2), BF16),
        grid=(lp // BLK, f2 // (2 * tc)),
        in_specs=[pl.BlockSpec((BLK, 2 * tc), lambda i, j: (i, j)),
                  pl.BlockSpec((SUBLANES, 2 * tc), lambda i, j: (prev_row(i), j)),
                  pl.BlockSpec((3, 2 * tc), lambda i, j: (0, j)),
                  pl.BlockSpec((1, 2 * tc), lambda i, j: (0, j))],
        out_specs=pl.BlockSpec((BLK, tc), lambda i, j: (i, j)),
        compiler_params=_params(("parallel", "parallel")),
    )(up, up, conv_w, conv_b)


def _conv_gelu_bwd(up, d_act, conv_w, conv_b, tc):
    lp, f2 = up.shape
    nb = lp // BLK
    rb = BLK // SUBLANES

    def du_of(u, da):
        gel, grad = _gelu_and_grad(u[:, :tc])
        return jnp.concatenate([da * u[:, tc:] * grad, da * gel], axis=1)

    def body(u_ref, p_ref, n_ref, da_ref, dan_ref, w_ref, b_ref, dup_ref, dcw_ref, dcb_ref):
        i = pl.program_id(1)
        cur = u_ref[...]
        u, s1, s2 = _conv_taps(cur, p_ref[...] * (i > 0).astype(F32), w_ref, b_ref)
        du = du_of(u, da_ref[...])
        u_next, _, _ = _conv_taps(n_ref[...], cur[BLK - SUBLANES:BLK, :], w_ref, b_ref)
        du_next = du_of(u_next, dan_ref[...]) * (i < nb - 1).astype(F32)
        n1 = _shift_up(du, du_next, 1)
        n2 = _shift_up(du, du_next, 2)
        dup_ref[...] = (w_ref[2:3, :] * du + w_ref[1:2, :] * n1 + w_ref[0:1, :] * n2).astype(BF16)

        @pl.when(i == 0)
        def _():
            dcw_ref[...] = jnp.zeros_like(dcw_ref)
            dcb_ref[...] = jnp.zeros_like(dcb_ref)

        dcw_ref[0:1, :] += jnp.sum(du * s2, axis=0, keepdims=True)
        dcw_ref[1:2, :] += jnp.sum(du * s1, axis=0, keepdims=True)
        dcw_ref[2:3, :] += jnp.sum(du * cur, axis=0, keepdims=True)
        dcb_ref[...] += jnp.sum(du, axis=0, keepdims=True)

    prev_row = lambda i: jnp.maximum(i * rb - 1, 0)
    next_row = lambda i: jnp.minimum((i + 1) * rb, nb * rb - 1)
    return pl.pallas_call(
        body, name="conv_gelu_bwd",
        out_shape=(jax.ShapeDtypeStruct((lp, f2), BF16), jax.ShapeDtypeStruct((3, f2), F32),
                   jax.ShapeDtypeStruct((1, f2), F32)),
        grid=(f2 // (2 * tc), nb),
        in_specs=[pl.BlockSpec((BLK, 2 * tc), lambda j, i: (i, j)),
                  pl.BlockSpec((SUBLANES, 2 * tc), lambda j, i: (prev_row(i), j)),
                  pl.BlockSpec((SUBLANES, 2 * tc), lambda j, i: (next_row(i), j)),
                  pl.BlockSpec((BLK, tc), lambda j, i: (i, j)),
                  pl.BlockSpec((SUBLANES, tc), lambda j, i: (next_row(i), j)),
                  pl.BlockSpec((3, 2 * tc), lambda j, i: (0, j)),
                  pl.BlockSpec((1, 2 * tc), lambda j, i: (0, j))],
        out_specs=(pl.BlockSpec((BLK, 2 * tc), lambda j, i: (i, j)),
                   pl.BlockSpec((3, 2 * tc), lambda j, i: (0, j)),
                   pl.BlockSpec((1, 2 * tc), lambda j, i: (0, j))),
        compiler_params=_params(("parallel", "arbitrary")),
    )(up, up, up, d_act, d_act, conv_w, conv_b)


def _out_loss(h1, ffn, g3, target, n_valid):
    lp, d = h1.shape

    def body(h_ref, f_ref, g_ref, t_ref, dy_ref, loss_ref):
        i = pl.program_id(0)

        @pl.when(i == 0)
        def _():
            loss_ref[...] = jnp.zeros_like(loss_ref)

        fv = f_ref[...]
        r = lax.rsqrt(jnp.mean(fv * fv, axis=-1, keepdims=True) + EPS)
        y = h_ref[...] + (fv * r) * g_ref[...]
        row = i * BLK + lax.broadcasted_iota(jnp.int32, (BLK, 1), 0)
        valid = (row >= N_META) & (row < n_valid)
        diff = jnp.where(valid, y - t_ref[...], 0.0)
        dy_ref[...] = diff * (1.0 / d)
        per_row = jnp.mean(diff * diff, axis=-1, keepdims=True)
        loss_ref[...] += 0.5 * jnp.sum(per_row, axis=0, keepdims=True)

    row_d = pl.BlockSpec((BLK, d), lambda i: (i, 0))
    return pl.pallas_call(
        body, name="out_loss",
        out_shape=(jax.ShapeDtypeStruct((lp, d), F32), jax.ShapeDtypeStruct((SUBLANES, LANES), F32)),
        grid=(lp // BLK,),
        in_specs=[row_d, row_d, pl.BlockSpec((1, d), lambda i: (0, 0)), row_d],
        out_specs=(row_d, pl.BlockSpec((SUBLANES, LANES), lambda i: (0, 0))),
        compiler_params=_params(("arbitrary",)),
    )(h1, ffn, g3, target)


def _head_masks(rows=BLK):
    lane = lax.broadcasted_iota(jnp.int32, (rows, LANES), 1)
    return [lane < HEAD_DIM, lane >= HEAD_DIM]


def _att_specs(base, lp):
    q_spec = pl.BlockSpec((BLK, LANES), lambda p, i: (i, base + p))
    k_spec = pl.BlockSpec((lp, LANES), lambda p, i: (0, base + N_PAIRS + p))
    v_spec = pl.BlockSpec((lp, LANES), lambda p, i: (0, base + 2 * N_PAIRS + p))
    return q_spec, k_spec, v_spec


def _kv_rows(j, nb=1):
    return pl.ds(pl.multiple_of(j * BLK, nb * BLK), nb * BLK)


def _walk_kv(i, tile, reverse=False, first_pair=0, more=None):
    pairs, odd = i // 2, i % 2

    def wide(t, carry):
        tile(2 * (pairs - 1 - t) if reverse else 2 * t, 2, False)
        return carry

    def single():
        @pl.when(odd == 1)
        def _():
            tile(i - 1, 1, False)

    if reverse:
        tile(i, 1, True)
        single()
        if more is None:
            lax.fori_loop(0, pairs, wide, 0)
        else:
            lax.while_loop(lambda c: (c[0] < pairs) & c[1], lambda c: (wide(c[0], c[0]) + 1, more()), (0, more()))
    else:
        lax.fori_loop(first_pair, pairs, wide, 0)
        single()
        tile(i, 1, True)


def _walk_kv_ahead(i, lead, follow, kept, first_pair=0):
    pairs, odd = i // 2, i % 2

    def keep(values):
        for ref, value in zip(kept, values):
            ref[...] = value

    @pl.when(pairs > first_pair)
    def _():
        keep(lead(2 * first_pair, 2))

    def wide(t, carry):
        ahead = lead(2 * jnp.minimum(t + 1, pairs - 1), 2)
        follow([ref[...] for ref in kept], 2 * t, 2, False)
        keep(ahead)
        return carry

    lax.fori_loop(first_pair, pairs, wide, 0)

    @pl.when(odd == 1)
    def _():
        follow(lead(i - 1, 1), i - 1, 1, False)

    follow(lead(i, 1), i, 1, True)


def _bf16_pieces(x):
    rnd = lambda a: lax.reduce_precision(a, exponent_bits=8, mantissa_bits=7)
    p0 = rnd(x)
    p1 = rnd(x - p0)
    p2 = rnd(x - p0 - p1)
    return [p0.astype(BF16), p1.astype(BF16), p2.astype(BF16)]


def _aug_lanes(cols):
    lp = cols[0].shape[0]
    vals = jnp.stack([c.astype(BF16) for c in cols], axis=-1)
    vals = vals.reshape(lp, N_PAIRS, 2, len(cols))[:, :, ::-1, :]
    vals = jnp.pad(vals, ((0, 0), (0, 0), (0, 0), (0, HEAD_DIM - len(cols))))
    return vals.reshape(lp, WIDTH)


N_AUG = 3


def _riding(exchange, n_in, n_out, n_scratch, grid):
    n = exchange.n if exchange else 0

    def split(refs):
        own_in, ex_in = refs[:n_in], refs[n_in:n_in + n]
        own_out, ex_out = refs[n_in + n:n_in + n + n_out], refs[n_in + n + n_out:n_in + 2 * n + n_out]
        rest = refs[n_in + 2 * n + n_out:]
        return own_in + own_out + rest[:n_scratch], (ex_in, ex_out, rest[n_scratch:])

    def first_step(ex_refs):
        if exchange:
            @pl.when((pl.program_id(0) == 0) & (pl.program_id(1) == 0))
            def _():
                exchange.start(*ex_refs)

    def last_step(ex_refs):
        if exchange:
            @pl.when((pl.program_id(0) == grid[0] - 1) & (pl.program_id(1) == grid[1] - 1))
            def _():
                exchange.wait(*ex_refs)

    return split, first_step, last_step


def _fox_fwd(qkv, qaug, kaug, exchange=None):
    lp = qkv.shape[0]
    nq = lp // BLK
    split, first_step, last_step = _riding(exchange, 5, 2, 3, (N_PAIRS, nq))

    def body(*refs):
        (q_ref, k_ref, v_ref, qa_ref, ka_ref, o_ref, lse_ref, acc_ref, m_ref, s_ref), ex_refs = split(refs)
        first_step(ex_refs)
        i = pl.program_id(1)
        hs = range(2)
        masks = _head_masks()
        qs = q_ref[...] * ATT_SCALE
        qa = qa_ref[...]
        qh = [jnp.where(masks[hh], qs, qa) for hh in hs]
        acc_ref[...] = jnp.zeros_like(acc_ref)
        m_ref[...] = jnp.full_like(m_ref, -1e30)
        row = lax.broadcasted_iota(jnp.int32, (BLK, BLK), 0)
        col = lax.broadcasted_iota(jnp.int32, (BLK, BLK), 1)
        causal = col <= row

        def scores(j, nb):
            rows = _kv_rows(j, nb)
            kmasks = _head_masks(nb * BLK)
            k, ka = k_ref[rows, :], ka_ref[rows, :]
            return [_dot(qh[hh], jnp.where(kmasks[hh], k, ka), NT) for hh in hs]

        def absorb(s, j, nb, diag):
            v = v_ref[_kv_rows(j, nb), :]
            kmasks = _head_masks(nb * BLK)
            vh = [jnp.where(kmasks[hh], v, jnp.ones_like(v)) for hh in hs]
            if diag:
                s = [jnp.where(causal, s[hh], -1e30) for hh in hs]
            m_prev = [m_ref[hh] for hh in hs]
            m_new = [jnp.maximum(m_prev[hh], jnp.max(s[hh], axis=-1, keepdims=True)) for hh in hs]
            p = [jnp.exp(s[hh] - m_new[hh]).astype(BF16) for hh in hs]
            for hh in hs:
                acc_ref[hh] = jnp.exp(m_prev[hh] - m_new[hh]) * acc_ref[hh] + _dot(p[hh], vh[hh], NN)
                m_ref[hh] = m_new[hh]

        _walk_kv_ahead(i, scores, absorb, [s_ref.at[hh] for hh in hs], _first_pair(qa, i))
        acc = [acc_ref[hh] for hh in hs]
        denom = [acc[0][:, HEAD_DIM:HEAD_DIM + 1], acc[1][:, 0:1]]
        o_ref[...] = jnp.where(masks[0], acc[0] / denom[0], acc[1] / denom[1])
        lse_ref[...] = jnp.where(masks[0], m_ref[0] + jnp.log(denom[0]), m_ref[1] + jnp.log(denom[1]))
        last_step(ex_refs)

    q_spec, k_spec, v_spec = _att_specs(0, lp)
    blk = pl.BlockSpec((BLK, LANES), lambda p, i: (i, p))
    col_full = pl.BlockSpec((lp, LANES), lambda p, i: (0, p))
    out = jax.ShapeDtypeStruct((lp, WIDTH), F32)
    ex = exchange
    return pl.pallas_call(
        body, name="fox_fwd",
        out_shape=(out, out) + tuple(ex.out_shapes if ex else ()),
        grid=(N_PAIRS, nq),
        in_specs=[q_spec, k_spec, v_spec, blk, col_full] + (ex.any_specs if ex else []),
        out_specs=(blk, blk) + tuple(ex.any_specs if ex else ()),
        scratch_shapes=[pltpu.VMEM((2, BLK, LANES), F32), pltpu.VMEM((2, BLK, 1), F32),
                        pltpu.VMEM((2, BLK, 2 * BLK), F32)] + (ex.scratch if ex else []),
        compiler_params=_params(("arbitrary", "arbitrary") if ex else ("parallel", "parallel")),
    )(qkv, qkv, qkv, qaug, kaug, *(ex.arrays if ex else []))


FIRST_BLOCK_LANE = 9


def _first_pair(qa, i):
    lane = lax.broadcasted_iota(jnp.int32, qa.shape, 1)
    first = jnp.max(jnp.where(lane == HEAD_DIM + FIRST_BLOCK_LANE, qa.astype(F32), 0.0)).astype(jnp.int32)
    return jnp.clip(first, 0, i) // 2


def _fox_first_blocks(qkv, c):
    lp = qkv.shape[0]
    nq = lp // BLK

    def body(x_ref, o_ref):
        xv = x_ref[...].astype(F32)
        sq = (xv * xv).astype(BF16)
        col = lax.broadcasted_iota(jnp.int32, (2 * WIDTH, LANES), 0)
        lane = lax.broadcasted_iota(jnp.int32, (2 * WIDTH, LANES), 1)
        pick = (col // HEAD_DIM == lane).astype(BF16)
        o_ref[...] = _dot(sq, pick, NN)

    norms = pl.pallas_call(
        body, name="fox_norms",
        out_shape=jax.ShapeDtypeStruct((lp, LANES), F32),
        grid=(nq,),
        in_specs=[pl.BlockSpec((BLK, 2 * WIDTH), lambda i: (i, 0))],
        out_specs=pl.BlockSpec((BLK, LANES), lambda i: (i, 0)),
        compiler_params=_params(("parallel",)),
    )(qkv)
    a_max = 1.02 * jnp.sqrt(norms[:, :N_HEADS].reshape(nq, BLK, N_HEADS).max(axis=1))
    b_max = 1.02 * jnp.sqrt(norms[:, N_HEADS:2 * N_HEADS].reshape(nq, BLK, N_HEADS).max(axis=1))
    c_max = c.reshape(nq, BLK, N_HEADS).max(axis=1)
    c_min = c.reshape(nq, BLK, N_HEADS).min(axis=1)
    bound = (a_max[:, None] * (b_max[None, :] + b_max[:, None]) * ATT_SCALE + c_max[:, None] - c_min[None, :])
    alive = (bound > EXP_IS_ZERO) | jnp.isnan(bound)
    alive = alive.reshape(nq, nq, N_PAIRS, 2).any(axis=-1)
    first = jnp.argmax(alive, axis=1).astype(F32)
    return jnp.repeat(jnp.repeat(first, 2, axis=1), BLK, axis=0)


def _head_dots(a, b):
    lp = a.shape[0]

    def body(a_ref, b_ref, o_ref):
        lane = lax.broadcasted_iota(jnp.int32, (BLK, LANES), 1)
        out = jnp.zeros((BLK, LANES), F32)
        for p in range(N_PAIRS):
            cols = slice(p * LANES, (p + 1) * LANES)
            prod = a_ref[:, cols] * b_ref[:, cols]
            for hh in range(2):
                part = jnp.where((lane >= HEAD_DIM) == (hh == 1), prod, 0.0)
                out = jnp.where(lane == 2 * p + hh, jnp.sum(part, axis=-1, keepdims=True), out)
        o_ref[...] = out

    row = pl.BlockSpec((BLK, WIDTH), lambda i: (i, 0))
    return pl.pallas_call(
        body, name="head_dots",
        out_shape=jax.ShapeDtypeStruct((lp, LANES), F32),
        grid=(lp // BLK,),
        in_specs=[row, row],
        out_specs=pl.BlockSpec((BLK, LANES), lambda i: (i, 0)),
        compiler_params=_params(("parallel",)),
    )(a, b)


def _fox_bwd(qkv, qaug, kaug, doaug, d_o, exchange=None):
    lp = qkv.shape[0]
    nq = lp // BLK
    split, first_step, last_step = _riding(exchange, 7, 5, 2, (N_PAIRS, nq))

    def body(*refs):
        (q_ref, k_ref, v_ref, qa_ref, ka_ref, da_ref, do_ref,
         dq_ref, dk_ref, dv_ref, dc_ref, dr_ref, acc_ref, rs_ref), ex_refs = split(refs)
        first_step(ex_refs)
        i = pl.program_id(1)
        hs = range(2)
        masks = _head_masks()
        qs = q_ref[...] * ATT_SCALE
        qa = qa_ref[...]
        qm = [jnp.where(masks[hh], qs, 0) for hh in hs]
        qh = [jnp.where(masks[hh], qs, qa) for hh in hs]
        dov = do_ref[...].astype(BF16)
        doa = da_ref[...]
        dom = [jnp.where(masks[hh], dov, 0) for hh in hs]
        doh = [jnp.where(masks[hh], dov, doa) for hh in hs]
        row = lax.broadcasted_iota(jnp.int32, (BLK, BLK), 0)
        col = lax.broadcasted_iota(jnp.int32, (BLK, BLK), 1)
        causal = col <= row

        @pl.when(i == 0)
        def _():
            dk_ref[...] = jnp.zeros_like(dk_ref)
            dv_ref[...] = jnp.zeros_like(dv_ref)
            dc_ref[...] = jnp.zeros_like(dc_ref)

        acc_ref[...] = jnp.zeros_like(acc_ref)
        rs_ref[...] = jnp.zeros_like(rs_ref)

        def lead(j, nb):
            rows = _kv_rows(j, nb)
            kmasks = _head_masks(nb * BLK)
            k, ka, v = k_ref[rows, :], ka_ref[rows, :], v_ref[rows, :]
            ones = (lax.broadcasted_iota(jnp.int32, v.shape, 1) % HEAD_DIM < N_AUG).astype(BF16)
            logp = [_dot(qh[hh], jnp.where(kmasks[hh], k, ka), NT) for hh in hs]
            dp = [_dot(doh[hh], jnp.where(kmasks[hh], v, ones), NT) for hh in hs]
            return logp + dp

        def follow(lead_out, j, nb, diag):
            rows = _kv_rows(j, nb)
            k = k_ref[rows, :]
            logp, dp = lead_out[:2], lead_out[2:]
            p = [jnp.exp(logp[hh]) for hh in hs]
            if diag:
                p = [jnp.where(causal, p[hh], 0.0) for hh in hs]
            ds = [p[hh] * dp[hh] for hh in hs]
            dsb = [ds[hh].astype(BF16) for hh in hs]
            for hh in hs:
                acc_ref[hh] += _dot(dsb[hh], k, NN)
                col_sums = jnp.sum(ds[hh], axis=0, keepdims=True)
                for b in range(nb):
                    dc_ref[0, j + b, hh:hh + 1, :] -= col_sums[:, b * BLK:(b + 1) * BLK]
                rs_ref[hh] += jnp.sum(ds[hh], axis=-1, keepdims=True)
            dk_ref[rows, :] += _dot(dsb[0], qm[0], TN) + _dot(dsb[1], qm[1], TN)
            dv_ref[rows, :] += _dot(p[0].astype(BF16), dom[0], TN) + _dot(p[1].astype(BF16), dom[1], TN)

        _walk_kv(i, lambda j, nb, diag: follow(lead(j, nb), j, nb, diag), first_pair=_first_pair(qa, i))
        dq_ref[...] = (jnp.where(masks[0], acc_ref[0], acc_ref[1]) * ATT_SCALE).astype(BF16)
        dr_ref[...] = jnp.where(masks[0], rs_ref[0], rs_ref[1])
        last_step(ex_refs)

    q_spec, k_spec, v_spec = _att_specs(0, lp)
    blk = pl.BlockSpec((BLK, LANES), lambda p, i: (i, p))
    col_full = pl.BlockSpec((lp, LANES), lambda p, i: (0, p))
    crow_spec = pl.BlockSpec((1, nq, 2, BLK), lambda p, i: (p, 0, 0, 0))
    ex = exchange
    return pl.pallas_call(
        body, name="fox_bwd",
        out_shape=(jax.ShapeDtypeStruct((lp, WIDTH), BF16), jax.ShapeDtypeStruct((lp, WIDTH), F32),
                   jax.ShapeDtypeStruct((lp, WIDTH), F32), jax.ShapeDtypeStruct((N_PAIRS, nq, 2, BLK), F32),
                   jax.ShapeDtypeStruct((lp, WIDTH), F32)) + tuple(ex.out_shapes if ex else ()),
        grid=(N_PAIRS, nq),
        in_specs=[q_spec, k_spec, v_spec, blk, col_full, blk, blk] + (ex.any_specs if ex else []),
        out_specs=(blk, col_full, col_full, crow_spec, blk) + tuple(ex.any_specs if ex else ()),
        scratch_shapes=[pltpu.VMEM((2, BLK, LANES), F32), pltpu.VMEM((2, BLK, 1), F32)] + (ex.scratch if ex else []),
        compiler_params=_params(("arbitrary", "arbitrary") if ex else ("parallel", "arbitrary")),
    )(qkv, qkv, qkv, qaug, kaug, doaug, d_o, *(ex.arrays if ex else []))


def _sb_scores(z):
    ell = jnp.minimum(z, 0.0) - jnp.log(1.0 + jnp.exp(-jnp.abs(z)))
    return ell, ell - z


def _stacked(tri):
    return jnp.concatenate([tri, tri], axis=0)


def _cumsum_dot(x, tri2):
    hi, lo = _split_bf16(x)
    return _dot(jnp.concatenate([hi, lo], axis=1), tri2, NN)


def _sb_units(qh, k, strict2, causal, nb, diag):
    hs, bs = range(2), range(nb)
    z = [_dot(qh[hh], k, NT) for hh in hs]
    sc = [[_sb_scores(z[hh][:, b * BLK:(b + 1) * BLK]) for b in bs] for hh in hs]
    ell = [[sc[hh][b][0] for b in bs] for hh in hs]
    kap = [[jnp.where(causal, sc[hh][b][1], 0.0) if diag else sc[hh][b][1] for b in bs] for hh in hs]
    later = [[_cumsum_dot(kap[hh][b], strict2) for b in bs] for hh in hs]
    return ell, kap, later


def _row_sum(x):
    return jnp.sum(x, axis=-1, keepdims=True)


def _join(blocks):
    joined = blocks[0] if len(blocks) == 1 else jnp.concatenate(blocks, axis=1)
    return joined.astype(BF16)


def _sb_fwd(qkv):
    lp = qkv.shape[0]
    nq = lp // BLK
    assert nq <= HEAD_DIM

    def body(q_ref, k_ref, v_ref, o_ref, lc_ref, acc_ref, car_ref):
        i = pl.program_id(1)
        masks = _head_masks()
        qs = q_ref[...] * ATT_SCALE
        qh = [jnp.where(mk, qs, 0).astype(BF16) for mk in masks]
        row = lax.broadcasted_iota(jnp.int32, (BLK, BLK), 0)
        col = lax.broadcasted_iota(jnp.int32, (BLK, BLK), 1)
        lane = lax.broadcasted_iota(jnp.int32, (BLK, LANES), 1)
        causal = col < row
        strict2 = _stacked((row > col).astype(BF16))
        acc_ref[...] = jnp.zeros_like(acc_ref)
        car_ref[...] = jnp.zeros_like(car_ref)
        lc_ref[...] = jnp.full_like(lc_ref, NOT_VISITED)

        def tile(j, nb, diag):
            hs, bs = range(2), range(nb)
            rows = _kv_rows(j, nb)
            k, v = k_ref[rows, :], v_ref[rows, :]
            ell, kap, later = _sb_units(qh, k, strict2, causal, nb, diag)
            car = [[None] * nb for _ in hs]
            for hh in hs:
                run = car_ref[hh]
                for b in reversed(bs):
                    car[hh][b] = run
                    run = run + _row_sum(kap[hh][b])
                car_ref[hh] = run
            if not diag:
                kept = lc_ref[...]
                for hh in hs:
                    for b in bs:
                        kept = jnp.where(lane == j + b + HEAD_DIM * hh, car[hh][b], kept)
                lc_ref[...] = kept
            a = [[jnp.exp(ell[hh][b] + later[hh][b] + car[hh][b]) for b in bs] for hh in hs]
            if diag:
                a = [[jnp.where(causal, a[hh][b], 0.0) for b in bs] for hh in hs]
            for hh in hs:
                acc_ref[hh] += _dot(_join(a[hh]), v, NN)

        _walk_kv(i, tile, reverse=True, more=lambda: jnp.max(car_ref[...]) > EXP_IS_ZERO)
        o_ref[...] = jnp.where(masks[0], acc_ref[0], acc_ref[1])

    q_spec, k_spec, v_spec = _att_specs(3 * N_PAIRS, lp)
    blk = pl.BlockSpec((BLK, LANES), lambda p, i: (i, p))
    out = jax.ShapeDtypeStruct((lp, WIDTH), F32)
    return pl.pallas_call(
        body, name="sb_fwd",
        out_shape=(out, out),
        grid=(N_PAIRS, nq),
        in_specs=[q_spec, k_spec, v_spec],
        out_specs=(blk, blk),
        scratch_shapes=[pltpu.VMEM((2, BLK, LANES), F32), pltpu.VMEM((2, BLK, 1), F32)],
        compiler_params=_params(("parallel", "parallel")),
    )(qkv, qkv, qkv)


def _sb_bwd(qkv, lcar, d_o):
    lp = qkv.shape[0]
    nq = lp // BLK

    def body(q_ref, k_ref, v_ref, lc_ref, do_ref, dq_ref, dk_ref, dv_ref, acc_ref, cg_ref):
        i = pl.program_id(1)
        masks = _head_masks()
        qs = q_ref[...] * ATT_SCALE
        qh = [jnp.where(mk, qs, 0).astype(BF16) for mk in masks]
        dov = do_ref[...]
        doh = [jnp.where(mk, dov, 0.0).astype(BF16) for mk in masks]
        lcv = lc_ref[...]
        lane_row = lax.broadcasted_iota(jnp.int32, (1, LANES), 1)
        alive = jnp.where(jnp.max(lcv, axis=0, keepdims=True) > EXP_IS_ZERO, 1.0, 0.0)
        n_alive = jnp.maximum(jnp.sum(jnp.where(lane_row < HEAD_DIM, alive, 0.0)),
                              jnp.sum(jnp.where(lane_row >= HEAD_DIM, alive, 0.0))).astype(jnp.int32)
        first_pair = jnp.maximum(i - n_alive, 0) // 2
        row = lax.broadcasted_iota(jnp.int32, (BLK, BLK), 0)
        col = lax.broadcasted_iota(jnp.int32, (BLK, BLK), 1)
        lane = lax.broadcasted_iota(jnp.int32, (BLK, LANES), 1)
        causal = col < row
        strict2 = _stacked((row > col).astype(BF16))
        before2 = _stacked((row < col).astype(BF16))

        @pl.when(i == 0)
        def _():
            dk_ref[...] = jnp.zeros_like(dk_ref)
            dv_ref[...] = jnp.zeros_like(dv_ref)

        acc_ref[...] = jnp.zeros_like(acc_ref)
        cg_ref[...] = jnp.zeros_like(cg_ref)

        def tile(j, nb, diag):
            hs, bs = range(2), range(nb)
            rows = _kv_rows(j, nb)
            k, v = k_ref[rows, :], v_ref[rows, :]
            if diag:
                car = [[0.0] for _ in hs]
            else:
                car = [[_row_sum(jnp.where(lane == j + b + HEAD_DIM * hh, lcv, 0.0)) for b in bs] for hh in hs]
            da = [_dot(doh[hh], v, NT) for hh in hs]
            ell, _, later = _sb_units(qh, k, strict2, causal, nb, diag)
            a = [[jnp.exp(ell[hh][b] + later[hh][b] + car[hh][b]) for b in bs] for hh in hs]
            if diag:
                a = [[jnp.where(causal, a[hh][b], 0.0) for b in bs] for hh in hs]
            g = [[da[hh][:, b * BLK:(b + 1) * BLK] * a[hh][b] for b in bs] for hh in hs]
            cg = [[_cumsum_dot(g[hh][b], before2) for b in bs] for hh in hs]
            before = [[None] * nb for _ in hs]
            for hh in hs:
                run = cg_ref[hh]
                for b in bs:
                    before[hh][b] = run
                    run = run + _row_sum(g[hh][b])
                cg_ref[hh] = run
            dz = [[g[hh][b] - jnp.exp(ell[hh][b]) * (g[hh][b] + cg[hh][b] + before[hh][b]) for b in bs] for hh in hs]
            if diag:
                dz = [[jnp.where(causal, dz[hh][b], 0.0) for b in bs] for hh in hs]
            dzb = [_join(dz[hh]) for hh in hs]
            ab = [_join(a[hh]) for hh in hs]
            for hh in hs:
                acc_ref[hh] += _dot(dzb[hh], k, NN)
            dk_ref[rows, :] += _dot(dzb[0], qh[0], TN) + _dot(dzb[1], qh[1], TN)
            dv_ref[rows, :] += _dot(ab[0], doh[0], TN) + _dot(ab[1], doh[1], TN)

        _walk_kv(i, tile, first_pair=first_pair)
        dq_ref[...] = (jnp.where(masks[0], acc_ref[0], acc_ref[1]) * ATT_SCALE).astype(BF16)

    q_spec, k_spec, v_spec = _att_specs(3 * N_PAIRS, lp)
    blk = pl.BlockSpec((BLK, LANES), lambda p, i: (i, p))
    col_full = pl.BlockSpec((lp, LANES), lambda p, i: (0, p))
    return pl.pallas_call(
        body, name="sb_bwd",
        out_shape=(jax.ShapeDtypeStruct((lp, WIDTH), BF16), jax.ShapeDtypeStruct((lp, WIDTH), F32),
                   jax.ShapeDtypeStruct((lp, WIDTH), F32)),
        grid=(N_PAIRS, nq),
        in_specs=[q_spec, k_spec, v_spec, blk, blk],
        out_specs=(blk, col_full, col_full),
        scratch_shapes=[pltpu.VMEM((2, BLK, LANES), F32), pltpu.VMEM((2, BLK, 1), F32)],
        compiler_params=_params(("parallel", "arbitrary")),
    )(qkv, qkv, qkv, lcar, d_o)


def _local_step(x, target, meta, gains, w_in, b_forget, w_fox, w_sb, w_out, w_up, conv_w, conv_b, w_down,
                ffn_block=None, late_weights=None, early_grads=None):
    s, d = x.shape
    n_valid = N_META + s
    lp = -(-n_valid // BLK) * BLK
    pad = lp - n_valid
    nq = lp // BLK

    h0 = jnp.concatenate([meta, x, jnp.zeros((pad, d), F32)], axis=0)
    tgt = jnp.concatenate([jnp.zeros((N_META, d), F32), target, jnp.zeros((pad, d), F32)], axis=0)

    q_a, k_a, v_a, f_a, q_b, k_b, v_b, g_a, g_b = jnp.split(
        w_in, [512, 1024, 1536, 1544, 2056, 2568, 3080, 4104], axis=1)
    w_qkv = jnp.concatenate([q_a, k_a, v_a, q_b, k_b, v_b], axis=1)
    w_gf = jnp.concatenate([g_a, g_b, f_a, jnp.zeros((d, F_PAD - N_HEADS), BF16)], axis=1)
    b_pad = jnp.concatenate([b_forget.reshape(1, N_HEADS), jnp.zeros((1, LANES - N_HEADS), F32)], axis=1)
    g0, g1, g2, g3 = (gains[i:i + 1] for i in range(4))

    xn1 = _rmsnorm_fwd(h0, g0, "norm1_fwd")
    qkv = _mm(xn1, w_qkv, "nn", BF16, "proj_qkv")
    gf = _mm(xn1, w_gf, "nn", F32, "proj_gates")
    fpre = gf[:, 2 * d:2 * d + LANES]
    c = _forget_fwd(fpre, b_pad)[:, :N_HEADS]
    c_pieces = _bf16_pieces(c)
    one = jnp.ones((lp, N_HEADS), BF16)
    kaug = _aug_lanes(3 * [one] + [-x for x in c_pieces] + 3 * [one])
    first_block = _fox_first_blocks(qkv, c)
    o_a, lse, *gathered = _fox_fwd(qkv, _aug_lanes(c_pieces + 3 * [one] + 3 * [0 * one] + [first_block]), kaug,
                                   late_weights[0] if late_weights else None)
    if late_weights:
        w_fox, w_sb, w_out, w_up, conv_w, w_down = late_weights[1](gathered)
    ffn_block = ffn_block or w_up.shape[1] // 2
    o_b, lcar = _sb_fwd(qkv)
    h1, ya, yb, gated, mixed = _mix_fwd(o_a, o_b, gf, h0, w_fox, w_sb, w_out, g1)
    xn3 = _rmsnorm_fwd(h1, g2, "norm3_fwd")
    up = _mm(xn3, w_up, "nn", F32, "ffn_up")
    act = _conv_gelu_fwd(up, conv_w, conv_b, ffn_block)
    ffn = _mm(act, w_down, "nn", F32, "ffn_down")
    dy, loss_acc = _out_loss(h1, ffn, g3, tgt, n_valid)
    loss = loss_acc[0, 0]

    d_ffn, dg3 = _rmsnorm_bwd(ffn, g3, dy, None, BF16, "norm4_bwd")
    d_act = _mm(d_ffn, w_down, "nt", F32, "ffn_down_dx")
    gw_down = _mm(act, d_ffn, "tn", BF16, "ffn_down_dw")
    d_up, g_conv_w, g_conv_b = _conv_gelu_bwd(up, d_act, conv_w, conv_b, ffn_block)
    d_xn3 = _mm(d_up, w_up, "nt", F32, "ffn_up_dx")
    gw_up = _mm(xn3, d_up, "tn", BF16, "ffn_up_dw")
    dh1, dg2 = _rmsnorm_bwd(h1, g2, d_xn3, dy, F32, "norm3_bwd")

    d_mixed, dg1 = _rmsnorm_bwd(mixed, g1, dh1, None, BF16, "norm2_bwd")
    d_gated = _mm(d_mixed, w_out, "nt", F32, "out_dx")
    gw_out = _mm(gated, d_mixed, "tn", BF16, "out_dw")
    d_ya, d_yb, d_ga, d_gb = _gate_bwd(d_gated, gf, ya, yb)
    d_oa = _mm(d_ya, w_fox, "nt", F32, "fox_o_dx")
    gw_fox = _mm(o_a, d_ya, "tn", BF16, "fox_o_dw")
    d_ob = _mm(d_yb, w_sb, "nt", F32, "sb_o_dx")
    gw_sb = _mm(o_b, d_yb, "tn", BF16, "sb_o_dw")
    neg_lse = [-x for x in _bf16_pieces(lse[:, ::HEAD_DIM])]
    neg_dsum = [-x for x in _bf16_pieces(_head_dots(d_oa, o_a)[:, :N_HEADS])]
    qaug = _aug_lanes(c_pieces + 3 * [one] + neg_lse + [first_block])
    early = {"w_o_fox": gw_fox, "w_o_sb": gw_sb, "w_out": gw_out, "w_up": gw_up, "conv_w": g_conv_w,
             "conv_b": g_conv_b, "w_down": gw_down}
    dq_a, dk_a, dv_a, dcrow, drow, *early_parts = _fox_bwd(
        qkv, qaug, kaug, _aug_lanes(neg_dsum), d_oa, early_grads(early) if early_grads else None)
    dq_b, dk_b, dv_b = _sb_bwd(qkv, lcar, d_ob)
    dc = dcrow.transpose(0, 2, 1, 3).reshape(N_HEADS, lp).T + drow[:, ::HEAD_DIM]
    dc = jnp.concatenate([dc, jnp.zeros((lp, LANES - N_HEADS), F32)], axis=1)
    df, db = _forget_bwd(dc, fpre, b_pad)
    lane = jnp.arange(LANES) < N_HEADS
    df = jnp.where(lane[None, :], df, 0.0)
    d_proj = jnp.concatenate(
        [dq_a, dk_a.astype(BF16), dv_a.astype(BF16), dq_b, dk_b.astype(BF16), dv_b.astype(BF16),
         d_ga, d_gb, df.astype(BF16), jnp.zeros((lp, F_PAD - LANES), BF16)], axis=1)
    w_in_p = jnp.concatenate([w_qkv, w_gf], axis=1)
    d_xn1 = _mm(d_proj, w_in_p, "nt", F32, "proj_dx")
    gw_in_p = _mm(xn1, d_proj, "tn", BF16, "proj_dw")
    dh0, dg0 = _rmsnorm_bwd(h0, g0, d_xn1, dh1, F32, "norm1_bwd")

    qkv_parts = jnp.split(gw_in_p[:, :6 * WIDTH], 6, axis=1)
    gw_in = jnp.concatenate(
        qkv_parts[:3] + [gw_in_p[:, 6 * WIDTH + 2 * d:6 * WIDTH + 2 * d + N_HEADS]] + qkv_parts[3:]
        + [gw_in_p[:, 6 * WIDTH:6 * WIDTH + 2 * d]], axis=1)
    grads = {
        "meta_tokens": dh0[:N_META],
        "norm_gains": jnp.concatenate([dg0, dg1, dg2, dg3], axis=0),
        "w_in": gw_in,
        "b_forget": db[:, :N_HEADS],
        "early_parts": early_parts,
        **early,
    }
    return loss, dh0[N_META:n_valid], grads


MESH_IDS = pl.DeviceIdType.MESH


def _window(ref, kind, idx, rows, cols):
    if kind == "slots":
        return ref.at[idx]
    if kind == "gate_value":
        half = N_DEV // 2
        idx = jnp.where(idx < half, 2 * idx, 2 * (idx - half) + 1)
        kind = "cols"
    if kind == "cols":
        return ref.at[:, pl.ds(pl.multiple_of(idx * cols, cols & -cols), cols)]
    return ref.at[pl.ds(pl.multiple_of(idx * rows, rows & -rows), rows), :]


def _gathered_shape(shape, kind):
    rows, cols = shape
    return {"slots": (N_DEV, rows, cols), "cols": (rows, N_DEV * cols), "gate_value": (rows, N_DEV * cols),
            "rows": (N_DEV * rows, cols)}[kind]


def _all_gather(shards, kinds):
    n = len(shards)

    def body(*refs):
        ins, outs = refs[:n], refs[n:2 * n]
        send_sems, recv_sems, local_sems = refs[2 * n:]
        x, y, c = lax.axis_index("x"), lax.axis_index("y"), lax.axis_index("c")
        me, sibling = (x, y, c), (x, y, 1 - c)
        chips = [(1 - x, y), (x, 1 - y), (1 - x, 1 - y)]

        def part(t, px, py, pc):
            return _window(outs[t], kinds[t], 4 * px + 2 * py + pc, *shards[t].shape)

        def copy(k, t, blk, to, src=None):
            return pltpu.make_async_remote_copy(
                src_ref=part(t, *blk) if src is None else src, dst_ref=part(t, *blk),
                send_sem=send_sems.at[k, t], recv_sem=recv_sems.at[k, t],
                device_id=to, device_id_type=MESH_IDS)

        mine = [pltpu.make_async_copy(ins[t], part(t, *me), local_sems.at[t]) for t in range(n)]
        for cp in mine:
            cp.start()
        first = [copy(0, t, me, sibling, src=ins[t]) for t in range(n)]
        first += [copy(1 + j, t, me, (*chip, c), src=ins[t]) for j, chip in enumerate(chips) for t in range(n)]
        for cp in first:
            cp.start()
        passed = []
        for j, chip in enumerate(chips):
            for t in range(n):
                copy(1 + j, t, (*chip, c), me).wait_recv()
                passed.append(copy(4 + j, t, (*chip, c), sibling))
                passed[-1].start()
        for t in range(n):
            copy(0, t, sibling, me).wait_recv()
        for j, chip in enumerate(chips):
            for t in range(n):
                copy(4 + j, t, (*chip, 1 - c), me).wait_recv()
        for cp in first + passed:
            cp.wait_send()
        for cp in mine:
            cp.wait()

    any_space = pl.BlockSpec(memory_space=pl.ANY)
    return pl.pallas_call(
        body, name="all_gather",
        out_shape=tuple(jax.ShapeDtypeStruct(_gathered_shape(a.shape, k), a.dtype) for a, k in zip(shards, kinds)),
        in_specs=[any_space] * n,
        out_specs=tuple([any_space] * n),
        scratch_shapes=[pltpu.SemaphoreType.DMA((7, n)), pltpu.SemaphoreType.DMA((7, n)),
                        pltpu.SemaphoreType.DMA((n,))],
    )(*shards)


class _Exchange:
    def __init__(self, arrays, kinds, shard_shapes, gather):
        self.arrays, self.kinds, self.shard_shapes, self.gather = list(arrays), list(kinds), list(shard_shapes), gather
        self.n = n = len(self.arrays)
        self.any_specs = [pl.BlockSpec(memory_space=pl.ANY)] * n
        if gather:
            shapes = [_gathered_shape(a.shape, k) for a, k in zip(self.arrays, kinds)]
        else:
            shapes = [(N_DEV,) + tuple(s) for s in shard_shapes]
        self.out_shapes = [jax.ShapeDtypeStruct(s, a.dtype) for s, a in zip(shapes, self.arrays)]
        self.scratch = [pltpu.SemaphoreType.DMA((N_DEV - 1, n)), pltpu.SemaphoreType.DMA((N_DEV - 1, n)),
                        pltpu.SemaphoreType.DMA((n,))]

    def copies(self, ins, outs, sems):
        send_sems, recv_sems, local_sems = sems
        x, y, c = lax.axis_index("x"), lax.axis_index("y"), lax.axis_index("c")
        my = 4 * x + 2 * y + c

        def src(t, receiver):
            if self.gather or self.kinds[t] == "all":
                return ins[t]
            return _window(ins[t], self.kinds[t], receiver, *self.shard_shapes[t])

        def dst(t, sender):
            if self.gather:
                return _window(outs[t], self.kinds[t], sender, *self.shard_shapes[t])
            return outs[t].at[sender]

        local = [pltpu.make_async_copy(src(t, my), dst(t, my), local_sems.at[t]) for t in range(self.n)]
        sends, arrivals = [], []
        for rel in range(1, N_DEV):
            px, py, pc = x ^ (rel >> 2), y ^ ((rel >> 1) & 1), c ^ (rel & 1)
            peer = 4 * px + 2 * py + pc
            for t in range(self.n):
                common = dict(send_sem=send_sems.at[rel - 1, t], recv_sem=recv_sems.at[rel - 1, t],
                              device_id=(px, py, pc), device_id_type=MESH_IDS)
                sends.append(pltpu.make_async_remote_copy(src_ref=src(t, peer), dst_ref=dst(t, my), **common))
                arrivals.append(pltpu.make_async_remote_copy(src_ref=src(t, my), dst_ref=dst(t, peer), **common))
        return local, sends, arrivals

    def start(self, ins, outs, sems):
        local, sends, _ = self.copies(ins, outs, sems)
        for cp in local + sends:
            cp.start()

    def wait(self, ins, outs, sems):
        local, sends, arrivals = self.copies(ins, outs, sems)
        for cp in arrivals:
            cp.wait_recv()
        for cp in sends:
            cp.wait_send()
        for cp in local:
            cp.wait()


def _exchange(grads, kinds, shard_shapes):
    ex = _Exchange(grads, kinds, shard_shapes, gather=False)
    n = ex.n

    def body(*refs):
        ins, outs, sems = refs[:n], refs[n:2 * n], refs[2 * n:]
        ex.start(ins, outs, sems)
        ex.wait(ins, outs, sems)

    return pl.pallas_call(
        body, name="grad_exchange",
        out_shape=tuple(ex.out_shapes),
        in_specs=ex.any_specs,
        out_specs=tuple(ex.any_specs),
        scratch_shapes=ex.scratch,
    )(*grads)


def _sum_adamw(parts, w, m, v, name):
    rows, cols = w.shape
    n, rows_p, cols_p = parts.shape
    tr = _tile(rows, BLK, SUBLANES) if rows > BLK else rows
    tp = tr if rows_p == rows else rows_p
    c1 = 1.0 - ADAM_B1 ** ADAM_STEP
    c2 = 1.0 - ADAM_B2 ** ADAM_STEP

    def body(p_ref, w_ref, m_ref, v_ref, g_ref, d_ref, nm_ref, nv_ref):
        gv = p_ref[0, 0:tr, 0:cols].astype(F32)
        for s in range(1, n):
            gv = gv + p_ref[s, 0:tr, 0:cols].astype(F32)
        g_ref[...] = gv
        nm = ADAM_B1 * m_ref[...] + (1.0 - ADAM_B1) * gv
        nv = ADAM_B2 * v_ref[...] + (1.0 - ADAM_B2) * (gv * gv)
        m_hat = nm / c1
        v_hat = nv / c2
        d_ref[...] = -ADAM_LR * (m_hat / (jnp.sqrt(v_hat) + ADAM_EPS) + ADAM_WD * w_ref[...])
        nm_ref[...] = nm
        nv_ref[...] = nv

    spec = pl.BlockSpec((tr, cols), lambda i: (i, 0))
    out = jax.ShapeDtypeStruct((rows, cols), F32)
    return pl.pallas_call(
        body, name=name,
        out_shape=(out, out, out, out),
        grid=(rows // tr,),
        in_specs=[pl.BlockSpec((n, tp, cols_p), lambda i: (0, i, 0)), spec, spec, spec],
        out_specs=(spec, spec, spec, spec),
        compiler_params=_params(("parallel",)),
    )(parts, w, m, v)


def _pad2(a, rows, cols):
    return jnp.pad(a, ((0, rows - a.shape[0]), (0, cols - a.shape[1])))


WEIGHTS = ["meta_tokens", "norm_gains", "w_in", "b_forget", "w_o_fox", "w_o_sb", "w_out", "w_up", "conv_w",
           "conv_b", "w_down"]


def kernel(x, meta_tokens, norm_gains, w_in, b_forget, w_o_fox, w_o_sb, w_out, w_up, conv_w, conv_b, w_down, loss_target, m_meta_tokens, m_norm_gains, m_w_in, m_b_forget, m_w_o_fox, m_w_o_sb, m_w_out, m_w_up, m_conv_w, m_conv_b, m_w_down, v_meta_tokens, v_norm_gains, v_w_in, v_b_forget, v_w_o_fox, v_w_o_sb, v_w_out, v_w_up, v_conv_w, v_conv_b, v_w_down):
    w = dict(meta_tokens=meta_tokens, norm_gains=norm_gains, w_in=w_in, b_forget=b_forget, w_o_fox=w_o_fox,
             w_o_sb=w_o_sb, w_out=w_out, w_up=w_up, conv_w=conv_w, conv_b=conv_b, w_down=w_down)
    mom = dict(meta_tokens=m_meta_tokens, norm_gains=m_norm_gains, w_in=m_w_in, b_forget=m_b_forget,
               w_o_fox=m_w_o_fox, w_o_sb=m_w_o_sb, w_out=m_w_out, w_up=m_w_up, conv_w=m_conv_w, conv_b=m_conv_b,
               w_down=m_w_down)
    vel = dict(meta_tokens=v_meta_tokens, norm_gains=v_norm_gains, w_in=v_w_in, b_forget=v_b_forget,
               w_o_fox=v_w_o_fox, w_o_sb=v_w_o_sb, w_out=v_w_out, w_up=v_w_up, conv_w=v_conv_w, conv_b=v_conv_b,
               w_down=v_w_down)
    w2 = {n: a.reshape(a.shape[-2:]) for n, a in w.items()}
    shard_shape = {n: a.shape for n, a in w2.items()}

    d = x.shape[-1]
    up_cols = shard_shape["w_up"][1]
    up_pad = -(-up_cols // LANES) * LANES
    half = N_DEV // 2
    pad_rows = lambda a: _pad2(a, SUBLANES, a.shape[1])

    shards = [
        ("w_in", "slots", w2["w_in"].astype(BF16)),
        ("meta_tokens", "cols", w2["meta_tokens"]),
        ("norm_gains", "cols", pad_rows(w2["norm_gains"])),
    ]
    later = [
        ("w_o_fox", "cols", w2["w_o_fox"].astype(BF16)),
        ("w_o_sb", "cols", w2["w_o_sb"].astype(BF16)),
        ("w_out", "rows", w2["w_out"].astype(BF16)),
        ("w_up", "gate_value", _pad2(w2["w_up"], d, up_pad).astype(BF16)),
        ("conv_w", "gate_value", _pad2(w2["conv_w"], SUBLANES, up_pad)),
        ("w_down", "rows", w2["w_down"].astype(BF16)),
    ]
    full = dict(zip([s[0] for s in shards], _all_gather([s[2] for s in shards], [s[1] for s in shards])))
    w_in_full = jnp.concatenate([full["w_in"][i] for i in range(N_DEV)], axis=1)
    conv_b_p = jnp.pad(w2["conv_b"].reshape(2, half, up_cols), ((0, 0), (0, 0), (0, up_pad - up_cols)))
    conv_b_p = conv_b_p.transpose(1, 0, 2)

    def finish_gather(gathered):
        w_fox, w_sb, w_out_full, w_up_p, conv_w_p, w_down_full = gathered
        w_down_p = jnp.pad(w_down_full.reshape(half, up_cols, d), ((0, 0), (0, up_pad - up_cols), (0, 0)))
        return w_fox, w_sb, w_out_full, w_up_p, conv_w_p[:3], w_down_p.reshape(half * up_pad, d)

    early_names = ["w_o_fox", "w_o_sb", "w_out", "w_up", "conv_w", "conv_b", "w_down"]

    def early_exchange(g):
        sends = {
            "w_o_fox": ("cols", g["w_o_fox"], shard_shape["w_o_fox"]),
            "w_o_sb": ("cols", g["w_o_sb"], shard_shape["w_o_sb"]),
            "w_out": ("rows", g["w_out"], shard_shape["w_out"]),
            "w_up": ("gate_value", g["w_up"], (d, up_pad)),
            "conv_w": ("gate_value", pad_rows(g["conv_w"]), (SUBLANES, up_pad)),
            "conv_b": ("all", pad_rows(g["conv_b"].reshape(half, 2, up_pad).transpose(1, 0, 2)[:, :, :up_cols]
                                       .reshape(1, -1)),
                       (SUBLANES, N_DEV * up_cols)),
            "w_down": ("rows", g["w_down"].reshape(half, up_pad, d)[:, :up_cols].reshape(half * up_cols, d),
                       shard_shape["w_down"]),
        }
        return _Exchange([sends[n][1] for n in early_names], [sends[n][0] for n in early_names],
                         [sends[n][2] for n in early_names], gather=False)

    late_weights = (_Exchange([s[2] for s in later], [s[1] for s in later], [s[2].shape for s in later], gather=True),
                    finish_gather)
    loss, grad_x, grads = _local_step(
        x[0], loss_target[0], full["meta_tokens"], full["norm_gains"][:4], w_in_full, w2["b_forget"],
        None, None, None, None, None, conv_b_p.reshape(1, N_DEV * up_pad), None, ffn_block=up_pad,
        late_weights=late_weights, early_grads=early_exchange)
    loss = lax.psum(loss, ("x", "y", "c"))

    in_cols = shard_shape["w_in"][1]
    late_names = ["meta_tokens", "norm_gains", "w_in", "b_forget"]
    sends = {
        "meta_tokens": ("cols", grads["meta_tokens"], (N_META, LANES)),
        "norm_gains": ("cols", pad_rows(grads["norm_gains"]), (SUBLANES, LANES)),
        "w_in": ("slots", jnp.stack([grads["w_in"][:, i * in_cols:(i + 1) * in_cols] for i in range(N_DEV)]),
                 shard_shape["w_in"]),
        "b_forget": ("all", _pad2(grads["b_forget"], SUBLANES, LANES), (SUBLANES, LANES)),
    }
    parts = dict(zip(late_names, _exchange([sends[n][1] for n in late_names], [sends[n][0] for n in late_names],
                                           [sends[n][2] for n in late_names])))
    parts.update(zip(early_names, grads["early_parts"]))

    grad, delta, new_m, new_v = {}, {}, {}, {}
    for n in WEIGHTS:
        shape = w[n].shape
        outs = _sum_adamw(parts[n], w2[n], mom[n].reshape(shard_shape[n]), vel[n].reshape(shard_shape[n]),
                          "adamw_" + n)
        grad[n], delta[n], new_m[n], new_v[n] = (o.reshape(shape) for o in outs)

    return (loss, grad_x[None], *[grad[n] for n in WEIGHTS], *[delta[n] for n in WEIGHTS],
            *[new_m[n] for n in WEIGHTS], *[new_v[n] for n in WEIGHTS])
```

```python
import functools
import math

import jax
import jax.numpy as jnp
from jax import lax
from jax.experimental import pallas as pl
from jax.experimental.pallas import tpu as pltpu

F32 = jnp.float32
BF16 = jnp.bfloat16

N_DEV = 8
N_META = 16
HEAD_DIM = 64
N_HEADS = 8
WIDTH = N_HEADS * HEAD_DIM
N_PAIRS = N_HEADS // 2
LANES = 128
SUBLANES = 8
EPS = 1e-6
ATT_SCALE = HEAD_DIM ** -0.5
BLK = 256
F_PAD = 256
VMEM_LIMIT = 48 << 20

ADAM_LR = 0.001
ADAM_B1 = 0.9
ADAM_B2 = 0.999
ADAM_EPS = 1e-08
ADAM_WD = 0.01
ADAM_STEP = 10

GELU_C = math.sqrt(2.0 / math.pi)
GELU_A = 0.044715
EXP_IS_ZERO = -110.0
NOT_VISITED = -1e30


def _params(sem, vmem=VMEM_LIMIT):
    return pltpu.CompilerParams(dimension_semantics=sem, vmem_limit_bytes=vmem)


def _tile(dim, cap, align=LANES):
    t = (min(cap, dim) // align) * align
    while t >= align:
        if dim % t == 0:
            return t
        t -= align
    return dim


def _log_sigmoid_parts(z):
    lp = jnp.log1p(jnp.exp(-jnp.abs(z)))
    return jnp.minimum(z, 0.0) - lp, jnp.minimum(-z, 0.0) - lp


def _sigmoid(x):
    return 1.0 / (1.0 + jnp.exp(-x))


def _split_bf16(x):
    hi = x.astype(BF16)
    lo = (x - hi.astype(F32)).astype(BF16)
    return hi, lo


def _dot(a, b, dims):
    return lax.dot_general(a, b, (dims, ((), ())), preferred_element_type=F32)


NN = ((1,), (0,))
NT = ((1,), (1,))
TN = ((0,), (0,))


def _mm(a, b, mode, out_dtype, name):
    if mode == "nn":
        (m, kc), (_, n) = a.shape, b.shape
    elif mode == "nt":
        (m, kc), (n, _) = a.shape, b.shape
    else:
        (kc, m), (_, n) = a.shape, b.shape
    if mode == "tn":
        tm, tn, tk = _tile(m, 1024), _tile(n, 1792), _tile(kc, 1408)
    else:
        tm, tn, tk = _tile(m, 1408), _tile(n, 1024), _tile(kc, 1536)
    nk = kc // tk
    dims = {"nn": NN, "nt": NT, "tn": TN}[mode]

    def body(a_ref, b_ref, o_ref, *scratch):
        k = pl.program_id(2)
        part = _dot(a_ref[...].astype(BF16), b_ref[...].astype(BF16), dims)
        if nk == 1:
            o_ref[...] = part.astype(out_dtype)
            return
        acc_ref, = scratch

        @pl.when(k == 0)
        def _():
            acc_ref[...] = part

        @pl.when(k > 0)
        def _():
            acc_ref[...] += part

        @pl.when(k == nk - 1)
        def _():
            o_ref[...] = acc_ref[...].astype(out_dtype)

    if mode == "tn":
        a_spec = pl.BlockSpec((tk, tm), lambda j, i, k: (k, i))
    else:
        a_spec = pl.BlockSpec((tm, tk), lambda j, i, k: (i, k))
    if mode == "nt":
        b_spec = pl.BlockSpec((tn, tk), lambda j, i, k: (j, k))
    else:
        b_spec = pl.BlockSpec((tk, tn), lambda j, i, k: (k, j))
    return pl.pallas_call(
        body, name=name,
        out_shape=jax.ShapeDtypeStruct((m, n), out_dtype),
        grid=(n // tn, m // tm, nk),
        in_specs=[a_spec, b_spec],
        out_specs=pl.BlockSpec((tm, tn), lambda j, i, k: (i, j)),
        scratch_shapes=[pltpu.VMEM((tm, tn), F32)] if nk > 1 else [],
        compiler_params=_params(("parallel", "parallel", "arbitrary")),
    )(a, b)


def _rmsnorm_fwd(x, g, name):
    lp, d = x.shape

    def body(x_ref, g_ref, o_ref):
        xv = x_ref[...]
        r = lax.rsqrt(jnp.mean(xv * xv, axis=-1, keepdims=True) + EPS)
        o_ref[...] = ((xv * r) * g_ref[...]).astype(BF16)

    return pl.pallas_call(
        body, name=name,
        out_shape=jax.ShapeDtypeStruct((lp, d), BF16),
        grid=(lp // BLK,),
        in_specs=[pl.BlockSpec((BLK, d), lambda i: (i, 0)), pl.BlockSpec((1, d), lambda i: (0, 0))],
        out_specs=pl.BlockSpec((BLK, d), lambda i: (i, 0)),
        compiler_params=_params(("parallel",)),
    )(x, g)


def _rmsnorm_bwd(x, g, dy, resid, out_dtype, name):
    lp, d = x.shape
    has_resid = resid is not None

    def body(*refs):
        if has_resid:
            x_ref, g_ref, dy_ref, r_ref, dx_ref, dg_ref = refs
        else:
            x_ref, g_ref, dy_ref, dx_ref, dg_ref = refs
        i = pl.program_id(0)
        xv = x_ref[...]
        dyv = dy_ref[...].astype(F32)
        r = lax.rsqrt(jnp.mean(xv * xv, axis=-1, keepdims=True) + EPS)
        xh = xv * r
        dyg = dyv * g_ref[...]
        dx = r * (dyg - xh * jnp.mean(dyg * xh, axis=-1, keepdims=True))
        if has_resid:
            dx = dx + r_ref[...]
        dx_ref[...] = dx.astype(out_dtype)

        @pl.when(i == 0)
        def _():
            dg_ref[...] = jnp.zeros_like(dg_ref)

        dg_ref[...] += jnp.sum(dyv * xh, axis=0, keepdims=True)

    row = pl.BlockSpec((BLK, d), lambda i: (i, 0))
    vec = pl.BlockSpec((1, d), lambda i: (0, 0))
    ins = [x, g, dy] + ([resid] if has_resid else [])
    in_specs = [row, vec, row] + ([row] if has_resid else [])
    return pl.pallas_call(
        body, name=name,
        out_shape=(jax.ShapeDtypeStruct((lp, d), out_dtype), jax.ShapeDtypeStruct((1, d), F32)),
        grid=(lp // BLK,),
        in_specs=in_specs,
        out_specs=(row, vec),
        compiler_params=_params(("arbitrary",)),
    )(*ins)


def _forget_fwd(fpre, b_pad):
    lp = fpre.shape[0]

    def body(f_ref, b_ref, c_ref, carry_ref):
        i = pl.program_id(0)

        @pl.when(i == 0)
        def _():
            carry_ref[...] = jnp.zeros_like(carry_ref)

        logf, _ = _log_sigmoid_parts(f_ref[...] + b_ref[...])
        row = lax.broadcasted_iota(jnp.int32, (BLK, BLK), 0)
        col = lax.broadcasted_iota(jnp.int32, (BLK, BLK), 1)
        tri = (col <= row).astype(BF16)
        p0 = logf.astype(BF16)
        r1 = logf - p0.astype(F32)
        p1 = r1.astype(BF16)
        p2 = (r1 - p1.astype(F32)).astype(BF16)
        c = _dot(tri, p0, NN) + _dot(tri, p1, NN) + _dot(tri, p2, NN) + carry_ref[0:1, :]
        c_ref[...] = c
        carry_ref[...] = jnp.broadcast_to(c[BLK - 1:BLK, :], carry_ref.shape)

    return pl.pallas_call(
        body, name="forget_fwd",
        out_shape=jax.ShapeDtypeStruct((lp, LANES), F32),
        grid=(lp // BLK,),
        in_specs=[pl.BlockSpec((BLK, LANES), lambda i: (i, 0)), pl.BlockSpec((1, LANES), lambda i: (0, 0))],
        out_specs=pl.BlockSpec((BLK, LANES), lambda i: (i, 0)),
        scratch_shapes=[pltpu.VMEM((SUBLANES, LANES), F32)],
        compiler_params=_params(("arbitrary",)),
    )(fpre, b_pad)


def _forget_bwd(dc, fpre, b_pad):
    lp = fpre.shape[0]
    nb = lp // BLK

    def body(dc_ref, f_ref, b_ref, df_ref, db_ref, carry_ref):
        i = pl.program_id(0)

        @pl.when(i == 0)
        def _():
            carry_ref[...] = jnp.zeros_like(carry_ref)
            db_ref[...] = jnp.zeros_like(db_ref)

        dcv = dc_ref[...]
        row = lax.broadcasted_iota(jnp.int32, (BLK, BLK), 0)
        col = lax.broadcasted_iota(jnp.int32, (BLK, BLK), 1)
        tri = (col >= row).astype(BF16)
        p0 = dcv.astype(BF16)
        r1 = dcv - p0.astype(F32)
        p1 = r1.astype(BF16)
        p2 = (r1 - p1.astype(F32)).astype(BF16)
        dlogf = _dot(tri, p0, NN) + _dot(tri, p1, NN) + _dot(tri, p2, NN) + carry_ref[0:1, :]
        carry_ref[...] = jnp.broadcast_to(dlogf[0:1, :], carry_ref.shape)
        _, ls_neg = _log_sigmoid_parts(f_ref[...] + b_ref[...])
        df = dlogf * jnp.exp(ls_neg)
        df_ref[...] = df
        db_ref[...] += jnp.sum(df, axis=0, keepdims=True)

    rev = pl.BlockSpec((BLK, LANES), lambda i: (nb - 1 - i, 0))
    vec = pl.BlockSpec((1, LANES), lambda i: (0, 0))
    return pl.pallas_call(
        body, name="forget_bwd",
        out_shape=(jax.ShapeDtypeStruct((lp, LANES), F32), jax.ShapeDtypeStruct((1, LANES), F32)),
        grid=(nb,),
        in_specs=[rev, rev, vec],
        out_specs=(rev, vec),
        scratch_shapes=[pltpu.VMEM((SUBLANES, LANES), F32)],
        compiler_params=_params(("arbitrary",)),
    )(dc, fpre, b_pad)


def _mix_fwd(o_a, o_b, gates, h0, w_fox, w_sb, w_out, g1, g2):
    lp, d = h0.shape

    def body(oa_ref, ob_ref, ga_ref, gb_ref, h_ref, wf_ref, ws_ref, wo_ref, g_ref, g2_ref,
             h1_ref, ya_ref, yb_ref, gated_ref, mixed_ref, xn_ref):
        ya = _dot(oa_ref[...].astype(BF16), wf_ref[...], NN)
        yb = _dot(ob_ref[...].astype(BF16), ws_ref[...], NN)
        gated = _sigmoid(ga_ref[...]) * ya + _sigmoid(gb_ref[...]) * yb
        gb16 = gated.astype(BF16)
        mixed = _dot(gb16, wo_ref[...], NN)
        r = lax.rsqrt(jnp.mean(mixed * mixed, axis=-1, keepdims=True) + EPS)
        h1 = h_ref[...] + (mixed * r) * g_ref[...]
        h1_ref[...] = h1
        r2 = lax.rsqrt(jnp.mean(h1 * h1, axis=-1, keepdims=True) + EPS)
        xn_ref[...] = ((h1 * r2) * g2_ref[...]).astype(BF16)
        ya_ref[...] = ya
        yb_ref[...] = yb
        gated_ref[...] = gb16
        mixed_ref[...] = mixed

    row_w = pl.BlockSpec((BLK, WIDTH), lambda i: (i, 0))
    row_d = pl.BlockSpec((BLK, d), lambda i: (i, 0))
    full = lambda s: pl.BlockSpec(s, lambda i: (0, 0))
    return pl.pallas_call(
        body, name="mix_fwd",
        out_shape=(jax.ShapeDtypeStruct((lp, d), F32), jax.ShapeDtypeStruct((lp, d), F32),
                   jax.ShapeDtypeStruct((lp, d), F32), jax.ShapeDtypeStruct((lp, d), BF16),
                   jax.ShapeDtypeStruct((lp, d), F32), jax.ShapeDtypeStruct((lp, d), BF16)),
        grid=(lp // BLK,),
        in_specs=[row_w, row_w, row_d, pl.BlockSpec((BLK, d), lambda i: (i, 1)), row_d,
                  full((WIDTH, d)), full((WIDTH, d)), full((d, d)), full((1, d)), full((1, d))],
        out_specs=(row_d, row_d, row_d, row_d, row_d, row_d),
        compiler_params=_params(("parallel",)),
    )(o_a, o_b, gates, gates, h0, w_fox, w_sb, w_out, g1, g2)


def _gate_bwd(d_gated, gates, ya, yb):
    lp, d = d_gated.shape

    def body(dg_ref, ga_ref, gb_ref, ya_ref, yb_ref, dya_ref, dyb_ref, dga_ref, dgb_ref):
        dg = dg_ref[...]
        sa = _sigmoid(ga_ref[...])
        sb = _sigmoid(gb_ref[...])
        dya_ref[...] = (dg * sa).astype(BF16)
        dyb_ref[...] = (dg * sb).astype(BF16)
        dga_ref[...] = (dg * ya_ref[...] * (sa * (1.0 - sa))).astype(BF16)
        dgb_ref[...] = (dg * yb_ref[...] * (sb * (1.0 - sb))).astype(BF16)

    row = pl.BlockSpec((BLK, d), lambda i: (i, 0))
    out = jax.ShapeDtypeStruct((lp, d), BF16)
    return pl.pallas_call(
        body, name="gate_bwd",
        out_shape=(out, out, out, out),
        grid=(lp // BLK,),
        in_specs=[row, row, pl.BlockSpec((BLK, d), lambda i: (i, 1)), row, row],
        out_specs=(row, row, row, row),
        compiler_params=_params(("parallel",)),
    )(d_gated, gates, gates, ya, yb)


def _shift_down(cur, prev, n):
    rolled = pltpu.roll(cur, n, 0)
    row = lax.broadcasted_iota(jnp.int32, cur.shape, 0)
    for t in range(n):
        rolled = jnp.where(row == t, prev[SUBLANES - n + t:SUBLANES - n + t + 1, :], rolled)
    return rolled


def _shift_up(cur, nxt, n):
    rows = cur.shape[0]
    rolled = pltpu.roll(cur, rows - n, 0)
    row = lax.broadcasted_iota(jnp.int32, cur.shape, 0)
    for t in range(n):
        rolled = jnp.where(row == rows - n + t, nxt[t:t + 1, :], rolled)
    return rolled


def _gelu(x):
    return 0.5 * x * (1.0 + jnp.tanh(GELU_C * (x + GELU_A * (x * x * x))))


def _gelu_and_grad(x):
    t = jnp.tanh(GELU_C * (x + GELU_A * (x * x * x)))
    half = 0.5 * (1.0 + t)
    return x * half, half + 0.5 * x * (1.0 - t * t) * (GELU_C * (1.0 + 3.0 * GELU_A * (x * x)))


def _conv_taps(cur, prev, w_ref, b_ref):
    s1 = _shift_down(cur, prev, 1)
    s2 = _shift_down(cur, prev, 2)
    u = b_ref[...] + w_ref[0:1, :] * s2
    u = u + w_ref[1:2, :] * s1
    u = u + w_ref[2:3, :] * cur
    return u, s1, s2


def _conv_gelu_fwd(up, conv_w, conv_b, tc):
    lp, f2 = up.shape
    rb = BLK // SUBLANES

    def body(u_ref, p_ref, w_ref, b_ref, act_ref):
        i = pl.program_id(0)
        keep = (i > 0).astype(F32)
        u, _, _ = _conv_taps(u_ref[...], p_ref[...] * keep, w_ref, b_ref)
        act_ref[...] = (_gelu(u[:, :tc]) * u[:, tc:]).astype(BF16)

    prev_row = lambda i: jnp.maximum(i * rb - 1, 0)
    return pl.pallas_call(
        body, name="conv_gelu_fwd",
        out_shape=jax.ShapeDtypeStruct((lp, f2 // 2), BF16),
        grid=(lp // BLK, f2 // (2 * tc)),
        in_specs=[pl.BlockSpec((BLK, 2 * tc), lambda i, j: (i, j)),
                  pl.BlockSpec((SUBLANES, 2 * tc), lambda i, j: (prev_row(i), j)),
                  pl.BlockSpec((3, 2 * tc), lambda i, j: (0, j)),
                  pl.BlockSpec((1, 2 * tc), lambda i, j: (0, j))],
        out_specs=pl.BlockSpec((BLK, tc), lambda i, j: (i, j)),
        compiler_params=_params(("parallel", "parallel")),
    )(up, up, conv_w, conv_b)


def _conv_gelu_bwd(up, d_act, conv_w, conv_b, tc):
    lp, f2 = up.shape
    nb = lp // BLK
    rb = BLK // SUBLANES

    def du_of(u, da):
        gel, grad = _gelu_and_grad(u[:, :tc])
        return jnp.concatenate([da * u[:, tc:] * grad, da * gel], axis=1)

    def body(u_ref, p_ref, n_ref, da_ref, dan_ref, w_ref, b_ref, dup_ref, dcw_ref, dcb_ref):
        i = pl.program_id(1)
        cur = u_ref[...]
        u, s1, s2 = _conv_taps(cur, p_ref[...] * (i > 0).astype(F32), w_ref, b_ref)
        du = du_of(u, da_ref[...])
        u_next, _, _ = _conv_taps(n_ref[...], cur[BLK - SUBLANES:BLK, :], w_ref, b_ref)
        du_next = du_of(u_next, dan_ref[...]) * (i < nb - 1).astype(F32)
        n1 = _shift_up(du, du_next, 1)
        n2 = _shift_up(du, du_next, 2)
        dup_ref[...] = (w_ref[2:3, :] * du + w_ref[1:2, :] * n1 + w_ref[0:1, :] * n2).astype(BF16)

        @pl.when(i == 0)
        def _():
            dcw_ref[...] = jnp.zeros_like(dcw_ref)
            dcb_ref[...] = jnp.zeros_like(dcb_ref)

        dcw_ref[0:1, :] += jnp.sum(du * s2, axis=0, keepdims=True)
        dcw_ref[1:2, :] += jnp.sum(du * s1, axis=0, keepdims=True)
        dcw_ref[2:3, :] += jnp.sum(du * cur, axis=0, keepdims=True)
        dcb_ref[...] += jnp.sum(du, axis=0, keepdims=True)

    prev_row = lambda i: jnp.maximum(i * rb - 1, 0)
    next_row = lambda i: jnp.minimum((i + 1) * rb, nb * rb - 1)
    return pl.pallas_call(
        body, name="conv_gelu_bwd",
        out_shape=(jax.ShapeDtypeStruct((lp, f2), BF16), jax.ShapeDtypeStruct((3, f2), F32),
                   jax.ShapeDtypeStruct((1, f2), F32)),
        grid=(f2 // (2 * tc), nb),
        in_specs=[pl.BlockSpec((BLK, 2 * tc), lambda j, i: (i, j)),
                  pl.BlockSpec((SUBLANES, 2 * tc), lambda j, i: (prev_row(i), j)),
                  pl.BlockSpec((SUBLANES, 2 * tc), lambda j, i: (next_row(i), j)),
                  pl.BlockSpec((BLK, tc), lambda j, i: (i, j)),
                  pl.BlockSpec((SUBLANES, tc), lambda j, i: (next_row(i), j)),
                  pl.BlockSpec((3, 2 * tc), lambda j, i: (0, j)),
                  pl.BlockSpec((1, 2 * tc), lambda j, i: (0, j))],
        out_specs=(pl.BlockSpec((BLK, 2 * tc), lambda j, i: (i, j)),
                   pl.BlockSpec((3, 2 * tc), lambda j, i: (0, j)),
                   pl.BlockSpec((1, 2 * tc), lambda j, i: (0, j))),
        compiler_params=_params(("parallel", "arbitrary")),
    )(up, up, up, d_act, d_act, conv_w, conv_b)


def _out_loss(h1, ffn, g3, target, n_valid):
    lp, d = h1.shape

    def body(h_ref, f_ref, g_ref, t_ref, dy_ref, loss_ref, df_ref, dg_ref):
        i = pl.program_id(0)

        @pl.when(i == 0)
        def _():
            loss_ref[...] = jnp.zeros_like(loss_ref)
            dg_ref[...] = jnp.zeros_like(dg_ref)

        fv = f_ref[...]
        r = lax.rsqrt(jnp.mean(fv * fv, axis=-1, keepdims=True) + EPS)
        xh = fv * r
        y = h_ref[...] + xh * g_ref[...]
        row = i * BLK + lax.broadcasted_iota(jnp.int32, (BLK, 1), 0)
        valid = (row >= N_META) & (row < n_valid)
        diff = jnp.where(valid, y - t_ref[...], 0.0)
        dy = diff * (1.0 / d)
        dy_ref[...] = dy
        per_row = jnp.mean(diff * diff, axis=-1, keepdims=True)
        loss_ref[...] += 0.5 * jnp.sum(per_row, axis=0, keepdims=True)
        dyg = dy * g_ref[...]
        df_ref[...] = (r * (dyg - xh * jnp.mean(dyg * xh, axis=-1, keepdims=True))).astype(BF16)
        dg_ref[...] += jnp.sum(dy * xh, axis=0, keepdims=True)

    row_d = pl.BlockSpec((BLK, d), lambda i: (i, 0))
    vec = pl.BlockSpec((1, d), lambda i: (0, 0))
    return pl.pallas_call(
        body, name="out_loss",
        out_shape=(jax.ShapeDtypeStruct((lp, d), F32), jax.ShapeDtypeStruct((SUBLANES, LANES), F32),
                   jax.ShapeDtypeStruct((lp, d), BF16), jax.ShapeDtypeStruct((1, d), F32)),
        grid=(lp // BLK,),
        in_specs=[row_d, row_d, vec, row_d],
        out_specs=(row_d, pl.BlockSpec((SUBLANES, LANES), lambda i: (0, 0)), row_d, vec),
        compiler_params=_params(("arbitrary",)),
    )(h1, ffn, g3, target)


def _head_masks(rows=BLK):
    lane = lax.broadcasted_iota(jnp.int32, (rows, LANES), 1)
    return [lane < HEAD_DIM, lane >= HEAD_DIM]


def _att_specs(base, lp):
    q_spec = pl.BlockSpec((BLK, LANES), lambda p, i: (i, base + p))
    k_spec = pl.BlockSpec((lp, LANES), lambda p, i: (0, base + N_PAIRS + p))
    v_spec = pl.BlockSpec((lp, LANES), lambda p, i: (0, base + 2 * N_PAIRS + p))
    return q_spec, k_spec, v_spec


def _kv_rows(j, nb=1):
    return pl.ds(pl.multiple_of(j * BLK, nb * BLK), nb * BLK)


def _walk_kv(i, tile, reverse=False, first_pair=0, more=None):
    pairs, odd = i // 2, i % 2

    def wide(t, carry):
        tile(2 * (pairs - 1 - t) if reverse else 2 * t, 2, False)
        return carry

    def single():
        @pl.when(odd == 1)
        def _():
            tile(i - 1, 1, False)

    if reverse:
        tile(i, 1, True)
        single()
        if more is None:
            lax.fori_loop(0, pairs, wide, 0)
        else:
            lax.while_loop(lambda c: (c[0] < pairs) & c[1], lambda c: (wide(c[0], c[0]) + 1, more()), (0, more()))
    else:
        lax.fori_loop(first_pair, pairs, wide, 0)
        single()
        tile(i, 1, True)


def _walk_kv_ahead(i, lead, follow, kept, first_pair=0):
    pairs, odd = i // 2, i % 2

    def keep(values):
        for ref, value in zip(kept, values):
            ref[...] = value

    @pl.when(pairs > first_pair)
    def _():
        keep(lead(2 * first_pair, 2))

    def wide(t, carry):
        ahead = lead(2 * jnp.minimum(t + 1, pairs - 1), 2)
        follow([ref[...] for ref in kept], 2 * t, 2, False)
        keep(ahead)
        return carry

    lax.fori_loop(first_pair, pairs, wide, 0)

    @pl.when(odd == 1)
    def _():
        follow(lead(i - 1, 1), i - 1, 1, False)

    follow(lead(i, 1), i, 1, True)


def _bf16_pieces(x):
    rnd = lambda a: lax.reduce_precision(a, exponent_bits=8, mantissa_bits=7)
    p0 = rnd(x)
    p1 = rnd(x - p0)
    p2 = rnd(x - p0 - p1)
    return [p0.astype(BF16), p1.astype(BF16), p2.astype(BF16)]


def _aug_lanes(cols):
    lp = cols[0].shape[0]
    vals = jnp.stack([c.astype(BF16) for c in cols], axis=-1)
    vals = vals.reshape(lp, N_PAIRS, 2, len(cols))[:, :, ::-1, :]
    vals = jnp.pad(vals, ((0, 0), (0, 0), (0, 0), (0, HEAD_DIM - len(cols))))
    return vals.reshape(lp, WIDTH)


N_AUG = 3


def _riding(exchange, n_in, n_out, n_scratch, grid):
    n = exchange.n if exchange else 0

    def split(refs):
        own_in, ex_in = refs[:n_in], refs[n_in:n_in + n]
        own_out, ex_out = refs[n_in + n:n_in + n + n_out], refs[n_in + n + n_out:n_in + 2 * n + n_out]
        rest = refs[n_in + 2 * n + n_out:]
        return own_in + own_out + rest[:n_scratch], (ex_in, ex_out, rest[n_scratch:])

    def first_step(ex_refs):
        if exchange:
            @pl.when((pl.program_id(0) == 0) & (pl.program_id(1) == 0))
            def _():
                exchange.start(*ex_refs)

    def last_step(ex_refs):
        if exchange:
            @pl.when((pl.program_id(0) == grid[0] - 1) & (pl.program_id(1) == grid[1] - 1))
            def _():
                exchange.wait(*ex_refs)

    return split, first_step, last_step


def _fox_fwd(qkv, qaug, kaug, exchange=None):
    lp = qkv.shape[0]
    nq = lp // BLK
    split, first_step, last_step = _riding(exchange, 5, 2, 3, (N_PAIRS, nq))

    def body(*refs):
        (q_ref, k_ref, v_ref, qa_ref, ka_ref, o_ref, lse_ref, acc_ref, m_ref, s_ref), ex_refs = split(refs)
        first_step(ex_refs)
        i = pl.program_id(1)
        hs = range(2)
        masks = _head_masks()
        qs = q_ref[...] * ATT_SCALE
        qa = qa_ref[...]
        qh = [jnp.where(masks[hh], qs, qa) for hh in hs]
        acc_ref[...] = jnp.zeros_like(acc_ref)
        m_ref[...] = jnp.full_like(m_ref, -1e30)
        row = lax.broadcasted_iota(jnp.int32, (BLK, BLK), 0)
        col = lax.broadcasted_iota(jnp.int32, (BLK, BLK), 1)
        causal = col <= row

        def scores(j, nb):
            rows = _kv_rows(j, nb)
            kmasks = _head_masks(nb * BLK)
            k, ka = k_ref[rows, :], ka_ref[rows, :]
            return [_dot(qh[hh], jnp.where(kmasks[hh], k, ka), NT) for hh in hs]

        def absorb(s, j, nb, diag):
            v = v_ref[_kv_rows(j, nb), :]
            kmasks = _head_masks(nb * BLK)
            vh = [jnp.where(kmasks[hh], v, jnp.ones_like(v)) for hh in hs]
            if diag:
                s = [jnp.where(causal, s[hh], -1e30) for hh in hs]
            m_prev = [m_ref[hh] for hh in hs]
            m_new = [jnp.maximum(m_prev[hh], jnp.max(s[hh], axis=-1, keepdims=True)) for hh in hs]
            p = [jnp.exp(s[hh] - m_new[hh]).astype(BF16) for hh in hs]
            for hh in hs:
                acc_ref[hh] = jnp.exp(m_prev[hh] - m_new[hh]) * acc_ref[hh] + _dot(p[hh], vh[hh], NN)
                m_ref[hh] = m_new[hh]

        _walk_kv_ahead(i, scores, absorb, [s_ref.at[hh] for hh in hs], _first_pair(qa, i))
        acc = [acc_ref[hh] for hh in hs]
        denom = [acc[0][:, HEAD_DIM:HEAD_DIM + 1], acc[1][:, 0:1]]
        o_ref[...] = jnp.where(masks[0], acc[0] / denom[0], acc[1] / denom[1])
        lse = jnp.where(masks[0], m_ref[0] + jnp.log(denom[0]), m_ref[1] + jnp.log(denom[1])).T
        lse_ref[0, 0, 0:1, :] = lse[0:1, :]
        lse_ref[0, 0, 1:2, :] = lse[HEAD_DIM:HEAD_DIM + 1, :]
        last_step(ex_refs)

    q_spec, k_spec, v_spec = _att_specs(0, lp)
    blk = pl.BlockSpec((BLK, LANES), lambda p, i: (i, p))
    col_full = pl.BlockSpec((lp, LANES), lambda p, i: (0, p))
    out = jax.ShapeDtypeStruct((lp, WIDTH), F32)
    ex = exchange
    return pl.pallas_call(
        body, name="fox_fwd",
        out_shape=(out, jax.ShapeDtypeStruct((N_PAIRS, nq, 2, BLK), F32)) + tuple(ex.out_shapes if ex else ()),
        grid=(N_PAIRS, nq),
        in_specs=[q_spec, k_spec, v_spec, blk, col_full] + (ex.any_specs if ex else []),
        out_specs=(blk, pl.BlockSpec((1, 1, 2, BLK), lambda p, i: (p, i, 0, 0))) + tuple(ex.any_specs if ex else ()),
        scratch_shapes=[pltpu.VMEM((2, BLK, LANES), F32), pltpu.VMEM((2, BLK, 1), F32),
                        pltpu.VMEM((2, BLK, 2 * BLK), F32)] + (ex.scratch if ex else []),
        compiler_params=_params(("arbitrary", "arbitrary") if ex else ("parallel", "parallel")),
    )(qkv, qkv, qkv, qaug, kaug, *(ex.arrays if ex else []))


FIRST_BLOCK_LANE = 9


def _first_pair(qa, i):
    lane = lax.broadcasted_iota(jnp.int32, qa.shape, 1)
    first = jnp.max(jnp.where(lane == HEAD_DIM + FIRST_BLOCK_LANE, qa.astype(F32), 0.0)).astype(jnp.int32)
    return jnp.clip(first, 0, i) // 2


def _fox_first_blocks(qkv, c):
    lp = qkv.shape[0]
    nq = lp // BLK

    def body(x_ref, o_ref):
        xv = x_ref[...].astype(F32)
        sq = (xv * xv).astype(BF16)
        col = lax.broadcasted_iota(jnp.int32, (2 * WIDTH, LANES), 0)
        lane = lax.broadcasted_iota(jnp.int32, (2 * WIDTH, LANES), 1)
        pick = (col // HEAD_DIM == lane).astype(BF16)
        o_ref[...] = _dot(sq, pick, NN)

    norms = pl.pallas_call(
        body, name="fox_norms",
        out_shape=jax.ShapeDtypeStruct((lp, LANES), F32),
        grid=(nq,),
        in_specs=[pl.BlockSpec((BLK, 2 * WIDTH), lambda i: (i, 0))],
        out_specs=pl.BlockSpec((BLK, LANES), lambda i: (i, 0)),
        compiler_params=_params(("parallel",)),
    )(qkv)
    a_max = 1.02 * jnp.sqrt(norms[:, :N_HEADS].reshape(nq, BLK, N_HEADS).max(axis=1))
    b_max = 1.02 * jnp.sqrt(norms[:, N_HEADS:2 * N_HEADS].reshape(nq, BLK, N_HEADS).max(axis=1))
    c_max = c.reshape(nq, BLK, N_HEADS).max(axis=1)
    c_min = c.reshape(nq, BLK, N_HEADS).min(axis=1)
    bound = (a_max[:, None] * (b_max[None, :] + b_max[:, None]) * ATT_SCALE + c_max[:, None] - c_min[None, :])
    alive = (bound > EXP_IS_ZERO) | jnp.isnan(bound)
    alive = alive.reshape(nq, nq, N_PAIRS, 2).any(axis=-1)
    first = jnp.argmax(alive, axis=1).astype(F32)
    return jnp.repeat(jnp.repeat(first, 2, axis=1), BLK, axis=0)


def _head_dots(a, b):
    lp = a.shape[0]

    def body(a_ref, b_ref, o_ref):
        lane = lax.broadcasted_iota(jnp.int32, (BLK, LANES), 1)
        out = jnp.zeros((BLK, LANES), F32)
        for p in range(N_PAIRS):
            cols = slice(p * LANES, (p + 1) * LANES)
            prod = a_ref[:, cols] * b_ref[:, cols]
            for hh in range(2):
                part = jnp.where((lane >= HEAD_DIM) == (hh == 1), prod, 0.0)
                out = jnp.where(lane == 2 * p + hh, jnp.sum(part, axis=-1, keepdims=True), out)
        o_ref[...] = out

    row = pl.BlockSpec((BLK, WIDTH), lambda i: (i, 0))
    return pl.pallas_call(
        body, name="head_dots",
        out_shape=jax.ShapeDtypeStruct((lp, LANES), F32),
        grid=(lp // BLK,),
        in_specs=[row, row],
        out_specs=pl.BlockSpec((BLK, LANES), lambda i: (i, 0)),
        compiler_params=_params(("parallel",)),
    )(a, b)


def _fox_bwd(qkv, qaug, kaug, doaug, d_o, exchange=None):
    lp = qkv.shape[0]
    nq = lp // BLK
    split, first_step, last_step = _riding(exchange, 7, 4, 2, (N_PAIRS, nq))

    def body(*refs):
        (q_ref, k_ref, v_ref, qa_ref, ka_ref, da_ref, do_ref,
         dq_ref, dk_ref, dv_ref, dc_ref, acc_ref, rs_ref), ex_refs = split(refs)
        first_step(ex_refs)
        i = pl.program_id(1)
        hs = range(2)
        masks = _head_masks()
        qs = q_ref[...] * ATT_SCALE
        qa = qa_ref[...]
        qm = [jnp.where(masks[hh], qs, 0) for hh in hs]
        qh = [jnp.where(masks[hh], qs, qa) for hh in hs]
        dov = do_ref[...].astype(BF16)
        doa = da_ref[...]
        dom = [jnp.where(masks[hh], dov, 0) for hh in hs]
        doh = [jnp.where(masks[hh], dov, doa) for hh in hs]
        row = lax.broadcasted_iota(jnp.int32, (BLK, BLK), 0)
        col = lax.broadcasted_iota(jnp.int32, (BLK, BLK), 1)
        causal = col <= row

        @pl.when(i == 0)
        def _():
            dk_ref[...] = jnp.zeros_like(dk_ref)
            dv_ref[...] = jnp.zeros_like(dv_ref)
            dc_ref[...] = jnp.zeros_like(dc_ref)

        acc_ref[...] = jnp.zeros_like(acc_ref)
        rs_ref[...] = jnp.zeros_like(rs_ref)

        def lead(j, nb):
            rows = _kv_rows(j, nb)
            kmasks = _head_masks(nb * BLK)
            k, ka, v = k_ref[rows, :], ka_ref[rows, :], v_ref[rows, :]
            ones = (lax.broadcasted_iota(jnp.int32, v.shape, 1) % HEAD_DIM < N_AUG).astype(BF16)
            logp = [_dot(qh[hh], jnp.where(kmasks[hh], k, ka), NT) for hh in hs]
            dp = [_dot(doh[hh], jnp.where(kmasks[hh], v, ones), NT) for hh in hs]
            return logp + dp

        def follow(lead_out, j, nb, diag):
            rows = _kv_rows(j, nb)
            k = k_ref[rows, :]
            logp, dp = lead_out[:2], lead_out[2:]
            p = [jnp.exp(logp[hh]) for hh in hs]
            if diag:
                p = [jnp.where(causal, p[hh], 0.0) for hh in hs]
            ds = [p[hh] * dp[hh] for hh in hs]
            dsb = [ds[hh].astype(BF16) for hh in hs]
            for hh in hs:
                acc_ref[hh] += _dot(dsb[hh], k, NN)
                col_sums = jnp.sum(ds[hh], axis=0, keepdims=True)
                for b in range(nb):
                    dc_ref[0, j + b, hh:hh + 1, :] -= col_sums[:, b * BLK:(b + 1) * BLK]
                rs_ref[hh] += jnp.sum(ds[hh], axis=-1, keepdims=True)
            dk_ref[rows, :] += _dot(dsb[0], qm[0], TN) + _dot(dsb[1], qm[1], TN)
            dv_ref[rows, :] += _dot(p[0].astype(BF16), dom[0], TN) + _dot(p[1].astype(BF16), dom[1], TN)

        _walk_kv(i, lambda j, nb, diag: follow(lead(j, nb), j, nb, diag), first_pair=_first_pair(qa, i))
        dq_ref[...] = (jnp.where(masks[0], acc_ref[0], acc_ref[1]) * ATT_SCALE).astype(BF16)
        row_sums = jnp.where(masks[0], rs_ref[0], rs_ref[1]).T
        dc_ref[0, i, 0:1, :] += row_sums[0:1, :]
        dc_ref[0, i, 1:2, :] += row_sums[HEAD_DIM:HEAD_DIM + 1, :]
        last_step(ex_refs)

    q_spec, k_spec, v_spec = _att_specs(0, lp)
    blk = pl.BlockSpec((BLK, LANES), lambda p, i: (i, p))
    col_full = pl.BlockSpec((lp, LANES), lambda p, i: (0, p))
    crow_spec = pl.BlockSpec((1, nq, 2, BLK), lambda p, i: (p, 0, 0, 0))
    ex = exchange
    return pl.pallas_call(
        body, name="fox_bwd",
        out_shape=(jax.ShapeDtypeStruct((lp, WIDTH), BF16), jax.ShapeDtypeStruct((lp, WIDTH), F32),
                   jax.ShapeDtypeStruct((lp, WIDTH), F32), jax.ShapeDtypeStruct((N_PAIRS, nq, 2, BLK), F32))
        + tuple(ex.out_shapes if ex else ()),
        grid=(N_PAIRS, nq),
        in_specs=[q_spec, k_spec, v_spec, blk, col_full, blk, blk] + (ex.any_specs if ex else []),
        out_specs=(blk, col_full, col_full, crow_spec) + tuple(ex.any_specs if ex else ()),
        scratch_shapes=[pltpu.VMEM((2, BLK, LANES), F32), pltpu.VMEM((2, BLK, 1), F32)] + (ex.scratch if ex else []),
        compiler_params=_params(("arbitrary", "arbitrary") if ex else ("parallel", "arbitrary")),
    )(qkv, qkv, qkv, qaug, kaug, doaug, d_o, *(ex.arrays if ex else []))


def _sb_scores(z):
    ell = jnp.minimum(z, 0.0) - jnp.log(1.0 + jnp.exp(-jnp.abs(z)))
    return ell, ell - z


def _stacked(tri):
    return jnp.concatenate([tri, tri], axis=0)


def _cumsum_dot(x, tri2):
    hi, lo = _split_bf16(x)
    return _dot(jnp.concatenate([hi, lo], axis=1), tri2, NN)


def _sb_units(qh, k, strict2, causal, nb, diag):
    hs, bs = range(2), range(nb)
    z = [_dot(qh[hh], k, NT) for hh in hs]
    sc = [[_sb_scores(z[hh][:, b * BLK:(b + 1) * BLK]) for b in bs] for hh in hs]
    ell = [[sc[hh][b][0] for b in bs] for hh in hs]
    kap = [[jnp.where(causal, sc[hh][b][1], 0.0) if diag else sc[hh][b][1] for b in bs] for hh in hs]
    later = [[_cumsum_dot(kap[hh][b], strict2) for b in bs] for hh in hs]
    return ell, kap, later


def _row_sum(x):
    return jnp.sum(x, axis=-1, keepdims=True)


def _join(blocks):
    joined = blocks[0] if len(blocks) == 1 else jnp.concatenate(blocks, axis=1)
    return joined.astype(BF16)


def _sb_fwd(qkv):
    lp = qkv.shape[0]
    nq = lp // BLK
    assert nq <= HEAD_DIM

    def body(q_ref, k_ref, v_ref, o_ref, lc_ref, acc_ref, car_ref):
        i = pl.program_id(1)
        masks = _head_masks()
        qs = q_ref[...] * ATT_SCALE
        qh = [jnp.where(mk, qs, 0).astype(BF16) for mk in masks]
        row = lax.broadcasted_iota(jnp.int32, (BLK, BLK), 0)
        col = lax.broadcasted_iota(jnp.int32, (BLK, BLK), 1)
        lane = lax.broadcasted_iota(jnp.int32, (BLK, LANES), 1)
        causal = col < row
        strict2 = _stacked((row > col).astype(BF16))
        acc_ref[...] = jnp.zeros_like(acc_ref)
        car_ref[...] = jnp.zeros_like(car_ref)
        lc_ref[...] = jnp.full_like(lc_ref, NOT_VISITED)

        def tile(j, nb, diag):
            hs, bs = range(2), range(nb)
            rows = _kv_rows(j, nb)
            k, v = k_ref[rows, :], v_ref[rows, :]
            ell, kap, later = _sb_units(qh, k, strict2, causal, nb, diag)
            car = [[None] * nb for _ in hs]
            for hh in hs:
                run = car_ref[hh]
                for b in reversed(bs):
                    car[hh][b] = run
                    run = run + _row_sum(kap[hh][b])
                car_ref[hh] = run
            if not diag:
                kept = lc_ref[...]
                for hh in hs:
                    for b in bs:
                        kept = jnp.where(lane == j + b + HEAD_DIM * hh, car[hh][b], kept)
                lc_ref[...] = kept
            a = [[jnp.exp(ell[hh][b] + later[hh][b] + car[hh][b]) for b in bs] for hh in hs]
            if diag:
                a = [[jnp.where(causal, a[hh][b], 0.0) for b in bs] for hh in hs]
            for hh in hs:
                acc_ref[hh] += _dot(_join(a[hh]), v, NN)

        _walk_kv(i, tile, reverse=True, more=lambda: jnp.max(car_ref[...]) > EXP_IS_ZERO)
        o_ref[...] = jnp.where(masks[0], acc_ref[0], acc_ref[1])

    q_spec, k_spec, v_spec = _att_specs(3 * N_PAIRS, lp)
    blk = pl.BlockSpec((BLK, LANES), lambda p, i: (i, p))
    out = jax.ShapeDtypeStruct((lp, WIDTH), F32)
    return pl.pallas_call(
        body, name="sb_fwd",
        out_shape=(out, out),
        grid=(N_PAIRS, nq),
        in_specs=[q_spec, k_spec, v_spec],
        out_specs=(blk, blk),
        scratch_shapes=[pltpu.VMEM((2, BLK, LANES), F32), pltpu.VMEM((2, BLK, 1), F32)],
        compiler_params=_params(("parallel", "parallel")),
    )(qkv, qkv, qkv)


def _sb_bwd(qkv, lcar, d_o):
    lp = qkv.shape[0]
    nq = lp // BLK

    def body(q_ref, k_ref, v_ref, lc_ref, do_ref, dq_ref, dk_ref, dv_ref, acc_ref, cg_ref):
        i = pl.program_id(1)
        masks = _head_masks()
        qs = q_ref[...] * ATT_SCALE
        qh = [jnp.where(mk, qs, 0).astype(BF16) for mk in masks]
        dov = do_ref[...]
        doh = [jnp.where(mk, dov, 0.0).astype(BF16) for mk in masks]
        lcv = lc_ref[...]
        lane_row = lax.broadcasted_iota(jnp.int32, (1, LANES), 1)
        alive = jnp.where(jnp.max(lcv, axis=0, keepdims=True) > EXP_IS_ZERO, 1.0, 0.0)
        n_alive = jnp.maximum(jnp.sum(jnp.where(lane_row < HEAD_DIM, alive, 0.0)),
                              jnp.sum(jnp.where(lane_row >= HEAD_DIM, alive, 0.0))).astype(jnp.int32)
        first_pair = jnp.maximum(i - n_alive, 0) // 2
        row = lax.broadcasted_iota(jnp.int32, (BLK, BLK), 0)
        col = lax.broadcasted_iota(jnp.int32, (BLK, BLK), 1)
        lane = lax.broadcasted_iota(jnp.int32, (BLK, LANES), 1)
        causal = col < row
        strict2 = _stacked((row > col).astype(BF16))
        before2 = _stacked((row < col).astype(BF16))

        @pl.when(i == 0)
        def _():
            dk_ref[...] = jnp.zeros_like(dk_ref)
            dv_ref[...] = jnp.zeros_like(dv_ref)

        acc_ref[...] = jnp.zeros_like(acc_ref)
        cg_ref[...] = jnp.zeros_like(cg_ref)

        def tile(j, nb, diag):
            hs, bs = range(2), range(nb)
            rows = _kv_rows(j, nb)
            k, v = k_ref[rows, :], v_ref[rows, :]
            if diag:
                car = [[0.0] for _ in hs]
            else:
                car = [[_row_sum(jnp.where(lane == j + b + HEAD_DIM * hh, lcv, 0.0)) for b in bs] for hh in hs]
            da = [_dot(doh[hh], v, NT) for hh in hs]
            ell, _, later = _sb_units(qh, k, strict2, causal, nb, diag)
            a = [[jnp.exp(ell[hh][b] + later[hh][b] + car[hh][b]) for b in bs] for hh in hs]
            if diag:
                a = [[jnp.where(causal, a[hh][b], 0.0) for b in bs] for hh in hs]
            g = [[da[hh][:, b * BLK:(b + 1) * BLK] * a[hh][b] for b in bs] for hh in hs]
            cg = [[_cumsum_dot(g[hh][b], before2) for b in bs] for hh in hs]
            before = [[None] * nb for _ in hs]
            for hh in hs:
                run = cg_ref[hh]
                for b in bs:
                    before[hh][b] = run
                    run = run + _row_sum(g[hh][b])
                cg_ref[hh] = run
            dz = [[g[hh][b] - jnp.exp(ell[hh][b]) * (g[hh][b] + cg[hh][b] + before[hh][b]) for b in bs] for hh in hs]
            if diag:
                dz = [[jnp.where(causal, dz[hh][b], 0.0) for b in bs] for hh in hs]
            dzb = [_join(dz[hh]) for hh in hs]
            ab = [_join(a[hh]) for hh in hs]
            for hh in hs:
                acc_ref[hh] += _dot(dzb[hh], k, NN)
            dk_ref[rows, :] += _dot(dzb[0], qh[0], TN) + _dot(dzb[1], qh[1], TN)
            dv_ref[rows, :] += _dot(ab[0], doh[0], TN) + _dot(ab[1], doh[1], TN)

        _walk_kv(i, tile, first_pair=first_pair)
        dq_ref[...] = (jnp.where(masks[0], acc_ref[0], acc_ref[1]) * ATT_SCALE).astype(BF16)

    q_spec, k_spec, v_spec = _att_specs(3 * N_PAIRS, lp)
    blk = pl.BlockSpec((BLK, LANES), lambda p, i: (i, p))
    col_full = pl.BlockSpec((lp, LANES), lambda p, i: (0, p))
    return pl.pallas_call(
        body, name="sb_bwd",
        out_shape=(jax.ShapeDtypeStruct((lp, WIDTH), BF16), jax.ShapeDtypeStruct((lp, WIDTH), F32),
                   jax.ShapeDtypeStruct((lp, WIDTH), F32)),
        grid=(N_PAIRS, nq),
        in_specs=[q_spec, k_spec, v_spec, blk, blk],
        out_specs=(blk, col_full, col_full),
        scratch_shapes=[pltpu.VMEM((2, BLK, LANES), F32), pltpu.VMEM((2, BLK, 1), F32)],
        compiler_params=_params(("parallel", "arbitrary")),
    )(qkv, qkv, qkv, lcar, d_o)


def _local_step(x, target, meta, gains, w_in, b_forget, w_fox, w_sb, w_out, w_up, conv_w, conv_b, w_down,
                ffn_block=None, late_weights=None, early_grads=None):
    s, d = x.shape
    n_valid = N_META + s
    lp = -(-n_valid // BLK) * BLK
    pad = lp - n_valid
    nq = lp // BLK

    h0 = jnp.concatenate([meta, x, jnp.zeros((pad, d), F32)], axis=0)
    tgt = jnp.concatenate([jnp.zeros((N_META, d), F32), target, jnp.zeros((pad, d), F32)], axis=0)

    q_a, k_a, v_a, f_a, q_b, k_b, v_b, g_a, g_b = jnp.split(
        w_in, [512, 1024, 1536, 1544, 2056, 2568, 3080, 4104], axis=1)
    w_qkv = jnp.concatenate([q_a, k_a, v_a, q_b, k_b, v_b], axis=1)
    w_gf = jnp.concatenate([g_a, g_b, f_a, jnp.zeros((d, F_PAD - N_HEADS), BF16)], axis=1)
    b_pad = jnp.concatenate([b_forget.reshape(1, N_HEADS), jnp.zeros((1, LANES - N_HEADS), F32)], axis=1)
    g0, g1, g2, g3 = (gains[i:i + 1] for i in range(4))

    xn1 = _rmsnorm_fwd(h0, g0, "norm1_fwd")
    qkv = _mm(xn1, w_qkv, "nn", BF16, "proj_qkv")
    gf = _mm(xn1, w_gf, "nn", F32, "proj_gates")
    fpre = gf[:, 2 * d:2 * d + LANES]
    c = _forget_fwd(fpre, b_pad)[:, :N_HEADS]
    c_pieces = _bf16_pieces(c)
    one = jnp.ones((lp, N_HEADS), BF16)
    kaug = _aug_lanes(3 * [one] + [-x for x in c_pieces] + 3 * [one])
    first_block = _fox_first_blocks(qkv, c)
    o_a, lse, *gathered = _fox_fwd(qkv, _aug_lanes(c_pieces + 3 * [one] + 3 * [0 * one] + [first_block]), kaug,
                                   late_weights[0] if late_weights else None)
    if late_weights:
        w_fox, w_sb, w_out, w_up, conv_w, w_down = late_weights[1](gathered)
    ffn_block = ffn_block or w_up.shape[1] // 2
    o_b, lcar = _sb_fwd(qkv)
    h1, ya, yb, gated, mixed, xn3 = _mix_fwd(o_a, o_b, gf, h0, w_fox, w_sb, w_out, g1, g2)
    up = _mm(xn3, w_up, "nn", F32, "ffn_up")
    act = _conv_gelu_fwd(up, conv_w, conv_b, ffn_block)
    ffn = _mm(act, w_down, "nn", F32, "ffn_down")
    dy, loss_acc, d_ffn, dg3 = _out_loss(h1, ffn, g3, tgt, n_valid)
    loss = loss_acc[0, 0]

    d_act = _mm(d_ffn, w_down, "nt", F32, "ffn_down_dx")
    gw_down = _mm(act, d_ffn, "tn", BF16, "ffn_down_dw")
    d_up, g_conv_w, g_conv_b = _conv_gelu_bwd(up, d_act, conv_w, conv_b, ffn_block)
    d_xn3 = _mm(d_up, w_up, "nt", F32, "ffn_up_dx")
    gw_up = _mm(xn3, d_up, "tn", BF16, "ffn_up_dw")
    dh1, dg2 = _rmsnorm_bwd(h1, g2, d_xn3, dy, F32, "norm3_bwd")

    d_mixed, dg1 = _rmsnorm_bwd(mixed, g1, dh1, None, BF16, "norm2_bwd")
    d_gated = _mm(d_mixed, w_out, "nt", F32, "out_dx")
    gw_out = _mm(gated, d_mixed, "tn", BF16, "out_dw")
    d_ya, d_yb, d_ga, d_gb = _gate_bwd(d_gated, gf, ya, yb)
    d_oa = _mm(d_ya, w_fox, "nt", F32, "fox_o_dx")
    gw_fox = _mm(o_a, d_ya, "tn", BF16, "fox_o_dw")
    d_ob = _mm(d_yb, w_sb, "nt", F32, "sb_o_dx")
    gw_sb = _mm(o_b, d_yb, "tn", BF16, "sb_o_dw")
    neg_lse = [-x for x in _bf16_pieces(lse.transpose(0, 2, 1, 3).reshape(N_HEADS, lp).T)]
    neg_dsum = [-x for x in _bf16_pieces(_head_dots(d_oa, o_a)[:, :N_HEADS])]
    qaug = _aug_lanes(c_pieces + 3 * [one] + neg_lse + [first_block])
    early = {"w_o_fox": gw_fox, "w_o_sb": gw_sb, "w_out": gw_out, "w_up": gw_up, "conv_w": g_conv_w,
             "conv_b": g_conv_b, "w_down": gw_down}
    dq_a, dk_a, dv_a, dcrow, *early_parts = _fox_bwd(
        qkv, qaug, kaug, _aug_lanes(neg_dsum), d_oa, early_grads(early) if early_grads else None)
    dq_b, dk_b, dv_b = _sb_bwd(qkv, lcar, d_ob)
    dc = dcrow.transpose(0, 2, 1, 3).reshape(N_HEADS, lp).T
    dc = jnp.concatenate([dc, jnp.zeros((lp, LANES - N_HEADS), F32)], axis=1)
    df, db = _forget_bwd(dc, fpre, b_pad)
    lane = jnp.arange(LANES) < N_HEADS
    df = jnp.where(lane[None, :], df, 0.0)
    d_proj = jnp.concatenate(
        [dq_a, dk_a.astype(BF16), dv_a.astype(BF16), dq_b, dk_b.astype(BF16), dv_b.astype(BF16),
         d_ga, d_gb, df.astype(BF16), jnp.zeros((lp, F_PAD - LANES), BF16)], axis=1)
    w_in_p = jnp.concatenate([w_qkv, w_gf], axis=1)
    d_xn1 = _mm(d_proj, w_in_p, "nt", F32, "proj_dx")
    gw_in_p = _mm(xn1, d_proj, "tn", BF16, "proj_dw")
    dh0, dg0 = _rmsnorm_bwd(h0, g0, d_xn1, dh1, F32, "norm1_bwd")

    qkv_parts = jnp.split(gw_in_p[:, :6 * WIDTH], 6, axis=1)
    gw_in = jnp.concatenate(
        qkv_parts[:3] + [gw_in_p[:, 6 * WIDTH + 2 * d:6 * WIDTH + 2 * d + N_HEADS]] + qkv_parts[3:]
        + [gw_in_p[:, 6 * WIDTH:6 * WIDTH + 2 * d]], axis=1)
    grads = {
        "meta_tokens": dh0[:N_META],
        "norm_gains": jnp.concatenate([dg0, dg1, dg2, dg3], axis=0),
        "w_in": gw_in,
        "b_forget": db[:, :N_HEADS],
        "early_parts": early_parts,
        **early,
    }
    return loss, dh0[N_META:n_valid], grads


MESH_IDS = pl.DeviceIdType.MESH


def _window(ref, kind, idx, rows, cols):
    if kind == "slots":
        return ref.at[idx]
    if kind == "gate_value":
        half = N_DEV // 2
        idx = jnp.where(idx < half, 2 * idx, 2 * (idx - half) + 1)
        kind = "cols"
    if kind == "cols":
        return ref.at[:, pl.ds(pl.multiple_of(idx * cols, cols & -cols), cols)]
    return ref.at[pl.ds(pl.multiple_of(idx * rows, rows & -rows), rows), :]


def _gathered_shape(shape, kind):
    rows, cols = shape
    return {"slots": (N_DEV, rows, cols), "cols": (rows, N_DEV * cols), "gate_value": (rows, N_DEV * cols),
            "rows": (N_DEV * rows, cols)}[kind]


def _all_gather(shards, kinds):
    n = len(shards)

    def body(*refs):
        ins, outs = refs[:n], refs[n:2 * n]
        send_sems, recv_sems, local_sems = refs[2 * n:]
        x, y, c = lax.axis_index("x"), lax.axis_index("y"), lax.axis_index("c")
        me, sibling = (x, y, c), (x, y, 1 - c)
        chips = [(1 - x, y), (x, 1 - y), (1 - x, 1 - y)]

        def part(t, px, py, pc):
            return _window(outs[t], kinds[t], 4 * px + 2 * py + pc, *shards[t].shape)

        def copy(k, t, blk, to, src=None):
            return pltpu.make_async_remote_copy(
                src_ref=part(t, *blk) if src is None else src, dst_ref=part(t, *blk),
                send_sem=send_sems.at[k, t], recv_sem=recv_sems.at[k, t],
                device_id=to, device_id_type=MESH_IDS)

        mine = [pltpu.make_async_copy(ins[t], part(t, *me), local_sems.at[t]) for t in range(n)]
        for cp in mine:
            cp.start()
        first = [copy(0, t, me, sibling, src=ins[t]) for t in range(n)]
        first += [copy(1 + j, t, me, (*chip, c), src=ins[t]) for j, chip in enumerate(chips) for t in range(n)]
        for cp in first:
            cp.start()
        passed = []
        for j, chip in enumerate(chips):
            for t in range(n):
                copy(1 + j, t, (*chip, c), me).wait_recv()
                passed.append(copy(4 + j, t, (*chip, c), sibling))
                passed[-1].start()
        for t in range(n):
            copy(0, t, sibling, me).wait_recv()
        for j, chip in enumerate(chips):
            for t in range(n):
                copy(4 + j, t, (*chip, 1 - c), me).wait_recv()
        for cp in first + passed:
            cp.wait_send()
        for cp in mine:
            cp.wait()

    any_space = pl.BlockSpec(memory_space=pl.ANY)
    return pl.pallas_call(
        body, name="all_gather",
        out_shape=tuple(jax.ShapeDtypeStruct(_gathered_shape(a.shape, k), a.dtype) for a, k in zip(shards, kinds)),
        in_specs=[any_space] * n,
        out_specs=tuple([any_space] * n),
        scratch_shapes=[pltpu.SemaphoreType.DMA((7, n)), pltpu.SemaphoreType.DMA((7, n)),
                        pltpu.SemaphoreType.DMA((n,))],
    )(*shards)


class _Exchange:
    def __init__(self, arrays, kinds, shard_shapes, gather):
        self.arrays, self.kinds, self.shard_shapes, self.gather = list(arrays), list(kinds), list(shard_shapes), gather
        self.n = n = len(self.arrays)
        self.any_specs = [pl.BlockSpec(memory_space=pl.ANY)] * n
        if gather:
            shapes = [_gathered_shape(a.shape, k) for a, k in zip(self.arrays, kinds)]
        else:
            shapes = [(N_DEV,) + tuple(s) for s in shard_shapes]
        self.out_shapes = [jax.ShapeDtypeStruct(s, a.dtype) for s, a in zip(shapes, self.arrays)]
        self.scratch = [pltpu.SemaphoreType.DMA((N_DEV - 1, n)), pltpu.SemaphoreType.DMA((N_DEV - 1, n)),
                        pltpu.SemaphoreType.DMA((n,))]

    def copies(self, ins, outs, sems):
        send_sems, recv_sems, local_sems = sems
        x, y, c = lax.axis_index("x"), lax.axis_index("y"), lax.axis_index("c")
        my = 4 * x + 2 * y + c

        def src(t, receiver):
            if self.gather or self.kinds[t] == "all":
                return ins[t]
            return _window(ins[t], self.kinds[t], receiver, *self.shard_shapes[t])

        def dst(t, sender):
            if self.gather:
                return _window(outs[t], self.kinds[t], sender, *self.shard_shapes[t])
            return outs[t].at[sender]

        local = [pltpu.make_async_copy(src(t, my), dst(t, my), local_sems.at[t]) for t in range(self.n)]
        sends, arrivals = [], []
        for rel in range(1, N_DEV):
            px, py, pc = x ^ (rel >> 2), y ^ ((rel >> 1) & 1), c ^ (rel & 1)
            peer = 4 * px + 2 * py + pc
            for t in range(self.n):
                common = dict(send_sem=send_sems.at[rel - 1, t], recv_sem=recv_sems.at[rel - 1, t],
                              device_id=(px, py, pc), device_id_type=MESH_IDS)
                sends.append(pltpu.make_async_remote_copy(src_ref=src(t, peer), dst_ref=dst(t, my), **common))
                arrivals.append(pltpu.make_async_remote_copy(src_ref=src(t, my), dst_ref=dst(t, peer), **common))
        return local, sends, arrivals

    def start(self, ins, outs, sems):
        local, sends, _ = self.copies(ins, outs, sems)
        for cp in local + sends:
            cp.start()

    def wait(self, ins, outs, sems):
        local, sends, arrivals = self.copies(ins, outs, sems)
        for cp in arrivals:
            cp.wait_recv()
        for cp in sends:
            cp.wait_send()
        for cp in local:
            cp.wait()


def _exchange(grads, kinds, shard_shapes):
    ex = _Exchange(grads, kinds, shard_shapes, gather=False)
    n = ex.n

    def body(*refs):
        ins, outs, sems = refs[:n], refs[n:2 * n], refs[2 * n:]
        ex.start(ins, outs, sems)
        ex.wait(ins, outs, sems)

    return pl.pallas_call(
        body, name="grad_exchange",
        out_shape=tuple(ex.out_shapes),
        in_specs=ex.any_specs,
        out_specs=tuple(ex.any_specs),
        scratch_shapes=ex.scratch,
    )(*grads)


def _sum_adamw(parts, w, m, v, name):
    rows, cols = w.shape
    n, rows_p, cols_p = parts.shape
    tr = _tile(rows, BLK, SUBLANES) if rows > BLK else rows
    tp = tr if rows_p == rows else rows_p
    c1 = 1.0 - ADAM_B1 ** ADAM_STEP
    c2 = 1.0 - ADAM_B2 ** ADAM_STEP

    def body(p_ref, w_ref, m_ref, v_ref, g_ref, d_ref, nm_ref, nv_ref):
        gv = p_ref[0, 0:tr, 0:cols].astype(F32)
        for s in range(1, n):
            gv = gv + p_ref[s, 0:tr, 0:cols].astype(F32)
        g_ref[...] = gv
        nm = ADAM_B1 * m_ref[...] + (1.0 - ADAM_B1) * gv
        nv = ADAM_B2 * v_ref[...] + (1.0 - ADAM_B2) * (gv * gv)
        m_hat = nm / c1
        v_hat = nv / c2
        d_ref[...] = -ADAM_LR * (m_hat / (jnp.sqrt(v_hat) + ADAM_EPS) + ADAM_WD * w_ref[...])
        nm_ref[...] = nm
        nv_ref[...] = nv

    spec = pl.BlockSpec((tr, cols), lambda i: (i, 0))
    out = jax.ShapeDtypeStruct((rows, cols), F32)
    return pl.pallas_call(
        body, name=name,
        out_shape=(out, out, out, out),
        grid=(rows // tr,),
        in_specs=[pl.BlockSpec((n, tp, cols_p), lambda i: (0, i, 0)), spec, spec, spec],
        out_specs=(spec, spec, spec, spec),
        compiler_params=_params(("parallel",)),
    )(parts, w, m, v)


def _pad2(a, rows, cols):
    return jnp.pad(a, ((0, rows - a.shape[0]), (0, cols - a.shape[1])))


WEIGHTS = ["meta_tokens", "norm_gains", "w_in", "b_forget", "w_o_fox", "w_o_sb", "w_out", "w_up", "conv_w",
           "conv_b", "w_down"]


def kernel(x, meta_tokens, norm_gains, w_in, b_forget, w_o_fox, w_o_sb, w_out, w_up, conv_w, conv_b, w_down, loss_target, m_meta_tokens, m_norm_gains, m_w_in, m_b_forget, m_w_o_fox, m_w_o_sb, m_w_out, m_w_up, m_conv_w, m_conv_b, m_w_down, v_meta_tokens, v_norm_gains, v_w_in, v_b_forget, v_w_o_fox, v_w_o_sb, v_w_out, v_w_up, v_conv_w, v_conv_b, v_w_down):
    w = dict(meta_tokens=meta_tokens, norm_gains=norm_gains, w_in=w_in, b_forget=b_forget, w_o_fox=w_o_fox,
             w_o_sb=w_o_sb, w_out=w_out, w_up=w_up, conv_w=conv_w, conv_b=conv_b, w_down=w_down)
    mom = dict(meta_tokens=m_meta_tokens, norm_gains=m_norm_gains, w_in=m_w_in, b_forget=m_b_forget,
               w_o_fox=m_w_o_fox, w_o_sb=m_w_o_sb, w_out=m_w_out, w_up=m_w_up, conv_w=m_conv_w, conv_b=m_conv_b,
               w_down=m_w_down)
    vel = dict(meta_tokens=v_meta_tokens, norm_gains=v_norm_gains, w_in=v_w_in, b_forget=v_b_forget,
               w_o_fox=v_w_o_fox, w_o_sb=v_w_o_sb, w_out=v_w_out, w_up=v_w_up, conv_w=v_conv_w, conv_b=v_conv_b,
               w_down=v_w_down)
    w2 = {n: a.reshape(a.shape[-2:]) for n, a in w.items()}
    shard_shape = {n: a.shape for n, a in w2.items()}

    d = x.shape[-1]
    up_cols = shard_shape["w_up"][1]
    up_pad = -(-up_cols // LANES) * LANES
    half = N_DEV // 2
    pad_rows = lambda a: _pad2(a, SUBLANES, a.shape[1])

    shards = [
        ("w_in", "slots", w2["w_in"].astype(BF16)),
        ("meta_tokens", "cols", w2["meta_tokens"]),
        ("norm_gains", "cols", pad_rows(w2["norm_gains"])),
    ]
    later = [
        ("w_o_fox", "cols", w2["w_o_fox"].astype(BF16)),
        ("w_o_sb", "cols", w2["w_o_sb"].astype(BF16)),
        ("w_out", "rows", w2["w_out"].astype(BF16)),
        ("w_up", "gate_value", _pad2(w2["w_up"], d, up_pad).astype(BF16)),
        ("conv_w", "gate_value", _pad2(w2["conv_w"], SUBLANES, up_pad)),
        ("w_down", "rows", w2["w_down"].astype(BF16)),
    ]
    full = dict(zip([s[0] for s in shards], _all_gather([s[2] for s in shards], [s[1] for s in shards])))
    w_in_full = jnp.concatenate([full["w_in"][i] for i in range(N_DEV)], axis=1)
    conv_b_p = jnp.pad(w2["conv_b"].reshape(2, half, up_cols), ((0, 0), (0, 0), (0, up_pad - up_cols)))
    conv_b_p = conv_b_p.transpose(1, 0, 2)

    def finish_gather(gathered):
        w_fox, w_sb, w_out_full, w_up_p, conv_w_p, w_down_full = gathered
        w_down_p = jnp.pad(w_down_full.reshape(half, up_cols, d), ((0, 0), (0, up_pad - up_cols), (0, 0)))
        return w_fox, w_sb, w_out_full, w_up_p, conv_w_p[:3], w_down_p.reshape(half * up_pad, d)

    early_names = ["w_o_fox", "w_o_sb", "w_out", "w_up", "conv_w", "conv_b", "w_down"]

    def early_exchange(g):
        sends = {
            "w_o_fox": ("cols", g["w_o_fox"], shard_shape["w_o_fox"]),
            "w_o_sb": ("cols", g["w_o_sb"], shard_shape["w_o_sb"]),
            "w_out": ("rows", g["w_out"], shard_shape["w_out"]),
            "w_up": ("gate_value", g["w_up"], (d, up_pad)),
            "conv_w": ("gate_value", pad_rows(g["conv_w"]), (SUBLANES, up_pad)),
            "conv_b": ("all", pad_rows(g["conv_b"].reshape(half, 2, up_pad).transpose(1, 0, 2)[:, :, :up_cols]
                                       .reshape(1, -1)),
                       (SUBLANES, N_DEV * up_cols)),
            "w_down": ("rows", g["w_down"].reshape(half, up_pad, d)[:, :up_cols].reshape(half * up_cols, d),
                       shard_shape["w_down"]),
        }
        return _Exchange([sends[n][1] for n in early_names], [sends[n][0] for n in early_names],
                         [sends[n][2] for n in early_names], gather=False)

    late_weights = (_Exchange([s[2] for s in later], [s[1] for s in later], [s[2].shape for s in later], gather=True),
                    finish_gather)
    loss, grad_x, grads = _local_step(
        x[0], loss_target[0], full["meta_tokens"], full["norm_gains"][:4], w_in_full, w2["b_forget"],
        None, None, None, None, None, conv_b_p.reshape(1, N_DEV * up_pad), None, ffn_block=up_pad,
        late_weights=late_weights, early_grads=early_exchange)
    loss = lax.psum(loss, ("x", "y", "c"))

    in_cols = shard_shape["w_in"][1]
    late_names = ["meta_tokens", "norm_gains", "w_in", "b_forget"]
    sends = {
        "meta_tokens": ("cols", grads["meta_tokens"], (N_META, LANES)),
        "norm_gains": ("cols", pad_rows(grads["norm_gains"]), (SUBLANES, LANES)),
        "w_in": ("slots", jnp.stack([grads["w_in"][:, i * in_cols:(i + 1) * in_cols] for i in range(N_DEV)]),
                 shard_shape["w_in"]),
        "b_forget": ("all", _pad2(grads["b_forget"], SUBLANES, LANES), (SUBLANES, LANES)),
    }
    parts = dict(zip(late_names, _exchange([sends[n][1] for n in late_names], [sends[n][0] for n in late_names],
                                           [sends[n][2] for n in late_names])))
    parts.update(zip(early_names, grads["early_parts"]))

    grad, delta, new_m, new_v = {}, {}, {}, {}
    for n in WEIGHTS:
        shape = w[n].shape
        outs = _sum_adamw(parts[n], w2[n], mom[n].reshape(shard_shape[n]), vel[n].reshape(shard_shape[n]),
                          "adamw_" + n)
        grad[n], delta[n], new_m[n], new_v[n] = (o.reshape(shape) for o in outs)

    return (loss, grad_x[None], *[grad[n] for n in WEIGHTS], *[delta[n] for n in WEIGHTS],
            *[new_m[n] for n in WEIGHTS], *[new_v[n] for n in WEIGHTS])
```

```python
import functools
import math

import jax
import jax.numpy as jnp
from jax import lax
from jax.experimental import pallas as pl
from jax.experimental.pallas import tpu as pltpu

F32 = jnp.float32
BF16 = jnp.bfloat16

N_DEV = 8
N_META = 16
HEAD_DIM = 64
N_HEADS = 8
WIDTH = N_HEADS * HEAD_DIM
N_PAIRS = N_HEADS // 2
LANES = 128
SUBLANES = 8
EPS = 1e-6
ATT_SCALE = HEAD_DIM ** -0.5
BLK = 256
F_PAD = 256
VMEM_LIMIT = 48 << 20

ADAM_LR = 0.001
ADAM_B1 = 0.9
ADAM_B2 = 0.999
ADAM_EPS = 1e-08
ADAM_WD = 0.01
ADAM_STEP = 10

GELU_C = math.sqrt(2.0 / math.pi)
GELU_A = 0.044715
EXP_IS_ZERO = -110.0
NOT_VISITED = -1e30


def _params(sem, vmem=VMEM_LIMIT):
    return pltpu.CompilerParams(dimension_semantics=sem, vmem_limit_bytes=vmem)


def _tile(dim, cap, align=LANES):
    t = (min(cap, dim) // align) * align
    while t >= align:
        if dim % t == 0:
            return t
        t -= align
    return dim


def _log_sigmoid_parts(z):
    lp = jnp.log1p(jnp.exp(-jnp.abs(z)))
    return jnp.minimum(z, 0.0) - lp, jnp.minimum(-z, 0.0) - lp


def _sigmoid(x):
    return 1.0 / (1.0 + jnp.exp(-x))


def _split_bf16(x):
    hi = x.astype(BF16)
    lo = (x - hi.astype(F32)).astype(BF16)
    return hi, lo


def _dot(a, b, dims):
    return lax.dot_general(a, b, (dims, ((), ())), preferred_element_type=F32)


NN = ((1,), (0,))
NT = ((1,), (1,))
TN = ((0,), (0,))


def _mm(a, b, mode, out_dtype, name):
    if mode == "nn":
        (m, kc), (_, n) = a.shape, b.shape
    elif mode == "nt":
        (m, kc), (n, _) = a.shape, b.shape
    else:
        (kc, m), (_, n) = a.shape, b.shape
    if mode == "tn":
        tm, tn, tk = _tile(m, 1024), _tile(n, 1792), _tile(kc, 1408)
    else:
        tm, tn, tk = _tile(m, 1408), _tile(n, 1024), _tile(kc, 1536)
    nk = kc // tk
    dims = {"nn": NN, "nt": NT, "tn": TN}[mode]

    def body(a_ref, b_ref, o_ref, *scratch):
        k = pl.program_id(2)
        part = _dot(a_ref[...].astype(BF16), b_ref[...].astype(BF16), dims)
        if nk == 1:
            o_ref[...] = part.astype(out_dtype)
            return
        acc_ref, = scratch

        @pl.when(k == 0)
        def _():
            acc_ref[...] = part

        @pl.when(k > 0)
        def _():
            acc_ref[...] += part

        @pl.when(k == nk - 1)
        def _():
            o_ref[...] = acc_ref[...].astype(out_dtype)

    if mode == "tn":
        a_spec = pl.BlockSpec((tk, tm), lambda j, i, k: (k, i))
    else:
        a_spec = pl.BlockSpec((tm, tk), lambda j, i, k: (i, k))
    if mode == "nt":
        b_spec = pl.BlockSpec((tn, tk), lambda j, i, k: (j, k))
    else:
        b_spec = pl.BlockSpec((tk, tn), lambda j, i, k: (k, j))
    return pl.pallas_call(
        body, name=name,
        out_shape=jax.ShapeDtypeStruct((m, n), out_dtype),
        grid=(n // tn, m // tm, nk),
        in_specs=[a_spec, b_spec],
        out_specs=pl.BlockSpec((tm, tn), lambda j, i, k: (i, j)),
        scratch_shapes=[pltpu.VMEM((tm, tn), F32)] if nk > 1 else [],
        compiler_params=_params(("parallel", "parallel", "arbitrary")),
    )(a, b)


def _shifted_rows(cur_ref, prev_ref, first=None):
    head = prev_ref[...] if first is None else jnp.where(pl.program_id(0) == 0, first, prev_ref[...])
    return jnp.concatenate([head, cur_ref[0:BLK - N_META, :]], axis=0)


def _shifted_specs(s, d):
    per = BLK // N_META
    return (pl.BlockSpec((BLK, d), lambda i: (jnp.minimum(i, s // BLK - 1), 0)),
            pl.BlockSpec((N_META, d), lambda i: (jnp.maximum(per * i - 1, 0), 0)))


def _first_norm_fwd(x, meta, g, lp):
    s, d = x.shape
    n_valid = N_META + s

    def body(x_ref, p_ref, m_ref, g_ref, h_ref, o_ref):
        row = pl.program_id(0) * BLK + lax.broadcasted_iota(jnp.int32, (BLK, 1), 0)
        h = jnp.where(row < n_valid, _shifted_rows(x_ref, p_ref, m_ref[...]), 0.0)
        h_ref[...] = h
        r = lax.rsqrt(jnp.mean(h * h, axis=-1, keepdims=True) + EPS)
        o_ref[...] = ((h * r) * g_ref[...]).astype(BF16)

    row_d = pl.BlockSpec((BLK, d), lambda i: (i, 0))
    return pl.pallas_call(
        body, name="norm1_fwd",
        out_shape=(jax.ShapeDtypeStruct((lp, d), F32), jax.ShapeDtypeStruct((lp, d), BF16)),
        grid=(lp // BLK,),
        in_specs=[*_shifted_specs(s, d), pl.BlockSpec((N_META, d), lambda i: (0, 0)),
                  pl.BlockSpec((1, d), lambda i: (0, 0))],
        out_specs=(row_d, row_d),
        compiler_params=_params(("parallel",)),
    )(x, x, meta, g)


def _first_norm_bwd(h0, g, dy, resid, s):
    lp, d = h0.shape
    nb = lp // BLK
    assert nb == s // BLK + 1

    def body(x_ref, g_ref, dy_ref, r_ref, gx_ref, gm_ref, dg_ref, keep_ref):
        i = pl.program_id(0)
        xv = x_ref[...]
        dyv = dy_ref[...]
        r = lax.rsqrt(jnp.mean(xv * xv, axis=-1, keepdims=True) + EPS)
        xh = xv * r
        dyg = dyv * g_ref[...]
        dx = r * (dyg - xh * jnp.mean(dyg * xh, axis=-1, keepdims=True)) + r_ref[...]

        @pl.when(i == 0)
        def _():
            dg_ref[...] = jnp.zeros_like(dg_ref)
            keep_ref[...] = jnp.zeros_like(keep_ref)
            gm_ref[...] = dx[0:N_META, :]

        gx_ref[...] = jnp.concatenate([keep_ref[...], dx[0:N_META, :]], axis=0)
        keep_ref[...] = dx[N_META:BLK, :]
        dg_ref[...] += jnp.sum(dyv * xh, axis=0, keepdims=True)

    row = pl.BlockSpec((BLK, d), lambda i: (i, 0))
    vec = pl.BlockSpec((1, d), lambda i: (0, 0))
    return pl.pallas_call(
        body, name="norm1_bwd",
        out_shape=(jax.ShapeDtypeStruct((s, d), F32), jax.ShapeDtypeStruct((N_META, d), F32),
                   jax.ShapeDtypeStruct((1, d), F32)),
        grid=(nb,),
        in_specs=[row, vec, row, row],
        out_specs=(pl.BlockSpec((BLK, d), lambda i: (jnp.maximum(i - 1, 0), 0)),
                   pl.BlockSpec((N_META, d), lambda i: (0, 0)), vec),
        scratch_shapes=[pltpu.VMEM((BLK - N_META, d), F32)],
        compiler_params=_params(("arbitrary",)),
    )(h0, g, dy, resid)


def _rmsnorm_bwd(x, g, dy, resid, out_dtype, name):
    lp, d = x.shape
    has_resid = resid is not None

    def body(*refs):
        if has_resid:
            x_ref, g_ref, dy_ref, r_ref, dx_ref, dg_ref = refs
        else:
            x_ref, g_ref, dy_ref, dx_ref, dg_ref = refs
        i = pl.program_id(0)
        xv = x_ref[...]
        dyv = dy_ref[...].astype(F32)
        r = lax.rsqrt(jnp.mean(xv * xv, axis=-1, keepdims=True) + EPS)
        xh = xv * r
        dyg = dyv * g_ref[...]
        dx = r * (dyg - xh * jnp.mean(dyg * xh, axis=-1, keepdims=True))
        if has_resid:
            dx = dx + r_ref[...]
        dx_ref[...] = dx.astype(out_dtype)

        @pl.when(i == 0)
        def _():
            dg_ref[...] = jnp.zeros_like(dg_ref)

        dg_ref[...] += jnp.sum(dyv * xh, axis=0, keepdims=True)

    row = pl.BlockSpec((BLK, d), lambda i: (i, 0))
    vec = pl.BlockSpec((1, d), lambda i: (0, 0))
    ins = [x, g, dy] + ([resid] if has_resid else [])
    in_specs = [row, vec, row] + ([row] if has_resid else [])
    return pl.pallas_call(
        body, name=name,
        out_shape=(jax.ShapeDtypeStruct((lp, d), out_dtype), jax.ShapeDtypeStruct((1, d), F32)),
        grid=(lp // BLK,),
        in_specs=in_specs,
        out_specs=(row, vec),
        compiler_params=_params(("arbitrary",)),
    )(*ins)


def _forget_fwd(fpre, b_pad):
    lp = fpre.shape[0]

    def body(f_ref, b_ref, c_ref, carry_ref):
        i = pl.program_id(0)

        @pl.when(i == 0)
        def _():
            carry_ref[...] = jnp.zeros_like(carry_ref)

        logf, _ = _log_sigmoid_parts(f_ref[...] + b_ref[...])
        row = lax.broadcasted_iota(jnp.int32, (BLK, BLK), 0)
        col = lax.broadcasted_iota(jnp.int32, (BLK, BLK), 1)
        tri = (col <= row).astype(BF16)
        p0 = logf.astype(BF16)
        r1 = logf - p0.astype(F32)
        p1 = r1.astype(BF16)
        p2 = (r1 - p1.astype(F32)).astype(BF16)
        c = _dot(tri, p0, NN) + _dot(tri, p1, NN) + _dot(tri, p2, NN) + carry_ref[0:1, :]
        c_ref[...] = c
        carry_ref[...] = jnp.broadcast_to(c[BLK - 1:BLK, :], carry_ref.shape)

    return pl.pallas_call(
        body, name="forget_fwd",
        out_shape=jax.ShapeDtypeStruct((lp, LANES), F32),
        grid=(lp // BLK,),
        in_specs=[pl.BlockSpec((BLK, LANES), lambda i: (i, 0)), pl.BlockSpec((1, LANES), lambda i: (0, 0))],
        out_specs=pl.BlockSpec((BLK, LANES), lambda i: (i, 0)),
        scratch_shapes=[pltpu.VMEM((SUBLANES, LANES), F32)],
        compiler_params=_params(("arbitrary",)),
    )(fpre, b_pad)


def _forget_bwd(dc, fpre, b_pad):
    lp = fpre.shape[0]
    nb = lp // BLK

    def body(dc_ref, f_ref, b_ref, df_ref, db_ref, carry_ref):
        i = pl.program_id(0)

        @pl.when(i == 0)
        def _():
            carry_ref[...] = jnp.zeros_like(carry_ref)
            db_ref[...] = jnp.zeros_like(db_ref)

        dcv = dc_ref[...]
        row = lax.broadcasted_iota(jnp.int32, (BLK, BLK), 0)
        col = lax.broadcasted_iota(jnp.int32, (BLK, BLK), 1)
        tri = (col >= row).astype(BF16)
        p0 = dcv.astype(BF16)
        r1 = dcv - p0.astype(F32)
        p1 = r1.astype(BF16)
        p2 = (r1 - p1.astype(F32)).astype(BF16)
        dlogf = _dot(tri, p0, NN) + _dot(tri, p1, NN) + _dot(tri, p2, NN) + carry_ref[0:1, :]
        carry_ref[...] = jnp.broadcast_to(dlogf[0:1, :], carry_ref.shape)
        _, ls_neg = _log_sigmoid_parts(f_ref[...] + b_ref[...])
        df = dlogf * jnp.exp(ls_neg)
        df_ref[...] = df
        db_ref[...] += jnp.sum(df, axis=0, keepdims=True)

    rev = pl.BlockSpec((BLK, LANES), lambda i: (nb - 1 - i, 0))
    vec = pl.BlockSpec((1, LANES), lambda i: (0, 0))
    return pl.pallas_call(
        body, name="forget_bwd",
        out_shape=(jax.ShapeDtypeStruct((lp, LANES), F32), jax.ShapeDtypeStruct((1, LANES), F32)),
        grid=(nb,),
        in_specs=[rev, rev, vec],
        out_specs=(rev, vec),
        scratch_shapes=[pltpu.VMEM((SUBLANES, LANES), F32)],
        compiler_params=_params(("arbitrary",)),
    )(dc, fpre, b_pad)


def _mix_fwd(o_a, o_b, gates, h0, w_fox, w_sb, w_out, g1, g2):
    lp, d = h0.shape

    def body(oa_ref, ob_ref, ga_ref, gb_ref, h_ref, wf_ref, ws_ref, wo_ref, g_ref, g2_ref,
             h1_ref, ya_ref, yb_ref, gated_ref, mixed_ref, xn_ref):
        ya = _dot(oa_ref[...].astype(BF16), wf_ref[...], NN)
        yb = _dot(ob_ref[...].astype(BF16), ws_ref[...], NN)
        gated = _sigmoid(ga_ref[...]) * ya + _sigmoid(gb_ref[...]) * yb
        gb16 = gated.astype(BF16)
        mixed = _dot(gb16, wo_ref[...], NN)
        r = lax.rsqrt(jnp.mean(mixed * mixed, axis=-1, keepdims=True) + EPS)
        h1 = h_ref[...] + (mixed * r) * g_ref[...]
        h1_ref[...] = h1
        r2 = lax.rsqrt(jnp.mean(h1 * h1, axis=-1, keepdims=True) + EPS)
        xn_ref[...] = ((h1 * r2) * g2_ref[...]).astype(BF16)
        ya_ref[...] = ya
        yb_ref[...] = yb
        gated_ref[...] = gb16
        mixed_ref[...] = mixed

    row_w = pl.BlockSpec((BLK, WIDTH), lambda i: (i, 0))
    row_d = pl.BlockSpec((BLK, d), lambda i: (i, 0))
    full = lambda s: pl.BlockSpec(s, lambda i: (0, 0))
    return pl.pallas_call(
        body, name="mix_fwd",
        out_shape=(jax.ShapeDtypeStruct((lp, d), F32), jax.ShapeDtypeStruct((lp, d), F32),
                   jax.ShapeDtypeStruct((lp, d), F32), jax.ShapeDtypeStruct((lp, d), BF16),
                   jax.ShapeDtypeStruct((lp, d), F32), jax.ShapeDtypeStruct((lp, d), BF16)),
        grid=(lp // BLK,),
        in_specs=[row_w, row_w, row_d, pl.BlockSpec((BLK, d), lambda i: (i, 1)), row_d,
                  full((WIDTH, d)), full((WIDTH, d)), full((d, d)), full((1, d)), full((1, d))],
        out_specs=(row_d, row_d, row_d, row_d, row_d, row_d),
        compiler_params=_params(("parallel",)),
    )(o_a, o_b, gates, gates, h0, w_fox, w_sb, w_out, g1, g2)


def _gate_bwd(d_gated, gates, ya, yb):
    lp, d = d_gated.shape

    def body(dg_ref, ga_ref, gb_ref, ya_ref, yb_ref, dya_ref, dyb_ref, dga_ref, dgb_ref):
        dg = dg_ref[...]
        sa = _sigmoid(ga_ref[...])
        sb = _sigmoid(gb_ref[...])
        dya_ref[...] = (dg * sa).astype(BF16)
        dyb_ref[...] = (dg * sb).astype(BF16)
        dga_ref[...] = (dg * ya_ref[...] * (sa * (1.0 - sa))).astype(BF16)
        dgb_ref[...] = (dg * yb_ref[...] * (sb * (1.0 - sb))).astype(BF16)

    row = pl.BlockSpec((BLK, d), lambda i: (i, 0))
    out = jax.ShapeDtypeStruct((lp, d), BF16)
    return pl.pallas_call(
        body, name="gate_bwd",
        out_shape=(out, out, out, out),
        grid=(lp // BLK,),
        in_specs=[row, row, pl.BlockSpec((BLK, d), lambda i: (i, 1)), row, row],
        out_specs=(row, row, row, row),
        compiler_params=_params(("parallel",)),
    )(d_gated, gates, gates, ya, yb)


def _shift_down(cur, prev, n):
    rolled = pltpu.roll(cur, n, 0)
    row = lax.broadcasted_iota(jnp.int32, prev.shape, 0)
    head = jnp.where(row < n, pltpu.roll(prev, n, 0), rolled[0:SUBLANES])
    return head if cur.shape[0] == SUBLANES else jnp.concatenate([head, rolled[SUBLANES:]], axis=0)


def _shift_up(cur, nxt, n):
    rows = cur.shape[0]
    rolled = pltpu.roll(cur, rows - n, 0)
    row = lax.broadcasted_iota(jnp.int32, nxt.shape, 0)
    tail = jnp.where(row >= SUBLANES - n, pltpu.roll(nxt, SUBLANES - n, 0), rolled[rows - SUBLANES:])
    return tail if rows == SUBLANES else jnp.concatenate([rolled[:rows - SUBLANES], tail], axis=0)


def _gelu(x):
    return 0.5 * x * (1.0 + jnp.tanh(GELU_C * (x + GELU_A * (x * x * x))))


def _gelu_and_grad(x):
    t = jnp.tanh(GELU_C * (x + GELU_A * (x * x * x)))
    half = 0.5 * (1.0 + t)
    return x * half, half + 0.5 * x * (1.0 - t * t) * (GELU_C * (1.0 + 3.0 * GELU_A * (x * x)))


def _conv_taps(cur, prev, w_ref, b_ref):
    s1 = _shift_down(cur, prev, 1)
    s2 = _shift_down(cur, prev, 2)
    u = b_ref[...] + w_ref[0:1, :] * s2
    u = u + w_ref[1:2, :] * s1
    u = u + w_ref[2:3, :] * cur
    return u, s1, s2


def _conv_gelu_fwd(up, conv_w, conv_b, tc):
    lp, f2 = up.shape
    rb = BLK // SUBLANES

    def body(u_ref, p_ref, w_ref, b_ref, act_ref):
        i = pl.program_id(0)
        keep = (i > 0).astype(F32)
        u, _, _ = _conv_taps(u_ref[...], p_ref[...] * keep, w_ref, b_ref)
        act_ref[...] = (_gelu(u[:, :tc]) * u[:, tc:]).astype(BF16)

    prev_row = lambda i: jnp.maximum(i * rb - 1, 0)
    return pl.pallas_call(
        body, name="conv_gelu_fwd",
        out_shape=jax.ShapeDtypeStruct((lp, f2 // 2), BF16),
        grid=(lp // BLK, f2 // (2 * tc)),
        in_specs=[pl.BlockSpec((BLK, 2 * tc), lambda i, j: (i, j)),
                  pl.BlockSpec((SUBLANES, 2 * tc), lambda i, j: (prev_row(i), j)),
                  pl.BlockSpec((3, 2 * tc), lambda i, j: (0, j)),
                  pl.BlockSpec((1, 2 * tc), lambda i, j: (0, j))],
        out_specs=pl.BlockSpec((BLK, tc), lambda i, j: (i, j)),
        compiler_params=_params(("parallel", "parallel")),
    )(up, up, conv_w, conv_b)


def _conv_gelu_bwd(up, d_act, conv_w, conv_b, tc):
    lp, f2 = up.shape
    nb = lp // BLK
    rb = BLK // SUBLANES

    def du_of(u, da):
        gel, grad = _gelu_and_grad(u[:, :tc])
        return jnp.concatenate([da * u[:, tc:] * grad, da * gel], axis=1)

    def body(u_ref, p_ref, n_ref, da_ref, dan_ref, w_ref, b_ref, dup_ref, dcw_ref, dcb_ref):
        i = pl.program_id(1)
        cur = u_ref[...]
        u, s1, s2 = _conv_taps(cur, p_ref[...] * (i > 0).astype(F32), w_ref, b_ref)
        du = du_of(u, da_ref[...])
        u_next, _, _ = _conv_taps(n_ref[...], cur[BLK - SUBLANES:BLK, :], w_ref, b_ref)
        du_next = du_of(u_next, dan_ref[...]) * (i < nb - 1).astype(F32)
        n1 = _shift_up(du, du_next, 1)
        n2 = _shift_up(du, du_next, 2)
        dup_ref[...] = (w_ref[2:3, :] * du + w_ref[1:2, :] * n1 + w_ref[0:1, :] * n2).astype(BF16)

        @pl.when(i == 0)
        def _():
            dcw_ref[...] = jnp.zeros_like(dcw_ref)
            dcb_ref[...] = jnp.zeros_like(dcb_ref)

        dcw_ref[0:1, :] += jnp.sum(du * s2, axis=0, keepdims=True)
        dcw_ref[1:2, :] += jnp.sum(du * s1, axis=0, keepdims=True)
        dcw_ref[2:3, :] += jnp.sum(du * cur, axis=0, keepdims=True)
        dcb_ref[...] += jnp.sum(du, axis=0, keepdims=True)

    prev_row = lambda i: jnp.maximum(i * rb - 1, 0)
    next_row = lambda i: jnp.minimum((i + 1) * rb, nb * rb - 1)
    return pl.pallas_call(
        body, name="conv_gelu_bwd",
        out_shape=(jax.ShapeDtypeStruct((lp, f2), BF16), jax.ShapeDtypeStruct((3, f2), F32),
                   jax.ShapeDtypeStruct((1, f2), F32)),
        grid=(f2 // (2 * tc), nb),
        in_specs=[pl.BlockSpec((BLK, 2 * tc), lambda j, i: (i, j)),
                  pl.BlockSpec((SUBLANES, 2 * tc), lambda j, i: (prev_row(i), j)),
                  pl.BlockSpec((SUBLANES, 2 * tc), lambda j, i: (next_row(i), j)),
                  pl.BlockSpec((BLK, tc), lambda j, i: (i, j)),
                  pl.BlockSpec((SUBLANES, tc), lambda j, i: (next_row(i), j)),
                  pl.BlockSpec((3, 2 * tc), lambda j, i: (0, j)),
                  pl.BlockSpec((1, 2 * tc), lambda j, i: (0, j))],
        out_specs=(pl.BlockSpec((BLK, 2 * tc), lambda j, i: (i, j)),
                   pl.BlockSpec((3, 2 * tc), lambda j, i: (0, j)),
                   pl.BlockSpec((1, 2 * tc), lambda j, i: (0, j))),
        compiler_params=_params(("parallel", "arbitrary")),
    )(up, up, up, d_act, d_act, conv_w, conv_b)


def _out_loss(h1, ffn, g3, target):
    lp, d = h1.shape
    s = target.shape[0]
    n_valid = N_META + s

    def body(h_ref, f_ref, g_ref, t_ref, tp_ref, dy_ref, loss_ref, df_ref, dg_ref):
        i = pl.program_id(0)

        @pl.when(i == 0)
        def _():
            loss_ref[...] = jnp.zeros_like(loss_ref)
            dg_ref[...] = jnp.zeros_like(dg_ref)

        fv = f_ref[...]
        r = lax.rsqrt(jnp.mean(fv * fv, axis=-1, keepdims=True) + EPS)
        xh = fv * r
        y = h_ref[...] + xh * g_ref[...]
        row = i * BLK + lax.broadcasted_iota(jnp.int32, (BLK, 1), 0)
        valid = (row >= N_META) & (row < n_valid)
        diff = jnp.where(valid, y - _shifted_rows(t_ref, tp_ref), 0.0)
        dy = diff * (1.0 / d)
        dy_ref[...] = dy
        per_row = jnp.mean(diff * diff, axis=-1, keepdims=True)
        loss_ref[...] += 0.5 * jnp.sum(per_row, axis=0, keepdims=True)
        dyg = dy * g_ref[...]
        df_ref[...] = (r * (dyg - xh * jnp.mean(dyg * xh, axis=-1, keepdims=True))).astype(BF16)
        dg_ref[...] += jnp.sum(dy * xh, axis=0, keepdims=True)

    row_d = pl.BlockSpec((BLK, d), lambda i: (i, 0))
    vec = pl.BlockSpec((1, d), lambda i: (0, 0))
    return pl.pallas_call(
        body, name="out_loss",
        out_shape=(jax.ShapeDtypeStruct((lp, d), F32), jax.ShapeDtypeStruct((SUBLANES, LANES), F32),
                   jax.ShapeDtypeStruct((lp, d), BF16), jax.ShapeDtypeStruct((1, d), F32)),
        grid=(lp // BLK,),
        in_specs=[row_d, row_d, vec, *_shifted_specs(s, d)],
        out_specs=(row_d, pl.BlockSpec((SUBLANES, LANES), lambda i: (0, 0)), row_d, vec),
        compiler_params=_params(("arbitrary",)),
    )(h1, ffn, g3, target, target)


def _head_masks(rows=BLK):
    lane = lax.broadcasted_iota(jnp.int32, (rows, LANES), 1)
    return [lane < HEAD_DIM, lane >= HEAD_DIM]


def _att_specs(base, lp):
    q_spec = pl.BlockSpec((BLK, LANES), lambda p, i: (i, base + p))
    k_spec = pl.BlockSpec((lp, LANES), lambda p, i: (0, base + N_PAIRS + p))
    v_spec = pl.BlockSpec((lp, LANES), lambda p, i: (0, base + 2 * N_PAIRS + p))
    return q_spec, k_spec, v_spec


def _kv_rows(j, nb=1):
    return pl.ds(pl.multiple_of(j * BLK, nb * BLK), nb * BLK)


def _walk_kv(i, tile, reverse=False, first_pair=0, more=None):
    pairs, odd = i // 2, i % 2

    def wide(t, carry):
        tile(2 * (pairs - 1 - t) if reverse else 2 * t, 2, False)
        return carry

    def single():
        @pl.when(odd == 1)
        def _():
            tile(i - 1, 1, False)

    if reverse:
        tile(i, 1, True)
        single()
        if more is None:
            lax.fori_loop(0, pairs, wide, 0)
        else:
            lax.while_loop(lambda c: (c[0] < pairs) & c[1], lambda c: (wide(c[0], c[0]) + 1, more()), (0, more()))
    else:
        lax.fori_loop(first_pair, pairs, wide, 0)
        single()
        tile(i, 1, True)


def _walk_kv_ahead(i, lead, follow, kept, first_pair=0):
    pairs, odd = i // 2, i % 2

    def keep(values):
        for ref, value in zip(kept, values):
            ref[...] = value

    @pl.when(pairs > first_pair)
    def _():
        keep(lead(2 * first_pair, 2))

    def wide(t, carry):
        ahead = lead(2 * jnp.minimum(t + 1, pairs - 1), 2)
        follow([ref[...] for ref in kept], 2 * t, 2, False)
        keep(ahead)
        return carry

    lax.fori_loop(first_pair, pairs, wide, 0)

    @pl.when(odd == 1)
    def _():
        follow(lead(i - 1, 1), i - 1, 1, False)

    follow(lead(i, 1), i, 1, True)


def _bf16_pieces(x):
    rnd = lambda a: lax.reduce_precision(a, exponent_bits=8, mantissa_bits=7)
    p0 = rnd(x)
    p1 = rnd(x - p0)
    p2 = rnd(x - p0 - p1)
    return [p0.astype(BF16), p1.astype(BF16), p2.astype(BF16)]


def _aug_lanes(cols):
    lp = cols[0].shape[0]
    vals = jnp.stack([c.astype(BF16) for c in cols], axis=-1)
    vals = vals.reshape(lp, N_PAIRS, 2, len(cols))[:, :, ::-1, :]
    vals = jnp.pad(vals, ((0, 0), (0, 0), (0, 0), (0, HEAD_DIM - len(cols))))
    return vals.reshape(lp, WIDTH)


N_AUG = 3


def _riding(exchange, n_in, n_out, n_scratch, grid):
    n = exchange.n if exchange else 0

    def split(refs):
        own_in, ex_in = refs[:n_in], refs[n_in:n_in + n]
        own_out, ex_out = refs[n_in + n:n_in + n + n_out], refs[n_in + n + n_out:n_in + 2 * n + n_out]
        rest = refs[n_in + 2 * n + n_out:]
        return own_in + own_out + rest[:n_scratch], (ex_in, ex_out, rest[n_scratch:])

    def first_step(ex_refs):
        if exchange:
            @pl.when((pl.program_id(0) == 0) & (pl.program_id(1) == 0))
            def _():
                exchange.start(*ex_refs)

    def last_step(ex_refs):
        if exchange:
            @pl.when((pl.program_id(0) == grid[0] - 1) & (pl.program_id(1) == grid[1] - 1))
            def _():
                exchange.wait(*ex_refs)

    return split, first_step, last_step


def _fox_fwd(qkv, qaug, kaug, exchange=None):
    lp = qkv.shape[0]
    nq = lp // BLK
    split, first_step, last_step = _riding(exchange, 5, 2, 3, (N_PAIRS, nq))

    def body(*refs):
        (q_ref, k_ref, v_ref, qa_ref, ka_ref, o_ref, lse_ref, acc_ref, m_ref, s_ref), ex_refs = split(refs)
        first_step(ex_refs)
        i = pl.program_id(1)
        hs = range(2)
        masks = _head_masks()
        qs = q_ref[...] * ATT_SCALE
        qa = qa_ref[...]
        qh = [jnp.where(masks[hh], qs, qa) for hh in hs]
        acc_ref[...] = jnp.zeros_like(acc_ref)
        m_ref[...] = jnp.full_like(m_ref, -1e30)
        row = lax.broadcasted_iota(jnp.int32, (BLK, BLK), 0)
        col = lax.broadcasted_iota(jnp.int32, (BLK, BLK), 1)
        causal = col <= row

        def scores(j, nb):
            rows = _kv_rows(j, nb)
            kmasks = _head_masks(nb * BLK)
            k, ka = k_ref[rows, :], ka_ref[rows, :]
            return [_dot(qh[hh], jnp.where(kmasks[hh], k, ka), NT) for hh in hs]

        def absorb(s, j, nb, diag):
            v = v_ref[_kv_rows(j, nb), :]
            kmasks = _head_masks(nb * BLK)
            vh = [jnp.where(kmasks[hh], v, jnp.ones_like(v)) for hh in hs]
            if diag:
                s = [jnp.where(causal, s[hh], -1e30) for hh in hs]
            m_prev = [m_ref[hh] for hh in hs]
            m_new = [jnp.maximum(m_prev[hh], jnp.max(s[hh], axis=-1, keepdims=True)) for hh in hs]
            p = [jnp.exp(s[hh] - m_new[hh]).astype(BF16) for hh in hs]
            for hh in hs:
                acc_ref[hh] = jnp.exp(m_prev[hh] - m_new[hh]) * acc_ref[hh] + _dot(p[hh], vh[hh], NN)
                m_ref[hh] = m_new[hh]

        _walk_kv_ahead(i, scores, absorb, [s_ref.at[hh] for hh in hs], _first_pair(qa, i))
        acc = [acc_ref[hh] for hh in hs]
        denom = [acc[0][:, HEAD_DIM:HEAD_DIM + 1], acc[1][:, 0:1]]
        o_ref[...] = jnp.where(masks[0], acc[0] / denom[0], acc[1] / denom[1])
        lse = jnp.where(masks[0], m_ref[0] + jnp.log(denom[0]), m_ref[1] + jnp.log(denom[1])).T
        lse_ref[0, 0, 0:1, :] = lse[0:1, :]
        lse_ref[0, 0, 1:2, :] = lse[HEAD_DIM:HEAD_DIM + 1, :]
        last_step(ex_refs)

    q_spec, k_spec, v_spec = _att_specs(0, lp)
    blk = pl.BlockSpec((BLK, LANES), lambda p, i: (i, p))
    col_full = pl.BlockSpec((lp, LANES), lambda p, i: (0, p))
    out = jax.ShapeDtypeStruct((lp, WIDTH), F32)
    ex = exchange
    return pl.pallas_call(
        body, name="fox_fwd",
        out_shape=(out, jax.ShapeDtypeStruct((N_PAIRS, nq, 2, BLK), F32)) + tuple(ex.out_shapes if ex else ()),
        grid=(N_PAIRS, nq),
        in_specs=[q_spec, k_spec, v_spec, blk, col_full] + (ex.any_specs if ex else []),
        out_specs=(blk, pl.BlockSpec((1, 1, 2, BLK), lambda p, i: (p, i, 0, 0))) + tuple(ex.any_specs if ex else ()),
        scratch_shapes=[pltpu.VMEM((2, BLK, LANES), F32), pltpu.VMEM((2, BLK, 1), F32),
                        pltpu.VMEM((2, BLK, 2 * BLK), F32)] + (ex.scratch if ex else []),
        compiler_params=_params(("arbitrary", "arbitrary") if ex else ("parallel", "parallel")),
    )(qkv, qkv, qkv, qaug, kaug, *(ex.arrays if ex else []))


FIRST_BLOCK_LANE = 9


def _first_pair(qa, i):
    lane = lax.broadcasted_iota(jnp.int32, qa.shape, 1)
    first = jnp.max(jnp.where(lane == HEAD_DIM + FIRST_BLOCK_LANE, qa.astype(F32), 0.0)).astype(jnp.int32)
    return jnp.clip(first, 0, i) // 2


def _fox_first_blocks(qkv, c):
    lp = qkv.shape[0]
    nq = lp // BLK

    def body(x_ref, o_ref):
        xv = x_ref[...].astype(F32)
        sq = (xv * xv).astype(BF16)
        col = lax.broadcasted_iota(jnp.int32, (2 * WIDTH, LANES), 0)
        lane = lax.broadcasted_iota(jnp.int32, (2 * WIDTH, LANES), 1)
        pick = (col // HEAD_DIM == lane).astype(BF16)
        o_ref[...] = _dot(sq, pick, NN)

    norms = pl.pallas_call(
        body, name="fox_norms",
        out_shape=jax.ShapeDtypeStruct((lp, LANES), F32),
        grid=(nq,),
        in_specs=[pl.BlockSpec((BLK, 2 * WIDTH), lambda i: (i, 0))],
        out_specs=pl.BlockSpec((BLK, LANES), lambda i: (i, 0)),
        compiler_params=_params(("parallel",)),
    )(qkv)
    a_max = 1.02 * jnp.sqrt(norms[:, :N_HEADS].reshape(nq, BLK, N_HEADS).max(axis=1))
    b_max = 1.02 * jnp.sqrt(norms[:, N_HEADS:2 * N_HEADS].reshape(nq, BLK, N_HEADS).max(axis=1))
    c_max = c.reshape(nq, BLK, N_HEADS).max(axis=1)
    c_min = c.reshape(nq, BLK, N_HEADS).min(axis=1)
    bound = (a_max[:, None] * (b_max[None, :] + b_max[:, None]) * ATT_SCALE + c_max[:, None] - c_min[None, :])
    alive = (bound > EXP_IS_ZERO) | jnp.isnan(bound)
    alive = alive.reshape(nq, nq, N_PAIRS, 2).any(axis=-1)
    first = jnp.argmax(alive, axis=1).astype(F32)
    return jnp.repeat(jnp.repeat(first, 2, axis=1), BLK, axis=0)


def _head_dots(a, b):
    lp = a.shape[0]

    def body(a_ref, b_ref, o_ref):
        lane = lax.broadcasted_iota(jnp.int32, (BLK, LANES), 1)
        out = jnp.zeros((BLK, LANES), F32)
        for p in range(N_PAIRS):
            cols = slice(p * LANES, (p + 1) * LANES)
            prod = a_ref[:, cols] * b_ref[:, cols]
            for hh in range(2):
                part = jnp.where((lane >= HEAD_DIM) == (hh == 1), prod, 0.0)
                out = jnp.where(lane == 2 * p + hh, jnp.sum(part, axis=-1, keepdims=True), out)
        o_ref[...] = out

    row = pl.BlockSpec((BLK, WIDTH), lambda i: (i, 0))
    return pl.pallas_call(
        body, name="head_dots",
        out_shape=jax.ShapeDtypeStruct((lp, LANES), F32),
        grid=(lp // BLK,),
        in_specs=[row, row],
        out_specs=pl.BlockSpec((BLK, LANES), lambda i: (i, 0)),
        compiler_params=_params(("parallel",)),
    )(a, b)


def _fox_bwd(qkv, qaug, kaug, doaug, d_o, exchange=None):
    lp = qkv.shape[0]
    nq = lp // BLK
    split, first_step, last_step = _riding(exchange, 7, 4, 2, (N_PAIRS, nq))

    def body(*refs):
        (q_ref, k_ref, v_ref, qa_ref, ka_ref, da_ref, do_ref,
         dq_ref, dk_ref, dv_ref, dc_ref, acc_ref, rs_ref), ex_refs = split(refs)
        first_step(ex_refs)
        i = pl.program_id(1)
        hs = range(2)
        masks = _head_masks()
        qs = q_ref[...] * ATT_SCALE
        qa = qa_ref[...]
        qm = [jnp.where(masks[hh], qs, 0) for hh in hs]
        qh = [jnp.where(masks[hh], qs, qa) for hh in hs]
        dov = do_ref[...].astype(BF16)
        doa = da_ref[...]
        dom = [jnp.where(masks[hh], dov, 0) for hh in hs]
        doh = [jnp.where(masks[hh], dov, doa) for hh in hs]
        row = lax.broadcasted_iota(jnp.int32, (BLK, BLK), 0)
        col = lax.broadcasted_iota(jnp.int32, (BLK, BLK), 1)
        causal = col <= row

        @pl.when(i == 0)
        def _():
            dk_ref[...] = jnp.zeros_like(dk_ref)
            dv_ref[...] = jnp.zeros_like(dv_ref)
            dc_ref[...] = jnp.zeros_like(dc_ref)

        acc_ref[...] = jnp.zeros_like(acc_ref)
        rs_ref[...] = jnp.zeros_like(rs_ref)

        def lead(j, nb):
            rows = _kv_rows(j, nb)
            kmasks = _head_masks(nb * BLK)
            k, ka, v = k_ref[rows, :], ka_ref[rows, :], v_ref[rows, :]
            ones = (lax.broadcasted_iota(jnp.int32, v.shape, 1) % HEAD_DIM < N_AUG).astype(BF16)
            logp = [_dot(qh[hh], jnp.where(kmasks[hh], k, ka), NT) for hh in hs]
            dp = [_dot(doh[hh], jnp.where(kmasks[hh], v, ones), NT) for hh in hs]
            return logp + dp

        def follow(lead_out, j, nb, diag):
            rows = _kv_rows(j, nb)
            k = k_ref[rows, :]
            logp, dp = lead_out[:2], lead_out[2:]
            p = [jnp.exp(logp[hh]) for hh in hs]
            if diag:
                p = [jnp.where(causal, p[hh], 0.0) for hh in hs]
            ds = [p[hh] * dp[hh] for hh in hs]
            dsb = [ds[hh].astype(BF16) for hh in hs]
            for hh in hs:
                acc_ref[hh] += _dot(dsb[hh], k, NN)
                col_sums = jnp.sum(ds[hh], axis=0, keepdims=True)
                for b in range(nb):
                    dc_ref[0, j + b, hh:hh + 1, :] -= col_sums[:, b * BLK:(b + 1) * BLK]
                rs_ref[hh] += jnp.sum(ds[hh], axis=-1, keepdims=True)
            dk_ref[rows, :] += _dot(dsb[0], qm[0], TN) + _dot(dsb[1], qm[1], TN)
            dv_ref[rows, :] += _dot(p[0].astype(BF16), dom[0], TN) + _dot(p[1].astype(BF16), dom[1], TN)

        _walk_kv(i, lambda j, nb, diag: follow(lead(j, nb), j, nb, diag), first_pair=_first_pair(qa, i))
        dq_ref[...] = (jnp.where(masks[0], acc_ref[0], acc_ref[1]) * ATT_SCALE).astype(BF16)
        row_sums = jnp.where(masks[0], rs_ref[0], rs_ref[1]).T
        dc_ref[0, i, 0:1, :] += row_sums[0:1, :]
        dc_ref[0, i, 1:2, :] += row_sums[HEAD_DIM:HEAD_DIM + 1, :]
        last_step(ex_refs)

    q_spec, k_spec, v_spec = _att_specs(0, lp)
    blk = pl.BlockSpec((BLK, LANES), lambda p, i: (i, p))
    col_full = pl.BlockSpec((lp, LANES), lambda p, i: (0, p))
    crow_spec = pl.BlockSpec((1, nq, 2, BLK), lambda p, i: (p, 0, 0, 0))
    ex = exchange
    return pl.pallas_call(
        body, name="fox_bwd",
        out_shape=(jax.ShapeDtypeStruct((lp, WIDTH), BF16), jax.ShapeDtypeStruct((lp, WIDTH), F32),
                   jax.ShapeDtypeStruct((lp, WIDTH), F32), jax.ShapeDtypeStruct((N_PAIRS, nq, 2, BLK), F32))
        + tuple(ex.out_shapes if ex else ()),
        grid=(N_PAIRS, nq),
        in_specs=[q_spec, k_spec, v_spec, blk, col_full, blk, blk] + (ex.any_specs if ex else []),
        out_specs=(blk, col_full, col_full, crow_spec) + tuple(ex.any_specs if ex else ()),
        scratch_shapes=[pltpu.VMEM((2, BLK, LANES), F32), pltpu.VMEM((2, BLK, 1), F32)] + (ex.scratch if ex else []),
        compiler_params=_params(("arbitrary", "arbitrary") if ex else ("parallel", "arbitrary")),
    )(qkv, qkv, qkv, qaug, kaug, doaug, d_o, *(ex.arrays if ex else []))


def _sb_scores(z):
    ell = jnp.minimum(z, 0.0) - jnp.log(1.0 + jnp.exp(-jnp.abs(z)))
    return ell, ell - z


def _stacked(tri):
    return jnp.concatenate([tri, tri], axis=0)


def _cumsum_dot(x, tri2):
    hi, lo = _split_bf16(x)
    return _dot(jnp.concatenate([hi, lo], axis=1), tri2, NN)


def _sb_units(qh, k, strict2, causal, nb, diag):
    hs, bs = range(2), range(nb)
    z = [_dot(qh[hh], k, NT) for hh in hs]
    sc = [[_sb_scores(z[hh][:, b * BLK:(b + 1) * BLK]) for b in bs] for hh in hs]
    ell = [[sc[hh][b][0] for b in bs] for hh in hs]
    kap = [[jnp.where(causal, sc[hh][b][1], 0.0) if diag else sc[hh][b][1] for b in bs] for hh in hs]
    later = [[_cumsum_dot(kap[hh][b], strict2) for b in bs] for hh in hs]
    return ell, kap, later


def _row_sum(x):
    return jnp.sum(x, axis=-1, keepdims=True)


def _join(blocks):
    joined = blocks[0] if len(blocks) == 1 else jnp.concatenate(blocks, axis=1)
    return joined.astype(BF16)


def _sb_fwd(qkv):
    lp = qkv.shape[0]
    nq = lp // BLK
    assert nq <= HEAD_DIM

    def body(q_ref, k_ref, v_ref, o_ref, lc_ref, acc_ref, car_ref):
        i = pl.program_id(1)
        masks = _head_masks()
        qs = q_ref[...] * ATT_SCALE
        qh = [jnp.where(mk, qs, 0).astype(BF16) for mk in masks]
        row = lax.broadcasted_iota(jnp.int32, (BLK, BLK), 0)
        col = lax.broadcasted_iota(jnp.int32, (BLK, BLK), 1)
        lane = lax.broadcasted_iota(jnp.int32, (BLK, LANES), 1)
        causal = col < row
        strict2 = _stacked((row > col).astype(BF16))
        acc_ref[...] = jnp.zeros_like(acc_ref)
        car_ref[...] = jnp.zeros_like(car_ref)
        lc_ref[...] = jnp.full_like(lc_ref, NOT_VISITED)

        def tile(j, nb, diag):
            hs, bs = range(2), range(nb)
            rows = _kv_rows(j, nb)
            k, v = k_ref[rows, :], v_ref[rows, :]
            ell, kap, later = _sb_units(qh, k, strict2, causal, nb, diag)
            car = [[None] * nb for _ in hs]
            for hh in hs:
                run = car_ref[hh]
                for b in reversed(bs):
                    car[hh][b] = run
                    run = run + _row_sum(kap[hh][b])
                car_ref[hh] = run
            if not diag:
                kept = lc_ref[...]
                for hh in hs:
                    for b in bs:
                        kept = jnp.where(lane == j + b + HEAD_DIM * hh, car[hh][b], kept)
                lc_ref[...] = kept
            a = [[jnp.exp(ell[hh][b] + later[hh][b] + car[hh][b]) for b in bs] for hh in hs]
            if diag:
                a = [[jnp.where(causal, a[hh][b], 0.0) for b in bs] for hh in hs]
            for hh in hs:
                acc_ref[hh] += _dot(_join(a[hh]), v, NN)

        _walk_kv(i, tile, reverse=True, more=lambda: jnp.max(car_ref[...]) > EXP_IS_ZERO)
        o_ref[...] = jnp.where(masks[0], acc_ref[0], acc_ref[1])

    q_spec, k_spec, v_spec = _att_specs(3 * N_PAIRS, lp)
    blk = pl.BlockSpec((BLK, LANES), lambda p, i: (i, p))
    out = jax.ShapeDtypeStruct((lp, WIDTH), F32)
    return pl.pallas_call(
        body, name="sb_fwd",
        out_shape=(out, out),
        grid=(N_PAIRS, nq),
        in_specs=[q_spec, k_spec, v_spec],
        out_specs=(blk, blk),
        scratch_shapes=[pltpu.VMEM((2, BLK, LANES), F32), pltpu.VMEM((2, BLK, 1), F32)],
        compiler_params=_params(("parallel", "parallel")),
    )(qkv, qkv, qkv)


def _sb_bwd(qkv, lcar, d_o):
    lp = qkv.shape[0]
    nq = lp // BLK

    def body(q_ref, k_ref, v_ref, lc_ref, do_ref, dq_ref, dk_ref, dv_ref, acc_ref, cg_ref):
        i = pl.program_id(1)
        masks = _head_masks()
        qs = q_ref[...] * ATT_SCALE
        qh = [jnp.where(mk, qs, 0).astype(BF16) for mk in masks]
        dov = do_ref[...]
        doh = [jnp.where(mk, dov, 0.0).astype(BF16) for mk in masks]
        lcv = lc_ref[...]
        lane_row = lax.broadcasted_iota(jnp.int32, (1, LANES), 1)
        alive = jnp.where(jnp.max(lcv, axis=0, keepdims=True) > EXP_IS_ZERO, 1.0, 0.0)
        n_alive = jnp.maximum(jnp.sum(jnp.where(lane_row < HEAD_DIM, alive, 0.0)),
                              jnp.sum(jnp.where(lane_row >= HEAD_DIM, alive, 0.0))).astype(jnp.int32)
        first_pair = jnp.maximum(i - n_alive, 0) // 2
        row = lax.broadcasted_iota(jnp.int32, (BLK, BLK), 0)
        col = lax.broadcasted_iota(jnp.int32, (BLK, BLK), 1)
        lane = lax.broadcasted_iota(jnp.int32, (BLK, LANES), 1)
        causal = col < row
        strict2 = _stacked((row > col).astype(BF16))
        before2 = _stacked((row < col).astype(BF16))

        @pl.when(i == 0)
        def _():
            dk_ref[...] = jnp.zeros_like(dk_ref)
            dv_ref[...] = jnp.zeros_like(dv_ref)

        acc_ref[...] = jnp.zeros_like(acc_ref)
        cg_ref[...] = jnp.zeros_like(cg_ref)

        def tile(j, nb, diag):
            hs, bs = range(2), range(nb)
            rows = _kv_rows(j, nb)
            k, v = k_ref[rows, :], v_ref[rows, :]
            if diag:
                car = [[0.0] for _ in hs]
            else:
                car = [[_row_sum(jnp.where(lane == j + b + HEAD_DIM * hh, lcv, 0.0)) for b in bs] for hh in hs]
            da = [_dot(doh[hh], v, NT) for hh in hs]
            ell, _, later = _sb_units(qh, k, strict2, causal, nb, diag)
            a = [[jnp.exp(ell[hh][b] + later[hh][b] + car[hh][b]) for b in bs] for hh in hs]
            if diag:
                a = [[jnp.where(causal, a[hh][b], 0.0) for b in bs] for hh in hs]
            g = [[da[hh][:, b * BLK:(b + 1) * BLK] * a[hh][b] for b in bs] for hh in hs]
            cg = [[_cumsum_dot(g[hh][b], before2) for b in bs] for hh in hs]
            before = [[None] * nb for _ in hs]
            for hh in hs:
                run = cg_ref[hh]
                for b in bs:
                    before[hh][b] = run
                    run = run + _row_sum(g[hh][b])
                cg_ref[hh] = run
            dz = [[g[hh][b] - jnp.exp(ell[hh][b]) * (g[hh][b] + cg[hh][b] + before[hh][b]) for b in bs] for hh in hs]
            if diag:
                dz = [[jnp.where(causal, dz[hh][b], 0.0) for b in bs] for hh in hs]
            dzb = [_join(dz[hh]) for hh in hs]
            ab = [_join(a[hh]) for hh in hs]
            for hh in hs:
                acc_ref[hh] += _dot(dzb[hh], k, NN)
            dk_ref[rows, :] += _dot(dzb[0], qh[0], TN) + _dot(dzb[1], qh[1], TN)
            dv_ref[rows, :] += _dot(ab[0], doh[0], TN) + _dot(ab[1], doh[1], TN)

        _walk_kv(i, tile, first_pair=first_pair)
        dq_ref[...] = (jnp.where(masks[0], acc_ref[0], acc_ref[1]) * ATT_SCALE).astype(BF16)

    q_spec, k_spec, v_spec = _att_specs(3 * N_PAIRS, lp)
    blk = pl.BlockSpec((BLK, LANES), lambda p, i: (i, p))
    col_full = pl.BlockSpec((lp, LANES), lambda p, i: (0, p))
    return pl.pallas_call(
        body, name="sb_bwd",
        out_shape=(jax.ShapeDtypeStruct((lp, WIDTH), BF16), jax.ShapeDtypeStruct((lp, WIDTH), F32),
                   jax.ShapeDtypeStruct((lp, WIDTH), F32)),
        grid=(N_PAIRS, nq),
        in_specs=[q_spec, k_spec, v_spec, blk, blk],
        out_specs=(blk, col_full, col_full),
        scratch_shapes=[pltpu.VMEM((2, BLK, LANES), F32), pltpu.VMEM((2, BLK, 1), F32)],
        compiler_params=_params(("parallel", "arbitrary")),
    )(qkv, qkv, qkv, lcar, d_o)


def _local_step(x, target, meta, gains, w_in, b_forget, w_fox, w_sb, w_out, w_up, conv_w, conv_b, w_down,
                ffn_block=None, late_weights=None, early_grads=None):
    s, d = x.shape
    n_valid = N_META + s
    lp = -(-n_valid // BLK) * BLK
    pad = lp - n_valid
    nq = lp // BLK


    q_a, k_a, v_a, f_a, q_b, k_b, v_b, g_a, g_b = jnp.split(
        w_in, [512, 1024, 1536, 1544, 2056, 2568, 3080, 4104], axis=1)
    w_qkv = jnp.concatenate([q_a, k_a, v_a, q_b, k_b, v_b], axis=1)
    w_gf = jnp.concatenate([g_a, g_b, f_a, jnp.zeros((d, F_PAD - N_HEADS), BF16)], axis=1)
    b_pad = jnp.concatenate([b_forget.reshape(1, N_HEADS), jnp.zeros((1, LANES - N_HEADS), F32)], axis=1)
    g0, g1, g2, g3 = (gains[i:i + 1] for i in range(4))

    h0, xn1 = _first_norm_fwd(x, meta, g0, lp)
    qkv = _mm(xn1, w_qkv, "nn", BF16, "proj_qkv")
    gf = _mm(xn1, w_gf, "nn", F32, "proj_gates")
    fpre = gf[:, 2 * d:2 * d + LANES]
    c = _forget_fwd(fpre, b_pad)[:, :N_HEADS]
    c_pieces = _bf16_pieces(c)
    one = jnp.ones((lp, N_HEADS), BF16)
    kaug = _aug_lanes(3 * [one] + [-x for x in c_pieces] + 3 * [one])
    first_block = _fox_first_blocks(qkv, c)
    o_a, lse, *gathered = _fox_fwd(qkv, _aug_lanes(c_pieces + 3 * [one] + 3 * [0 * one] + [first_block]), kaug,
                                   late_weights[0] if late_weights else None)
    if late_weights:
        w_fox, w_sb, w_out, w_up, conv_w, w_down = late_weights[1](gathered)
    ffn_block = ffn_block or w_up.shape[1] // 2
    o_b, lcar = _sb_fwd(qkv)
    h1, ya, yb, gated, mixed, xn3 = _mix_fwd(o_a, o_b, gf, h0, w_fox, w_sb, w_out, g1, g2)
    up = _mm(xn3, w_up, "nn", F32, "ffn_up")
    act = _conv_gelu_fwd(up, conv_w, conv_b, ffn_block)
    ffn = _mm(act, w_down, "nn", F32, "ffn_down")
    dy, loss_acc, d_ffn, dg3 = _out_loss(h1, ffn, g3, target)
    loss = loss_acc[0, 0]

    d_act = _mm(d_ffn, w_down, "nt", F32, "ffn_down_dx")
    gw_down = _mm(act, d_ffn, "tn", BF16, "ffn_down_dw")
    d_up, g_conv_w, g_conv_b = _conv_gelu_bwd(up, d_act, conv_w, conv_b, ffn_block)
    d_xn3 = _mm(d_up, w_up, "nt", F32, "ffn_up_dx")
    gw_up = _mm(xn3, d_up, "tn", BF16, "ffn_up_dw")
    dh1, dg2 = _rmsnorm_bwd(h1, g2, d_xn3, dy, F32, "norm3_bwd")

    d_mixed, dg1 = _rmsnorm_bwd(mixed, g1, dh1, None, BF16, "norm2_bwd")
    d_gated = _mm(d_mixed, w_out, "nt", F32, "out_dx")
    gw_out = _mm(gated, d_mixed, "tn", BF16, "out_dw")
    d_ya, d_yb, d_ga, d_gb = _gate_bwd(d_gated, gf, ya, yb)
    d_oa = _mm(d_ya, w_fox, "nt", F32, "fox_o_dx")
    gw_fox = _mm(o_a, d_ya, "tn", BF16, "fox_o_dw")
    d_ob = _mm(d_yb, w_sb, "nt", F32, "sb_o_dx")
    gw_sb = _mm(o_b, d_yb, "tn", BF16, "sb_o_dw")
    neg_lse = [-x for x in _bf16_pieces(lse.transpose(0, 2, 1, 3).reshape(N_HEADS, lp).T)]
    neg_dsum = [-x for x in _bf16_pieces(_head_dots(d_oa, o_a)[:, :N_HEADS])]
    qaug = _aug_lanes(c_pieces + 3 * [one] + neg_lse + [first_block])
    early = {"w_o_fox": gw_fox, "w_o_sb": gw_sb, "w_out": gw_out, "w_up": gw_up, "conv_w": g_conv_w,
             "conv_b": g_conv_b, "w_down": gw_down}
    dq_a, dk_a, dv_a, dcrow, *early_parts = _fox_bwd(
        qkv, qaug, kaug, _aug_lanes(neg_dsum), d_oa, early_grads(early) if early_grads else None)
    dq_b, dk_b, dv_b = _sb_bwd(qkv, lcar, d_ob)
    dc = dcrow.transpose(0, 2, 1, 3).reshape(N_HEADS, lp).T
    dc = jnp.concatenate([dc, jnp.zeros((lp, LANES - N_HEADS), F32)], axis=1)
    df, db = _forget_bwd(dc, fpre, b_pad)
    lane = jnp.arange(LANES) < N_HEADS
    df = jnp.where(lane[None, :], df, 0.0)
    d_proj = jnp.concatenate(
        [dq_a, dk_a.astype(BF16), dv_a.astype(BF16), dq_b, dk_b.astype(BF16), dv_b.astype(BF16),
         d_ga, d_gb, df.astype(BF16), jnp.zeros((lp, F_PAD - LANES), BF16)], axis=1)
    w_in_p = jnp.concatenate([w_qkv, w_gf], axis=1)
    d_xn1 = _mm(d_proj, w_in_p, "nt", F32, "proj_dx")
    gw_in_p = _mm(xn1, d_proj, "tn", BF16, "proj_dw")
    grad_x, grad_meta, dg0 = _first_norm_bwd(h0, g0, d_xn1, dh1, s)

    qkv_parts = jnp.split(gw_in_p[:, :6 * WIDTH], 6, axis=1)
    gw_in = jnp.concatenate(
        qkv_parts[:3] + [gw_in_p[:, 6 * WIDTH + 2 * d:6 * WIDTH + 2 * d + N_HEADS]] + qkv_parts[3:]
        + [gw_in_p[:, 6 * WIDTH:6 * WIDTH + 2 * d]], axis=1)
    grads = {
        "meta_tokens": grad_meta,
        "norm_gains": jnp.concatenate([dg0, dg1, dg2, dg3], axis=0),
        "w_in": gw_in,
        "b_forget": db[:, :N_HEADS],
        "early_parts": early_parts,
        **early,
    }
    return loss, grad_x, grads


MESH_IDS = pl.DeviceIdType.MESH


def _window(ref, kind, idx, rows, cols):
    if kind == "slots":
        return ref.at[idx]
    if kind == "gate_value":
        half = N_DEV // 2
        idx = jnp.where(idx < half, 2 * idx, 2 * (idx - half) + 1)
        kind = "cols"
    if kind == "cols":
        return ref.at[:, pl.ds(pl.multiple_of(idx * cols, cols & -cols), cols)]
    return ref.at[pl.ds(pl.multiple_of(idx * rows, rows & -rows), rows), :]


def _gathered_shape(shape, kind):
    rows, cols = shape
    return {"slots": (N_DEV, rows, cols), "cols": (rows, N_DEV * cols), "gate_value": (rows, N_DEV * cols),
            "rows": (N_DEV * rows, cols)}[kind]


def _all_gather(shards, kinds):
    n = len(shards)

    def body(*refs):
        ins, outs = refs[:n], refs[n:2 * n]
        send_sems, recv_sems, local_sems = refs[2 * n:]
        x, y, c = lax.axis_index("x"), lax.axis_index("y"), lax.axis_index("c")
        me, sibling = (x, y, c), (x, y, 1 - c)
        chips = [(1 - x, y), (x, 1 - y), (1 - x, 1 - y)]

        def part(t, px, py, pc):
            return _window(outs[t], kinds[t], 4 * px + 2 * py + pc, *shards[t].shape)

        def copy(k, t, blk, to, src=None):
            return pltpu.make_async_remote_copy(
                src_ref=part(t, *blk) if src is None else src, dst_ref=part(t, *blk),
                send_sem=send_sems.at[k, t], recv_sem=recv_sems.at[k, t],
                device_id=to, device_id_type=MESH_IDS)

        mine = [pltpu.make_async_copy(ins[t], part(t, *me), local_sems.at[t]) for t in range(n)]
        for cp in mine:
            cp.start()
        first = [copy(0, t, me, sibling, src=ins[t]) for t in range(n)]
        first += [copy(1 + j, t, me, (*chip, c), src=ins[t]) for j, chip in enumerate(chips) for t in range(n)]
        for cp in first:
            cp.start()
        passed = []
        for j, chip in enumerate(chips):
            for t in range(n):
                copy(1 + j, t, (*chip, c), me).wait_recv()
                passed.append(copy(4 + j, t, (*chip, c), sibling))
                passed[-1].start()
        for t in range(n):
            copy(0, t, sibling, me).wait_recv()
        for j, chip in enumerate(chips):
            for t in range(n):
                copy(4 + j, t, (*chip, 1 - c), me).wait_recv()
        for cp in first + passed:
            cp.wait_send()
        for cp in mine:
            cp.wait()

    any_space = pl.BlockSpec(memory_space=pl.ANY)
    return pl.pallas_call(
        body, name="all_gather",
        out_shape=tuple(jax.ShapeDtypeStruct(_gathered_shape(a.shape, k), a.dtype) for a, k in zip(shards, kinds)),
        in_specs=[any_space] * n,
        out_specs=tuple([any_space] * n),
        scratch_shapes=[pltpu.SemaphoreType.DMA((7, n)), pltpu.SemaphoreType.DMA((7, n)),
                        pltpu.SemaphoreType.DMA((n,))],
    )(*shards)


class _Exchange:
    def __init__(self, arrays, kinds, shard_shapes, gather):
        self.arrays, self.kinds, self.shard_shapes, self.gather = list(arrays), list(kinds), list(shard_shapes), gather
        self.n = n = len(self.arrays)
        self.any_specs = [pl.BlockSpec(memory_space=pl.ANY)] * n
        if gather:
            shapes = [_gathered_shape(a.shape, k) for a, k in zip(self.arrays, kinds)]
        else:
            shapes = [(N_DEV,) + tuple(s) for s in shard_shapes]
        self.out_shapes = [jax.ShapeDtypeStruct(s, a.dtype) for s, a in zip(shapes, self.arrays)]
        self.scratch = [pltpu.SemaphoreType.DMA((N_DEV - 1, n)), pltpu.SemaphoreType.DMA((N_DEV - 1, n)),
                        pltpu.SemaphoreType.DMA((n,))]

    def copies(self, ins, outs, sems):
        send_sems, recv_sems, local_sems = sems
        x, y, c = lax.axis_index("x"), lax.axis_index("y"), lax.axis_index("c")
        my = 4 * x + 2 * y + c

        def src(t, receiver):
            if self.gather or self.kinds[t] == "all":
                return ins[t]
            return _window(ins[t], self.kinds[t], receiver, *self.shard_shapes[t])

        def dst(t, sender):
            if self.gather:
                return _window(outs[t], self.kinds[t], sender, *self.shard_shapes[t])
            return outs[t].at[sender]

        local = [pltpu.make_async_copy(src(t, my), dst(t, my), local_sems.at[t]) for t in range(self.n)]
        sends, arrivals = [], []
        for rel in range(1, N_DEV):
            px, py, pc = x ^ (rel >> 2), y ^ ((rel >> 1) & 1), c ^ (rel & 1)
            peer = 4 * px + 2 * py + pc
            for t in range(self.n):
                common = dict(send_sem=send_sems.at[rel - 1, t], recv_sem=recv_sems.at[rel - 1, t],
                              device_id=(px, py, pc), device_id_type=MESH_IDS)
                sends.append(pltpu.make_async_remote_copy(src_ref=src(t, peer), dst_ref=dst(t, my), **common))
                arrivals.append(pltpu.make_async_remote_copy(src_ref=src(t, my), dst_ref=dst(t, peer), **common))
        return local, sends, arrivals

    def start(self, ins, outs, sems):
        local, sends, _ = self.copies(ins, outs, sems)
        for cp in local + sends:
            cp.start()

    def wait(self, ins, outs, sems):
        local, sends, arrivals = self.copies(ins, outs, sems)
        for cp in arrivals:
            cp.wait_recv()
        for cp in sends:
            cp.wait_send()
        for cp in local:
            cp.wait()


def _exchange(grads, kinds, shard_shapes):
    ex = _Exchange(grads, kinds, shard_shapes, gather=False)
    n = ex.n

    def body(*refs):
        ins, outs, sems = refs[:n], refs[n:2 * n], refs[2 * n:]
        ex.start(ins, outs, sems)
        ex.wait(ins, outs, sems)

    return pl.pallas_call(
        body, name="grad_exchange",
        out_shape=tuple(ex.out_shapes),
        in_specs=ex.any_specs,
        out_specs=tuple(ex.any_specs),
        scratch_shapes=ex.scratch,
    )(*grads)


def _sum_adamw(parts, w, m, v, name):
    rows, cols = w.shape
    n, rows_p, cols_p = parts.shape
    tr = _tile(rows, BLK, SUBLANES) if rows > BLK else rows
    tp = tr if rows_p == rows else rows_p
    c1 = 1.0 - ADAM_B1 ** ADAM_STEP
    c2 = 1.0 - ADAM_B2 ** ADAM_STEP

    def body(p_ref, w_ref, m_ref, v_ref, g_ref, d_ref, nm_ref, nv_ref):
        gv = p_ref[0, 0:tr, 0:cols].astype(F32)
        for s in range(1, n):
            gv = gv + p_ref[s, 0:tr, 0:cols].astype(F32)
        g_ref[...] = gv
        nm = ADAM_B1 * m_ref[...] + (1.0 - ADAM_B1) * gv
        nv = ADAM_B2 * v_ref[...] + (1.0 - ADAM_B2) * (gv * gv)
        m_hat = nm / c1
        v_hat = nv / c2
        d_ref[...] = -ADAM_LR * (m_hat / (jnp.sqrt(v_hat) + ADAM_EPS) + ADAM_WD * w_ref[...])
        nm_ref[...] = nm
        nv_ref[...] = nv

    spec = pl.BlockSpec((tr, cols), lambda i: (i, 0))
    out = jax.ShapeDtypeStruct((rows, cols), F32)
    return pl.pallas_call(
        body, name=name,
        out_shape=(out, out, out, out),
        grid=(rows // tr,),
        in_specs=[pl.BlockSpec((n, tp, cols_p), lambda i: (0, i, 0)), spec, spec, spec],
        out_specs=(spec, spec, spec, spec),
        compiler_params=_params(("parallel",)),
    )(parts, w, m, v)


def _pad2(a, rows, cols):
    return jnp.pad(a, ((0, rows - a.shape[0]), (0, cols - a.shape[1])))


WEIGHTS = ["meta_tokens", "norm_gains", "w_in", "b_forget", "w_o_fox", "w_o_sb", "w_out", "w_up", "conv_w",
           "conv_b", "w_down"]


def kernel(x, meta_tokens, norm_gains, w_in, b_forget, w_o_fox, w_o_sb, w_out, w_up, conv_w, conv_b, w_down, loss_target, m_meta_tokens, m_norm_gains, m_w_in, m_b_forget, m_w_o_fox, m_w_o_sb, m_w_out, m_w_up, m_conv_w, m_conv_b, m_w_down, v_meta_tokens, v_norm_gains, v_w_in, v_b_forget, v_w_o_fox, v_w_o_sb, v_w_out, v_w_up, v_conv_w, v_conv_b, v_w_down):
    w = dict(meta_tokens=meta_tokens, norm_gains=norm_gains, w_in=w_in, b_forget=b_forget, w_o_fox=w_o_fox,
             w_o_sb=w_o_sb, w_out=w_out, w_up=w_up, conv_w=conv_w, conv_b=conv_b, w_down=w_down)
    mom = dict(meta_tokens=m_meta_tokens, norm_gains=m_norm_gains, w_in=m_w_in, b_forget=m_b_forget,
               w_o_fox=m_w_o_fox, w_o_sb=m_w_o_sb, w_out=m_w_out, w_up=m_w_up, conv_w=m_conv_w, conv_b=m_conv_b,
               w_down=m_w_down)
    vel = dict(meta_tokens=v_meta_tokens, norm_gains=v_norm_gains, w_in=v_w_in, b_forget=v_b_forget,
               w_o_fox=v_w_o_fox, w_o_sb=v_w_o_sb, w_out=v_w_out, w_up=v_w_up, conv_w=v_conv_w, conv_b=v_conv_b,
               w_down=v_w_down)
    w2 = {n: a.reshape(a.shape[-2:]) for n, a in w.items()}
    shard_shape = {n: a.shape for n, a in w2.items()}

    d = x.shape[-1]
    up_cols = shard_shape["w_up"][1]
    up_pad = -(-up_cols // LANES) * LANES
    half = N_DEV // 2
    pad_rows = lambda a: _pad2(a, SUBLANES, a.shape[1])

    shards = [
        ("w_in", "slots", w2["w_in"].astype(BF16)),
        ("meta_tokens", "cols", w2["meta_tokens"]),
        ("norm_gains", "cols", pad_rows(w2["norm_gains"])),
    ]
    later = [
        ("w_o_fox", "cols", w2["w_o_fox"].astype(BF16)),
        ("w_o_sb", "cols", w2["w_o_sb"].astype(BF16)),
        ("w_out", "rows", w2["w_out"].astype(BF16)),
        ("w_up", "gate_value", _pad2(w2["w_up"], d, up_pad).astype(BF16)),
        ("conv_w", "gate_value", _pad2(w2["conv_w"], SUBLANES, up_pad)),
        ("w_down", "rows", w2["w_down"].astype(BF16)),
    ]
    full = dict(zip([s[0] for s in shards], _all_gather([s[2] for s in shards], [s[1] for s in shards])))
    w_in_full = jnp.concatenate([full["w_in"][i] for i in range(N_DEV)], axis=1)
    conv_b_p = jnp.pad(w2["conv_b"].reshape(2, half, up_cols), ((0, 0), (0, 0), (0, up_pad - up_cols)))
    conv_b_p = conv_b_p.transpose(1, 0, 2)

    def finish_gather(gathered):
        w_fox, w_sb, w_out_full, w_up_p, conv_w_p, w_down_full = gathered
        w_down_p = jnp.pad(w_down_full.reshape(half, up_cols, d), ((0, 0), (0, up_pad - up_cols), (0, 0)))
        return w_fox, w_sb, w_out_full, w_up_p, conv_w_p[:3], w_down_p.reshape(half * up_pad, d)

    early_names = ["w_o_fox", "w_o_sb", "w_out", "w_up", "conv_w", "conv_b", "w_down"]

    def early_exchange(g):
        sends = {
            "w_o_fox": ("cols", g["w_o_fox"], shard_shape["w_o_fox"]),
            "w_o_sb": ("cols", g["w_o_sb"], shard_shape["w_o_sb"]),
            "w_out": ("rows", g["w_out"], shard_shape["w_out"]),
            "w_up": ("gate_value", g["w_up"], (d, up_pad)),
            "conv_w": ("gate_value", pad_rows(g["conv_w"]), (SUBLANES, up_pad)),
            "conv_b": ("all", pad_rows(g["conv_b"].reshape(half, 2, up_pad).transpose(1, 0, 2)[:, :, :up_cols]
                                       .reshape(1, -1)),
                       (SUBLANES, N_DEV * up_cols)),
            "w_down": ("rows", g["w_down"].reshape(half, up_pad, d)[:, :up_cols].reshape(half * up_cols, d),
                       shard_shape["w_down"]),
        }
        return _Exchange([sends[n][1] for n in early_names], [sends[n][0] for n in early_names],
                         [sends[n][2] for n in early_names], gather=False)

    late_weights = (_Exchange([s[2] for s in later], [s[1] for s in later], [s[2].shape for s in later], gather=True),
                    finish_gather)
    loss, grad_x, grads = _local_step(
        x[0], loss_target[0], full["meta_tokens"], full["norm_gains"][:4], w_in_full, w2["b_forget"],
        None, None, None, None, None, conv_b_p.reshape(1, N_DEV * up_pad), None, ffn_block=up_pad,
        late_weights=late_weights, early_grads=early_exchange)
    loss = lax.psum(loss, ("x", "y", "c"))

    in_cols = shard_shape["w_in"][1]
    late_names = ["meta_tokens", "norm_gains", "w_in", "b_forget"]
    sends = {
        "meta_tokens": ("cols", grads["meta_tokens"], (N_META, LANES)),
        "norm_gains": ("cols", pad_rows(grads["norm_gains"]), (SUBLANES, LANES)),
        "w_in": ("slots", jnp.stack([grads["w_in"][:, i * in_cols:(i + 1) * in_cols] for i in range(N_DEV)]),
                 shard_shape["w_in"]),
        "b_forget": ("all", _pad2(grads["b_forget"], SUBLANES, LANES), (SUBLANES, LANES)),
    }
    parts = dict(zip(late_names, _exchange([sends[n][1] for n in late_names], [sends[n][0] for n in late_names],
                                           [sends[n][2] for n in late_names])))
    parts.update(zip(early_names, grads["early_parts"]))

    grad, delta, new_m, new_v = {}, {}, {}, {}
    for n in WEIGHTS:
        shape = w[n].shape
        outs = _sum_adamw(parts[n], w2[n], mom[n].reshape(shard_shape[n]), vel[n].reshape(shard_shape[n]),
                          "adamw_" + n)
        grad[n], delta[n], new_m[n], new_v[n] = (o.reshape(shape) for o in outs)

    return (loss, grad_x[None], *[grad[n] for n in WEIGHTS], *[delta[n] for n in WEIGHTS],
            *[new_m[n] for n in WEIGHTS], *[new_v[n] for n in WEIGHTS])
```

```python
import functools
import math

import jax
import jax.numpy as jnp
from jax import lax
from jax.experimental import pallas as pl
from jax.experimental.pallas import tpu as pltpu

F32 = jnp.float32
BF16 = jnp.bfloat16

N_DEV = 8
N_META = 16
HEAD_DIM = 64
N_HEADS = 8
WIDTH = N_HEADS * HEAD_DIM
N_PAIRS = N_HEADS // 2
LANES = 128
SUBLANES = 8
EPS = 1e-6
ATT_SCALE = HEAD_DIM ** -0.5
BLK = 256
F_PAD = 256
VMEM_LIMIT = 48 << 20

ADAM_LR = 0.001
ADAM_B1 = 0.9
ADAM_B2 = 0.999
ADAM_EPS = 1e-08
ADAM_WD = 0.01
ADAM_STEP = 10

GELU_C = math.sqrt(2.0 / math.pi)
GELU_A = 0.044715
EXP_IS_ZERO = -110.0
NOT_VISITED = -1e30


def _params(sem, vmem=VMEM_LIMIT):
    return pltpu.CompilerParams(dimension_semantics=sem, vmem_limit_bytes=vmem)


def _tile(dim, cap, align=LANES):
    t = (min(cap, dim) // align) * align
    while t >= align:
        if dim % t == 0:
            return t
        t -= align
    return dim


def _log_sigmoid_parts(z):
    lp = jnp.log1p(jnp.exp(-jnp.abs(z)))
    return jnp.minimum(z, 0.0) - lp, jnp.minimum(-z, 0.0) - lp


def _sigmoid(x):
    return 1.0 / (1.0 + jnp.exp(-x))


def _split_bf16(x):
    hi = x.astype(BF16)
    lo = (x - hi.astype(F32)).astype(BF16)
    return hi, lo


def _dot(a, b, dims):
    return lax.dot_general(a, b, (dims, ((), ())), preferred_element_type=F32)


NN = ((1,), (0,))
NT = ((1,), (1,))
TN = ((0,), (0,))


def _mm(a, b, mode, out_dtype, name, exchange=None):
    if mode == "nn":
        (m, kc), (_, n) = a.shape, b.shape
    elif mode == "nt":
        (m, kc), (n, _) = a.shape, b.shape
    else:
        (kc, m), (_, n) = a.shape, b.shape
    if mode == "tn":
        tm, tn, tk = _tile(m, 1024), _tile(n, 1792), _tile(kc, 1408)
    else:
        tm, tn, tk = _tile(m, 1408), _tile(n, 1024), _tile(kc, 1536)
    nk = kc // tk
    dims = {"nn": NN, "nt": NT, "tn": TN}[mode]
    grid = (n // tn, m // tm, nk)
    split, first_step, last_step = _riding(exchange, 2, 1, 1 if nk > 1 else 0, grid)

    def body(*refs):
        (a_ref, b_ref, o_ref, *scratch), ex_refs = split(refs)
        first_step(ex_refs)
        k = pl.program_id(2)
        part = _dot(a_ref[...].astype(BF16), b_ref[...].astype(BF16), dims)
        if nk == 1:
            o_ref[...] = part.astype(out_dtype)
        else:
            acc_ref, = scratch

            @pl.when(k == 0)
            def _():
                acc_ref[...] = part

            @pl.when(k > 0)
            def _():
                acc_ref[...] += part

            @pl.when(k == nk - 1)
            def _():
                o_ref[...] = acc_ref[...].astype(out_dtype)
        last_step(ex_refs)

    if mode == "tn":
        a_spec = pl.BlockSpec((tk, tm), lambda j, i, k: (k, i))
    else:
        a_spec = pl.BlockSpec((tm, tk), lambda j, i, k: (i, k))
    if mode == "nt":
        b_spec = pl.BlockSpec((tn, tk), lambda j, i, k: (j, k))
    else:
        b_spec = pl.BlockSpec((tk, tn), lambda j, i, k: (k, j))
    ex = exchange
    out = pl.pallas_call(
        body, name=name,
        out_shape=(jax.ShapeDtypeStruct((m, n), out_dtype),) + tuple(ex.out_shapes if ex else ()),
        grid=grid,
        in_specs=[a_spec, b_spec] + (ex.any_specs if ex else []),
        out_specs=(pl.BlockSpec((tm, tn), lambda j, i, k: (i, j)),) + tuple(ex.any_specs if ex else ()),
        scratch_shapes=([pltpu.VMEM((tm, tn), F32)] if nk > 1 else []) + (ex.scratch if ex else []),
        compiler_params=_params(("arbitrary",) * 3 if ex else ("parallel", "parallel", "arbitrary")),
    )(a, b, *(ex.arrays if ex else []))
    return out if ex else out[0]


def _shifted_rows(cur_ref, prev_ref, first=None):
    head = prev_ref[...] if first is None else jnp.where(pl.program_id(0) == 0, first, prev_ref[...])
    return jnp.concatenate([head, cur_ref[0:BLK - N_META, :]], axis=0)


def _shifted_specs(s, d):
    per = BLK // N_META
    return (pl.BlockSpec((BLK, d), lambda i: (jnp.minimum(i, s // BLK - 1), 0)),
            pl.BlockSpec((N_META, d), lambda i: (jnp.maximum(per * i - 1, 0), 0)))


def _first_norm_fwd(x, meta, g, lp):
    s, d = x.shape
    n_valid = N_META + s

    def body(x_ref, p_ref, m_ref, g_ref, h_ref, o_ref):
        row = pl.program_id(0) * BLK + lax.broadcasted_iota(jnp.int32, (BLK, 1), 0)
        h = jnp.where(row < n_valid, _shifted_rows(x_ref, p_ref, m_ref[...]), 0.0)
        h_ref[...] = h
        r = lax.rsqrt(jnp.mean(h * h, axis=-1, keepdims=True) + EPS)
        o_ref[...] = ((h * r) * g_ref[...]).astype(BF16)

    row_d = pl.BlockSpec((BLK, d), lambda i: (i, 0))
    return pl.pallas_call(
        body, name="norm1_fwd",
        out_shape=(jax.ShapeDtypeStruct((lp, d), F32), jax.ShapeDtypeStruct((lp, d), BF16)),
        grid=(lp // BLK,),
        in_specs=[*_shifted_specs(s, d), pl.BlockSpec((N_META, d), lambda i: (0, 0)),
                  pl.BlockSpec((1, d), lambda i: (0, 0))],
        out_specs=(row_d, row_d),
        compiler_params=_params(("parallel",)),
    )(x, x, meta, g)


def _first_norm_bwd(h0, g, dy, resid, s):
    lp, d = h0.shape
    nb = lp // BLK
    assert nb == s // BLK + 1

    def body(x_ref, g_ref, dy_ref, r_ref, gx_ref, gm_ref, dg_ref, keep_ref):
        i = pl.program_id(0)
        xv = x_ref[...]
        dyv = dy_ref[...]
        r = lax.rsqrt(jnp.mean(xv * xv, axis=-1, keepdims=True) + EPS)
        xh = xv * r
        dyg = dyv * g_ref[...]
        dx = r * (dyg - xh * jnp.mean(dyg * xh, axis=-1, keepdims=True)) + r_ref[...]

        @pl.when(i == 0)
        def _():
            dg_ref[...] = jnp.zeros_like(dg_ref)
            keep_ref[...] = jnp.zeros_like(keep_ref)
            gm_ref[...] = dx[0:N_META, :]

        gx_ref[...] = jnp.concatenate([keep_ref[...], dx[0:N_META, :]], axis=0)
        keep_ref[...] = dx[N_META:BLK, :]
        dg_ref[...] += jnp.sum(dyv * xh, axis=0, keepdims=True)

    row = pl.BlockSpec((BLK, d), lambda i: (i, 0))
    vec = pl.BlockSpec((1, d), lambda i: (0, 0))
    return pl.pallas_call(
        body, name="norm1_bwd",
        out_shape=(jax.ShapeDtypeStruct((s, d), F32), jax.ShapeDtypeStruct((N_META, d), F32),
                   jax.ShapeDtypeStruct((1, d), F32)),
        grid=(nb,),
        in_specs=[row, vec, row, row],
        out_specs=(pl.BlockSpec((BLK, d), lambda i: (jnp.maximum(i - 1, 0), 0)),
                   pl.BlockSpec((N_META, d), lambda i: (0, 0)), vec),
        scratch_shapes=[pltpu.VMEM((BLK - N_META, d), F32)],
        compiler_params=_params(("arbitrary",)),
    )(h0, g, dy, resid)


def _rmsnorm_bwd(x, g, dy, resid, out_dtype, name):
    lp, d = x.shape
    has_resid = resid is not None

    def body(*refs):
        if has_resid:
            x_ref, g_ref, dy_ref, r_ref, dx_ref, dg_ref = refs
        else:
            x_ref, g_ref, dy_ref, dx_ref, dg_ref = refs
        i = pl.program_id(0)
        xv = x_ref[...]
        dyv = dy_ref[...].astype(F32)
        r = lax.rsqrt(jnp.mean(xv * xv, axis=-1, keepdims=True) + EPS)
        xh = xv * r
        dyg = dyv * g_ref[...]
        dx = r * (dyg - xh * jnp.mean(dyg * xh, axis=-1, keepdims=True))
        if has_resid:
            dx = dx + r_ref[...]
        dx_ref[...] = dx.astype(out_dtype)

        @pl.when(i == 0)
        def _():
            dg_ref[...] = jnp.zeros_like(dg_ref)

        dg_ref[...] += jnp.sum(dyv * xh, axis=0, keepdims=True)

    row = pl.BlockSpec((BLK, d), lambda i: (i, 0))
    vec = pl.BlockSpec((1, d), lambda i: (0, 0))
    ins = [x, g, dy] + ([resid] if has_resid else [])
    in_specs = [row, vec, row] + ([row] if has_resid else [])
    return pl.pallas_call(
        body, name=name,
        out_shape=(jax.ShapeDtypeStruct((lp, d), out_dtype), jax.ShapeDtypeStruct((1, d), F32)),
        grid=(lp // BLK,),
        in_specs=in_specs,
        out_specs=(row, vec),
        compiler_params=_params(("arbitrary",)),
    )(*ins)


def _forget_fwd(fpre, b_pad):
    lp = fpre.shape[0]

    def body(f_ref, b_ref, c_ref, carry_ref):
        i = pl.program_id(0)

        @pl.when(i == 0)
        def _():
            carry_ref[...] = jnp.zeros_like(carry_ref)

        logf, _ = _log_sigmoid_parts(f_ref[...] + b_ref[...])
        row = lax.broadcasted_iota(jnp.int32, (BLK, BLK), 0)
        col = lax.broadcasted_iota(jnp.int32, (BLK, BLK), 1)
        tri = (col <= row).astype(BF16)
        p0 = logf.astype(BF16)
        r1 = logf - p0.astype(F32)
        p1 = r1.astype(BF16)
        p2 = (r1 - p1.astype(F32)).astype(BF16)
        c = _dot(tri, p0, NN) + _dot(tri, p1, NN) + _dot(tri, p2, NN) + carry_ref[0:1, :]
        c_ref[...] = c
        carry_ref[...] = jnp.broadcast_to(c[BLK - 1:BLK, :], carry_ref.shape)

    return pl.pallas_call(
        body, name="forget_fwd",
        out_shape=jax.ShapeDtypeStruct((lp, LANES), F32),
        grid=(lp // BLK,),
        in_specs=[pl.BlockSpec((BLK, LANES), lambda i: (i, 0)), pl.BlockSpec((1, LANES), lambda i: (0, 0))],
        out_specs=pl.BlockSpec((BLK, LANES), lambda i: (i, 0)),
        scratch_shapes=[pltpu.VMEM((SUBLANES, LANES), F32)],
        compiler_params=_params(("arbitrary",)),
    )(fpre, b_pad)


def _forget_bwd(dc, fpre, b_pad):
    lp = fpre.shape[0]
    nb = lp // BLK

    def body(dc_ref, f_ref, b_ref, df_ref, db_ref, carry_ref):
        i = pl.program_id(0)

        @pl.when(i == 0)
        def _():
            carry_ref[...] = jnp.zeros_like(carry_ref)
            db_ref[...] = jnp.zeros_like(db_ref)

        dcv = dc_ref[...]
        row = lax.broadcasted_iota(jnp.int32, (BLK, BLK), 0)
        col = lax.broadcasted_iota(jnp.int32, (BLK, BLK), 1)
        tri = (col >= row).astype(BF16)
        p0 = dcv.astype(BF16)
        r1 = dcv - p0.astype(F32)
        p1 = r1.astype(BF16)
        p2 = (r1 - p1.astype(F32)).astype(BF16)
        dlogf = _dot(tri, p0, NN) + _dot(tri, p1, NN) + _dot(tri, p2, NN) + carry_ref[0:1, :]
        carry_ref[...] = jnp.broadcast_to(dlogf[0:1, :], carry_ref.shape)
        _, ls_neg = _log_sigmoid_parts(f_ref[...] + b_ref[...])
        df = dlogf * jnp.exp(ls_neg)
        df_ref[...] = df
        db_ref[...] += jnp.sum(df, axis=0, keepdims=True)

    rev = pl.BlockSpec((BLK, LANES), lambda i: (nb - 1 - i, 0))
    vec = pl.BlockSpec((1, LANES), lambda i: (0, 0))
    return pl.pallas_call(
        body, name="forget_bwd",
        out_shape=(jax.ShapeDtypeStruct((lp, LANES), F32), jax.ShapeDtypeStruct((1, LANES), F32)),
        grid=(nb,),
        in_specs=[rev, rev, vec],
        out_specs=(rev, vec),
        scratch_shapes=[pltpu.VMEM((SUBLANES, LANES), F32)],
        compiler_params=_params(("arbitrary",)),
    )(dc, fpre, b_pad)


def _mix_fwd(o_a, o_b, gates, h0, w_fox, w_sb, w_out, g1, g2):
    lp, d = h0.shape

    def body(oa_ref, ob_ref, ga_ref, gb_ref, h_ref, wf_ref, ws_ref, wo_ref, g_ref, g2_ref,
             h1_ref, ya_ref, yb_ref, gated_ref, mixed_ref, xn_ref):
        ya = _dot(oa_ref[...].astype(BF16), wf_ref[...], NN)
        yb = _dot(ob_ref[...].astype(BF16), ws_ref[...], NN)
        gated = _sigmoid(ga_ref[...]) * ya + _sigmoid(gb_ref[...]) * yb
        gb16 = gated.astype(BF16)
        mixed = _dot(gb16, wo_ref[...], NN)
        r = lax.rsqrt(jnp.mean(mixed * mixed, axis=-1, keepdims=True) + EPS)
        h1 = h_ref[...] + (mixed * r) * g_ref[...]
        h1_ref[...] = h1
        r2 = lax.rsqrt(jnp.mean(h1 * h1, axis=-1, keepdims=True) + EPS)
        xn_ref[...] = ((h1 * r2) * g2_ref[...]).astype(BF16)
        ya_ref[...] = ya
        yb_ref[...] = yb
        gated_ref[...] = gb16
        mixed_ref[...] = mixed

    row_w = pl.BlockSpec((BLK, WIDTH), lambda i: (i, 0))
    row_d = pl.BlockSpec((BLK, d), lambda i: (i, 0))
    full = lambda s: pl.BlockSpec(s, lambda i: (0, 0))
    return pl.pallas_call(
        body, name="mix_fwd",
        out_shape=(jax.ShapeDtypeStruct((lp, d), F32), jax.ShapeDtypeStruct((lp, d), F32),
                   jax.ShapeDtypeStruct((lp, d), F32), jax.ShapeDtypeStruct((lp, d), BF16),
                   jax.ShapeDtypeStruct((lp, d), F32), jax.ShapeDtypeStruct((lp, d), BF16)),
        grid=(lp // BLK,),
        in_specs=[row_w, row_w, row_d, pl.BlockSpec((BLK, d), lambda i: (i, 1)), row_d,
                  full((WIDTH, d)), full((WIDTH, d)), full((d, d)), full((1, d)), full((1, d))],
        out_specs=(row_d, row_d, row_d, row_d, row_d, row_d),
        compiler_params=_params(("parallel",)),
    )(o_a, o_b, gates, gates, h0, w_fox, w_sb, w_out, g1, g2)


def _gate_bwd(d_gated, gates, ya, yb):
    lp, d = d_gated.shape

    def body(dg_ref, ga_ref, gb_ref, ya_ref, yb_ref, dya_ref, dyb_ref, dga_ref, dgb_ref):
        dg = dg_ref[...]
        sa = _sigmoid(ga_ref[...])
        sb = _sigmoid(gb_ref[...])
        dya_ref[...] = (dg * sa).astype(BF16)
        dyb_ref[...] = (dg * sb).astype(BF16)
        dga_ref[...] = (dg * ya_ref[...] * (sa * (1.0 - sa))).astype(BF16)
        dgb_ref[...] = (dg * yb_ref[...] * (sb * (1.0 - sb))).astype(BF16)

    row = pl.BlockSpec((BLK, d), lambda i: (i, 0))
    out = jax.ShapeDtypeStruct((lp, d), BF16)
    return pl.pallas_call(
        body, name="gate_bwd",
        out_shape=(out, out, out, out),
        grid=(lp // BLK,),
        in_specs=[row, row, pl.BlockSpec((BLK, d), lambda i: (i, 1)), row, row],
        out_specs=(row, row, row, row),
        compiler_params=_params(("parallel",)),
    )(d_gated, gates, gates, ya, yb)


def _shift_down(cur, prev, n):
    rolled = pltpu.roll(cur, n, 0)
    row = lax.broadcasted_iota(jnp.int32, prev.shape, 0)
    head = jnp.where(row < n, pltpu.roll(prev, n, 0), rolled[0:SUBLANES])
    return head if cur.shape[0] == SUBLANES else jnp.concatenate([head, rolled[SUBLANES:]], axis=0)


def _shift_up(cur, nxt, n):
    rows = cur.shape[0]
    rolled = pltpu.roll(cur, rows - n, 0)
    row = lax.broadcasted_iota(jnp.int32, nxt.shape, 0)
    tail = jnp.where(row >= SUBLANES - n, pltpu.roll(nxt, SUBLANES - n, 0), rolled[rows - SUBLANES:])
    return tail if rows == SUBLANES else jnp.concatenate([rolled[:rows - SUBLANES], tail], axis=0)


def _gelu(x):
    return 0.5 * x * (1.0 + jnp.tanh(GELU_C * (x + GELU_A * (x * x * x))))


def _gelu_and_grad(x):
    t = jnp.tanh(GELU_C * (x + GELU_A * (x * x * x)))
    half = 0.5 * (1.0 + t)
    return x * half, half + 0.5 * x * (1.0 - t * t) * (GELU_C * (1.0 + 3.0 * GELU_A * (x * x)))


def _conv_taps(cur, prev, w_ref, b_ref):
    s1 = _shift_down(cur, prev, 1)
    s2 = _shift_down(cur, prev, 2)
    u = b_ref[...] + w_ref[0:1, :] * s2
    u = u + w_ref[1:2, :] * s1
    u = u + w_ref[2:3, :] * cur
    return u, s1, s2


def _conv_gelu_fwd(up, conv_w, conv_b, tc):
    lp, f2 = up.shape
    rb = BLK // SUBLANES

    def body(u_ref, p_ref, w_ref, b_ref, act_ref):
        i = pl.program_id(0)
        keep = (i > 0).astype(F32)
        u, _, _ = _conv_taps(u_ref[...], p_ref[...] * keep, w_ref, b_ref)
        act_ref[...] = (_gelu(u[:, :tc]) * u[:, tc:]).astype(BF16)

    prev_row = lambda i: jnp.maximum(i * rb - 1, 0)
    return pl.pallas_call(
        body, name="conv_gelu_fwd",
        out_shape=jax.ShapeDtypeStruct((lp, f2 // 2), BF16),
        grid=(lp // BLK, f2 // (2 * tc)),
        in_specs=[pl.BlockSpec((BLK, 2 * tc), lambda i, j: (i, j)),
                  pl.BlockSpec((SUBLANES, 2 * tc), lambda i, j: (prev_row(i), j)),
                  pl.BlockSpec((3, 2 * tc), lambda i, j: (0, j)),
                  pl.BlockSpec((1, 2 * tc), lambda i, j: (0, j))],
        out_specs=pl.BlockSpec((BLK, tc), lambda i, j: (i, j)),
        compiler_params=_params(("parallel", "parallel")),
    )(up, up, conv_w, conv_b)


def _conv_gelu_bwd(up, d_act, conv_w, conv_b, tc):
    lp, f2 = up.shape
    nb = lp // BLK
    rb = BLK // SUBLANES

    def du_of(u, da):
        gel, grad = _gelu_and_grad(u[:, :tc])
        return jnp.concatenate([da * u[:, tc:] * grad, da * gel], axis=1)

    def body(u_ref, p_ref, n_ref, da_ref, dan_ref, w_ref, b_ref, dup_ref, dcw_ref, dcb_ref):
        i = pl.program_id(1)
        cur = u_ref[...]
        u, s1, s2 = _conv_taps(cur, p_ref[...] * (i > 0).astype(F32), w_ref, b_ref)
        du = du_of(u, da_ref[...])
        u_next, _, _ = _conv_taps(n_ref[...], cur[BLK - SUBLANES:BLK, :], w_ref, b_ref)
        du_next = du_of(u_next, dan_ref[...]) * (i < nb - 1).astype(F32)
        n1 = _shift_up(du, du_next, 1)
        n2 = _shift_up(du, du_next, 2)
        dup_ref[...] = (w_ref[2:3, :] * du + w_ref[1:2, :] * n1 + w_ref[0:1, :] * n2).astype(BF16)

        @pl.when(i == 0)
        def _():
            dcw_ref[...] = jnp.zeros_like(dcw_ref)
            dcb_ref[...] = jnp.zeros_like(dcb_ref)

        dcw_ref[0:1, :] += jnp.sum(du * s2, axis=0, keepdims=True)
        dcw_ref[1:2, :] += jnp.sum(du * s1, axis=0, keepdims=True)
        dcw_ref[2:3, :] += jnp.sum(du * cur, axis=0, keepdims=True)
        dcb_ref[...] += jnp.sum(du, axis=0, keepdims=True)

    prev_row = lambda i: jnp.maximum(i * rb - 1, 0)
    next_row = lambda i: jnp.minimum((i + 1) * rb, nb * rb - 1)
    return pl.pallas_call(
        body, name="conv_gelu_bwd",
        out_shape=(jax.ShapeDtypeStruct((lp, f2), BF16), jax.ShapeDtypeStruct((3, f2), F32),
                   jax.ShapeDtypeStruct((1, f2), F32)),
        grid=(f2 // (2 * tc), nb),
        in_specs=[pl.BlockSpec((BLK, 2 * tc), lambda j, i: (i, j)),
                  pl.BlockSpec((SUBLANES, 2 * tc), lambda j, i: (prev_row(i), j)),
                  pl.BlockSpec((SUBLANES, 2 * tc), lambda j, i: (next_row(i), j)),
                  pl.BlockSpec((BLK, tc), lambda j, i: (i, j)),
                  pl.BlockSpec((SUBLANES, tc), lambda j, i: (next_row(i), j)),
                  pl.BlockSpec((3, 2 * tc), lambda j, i: (0, j)),
                  pl.BlockSpec((1, 2 * tc), lambda j, i: (0, j))],
        out_specs=(pl.BlockSpec((BLK, 2 * tc), lambda j, i: (i, j)),
                   pl.BlockSpec((3, 2 * tc), lambda j, i: (0, j)),
                   pl.BlockSpec((1, 2 * tc), lambda j, i: (0, j))),
        compiler_params=_params(("parallel", "arbitrary")),
    )(up, up, up, d_act, d_act, conv_w, conv_b)


def _out_loss(h1, ffn, g3, target):
    lp, d = h1.shape
    s = target.shape[0]
    n_valid = N_META + s

    def body(h_ref, f_ref, g_ref, t_ref, tp_ref, dy_ref, loss_ref, df_ref, dg_ref):
        i = pl.program_id(0)

        @pl.when(i == 0)
        def _():
            loss_ref[...] = jnp.zeros_like(loss_ref)
            dg_ref[...] = jnp.zeros_like(dg_ref)

        fv = f_ref[...]
        r = lax.rsqrt(jnp.mean(fv * fv, axis=-1, keepdims=True) + EPS)
        xh = fv * r
        y = h_ref[...] + xh * g_ref[...]
        row = i * BLK + lax.broadcasted_iota(jnp.int32, (BLK, 1), 0)
        valid = (row >= N_META) & (row < n_valid)
        diff = jnp.where(valid, y - _shifted_rows(t_ref, tp_ref), 0.0)
        dy = diff * (1.0 / d)
        dy_ref[...] = dy
        per_row = jnp.mean(diff * diff, axis=-1, keepdims=True)
        loss_ref[...] += 0.5 * jnp.sum(per_row, axis=0, keepdims=True)
        dyg = dy * g_ref[...]
        df_ref[...] = (r * (dyg - xh * jnp.mean(dyg * xh, axis=-1, keepdims=True))).astype(BF16)
        dg_ref[...] += jnp.sum(dy * xh, axis=0, keepdims=True)

    row_d = pl.BlockSpec((BLK, d), lambda i: (i, 0))
    vec = pl.BlockSpec((1, d), lambda i: (0, 0))
    return pl.pallas_call(
        body, name="out_loss",
        out_shape=(jax.ShapeDtypeStruct((lp, d), F32), jax.ShapeDtypeStruct((SUBLANES, LANES), F32),
                   jax.ShapeDtypeStruct((lp, d), BF16), jax.ShapeDtypeStruct((1, d), F32)),
        grid=(lp // BLK,),
        in_specs=[row_d, row_d, vec, *_shifted_specs(s, d)],
        out_specs=(row_d, pl.BlockSpec((SUBLANES, LANES), lambda i: (0, 0)), row_d, vec),
        compiler_params=_params(("arbitrary",)),
    )(h1, ffn, g3, target, target)


def _head_masks(rows=BLK):
    lane = lax.broadcasted_iota(jnp.int32, (rows, LANES), 1)
    return [lane < HEAD_DIM, lane >= HEAD_DIM]


def _att_specs(base, lp):
    q_spec = pl.BlockSpec((BLK, LANES), lambda p, i: (i, base + p))
    k_spec = pl.BlockSpec((lp, LANES), lambda p, i: (0, base + N_PAIRS + p))
    v_spec = pl.BlockSpec((lp, LANES), lambda p, i: (0, base + 2 * N_PAIRS + p))
    return q_spec, k_spec, v_spec


def _kv_rows(j, nb=1):
    return pl.ds(pl.multiple_of(j * BLK, nb * BLK), nb * BLK)


def _walk_kv(i, tile, reverse=False, first_pair=0, more=None):
    pairs, odd = i // 2, i % 2

    def wide(t, carry):
        tile(2 * (pairs - 1 - t) if reverse else 2 * t, 2, False)
        return carry

    def single():
        @pl.when(odd == 1)
        def _():
            tile(i - 1, 1, False)

    if reverse:
        tile(i, 1, True)
        single()
        if more is None:
            lax.fori_loop(0, pairs, wide, 0)
        else:
            lax.while_loop(lambda c: (c[0] < pairs) & c[1], lambda c: (wide(c[0], c[0]) + 1, more()), (0, more()))
    else:
        lax.fori_loop(first_pair, pairs, wide, 0)
        single()
        tile(i, 1, True)


def _walk_kv_ahead(i, lead, follow, kept, first_pair=0):
    pairs, odd = i // 2, i % 2

    def keep(values):
        for ref, value in zip(kept, values):
            ref[...] = value

    @pl.when(pairs > first_pair)
    def _():
        keep(lead(2 * first_pair, 2))

    def wide(t, carry):
        ahead = lead(2 * jnp.minimum(t + 1, pairs - 1), 2)
        follow([ref[...] for ref in kept], 2 * t, 2, False)
        keep(ahead)
        return carry

    lax.fori_loop(first_pair, pairs, wide, 0)

    @pl.when(odd == 1)
    def _():
        follow(lead(i - 1, 1), i - 1, 1, False)

    follow(lead(i, 1), i, 1, True)


def _bf16_pieces(x):
    rnd = lambda a: lax.reduce_precision(a, exponent_bits=8, mantissa_bits=7)
    p0 = rnd(x)
    p1 = rnd(x - p0)
    p2 = rnd(x - p0 - p1)
    return [p0.astype(BF16), p1.astype(BF16), p2.astype(BF16)]


def _aug_lanes(cols):
    lp = cols[0].shape[0]
    vals = jnp.stack([c.astype(BF16) for c in cols], axis=-1)
    vals = vals.reshape(lp, N_PAIRS, 2, len(cols))[:, :, ::-1, :]
    vals = jnp.pad(vals, ((0, 0), (0, 0), (0, 0), (0, HEAD_DIM - len(cols))))
    return vals.reshape(lp, WIDTH)


N_AUG = 3


def _riding(exchange, n_in, n_out, n_scratch, grid):
    n = exchange.n if exchange else 0

    def split(refs):
        own_in, ex_in = refs[:n_in], refs[n_in:n_in + n]
        own_out, ex_out = refs[n_in + n:n_in + n + n_out], refs[n_in + n + n_out:n_in + 2 * n + n_out]
        rest = refs[n_in + 2 * n + n_out:]
        return own_in + own_out + rest[:n_scratch], (ex_in, ex_out, rest[n_scratch:])

    def at(step_of):
        here = pl.program_id(0) == step_of(grid[0])
        for axis in range(1, len(grid)):
            here = here & (pl.program_id(axis) == step_of(grid[axis]))
        return here

    def first_step(ex_refs):
        if exchange:
            @pl.when(at(lambda size: 0))
            def _():
                exchange.start(*ex_refs)

    def last_step(ex_refs):
        if exchange:
            @pl.when(at(lambda size: size - 1))
            def _():
                exchange.wait(*ex_refs)

    return split, first_step, last_step


def _fox_fwd(qkv, qaug, kaug, exchange=None):
    lp = qkv.shape[0]
    nq = lp // BLK
    split, first_step, last_step = _riding(exchange, 5, 2, 3, (N_PAIRS, nq))

    def body(*refs):
        (q_ref, k_ref, v_ref, qa_ref, ka_ref, o_ref, lse_ref, acc_ref, m_ref, s_ref), ex_refs = split(refs)
        first_step(ex_refs)
        i = pl.program_id(1)
        hs = range(2)
        masks = _head_masks()
        qs = q_ref[...] * ATT_SCALE
        qa = qa_ref[...]
        qh = [jnp.where(masks[hh], qs, qa) for hh in hs]
        acc_ref[...] = jnp.zeros_like(acc_ref)
        m_ref[...] = jnp.full_like(m_ref, -1e30)
        row = lax.broadcasted_iota(jnp.int32, (BLK, BLK), 0)
        col = lax.broadcasted_iota(jnp.int32, (BLK, BLK), 1)
        causal = col <= row

        def scores(j, nb):
            rows = _kv_rows(j, nb)
            kmasks = _head_masks(nb * BLK)
            k, ka = k_ref[rows, :], ka_ref[rows, :]
            return [_dot(qh[hh], jnp.where(kmasks[hh], k, ka), NT) for hh in hs]

        def absorb(s, j, nb, diag):
            v = v_ref[_kv_rows(j, nb), :]
            kmasks = _head_masks(nb * BLK)
            vh = [jnp.where(kmasks[hh], v, jnp.ones_like(v)) for hh in hs]
            if diag:
                s = [jnp.where(causal, s[hh], -1e30) for hh in hs]
            m_prev = [m_ref[hh] for hh in hs]
            m_new = [jnp.maximum(m_prev[hh], jnp.max(s[hh], axis=-1, keepdims=True)) for hh in hs]
            p = [jnp.exp(s[hh] - m_new[hh]).astype(BF16) for hh in hs]
            for hh in hs:
                acc_ref[hh] = jnp.exp(m_prev[hh] - m_new[hh]) * acc_ref[hh] + _dot(p[hh], vh[hh], NN)
                m_ref[hh] = m_new[hh]

        _walk_kv_ahead(i, scores, absorb, [s_ref.at[hh] for hh in hs], _first_pair(qa, i))
        acc = [acc_ref[hh] for hh in hs]
        denom = [acc[0][:, HEAD_DIM:HEAD_DIM + 1], acc[1][:, 0:1]]
        o_ref[...] = jnp.where(masks[0], acc[0] / denom[0], acc[1] / denom[1])
        lse = jnp.where(masks[0], m_ref[0] + jnp.log(denom[0]), m_ref[1] + jnp.log(denom[1])).T
        lse_ref[0, 0, 0:1, :] = lse[0:1, :]
        lse_ref[0, 0, 1:2, :] = lse[HEAD_DIM:HEAD_DIM + 1, :]
        last_step(ex_refs)

    q_spec, k_spec, v_spec = _att_specs(0, lp)
    blk = pl.BlockSpec((BLK, LANES), lambda p, i: (i, p))
    col_full = pl.BlockSpec((lp, LANES), lambda p, i: (0, p))
    out = jax.ShapeDtypeStruct((lp, WIDTH), F32)
    ex = exchange
    return pl.pallas_call(
        body, name="fox_fwd",
        out_shape=(out, jax.ShapeDtypeStruct((N_PAIRS, nq, 2, BLK), F32)) + tuple(ex.out_shapes if ex else ()),
        grid=(N_PAIRS, nq),
        in_specs=[q_spec, k_spec, v_spec, blk, col_full] + (ex.any_specs if ex else []),
        out_specs=(blk, pl.BlockSpec((1, 1, 2, BLK), lambda p, i: (p, i, 0, 0))) + tuple(ex.any_specs if ex else ()),
        scratch_shapes=[pltpu.VMEM((2, BLK, LANES), F32), pltpu.VMEM((2, BLK, 1), F32),
                        pltpu.VMEM((2, BLK, 2 * BLK), F32)] + (ex.scratch if ex else []),
        compiler_params=_params(("arbitrary", "arbitrary") if ex else ("parallel", "parallel")),
    )(qkv, qkv, qkv, qaug, kaug, *(ex.arrays if ex else []))


FIRST_BLOCK_LANE = 9


def _first_pair(qa, i):
    lane = lax.broadcasted_iota(jnp.int32, qa.shape, 1)
    first = jnp.max(jnp.where(lane == HEAD_DIM + FIRST_BLOCK_LANE, qa.astype(F32), 0.0)).astype(jnp.int32)
    return jnp.clip(first, 0, i) // 2


def _fox_first_blocks(qkv, c):
    lp = qkv.shape[0]
    nq = lp // BLK

    def body(x_ref, o_ref):
        xv = x_ref[...].astype(F32)
        sq = (xv * xv).astype(BF16)
        col = lax.broadcasted_iota(jnp.int32, (2 * WIDTH, LANES), 0)
        lane = lax.broadcasted_iota(jnp.int32, (2 * WIDTH, LANES), 1)
        pick = (col // HEAD_DIM == lane).astype(BF16)
        o_ref[...] = _dot(sq, pick, NN)

    norms = pl.pallas_call(
        body, name="fox_norms",
        out_shape=jax.ShapeDtypeStruct((lp, LANES), F32),
        grid=(nq,),
        in_specs=[pl.BlockSpec((BLK, 2 * WIDTH), lambda i: (i, 0))],
        out_specs=pl.BlockSpec((BLK, LANES), lambda i: (i, 0)),
        compiler_params=_params(("parallel",)),
    )(qkv)
    a_max = 1.02 * jnp.sqrt(norms[:, :N_HEADS].reshape(nq, BLK, N_HEADS).max(axis=1))
    b_max = 1.02 * jnp.sqrt(norms[:, N_HEADS:2 * N_HEADS].reshape(nq, BLK, N_HEADS).max(axis=1))
    c_max = c.reshape(nq, BLK, N_HEADS).max(axis=1)
    c_min = c.reshape(nq, BLK, N_HEADS).min(axis=1)
    bound = (a_max[:, None] * (b_max[None, :] + b_max[:, None]) * ATT_SCALE + c_max[:, None] - c_min[None, :])
    alive = (bound > EXP_IS_ZERO) | jnp.isnan(bound)
    alive = alive.reshape(nq, nq, N_PAIRS, 2).any(axis=-1)
    first = jnp.argmax(alive, axis=1).astype(F32)
    return jnp.repeat(jnp.repeat(first, 2, axis=1), BLK, axis=0)


def _head_dots(a, b):
    lp = a.shape[0]

    def body(a_ref, b_ref, o_ref):
        prod = a_ref[...] * b_ref[...]
        col = lax.broadcasted_iota(jnp.int32, (WIDTH, LANES), 0)
        lane = lax.broadcasted_iota(jnp.int32, (WIDTH, LANES), 1)
        pick = (col // HEAD_DIM == lane).astype(BF16)
        hi = prod.astype(BF16)
        mid, lo = _split_bf16(prod - hi.astype(F32))
        o_ref[...] = _dot(hi, pick, NN) + _dot(mid, pick, NN) + _dot(lo, pick, NN)

    row = pl.BlockSpec((BLK, WIDTH), lambda i: (i, 0))
    return pl.pallas_call(
        body, name="head_dots",
        out_shape=jax.ShapeDtypeStruct((lp, LANES), F32),
        grid=(lp // BLK,),
        in_specs=[row, row],
        out_specs=pl.BlockSpec((BLK, LANES), lambda i: (i, 0)),
        compiler_params=_params(("parallel",)),
    )(a, b)


def _fox_bwd(qkv, qaug, kaug, doaug, d_o, exchange=None):
    lp = qkv.shape[0]
    nq = lp // BLK
    split, first_step, last_step = _riding(exchange, 7, 4, 2, (N_PAIRS, nq))

    def body(*refs):
        (q_ref, k_ref, v_ref, qa_ref, ka_ref, da_ref, do_ref,
         dq_ref, dk_ref, dv_ref, dc_ref, acc_ref, rs_ref), ex_refs = split(refs)
        first_step(ex_refs)
        i = pl.program_id(1)
        hs = range(2)
        masks = _head_masks()
        qs = q_ref[...] * ATT_SCALE
        qa = qa_ref[...]
        qm = [jnp.where(masks[hh], qs, 0) for hh in hs]
        qh = [jnp.where(masks[hh], qs, qa) for hh in hs]
        dov = do_ref[...].astype(BF16)
        doa = da_ref[...]
        dom = [jnp.where(masks[hh], dov, 0) for hh in hs]
        doh = [jnp.where(masks[hh], dov, doa) for hh in hs]
        row = lax.broadcasted_iota(jnp.int32, (BLK, BLK), 0)
        col = lax.broadcasted_iota(jnp.int32, (BLK, BLK), 1)
        causal = col <= row

        @pl.when(i == 0)
        def _():
            dk_ref[...] = jnp.zeros_like(dk_ref)
            dv_ref[...] = jnp.zeros_like(dv_ref)
            dc_ref[...] = jnp.zeros_like(dc_ref)

        acc_ref[...] = jnp.zeros_like(acc_ref)
        rs_ref[...] = jnp.zeros_like(rs_ref)

        def lead(j, nb):
            rows = _kv_rows(j, nb)
            kmasks = _head_masks(nb * BLK)
            k, ka, v = k_ref[rows, :], ka_ref[rows, :], v_ref[rows, :]
            ones = (lax.broadcasted_iota(jnp.int32, v.shape, 1) % HEAD_DIM < N_AUG).astype(BF16)
            logp = [_dot(qh[hh], jnp.where(kmasks[hh], k, ka), NT) for hh in hs]
            dp = [_dot(doh[hh], jnp.where(kmasks[hh], v, ones), NT) for hh in hs]
            return logp + dp

        def follow(lead_out, j, nb, diag):
            rows = _kv_rows(j, nb)
            k = k_ref[rows, :]
            logp, dp = lead_out[:2], lead_out[2:]
            p = [jnp.exp(logp[hh]) for hh in hs]
            if diag:
                p = [jnp.where(causal, p[hh], 0.0) for hh in hs]
            ds = [p[hh] * dp[hh] for hh in hs]
            dsb = [ds[hh].astype(BF16) for hh in hs]
            for hh in hs:
                acc_ref[hh] += _dot(dsb[hh], k, NN)
                col_sums = jnp.sum(ds[hh], axis=0, keepdims=True)
                for b in range(nb):
                    dc_ref[0, j + b, hh:hh + 1, :] -= col_sums[:, b * BLK:(b + 1) * BLK]
                rs_ref[hh] += jnp.sum(ds[hh], axis=-1, keepdims=True)
            dk_ref[rows, :] += _dot(dsb[0], qm[0], TN) + _dot(dsb[1], qm[1], TN)
            dv_ref[rows, :] += _dot(p[0].astype(BF16), dom[0], TN) + _dot(p[1].astype(BF16), dom[1], TN)

        _walk_kv(i, lambda j, nb, diag: follow(lead(j, nb), j, nb, diag), first_pair=_first_pair(qa, i))
        dq_ref[...] = (jnp.where(masks[0], acc_ref[0], acc_ref[1]) * ATT_SCALE).astype(BF16)
        row_sums = jnp.where(masks[0], rs_ref[0], rs_ref[1]).T
        dc_ref[0, i, 0:1, :] += row_sums[0:1, :]
        dc_ref[0, i, 1:2, :] += row_sums[HEAD_DIM:HEAD_DIM + 1, :]
        last_step(ex_refs)

    q_spec, k_spec, v_spec = _att_specs(0, lp)
    blk = pl.BlockSpec((BLK, LANES), lambda p, i: (i, p))
    col_full = pl.BlockSpec((lp, LANES), lambda p, i: (0, p))
    crow_spec = pl.BlockSpec((1, nq, 2, BLK), lambda p, i: (p, 0, 0, 0))
    ex = exchange
    return pl.pallas_call(
        body, name="fox_bwd",
        out_shape=(jax.ShapeDtypeStruct((lp, WIDTH), BF16), jax.ShapeDtypeStruct((lp, WIDTH), F32),
                   jax.ShapeDtypeStruct((lp, WIDTH), F32), jax.ShapeDtypeStruct((N_PAIRS, nq, 2, BLK), F32))
        + tuple(ex.out_shapes if ex else ()),
        grid=(N_PAIRS, nq),
        in_specs=[q_spec, k_spec, v_spec, blk, col_full, blk, blk] + (ex.any_specs if ex else []),
        out_specs=(blk, col_full, col_full, crow_spec) + tuple(ex.any_specs if ex else ()),
        scratch_shapes=[pltpu.VMEM((2, BLK, LANES), F32), pltpu.VMEM((2, BLK, 1), F32)] + (ex.scratch if ex else []),
        compiler_params=_params(("arbitrary", "arbitrary") if ex else ("parallel", "arbitrary")),
    )(qkv, qkv, qkv, qaug, kaug, doaug, d_o, *(ex.arrays if ex else []))


def _sb_scores(z):
    ell = jnp.minimum(z, 0.0) - jnp.log(1.0 + jnp.exp(-jnp.abs(z)))
    return ell, ell - z


def _stacked(tri):
    return jnp.concatenate([tri, tri], axis=0)


def _cumsum_dot(x, tri2):
    hi, lo = _split_bf16(x)
    return _dot(jnp.concatenate([hi, lo], axis=1), tri2, NN)


def _sb_units(qh, k, strict2, causal, nb, diag):
    hs, bs = range(2), range(nb)
    z = [_dot(qh[hh], k, NT) for hh in hs]
    sc = [[_sb_scores(z[hh][:, b * BLK:(b + 1) * BLK]) for b in bs] for hh in hs]
    ell = [[sc[hh][b][0] for b in bs] for hh in hs]
    kap = [[jnp.where(causal, sc[hh][b][1], 0.0) if diag else sc[hh][b][1] for b in bs] for hh in hs]
    later = [[_cumsum_dot(kap[hh][b], strict2) for b in bs] for hh in hs]
    return ell, kap, later


def _row_sum(x):
    return jnp.sum(x, axis=-1, keepdims=True)


def _join(blocks):
    joined = blocks[0] if len(blocks) == 1 else jnp.concatenate(blocks, axis=1)
    return joined.astype(BF16)


def _sb_fwd(qkv):
    lp = qkv.shape[0]
    nq = lp // BLK
    assert nq <= HEAD_DIM

    def body(q_ref, k_ref, v_ref, o_ref, lc_ref, acc_ref, car_ref):
        i = pl.program_id(1)
        masks = _head_masks()
        qs = q_ref[...] * ATT_SCALE
        qh = [jnp.where(mk, qs, 0).astype(BF16) for mk in masks]
        row = lax.broadcasted_iota(jnp.int32, (BLK, BLK), 0)
        col = lax.broadcasted_iota(jnp.int32, (BLK, BLK), 1)
        lane = lax.broadcasted_iota(jnp.int32, (BLK, LANES), 1)
        causal = col < row
        strict2 = _stacked((row > col).astype(BF16))
        acc_ref[...] = jnp.zeros_like(acc_ref)
        car_ref[...] = jnp.zeros_like(car_ref)
        lc_ref[...] = jnp.full_like(lc_ref, NOT_VISITED)

        def tile(j, nb, diag):
            hs, bs = range(2), range(nb)
            rows = _kv_rows(j, nb)
            k, v = k_ref[rows, :], v_ref[rows, :]
            ell, kap, later = _sb_units(qh, k, strict2, causal, nb, diag)
            car = [[None] * nb for _ in hs]
            for hh in hs:
                run = car_ref[hh]
                for b in reversed(bs):
                    car[hh][b] = run
                    run = run + _row_sum(kap[hh][b])
                car_ref[hh] = run
            if not diag:
                kept = lc_ref[...]
                for hh in hs:
                    for b in bs:
                        kept = jnp.where(lane == j + b + HEAD_DIM * hh, car[hh][b], kept)
                lc_ref[...] = kept
            a = [[jnp.exp(ell[hh][b] + later[hh][b] + car[hh][b]) for b in bs] for hh in hs]
            if diag:
                a = [[jnp.where(causal, a[hh][b], 0.0) for b in bs] for hh in hs]
            for hh in hs:
                acc_ref[hh] += _dot(_join(a[hh]), v, NN)

        _walk_kv(i, tile, reverse=True, more=lambda: jnp.max(car_ref[...]) > EXP_IS_ZERO)
        o_ref[...] = jnp.where(masks[0], acc_ref[0], acc_ref[1])

    q_spec, k_spec, v_spec = _att_specs(3 * N_PAIRS, lp)
    blk = pl.BlockSpec((BLK, LANES), lambda p, i: (i, p))
    out = jax.ShapeDtypeStruct((lp, WIDTH), F32)
    return pl.pallas_call(
        body, name="sb_fwd",
        out_shape=(out, out),
        grid=(N_PAIRS, nq),
        in_specs=[q_spec, k_spec, v_spec],
        out_specs=(blk, blk),
        scratch_shapes=[pltpu.VMEM((2, BLK, LANES), F32), pltpu.VMEM((2, BLK, 1), F32)],
        compiler_params=_params(("parallel", "parallel")),
    )(qkv, qkv, qkv)


def _sb_bwd(qkv, lcar, d_o):
    lp = qkv.shape[0]
    nq = lp // BLK

    def body(q_ref, k_ref, v_ref, lc_ref, do_ref, dq_ref, dk_ref, dv_ref, acc_ref, cg_ref):
        i = pl.program_id(1)
        masks = _head_masks()
        qs = q_ref[...] * ATT_SCALE
        qh = [jnp.where(mk, qs, 0).astype(BF16) for mk in masks]
        dov = do_ref[...]
        doh = [jnp.where(mk, dov, 0.0).astype(BF16) for mk in masks]
        lcv = lc_ref[...]
        lane_row = lax.broadcasted_iota(jnp.int32, (1, LANES), 1)
        alive = jnp.where(jnp.max(lcv, axis=0, keepdims=True) > EXP_IS_ZERO, 1.0, 0.0)
        n_alive = jnp.maximum(jnp.sum(jnp.where(lane_row < HEAD_DIM, alive, 0.0)),
                              jnp.sum(jnp.where(lane_row >= HEAD_DIM, alive, 0.0))).astype(jnp.int32)
        first_pair = jnp.maximum(i - n_alive, 0) // 2
        row = lax.broadcasted_iota(jnp.int32, (BLK, BLK), 0)
        col = lax.broadcasted_iota(jnp.int32, (BLK, BLK), 1)
        lane = lax.broadcasted_iota(jnp.int32, (BLK, LANES), 1)
        causal = col < row
        strict2 = _stacked((row > col).astype(BF16))
        before2 = _stacked((row < col).astype(BF16))

        @pl.when(i == 0)
        def _():
            dk_ref[...] = jnp.zeros_like(dk_ref)
            dv_ref[...] = jnp.zeros_like(dv_ref)

        acc_ref[...] = jnp.zeros_like(acc_ref)
        cg_ref[...] = jnp.zeros_like(cg_ref)

        def tile(j, nb, diag):
            hs, bs = range(2), range(nb)
            rows = _kv_rows(j, nb)
            k, v = k_ref[rows, :], v_ref[rows, :]
            if diag:
                car = [[0.0] for _ in hs]
            else:
                car = [[_row_sum(jnp.where(lane == j + b + HEAD_DIM * hh, lcv, 0.0)) for b in bs] for hh in hs]
            da = [_dot(doh[hh], v, NT) for hh in hs]
            ell, _, later = _sb_units(qh, k, strict2, causal, nb, diag)
            a = [[jnp.exp(ell[hh][b] + later[hh][b] + car[hh][b]) for b in bs] for hh in hs]
            if diag:
                a = [[jnp.where(causal, a[hh][b], 0.0) for b in bs] for hh in hs]
            g = [[da[hh][:, b * BLK:(b + 1) * BLK] * a[hh][b] for b in bs] for hh in hs]
            cg = [[_cumsum_dot(g[hh][b], before2) for b in bs] for hh in hs]
            before = [[None] * nb for _ in hs]
            for hh in hs:
                run = cg_ref[hh]
                for b in bs:
                    before[hh][b] = run
                    run = run + _row_sum(g[hh][b])
                cg_ref[hh] = run
            dz = [[g[hh][b] - jnp.exp(ell[hh][b]) * (g[hh][b] + cg[hh][b] + before[hh][b]) for b in bs] for hh in hs]
            if diag:
                dz = [[jnp.where(causal, dz[hh][b], 0.0) for b in bs] for hh in hs]
            dzb = [_join(dz[hh]) for hh in hs]
            ab = [_join(a[hh]) for hh in hs]
            for hh in hs:
                acc_ref[hh] += _dot(dzb[hh], k, NN)
            dk_ref[rows, :] += _dot(dzb[0], qh[0], TN) + _dot(dzb[1], qh[1], TN)
            dv_ref[rows, :] += _dot(ab[0], doh[0], TN) + _dot(ab[1], doh[1], TN)

        _walk_kv(i, tile, first_pair=first_pair)
        dq_ref[...] = (jnp.where(masks[0], acc_ref[0], acc_ref[1]) * ATT_SCALE).astype(BF16)

    q_spec, k_spec, v_spec = _att_specs(3 * N_PAIRS, lp)
    blk = pl.BlockSpec((BLK, LANES), lambda p, i: (i, p))
    col_full = pl.BlockSpec((lp, LANES), lambda p, i: (0, p))
    return pl.pallas_call(
        body, name="sb_bwd",
        out_shape=(jax.ShapeDtypeStruct((lp, WIDTH), BF16), jax.ShapeDtypeStruct((lp, WIDTH), F32),
                   jax.ShapeDtypeStruct((lp, WIDTH), F32)),
        grid=(N_PAIRS, nq),
        in_specs=[q_spec, k_spec, v_spec, blk, blk],
        out_specs=(blk, col_full, col_full),
        scratch_shapes=[pltpu.VMEM((2, BLK, LANES), F32), pltpu.VMEM((2, BLK, 1), F32)],
        compiler_params=_params(("parallel", "arbitrary")),
    )(qkv, qkv, qkv, lcar, d_o)


def _local_step(x, target, meta, gains, w_in, b_forget, w_fox, w_sb, w_out, w_up, conv_w, conv_b, w_down,
                ffn_block=None, late_weights=None, early_grads=None, w_in_grad=None):
    s, d = x.shape
    n_valid = N_META + s
    lp = -(-n_valid // BLK) * BLK
    pad = lp - n_valid
    nq = lp // BLK


    q_a, k_a, v_a, f_a, q_b, k_b, v_b, g_a, g_b = jnp.split(
        w_in, [512, 1024, 1536, 1544, 2056, 2568, 3080, 4104], axis=1)
    w_qkv = jnp.concatenate([q_a, k_a, v_a, q_b, k_b, v_b], axis=1)
    w_gf = jnp.concatenate([g_a, g_b, f_a, jnp.zeros((d, F_PAD - N_HEADS), BF16)], axis=1)
    b_pad = jnp.concatenate([b_forget.reshape(1, N_HEADS), jnp.zeros((1, LANES - N_HEADS), F32)], axis=1)
    g0, g1, g2, g3 = (gains[i:i + 1] for i in range(4))

    h0, xn1 = _first_norm_fwd(x, meta, g0, lp)
    qkv = _mm(xn1, w_qkv, "nn", BF16, "proj_qkv")
    gf = _mm(xn1, w_gf, "nn", F32, "proj_gates")
    fpre = gf[:, 2 * d:2 * d + LANES]
    c = _forget_fwd(fpre, b_pad)[:, :N_HEADS]
    c_pieces = _bf16_pieces(c)
    one = jnp.ones((lp, N_HEADS), BF16)
    kaug = _aug_lanes(3 * [one] + [-x for x in c_pieces] + 3 * [one])
    first_block = _fox_first_blocks(qkv, c)
    o_a, lse, *gathered = _fox_fwd(qkv, _aug_lanes(c_pieces + 3 * [one] + 3 * [0 * one] + [first_block]), kaug,
                                   late_weights[0] if late_weights else None)
    if late_weights:
        w_fox, w_sb, w_out, w_up, conv_w, w_down = late_weights[1](gathered)
    ffn_block = ffn_block or w_up.shape[1] // 2
    o_b, lcar = _sb_fwd(qkv)
    h1, ya, yb, gated, mixed, xn3 = _mix_fwd(o_a, o_b, gf, h0, w_fox, w_sb, w_out, g1, g2)
    up = _mm(xn3, w_up, "nn", F32, "ffn_up")
    act = _conv_gelu_fwd(up, conv_w, conv_b, ffn_block)
    ffn = _mm(act, w_down, "nn", F32, "ffn_down")
    dy, loss_acc, d_ffn, dg3 = _out_loss(h1, ffn, g3, target)
    loss = loss_acc[0, 0]

    d_act = _mm(d_ffn, w_down, "nt", F32, "ffn_down_dx")
    gw_down = _mm(act, d_ffn, "tn", BF16, "ffn_down_dw")
    d_up, g_conv_w, g_conv_b = _conv_gelu_bwd(up, d_act, conv_w, conv_b, ffn_block)
    d_xn3 = _mm(d_up, w_up, "nt", F32, "ffn_up_dx")
    gw_up = _mm(xn3, d_up, "tn", BF16, "ffn_up_dw")
    dh1, dg2 = _rmsnorm_bwd(h1, g2, d_xn3, dy, F32, "norm3_bwd")

    d_mixed, dg1 = _rmsnorm_bwd(mixed, g1, dh1, None, BF16, "norm2_bwd")
    d_gated = _mm(d_mixed, w_out, "nt", F32, "out_dx")
    gw_out = _mm(gated, d_mixed, "tn", BF16, "out_dw")
    d_ya, d_yb, d_ga, d_gb = _gate_bwd(d_gated, gf, ya, yb)
    d_oa = _mm(d_ya, w_fox, "nt", F32, "fox_o_dx")
    gw_fox = _mm(o_a, d_ya, "tn", BF16, "fox_o_dw")
    d_ob = _mm(d_yb, w_sb, "nt", F32, "sb_o_dx")
    gw_sb = _mm(o_b, d_yb, "tn", BF16, "sb_o_dw")
    neg_lse = [-x for x in _bf16_pieces(lse.transpose(0, 2, 1, 3).reshape(N_HEADS, lp).T)]
    neg_dsum = [-x for x in _bf16_pieces(_head_dots(d_oa, o_a)[:, :N_HEADS])]
    qaug = _aug_lanes(c_pieces + 3 * [one] + neg_lse + [first_block])
    early = {"w_o_fox": gw_fox, "w_o_sb": gw_sb, "w_out": gw_out, "w_up": gw_up, "conv_w": g_conv_w,
             "conv_b": g_conv_b, "w_down": gw_down}
    dq_a, dk_a, dv_a, dcrow, *early_parts = _fox_bwd(
        qkv, qaug, kaug, _aug_lanes(neg_dsum), d_oa, early_grads(early) if early_grads else None)
    dq_b, dk_b, dv_b = _sb_bwd(qkv, lcar, d_ob)
    dc = dcrow.transpose(0, 2, 1, 3).reshape(N_HEADS, lp).T
    dc = jnp.concatenate([dc, jnp.zeros((lp, LANES - N_HEADS), F32)], axis=1)
    df, db = _forget_bwd(dc, fpre, b_pad)
    lane = jnp.arange(LANES) < N_HEADS
    df = jnp.where(lane[None, :], df, 0.0)
    d_proj = jnp.concatenate(
        [dq_a, dk_a.astype(BF16), dv_a.astype(BF16), dq_b, dk_b.astype(BF16), dv_b.astype(BF16),
         d_ga, d_gb, df.astype(BF16), jnp.zeros((lp, F_PAD - LANES), BF16)], axis=1)
    w_in_p = jnp.concatenate([w_qkv, w_gf], axis=1)
    gw_in_p = _mm(xn1, d_proj, "tn", BF16, "proj_dw")
    qkv_parts = jnp.split(gw_in_p[:, :6 * WIDTH], 6, axis=1)
    gw_in = jnp.concatenate(
        qkv_parts[:3] + [gw_in_p[:, 6 * WIDTH + 2 * d:6 * WIDTH + 2 * d + N_HEADS]] + qkv_parts[3:]
        + [gw_in_p[:, 6 * WIDTH:6 * WIDTH + 2 * d]], axis=1)
    if w_in_grad:
        d_xn1, *w_in_parts = _mm(d_proj, w_in_p, "nt", F32, "proj_dx", w_in_grad(gw_in))
    else:
        d_xn1, w_in_parts = _mm(d_proj, w_in_p, "nt", F32, "proj_dx"), []
    grad_x, grad_meta, dg0 = _first_norm_bwd(h0, g0, d_xn1, dh1, s)
    grads = {
        "w_in_parts": w_in_parts,
        "meta_tokens": grad_meta,
        "norm_gains": jnp.concatenate([dg0, dg1, dg2, dg3], axis=0),
        "w_in": gw_in,
        "b_forget": db[:, :N_HEADS],
        "early_parts": early_parts,
        **early,
    }
    return loss, grad_x, grads


MESH_IDS = pl.DeviceIdType.MESH


def _window(ref, kind, idx, rows, cols):
    if kind == "slots":
        return ref.at[idx]
    if kind == "gate_value":
        half = N_DEV // 2
        idx = jnp.where(idx < half, 2 * idx, 2 * (idx - half) + 1)
        kind = "cols"
    if kind == "cols":
        return ref.at[:, pl.ds(pl.multiple_of(idx * cols, cols & -cols), cols)]
    return ref.at[pl.ds(pl.multiple_of(idx * rows, rows & -rows), rows), :]


def _gathered_shape(shape, kind):
    rows, cols = shape
    return {"slots": (N_DEV, rows, cols), "cols": (rows, N_DEV * cols), "gate_value": (rows, N_DEV * cols),
            "rows": (N_DEV * rows, cols)}[kind]


def _all_gather(shards, kinds):
    n = len(shards)

    def body(*refs):
        ins, outs = refs[:n], refs[n:2 * n]
        send_sems, recv_sems, local_sems = refs[2 * n:]
        x, y, c = lax.axis_index("x"), lax.axis_index("y"), lax.axis_index("c")
        me, sibling = (x, y, c), (x, y, 1 - c)
        chips = [(1 - x, y), (x, 1 - y), (1 - x, 1 - y)]

        def part(t, px, py, pc):
            return _window(outs[t], kinds[t], 4 * px + 2 * py + pc, *shards[t].shape)

        def copy(k, t, blk, to, src=None):
            return pltpu.make_async_remote_copy(
                src_ref=part(t, *blk) if src is None else src, dst_ref=part(t, *blk),
                send_sem=send_sems.at[k, t], recv_sem=recv_sems.at[k, t],
                device_id=to, device_id_type=MESH_IDS)

        mine = [pltpu.make_async_copy(ins[t], part(t, *me), local_sems.at[t]) for t in range(n)]
        for cp in mine:
            cp.start()
        first = [copy(0, t, me, sibling, src=ins[t]) for t in range(n)]
        first += [copy(1 + j, t, me, (*chip, c), src=ins[t]) for j, chip in enumerate(chips) for t in range(n)]
        for cp in first:
            cp.start()
        passed = []
        for j, chip in enumerate(chips):
            for t in range(n):
                copy(1 + j, t, (*chip, c), me).wait_recv()
                passed.append(copy(4 + j, t, (*chip, c), sibling))
                passed[-1].start()
        for t in range(n):
            copy(0, t, sibling, me).wait_recv()
        for j, chip in enumerate(chips):
            for t in range(n):
                copy(4 + j, t, (*chip, 1 - c), me).wait_recv()
        for cp in first + passed:
            cp.wait_send()
        for cp in mine:
            cp.wait()

    any_space = pl.BlockSpec(memory_space=pl.ANY)
    return pl.pallas_call(
        body, name="all_gather",
        out_shape=tuple(jax.ShapeDtypeStruct(_gathered_shape(a.shape, k), a.dtype) for a, k in zip(shards, kinds)),
        in_specs=[any_space] * n,
        out_specs=tuple([any_space] * n),
        scratch_shapes=[pltpu.SemaphoreType.DMA((7, n)), pltpu.SemaphoreType.DMA((7, n)),
                        pltpu.SemaphoreType.DMA((n,))],
    )(*shards)


class _Exchange:
    def __init__(self, arrays, kinds, shard_shapes, gather):
        self.arrays, self.kinds, self.shard_shapes, self.gather = list(arrays), list(kinds), list(shard_shapes), gather
        self.n = n = len(self.arrays)
        self.any_specs = [pl.BlockSpec(memory_space=pl.ANY)] * n
        if gather:
            shapes = [_gathered_shape(a.shape, k) for a, k in zip(self.arrays, kinds)]
        else:
            shapes = [(N_DEV,) + tuple(s) for s in shard_shapes]
        self.out_shapes = [jax.ShapeDtypeStruct(s, a.dtype) for s, a in zip(shapes, self.arrays)]
        self.scratch = [pltpu.SemaphoreType.DMA((N_DEV - 1, n)), pltpu.SemaphoreType.DMA((N_DEV - 1, n)),
                        pltpu.SemaphoreType.DMA((n,))]

    def copies(self, ins, outs, sems):
        send_sems, recv_sems, local_sems = sems
        x, y, c = lax.axis_index("x"), lax.axis_index("y"), lax.axis_index("c")
        my = 4 * x + 2 * y + c

        def src(t, receiver):
            if self.gather or self.kinds[t] == "all":
                return ins[t]
            return _window(ins[t], self.kinds[t], receiver, *self.shard_shapes[t])

        def dst(t, sender):
            if self.gather:
                return _window(outs[t], self.kinds[t], sender, *self.shard_shapes[t])
            return outs[t].at[sender]

        local = [pltpu.make_async_copy(src(t, my), dst(t, my), local_sems.at[t]) for t in range(self.n)]
        sends, arrivals = [], []
        for rel in range(1, N_DEV):
            px, py, pc = x ^ (rel >> 2), y ^ ((rel >> 1) & 1), c ^ (rel & 1)
            peer = 4 * px + 2 * py + pc
            for t in range(self.n):
                common = dict(send_sem=send_sems.at[rel - 1, t], recv_sem=recv_sems.at[rel - 1, t],
                              device_id=(px, py, pc), device_id_type=MESH_IDS)
                sends.append(pltpu.make_async_remote_copy(src_ref=src(t, peer), dst_ref=dst(t, my), **common))
                arrivals.append(pltpu.make_async_remote_copy(src_ref=src(t, my), dst_ref=dst(t, peer), **common))
        return local, sends, arrivals

    def start(self, ins, outs, sems):
        local, sends, _ = self.copies(ins, outs, sems)
        for cp in local + sends:
            cp.start()

    def wait(self, ins, outs, sems):
        local, sends, arrivals = self.copies(ins, outs, sems)
        for cp in arrivals:
            cp.wait_recv()
        for cp in sends:
            cp.wait_send()
        for cp in local:
            cp.wait()


def _exchange(grads, kinds, shard_shapes):
    ex = _Exchange(grads, kinds, shard_shapes, gather=False)
    n = ex.n

    def body(*refs):
        ins, outs, sems = refs[:n], refs[n:2 * n], refs[2 * n:]
        ex.start(ins, outs, sems)
        ex.wait(ins, outs, sems)

    return pl.pallas_call(
        body, name="grad_exchange",
        out_shape=tuple(ex.out_shapes),
        in_specs=ex.any_specs,
        out_specs=tuple(ex.any_specs),
        scratch_shapes=ex.scratch,
    )(*grads)


def _sum_adamw(parts, w, m, v, name):
    rows, cols = w.shape
    n, rows_p, cols_p = parts.shape
    tr = _tile(rows, BLK, SUBLANES) if rows > BLK else rows
    tp = tr if rows_p == rows else rows_p
    c1 = 1.0 - ADAM_B1 ** ADAM_STEP
    c2 = 1.0 - ADAM_B2 ** ADAM_STEP

    def body(p_ref, w_ref, m_ref, v_ref, g_ref, d_ref, nm_ref, nv_ref):
        gv = p_ref[0, 0:tr, 0:cols].astype(F32)
        for s in range(1, n):
            gv = gv + p_ref[s, 0:tr, 0:cols].astype(F32)
        g_ref[...] = gv
        nm = ADAM_B1 * m_ref[...] + (1.0 - ADAM_B1) * gv
        nv = ADAM_B2 * v_ref[...] + (1.0 - ADAM_B2) * (gv * gv)
        m_hat = nm / c1
        v_hat = nv / c2
        d_ref[...] = -ADAM_LR * (m_hat / (jnp.sqrt(v_hat) + ADAM_EPS) + ADAM_WD * w_ref[...])
        nm_ref[...] = nm
        nv_ref[...] = nv

    spec = pl.BlockSpec((tr, cols), lambda i: (i, 0))
    out = jax.ShapeDtypeStruct((rows, cols), F32)
    return pl.pallas_call(
        body, name=name,
        out_shape=(out, out, out, out),
        grid=(rows // tr,),
        in_specs=[pl.BlockSpec((n, tp, cols_p), lambda i: (0, i, 0)), spec, spec, spec],
        out_specs=(spec, spec, spec, spec),
        compiler_params=_params(("parallel",)),
    )(parts, w, m, v)


def _pad2(a, rows, cols):
    return jnp.pad(a, ((0, rows - a.shape[0]), (0, cols - a.shape[1])))


WEIGHTS = ["meta_tokens", "norm_gains", "w_in", "b_forget", "w_o_fox", "w_o_sb", "w_out", "w_up", "conv_w",
           "conv_b", "w_down"]


def kernel(x, meta_tokens, norm_gains, w_in, b_forget, w_o_fox, w_o_sb, w_out, w_up, conv_w, conv_b, w_down, loss_target, m_meta_tokens, m_norm_gains, m_w_in, m_b_forget, m_w_o_fox, m_w_o_sb, m_w_out, m_w_up, m_conv_w, m_conv_b, m_w_down, v_meta_tokens, v_norm_gains, v_w_in, v_b_forget, v_w_o_fox, v_w_o_sb, v_w_out, v_w_up, v_conv_w, v_conv_b, v_w_down):
    w = dict(meta_tokens=meta_tokens, norm_gains=norm_gains, w_in=w_in, b_forget=b_forget, w_o_fox=w_o_fox,
             w_o_sb=w_o_sb, w_out=w_out, w_up=w_up, conv_w=conv_w, conv_b=conv_b, w_down=w_down)
    mom = dict(meta_tokens=m_meta_tokens, norm_gains=m_norm_gains, w_in=m_w_in, b_forget=m_b_forget,
               w_o_fox=m_w_o_fox, w_o_sb=m_w_o_sb, w_out=m_w_out, w_up=m_w_up, conv_w=m_conv_w, conv_b=m_conv_b,
               w_down=m_w_down)
    vel = dict(meta_tokens=v_meta_tokens, norm_gains=v_norm_gains, w_in=v_w_in, b_forget=v_b_forget,
               w_o_fox=v_w_o_fox, w_o_sb=v_w_o_sb, w_out=v_w_out, w_up=v_w_up, conv_w=v_conv_w, conv_b=v_conv_b,
               w_down=v_w_down)
    w2 = {n: a.reshape(a.shape[-2:]) for n, a in w.items()}
    shard_shape = {n: a.shape for n, a in w2.items()}

    d = x.shape[-1]
    up_cols = shard_shape["w_up"][1]
    up_pad = -(-up_cols // LANES) * LANES
    half = N_DEV // 2
    pad_rows = lambda a: _pad2(a, SUBLANES, a.shape[1])

    shards = [
        ("w_in", "slots", w2["w_in"].astype(BF16)),
        ("meta_tokens", "cols", w2["meta_tokens"]),
        ("norm_gains", "cols", pad_rows(w2["norm_gains"])),
    ]
    later = [
        ("w_o_fox", "cols", w2["w_o_fox"].astype(BF16)),
        ("w_o_sb", "cols", w2["w_o_sb"].astype(BF16)),
        ("w_out", "rows", w2["w_out"].astype(BF16)),
        ("w_up", "gate_value", _pad2(w2["w_up"], d, up_pad).astype(BF16)),
        ("conv_w", "gate_value", _pad2(w2["conv_w"], SUBLANES, up_pad)),
        ("w_down", "rows", w2["w_down"].astype(BF16)),
    ]
    full = dict(zip([s[0] for s in shards], _all_gather([s[2] for s in shards], [s[1] for s in shards])))
    w_in_full = jnp.concatenate([full["w_in"][i] for i in range(N_DEV)], axis=1)
    conv_b_p = jnp.pad(w2["conv_b"].reshape(2, half, up_cols), ((0, 0), (0, 0), (0, up_pad - up_cols)))
    conv_b_p = conv_b_p.transpose(1, 0, 2)

    def finish_gather(gathered):
        w_fox, w_sb, w_out_full, w_up_p, conv_w_p, w_down_full = gathered
        w_down_p = jnp.pad(w_down_full.reshape(half, up_cols, d), ((0, 0), (0, up_pad - up_cols), (0, 0)))
        return w_fox, w_sb, w_out_full, w_up_p, conv_w_p[:3], w_down_p.reshape(half * up_pad, d)

    early_names = ["w_o_fox", "w_o_sb", "w_out", "w_up", "conv_w", "conv_b", "w_down"]

    def early_exchange(g):
        sends = {
            "w_o_fox": ("cols", g["w_o_fox"], shard_shape["w_o_fox"]),
            "w_o_sb": ("cols", g["w_o_sb"], shard_shape["w_o_sb"]),
            "w_out": ("rows", g["w_out"], shard_shape["w_out"]),
            "w_up": ("gate_value", g["w_up"], (d, up_pad)),
            "conv_w": ("gate_value", pad_rows(g["conv_w"]), (SUBLANES, up_pad)),
            "conv_b": ("all", pad_rows(g["conv_b"].reshape(half, 2, up_pad).transpose(1, 0, 2)[:, :, :up_cols]
                                       .reshape(1, -1)),
                       (SUBLANES, N_DEV * up_cols)),
            "w_down": ("rows", g["w_down"].reshape(half, up_pad, d)[:, :up_cols].reshape(half * up_cols, d),
                       shard_shape["w_down"]),
        }
        return _Exchange([sends[n][1] for n in early_names], [sends[n][0] for n in early_names],
                         [sends[n][2] for n in early_names], gather=False)

    def w_in_exchange(g):
        in_cols = shard_shape["w_in"][1]
        return _Exchange([jnp.stack([g[:, i * in_cols:(i + 1) * in_cols] for i in range(N_DEV)])], ["slots"],
                         [shard_shape["w_in"]], gather=False)

    late_weights = (_Exchange([s[2] for s in later], [s[1] for s in later], [s[2].shape for s in later], gather=True),
                    finish_gather)
    loss, grad_x, grads = _local_step(
        x[0], loss_target[0], full["meta_tokens"], full["norm_gains"][:4], w_in_full, w2["b_forget"],
        None, None, None, None, None, conv_b_p.reshape(1, N_DEV * up_pad), None, ffn_block=up_pad,
        late_weights=late_weights, early_grads=early_exchange, w_in_grad=w_in_exchange)
    loss = lax.psum(loss, ("x", "y", "c"))

    late_names = ["meta_tokens", "norm_gains", "b_forget"]
    sends = {
        "meta_tokens": ("cols", grads["meta_tokens"], (N_META, LANES)),
        "norm_gains": ("cols", pad_rows(grads["norm_gains"]), (SUBLANES, LANES)),
        "b_forget": ("all", _pad2(grads["b_forget"], SUBLANES, LANES), (SUBLANES, LANES)),
    }
    parts = dict(zip(late_names, _exchange([sends[n][1] for n in late_names], [sends[n][0] for n in late_names],
                                           [sends[n][2] for n in late_names])))
    parts.update(zip(early_names, grads["early_parts"]))
    parts["w_in"], = grads["w_in_parts"]

    grad, delta, new_m, new_v = {}, {}, {}, {}
    for n in WEIGHTS:
        shape = w[n].shape
        outs = _sum_adamw(parts[n], w2[n], mom[n].reshape(shard_shape[n]), vel[n].reshape(shard_shape[n]),
                          "adamw_" + n)
        grad[n], delta[n], new_m[n], new_v[n] = (o.reshape(shape) for o in outs)

    return (loss, grad_x[None], *[grad[n] for n in WEIGHTS], *[delta[n] for n in WEIGHTS],
            *[new_m[n] for n in WEIGHTS], *[new_v[n] for n in WEIGHTS])
```

```python
import functools
import math

import jax
import jax.numpy as jnp
from jax import lax
from jax.experimental import pallas as pl
from jax.experimental.pallas import tpu as pltpu

F32 = jnp.float32
BF16 = jnp.bfloat16

N_DEV = 8
N_META = 16
HEAD_DIM = 64
N_HEADS = 8
WIDTH = N_HEADS * HEAD_DIM
N_PAIRS = N_HEADS // 2
LANES = 128
SUBLANES = 8
EPS = 1e-6
ATT_SCALE = HEAD_DIM ** -0.5
BLK = 256
F_PAD = 256
VMEM_LIMIT = 48 << 20

ADAM_LR = 0.001
ADAM_B1 = 0.9
ADAM_B2 = 0.999
ADAM_EPS = 1e-08
ADAM_WD = 0.01
ADAM_STEP = 10

GELU_C = math.sqrt(2.0 / math.pi)
GELU_A = 0.044715
EXP_IS_ZERO = -110.0
NOT_VISITED = -1e30


def _params(sem, vmem=VMEM_LIMIT):
    return pltpu.CompilerParams(dimension_semantics=sem, vmem_limit_bytes=vmem)


def _tile(dim, cap, align=LANES):
    t = (min(cap, dim) // align) * align
    while t >= align:
        if dim % t == 0:
            return t
        t -= align
    return dim


def _log_sigmoid_parts(z):
    lp = jnp.log1p(jnp.exp(-jnp.abs(z)))
    return jnp.minimum(z, 0.0) - lp, jnp.minimum(-z, 0.0) - lp


def _sigmoid(x):
    return 1.0 / (1.0 + jnp.exp(-x))


def _split_bf16(x):
    hi = x.astype(BF16)
    lo = (x - hi.astype(F32)).astype(BF16)
    return hi, lo


def _dot(a, b, dims):
    return lax.dot_general(a, b, (dims, ((), ())), preferred_element_type=F32)


NN = ((1,), (0,))
NT = ((1,), (1,))
TN = ((0,), (0,))


def _mm(a, b, mode, out_dtype, name, exchange=None):
    if mode == "nn":
        (m, kc), (_, n) = a.shape, b.shape
    elif mode == "nt":
        (m, kc), (n, _) = a.shape, b.shape
    else:
        (kc, m), (_, n) = a.shape, b.shape
    if mode == "tn":
        tm, tn, tk = _tile(m, 1024), _tile(n, 1792), _tile(kc, 1408)
    else:
        tm, tn, tk = _tile(m, 1408), _tile(n, 1024), _tile(kc, 1536)
    nk = kc // tk
    dims = {"nn": NN, "nt": NT, "tn": TN}[mode]
    grid = (n // tn, m // tm, nk)
    split, first_step, last_step = _riding(exchange, 2, 1, 1 if nk > 1 else 0, grid)

    def body(*refs):
        (a_ref, b_ref, o_ref, *scratch), ex_refs = split(refs)
        first_step(ex_refs)
        k = pl.program_id(2)
        part = _dot(a_ref[...].astype(BF16), b_ref[...].astype(BF16), dims)
        if nk == 1:
            o_ref[...] = part.astype(out_dtype)
        else:
            acc_ref, = scratch

            @pl.when(k == 0)
            def _():
                acc_ref[...] = part

            @pl.when(k > 0)
            def _():
                acc_ref[...] += part

            @pl.when(k == nk - 1)
            def _():
                o_ref[...] = acc_ref[...].astype(out_dtype)
        last_step(ex_refs)

    if mode == "tn":
        a_spec = pl.BlockSpec((tk, tm), lambda j, i, k: (k, i))
    else:
        a_spec = pl.BlockSpec((tm, tk), lambda j, i, k: (i, k))
    if mode == "nt":
        b_spec = pl.BlockSpec((tn, tk), lambda j, i, k: (j, k))
    else:
        b_spec = pl.BlockSpec((tk, tn), lambda j, i, k: (k, j))
    ex = exchange
    out = pl.pallas_call(
        body, name=name,
        out_shape=(jax.ShapeDtypeStruct((m, n), out_dtype),) + tuple(ex.out_shapes if ex else ()),
        grid=grid,
        in_specs=[a_spec, b_spec] + (ex.any_specs if ex else []),
        out_specs=(pl.BlockSpec((tm, tn), lambda j, i, k: (i, j)),) + tuple(ex.any_specs if ex else ()),
        scratch_shapes=([pltpu.VMEM((tm, tn), F32)] if nk > 1 else []) + (ex.scratch if ex else []),
        compiler_params=_params(("arbitrary",) * 3 if ex else ("parallel", "parallel", "arbitrary")),
    )(a, b, *(ex.arrays if ex else []))
    return out if ex else out[0]


def _shifted_rows(cur_ref, prev_ref, first=None):
    head = prev_ref[...] if first is None else jnp.where(pl.program_id(0) == 0, first, prev_ref[...])
    return jnp.concatenate([head, cur_ref[0:BLK - N_META, :]], axis=0)


def _shifted_specs(s, d):
    per = BLK // N_META
    return (pl.BlockSpec((BLK, d), lambda i: (jnp.minimum(i, s // BLK - 1), 0)),
            pl.BlockSpec((N_META, d), lambda i: (jnp.maximum(per * i - 1, 0), 0)))


def _first_norm_fwd(x, meta, g, lp):
    s, d = x.shape
    n_valid = N_META + s

    def body(x_ref, p_ref, m_ref, g_ref, h_ref, o_ref):
        row = pl.program_id(0) * BLK + lax.broadcasted_iota(jnp.int32, (BLK, 1), 0)
        h = jnp.where(row < n_valid, _shifted_rows(x_ref, p_ref, m_ref[...]), 0.0)
        h_ref[...] = h
        r = lax.rsqrt(jnp.mean(h * h, axis=-1, keepdims=True) + EPS)
        o_ref[...] = ((h * r) * g_ref[...]).astype(BF16)

    row_d = pl.BlockSpec((BLK, d), lambda i: (i, 0))
    return pl.pallas_call(
        body, name="norm1_fwd",
        out_shape=(jax.ShapeDtypeStruct((lp, d), F32), jax.ShapeDtypeStruct((lp, d), BF16)),
        grid=(lp // BLK,),
        in_specs=[*_shifted_specs(s, d), pl.BlockSpec((N_META, d), lambda i: (0, 0)),
                  pl.BlockSpec((1, d), lambda i: (0, 0))],
        out_specs=(row_d, row_d),
        compiler_params=_params(("parallel",)),
    )(x, x, meta, g)


def _first_norm_bwd(h0, g, dy, resid, s):
    lp, d = h0.shape
    nb = lp // BLK
    assert nb == s // BLK + 1

    def body(x_ref, g_ref, dy_ref, r_ref, gx_ref, gm_ref, dg_ref, keep_ref):
        i = pl.program_id(0)
        xv = x_ref[...]
        dyv = dy_ref[...]
        r = lax.rsqrt(jnp.mean(xv * xv, axis=-1, keepdims=True) + EPS)
        xh = xv * r
        dyg = dyv * g_ref[...]
        dx = r * (dyg - xh * jnp.mean(dyg * xh, axis=-1, keepdims=True)) + r_ref[...]

        @pl.when(i == 0)
        def _():
            dg_ref[...] = jnp.zeros_like(dg_ref)
            keep_ref[...] = jnp.zeros_like(keep_ref)
            gm_ref[...] = dx[0:N_META, :]

        gx_ref[...] = jnp.concatenate([keep_ref[...], dx[0:N_META, :]], axis=0)
        keep_ref[...] = dx[N_META:BLK, :]
        dg_ref[...] += jnp.sum(dyv * xh, axis=0, keepdims=True)

    row = pl.BlockSpec((BLK, d), lambda i: (i, 0))
    vec = pl.BlockSpec((1, d), lambda i: (0, 0))
    return pl.pallas_call(
        body, name="norm1_bwd",
        out_shape=(jax.ShapeDtypeStruct((s, d), F32), jax.ShapeDtypeStruct((N_META, d), F32),
                   jax.ShapeDtypeStruct((1, d), F32)),
        grid=(nb,),
        in_specs=[row, vec, row, row],
        out_specs=(pl.BlockSpec((BLK, d), lambda i: (jnp.maximum(i - 1, 0), 0)),
                   pl.BlockSpec((N_META, d), lambda i: (0, 0)), vec),
        scratch_shapes=[pltpu.VMEM((BLK - N_META, d), F32)],
        compiler_params=_params(("arbitrary",)),
    )(h0, g, dy, resid)


def _rmsnorm_bwd(x, g, dy, resid, out_dtype, name):
    lp, d = x.shape
    has_resid = resid is not None

    def body(*refs):
        if has_resid:
            x_ref, g_ref, dy_ref, r_ref, dx_ref, dg_ref = refs
        else:
            x_ref, g_ref, dy_ref, dx_ref, dg_ref = refs
        i = pl.program_id(0)
        xv = x_ref[...]
        dyv = dy_ref[...].astype(F32)
        r = lax.rsqrt(jnp.mean(xv * xv, axis=-1, keepdims=True) + EPS)
        xh = xv * r
        dyg = dyv * g_ref[...]
        dx = r * (dyg - xh * jnp.mean(dyg * xh, axis=-1, keepdims=True))
        if has_resid:
            dx = dx + r_ref[...]
        dx_ref[...] = dx.astype(out_dtype)

        @pl.when(i == 0)
        def _():
            dg_ref[...] = jnp.zeros_like(dg_ref)

        dg_ref[...] += jnp.sum(dyv * xh, axis=0, keepdims=True)

    row = pl.BlockSpec((BLK, d), lambda i: (i, 0))
    vec = pl.BlockSpec((1, d), lambda i: (0, 0))
    ins = [x, g, dy] + ([resid] if has_resid else [])
    in_specs = [row, vec, row] + ([row] if has_resid else [])
    return pl.pallas_call(
        body, name=name,
        out_shape=(jax.ShapeDtypeStruct((lp, d), out_dtype), jax.ShapeDtypeStruct((1, d), F32)),
        grid=(lp // BLK,),
        in_specs=in_specs,
        out_specs=(row, vec),
        compiler_params=_params(("arbitrary",)),
    )(*ins)


def _forget_fwd(fpre, b_pad):
    lp = fpre.shape[0]

    def body(f_ref, b_ref, c_ref, carry_ref):
        i = pl.program_id(0)

        @pl.when(i == 0)
        def _():
            carry_ref[...] = jnp.zeros_like(carry_ref)

        logf, _ = _log_sigmoid_parts(f_ref[...] + b_ref[...])
        row = lax.broadcasted_iota(jnp.int32, (BLK, BLK), 0)
        col = lax.broadcasted_iota(jnp.int32, (BLK, BLK), 1)
        tri = (col <= row).astype(BF16)
        p0 = logf.astype(BF16)
        r1 = logf - p0.astype(F32)
        p1 = r1.astype(BF16)
        p2 = (r1 - p1.astype(F32)).astype(BF16)
        c = _dot(tri, p0, NN) + _dot(tri, p1, NN) + _dot(tri, p2, NN) + carry_ref[0:1, :]
        c_ref[...] = c
        carry_ref[...] = jnp.broadcast_to(c[BLK - 1:BLK, :], carry_ref.shape)

    return pl.pallas_call(
        body, name="forget_fwd",
        out_shape=jax.ShapeDtypeStruct((lp, LANES), F32),
        grid=(lp // BLK,),
        in_specs=[pl.BlockSpec((BLK, LANES), lambda i: (i, 0)), pl.BlockSpec((1, LANES), lambda i: (0, 0))],
        out_specs=pl.BlockSpec((BLK, LANES), lambda i: (i, 0)),
        scratch_shapes=[pltpu.VMEM((SUBLANES, LANES), F32)],
        compiler_params=_params(("arbitrary",)),
    )(fpre, b_pad)


def _forget_bwd(dc, fpre, b_pad):
    lp = fpre.shape[0]
    nb = lp // BLK

    def body(dc_ref, f_ref, b_ref, df_ref, db_ref, carry_ref):
        i = pl.program_id(0)

        @pl.when(i == 0)
        def _():
            carry_ref[...] = jnp.zeros_like(carry_ref)
            db_ref[...] = jnp.zeros_like(db_ref)

        dcv = dc_ref[...]
        row = lax.broadcasted_iota(jnp.int32, (BLK, BLK), 0)
        col = lax.broadcasted_iota(jnp.int32, (BLK, BLK), 1)
        tri = (col >= row).astype(BF16)
        p0 = dcv.astype(BF16)
        r1 = dcv - p0.astype(F32)
        p1 = r1.astype(BF16)
        p2 = (r1 - p1.astype(F32)).astype(BF16)
        dlogf = _dot(tri, p0, NN) + _dot(tri, p1, NN) + _dot(tri, p2, NN) + carry_ref[0:1, :]
        carry_ref[...] = jnp.broadcast_to(dlogf[0:1, :], carry_ref.shape)
        _, ls_neg = _log_sigmoid_parts(f_ref[...] + b_ref[...])
        df = dlogf * jnp.exp(ls_neg)
        df_ref[...] = df
        db_ref[...] += jnp.sum(df, axis=0, keepdims=True)

    rev = pl.BlockSpec((BLK, LANES), lambda i: (nb - 1 - i, 0))
    vec = pl.BlockSpec((1, LANES), lambda i: (0, 0))
    return pl.pallas_call(
        body, name="forget_bwd",
        out_shape=(jax.ShapeDtypeStruct((lp, LANES), F32), jax.ShapeDtypeStruct((1, LANES), F32)),
        grid=(nb,),
        in_specs=[rev, rev, vec],
        out_specs=(rev, vec),
        scratch_shapes=[pltpu.VMEM((SUBLANES, LANES), F32)],
        compiler_params=_params(("arbitrary",)),
    )(dc, fpre, b_pad)


def _mix_fwd(o_a, o_b, gates, h0, w_fox, w_sb, w_out, g1, g2):
    lp, d = h0.shape

    def body(oa_ref, ob_ref, ga_ref, gb_ref, h_ref, wf_ref, ws_ref, wo_ref, g_ref, g2_ref,
             h1_ref, ya_ref, yb_ref, gated_ref, mixed_ref, xn_ref):
        ya = _dot(oa_ref[...].astype(BF16), wf_ref[...], NN)
        yb = _dot(ob_ref[...].astype(BF16), ws_ref[...], NN)
        gated = _sigmoid(ga_ref[...]) * ya + _sigmoid(gb_ref[...]) * yb
        gb16 = gated.astype(BF16)
        mixed = _dot(gb16, wo_ref[...], NN)
        r = lax.rsqrt(jnp.mean(mixed * mixed, axis=-1, keepdims=True) + EPS)
        h1 = h_ref[...] + (mixed * r) * g_ref[...]
        h1_ref[...] = h1
        r2 = lax.rsqrt(jnp.mean(h1 * h1, axis=-1, keepdims=True) + EPS)
        xn_ref[...] = ((h1 * r2) * g2_ref[...]).astype(BF16)
        ya_ref[...] = ya
        yb_ref[...] = yb
        gated_ref[...] = gb16
        mixed_ref[...] = mixed

    row_w = pl.BlockSpec((BLK, WIDTH), lambda i: (i, 0))
    row_d = pl.BlockSpec((BLK, d), lambda i: (i, 0))
    full = lambda s: pl.BlockSpec(s, lambda i: (0, 0))
    return pl.pallas_call(
        body, name="mix_fwd",
        out_shape=(jax.ShapeDtypeStruct((lp, d), F32), jax.ShapeDtypeStruct((lp, d), F32),
                   jax.ShapeDtypeStruct((lp, d), F32), jax.ShapeDtypeStruct((lp, d), BF16),
                   jax.ShapeDtypeStruct((lp, d), F32), jax.ShapeDtypeStruct((lp, d), BF16)),
        grid=(lp // BLK,),
        in_specs=[row_w, row_w, row_d, pl.BlockSpec((BLK, d), lambda i: (i, 1)), row_d,
                  full((WIDTH, d)), full((WIDTH, d)), full((d, d)), full((1, d)), full((1, d))],
        out_specs=(row_d, row_d, row_d, row_d, row_d, row_d),
        compiler_params=_params(("parallel",)),
    )(o_a, o_b, gates, gates, h0, w_fox, w_sb, w_out, g1, g2)


def _gate_bwd(d_gated, gates, ya, yb):
    lp, d = d_gated.shape

    def body(dg_ref, ga_ref, gb_ref, ya_ref, yb_ref, dya_ref, dyb_ref, dga_ref, dgb_ref):
        dg = dg_ref[...]
        sa = _sigmoid(ga_ref[...])
        sb = _sigmoid(gb_ref[...])
        dya_ref[...] = (dg * sa).astype(BF16)
        dyb_ref[...] = (dg * sb).astype(BF16)
        dga_ref[...] = (dg * ya_ref[...] * (sa * (1.0 - sa))).astype(BF16)
        dgb_ref[...] = (dg * yb_ref[...] * (sb * (1.0 - sb))).astype(BF16)

    row = pl.BlockSpec((BLK, d), lambda i: (i, 0))
    out = jax.ShapeDtypeStruct((lp, d), BF16)
    return pl.pallas_call(
        body, name="gate_bwd",
        out_shape=(out, out, out, out),
        grid=(lp // BLK,),
        in_specs=[row, row, pl.BlockSpec((BLK, d), lambda i: (i, 1)), row, row],
        out_specs=(row, row, row, row),
        compiler_params=_params(("parallel",)),
    )(d_gated, gates, gates, ya, yb)


def _shift_down(cur, prev, n):
    rolled = pltpu.roll(cur, n, 0)
    row = lax.broadcasted_iota(jnp.int32, prev.shape, 0)
    head = jnp.where(row < n, pltpu.roll(prev, n, 0), rolled[0:SUBLANES])
    return head if cur.shape[0] == SUBLANES else jnp.concatenate([head, rolled[SUBLANES:]], axis=0)


def _shift_up(cur, nxt, n):
    rows = cur.shape[0]
    rolled = pltpu.roll(cur, rows - n, 0)
    row = lax.broadcasted_iota(jnp.int32, nxt.shape, 0)
    tail = jnp.where(row >= SUBLANES - n, pltpu.roll(nxt, SUBLANES - n, 0), rolled[rows - SUBLANES:])
    return tail if rows == SUBLANES else jnp.concatenate([rolled[:rows - SUBLANES], tail], axis=0)


def _gelu(x):
    return 0.5 * x * (1.0 + jnp.tanh(GELU_C * (x + GELU_A * (x * x * x))))


def _gelu_and_grad(x):
    t = jnp.tanh(GELU_C * (x + GELU_A * (x * x * x)))
    half = 0.5 * (1.0 + t)
    return x * half, half + 0.5 * x * (1.0 - t * t) * (GELU_C * (1.0 + 3.0 * GELU_A * (x * x)))


def _conv_taps(cur, prev, w_ref, b_ref):
    s1 = _shift_down(cur, prev, 1)
    s2 = _shift_down(cur, prev, 2)
    u = b_ref[...] + w_ref[0:1, :] * s2
    u = u + w_ref[1:2, :] * s1
    u = u + w_ref[2:3, :] * cur
    return u, s1, s2


def _conv_gelu_fwd(up, conv_w, conv_b, tc):
    lp, f2 = up.shape
    rb = BLK // SUBLANES

    def body(u_ref, p_ref, w_ref, b_ref, act_ref):
        i = pl.program_id(0)
        keep = (i > 0).astype(F32)
        u, _, _ = _conv_taps(u_ref[...], p_ref[...] * keep, w_ref, b_ref)
        act_ref[...] = (_gelu(u[:, :tc]) * u[:, tc:]).astype(BF16)

    prev_row = lambda i: jnp.maximum(i * rb - 1, 0)
    return pl.pallas_call(
        body, name="conv_gelu_fwd",
        out_shape=jax.ShapeDtypeStruct((lp, f2 // 2), BF16),
        grid=(lp // BLK, f2 // (2 * tc)),
        in_specs=[pl.BlockSpec((BLK, 2 * tc), lambda i, j: (i, j)),
                  pl.BlockSpec((SUBLANES, 2 * tc), lambda i, j: (prev_row(i), j)),
                  pl.BlockSpec((3, 2 * tc), lambda i, j: (0, j)),
                  pl.BlockSpec((1, 2 * tc), lambda i, j: (0, j))],
        out_specs=pl.BlockSpec((BLK, tc), lambda i, j: (i, j)),
        compiler_params=_params(("parallel", "parallel")),
    )(up, up, conv_w, conv_b)


def _conv_gelu_bwd(up, d_act, conv_w, conv_b, tc):
    lp, f2 = up.shape
    nb = lp // BLK
    rb = BLK // SUBLANES

    def du_of(u, da):
        gel, grad = _gelu_and_grad(u[:, :tc])
        return jnp.concatenate([da * u[:, tc:] * grad, da * gel], axis=1)

    def body(u_ref, p_ref, n_ref, da_ref, dan_ref, w_ref, b_ref, dup_ref, dcw_ref, dcb_ref):
        i = pl.program_id(1)
        cur = u_ref[...]
        u, s1, s2 = _conv_taps(cur, p_ref[...] * (i > 0).astype(F32), w_ref, b_ref)
        du = du_of(u, da_ref[...])
        u_next, _, _ = _conv_taps(n_ref[...], cur[BLK - SUBLANES:BLK, :], w_ref, b_ref)
        du_next = du_of(u_next, dan_ref[...]) * (i < nb - 1).astype(F32)
        n1 = _shift_up(du, du_next, 1)
        n2 = _shift_up(du, du_next, 2)
        dup_ref[...] = (w_ref[2:3, :] * du + w_ref[1:2, :] * n1 + w_ref[0:1, :] * n2).astype(BF16)

        @pl.when(i == 0)
        def _():
            dcw_ref[...] = jnp.zeros_like(dcw_ref)
            dcb_ref[...] = jnp.zeros_like(dcb_ref)

        dcw_ref[0:1, :] += jnp.sum(du * s2, axis=0, keepdims=True)
        dcw_ref[1:2, :] += jnp.sum(du * s1, axis=0, keepdims=True)
        dcw_ref[2:3, :] += jnp.sum(du * cur, axis=0, keepdims=True)
        dcb_ref[...] += jnp.sum(du, axis=0, keepdims=True)

    prev_row = lambda i: jnp.maximum(i * rb - 1, 0)
    next_row = lambda i: jnp.minimum((i + 1) * rb, nb * rb - 1)
    return pl.pallas_call(
        body, name="conv_gelu_bwd",
        out_shape=(jax.ShapeDtypeStruct((lp, f2), BF16), jax.ShapeDtypeStruct((3, f2), F32),
                   jax.ShapeDtypeStruct((1, f2), F32)),
        grid=(f2 // (2 * tc), nb),
        in_specs=[pl.BlockSpec((BLK, 2 * tc), lambda j, i: (i, j)),
                  pl.BlockSpec((SUBLANES, 2 * tc), lambda j, i: (prev_row(i), j)),
                  pl.BlockSpec((SUBLANES, 2 * tc), lambda j, i: (next_row(i), j)),
                  pl.BlockSpec((BLK, tc), lambda j, i: (i, j)),
                  pl.BlockSpec((SUBLANES, tc), lambda j, i: (next_row(i), j)),
                  pl.BlockSpec((3, 2 * tc), lambda j, i: (0, j)),
                  pl.BlockSpec((1, 2 * tc), lambda j, i: (0, j))],
        out_specs=(pl.BlockSpec((BLK, 2 * tc), lambda j, i: (i, j)),
                   pl.BlockSpec((3, 2 * tc), lambda j, i: (0, j)),
                   pl.BlockSpec((1, 2 * tc), lambda j, i: (0, j))),
        compiler_params=_params(("parallel", "arbitrary")),
    )(up, up, up, d_act, d_act, conv_w, conv_b)


def _out_loss(h1, ffn, g3, target):
    lp, d = h1.shape
    s = target.shape[0]
    n_valid = N_META + s

    def body(h_ref, f_ref, g_ref, t_ref, tp_ref, dy_ref, loss_ref, df_ref, dg_ref):
        i = pl.program_id(0)

        @pl.when(i == 0)
        def _():
            loss_ref[...] = jnp.zeros_like(loss_ref)
            dg_ref[...] = jnp.zeros_like(dg_ref)

        fv = f_ref[...]
        r = lax.rsqrt(jnp.mean(fv * fv, axis=-1, keepdims=True) + EPS)
        xh = fv * r
        y = h_ref[...] + xh * g_ref[...]
        row = i * BLK + lax.broadcasted_iota(jnp.int32, (BLK, 1), 0)
        valid = (row >= N_META) & (row < n_valid)
        diff = jnp.where(valid, y - _shifted_rows(t_ref, tp_ref), 0.0)
        dy = diff * (1.0 / d)
        dy_ref[...] = dy
        per_row = jnp.mean(diff * diff, axis=-1, keepdims=True)
        loss_ref[...] += 0.5 * jnp.sum(per_row, axis=0, keepdims=True)
        dyg = dy * g_ref[...]
        df_ref[...] = (r * (dyg - xh * jnp.mean(dyg * xh, axis=-1, keepdims=True))).astype(BF16)
        dg_ref[...] += jnp.sum(dy * xh, axis=0, keepdims=True)

    row_d = pl.BlockSpec((BLK, d), lambda i: (i, 0))
    vec = pl.BlockSpec((1, d), lambda i: (0, 0))
    return pl.pallas_call(
        body, name="out_loss",
        out_shape=(jax.ShapeDtypeStruct((lp, d), F32), jax.ShapeDtypeStruct((SUBLANES, LANES), F32),
                   jax.ShapeDtypeStruct((lp, d), BF16), jax.ShapeDtypeStruct((1, d), F32)),
        grid=(lp // BLK,),
        in_specs=[row_d, row_d, vec, *_shifted_specs(s, d)],
        out_specs=(row_d, pl.BlockSpec((SUBLANES, LANES), lambda i: (0, 0)), row_d, vec),
        compiler_params=_params(("arbitrary",)),
    )(h1, ffn, g3, target, target)


def _head_masks(rows=BLK):
    lane = lax.broadcasted_iota(jnp.int32, (rows, LANES), 1)
    return [lane < HEAD_DIM, lane >= HEAD_DIM]


def _att_specs(base, lp):
    q_spec = pl.BlockSpec((BLK, LANES), lambda p, i: (i, base + p))
    k_spec = pl.BlockSpec((lp, LANES), lambda p, i: (0, base + N_PAIRS + p))
    v_spec = pl.BlockSpec((lp, LANES), lambda p, i: (0, base + 2 * N_PAIRS + p))
    return q_spec, k_spec, v_spec


def _kv_rows(j, nb=1):
    return pl.ds(pl.multiple_of(j * BLK, nb * BLK), nb * BLK)


def _walk_kv(i, tile, reverse=False, first_block=0, more=None):
    if reverse:
        alive = more if more is not None else (lambda: True)
        tile(i, 1, True)

        @pl.when((i >= 1) & alive())
        def _():
            tile(i - 1, 1, False)

        rest = jnp.maximum(i - 1, 0)

        @pl.when((rest % 2 == 1) & alive())
        def _():
            tile(rest - 1, 1, False)

        pairs = rest // 2

        def wide(t):
            tile(2 * (pairs - 1 - t), 2, False)
            return t + 1

        lax.while_loop(lambda c: (c[0] < pairs) & c[1], lambda c: (wide(c[0]), alive()), (0, alive()))
        return

    pairs, odd = i // 2, i % 2

    @pl.when((first_block % 2 == 1) & (first_block < i))
    def _():
        tile(first_block, 1, False)

    def wide(t, carry):
        tile(2 * t, 2, False)
        return carry

    lax.fori_loop((first_block + 1) // 2, pairs, wide, 0)

    @pl.when(odd == 1)
    def _():
        tile(i - 1, 1, False)

    tile(i, 1, True)


def _walk_kv_ahead(i, lead, follow, kept, first_block=0):
    pairs, odd = i // 2, i % 2
    first_pair = (first_block + 1) // 2

    def keep(values):
        for ref, value in zip(kept, values):
            ref[...] = value

    @pl.when((first_block % 2 == 1) & (first_block < i))
    def _():
        follow(lead(first_block, 1), first_block, 1, False)

    @pl.when(pairs > first_pair)
    def _():
        keep(lead(2 * first_pair, 2))

    def wide(t, carry):
        ahead = lead(2 * jnp.minimum(t + 1, pairs - 1), 2)
        follow([ref[...] for ref in kept], 2 * t, 2, False)
        keep(ahead)
        return carry

    lax.fori_loop(first_pair, pairs, wide, 0)

    @pl.when(odd == 1)
    def _():
        follow(lead(i - 1, 1), i - 1, 1, False)

    follow(lead(i, 1), i, 1, True)


def _bf16_pieces(x):
    rnd = lambda a: lax.reduce_precision(a, exponent_bits=8, mantissa_bits=7)
    p0 = rnd(x)
    p1 = rnd(x - p0)
    p2 = rnd(x - p0 - p1)
    return [p0.astype(BF16), p1.astype(BF16), p2.astype(BF16)]


def _aug_lanes(cols):
    lp = cols[0].shape[0]
    vals = jnp.stack([c.astype(BF16) for c in cols], axis=-1)
    vals = vals.reshape(lp, N_PAIRS, 2, len(cols))[:, :, ::-1, :]
    vals = jnp.pad(vals, ((0, 0), (0, 0), (0, 0), (0, HEAD_DIM - len(cols))))
    return vals.reshape(lp, WIDTH)


N_AUG = 3


def _riding(exchange, n_in, n_out, n_scratch, grid):
    n = exchange.n if exchange else 0

    def split(refs):
        own_in, ex_in = refs[:n_in], refs[n_in:n_in + n]
        own_out, ex_out = refs[n_in + n:n_in + n + n_out], refs[n_in + n + n_out:n_in + 2 * n + n_out]
        rest = refs[n_in + 2 * n + n_out:]
        return own_in + own_out + rest[:n_scratch], (ex_in, ex_out, rest[n_scratch:])

    def at(step_of):
        here = pl.program_id(0) == step_of(grid[0])
        for axis in range(1, len(grid)):
            here = here & (pl.program_id(axis) == step_of(grid[axis]))
        return here

    def first_step(ex_refs):
        if exchange:
            @pl.when(at(lambda size: 0))
            def _():
                exchange.start(*ex_refs)

    def last_step(ex_refs):
        if exchange:
            @pl.when(at(lambda size: size - 1))
            def _():
                exchange.wait(*ex_refs)

    return split, first_step, last_step


def _fox_fwd(qkv, qaug, kaug, exchange=None):
    lp = qkv.shape[0]
    nq = lp // BLK
    split, first_step, last_step = _riding(exchange, 5, 2, 3, (N_PAIRS, nq))

    def body(*refs):
        (q_ref, k_ref, v_ref, qa_ref, ka_ref, o_ref, lse_ref, acc_ref, m_ref, s_ref), ex_refs = split(refs)
        first_step(ex_refs)
        i = pl.program_id(1)
        hs = range(2)
        masks = _head_masks()
        qs = q_ref[...] * ATT_SCALE
        qa = qa_ref[...]
        qh = [jnp.where(masks[hh], qs, qa) for hh in hs]
        acc_ref[...] = jnp.zeros_like(acc_ref)
        m_ref[...] = jnp.full_like(m_ref, -1e30)
        row = lax.broadcasted_iota(jnp.int32, (BLK, BLK), 0)
        col = lax.broadcasted_iota(jnp.int32, (BLK, BLK), 1)
        causal = col <= row

        def scores(j, nb):
            rows = _kv_rows(j, nb)
            kmasks = _head_masks(nb * BLK)
            k, ka = k_ref[rows, :], ka_ref[rows, :]
            return [_dot(qh[hh], jnp.where(kmasks[hh], k, ka), NT) for hh in hs]

        def absorb(s, j, nb, diag):
            v = v_ref[_kv_rows(j, nb), :]
            kmasks = _head_masks(nb * BLK)
            vh = [jnp.where(kmasks[hh], v, jnp.ones_like(v)) for hh in hs]
            if diag:
                s = [jnp.where(causal, s[hh], -1e30) for hh in hs]
            m_prev = [m_ref[hh] for hh in hs]
            m_new = [jnp.maximum(m_prev[hh], jnp.max(s[hh], axis=-1, keepdims=True)) for hh in hs]
            p = [jnp.exp(s[hh] - m_new[hh]).astype(BF16) for hh in hs]
            for hh in hs:
                acc_ref[hh] = jnp.exp(m_prev[hh] - m_new[hh]) * acc_ref[hh] + _dot(p[hh], vh[hh], NN)
                m_ref[hh] = m_new[hh]

        _walk_kv_ahead(i, scores, absorb, [s_ref.at[hh] for hh in hs], _first_block(qa, i))
        acc = [acc_ref[hh] for hh in hs]
        denom = [acc[0][:, HEAD_DIM:HEAD_DIM + 1], acc[1][:, 0:1]]
        o_ref[...] = jnp.where(masks[0], acc[0] / denom[0], acc[1] / denom[1])
        lse = jnp.where(masks[0], m_ref[0] + jnp.log(denom[0]), m_ref[1] + jnp.log(denom[1])).T
        lse_ref[0, 0, 0:1, :] = lse[0:1, :]
        lse_ref[0, 0, 1:2, :] = lse[HEAD_DIM:HEAD_DIM + 1, :]
        last_step(ex_refs)

    q_spec, k_spec, v_spec = _att_specs(0, lp)
    blk = pl.BlockSpec((BLK, LANES), lambda p, i: (i, p))
    col_full = pl.BlockSpec((lp, LANES), lambda p, i: (0, p))
    out = jax.ShapeDtypeStruct((lp, WIDTH), F32)
    ex = exchange
    return pl.pallas_call(
        body, name="fox_fwd",
        out_shape=(out, jax.ShapeDtypeStruct((N_PAIRS, nq, 2, BLK), F32)) + tuple(ex.out_shapes if ex else ()),
        grid=(N_PAIRS, nq),
        in_specs=[q_spec, k_spec, v_spec, blk, col_full] + (ex.any_specs if ex else []),
        out_specs=(blk, pl.BlockSpec((1, 1, 2, BLK), lambda p, i: (p, i, 0, 0))) + tuple(ex.any_specs if ex else ()),
        scratch_shapes=[pltpu.VMEM((2, BLK, LANES), F32), pltpu.VMEM((2, BLK, 1), F32),
                        pltpu.VMEM((2, BLK, 2 * BLK), F32)] + (ex.scratch if ex else []),
        compiler_params=_params(("arbitrary", "arbitrary") if ex else ("parallel", "parallel")),
    )(qkv, qkv, qkv, qaug, kaug, *(ex.arrays if ex else []))


FIRST_BLOCK_LANE = 9


def _first_block(qa, i):
    lane = lax.broadcasted_iota(jnp.int32, qa.shape, 1)
    first = jnp.max(jnp.where(lane == HEAD_DIM + FIRST_BLOCK_LANE, qa.astype(F32), 0.0)).astype(jnp.int32)
    return jnp.clip(first, 0, i)


def _fox_first_blocks(qkv, c):
    lp = qkv.shape[0]
    nq = lp // BLK

    def body(x_ref, o_ref):
        xv = x_ref[...].astype(F32)
        sq = (xv * xv).astype(BF16)
        col = lax.broadcasted_iota(jnp.int32, (2 * WIDTH, LANES), 0)
        lane = lax.broadcasted_iota(jnp.int32, (2 * WIDTH, LANES), 1)
        pick = (col // HEAD_DIM == lane).astype(BF16)
        o_ref[...] = _dot(sq, pick, NN)

    norms = pl.pallas_call(
        body, name="fox_norms",
        out_shape=jax.ShapeDtypeStruct((lp, LANES), F32),
        grid=(nq,),
        in_specs=[pl.BlockSpec((BLK, 2 * WIDTH), lambda i: (i, 0))],
        out_specs=pl.BlockSpec((BLK, LANES), lambda i: (i, 0)),
        compiler_params=_params(("parallel",)),
    )(qkv)
    a_max = 1.02 * jnp.sqrt(norms[:, :N_HEADS].reshape(nq, BLK, N_HEADS).max(axis=1))
    b_max = 1.02 * jnp.sqrt(norms[:, N_HEADS:2 * N_HEADS].reshape(nq, BLK, N_HEADS).max(axis=1))
    c_max = c.reshape(nq, BLK, N_HEADS).max(axis=1)
    c_min = c.reshape(nq, BLK, N_HEADS).min(axis=1)
    bound = (a_max[:, None] * (b_max[None, :] + b_max[:, None]) * ATT_SCALE + c_max[:, None] - c_min[None, :])
    alive = (bound > EXP_IS_ZERO) | jnp.isnan(bound)
    alive = alive.reshape(nq, nq, N_PAIRS, 2).any(axis=-1)
    first = jnp.argmax(alive, axis=1).astype(F32)
    return jnp.repeat(jnp.repeat(first, 2, axis=1), BLK, axis=0)


def _head_dots(a, b):
    lp = a.shape[0]

    def body(a_ref, b_ref, o_ref):
        prod = a_ref[...] * b_ref[...]
        col = lax.broadcasted_iota(jnp.int32, (WIDTH, LANES), 0)
        lane = lax.broadcasted_iota(jnp.int32, (WIDTH, LANES), 1)
        pick = (col // HEAD_DIM == lane).astype(BF16)
        hi = prod.astype(BF16)
        mid, lo = _split_bf16(prod - hi.astype(F32))
        o_ref[...] = _dot(hi, pick, NN) + _dot(mid, pick, NN) + _dot(lo, pick, NN)

    row = pl.BlockSpec((BLK, WIDTH), lambda i: (i, 0))
    return pl.pallas_call(
        body, name="head_dots",
        out_shape=jax.ShapeDtypeStruct((lp, LANES), F32),
        grid=(lp // BLK,),
        in_specs=[row, row],
        out_specs=pl.BlockSpec((BLK, LANES), lambda i: (i, 0)),
        compiler_params=_params(("parallel",)),
    )(a, b)


def _fox_bwd(qkv, qaug, kaug, doaug, d_o, exchange=None):
    lp = qkv.shape[0]
    nq = lp // BLK
    split, first_step, last_step = _riding(exchange, 7, 4, 2, (N_PAIRS, nq))

    def body(*refs):
        (q_ref, k_ref, v_ref, qa_ref, ka_ref, da_ref, do_ref,
         dq_ref, dk_ref, dv_ref, dc_ref, acc_ref, rs_ref), ex_refs = split(refs)
        first_step(ex_refs)
        i = pl.program_id(1)
        hs = range(2)
        masks = _head_masks()
        qs = q_ref[...] * ATT_SCALE
        qa = qa_ref[...]
        qm = [jnp.where(masks[hh], qs, 0) for hh in hs]
        qh = [jnp.where(masks[hh], qs, qa) for hh in hs]
        dov = do_ref[...].astype(BF16)
        doa = da_ref[...]
        dom = [jnp.where(masks[hh], dov, 0) for hh in hs]
        doh = [jnp.where(masks[hh], dov, doa) for hh in hs]
        row = lax.broadcasted_iota(jnp.int32, (BLK, BLK), 0)
        col = lax.broadcasted_iota(jnp.int32, (BLK, BLK), 1)
        causal = col <= row

        @pl.when(i == 0)
        def _():
            dk_ref[...] = jnp.zeros_like(dk_ref)
            dv_ref[...] = jnp.zeros_like(dv_ref)
            dc_ref[...] = jnp.zeros_like(dc_ref)

        acc_ref[...] = jnp.zeros_like(acc_ref)
        rs_ref[...] = jnp.zeros_like(rs_ref)

        def lead(j, nb):
            rows = _kv_rows(j, nb)
            kmasks = _head_masks(nb * BLK)
            k, ka, v = k_ref[rows, :], ka_ref[rows, :], v_ref[rows, :]
            ones = (lax.broadcasted_iota(jnp.int32, v.shape, 1) % HEAD_DIM < N_AUG).astype(BF16)
            logp = [_dot(qh[hh], jnp.where(kmasks[hh], k, ka), NT) for hh in hs]
            dp = [_dot(doh[hh], jnp.where(kmasks[hh], v, ones), NT) for hh in hs]
            return logp + dp

        def follow(lead_out, j, nb, diag):
            rows = _kv_rows(j, nb)
            k = k_ref[rows, :]
            logp, dp = lead_out[:2], lead_out[2:]
            p = [jnp.exp(logp[hh]) for hh in hs]
            if diag:
                p = [jnp.where(causal, p[hh], 0.0) for hh in hs]
            ds = [p[hh] * dp[hh] for hh in hs]
            dsb = [ds[hh].astype(BF16) for hh in hs]
            for hh in hs:
                acc_ref[hh] += _dot(dsb[hh], k, NN)
                col_sums = jnp.sum(ds[hh], axis=0, keepdims=True)
                for b in range(nb):
                    dc_ref[0, j + b, hh:hh + 1, :] -= col_sums[:, b * BLK:(b + 1) * BLK]
                rs_ref[hh] += jnp.sum(ds[hh], axis=-1, keepdims=True)
            dk_ref[rows, :] += _dot(dsb[0], qm[0], TN) + _dot(dsb[1], qm[1], TN)
            dv_ref[rows, :] += _dot(p[0].astype(BF16), dom[0], TN) + _dot(p[1].astype(BF16), dom[1], TN)

        _walk_kv(i, lambda j, nb, diag: follow(lead(j, nb), j, nb, diag), first_block=_first_block(qa, i))
        dq_ref[...] = (jnp.where(masks[0], acc_ref[0], acc_ref[1]) * ATT_SCALE).astype(BF16)
        row_sums = jnp.where(masks[0], rs_ref[0], rs_ref[1]).T
        dc_ref[0, i, 0:1, :] += row_sums[0:1, :]
        dc_ref[0, i, 1:2, :] += row_sums[HEAD_DIM:HEAD_DIM + 1, :]
        last_step(ex_refs)

    q_spec, k_spec, v_spec = _att_specs(0, lp)
    blk = pl.BlockSpec((BLK, LANES), lambda p, i: (i, p))
    col_full = pl.BlockSpec((lp, LANES), lambda p, i: (0, p))
    crow_spec = pl.BlockSpec((1, nq, 2, BLK), lambda p, i: (p, 0, 0, 0))
    ex = exchange
    return pl.pallas_call(
        body, name="fox_bwd",
        out_shape=(jax.ShapeDtypeStruct((lp, WIDTH), BF16), jax.ShapeDtypeStruct((lp, WIDTH), F32),
                   jax.ShapeDtypeStruct((lp, WIDTH), F32), jax.ShapeDtypeStruct((N_PAIRS, nq, 2, BLK), F32))
        + tuple(ex.out_shapes if ex else ()),
        grid=(N_PAIRS, nq),
        in_specs=[q_spec, k_spec, v_spec, blk, col_full, blk, blk] + (ex.any_specs if ex else []),
        out_specs=(blk, col_full, col_full, crow_spec) + tuple(ex.any_specs if ex else ()),
        scratch_shapes=[pltpu.VMEM((2, BLK, LANES), F32), pltpu.VMEM((2, BLK, 1), F32)] + (ex.scratch if ex else []),
        compiler_params=_params(("arbitrary", "arbitrary") if ex else ("parallel", "arbitrary")),
    )(qkv, qkv, qkv, qaug, kaug, doaug, d_o, *(ex.arrays if ex else []))


def _sb_scores(z):
    ell = jnp.minimum(z, 0.0) - jnp.log(1.0 + jnp.exp(-jnp.abs(z)))
    return ell, ell - z


def _stacked(tri):
    return jnp.concatenate([tri, tri], axis=0)


def _cumsum_dot(x, tri2):
    hi, lo = _split_bf16(x)
    return _dot(jnp.concatenate([hi, lo], axis=1), tri2, NN)


def _sb_units(qh, k, strict2, causal, nb, diag):
    hs, bs = range(2), range(nb)
    z = [_dot(qh[hh], k, NT) for hh in hs]
    sc = [[_sb_scores(z[hh][:, b * BLK:(b + 1) * BLK]) for b in bs] for hh in hs]
    ell = [[sc[hh][b][0] for b in bs] for hh in hs]
    kap = [[jnp.where(causal, sc[hh][b][1], 0.0) if diag else sc[hh][b][1] for b in bs] for hh in hs]
    later = [[_cumsum_dot(kap[hh][b], strict2) for b in bs] for hh in hs]
    return ell, kap, later


def _row_sum(x):
    return jnp.sum(x, axis=-1, keepdims=True)


def _join(blocks):
    joined = blocks[0] if len(blocks) == 1 else jnp.concatenate(blocks, axis=1)
    return joined.astype(BF16)


def _sb_fwd(qkv):
    lp = qkv.shape[0]
    nq = lp // BLK
    assert nq <= HEAD_DIM

    def body(q_ref, k_ref, v_ref, o_ref, lc_ref, acc_ref, car_ref):
        i = pl.program_id(1)
        masks = _head_masks()
        qs = q_ref[...] * ATT_SCALE
        qh = [jnp.where(mk, qs, 0).astype(BF16) for mk in masks]
        row = lax.broadcasted_iota(jnp.int32, (BLK, BLK), 0)
        col = lax.broadcasted_iota(jnp.int32, (BLK, BLK), 1)
        lane = lax.broadcasted_iota(jnp.int32, (BLK, LANES), 1)
        causal = col < row
        strict2 = _stacked((row > col).astype(BF16))
        acc_ref[...] = jnp.zeros_like(acc_ref)
        car_ref[...] = jnp.zeros_like(car_ref)
        lc_ref[...] = jnp.full_like(lc_ref, NOT_VISITED)

        def tile(j, nb, diag):
            hs, bs = range(2), range(nb)
            rows = _kv_rows(j, nb)
            k, v = k_ref[rows, :], v_ref[rows, :]
            ell, kap, later = _sb_units(qh, k, strict2, causal, nb, diag)
            car = [[None] * nb for _ in hs]
            for hh in hs:
                run = car_ref[hh]
                for b in reversed(bs):
                    car[hh][b] = run
                    run = run + _row_sum(kap[hh][b])
                car_ref[hh] = run
            if not diag:
                kept = lc_ref[...]
                for hh in hs:
                    for b in bs:
                        kept = jnp.where(lane == j + b + HEAD_DIM * hh, car[hh][b], kept)
                lc_ref[...] = kept
            a = [[jnp.exp(ell[hh][b] + later[hh][b] + car[hh][b]) for b in bs] for hh in hs]
            if diag:
                a = [[jnp.where(causal, a[hh][b], 0.0) for b in bs] for hh in hs]
            for hh in hs:
                acc_ref[hh] += _dot(_join(a[hh]), v, NN)

        _walk_kv(i, tile, reverse=True, more=lambda: jnp.max(car_ref[...]) > EXP_IS_ZERO)
        o_ref[...] = jnp.where(masks[0], acc_ref[0], acc_ref[1])

    q_spec, k_spec, v_spec = _att_specs(3 * N_PAIRS, lp)
    blk = pl.BlockSpec((BLK, LANES), lambda p, i: (i, p))
    out = jax.ShapeDtypeStruct((lp, WIDTH), F32)
    return pl.pallas_call(
        body, name="sb_fwd",
        out_shape=(out, out),
        grid=(N_PAIRS, nq),
        in_specs=[q_spec, k_spec, v_spec],
        out_specs=(blk, blk),
        scratch_shapes=[pltpu.VMEM((2, BLK, LANES), F32), pltpu.VMEM((2, BLK, 1), F32)],
        compiler_params=_params(("parallel", "parallel")),
    )(qkv, qkv, qkv)


def _sb_bwd(qkv, lcar, d_o):
    lp = qkv.shape[0]
    nq = lp // BLK

    def body(q_ref, k_ref, v_ref, lc_ref, do_ref, dq_ref, dk_ref, dv_ref, acc_ref, cg_ref):
        i = pl.program_id(1)
        masks = _head_masks()
        qs = q_ref[...] * ATT_SCALE
        qh = [jnp.where(mk, qs, 0).astype(BF16) for mk in masks]
        dov = do_ref[...]
        doh = [jnp.where(mk, dov, 0.0).astype(BF16) for mk in masks]
        lcv = lc_ref[...]
        lane_row = lax.broadcasted_iota(jnp.int32, (1, LANES), 1)
        alive = jnp.where(jnp.max(lcv, axis=0, keepdims=True) > EXP_IS_ZERO, 1.0, 0.0)
        n_alive = jnp.maximum(jnp.sum(jnp.where(lane_row < HEAD_DIM, alive, 0.0)),
                              jnp.sum(jnp.where(lane_row >= HEAD_DIM, alive, 0.0))).astype(jnp.int32)
        first_block = jnp.maximum(i - n_alive, 0)
        row = lax.broadcasted_iota(jnp.int32, (BLK, BLK), 0)
        col = lax.broadcasted_iota(jnp.int32, (BLK, BLK), 1)
        lane = lax.broadcasted_iota(jnp.int32, (BLK, LANES), 1)
        causal = col < row
        strict2 = _stacked((row > col).astype(BF16))
        before2 = _stacked((row < col).astype(BF16))

        @pl.when(i == 0)
        def _():
            dk_ref[...] = jnp.zeros_like(dk_ref)
            dv_ref[...] = jnp.zeros_like(dv_ref)

        acc_ref[...] = jnp.zeros_like(acc_ref)
        cg_ref[...] = jnp.zeros_like(cg_ref)

        def tile(j, nb, diag):
            hs, bs = range(2), range(nb)
            rows = _kv_rows(j, nb)
            k, v = k_ref[rows, :], v_ref[rows, :]
            if diag:
                car = [[0.0] for _ in hs]
            else:
                car = [[_row_sum(jnp.where(lane == j + b + HEAD_DIM * hh, lcv, 0.0)) for b in bs] for hh in hs]
            da = [_dot(doh[hh], v, NT) for hh in hs]
            ell, _, later = _sb_units(qh, k, strict2, causal, nb, diag)
            a = [[jnp.exp(ell[hh][b] + later[hh][b] + car[hh][b]) for b in bs] for hh in hs]
            if diag:
                a = [[jnp.where(causal, a[hh][b], 0.0) for b in bs] for hh in hs]
            g = [[da[hh][:, b * BLK:(b + 1) * BLK] * a[hh][b] for b in bs] for hh in hs]
            cg = [[_cumsum_dot(g[hh][b], before2) for b in bs] for hh in hs]
            before = [[None] * nb for _ in hs]
            for hh in hs:
                run = cg_ref[hh]
                for b in bs:
                    before[hh][b] = run
                    run = run + _row_sum(g[hh][b])
                cg_ref[hh] = run
            dz = [[g[hh][b] - jnp.exp(ell[hh][b]) * (g[hh][b] + cg[hh][b] + before[hh][b]) for b in bs] for hh in hs]
            if diag:
                dz = [[jnp.where(causal, dz[hh][b], 0.0) for b in bs] for hh in hs]
            dzb = [_join(dz[hh]) for hh in hs]
            ab = [_join(a[hh]) for hh in hs]
            for hh in hs:
                acc_ref[hh] += _dot(dzb[hh], k, NN)
            dk_ref[rows, :] += _dot(dzb[0], qh[0], TN) + _dot(dzb[1], qh[1], TN)
            dv_ref[rows, :] += _dot(ab[0], doh[0], TN) + _dot(ab[1], doh[1], TN)

        _walk_kv(i, tile, first_block=first_block)
        dq_ref[...] = (jnp.where(masks[0], acc_ref[0], acc_ref[1]) * ATT_SCALE).astype(BF16)

    q_spec, k_spec, v_spec = _att_specs(3 * N_PAIRS, lp)
    blk = pl.BlockSpec((BLK, LANES), lambda p, i: (i, p))
    col_full = pl.BlockSpec((lp, LANES), lambda p, i: (0, p))
    return pl.pallas_call(
        body, name="sb_bwd",
        out_shape=(jax.ShapeDtypeStruct((lp, WIDTH), BF16), jax.ShapeDtypeStruct((lp, WIDTH), F32),
                   jax.ShapeDtypeStruct((lp, WIDTH), F32)),
        grid=(N_PAIRS, nq),
        in_specs=[q_spec, k_spec, v_spec, blk, blk],
        out_specs=(blk, col_full, col_full),
        scratch_shapes=[pltpu.VMEM((2, BLK, LANES), F32), pltpu.VMEM((2, BLK, 1), F32)],
        compiler_params=_params(("parallel", "arbitrary")),
    )(qkv, qkv, qkv, lcar, d_o)


def _local_step(x, target, meta, gains, w_in, b_forget, w_fox, w_sb, w_out, w_up, conv_w, conv_b, w_down,
                ffn_block=None, late_weights=None, early_grads=None, w_in_grad=None):
    s, d = x.shape
    n_valid = N_META + s
    lp = -(-n_valid // BLK) * BLK
    pad = lp - n_valid
    nq = lp // BLK


    q_a, k_a, v_a, f_a, q_b, k_b, v_b, g_a, g_b = jnp.split(
        w_in, [512, 1024, 1536, 1544, 2056, 2568, 3080, 4104], axis=1)
    w_qkv = jnp.concatenate([q_a, k_a, v_a, q_b, k_b, v_b], axis=1)
    w_gf = jnp.concatenate([g_a, g_b, f_a, jnp.zeros((d, F_PAD - N_HEADS), BF16)], axis=1)
    b_pad = jnp.concatenate([b_forget.reshape(1, N_HEADS), jnp.zeros((1, LANES - N_HEADS), F32)], axis=1)
    g0, g1, g2, g3 = (gains[i:i + 1] for i in range(4))

    h0, xn1 = _first_norm_fwd(x, meta, g0, lp)
    qkv = _mm(xn1, w_qkv, "nn", BF16, "proj_qkv")
    gf = _mm(xn1, w_gf, "nn", F32, "proj_gates")
    fpre = gf[:, 2 * d:2 * d + LANES]
    c = _forget_fwd(fpre, b_pad)[:, :N_HEADS]
    c_pieces = _bf16_pieces(c)
    one = jnp.ones((lp, N_HEADS), BF16)
    kaug = _aug_lanes(3 * [one] + [-x for x in c_pieces] + 3 * [one])
    first_block = _fox_first_blocks(qkv, c)
    o_a, lse, *gathered = _fox_fwd(qkv, _aug_lanes(c_pieces + 3 * [one] + 3 * [0 * one] + [first_block]), kaug,
                                   late_weights[0] if late_weights else None)
    if late_weights:
        w_fox, w_sb, w_out, w_up, conv_w, w_down = late_weights[1](gathered)
    ffn_block = ffn_block or w_up.shape[1] // 2
    o_b, lcar = _sb_fwd(qkv)
    h1, ya, yb, gated, mixed, xn3 = _mix_fwd(o_a, o_b, gf, h0, w_fox, w_sb, w_out, g1, g2)
    up = _mm(xn3, w_up, "nn", F32, "ffn_up")
    act = _conv_gelu_fwd(up, conv_w, conv_b, ffn_block)
    ffn = _mm(act, w_down, "nn", F32, "ffn_down")
    dy, loss_acc, d_ffn, dg3 = _out_loss(h1, ffn, g3, target)
    loss = loss_acc[0, 0]

    d_act = _mm(d_ffn, w_down, "nt", F32, "ffn_down_dx")
    gw_down = _mm(act, d_ffn, "tn", BF16, "ffn_down_dw")
    d_up, g_conv_w, g_conv_b = _conv_gelu_bwd(up, d_act, conv_w, conv_b, ffn_block)
    d_xn3 = _mm(d_up, w_up, "nt", F32, "ffn_up_dx")
    gw_up = _mm(xn3, d_up, "tn", BF16, "ffn_up_dw")
    dh1, dg2 = _rmsnorm_bwd(h1, g2, d_xn3, dy, F32, "norm3_bwd")

    d_mixed, dg1 = _rmsnorm_bwd(mixed, g1, dh1, None, BF16, "norm2_bwd")
    d_gated = _mm(d_mixed, w_out, "nt", F32, "out_dx")
    gw_out = _mm(gated, d_mixed, "tn", BF16, "out_dw")
    d_ya, d_yb, d_ga, d_gb = _gate_bwd(d_gated, gf, ya, yb)
    d_oa = _mm(d_ya, w_fox, "nt", F32, "fox_o_dx")
    gw_fox = _mm(o_a, d_ya, "tn", BF16, "fox_o_dw")
    d_ob = _mm(d_yb, w_sb, "nt", F32, "sb_o_dx")
    gw_sb = _mm(o_b, d_yb, "tn", BF16, "sb_o_dw")
    neg_lse = [-x for x in _bf16_pieces(lse.transpose(0, 2, 1, 3).reshape(N_HEADS, lp).T)]
    neg_dsum = [-x for x in _bf16_pieces(_head_dots(d_oa, o_a)[:, :N_HEADS])]
    qaug = _aug_lanes(c_pieces + 3 * [one] + neg_lse + [first_block])
    early = {"w_o_fox": gw_fox, "w_o_sb": gw_sb, "w_out": gw_out, "w_up": gw_up, "conv_w": g_conv_w,
             "conv_b": g_conv_b, "w_down": gw_down}
    dq_a, dk_a, dv_a, dcrow, *early_parts = _fox_bwd(
        qkv, qaug, kaug, _aug_lanes(neg_dsum), d_oa, early_grads(early) if early_grads else None)
    dq_b, dk_b, dv_b = _sb_bwd(qkv, lcar, d_ob)
    dc = dcrow.transpose(0, 2, 1, 3).reshape(N_HEADS, lp).T
    dc = jnp.concatenate([dc, jnp.zeros((lp, LANES - N_HEADS), F32)], axis=1)
    df, db = _forget_bwd(dc, fpre, b_pad)
    lane = jnp.arange(LANES) < N_HEADS
    df = jnp.where(lane[None, :], df, 0.0)
    d_proj = jnp.concatenate(
        [dq_a, dk_a.astype(BF16), dv_a.astype(BF16), dq_b, dk_b.astype(BF16), dv_b.astype(BF16),
         d_ga, d_gb, df.astype(BF16), jnp.zeros((lp, F_PAD - LANES), BF16)], axis=1)
    w_in_p = jnp.concatenate([w_qkv, w_gf], axis=1)
    gw_in_p = _mm(xn1, d_proj, "tn", BF16, "proj_dw")
    qkv_parts = jnp.split(gw_in_p[:, :6 * WIDTH], 6, axis=1)
    gw_in = jnp.concatenate(
        qkv_parts[:3] + [gw_in_p[:, 6 * WIDTH + 2 * d:6 * WIDTH + 2 * d + N_HEADS]] + qkv_parts[3:]
        + [gw_in_p[:, 6 * WIDTH:6 * WIDTH + 2 * d]], axis=1)
    if w_in_grad:
        d_xn1, *w_in_parts = _mm(d_proj, w_in_p, "nt", F32, "proj_dx", w_in_grad(gw_in))
    else:
        d_xn1, w_in_parts = _mm(d_proj, w_in_p, "nt", F32, "proj_dx"), []
    grad_x, grad_meta, dg0 = _first_norm_bwd(h0, g0, d_xn1, dh1, s)
    grads = {
        "w_in_parts": w_in_parts,
        "meta_tokens": grad_meta,
        "norm_gains": jnp.concatenate([dg0, dg1, dg2, dg3], axis=0),
        "w_in": gw_in,
        "b_forget": db[:, :N_HEADS],
        "early_parts": early_parts,
        **early,
    }
    return loss, grad_x, grads


MESH_IDS = pl.DeviceIdType.MESH


def _window(ref, kind, idx, rows, cols):
    if kind == "slots":
        return ref.at[idx]
    if kind == "gate_value":
        half = N_DEV // 2
        idx = jnp.where(idx < half, 2 * idx, 2 * (idx - half) + 1)
        kind = "cols"
    if kind == "cols":
        return ref.at[:, pl.ds(pl.multiple_of(idx * cols, cols & -cols), cols)]
    return ref.at[pl.ds(pl.multiple_of(idx * rows, rows & -rows), rows), :]


def _gathered_shape(shape, kind):
    rows, cols = shape
    return {"slots": (N_DEV, rows, cols), "cols": (rows, N_DEV * cols), "gate_value": (rows, N_DEV * cols),
            "rows": (N_DEV * rows, cols)}[kind]


def _all_gather(shards, kinds):
    n = len(shards)

    def body(*refs):
        ins, outs = refs[:n], refs[n:2 * n]
        send_sems, recv_sems, local_sems = refs[2 * n:]
        x, y, c = lax.axis_index("x"), lax.axis_index("y"), lax.axis_index("c")
        me, sibling = (x, y, c), (x, y, 1 - c)
        chips = [(1 - x, y), (x, 1 - y), (1 - x, 1 - y)]

        def part(t, px, py, pc):
            return _window(outs[t], kinds[t], 4 * px + 2 * py + pc, *shards[t].shape)

        def copy(k, t, blk, to, src=None):
            return pltpu.make_async_remote_copy(
                src_ref=part(t, *blk) if src is None else src, dst_ref=part(t, *blk),
                send_sem=send_sems.at[k, t], recv_sem=recv_sems.at[k, t],
                device_id=to, device_id_type=MESH_IDS)

        mine = [pltpu.make_async_copy(ins[t], part(t, *me), local_sems.at[t]) for t in range(n)]
        for cp in mine:
            cp.start()
        first = [copy(0, t, me, sibling, src=ins[t]) for t in range(n)]
        first += [copy(1 + j, t, me, (*chip, c), src=ins[t]) for j, chip in enumerate(chips) for t in range(n)]
        for cp in first:
            cp.start()
        passed = []
        for j, chip in enumerate(chips):
            for t in range(n):
                copy(1 + j, t, (*chip, c), me).wait_recv()
                passed.append(copy(4 + j, t, (*chip, c), sibling))
                passed[-1].start()
        for t in range(n):
            copy(0, t, sibling, me).wait_recv()
        for j, chip in enumerate(chips):
            for t in range(n):
                copy(4 + j, t, (*chip, 1 - c), me).wait_recv()
        for cp in first + passed:
            cp.wait_send()
        for cp in mine:
            cp.wait()

    any_space = pl.BlockSpec(memory_space=pl.ANY)
    return pl.pallas_call(
        body, name="all_gather",
        out_shape=tuple(jax.ShapeDtypeStruct(_gathered_shape(a.shape, k), a.dtype) for a, k in zip(shards, kinds)),
        in_specs=[any_space] * n,
        out_specs=tuple([any_space] * n),
        scratch_shapes=[pltpu.SemaphoreType.DMA((7, n)), pltpu.SemaphoreType.DMA((7, n)),
                        pltpu.SemaphoreType.DMA((n,))],
    )(*shards)


class _Exchange:
    def __init__(self, arrays, kinds, shard_shapes, gather):
        self.arrays, self.kinds, self.shard_shapes, self.gather = list(arrays), list(kinds), list(shard_shapes), gather
        self.n = n = len(self.arrays)
        self.any_specs = [pl.BlockSpec(memory_space=pl.ANY)] * n
        if gather:
            shapes = [_gathered_shape(a.shape, k) for a, k in zip(self.arrays, kinds)]
        else:
            shapes = [(N_DEV,) + tuple(s) for s in shard_shapes]
        self.out_shapes = [jax.ShapeDtypeStruct(s, a.dtype) for s, a in zip(shapes, self.arrays)]
        self.scratch = [pltpu.SemaphoreType.DMA((N_DEV - 1, n)), pltpu.SemaphoreType.DMA((N_DEV - 1, n)),
                        pltpu.SemaphoreType.DMA((n,))]

    def copies(self, ins, outs, sems):
        send_sems, recv_sems, local_sems = sems
        x, y, c = lax.axis_index("x"), lax.axis_index("y"), lax.axis_index("c")
        my = 4 * x + 2 * y + c

        def src(t, receiver):
            if self.gather or self.kinds[t] == "all":
                return ins[t]
            return _window(ins[t], self.kinds[t], receiver, *self.shard_shapes[t])

        def dst(t, sender):
            if self.gather:
                return _window(outs[t], self.kinds[t], sender, *self.shard_shapes[t])
            return outs[t].at[sender]

        local = [pltpu.make_async_copy(src(t, my), dst(t, my), local_sems.at[t]) for t in range(self.n)]
        sends, arrivals = [], []
        for rel in range(1, N_DEV):
            px, py, pc = x ^ (rel >> 2), y ^ ((rel >> 1) & 1), c ^ (rel & 1)
            peer = 4 * px + 2 * py + pc
            for t in range(self.n):
                common = dict(send_sem=send_sems.at[rel - 1, t], recv_sem=recv_sems.at[rel - 1, t],
                              device_id=(px, py, pc), device_id_type=MESH_IDS)
                sends.append(pltpu.make_async_remote_copy(src_ref=src(t, peer), dst_ref=dst(t, my), **common))
                arrivals.append(pltpu.make_async_remote_copy(src_ref=src(t, my), dst_ref=dst(t, peer), **common))
        return local, sends, arrivals

    def start(self, ins, outs, sems):
        local, sends, _ = self.copies(ins, outs, sems)
        for cp in local + sends:
            cp.start()

    def wait(self, ins, outs, sems):
        local, sends, arrivals = self.copies(ins, outs, sems)
        for cp in arrivals:
            cp.wait_recv()
        for cp in sends:
            cp.wait_send()
        for cp in local:
            cp.wait()


def _exchange(grads, kinds, shard_shapes):
    ex = _Exchange(grads, kinds, shard_shapes, gather=False)
    n = ex.n

    def body(*refs):
        ins, outs, sems = refs[:n], refs[n:2 * n], refs[2 * n:]
        ex.start(ins, outs, sems)
        ex.wait(ins, outs, sems)

    return pl.pallas_call(
        body, name="grad_exchange",
        out_shape=tuple(ex.out_shapes),
        in_specs=ex.any_specs,
        out_specs=tuple(ex.any_specs),
        scratch_shapes=ex.scratch,
    )(*grads)


def _sum_adamw(parts, w, m, v, name):
    rows, cols = w.shape
    n, rows_p, cols_p = parts.shape
    tr = _tile(rows, BLK, SUBLANES) if rows > BLK else rows
    tp = tr if rows_p == rows else rows_p
    c1 = 1.0 - ADAM_B1 ** ADAM_STEP
    c2 = 1.0 - ADAM_B2 ** ADAM_STEP

    def body(p_ref, w_ref, m_ref, v_ref, g_ref, d_ref, nm_ref, nv_ref):
        gv = p_ref[0, 0:tr, 0:cols].astype(F32)
        for s in range(1, n):
            gv = gv + p_ref[s, 0:tr, 0:cols].astype(F32)
        g_ref[...] = gv
        nm = ADAM_B1 * m_ref[...] + (1.0 - ADAM_B1) * gv
        nv = ADAM_B2 * v_ref[...] + (1.0 - ADAM_B2) * (gv * gv)
        m_hat = nm / c1
        v_hat = nv / c2
        d_ref[...] = -ADAM_LR * (m_hat / (jnp.sqrt(v_hat) + ADAM_EPS) + ADAM_WD * w_ref[...])
        nm_ref[...] = nm
        nv_ref[...] = nv

    spec = pl.BlockSpec((tr, cols), lambda i: (i, 0))
    out = jax.ShapeDtypeStruct((rows, cols), F32)
    return pl.pallas_call(
        body, name=name,
        out_shape=(out, out, out, out),
        grid=(rows // tr,),
        in_specs=[pl.BlockSpec((n, tp, cols_p), lambda i: (0, i, 0)), spec, spec, spec],
        out_specs=(spec, spec, spec, spec),
        compiler_params=_params(("parallel",)),
    )(parts, w, m, v)


def _pad2(a, rows, cols):
    return jnp.pad(a, ((0, rows - a.shape[0]), (0, cols - a.shape[1])))


WEIGHTS = ["meta_tokens", "norm_gains", "w_in", "b_forget", "w_o_fox", "w_o_sb", "w_out", "w_up", "conv_w",
           "conv_b", "w_down"]


def kernel(x, meta_tokens, norm_gains, w_in, b_forget, w_o_fox, w_o_sb, w_out, w_up, conv_w, conv_b, w_down, loss_target, m_meta_tokens, m_norm_gains, m_w_in, m_b_forget, m_w_o_fox, m_w_o_sb, m_w_out, m_w_up, m_conv_w, m_conv_b, m_w_down, v_meta_tokens, v_norm_gains, v_w_in, v_b_forget, v_w_o_fox, v_w_o_sb, v_w_out, v_w_up, v_conv_w, v_conv_b, v_w_down):
    w = dict(meta_tokens=meta_tokens, norm_gains=norm_gains, w_in=w_in, b_forget=b_forget, w_o_fox=w_o_fox,
             w_o_sb=w_o_sb, w_out=w_out, w_up=w_up, conv_w=conv_w, conv_b=conv_b, w_down=w_down)
    mom = dict(meta_tokens=m_meta_tokens, norm_gains=m_norm_gains, w_in=m_w_in, b_forget=m_b_forget,
               w_o_fox=m_w_o_fox, w_o_sb=m_w_o_sb, w_out=m_w_out, w_up=m_w_up, conv_w=m_conv_w, conv_b=m_conv_b,
               w_down=m_w_down)
    vel = dict(meta_tokens=v_meta_tokens, norm_gains=v_norm_gains, w_in=v_w_in, b_forget=v_b_forget,
               w_o_fox=v_w_o_fox, w_o_sb=v_w_o_sb, w_out=v_w_out, w_up=v_w_up, conv_w=v_conv_w, conv_b=v_conv_b,
               w_down=v_w_down)
    w2 = {n: a.reshape(a.shape[-2:]) for n, a in w.items()}
    shard_shape = {n: a.shape for n, a in w2.items()}

    d = x.shape[-1]
    up_cols = shard_shape["w_up"][1]
    up_pad = -(-up_cols // LANES) * LANES
    half = N_DEV // 2
    pad_rows = lambda a: _pad2(a, SUBLANES, a.shape[1])

    shards = [
        ("w_in", "slots", w2["w_in"].astype(BF16)),
        ("meta_tokens", "cols", w2["meta_tokens"]),
        ("norm_gains", "cols", pad_rows(w2["norm_gains"])),
    ]
    later = [
        ("w_o_fox", "cols", w2["w_o_fox"].astype(BF16)),
        ("w_o_sb", "cols", w2["w_o_sb"].astype(BF16)),
        ("w_out", "rows", w2["w_out"].astype(BF16)),
        ("w_up", "gate_value", _pad2(w2["w_up"], d, up_pad).astype(BF16)),
        ("conv_w", "gate_value", _pad2(w2["conv_w"], SUBLANES, up_pad)),
        ("w_down", "rows", w2["w_down"].astype(BF16)),
    ]
    full = dict(zip([s[0] for s in shards], _all_gather([s[2] for s in shards], [s[1] for s in shards])))
    w_in_full = jnp.concatenate([full["w_in"][i] for i in range(N_DEV)], axis=1)
    conv_b_p = jnp.pad(w2["conv_b"].reshape(2, half, up_cols), ((0, 0), (0, 0), (0, up_pad - up_cols)))
    conv_b_p = conv_b_p.transpose(1, 0, 2)

    def finish_gather(gathered):
        w_fox, w_sb, w_out_full, w_up_p, conv_w_p, w_down_full = gathered
        w_down_p = jnp.pad(w_down_full.reshape(half, up_cols, d), ((0, 0), (0, up_pad - up_cols), (0, 0)))
        return w_fox, w_sb, w_out_full, w_up_p, conv_w_p[:3], w_down_p.reshape(half * up_pad, d)

    early_names = ["w_o_fox", "w_o_sb", "w_out", "w_up", "conv_w", "conv_b", "w_down"]

    def early_exchange(g):
        sends = {
            "w_o_fox": ("cols", g["w_o_fox"], shard_shape["w_o_fox"]),
            "w_o_sb": ("cols", g["w_o_sb"], shard_shape["w_o_sb"]),
            "w_out": ("rows", g["w_out"], shard_shape["w_out"]),
            "w_up": ("gate_value", g["w_up"], (d, up_pad)),
            "conv_w": ("gate_value", pad_rows(g["conv_w"]), (SUBLANES, up_pad)),
            "conv_b": ("all", pad_rows(g["conv_b"].reshape(half, 2, up_pad).transpose(1, 0, 2)[:, :, :up_cols]
                                       .reshape(1, -1)),
                       (SUBLANES, N_DEV * up_cols)),
            "w_down": ("rows", g["w_down"].reshape(half, up_pad, d)[:, :up_cols].reshape(half * up_cols, d),
                       shard_shape["w_down"]),
        }
        return _Exchange([sends[n][1] for n in early_names], [sends[n][0] for n in early_names],
                         [sends[n][2] for n in early_names], gather=False)

    def w_in_exchange(g):
        in_cols = shard_shape["w_in"][1]
        return _Exchange([jnp.stack([g[:, i * in_cols:(i + 1) * in_cols] for i in range(N_DEV)])], ["slots"],
                         [shard_shape["w_in"]], gather=False)

    late_weights = (_Exchange([s[2] for s in later], [s[1] for s in later], [s[2].shape for s in later], gather=True),
                    finish_gather)
    loss, grad_x, grads = _local_step(
        x[0], loss_target[0], full["meta_tokens"], full["norm_gains"][:4], w_in_full, w2["b_forget"],
        None, None, None, None, None, conv_b_p.reshape(1, N_DEV * up_pad), None, ffn_block=up_pad,
        late_weights=late_weights, early_grads=early_exchange, w_in_grad=w_in_exchange)
    loss = lax.psum(loss, ("x", "y", "c"))

    late_names = ["meta_tokens", "norm_gains", "b_forget"]
    sends = {
        "meta_tokens": ("cols", grads["meta_tokens"], (N_META, LANES)),
        "norm_gains": ("cols", pad_rows(grads["norm_gains"]), (SUBLANES, LANES)),
        "b_forget": ("all", _pad2(grads["b_forget"], SUBLANES, LANES), (SUBLANES, LANES)),
    }
    parts = dict(zip(late_names, _exchange([sends[n][1] for n in late_names], [sends[n][0] for n in late_names],
                                           [sends[n][2] for n in late_names])))
    parts.update(zip(early_names, grads["early_parts"]))
    parts["w_in"], = grads["w_in_parts"]

    grad, delta, new_m, new_v = {}, {}, {}, {}
    for n in WEIGHTS:
        shape = w[n].shape
        outs = _sum_adamw(parts[n], w2[n], mom[n].reshape(shard_shape[n]), vel[n].reshape(shard_shape[n]),
                          "adamw_" + n)
        grad[n], delta[n], new_m[n], new_v[n] = (o.reshape(shape) for o in outs)

    return (loss, grad_x[None], *[grad[n] for n in WEIGHTS], *[delta[n] for n in WEIGHTS],
            *[new_m[n] for n in WEIGHTS], *[new_v[n] for n in WEIGHTS])
```

```python
import functools
import math

import jax
import jax.numpy as jnp
from jax import lax
from jax.experimental import pallas as pl
from jax.experimental.pallas import tpu as pltpu

F32 = jnp.float32
BF16 = jnp.bfloat16

N_DEV = 8
N_META = 16
HEAD_DIM = 64
N_HEADS = 8
WIDTH = N_HEADS * HEAD_DIM
N_PAIRS = N_HEADS // 2
LANES = 128
SUBLANES = 8
EPS = 1e-6
ATT_SCALE = HEAD_DIM ** -0.5
BLK = 256
F_PAD = 256
VMEM_LIMIT = 48 << 20

ADAM_LR = 0.001
ADAM_B1 = 0.9
ADAM_B2 = 0.999
ADAM_EPS = 1e-08
ADAM_WD = 0.01
ADAM_STEP = 10

GELU_C = math.sqrt(2.0 / math.pi)
GELU_A = 0.044715
EXP_IS_ZERO = -110.0
NOT_VISITED = -1e30


def _params(sem, vmem=VMEM_LIMIT):
    return pltpu.CompilerParams(dimension_semantics=sem, vmem_limit_bytes=vmem)


def _tile(dim, cap, align=LANES):
    t = (min(cap, dim) // align) * align
    while t >= align:
        if dim % t == 0:
            return t
        t -= align
    return dim


def _log_sigmoid_parts(z):
    lp = jnp.log1p(jnp.exp(-jnp.abs(z)))
    return jnp.minimum(z, 0.0) - lp, jnp.minimum(-z, 0.0) - lp


def _sigmoid(x):
    return 1.0 / (1.0 + jnp.exp(-x))


def _split_bf16(x):
    hi = x.astype(BF16)
    lo = (x - hi.astype(F32)).astype(BF16)
    return hi, lo


def _dot(a, b, dims):
    return lax.dot_general(a, b, (dims, ((), ())), preferred_element_type=F32)


NN = ((1,), (0,))
NT = ((1,), (1,))
TN = ((0,), (0,))


def _mm(a, b, mode, out_dtype, name, exchange=None):
    if mode == "nn":
        (m, kc), (_, n) = a.shape, b.shape
    elif mode == "nt":
        (m, kc), (n, _) = a.shape, b.shape
    else:
        (kc, m), (_, n) = a.shape, b.shape
    if mode == "tn":
        tm, tn, tk = _tile(m, 1024), _tile(n, 1792), _tile(kc, 1408)
    else:
        tm, tn, tk = _tile(m, 1408), _tile(n, 1024), _tile(kc, 1536)
    nk = kc // tk
    dims = {"nn": NN, "nt": NT, "tn": TN}[mode]
    grid = (n // tn, m // tm, nk)
    split, first_step, last_step = _riding(exchange, 2, 1, 1 if nk > 1 else 0, grid)

    def body(*refs):
        (a_ref, b_ref, o_ref, *scratch), ex_refs = split(refs)
        first_step(ex_refs)
        k = pl.program_id(2)
        part = _dot(a_ref[...].astype(BF16), b_ref[...].astype(BF16), dims)
        if nk == 1:
            o_ref[...] = part.astype(out_dtype)
        else:
            acc_ref, = scratch

            @pl.when(k == 0)
            def _():
                acc_ref[...] = part

            @pl.when(k > 0)
            def _():
                acc_ref[...] += part

            @pl.when(k == nk - 1)
            def _():
                o_ref[...] = acc_ref[...].astype(out_dtype)
        last_step(ex_refs)

    if mode == "tn":
        a_spec = pl.BlockSpec((tk, tm), lambda j, i, k: (k, i))
    else:
        a_spec = pl.BlockSpec((tm, tk), lambda j, i, k: (i, k))
    if mode == "nt":
        b_spec = pl.BlockSpec((tn, tk), lambda j, i, k: (j, k))
    else:
        b_spec = pl.BlockSpec((tk, tn), lambda j, i, k: (k, j))
    ex = exchange
    out = pl.pallas_call(
        body, name=name,
        out_shape=(jax.ShapeDtypeStruct((m, n), out_dtype),) + tuple(ex.out_shapes if ex else ()),
        grid=grid,
        in_specs=[a_spec, b_spec] + (ex.any_specs if ex else []),
        out_specs=(pl.BlockSpec((tm, tn), lambda j, i, k: (i, j)),) + tuple(ex.any_specs if ex else ()),
        scratch_shapes=([pltpu.VMEM((tm, tn), F32)] if nk > 1 else []) + (ex.scratch if ex else []),
        compiler_params=_params(("arbitrary",) * 3 if ex else ("parallel", "parallel", "arbitrary")),
    )(a, b, *(ex.arrays if ex else []))
    return out if ex else out[0]


def _shifted_rows(cur_ref, prev_ref, first=None):
    head = prev_ref[...] if first is None else jnp.where(pl.program_id(0) == 0, first, prev_ref[...])
    return jnp.concatenate([head, cur_ref[0:BLK - N_META, :]], axis=0)


def _shifted_specs(s, d):
    per = BLK // N_META
    return (pl.BlockSpec((BLK, d), lambda i: (jnp.minimum(i, s // BLK - 1), 0)),
            pl.BlockSpec((N_META, d), lambda i: (jnp.maximum(per * i - 1, 0), 0)))


def _first_norm_fwd(x, meta, g, lp):
    s, d = x.shape
    n_valid = N_META + s

    def body(x_ref, p_ref, m_ref, g_ref, h_ref, o_ref):
        row = pl.program_id(0) * BLK + lax.broadcasted_iota(jnp.int32, (BLK, 1), 0)
        h = jnp.where(row < n_valid, _shifted_rows(x_ref, p_ref, m_ref[...]), 0.0)
        h_ref[...] = h
        r = lax.rsqrt(jnp.mean(h * h, axis=-1, keepdims=True) + EPS)
        o_ref[...] = ((h * r) * g_ref[...]).astype(BF16)

    row_d = pl.BlockSpec((BLK, d), lambda i: (i, 0))
    return pl.pallas_call(
        body, name="norm1_fwd",
        out_shape=(jax.ShapeDtypeStruct((lp, d), F32), jax.ShapeDtypeStruct((lp, d), BF16)),
        grid=(lp // BLK,),
        in_specs=[*_shifted_specs(s, d), pl.BlockSpec((N_META, d), lambda i: (0, 0)),
                  pl.BlockSpec((1, d), lambda i: (0, 0))],
        out_specs=(row_d, row_d),
        compiler_params=_params(("parallel",)),
    )(x, x, meta, g)


def _first_norm_bwd(h0, g, dy, resid, s):
    lp, d = h0.shape
    nb = lp // BLK
    assert nb == s // BLK + 1

    def body(x_ref, g_ref, dy_ref, r_ref, gx_ref, gm_ref, dg_ref, keep_ref):
        i = pl.program_id(0)
        xv = x_ref[...]
        dyv = dy_ref[...]
        r = lax.rsqrt(jnp.mean(xv * xv, axis=-1, keepdims=True) + EPS)
        xh = xv * r
        dyg = dyv * g_ref[...]
        dx = r * (dyg - xh * jnp.mean(dyg * xh, axis=-1, keepdims=True)) + r_ref[...]

        @pl.when(i == 0)
        def _():
            dg_ref[...] = jnp.zeros_like(dg_ref)
            keep_ref[...] = jnp.zeros_like(keep_ref)
            gm_ref[...] = dx[0:N_META, :]

        gx_ref[...] = jnp.concatenate([keep_ref[...], dx[0:N_META, :]], axis=0)
        keep_ref[...] = dx[N_META:BLK, :]
        dg_ref[...] += jnp.sum(dyv * xh, axis=0, keepdims=True)

    row = pl.BlockSpec((BLK, d), lambda i: (i, 0))
    vec = pl.BlockSpec((1, d), lambda i: (0, 0))
    return pl.pallas_call(
        body, name="norm1_bwd",
        out_shape=(jax.ShapeDtypeStruct((s, d), F32), jax.ShapeDtypeStruct((N_META, d), F32),
                   jax.ShapeDtypeStruct((1, d), F32)),
        grid=(nb,),
        in_specs=[row, vec, row, row],
        out_specs=(pl.BlockSpec((BLK, d), lambda i: (jnp.maximum(i - 1, 0), 0)),
                   pl.BlockSpec((N_META, d), lambda i: (0, 0)), vec),
        scratch_shapes=[pltpu.VMEM((BLK - N_META, d), F32)],
        compiler_params=_params(("arbitrary",)),
    )(h0, g, dy, resid)


def _rmsnorm_bwd(x, g, dy, resid, out_dtype, name):
    lp, d = x.shape
    has_resid = resid is not None

    def body(*refs):
        if has_resid:
            x_ref, g_ref, dy_ref, r_ref, dx_ref, dg_ref = refs
        else:
            x_ref, g_ref, dy_ref, dx_ref, dg_ref = refs
        i = pl.program_id(0)
        xv = x_ref[...]
        dyv = dy_ref[...].astype(F32)
        r = lax.rsqrt(jnp.mean(xv * xv, axis=-1, keepdims=True) + EPS)
        xh = xv * r
        dyg = dyv * g_ref[...]
        dx = r * (dyg - xh * jnp.mean(dyg * xh, axis=-1, keepdims=True))
        if has_resid:
            dx = dx + r_ref[...]
        dx_ref[...] = dx.astype(out_dtype)

        @pl.when(i == 0)
        def _():
            dg_ref[...] = jnp.zeros_like(dg_ref)

        dg_ref[...] += jnp.sum(dyv * xh, axis=0, keepdims=True)

    row = pl.BlockSpec((BLK, d), lambda i: (i, 0))
    vec = pl.BlockSpec((1, d), lambda i: (0, 0))
    ins = [x, g, dy] + ([resid] if has_resid else [])
    in_specs = [row, vec, row] + ([row] if has_resid else [])
    return pl.pallas_call(
        body, name=name,
        out_shape=(jax.ShapeDtypeStruct((lp, d), out_dtype), jax.ShapeDtypeStruct((1, d), F32)),
        grid=(lp // BLK,),
        in_specs=in_specs,
        out_specs=(row, vec),
        compiler_params=_params(("arbitrary",)),
    )(*ins)


def _forget_fwd(fpre, b_pad):
    lp = fpre.shape[0]

    def body(f_ref, b_ref, c_ref, carry_ref):
        i = pl.program_id(0)

        @pl.when(i == 0)
        def _():
            carry_ref[...] = jnp.zeros_like(carry_ref)

        logf, _ = _log_sigmoid_parts(f_ref[...] + b_ref[...])
        row = lax.broadcasted_iota(jnp.int32, (BLK, BLK), 0)
        col = lax.broadcasted_iota(jnp.int32, (BLK, BLK), 1)
        tri = (col <= row).astype(BF16)
        p0 = logf.astype(BF16)
        r1 = logf - p0.astype(F32)
        p1 = r1.astype(BF16)
        p2 = (r1 - p1.astype(F32)).astype(BF16)
        c = _dot(tri, p0, NN) + _dot(tri, p1, NN) + _dot(tri, p2, NN) + carry_ref[0:1, :]
        c_ref[...] = c
        carry_ref[...] = jnp.broadcast_to(c[BLK - 1:BLK, :], carry_ref.shape)

    return pl.pallas_call(
        body, name="forget_fwd",
        out_shape=jax.ShapeDtypeStruct((lp, LANES), F32),
        grid=(lp // BLK,),
        in_specs=[pl.BlockSpec((BLK, LANES), lambda i: (i, 0)), pl.BlockSpec((1, LANES), lambda i: (0, 0))],
        out_specs=pl.BlockSpec((BLK, LANES), lambda i: (i, 0)),
        scratch_shapes=[pltpu.VMEM((SUBLANES, LANES), F32)],
        compiler_params=_params(("arbitrary",)),
    )(fpre, b_pad)


def _forget_bwd(dc, fpre, b_pad):
    lp = fpre.shape[0]
    nb = lp // BLK

    def body(dc_ref, f_ref, b_ref, df_ref, db_ref, carry_ref):
        i = pl.program_id(0)

        @pl.when(i == 0)
        def _():
            carry_ref[...] = jnp.zeros_like(carry_ref)
            db_ref[...] = jnp.zeros_like(db_ref)

        dcv = dc_ref[...]
        row = lax.broadcasted_iota(jnp.int32, (BLK, BLK), 0)
        col = lax.broadcasted_iota(jnp.int32, (BLK, BLK), 1)
        tri = (col >= row).astype(BF16)
        p0 = dcv.astype(BF16)
        r1 = dcv - p0.astype(F32)
        p1 = r1.astype(BF16)
        p2 = (r1 - p1.astype(F32)).astype(BF16)
        dlogf = _dot(tri, p0, NN) + _dot(tri, p1, NN) + _dot(tri, p2, NN) + carry_ref[0:1, :]
        carry_ref[...] = jnp.broadcast_to(dlogf[0:1, :], carry_ref.shape)
        _, ls_neg = _log_sigmoid_parts(f_ref[...] + b_ref[...])
        df = dlogf * jnp.exp(ls_neg)
        df_ref[...] = df
        db_ref[...] += jnp.sum(df, axis=0, keepdims=True)

    rev = pl.BlockSpec((BLK, LANES), lambda i: (nb - 1 - i, 0))
    vec = pl.BlockSpec((1, LANES), lambda i: (0, 0))
    return pl.pallas_call(
        body, name="forget_bwd",
        out_shape=(jax.ShapeDtypeStruct((lp, LANES), F32), jax.ShapeDtypeStruct((1, LANES), F32)),
        grid=(nb,),
        in_specs=[rev, rev, vec],
        out_specs=(rev, vec),
        scratch_shapes=[pltpu.VMEM((SUBLANES, LANES), F32)],
        compiler_params=_params(("arbitrary",)),
    )(dc, fpre, b_pad)


def _mix_fwd(o_a, o_b, gates, h0, w_fox, w_sb, w_out, g1, g2):
    lp, d = h0.shape

    def body(oa_ref, ob_ref, ga_ref, gb_ref, h_ref, wf_ref, ws_ref, wo_ref, g_ref, g2_ref,
             h1_ref, gated_ref, mixed_ref, xn_ref):
        ya = _dot(oa_ref[...].astype(BF16), wf_ref[...], NN)
        yb = _dot(ob_ref[...].astype(BF16), ws_ref[...], NN)
        gated = _sigmoid(ga_ref[...]) * ya + _sigmoid(gb_ref[...]) * yb
        gb16 = gated.astype(BF16)
        mixed = _dot(gb16, wo_ref[...], NN)
        r = lax.rsqrt(jnp.mean(mixed * mixed, axis=-1, keepdims=True) + EPS)
        h1 = h_ref[...] + (mixed * r) * g_ref[...]
        h1_ref[...] = h1
        r2 = lax.rsqrt(jnp.mean(h1 * h1, axis=-1, keepdims=True) + EPS)
        xn_ref[...] = ((h1 * r2) * g2_ref[...]).astype(BF16)
        gated_ref[...] = gb16
        mixed_ref[...] = mixed

    row_w = pl.BlockSpec((BLK, WIDTH), lambda i: (i, 0))
    row_d = pl.BlockSpec((BLK, d), lambda i: (i, 0))
    full = lambda s: pl.BlockSpec(s, lambda i: (0, 0))
    return pl.pallas_call(
        body, name="mix_fwd",
        out_shape=(jax.ShapeDtypeStruct((lp, d), F32), jax.ShapeDtypeStruct((lp, d), BF16),
                   jax.ShapeDtypeStruct((lp, d), F32), jax.ShapeDtypeStruct((lp, d), BF16)),
        grid=(lp // BLK,),
        in_specs=[row_w, row_w, row_d, pl.BlockSpec((BLK, d), lambda i: (i, 1)), row_d,
                  full((WIDTH, d)), full((WIDTH, d)), full((d, d)), full((1, d)), full((1, d))],
        out_specs=(row_d, row_d, row_d, row_d),
        compiler_params=_params(("parallel",)),
    )(o_a, o_b, gates, gates, h0, w_fox, w_sb, w_out, g1, g2)


def _gate_bwd(d_mixed, gates, o_a, o_b, w_fox, w_sb, w_out):
    lp, d = d_mixed.shape

    def body(dm_ref, ga_ref, gb_ref, oa_ref, ob_ref, wf_ref, ws_ref, wo_ref, dya_ref, dyb_ref, dga_ref, dgb_ref):
        dg = _dot(dm_ref[...], wo_ref[...], NT)
        ya = _dot(oa_ref[...].astype(BF16), wf_ref[...], NN)
        yb = _dot(ob_ref[...].astype(BF16), ws_ref[...], NN)
        sa = _sigmoid(ga_ref[...])
        sb = _sigmoid(gb_ref[...])
        dya_ref[...] = (dg * sa).astype(BF16)
        dyb_ref[...] = (dg * sb).astype(BF16)
        dga_ref[...] = (dg * ya * (sa * (1.0 - sa))).astype(BF16)
        dgb_ref[...] = (dg * yb * (sb * (1.0 - sb))).astype(BF16)

    row = pl.BlockSpec((BLK, d), lambda i: (i, 0))
    row_w = pl.BlockSpec((BLK, WIDTH), lambda i: (i, 0))
    full = lambda s: pl.BlockSpec(s, lambda i: (0, 0))
    out = jax.ShapeDtypeStruct((lp, d), BF16)
    return pl.pallas_call(
        body, name="gate_bwd",
        out_shape=(out, out, out, out),
        grid=(lp // BLK,),
        in_specs=[row, row, pl.BlockSpec((BLK, d), lambda i: (i, 1)), row_w, row_w,
                  full((WIDTH, d)), full((WIDTH, d)), full((d, d))],
        out_specs=(row, row, row, row),
        compiler_params=_params(("parallel",)),
    )(d_mixed, gates, gates, o_a, o_b, w_fox, w_sb, w_out)


def _shift_down(cur, prev, n):
    rolled = pltpu.roll(cur, n, 0)
    row = lax.broadcasted_iota(jnp.int32, prev.shape, 0)
    head = jnp.where(row < n, pltpu.roll(prev, n, 0), rolled[0:SUBLANES])
    return head if cur.shape[0] == SUBLANES else jnp.concatenate([head, rolled[SUBLANES:]], axis=0)


def _shift_up(cur, nxt, n):
    rows = cur.shape[0]
    rolled = pltpu.roll(cur, rows - n, 0)
    row = lax.broadcasted_iota(jnp.int32, nxt.shape, 0)
    tail = jnp.where(row >= SUBLANES - n, pltpu.roll(nxt, SUBLANES - n, 0), rolled[rows - SUBLANES:])
    return tail if rows == SUBLANES else jnp.concatenate([rolled[:rows - SUBLANES], tail], axis=0)


def _gelu(x):
    return 0.5 * x * (1.0 + jnp.tanh(GELU_C * (x + GELU_A * (x * x * x))))


def _gelu_and_grad(x):
    t = jnp.tanh(GELU_C * (x + GELU_A * (x * x * x)))
    half = 0.5 * (1.0 + t)
    return x * half, half + 0.5 * x * (1.0 - t * t) * (GELU_C * (1.0 + 3.0 * GELU_A * (x * x)))


def _conv_taps(cur, prev, w_ref, b_ref):
    s1 = _shift_down(cur, prev, 1)
    s2 = _shift_down(cur, prev, 2)
    u = b_ref[...] + w_ref[0:1, :] * s2
    u = u + w_ref[1:2, :] * s1
    u = u + w_ref[2:3, :] * cur
    return u, s1, s2


def _conv_gelu_fwd(up, conv_w, conv_b, tc):
    lp, f2 = up.shape
    rb = BLK // SUBLANES

    def body(u_ref, p_ref, w_ref, b_ref, act_ref):
        i = pl.program_id(0)
        keep = (i > 0).astype(F32)
        u, _, _ = _conv_taps(u_ref[...], p_ref[...] * keep, w_ref, b_ref)
        act_ref[...] = (_gelu(u[:, :tc]) * u[:, tc:]).astype(BF16)

    prev_row = lambda i: jnp.maximum(i * rb - 1, 0)
    return pl.pallas_call(
        body, name="conv_gelu_fwd",
        out_shape=jax.ShapeDtypeStruct((lp, f2 // 2), BF16),
        grid=(lp // BLK, f2 // (2 * tc)),
        in_specs=[pl.BlockSpec((BLK, 2 * tc), lambda i, j: (i, j)),
                  pl.BlockSpec((SUBLANES, 2 * tc), lambda i, j: (prev_row(i), j)),
                  pl.BlockSpec((3, 2 * tc), lambda i, j: (0, j)),
                  pl.BlockSpec((1, 2 * tc), lambda i, j: (0, j))],
        out_specs=pl.BlockSpec((BLK, tc), lambda i, j: (i, j)),
        compiler_params=_params(("parallel", "parallel")),
    )(up, up, conv_w, conv_b)


def _conv_gelu_bwd(up, d_act, conv_w, conv_b, tc):
    lp, f2 = up.shape
    nb = lp // BLK
    rb = BLK // SUBLANES

    def du_of(u, da):
        gel, grad = _gelu_and_grad(u[:, :tc])
        return jnp.concatenate([da * u[:, tc:] * grad, da * gel], axis=1)

    def body(u_ref, p_ref, n_ref, da_ref, dan_ref, w_ref, b_ref, dup_ref, dcw_ref, dcb_ref):
        i = pl.program_id(1)
        cur = u_ref[...]
        u, s1, s2 = _conv_taps(cur, p_ref[...] * (i > 0).astype(F32), w_ref, b_ref)
        du = du_of(u, da_ref[...])
        u_next, _, _ = _conv_taps(n_ref[...], cur[BLK - SUBLANES:BLK, :], w_ref, b_ref)
        du_next = du_of(u_next, dan_ref[...]) * (i < nb - 1).astype(F32)
        n1 = _shift_up(du, du_next, 1)
        n2 = _shift_up(du, du_next, 2)
        dup_ref[...] = (w_ref[2:3, :] * du + w_ref[1:2, :] * n1 + w_ref[0:1, :] * n2).astype(BF16)

        @pl.when(i == 0)
        def _():
            dcw_ref[...] = jnp.zeros_like(dcw_ref)
            dcb_ref[...] = jnp.zeros_like(dcb_ref)

        dcw_ref[0:1, :] += jnp.sum(du * s2, axis=0, keepdims=True)
        dcw_ref[1:2, :] += jnp.sum(du * s1, axis=0, keepdims=True)
        dcw_ref[2:3, :] += jnp.sum(du * cur, axis=0, keepdims=True)
        dcb_ref[...] += jnp.sum(du, axis=0, keepdims=True)

    prev_row = lambda i: jnp.maximum(i * rb - 1, 0)
    next_row = lambda i: jnp.minimum((i + 1) * rb, nb * rb - 1)
    return pl.pallas_call(
        body, name="conv_gelu_bwd",
        out_shape=(jax.ShapeDtypeStruct((lp, f2), BF16), jax.ShapeDtypeStruct((3, f2), F32),
                   jax.ShapeDtypeStruct((1, f2), F32)),
        grid=(f2 // (2 * tc), nb),
        in_specs=[pl.BlockSpec((BLK, 2 * tc), lambda j, i: (i, j)),
                  pl.BlockSpec((SUBLANES, 2 * tc), lambda j, i: (prev_row(i), j)),
                  pl.BlockSpec((SUBLANES, 2 * tc), lambda j, i: (next_row(i), j)),
                  pl.BlockSpec((BLK, tc), lambda j, i: (i, j)),
                  pl.BlockSpec((SUBLANES, tc), lambda j, i: (next_row(i), j)),
                  pl.BlockSpec((3, 2 * tc), lambda j, i: (0, j)),
                  pl.BlockSpec((1, 2 * tc), lambda j, i: (0, j))],
        out_specs=(pl.BlockSpec((BLK, 2 * tc), lambda j, i: (i, j)),
                   pl.BlockSpec((3, 2 * tc), lambda j, i: (0, j)),
                   pl.BlockSpec((1, 2 * tc), lambda j, i: (0, j))),
        compiler_params=_params(("parallel", "arbitrary")),
    )(up, up, up, d_act, d_act, conv_w, conv_b)


def _out_loss(h1, ffn, g3, target):
    lp, d = h1.shape
    s = target.shape[0]
    n_valid = N_META + s

    def body(h_ref, f_ref, g_ref, t_ref, tp_ref, dy_ref, loss_ref, df_ref, dg_ref):
        i = pl.program_id(0)

        @pl.when(i == 0)
        def _():
            loss_ref[...] = jnp.zeros_like(loss_ref)
            dg_ref[...] = jnp.zeros_like(dg_ref)

        fv = f_ref[...]
        r = lax.rsqrt(jnp.mean(fv * fv, axis=-1, keepdims=True) + EPS)
        xh = fv * r
        y = h_ref[...] + xh * g_ref[...]
        row = i * BLK + lax.broadcasted_iota(jnp.int32, (BLK, 1), 0)
        valid = (row >= N_META) & (row < n_valid)
        diff = jnp.where(valid, y - _shifted_rows(t_ref, tp_ref), 0.0)
        dy = diff * (1.0 / d)
        dy_ref[...] = dy
        per_row = jnp.mean(diff * diff, axis=-1, keepdims=True)
        loss_ref[...] += 0.5 * jnp.sum(per_row, axis=0, keepdims=True)
        dyg = dy * g_ref[...]
        df_ref[...] = (r * (dyg - xh * jnp.mean(dyg * xh, axis=-1, keepdims=True))).astype(BF16)
        dg_ref[...] += jnp.sum(dy * xh, axis=0, keepdims=True)

    row_d = pl.BlockSpec((BLK, d), lambda i: (i, 0))
    vec = pl.BlockSpec((1, d), lambda i: (0, 0))
    return pl.pallas_call(
        body, name="out_loss",
        out_shape=(jax.ShapeDtypeStruct((lp, d), F32), jax.ShapeDtypeStruct((SUBLANES, LANES), F32),
                   jax.ShapeDtypeStruct((lp, d), BF16), jax.ShapeDtypeStruct((1, d), F32)),
        grid=(lp // BLK,),
        in_specs=[row_d, row_d, vec, *_shifted_specs(s, d)],
        out_specs=(row_d, pl.BlockSpec((SUBLANES, LANES), lambda i: (0, 0)), row_d, vec),
        compiler_params=_params(("arbitrary",)),
    )(h1, ffn, g3, target, target)


def _head_masks(rows=BLK):
    lane = lax.broadcasted_iota(jnp.int32, (rows, LANES), 1)
    return [lane < HEAD_DIM, lane >= HEAD_DIM]


def _att_specs(base, lp):
    q_spec = pl.BlockSpec((BLK, LANES), lambda p, i: (i, base + p))
    k_spec = pl.BlockSpec((lp, LANES), lambda p, i: (0, base + N_PAIRS + p))
    v_spec = pl.BlockSpec((lp, LANES), lambda p, i: (0, base + 2 * N_PAIRS + p))
    return q_spec, k_spec, v_spec


def _kv_rows(j, nb=1):
    return pl.ds(pl.multiple_of(j * BLK, nb * BLK), nb * BLK)


def _walk_kv(i, tile, reverse=False, first_block=0, more=None):
    if reverse:
        alive = more if more is not None else (lambda: True)
        tile(i, 1, True)

        @pl.when((i >= 1) & alive())
        def _():
            tile(i - 1, 1, False)

        rest = jnp.maximum(i - 1, 0)

        @pl.when((rest % 2 == 1) & alive())
        def _():
            tile(rest - 1, 1, False)

        pairs = rest // 2

        def wide(t):
            tile(2 * (pairs - 1 - t), 2, False)
            return t + 1

        lax.while_loop(lambda c: (c[0] < pairs) & c[1], lambda c: (wide(c[0]), alive()), (0, alive()))
        return

    pairs, odd = i // 2, i % 2

    @pl.when((first_block % 2 == 1) & (first_block < i))
    def _():
        tile(first_block, 1, False)

    def wide(t, carry):
        tile(2 * t, 2, False)
        return carry

    lax.fori_loop((first_block + 1) // 2, pairs, wide, 0)

    @pl.when(odd == 1)
    def _():
        tile(i - 1, 1, False)

    tile(i, 1, True)


def _walk_kv_ahead(i, lead, follow, kept, first_block=0):
    pairs, odd = i // 2, i % 2
    first_pair = (first_block + 1) // 2

    def keep(values):
        for ref, value in zip(kept, values):
            ref[...] = value

    @pl.when((first_block % 2 == 1) & (first_block < i))
    def _():
        follow(lead(first_block, 1), first_block, 1, False)

    @pl.when(pairs > first_pair)
    def _():
        keep(lead(2 * first_pair, 2))

    def wide(t, carry):
        ahead = lead(2 * jnp.minimum(t + 1, pairs - 1), 2)
        follow([ref[...] for ref in kept], 2 * t, 2, False)
        keep(ahead)
        return carry

    lax.fori_loop(first_pair, pairs, wide, 0)

    @pl.when(odd == 1)
    def _():
        follow(lead(i - 1, 1), i - 1, 1, False)

    follow(lead(i, 1), i, 1, True)


def _bf16_pieces(x):
    rnd = lambda a: lax.reduce_precision(a, exponent_bits=8, mantissa_bits=7)
    p0 = rnd(x)
    p1 = rnd(x - p0)
    p2 = rnd(x - p0 - p1)
    return [p0.astype(BF16), p1.astype(BF16), p2.astype(BF16)]


def _aug_lanes(cols):
    lp = cols[0].shape[0]
    vals = jnp.stack([c.astype(BF16) for c in cols], axis=-1)
    vals = vals.reshape(lp, N_PAIRS, 2, len(cols))[:, :, ::-1, :]
    vals = jnp.pad(vals, ((0, 0), (0, 0), (0, 0), (0, HEAD_DIM - len(cols))))
    return vals.reshape(lp, WIDTH)


N_AUG = 3


def _riding(exchange, n_in, n_out, n_scratch, grid):
    n = exchange.n if exchange else 0

    def split(refs):
        own_in, ex_in = refs[:n_in], refs[n_in:n_in + n]
        own_out, ex_out = refs[n_in + n:n_in + n + n_out], refs[n_in + n + n_out:n_in + 2 * n + n_out]
        rest = refs[n_in + 2 * n + n_out:]
        return own_in + own_out + rest[:n_scratch], (ex_in, ex_out, rest[n_scratch:])

    def at(step_of):
        here = pl.program_id(0) == step_of(grid[0])
        for axis in range(1, len(grid)):
            here = here & (pl.program_id(axis) == step_of(grid[axis]))
        return here

    def first_step(ex_refs):
        if exchange:
            @pl.when(at(lambda size: 0))
            def _():
                exchange.start(*ex_refs)

    def last_step(ex_refs):
        if exchange:
            @pl.when(at(lambda size: size - 1))
            def _():
                exchange.wait(*ex_refs)

    return split, first_step, last_step


def _fox_fwd(qkv, qaug, kaug, exchange=None):
    lp = qkv.shape[0]
    nq = lp // BLK
    split, first_step, last_step = _riding(exchange, 5, 2, 3, (N_PAIRS, nq))

    def body(*refs):
        (q_ref, k_ref, v_ref, qa_ref, ka_ref, o_ref, lse_ref, acc_ref, m_ref, s_ref), ex_refs = split(refs)
        first_step(ex_refs)
        i = pl.program_id(1)
        hs = range(2)
        masks = _head_masks()
        qs = q_ref[...] * ATT_SCALE
        qa = qa_ref[...]
        qh = [jnp.where(masks[hh], qs, qa) for hh in hs]
        acc_ref[...] = jnp.zeros_like(acc_ref)
        m_ref[...] = jnp.full_like(m_ref, -1e30)
        row = lax.broadcasted_iota(jnp.int32, (BLK, BLK), 0)
        col = lax.broadcasted_iota(jnp.int32, (BLK, BLK), 1)
        causal = col <= row

        def scores(j, nb):
            rows = _kv_rows(j, nb)
            kmasks = _head_masks(nb * BLK)
            k, ka = k_ref[rows, :], ka_ref[rows, :]
            return [_dot(qh[hh], jnp.where(kmasks[hh], k, ka), NT) for hh in hs]

        def absorb(s, j, nb, diag):
            v = v_ref[_kv_rows(j, nb), :]
            kmasks = _head_masks(nb * BLK)
            vh = [jnp.where(kmasks[hh], v, jnp.ones_like(v)) for hh in hs]
            if diag:
                s = [jnp.where(causal, s[hh], -1e30) for hh in hs]
            m_prev = [m_ref[hh] for hh in hs]
            m_new = [jnp.maximum(m_prev[hh], jnp.max(s[hh], axis=-1, keepdims=True)) for hh in hs]
            p = [jnp.exp(s[hh] - m_new[hh]).astype(BF16) for hh in hs]
            for hh in hs:
                acc_ref[hh] = jnp.exp(m_prev[hh] - m_new[hh]) * acc_ref[hh] + _dot(p[hh], vh[hh], NN)
                m_ref[hh] = m_new[hh]

        _walk_kv_ahead(i, scores, absorb, [s_ref.at[hh] for hh in hs], _first_block(qa, i))
        acc = [acc_ref[hh] for hh in hs]
        denom = [acc[0][:, HEAD_DIM:HEAD_DIM + 1], acc[1][:, 0:1]]
        o_ref[...] = jnp.where(masks[0], acc[0] / denom[0], acc[1] / denom[1])
        lse = jnp.where(masks[0], m_ref[0] + jnp.log(denom[0]), m_ref[1] + jnp.log(denom[1])).T
        lse_ref[0, 0, 0:1, :] = lse[0:1, :]
        lse_ref[0, 0, 1:2, :] = lse[HEAD_DIM:HEAD_DIM + 1, :]
        last_step(ex_refs)

    q_spec, k_spec, v_spec = _att_specs(0, lp)
    blk = pl.BlockSpec((BLK, LANES), lambda p, i: (i, p))
    col_full = pl.BlockSpec((lp, LANES), lambda p, i: (0, p))
    out = jax.ShapeDtypeStruct((lp, WIDTH), F32)
    ex = exchange
    return pl.pallas_call(
        body, name="fox_fwd",
        out_shape=(out, jax.ShapeDtypeStruct((N_PAIRS, nq, 2, BLK), F32)) + tuple(ex.out_shapes if ex else ()),
        grid=(N_PAIRS, nq),
        in_specs=[q_spec, k_spec, v_spec, blk, col_full] + (ex.any_specs if ex else []),
        out_specs=(blk, pl.BlockSpec((1, 1, 2, BLK), lambda p, i: (p, i, 0, 0))) + tuple(ex.any_specs if ex else ()),
        scratch_shapes=[pltpu.VMEM((2, BLK, LANES), F32), pltpu.VMEM((2, BLK, 1), F32),
                        pltpu.VMEM((2, BLK, 2 * BLK), F32)] + (ex.scratch if ex else []),
        compiler_params=_params(("arbitrary", "arbitrary") if ex else ("parallel", "parallel")),
    )(qkv, qkv, qkv, qaug, kaug, *(ex.arrays if ex else []))


FIRST_BLOCK_LANE = 9


def _first_block(qa, i):
    lane = lax.broadcasted_iota(jnp.int32, qa.shape, 1)
    first = jnp.max(jnp.where(lane == HEAD_DIM + FIRST_BLOCK_LANE, qa.astype(F32), 0.0)).astype(jnp.int32)
    return jnp.clip(first, 0, i)


def _fox_first_blocks(qkv, c):
    lp = qkv.shape[0]
    nq = lp // BLK

    def body(x_ref, o_ref):
        xv = x_ref[...].astype(F32)
        sq = (xv * xv).astype(BF16)
        col = lax.broadcasted_iota(jnp.int32, (2 * WIDTH, LANES), 0)
        lane = lax.broadcasted_iota(jnp.int32, (2 * WIDTH, LANES), 1)
        pick = (col // HEAD_DIM == lane).astype(BF16)
        o_ref[...] = _dot(sq, pick, NN)

    norms = pl.pallas_call(
        body, name="fox_norms",
        out_shape=jax.ShapeDtypeStruct((lp, LANES), F32),
        grid=(nq,),
        in_specs=[pl.BlockSpec((BLK, 2 * WIDTH), lambda i: (i, 0))],
        out_specs=pl.BlockSpec((BLK, LANES), lambda i: (i, 0)),
        compiler_params=_params(("parallel",)),
    )(qkv)
    a_max = 1.02 * jnp.sqrt(norms[:, :N_HEADS].reshape(nq, BLK, N_HEADS).max(axis=1))
    b_max = 1.02 * jnp.sqrt(norms[:, N_HEADS:2 * N_HEADS].reshape(nq, BLK, N_HEADS).max(axis=1))
    c_max = c.reshape(nq, BLK, N_HEADS).max(axis=1)
    c_min = c.reshape(nq, BLK, N_HEADS).min(axis=1)
    bound = (a_max[:, None] * (b_max[None, :] + b_max[:, None]) * ATT_SCALE + c_max[:, None] - c_min[None, :])
    alive = (bound > EXP_IS_ZERO) | jnp.isnan(bound)
    alive = alive.reshape(nq, nq, N_PAIRS, 2).any(axis=-1)
    first = jnp.argmax(alive, axis=1).astype(F32)
    return jnp.repeat(jnp.repeat(first, 2, axis=1), BLK, axis=0)


def _head_dots(a, b):
    lp = a.shape[0]

    def body(a_ref, b_ref, o_ref):
        prod = a_ref[...] * b_ref[...]
        col = lax.broadcasted_iota(jnp.int32, (WIDTH, LANES), 0)
        lane = lax.broadcasted_iota(jnp.int32, (WIDTH, LANES), 1)
        pick = (col // HEAD_DIM == lane).astype(BF16)
        hi = prod.astype(BF16)
        mid, lo = _split_bf16(prod - hi.astype(F32))
        o_ref[...] = _dot(hi, pick, NN) + _dot(mid, pick, NN) + _dot(lo, pick, NN)

    row = pl.BlockSpec((BLK, WIDTH), lambda i: (i, 0))
    return pl.pallas_call(
        body, name="head_dots",
        out_shape=jax.ShapeDtypeStruct((lp, LANES), F32),
        grid=(lp // BLK,),
        in_specs=[row, row],
        out_specs=pl.BlockSpec((BLK, LANES), lambda i: (i, 0)),
        compiler_params=_params(("parallel",)),
    )(a, b)


def _fox_bwd(qkv, qaug, kaug, doaug, d_o, exchange=None):
    lp = qkv.shape[0]
    nq = lp // BLK
    split, first_step, last_step = _riding(exchange, 7, 4, 2, (N_PAIRS, nq))

    def body(*refs):
        (q_ref, k_ref, v_ref, qa_ref, ka_ref, da_ref, do_ref,
         dq_ref, dk_ref, dv_ref, dc_ref, acc_ref, rs_ref), ex_refs = split(refs)
        first_step(ex_refs)
        i = pl.program_id(1)
        hs = range(2)
        masks = _head_masks()
        qs = q_ref[...] * ATT_SCALE
        qa = qa_ref[...]
        qm = [jnp.where(masks[hh], qs, 0) for hh in hs]
        qh = [jnp.where(masks[hh], qs, qa) for hh in hs]
        dov = do_ref[...].astype(BF16)
        doa = da_ref[...]
        dom = [jnp.where(masks[hh], dov, 0) for hh in hs]
        doh = [jnp.where(masks[hh], dov, doa) for hh in hs]
        row = lax.broadcasted_iota(jnp.int32, (BLK, BLK), 0)
        col = lax.broadcasted_iota(jnp.int32, (BLK, BLK), 1)
        causal = col <= row

        @pl.when(i == 0)
        def _():
            dk_ref[...] = jnp.zeros_like(dk_ref)
            dv_ref[...] = jnp.zeros_like(dv_ref)
            dc_ref[...] = jnp.zeros_like(dc_ref)

        acc_ref[...] = jnp.zeros_like(acc_ref)
        rs_ref[...] = jnp.zeros_like(rs_ref)

        def lead(j, nb):
            rows = _kv_rows(j, nb)
            kmasks = _head_masks(nb * BLK)
            k, ka, v = k_ref[rows, :], ka_ref[rows, :], v_ref[rows, :]
            ones = (lax.broadcasted_iota(jnp.int32, v.shape, 1) % HEAD_DIM < N_AUG).astype(BF16)
            logp = [_dot(qh[hh], jnp.where(kmasks[hh], k, ka), NT) for hh in hs]
            dp = [_dot(doh[hh], jnp.where(kmasks[hh], v, ones), NT) for hh in hs]
            return logp + dp

        def follow(lead_out, j, nb, diag):
            rows = _kv_rows(j, nb)
            k = k_ref[rows, :]
            logp, dp = lead_out[:2], lead_out[2:]
            p = [jnp.exp(logp[hh]) for hh in hs]
            if diag:
                p = [jnp.where(causal, p[hh], 0.0) for hh in hs]
            ds = [p[hh] * dp[hh] for hh in hs]
            dsb = [ds[hh].astype(BF16) for hh in hs]
            for hh in hs:
                acc_ref[hh] += _dot(dsb[hh], k, NN)
                col_sums = jnp.sum(ds[hh], axis=0, keepdims=True)
                for b in range(nb):
                    dc_ref[0, j + b, hh:hh + 1, :] -= col_sums[:, b * BLK:(b + 1) * BLK]
                rs_ref[hh] += jnp.sum(ds[hh], axis=-1, keepdims=True)
            dk_ref[rows, :] += _dot(dsb[0], qm[0], TN) + _dot(dsb[1], qm[1], TN)
            dv_ref[rows, :] += _dot(p[0].astype(BF16), dom[0], TN) + _dot(p[1].astype(BF16), dom[1], TN)

        _walk_kv(i, lambda j, nb, diag: follow(lead(j, nb), j, nb, diag), first_block=_first_block(qa, i))
        dq_ref[...] = (jnp.where(masks[0], acc_ref[0], acc_ref[1]) * ATT_SCALE).astype(BF16)
        row_sums = jnp.where(masks[0], rs_ref[0], rs_ref[1]).T
        dc_ref[0, i, 0:1, :] += row_sums[0:1, :]
        dc_ref[0, i, 1:2, :] += row_sums[HEAD_DIM:HEAD_DIM + 1, :]
        last_step(ex_refs)

    q_spec, k_spec, v_spec = _att_specs(0, lp)
    blk = pl.BlockSpec((BLK, LANES), lambda p, i: (i, p))
    col_full = pl.BlockSpec((lp, LANES), lambda p, i: (0, p))
    crow_spec = pl.BlockSpec((1, nq, 2, BLK), lambda p, i: (p, 0, 0, 0))
    ex = exchange
    return pl.pallas_call(
        body, name="fox_bwd",
        out_shape=(jax.ShapeDtypeStruct((lp, WIDTH), BF16), jax.ShapeDtypeStruct((lp, WIDTH), F32),
                   jax.ShapeDtypeStruct((lp, WIDTH), F32), jax.ShapeDtypeStruct((N_PAIRS, nq, 2, BLK), F32))
        + tuple(ex.out_shapes if ex else ()),
        grid=(N_PAIRS, nq),
        in_specs=[q_spec, k_spec, v_spec, blk, col_full, blk, blk] + (ex.any_specs if ex else []),
        out_specs=(blk, col_full, col_full, crow_spec) + tuple(ex.any_specs if ex else ()),
        scratch_shapes=[pltpu.VMEM((2, BLK, LANES), F32), pltpu.VMEM((2, BLK, 1), F32)] + (ex.scratch if ex else []),
        compiler_params=_params(("arbitrary", "arbitrary") if ex else ("parallel", "arbitrary")),
    )(qkv, qkv, qkv, qaug, kaug, doaug, d_o, *(ex.arrays if ex else []))


def _sb_scores(z):
    ell = jnp.minimum(z, 0.0) - jnp.log(1.0 + jnp.exp(-jnp.abs(z)))
    return ell, ell - z


def _stacked(tri):
    return jnp.concatenate([tri, tri], axis=0)


def _cumsum_dot(x, tri2):
    hi, lo = _split_bf16(x)
    return _dot(jnp.concatenate([hi, lo], axis=1), tri2, NN)


def _sb_units(qh, k, strict2, causal, nb, diag):
    hs, bs = range(2), range(nb)
    z = [_dot(qh[hh], k, NT) for hh in hs]
    sc = [[_sb_scores(z[hh][:, b * BLK:(b + 1) * BLK]) for b in bs] for hh in hs]
    ell = [[sc[hh][b][0] for b in bs] for hh in hs]
    kap = [[jnp.where(causal, sc[hh][b][1], 0.0) if diag else sc[hh][b][1] for b in bs] for hh in hs]
    later = [[_cumsum_dot(kap[hh][b], strict2) for b in bs] for hh in hs]
    return ell, kap, later


def _row_sum(x):
    return jnp.sum(x, axis=-1, keepdims=True)


def _join(blocks):
    joined = blocks[0] if len(blocks) == 1 else jnp.concatenate(blocks, axis=1)
    return joined.astype(BF16)


def _sb_fwd(qkv):
    lp = qkv.shape[0]
    nq = lp // BLK
    assert nq <= HEAD_DIM

    def body(q_ref, k_ref, v_ref, o_ref, lc_ref, acc_ref, car_ref):
        i = pl.program_id(1)
        masks = _head_masks()
        qs = q_ref[...] * ATT_SCALE
        qh = [jnp.where(mk, qs, 0).astype(BF16) for mk in masks]
        row = lax.broadcasted_iota(jnp.int32, (BLK, BLK), 0)
        col = lax.broadcasted_iota(jnp.int32, (BLK, BLK), 1)
        lane = lax.broadcasted_iota(jnp.int32, (BLK, LANES), 1)
        causal = col < row
        strict2 = _stacked((row > col).astype(BF16))
        acc_ref[...] = jnp.zeros_like(acc_ref)
        car_ref[...] = jnp.zeros_like(car_ref)
        lc_ref[...] = jnp.full_like(lc_ref, NOT_VISITED)

        def tile(j, nb, diag):
            hs, bs = range(2), range(nb)
            rows = _kv_rows(j, nb)
            k, v = k_ref[rows, :], v_ref[rows, :]
            ell, kap, later = _sb_units(qh, k, strict2, causal, nb, diag)
            car = [[None] * nb for _ in hs]
            for hh in hs:
                run = car_ref[hh]
                for b in reversed(bs):
                    car[hh][b] = run
                    run = run + _row_sum(kap[hh][b])
                car_ref[hh] = run
            if not diag:
                kept = lc_ref[...]
                for hh in hs:
                    for b in bs:
                        kept = jnp.where(lane == j + b + HEAD_DIM * hh, car[hh][b], kept)
                lc_ref[...] = kept
            a = [[jnp.exp(ell[hh][b] + later[hh][b] + car[hh][b]) for b in bs] for hh in hs]
            if diag:
                a = [[jnp.where(causal, a[hh][b], 0.0) for b in bs] for hh in hs]
            for hh in hs:
                acc_ref[hh] += _dot(_join(a[hh]), v, NN)

        _walk_kv(i, tile, reverse=True, more=lambda: jnp.max(car_ref[...]) > EXP_IS_ZERO)
        o_ref[...] = jnp.where(masks[0], acc_ref[0], acc_ref[1])

    q_spec, k_spec, v_spec = _att_specs(3 * N_PAIRS, lp)
    blk = pl.BlockSpec((BLK, LANES), lambda p, i: (i, p))
    out = jax.ShapeDtypeStruct((lp, WIDTH), F32)
    return pl.pallas_call(
        body, name="sb_fwd",
        out_shape=(out, out),
        grid=(N_PAIRS, nq),
        in_specs=[q_spec, k_spec, v_spec],
        out_specs=(blk, blk),
        scratch_shapes=[pltpu.VMEM((2, BLK, LANES), F32), pltpu.VMEM((2, BLK, 1), F32)],
        compiler_params=_params(("parallel", "parallel")),
    )(qkv, qkv, qkv)


def _sb_bwd(qkv, lcar, d_o):
    lp = qkv.shape[0]
    nq = lp // BLK

    def body(q_ref, k_ref, v_ref, lc_ref, do_ref, dq_ref, dk_ref, dv_ref, acc_ref, cg_ref):
        i = pl.program_id(1)
        masks = _head_masks()
        qs = q_ref[...] * ATT_SCALE
        qh = [jnp.where(mk, qs, 0).astype(BF16) for mk in masks]
        dov = do_ref[...]
        doh = [jnp.where(mk, dov, 0.0).astype(BF16) for mk in masks]
        lcv = lc_ref[...]
        lane_row = lax.broadcasted_iota(jnp.int32, (1, LANES), 1)
        alive = jnp.where(jnp.max(lcv, axis=0, keepdims=True) > EXP_IS_ZERO, 1.0, 0.0)
        n_alive = jnp.maximum(jnp.sum(jnp.where(lane_row < HEAD_DIM, alive, 0.0)),
                              jnp.sum(jnp.where(lane_row >= HEAD_DIM, alive, 0.0))).astype(jnp.int32)
        first_block = jnp.maximum(i - n_alive, 0)
        row = lax.broadcasted_iota(jnp.int32, (BLK, BLK), 0)
        col = lax.broadcasted_iota(jnp.int32, (BLK, BLK), 1)
        lane = lax.broadcasted_iota(jnp.int32, (BLK, LANES), 1)
        causal = col < row
        strict2 = _stacked((row > col).astype(BF16))
        before2 = _stacked((row < col).astype(BF16))

        @pl.when(i == 0)
        def _():
            dk_ref[...] = jnp.zeros_like(dk_ref)
            dv_ref[...] = jnp.zeros_like(dv_ref)

        acc_ref[...] = jnp.zeros_like(acc_ref)
        cg_ref[...] = jnp.zeros_like(cg_ref)

        def tile(j, nb, diag):
            hs, bs = range(2), range(nb)
            rows = _kv_rows(j, nb)
            k, v = k_ref[rows, :], v_ref[rows, :]
            if diag:
                car = [[0.0] for _ in hs]
            else:
                car = [[_row_sum(jnp.where(lane == j + b + HEAD_DIM * hh, lcv, 0.0)) for b in bs] for hh in hs]
            da = [_dot(doh[hh], v, NT) for hh in hs]
            ell, _, later = _sb_units(qh, k, strict2, causal, nb, diag)
            a = [[jnp.exp(ell[hh][b] + later[hh][b] + car[hh][b]) for b in bs] for hh in hs]
            if diag:
                a = [[jnp.where(causal, a[hh][b], 0.0) for b in bs] for hh in hs]
            g = [[da[hh][:, b * BLK:(b + 1) * BLK] * a[hh][b] for b in bs] for hh in hs]
            cg = [[_cumsum_dot(g[hh][b], before2) for b in bs] for hh in hs]
            before = [[None] * nb for _ in hs]
            for hh in hs:
                run = cg_ref[hh]
                for b in bs:
                    before[hh][b] = run
                    run = run + _row_sum(g[hh][b])
                cg_ref[hh] = run
            dz = [[g[hh][b] - jnp.exp(ell[hh][b]) * (g[hh][b] + cg[hh][b] + before[hh][b]) for b in bs] for hh in hs]
            if diag:
                dz = [[jnp.where(causal, dz[hh][b], 0.0) for b in bs] for hh in hs]
            dzb = [_join(dz[hh]) for hh in hs]
            ab = [_join(a[hh]) for hh in hs]
            for hh in hs:
                acc_ref[hh] += _dot(dzb[hh], k, NN)
            dk_ref[rows, :] += _dot(dzb[0], qh[0], TN) + _dot(dzb[1], qh[1], TN)
            dv_ref[rows, :] += _dot(ab[0], doh[0], TN) + _dot(ab[1], doh[1], TN)

        _walk_kv(i, tile, first_block=first_block)
        dq_ref[...] = (jnp.where(masks[0], acc_ref[0], acc_ref[1]) * ATT_SCALE).astype(BF16)

    q_spec, k_spec, v_spec = _att_specs(3 * N_PAIRS, lp)
    blk = pl.BlockSpec((BLK, LANES), lambda p, i: (i, p))
    col_full = pl.BlockSpec((lp, LANES), lambda p, i: (0, p))
    return pl.pallas_call(
        body, name="sb_bwd",
        out_shape=(jax.ShapeDtypeStruct((lp, WIDTH), BF16), jax.ShapeDtypeStruct((lp, WIDTH), F32),
                   jax.ShapeDtypeStruct((lp, WIDTH), F32)),
        grid=(N_PAIRS, nq),
        in_specs=[q_spec, k_spec, v_spec, blk, blk],
        out_specs=(blk, col_full, col_full),
        scratch_shapes=[pltpu.VMEM((2, BLK, LANES), F32), pltpu.VMEM((2, BLK, 1), F32)],
        compiler_params=_params(("parallel", "arbitrary")),
    )(qkv, qkv, qkv, lcar, d_o)


def _local_step(x, target, meta, gains, w_in, b_forget, w_fox, w_sb, w_out, w_up, conv_w, conv_b, w_down,
                ffn_block=None, late_weights=None, early_grads=None, w_in_grad=None):
    s, d = x.shape
    n_valid = N_META + s
    lp = -(-n_valid // BLK) * BLK
    pad = lp - n_valid
    nq = lp // BLK


    q_a, k_a, v_a, f_a, q_b, k_b, v_b, g_a, g_b = jnp.split(
        w_in, [512, 1024, 1536, 1544, 2056, 2568, 3080, 4104], axis=1)
    w_qkv = jnp.concatenate([q_a, k_a, v_a, q_b, k_b, v_b], axis=1)
    w_gf = jnp.concatenate([g_a, g_b, f_a, jnp.zeros((d, F_PAD - N_HEADS), BF16)], axis=1)
    b_pad = jnp.concatenate([b_forget.reshape(1, N_HEADS), jnp.zeros((1, LANES - N_HEADS), F32)], axis=1)
    g0, g1, g2, g3 = (gains[i:i + 1] for i in range(4))

    h0, xn1 = _first_norm_fwd(x, meta, g0, lp)
    qkv = _mm(xn1, w_qkv, "nn", BF16, "proj_qkv")
    gf = _mm(xn1, w_gf, "nn", F32, "proj_gates")
    fpre = gf[:, 2 * d:2 * d + LANES]
    c = _forget_fwd(fpre, b_pad)[:, :N_HEADS]
    c_pieces = _bf16_pieces(c)
    one = jnp.ones((lp, N_HEADS), BF16)
    kaug = _aug_lanes(3 * [one] + [-x for x in c_pieces] + 3 * [one])
    first_block = _fox_first_blocks(qkv, c)
    o_a, lse, *gathered = _fox_fwd(qkv, _aug_lanes(c_pieces + 3 * [one] + 3 * [0 * one] + [first_block]), kaug,
                                   late_weights[0] if late_weights else None)
    if late_weights:
        w_fox, w_sb, w_out, w_up, conv_w, w_down = late_weights[1](gathered)
    ffn_block = ffn_block or w_up.shape[1] // 2
    o_b, lcar = _sb_fwd(qkv)
    h1, gated, mixed, xn3 = _mix_fwd(o_a, o_b, gf, h0, w_fox, w_sb, w_out, g1, g2)
    up = _mm(xn3, w_up, "nn", F32, "ffn_up")
    act = _conv_gelu_fwd(up, conv_w, conv_b, ffn_block)
    ffn = _mm(act, w_down, "nn", F32, "ffn_down")
    dy, loss_acc, d_ffn, dg3 = _out_loss(h1, ffn, g3, target)
    loss = loss_acc[0, 0]

    d_act = _mm(d_ffn, w_down, "nt", F32, "ffn_down_dx")
    gw_down = _mm(act, d_ffn, "tn", BF16, "ffn_down_dw")
    d_up, g_conv_w, g_conv_b = _conv_gelu_bwd(up, d_act, conv_w, conv_b, ffn_block)
    d_xn3 = _mm(d_up, w_up, "nt", F32, "ffn_up_dx")
    gw_up = _mm(xn3, d_up, "tn", BF16, "ffn_up_dw")
    dh1, dg2 = _rmsnorm_bwd(h1, g2, d_xn3, dy, F32, "norm3_bwd")

    d_mixed, dg1 = _rmsnorm_bwd(mixed, g1, dh1, None, BF16, "norm2_bwd")
    gw_out = _mm(gated, d_mixed, "tn", BF16, "out_dw")
    d_ya, d_yb, d_ga, d_gb = _gate_bwd(d_mixed, gf, o_a, o_b, w_fox, w_sb, w_out)
    d_oa = _mm(d_ya, w_fox, "nt", F32, "fox_o_dx")
    gw_fox = _mm(o_a, d_ya, "tn", BF16, "fox_o_dw")
    d_ob = _mm(d_yb, w_sb, "nt", F32, "sb_o_dx")
    gw_sb = _mm(o_b, d_yb, "tn", BF16, "sb_o_dw")
    neg_lse = [-x for x in _bf16_pieces(lse.transpose(0, 2, 1, 3).reshape(N_HEADS, lp).T)]
    neg_dsum = [-x for x in _bf16_pieces(_head_dots(d_oa, o_a)[:, :N_HEADS])]
    qaug = _aug_lanes(c_pieces + 3 * [one] + neg_lse + [first_block])
    early = {"w_o_fox": gw_fox, "w_o_sb": gw_sb, "w_out": gw_out, "w_up": gw_up, "conv_w": g_conv_w,
             "conv_b": g_conv_b, "w_down": gw_down}
    dq_a, dk_a, dv_a, dcrow, *early_parts = _fox_bwd(
        qkv, qaug, kaug, _aug_lanes(neg_dsum), d_oa, early_grads(early) if early_grads else None)
    dq_b, dk_b, dv_b = _sb_bwd(qkv, lcar, d_ob)
    dc = dcrow.transpose(0, 2, 1, 3).reshape(N_HEADS, lp).T
    dc = jnp.concatenate([dc, jnp.zeros((lp, LANES - N_HEADS), F32)], axis=1)
    df, db = _forget_bwd(dc, fpre, b_pad)
    lane = jnp.arange(LANES) < N_HEADS
    df = jnp.where(lane[None, :], df, 0.0)
    d_proj = jnp.concatenate(
        [dq_a, dk_a.astype(BF16), dv_a.astype(BF16), dq_b, dk_b.astype(BF16), dv_b.astype(BF16),
         d_ga, d_gb, df.astype(BF16), jnp.zeros((lp, F_PAD - LANES), BF16)], axis=1)
    w_in_p = jnp.concatenate([w_qkv, w_gf], axis=1)
    gw_in_p = _mm(xn1, d_proj, "tn", BF16, "proj_dw")
    qkv_parts = jnp.split(gw_in_p[:, :6 * WIDTH], 6, axis=1)
    gw_in = jnp.concatenate(
        qkv_parts[:3] + [gw_in_p[:, 6 * WIDTH + 2 * d:6 * WIDTH + 2 * d + N_HEADS]] + qkv_parts[3:]
        + [gw_in_p[:, 6 * WIDTH:6 * WIDTH + 2 * d]], axis=1)
    if w_in_grad:
        d_xn1, *w_in_parts = _mm(d_proj, w_in_p, "nt", F32, "proj_dx", w_in_grad(gw_in))
    else:
        d_xn1, w_in_parts = _mm(d_proj, w_in_p, "nt", F32, "proj_dx"), []
    grad_x, grad_meta, dg0 = _first_norm_bwd(h0, g0, d_xn1, dh1, s)
    grads = {
        "w_in_parts": w_in_parts,
        "meta_tokens": grad_meta,
        "norm_gains": jnp.concatenate([dg0, dg1, dg2, dg3], axis=0),
        "w_in": gw_in,
        "b_forget": db[:, :N_HEADS],
        "early_parts": early_parts,
        **early,
    }
    return loss, grad_x, grads


MESH_IDS = pl.DeviceIdType.MESH


def _window(ref, kind, idx, rows, cols):
    if kind == "slots":
        return ref.at[idx]
    if kind == "gate_value":
        half = N_DEV // 2
        idx = jnp.where(idx < half, 2 * idx, 2 * (idx - half) + 1)
        kind = "cols"
    if kind == "cols":
        return ref.at[:, pl.ds(pl.multiple_of(idx * cols, cols & -cols), cols)]
    return ref.at[pl.ds(pl.multiple_of(idx * rows, rows & -rows), rows), :]


def _gathered_shape(shape, kind):
    rows, cols = shape
    return {"slots": (N_DEV, rows, cols), "cols": (rows, N_DEV * cols), "gate_value": (rows, N_DEV * cols),
            "rows": (N_DEV * rows, cols)}[kind]


def _all_gather(shards, kinds):
    n = len(shards)

    def body(*refs):
        ins, outs = refs[:n], refs[n:2 * n]
        send_sems, recv_sems, local_sems = refs[2 * n:]
        x, y, c = lax.axis_index("x"), lax.axis_index("y"), lax.axis_index("c")
        me, sibling = (x, y, c), (x, y, 1 - c)
        chips = [(1 - x, y), (x, 1 - y), (1 - x, 1 - y)]

        def part(t, px, py, pc):
            return _window(outs[t], kinds[t], 4 * px + 2 * py + pc, *shards[t].shape)

        def copy(k, t, blk, to, src=None):
            return pltpu.make_async_remote_copy(
                src_ref=part(t, *blk) if src is None else src, dst_ref=part(t, *blk),
                send_sem=send_sems.at[k, t], recv_sem=recv_sems.at[k, t],
                device_id=to, device_id_type=MESH_IDS)

        mine = [pltpu.make_async_copy(ins[t], part(t, *me), local_sems.at[t]) for t in range(n)]
        for cp in mine:
            cp.start()
        first = [copy(0, t, me, sibling, src=ins[t]) for t in range(n)]
        first += [copy(1 + j, t, me, (*chip, c), src=ins[t]) for j, chip in enumerate(chips) for t in range(n)]
        for cp in first:
            cp.start()
        passed = []
        for j, chip in enumerate(chips):
            for t in range(n):
                copy(1 + j, t, (*chip, c), me).wait_recv()
                passed.append(copy(4 + j, t, (*chip, c), sibling))
                passed[-1].start()
        for t in range(n):
            copy(0, t, sibling, me).wait_recv()
        for j, chip in enumerate(chips):
            for t in range(n):
                copy(4 + j, t, (*chip, 1 - c), me).wait_recv()
        for cp in first + passed:
            cp.wait_send()
        for cp in mine:
            cp.wait()

    any_space = pl.BlockSpec(memory_space=pl.ANY)
    return pl.pallas_call(
        body, name="all_gather",
        out_shape=tuple(jax.ShapeDtypeStruct(_gathered_shape(a.shape, k), a.dtype) for a, k in zip(shards, kinds)),
        in_specs=[any_space] * n,
        out_specs=tuple([any_space] * n),
        scratch_shapes=[pltpu.SemaphoreType.DMA((7, n)), pltpu.SemaphoreType.DMA((7, n)),
                        pltpu.SemaphoreType.DMA((n,))],
    )(*shards)


class _Exchange:
    def __init__(self, arrays, kinds, shard_shapes, gather):
        self.arrays, self.kinds, self.shard_shapes, self.gather = list(arrays), list(kinds), list(shard_shapes), gather
        self.n = n = len(self.arrays)
        self.any_specs = [pl.BlockSpec(memory_space=pl.ANY)] * n
        if gather:
            shapes = [_gathered_shape(a.shape, k) for a, k in zip(self.arrays, kinds)]
        else:
            shapes = [(N_DEV,) + tuple(s) for s in shard_shapes]
        self.out_shapes = [jax.ShapeDtypeStruct(s, a.dtype) for s, a in zip(shapes, self.arrays)]
        self.scratch = [pltpu.SemaphoreType.DMA((N_DEV - 1, n)), pltpu.SemaphoreType.DMA((N_DEV - 1, n)),
                        pltpu.SemaphoreType.DMA((n,))]

    def copies(self, ins, outs, sems):
        send_sems, recv_sems, local_sems = sems
        x, y, c = lax.axis_index("x"), lax.axis_index("y"), lax.axis_index("c")
        my = 4 * x + 2 * y + c

        def src(t, receiver):
            if self.gather or self.kinds[t] == "all":
                return ins[t]
            return _window(ins[t], self.kinds[t], receiver, *self.shard_shapes[t])

        def dst(t, sender):
            if self.gather:
                return _window(outs[t], self.kinds[t], sender, *self.shard_shapes[t])
            return outs[t].at[sender]

        local = [pltpu.make_async_copy(src(t, my), dst(t, my), local_sems.at[t]) for t in range(self.n)]
        sends, arrivals = [], []
        for rel in range(1, N_DEV):
            px, py, pc = x ^ (rel >> 2), y ^ ((rel >> 1) & 1), c ^ (rel & 1)
            peer = 4 * px + 2 * py + pc
            for t in range(self.n):
                common = dict(send_sem=send_sems.at[rel - 1, t], recv_sem=recv_sems.at[rel - 1, t],
                              device_id=(px, py, pc), device_id_type=MESH_IDS)
                sends.append(pltpu.make_async_remote_copy(src_ref=src(t, peer), dst_ref=dst(t, my), **common))
                arrivals.append(pltpu.make_async_remote_copy(src_ref=src(t, my), dst_ref=dst(t, peer), **common))
        return local, sends, arrivals

    def start(self, ins, outs, sems):
        local, sends, _ = self.copies(ins, outs, sems)
        for cp in local + sends:
            cp.start()

    def wait(self, ins, outs, sems):
        local, sends, arrivals = self.copies(ins, outs, sems)
        for cp in arrivals:
            cp.wait_recv()
        for cp in sends:
            cp.wait_send()
        for cp in local:
            cp.wait()


def _exchange(grads, kinds, shard_shapes):
    ex = _Exchange(grads, kinds, shard_shapes, gather=False)
    n = ex.n

    def body(*refs):
        ins, outs, sems = refs[:n], refs[n:2 * n], refs[2 * n:]
        ex.start(ins, outs, sems)
        ex.wait(ins, outs, sems)

    return pl.pallas_call(
        body, name="grad_exchange",
        out_shape=tuple(ex.out_shapes),
        in_specs=ex.any_specs,
        out_specs=tuple(ex.any_specs),
        scratch_shapes=ex.scratch,
    )(*grads)


def _sum_adamw(parts, w, m, v, name):
    rows, cols = w.shape
    n, rows_p, cols_p = parts.shape
    tr = _tile(rows, BLK, SUBLANES) if rows > BLK else rows
    tp = tr if rows_p == rows else rows_p
    c1 = 1.0 - ADAM_B1 ** ADAM_STEP
    c2 = 1.0 - ADAM_B2 ** ADAM_STEP

    def body(p_ref, w_ref, m_ref, v_ref, g_ref, d_ref, nm_ref, nv_ref):
        gv = p_ref[0, 0:tr, 0:cols].astype(F32)
        for s in range(1, n):
            gv = gv + p_ref[s, 0:tr, 0:cols].astype(F32)
        g_ref[...] = gv
        nm = ADAM_B1 * m_ref[...] + (1.0 - ADAM_B1) * gv
        nv = ADAM_B2 * v_ref[...] + (1.0 - ADAM_B2) * (gv * gv)
        m_hat = nm / c1
        v_hat = nv / c2
        d_ref[...] = -ADAM_LR * (m_hat / (jnp.sqrt(v_hat) + ADAM_EPS) + ADAM_WD * w_ref[...])
        nm_ref[...] = nm
        nv_ref[...] = nv

    spec = pl.BlockSpec((tr, cols), lambda i: (i, 0))
    out = jax.ShapeDtypeStruct((rows, cols), F32)
    return pl.pallas_call(
        body, name=name,
        out_shape=(out, out, out, out),
        grid=(rows // tr,),
        in_specs=[pl.BlockSpec((n, tp, cols_p), lambda i: (0, i, 0)), spec, spec, spec],
        out_specs=(spec, spec, spec, spec),
        compiler_params=_params(("parallel",)),
    )(parts, w, m, v)


def _pad2(a, rows, cols):
    return jnp.pad(a, ((0, rows - a.shape[0]), (0, cols - a.shape[1])))


WEIGHTS = ["meta_tokens", "norm_gains", "w_in", "b_forget", "w_o_fox", "w_o_sb", "w_out", "w_up", "conv_w",
           "conv_b", "w_down"]


def kernel(x, meta_tokens, norm_gains, w_in, b_forget, w_o_fox, w_o_sb, w_out, w_up, conv_w, conv_b, w_down, loss_target, m_meta_tokens, m_norm_gains, m_w_in, m_b_forget, m_w_o_fox, m_w_o_sb, m_w_out, m_w_up, m_conv_w, m_conv_b, m_w_down, v_meta_tokens, v_norm_gains, v_w_in, v_b_forget, v_w_o_fox, v_w_o_sb, v_w_out, v_w_up, v_conv_w, v_conv_b, v_w_down):
    w = dict(meta_tokens=meta_tokens, norm_gains=norm_gains, w_in=w_in, b_forget=b_forget, w_o_fox=w_o_fox,
             w_o_sb=w_o_sb, w_out=w_out, w_up=w_up, conv_w=conv_w, conv_b=conv_b, w_down=w_down)
    mom = dict(meta_tokens=m_meta_tokens, norm_gains=m_norm_gains, w_in=m_w_in, b_forget=m_b_forget,
               w_o_fox=m_w_o_fox, w_o_sb=m_w_o_sb, w_out=m_w_out, w_up=m_w_up, conv_w=m_conv_w, conv_b=m_conv_b,
               w_down=m_w_down)
    vel = dict(meta_tokens=v_meta_tokens, norm_gains=v_norm_gains, w_in=v_w_in, b_forget=v_b_forget,
               w_o_fox=v_w_o_fox, w_o_sb=v_w_o_sb, w_out=v_w_out, w_up=v_w_up, conv_w=v_conv_w, conv_b=v_conv_b,
               w_down=v_w_down)
    w2 = {n: a.reshape(a.shape[-2:]) for n, a in w.items()}
    shard_shape = {n: a.shape for n, a in w2.items()}

    d = x.shape[-1]
    up_cols = shard_shape["w_up"][1]
    up_pad = -(-up_cols // LANES) * LANES
    half = N_DEV // 2
    pad_rows = lambda a: _pad2(a, SUBLANES, a.shape[1])

    shards = [
        ("w_in", "slots", w2["w_in"].astype(BF16)),
        ("meta_tokens", "cols", w2["meta_tokens"]),
        ("norm_gains", "cols", pad_rows(w2["norm_gains"])),
    ]
    later = [
        ("w_o_fox", "cols", w2["w_o_fox"].astype(BF16)),
        ("w_o_sb", "cols", w2["w_o_sb"].astype(BF16)),
        ("w_out", "rows", w2["w_out"].astype(BF16)),
        ("w_up", "gate_value", _pad2(w2["w_up"], d, up_pad).astype(BF16)),
        ("conv_w", "gate_value", _pad2(w2["conv_w"], SUBLANES, up_pad)),
        ("w_down", "rows", w2["w_down"].astype(BF16)),
    ]
    full = dict(zip([s[0] for s in shards], _all_gather([s[2] for s in shards], [s[1] for s in shards])))
    w_in_full = jnp.concatenate([full["w_in"][i] for i in range(N_DEV)], axis=1)
    conv_b_p = jnp.pad(w2["conv_b"].reshape(2, half, up_cols), ((0, 0), (0, 0), (0, up_pad - up_cols)))
    conv_b_p = conv_b_p.transpose(1, 0, 2)

    def finish_gather(gathered):
        w_fox, w_sb, w_out_full, w_up_p, conv_w_p, w_down_full = gathered
        w_down_p = jnp.pad(w_down_full.reshape(half, up_cols, d), ((0, 0), (0, up_pad - up_cols), (0, 0)))
        return w_fox, w_sb, w_out_full, w_up_p, conv_w_p[:3], w_down_p.reshape(half * up_pad, d)

    early_names = ["w_o_fox", "w_o_sb", "w_out", "w_up", "conv_w", "conv_b", "w_down"]

    def early_exchange(g):
        sends = {
            "w_o_fox": ("cols", g["w_o_fox"], shard_shape["w_o_fox"]),
            "w_o_sb": ("cols", g["w_o_sb"], shard_shape["w_o_sb"]),
            "w_out": ("rows", g["w_out"], shard_shape["w_out"]),
            "w_up": ("gate_value", g["w_up"], (d, up_pad)),
            "conv_w": ("gate_value", pad_rows(g["conv_w"]), (SUBLANES, up_pad)),
            "conv_b": ("all", pad_rows(g["conv_b"].reshape(half, 2, up_pad).transpose(1, 0, 2)[:, :, :up_cols]
                                       .reshape(1, -1)),
                       (SUBLANES, N_DEV * up_cols)),
            "w_down": ("rows", g["w_down"].reshape(half, up_pad, d)[:, :up_cols].reshape(half * up_cols, d),
                       shard_shape["w_down"]),
        }
        return _Exchange([sends[n][1] for n in early_names], [sends[n][0] for n in early_names],
                         [sends[n][2] for n in early_names], gather=False)

    def w_in_exchange(g):
        in_cols = shard_shape["w_in"][1]
        return _Exchange([jnp.stack([g[:, i * in_cols:(i + 1) * in_cols] for i in range(N_DEV)])], ["slots"],
                         [shard_shape["w_in"]], gather=False)

    late_weights = (_Exchange([s[2] for s in later], [s[1] for s in later], [s[2].shape for s in later], gather=True),
                    finish_gather)
    loss, grad_x, grads = _local_step(
        x[0], loss_target[0], full["meta_tokens"], full["norm_gains"][:4], w_in_full, w2["b_forget"],
        None, None, None, None, None, conv_b_p.reshape(1, N_DEV * up_pad), None, ffn_block=up_pad,
        late_weights=late_weights, early_grads=early_exchange, w_in_grad=w_in_exchange)
    loss = lax.psum(loss, ("x", "y", "c"))

    late_names = ["meta_tokens", "norm_gains", "b_forget"]
    sends = {
        "meta_tokens": ("cols", grads["meta_tokens"], (N_META, LANES)),
        "norm_gains": ("cols", pad_rows(grads["norm_gains"]), (SUBLANES, LANES)),
        "b_forget": ("all", _pad2(grads["b_forget"], SUBLANES, LANES), (SUBLANES, LANES)),
    }
    parts = dict(zip(late_names, _exchange([sends[n][1] for n in late_names], [sends[n][0] for n in late_names],
                                           [sends[n][2] for n in late_names])))
    parts.update(zip(early_names, grads["early_parts"]))
    parts["w_in"], = grads["w_in_parts"]

    grad, delta, new_m, new_v = {}, {}, {}, {}
    for n in WEIGHTS:
        shape = w[n].shape
        outs = _sum_adamw(parts[n], w2[n], mom[n].reshape(shard_shape[n]), vel[n].reshape(shard_shape[n]),
                          "adamw_" + n)
        grad[n], delta[n], new_m[n], new_v[n] = (o.reshape(shape) for o in outs)

    return (loss, grad_x[None], *[grad[n] for n in WEIGHTS], *[delta[n] for n in WEIGHTS],
            *[new_m[n] for n in WEIGHTS], *[new_v[n] for n in WEIGHTS])
```

```python
import functools
import math

import jax
import jax.numpy as jnp
from jax import lax
from jax.experimental import pallas as pl
from jax.experimental.pallas import tpu as pltpu

F32 = jnp.float32
BF16 = jnp.bfloat16

N_DEV = 8
N_META = 16
HEAD_DIM = 64
N_HEADS = 8
WIDTH = N_HEADS * HEAD_DIM
N_PAIRS = N_HEADS // 2
LANES = 128
SUBLANES = 8
EPS = 1e-6
ATT_SCALE = HEAD_DIM ** -0.5
BLK = 256
F_PAD = 256
VMEM_LIMIT = 48 << 20

ADAM_LR = 0.001
ADAM_B1 = 0.9
ADAM_B2 = 0.999
ADAM_EPS = 1e-08
ADAM_WD = 0.01
ADAM_STEP = 10

GELU_C = math.sqrt(2.0 / math.pi)
GELU_A = 0.044715
EXP_IS_ZERO = -110.0
NOT_VISITED = -1e30


def _params(sem, vmem=VMEM_LIMIT):
    return pltpu.CompilerParams(dimension_semantics=sem, vmem_limit_bytes=vmem)


def _tile(dim, cap, align=LANES):
    t = (min(cap, dim) // align) * align
    while t >= align:
        if dim % t == 0:
            return t
        t -= align
    return dim


def _log_sigmoid_parts(z):
    lp = jnp.log1p(jnp.exp(-jnp.abs(z)))
    return jnp.minimum(z, 0.0) - lp, jnp.minimum(-z, 0.0) - lp


def _sigmoid(x):
    return 1.0 / (1.0 + jnp.exp(-x))


def _split_bf16(x):
    hi = x.astype(BF16)
    lo = (x - hi.astype(F32)).astype(BF16)
    return hi, lo


def _dot(a, b, dims):
    return lax.dot_general(a, b, (dims, ((), ())), preferred_element_type=F32)


NN = ((1,), (0,))
NT = ((1,), (1,))
TN = ((0,), (0,))


def _mm(a, b, mode, out_dtype, name, exchange=None):
    if mode == "nn":
        (m, kc), (_, n) = a.shape, b.shape
    elif mode == "nt":
        (m, kc), (n, _) = a.shape, b.shape
    else:
        (kc, m), (_, n) = a.shape, b.shape
    if mode == "tn":
        tm, tn, tk = _tile(m, 1024), _tile(n, 1792), _tile(kc, 1408)
    else:
        tm, tn, tk = _tile(m, 1408), _tile(n, 1024), _tile(kc, 1536)
    nk = kc // tk
    dims = {"nn": NN, "nt": NT, "tn": TN}[mode]
    grid = (n // tn, m // tm, nk)
    split, first_step, last_step = _riding(exchange, 2, 1, 1 if nk > 1 else 0, grid)

    def body(*refs):
        (a_ref, b_ref, o_ref, *scratch), ex_refs = split(refs)
        first_step(ex_refs)
        k = pl.program_id(2)
        part = _dot(a_ref[...].astype(BF16), b_ref[...].astype(BF16), dims)
        if nk == 1:
            o_ref[...] = part.astype(out_dtype)
        else:
            acc_ref, = scratch

            @pl.when(k == 0)
            def _():
                acc_ref[...] = part

            @pl.when(k > 0)
            def _():
                acc_ref[...] += part

            @pl.when(k == nk - 1)
            def _():
                o_ref[...] = acc_ref[...].astype(out_dtype)
        last_step(ex_refs)

    if mode == "tn":
        a_spec = pl.BlockSpec((tk, tm), lambda j, i, k: (k, i))
    else:
        a_spec = pl.BlockSpec((tm, tk), lambda j, i, k: (i, k))
    if mode == "nt":
        b_spec = pl.BlockSpec((tn, tk), lambda j, i, k: (j, k))
    else:
        b_spec = pl.BlockSpec((tk, tn), lambda j, i, k: (k, j))
    ex = exchange
    out = pl.pallas_call(
        body, name=name,
        out_shape=(jax.ShapeDtypeStruct((m, n), out_dtype),) + tuple(ex.out_shapes if ex else ()),
        grid=grid,
        in_specs=[a_spec, b_spec] + (ex.any_specs if ex else []),
        out_specs=(pl.BlockSpec((tm, tn), lambda j, i, k: (i, j)),) + tuple(ex.any_specs if ex else ()),
        scratch_shapes=([pltpu.VMEM((tm, tn), F32)] if nk > 1 else []) + (ex.scratch if ex else []),
        compiler_params=_params(("arbitrary",) * 3 if ex else ("parallel", "parallel", "arbitrary")),
    )(a, b, *(ex.arrays if ex else []))
    return out if ex else out[0]


def _shifted_rows(cur_ref, prev_ref, first=None):
    head = prev_ref[...] if first is None else jnp.where(pl.program_id(0) == 0, first, prev_ref[...])
    return jnp.concatenate([head, cur_ref[0:BLK - N_META, :]], axis=0)


def _shifted_specs(s, d):
    per = BLK // N_META
    return (pl.BlockSpec((BLK, d), lambda i: (jnp.minimum(i, s // BLK - 1), 0)),
            pl.BlockSpec((N_META, d), lambda i: (jnp.maximum(per * i - 1, 0), 0)))


def _first_norm_fwd(x, meta, g, lp):
    s, d = x.shape
    n_valid = N_META + s

    def body(x_ref, p_ref, m_ref, g_ref, h_ref, o_ref):
        row = pl.program_id(0) * BLK + lax.broadcasted_iota(jnp.int32, (BLK, 1), 0)
        h = jnp.where(row < n_valid, _shifted_rows(x_ref, p_ref, m_ref[...]), 0.0)
        h_ref[...] = h
        r = lax.rsqrt(jnp.mean(h * h, axis=-1, keepdims=True) + EPS)
        o_ref[...] = ((h * r) * g_ref[...]).astype(BF16)

    row_d = pl.BlockSpec((BLK, d), lambda i: (i, 0))
    return pl.pallas_call(
        body, name="norm1_fwd",
        out_shape=(jax.ShapeDtypeStruct((lp, d), F32), jax.ShapeDtypeStruct((lp, d), BF16)),
        grid=(lp // BLK,),
        in_specs=[*_shifted_specs(s, d), pl.BlockSpec((N_META, d), lambda i: (0, 0)),
                  pl.BlockSpec((1, d), lambda i: (0, 0))],
        out_specs=(row_d, row_d),
        compiler_params=_params(("parallel",)),
    )(x, x, meta, g)


def _first_norm_bwd(h0, g, dy, resid, s):
    lp, d = h0.shape
    nb = lp // BLK
    assert nb == s // BLK + 1

    def body(x_ref, g_ref, dy_ref, r_ref, gx_ref, gm_ref, dg_ref, keep_ref):
        i = pl.program_id(0)
        xv = x_ref[...]
        dyv = dy_ref[...]
        r = lax.rsqrt(jnp.mean(xv * xv, axis=-1, keepdims=True) + EPS)
        xh = xv * r
        dyg = dyv * g_ref[...]
        dx = r * (dyg - xh * jnp.mean(dyg * xh, axis=-1, keepdims=True)) + r_ref[...]

        @pl.when(i == 0)
        def _():
            dg_ref[...] = jnp.zeros_like(dg_ref)
            keep_ref[...] = jnp.zeros_like(keep_ref)
            gm_ref[...] = dx[0:N_META, :]

        gx_ref[...] = jnp.concatenate([keep_ref[...], dx[0:N_META, :]], axis=0)
        keep_ref[...] = dx[N_META:BLK, :]
        dg_ref[...] += jnp.sum(dyv * xh, axis=0, keepdims=True)

    row = pl.BlockSpec((BLK, d), lambda i: (i, 0))
    vec = pl.BlockSpec((1, d), lambda i: (0, 0))
    return pl.pallas_call(
        body, name="norm1_bwd",
        out_shape=(jax.ShapeDtypeStruct((s, d), F32), jax.ShapeDtypeStruct((N_META, d), F32),
                   jax.ShapeDtypeStruct((1, d), F32)),
        grid=(nb,),
        in_specs=[row, vec, row, row],
        out_specs=(pl.BlockSpec((BLK, d), lambda i: (jnp.maximum(i - 1, 0), 0)),
                   pl.BlockSpec((N_META, d), lambda i: (0, 0)), vec),
        scratch_shapes=[pltpu.VMEM((BLK - N_META, d), F32)],
        compiler_params=_params(("arbitrary",)),
    )(h0, g, dy, resid)


def _rmsnorm_bwd(x, g, dy, resid, out_dtype, name):
    lp, d = x.shape
    has_resid = resid is not None

    def body(*refs):
        if has_resid:
            x_ref, g_ref, dy_ref, r_ref, dx_ref, dg_ref = refs
        else:
            x_ref, g_ref, dy_ref, dx_ref, dg_ref = refs
        i = pl.program_id(0)
        xv = x_ref[...]
        dyv = dy_ref[...].astype(F32)
        r = lax.rsqrt(jnp.mean(xv * xv, axis=-1, keepdims=True) + EPS)
        xh = xv * r
        dyg = dyv * g_ref[...]
        dx = r * (dyg - xh * jnp.mean(dyg * xh, axis=-1, keepdims=True))
        if has_resid:
            dx = dx + r_ref[...]
        dx_ref[...] = dx.astype(out_dtype)

        @pl.when(i == 0)
        def _():
            dg_ref[...] = jnp.zeros_like(dg_ref)

        dg_ref[...] += jnp.sum(dyv * xh, axis=0, keepdims=True)

    row = pl.BlockSpec((BLK, d), lambda i: (i, 0))
    vec = pl.BlockSpec((1, d), lambda i: (0, 0))
    ins = [x, g, dy] + ([resid] if has_resid else [])
    in_specs = [row, vec, row] + ([row] if has_resid else [])
    return pl.pallas_call(
        body, name=name,
        out_shape=(jax.ShapeDtypeStruct((lp, d), out_dtype), jax.ShapeDtypeStruct((1, d), F32)),
        grid=(lp // BLK,),
        in_specs=in_specs,
        out_specs=(row, vec),
        compiler_params=_params(("arbitrary",)),
    )(*ins)


def _norm2_bwd(gated, w_out, g, dy):
    lp, d = dy.shape

    def body(a_ref, w_ref, g_ref, dy_ref, dx_ref, dg_ref):
        i = pl.program_id(0)
        xv = _dot(a_ref[...], w_ref[...], NN)
        dyv = dy_ref[...]
        r = lax.rsqrt(jnp.mean(xv * xv, axis=-1, keepdims=True) + EPS)
        xh = xv * r
        dyg = dyv * g_ref[...]
        dx_ref[...] = (r * (dyg - xh * jnp.mean(dyg * xh, axis=-1, keepdims=True))).astype(BF16)

        @pl.when(i == 0)
        def _():
            dg_ref[...] = jnp.zeros_like(dg_ref)

        dg_ref[...] += jnp.sum(dyv * xh, axis=0, keepdims=True)

    row = pl.BlockSpec((BLK, d), lambda i: (i, 0))
    vec = pl.BlockSpec((1, d), lambda i: (0, 0))
    return pl.pallas_call(
        body, name="norm2_bwd",
        out_shape=(jax.ShapeDtypeStruct((lp, d), BF16), jax.ShapeDtypeStruct((1, d), F32)),
        grid=(lp // BLK,),
        in_specs=[row, pl.BlockSpec((d, d), lambda i: (0, 0)), vec, row],
        out_specs=(row, vec),
        compiler_params=_params(("arbitrary",)),
    )(gated, w_out, g, dy)


def _forget_fwd(fpre, b_pad):
    lp = fpre.shape[0]

    def body(f_ref, b_ref, c_ref, carry_ref):
        i = pl.program_id(0)

        @pl.when(i == 0)
        def _():
            carry_ref[...] = jnp.zeros_like(carry_ref)

        logf, _ = _log_sigmoid_parts(f_ref[...] + b_ref[...])
        row = lax.broadcasted_iota(jnp.int32, (BLK, BLK), 0)
        col = lax.broadcasted_iota(jnp.int32, (BLK, BLK), 1)
        tri = (col <= row).astype(BF16)
        p0 = logf.astype(BF16)
        r1 = logf - p0.astype(F32)
        p1 = r1.astype(BF16)
        p2 = (r1 - p1.astype(F32)).astype(BF16)
        c = _dot(tri, p0, NN) + _dot(tri, p1, NN) + _dot(tri, p2, NN) + carry_ref[0:1, :]
        c_ref[...] = c
        carry_ref[...] = jnp.broadcast_to(c[BLK - 1:BLK, :], carry_ref.shape)

    return pl.pallas_call(
        body, name="forget_fwd",
        out_shape=jax.ShapeDtypeStruct((lp, LANES), F32),
        grid=(lp // BLK,),
        in_specs=[pl.BlockSpec((BLK, LANES), lambda i: (i, 0)), pl.BlockSpec((1, LANES), lambda i: (0, 0))],
        out_specs=pl.BlockSpec((BLK, LANES), lambda i: (i, 0)),
        scratch_shapes=[pltpu.VMEM((SUBLANES, LANES), F32)],
        compiler_params=_params(("arbitrary",)),
    )(fpre, b_pad)


def _forget_bwd(dc, fpre, b_pad):
    lp = fpre.shape[0]
    nb = lp // BLK

    def body(dc_ref, f_ref, b_ref, df_ref, db_ref, carry_ref):
        i = pl.program_id(0)

        @pl.when(i == 0)
        def _():
            carry_ref[...] = jnp.zeros_like(carry_ref)
            db_ref[...] = jnp.zeros_like(db_ref)

        dcv = dc_ref[...]
        row = lax.broadcasted_iota(jnp.int32, (BLK, BLK), 0)
        col = lax.broadcasted_iota(jnp.int32, (BLK, BLK), 1)
        tri = (col >= row).astype(BF16)
        p0 = dcv.astype(BF16)
        r1 = dcv - p0.astype(F32)
        p1 = r1.astype(BF16)
        p2 = (r1 - p1.astype(F32)).astype(BF16)
        dlogf = _dot(tri, p0, NN) + _dot(tri, p1, NN) + _dot(tri, p2, NN) + carry_ref[0:1, :]
        carry_ref[...] = jnp.broadcast_to(dlogf[0:1, :], carry_ref.shape)
        _, ls_neg = _log_sigmoid_parts(f_ref[...] + b_ref[...])
        df = dlogf * jnp.exp(ls_neg)
        df_ref[...] = df
        db_ref[...] += jnp.sum(df, axis=0, keepdims=True)

    rev = pl.BlockSpec((BLK, LANES), lambda i: (nb - 1 - i, 0))
    vec = pl.BlockSpec((1, LANES), lambda i: (0, 0))
    return pl.pallas_call(
        body, name="forget_bwd",
        out_shape=(jax.ShapeDtypeStruct((lp, LANES), F32), jax.ShapeDtypeStruct((1, LANES), F32)),
        grid=(nb,),
        in_specs=[rev, rev, vec],
        out_specs=(rev, vec),
        scratch_shapes=[pltpu.VMEM((SUBLANES, LANES), F32)],
        compiler_params=_params(("arbitrary",)),
    )(dc, fpre, b_pad)


def _mix_fwd(o_a, o_b, gates, h0, w_fox, w_sb, w_out, g1, g2):
    lp, d = h0.shape

    def body(oa_ref, ob_ref, ga_ref, gb_ref, h_ref, wf_ref, ws_ref, wo_ref, g_ref, g2_ref,
             h1_ref, gated_ref, xn_ref):
        ya = _dot(oa_ref[...].astype(BF16), wf_ref[...], NN)
        yb = _dot(ob_ref[...].astype(BF16), ws_ref[...], NN)
        gated = _sigmoid(ga_ref[...]) * ya + _sigmoid(gb_ref[...]) * yb
        gb16 = gated.astype(BF16)
        mixed = _dot(gb16, wo_ref[...], NN)
        r = lax.rsqrt(jnp.mean(mixed * mixed, axis=-1, keepdims=True) + EPS)
        h1 = h_ref[...] + (mixed * r) * g_ref[...]
        h1_ref[...] = h1
        r2 = lax.rsqrt(jnp.mean(h1 * h1, axis=-1, keepdims=True) + EPS)
        xn_ref[...] = ((h1 * r2) * g2_ref[...]).astype(BF16)
        gated_ref[...] = gb16

    row_w = pl.BlockSpec((BLK, WIDTH), lambda i: (i, 0))
    row_d = pl.BlockSpec((BLK, d), lambda i: (i, 0))
    full = lambda s: pl.BlockSpec(s, lambda i: (0, 0))
    return pl.pallas_call(
        body, name="mix_fwd",
        out_shape=(jax.ShapeDtypeStruct((lp, d), F32), jax.ShapeDtypeStruct((lp, d), BF16),
                   jax.ShapeDtypeStruct((lp, d), BF16)),
        grid=(lp // BLK,),
        in_specs=[row_w, row_w, row_d, pl.BlockSpec((BLK, d), lambda i: (i, 1)), row_d,
                  full((WIDTH, d)), full((WIDTH, d)), full((d, d)), full((1, d)), full((1, d))],
        out_specs=(row_d, row_d, row_d),
        compiler_params=_params(("parallel",)),
    )(o_a, o_b, gates, gates, h0, w_fox, w_sb, w_out, g1, g2)


def _gate_bwd(d_mixed, gates, o_a, o_b, w_fox, w_sb, w_out):
    lp, d = d_mixed.shape

    def body(dm_ref, ga_ref, gb_ref, oa_ref, ob_ref, wf_ref, ws_ref, wo_ref, dya_ref, dyb_ref, dga_ref, dgb_ref):
        dg = _dot(dm_ref[...], wo_ref[...], NT)
        ya = _dot(oa_ref[...].astype(BF16), wf_ref[...], NN)
        yb = _dot(ob_ref[...].astype(BF16), ws_ref[...], NN)
        sa = _sigmoid(ga_ref[...])
        sb = _sigmoid(gb_ref[...])
        dya_ref[...] = (dg * sa).astype(BF16)
        dyb_ref[...] = (dg * sb).astype(BF16)
        dga_ref[...] = (dg * ya * (sa * (1.0 - sa))).astype(BF16)
        dgb_ref[...] = (dg * yb * (sb * (1.0 - sb))).astype(BF16)

    row = pl.BlockSpec((BLK, d), lambda i: (i, 0))
    row_w = pl.BlockSpec((BLK, WIDTH), lambda i: (i, 0))
    full = lambda s: pl.BlockSpec(s, lambda i: (0, 0))
    out = jax.ShapeDtypeStruct((lp, d), BF16)
    return pl.pallas_call(
        body, name="gate_bwd",
        out_shape=(out, out, out, out),
        grid=(lp // BLK,),
        in_specs=[row, row, pl.BlockSpec((BLK, d), lambda i: (i, 1)), row_w, row_w,
                  full((WIDTH, d)), full((WIDTH, d)), full((d, d))],
        out_specs=(row, row, row, row),
        compiler_params=_params(("parallel",)),
    )(d_mixed, gates, gates, o_a, o_b, w_fox, w_sb, w_out)


def _shift_down(cur, prev, n):
    rolled = pltpu.roll(cur, n, 0)
    row = lax.broadcasted_iota(jnp.int32, prev.shape, 0)
    head = jnp.where(row < n, pltpu.roll(prev, n, 0), rolled[0:SUBLANES])
    return head if cur.shape[0] == SUBLANES else jnp.concatenate([head, rolled[SUBLANES:]], axis=0)


def _shift_up(cur, nxt, n):
    rows = cur.shape[0]
    rolled = pltpu.roll(cur, rows - n, 0)
    row = lax.broadcasted_iota(jnp.int32, nxt.shape, 0)
    tail = jnp.where(row >= SUBLANES - n, pltpu.roll(nxt, SUBLANES - n, 0), rolled[rows - SUBLANES:])
    return tail if rows == SUBLANES else jnp.concatenate([rolled[:rows - SUBLANES], tail], axis=0)


def _gelu(x):
    return 0.5 * x * (1.0 + jnp.tanh(GELU_C * (x + GELU_A * (x * x * x))))


def _gelu_and_grad(x):
    t = jnp.tanh(GELU_C * (x + GELU_A * (x * x * x)))
    half = 0.5 * (1.0 + t)
    return x * half, half + 0.5 * x * (1.0 - t * t) * (GELU_C * (1.0 + 3.0 * GELU_A * (x * x)))


def _conv_taps(cur, prev, w_ref, b_ref):
    s1 = _shift_down(cur, prev, 1)
    s2 = _shift_down(cur, prev, 2)
    u = b_ref[...] + w_ref[0:1, :] * s2
    u = u + w_ref[1:2, :] * s1
    u = u + w_ref[2:3, :] * cur
    return u, s1, s2


def _conv_gelu_fwd(up, conv_w, conv_b, tc):
    lp, f2 = up.shape
    rb = BLK // SUBLANES

    def body(u_ref, p_ref, w_ref, b_ref, act_ref):
        i = pl.program_id(0)
        keep = (i > 0).astype(F32)
        u, _, _ = _conv_taps(u_ref[...], p_ref[...] * keep, w_ref, b_ref)
        act_ref[...] = (_gelu(u[:, :tc]) * u[:, tc:]).astype(BF16)

    prev_row = lambda i: jnp.maximum(i * rb - 1, 0)
    return pl.pallas_call(
        body, name="conv_gelu_fwd",
        out_shape=jax.ShapeDtypeStruct((lp, f2 // 2), BF16),
        grid=(lp // BLK, f2 // (2 * tc)),
        in_specs=[pl.BlockSpec((BLK, 2 * tc), lambda i, j: (i, j)),
                  pl.BlockSpec((SUBLANES, 2 * tc), lambda i, j: (prev_row(i), j)),
                  pl.BlockSpec((3, 2 * tc), lambda i, j: (0, j)),
                  pl.BlockSpec((1, 2 * tc), lambda i, j: (0, j))],
        out_specs=pl.BlockSpec((BLK, tc), lambda i, j: (i, j)),
        compiler_params=_params(("parallel", "parallel")),
    )(up, up, conv_w, conv_b)


def _conv_gelu_bwd(up, d_act, conv_w, conv_b, tc):
    lp, f2 = up.shape
    nb = lp // BLK
    rb = BLK // SUBLANES

    def du_of(u, da):
        gel, grad = _gelu_and_grad(u[:, :tc])
        return jnp.concatenate([da * u[:, tc:] * grad, da * gel], axis=1)

    def body(u_ref, p_ref, n_ref, da_ref, dan_ref, w_ref, b_ref, dup_ref, dcw_ref, dcb_ref):
        i = pl.program_id(1)
        cur = u_ref[...]
        u, s1, s2 = _conv_taps(cur, p_ref[...] * (i > 0).astype(F32), w_ref, b_ref)
        du = du_of(u, da_ref[...])
        u_next, _, _ = _conv_taps(n_ref[...], cur[BLK - SUBLANES:BLK, :], w_ref, b_ref)
        du_next = du_of(u_next, dan_ref[...]) * (i < nb - 1).astype(F32)
        n1 = _shift_up(du, du_next, 1)
        n2 = _shift_up(du, du_next, 2)
        dup_ref[...] = (w_ref[2:3, :] * du + w_ref[1:2, :] * n1 + w_ref[0:1, :] * n2).astype(BF16)

        @pl.when(i == 0)
        def _():
            dcw_ref[...] = jnp.zeros_like(dcw_ref)
            dcb_ref[...] = jnp.zeros_like(dcb_ref)

        dcw_ref[0:1, :] += jnp.sum(du * s2, axis=0, keepdims=True)
        dcw_ref[1:2, :] += jnp.sum(du * s1, axis=0, keepdims=True)
        dcw_ref[2:3, :] += jnp.sum(du * cur, axis=0, keepdims=True)
        dcb_ref[...] += jnp.sum(du, axis=0, keepdims=True)

    prev_row = lambda i: jnp.maximum(i * rb - 1, 0)
    next_row = lambda i: jnp.minimum((i + 1) * rb, nb * rb - 1)
    return pl.pallas_call(
        body, name="conv_gelu_bwd",
        out_shape=(jax.ShapeDtypeStruct((lp, f2), BF16), jax.ShapeDtypeStruct((3, f2), F32),
                   jax.ShapeDtypeStruct((1, f2), F32)),
        grid=(f2 // (2 * tc), nb),
        in_specs=[pl.BlockSpec((BLK, 2 * tc), lambda j, i: (i, j)),
                  pl.BlockSpec((SUBLANES, 2 * tc), lambda j, i: (prev_row(i), j)),
                  pl.BlockSpec((SUBLANES, 2 * tc), lambda j, i: (next_row(i), j)),
                  pl.BlockSpec((BLK, tc), lambda j, i: (i, j)),
                  pl.BlockSpec((SUBLANES, tc), lambda j, i: (next_row(i), j)),
                  pl.BlockSpec((3, 2 * tc), lambda j, i: (0, j)),
                  pl.BlockSpec((1, 2 * tc), lambda j, i: (0, j))],
        out_specs=(pl.BlockSpec((BLK, 2 * tc), lambda j, i: (i, j)),
                   pl.BlockSpec((3, 2 * tc), lambda j, i: (0, j)),
                   pl.BlockSpec((1, 2 * tc), lambda j, i: (0, j))),
        compiler_params=_params(("parallel", "arbitrary")),
    )(up, up, up, d_act, d_act, conv_w, conv_b)


def _out_loss(h1, ffn, g3, target):
    lp, d = h1.shape
    s = target.shape[0]
    n_valid = N_META + s

    def body(h_ref, f_ref, g_ref, t_ref, tp_ref, dy_ref, loss_ref, df_ref, dg_ref):
        i = pl.program_id(0)

        @pl.when(i == 0)
        def _():
            loss_ref[...] = jnp.zeros_like(loss_ref)
            dg_ref[...] = jnp.zeros_like(dg_ref)

        fv = f_ref[...]
        r = lax.rsqrt(jnp.mean(fv * fv, axis=-1, keepdims=True) + EPS)
        xh = fv * r
        y = h_ref[...] + xh * g_ref[...]
        row = i * BLK + lax.broadcasted_iota(jnp.int32, (BLK, 1), 0)
        valid = (row >= N_META) & (row < n_valid)
        diff = jnp.where(valid, y - _shifted_rows(t_ref, tp_ref), 0.0)
        dy = diff * (1.0 / d)
        dy_ref[...] = dy
        per_row = jnp.mean(diff * diff, axis=-1, keepdims=True)
        loss_ref[...] += 0.5 * jnp.sum(per_row, axis=0, keepdims=True)
        dyg = dy * g_ref[...]
        df_ref[...] = (r * (dyg - xh * jnp.mean(dyg * xh, axis=-1, keepdims=True))).astype(BF16)
        dg_ref[...] += jnp.sum(dy * xh, axis=0, keepdims=True)

    row_d = pl.BlockSpec((BLK, d), lambda i: (i, 0))
    vec = pl.BlockSpec((1, d), lambda i: (0, 0))
    return pl.pallas_call(
        body, name="out_loss",
        out_shape=(jax.ShapeDtypeStruct((lp, d), F32), jax.ShapeDtypeStruct((SUBLANES, LANES), F32),
                   jax.ShapeDtypeStruct((lp, d), BF16), jax.ShapeDtypeStruct((1, d), F32)),
        grid=(lp // BLK,),
        in_specs=[row_d, row_d, vec, *_shifted_specs(s, d)],
        out_specs=(row_d, pl.BlockSpec((SUBLANES, LANES), lambda i: (0, 0)), row_d, vec),
        compiler_params=_params(("arbitrary",)),
    )(h1, ffn, g3, target, target)


def _head_masks(rows=BLK):
    lane = lax.broadcasted_iota(jnp.int32, (rows, LANES), 1)
    return [lane < HEAD_DIM, lane >= HEAD_DIM]


def _att_specs(base, lp):
    q_spec = pl.BlockSpec((BLK, LANES), lambda p, i: (i, base + p))
    k_spec = pl.BlockSpec((lp, LANES), lambda p, i: (0, base + N_PAIRS + p))
    v_spec = pl.BlockSpec((lp, LANES), lambda p, i: (0, base + 2 * N_PAIRS + p))
    return q_spec, k_spec, v_spec


def _kv_rows(j, nb=1):
    return pl.ds(pl.multiple_of(j * BLK, nb * BLK), nb * BLK)


def _walk_kv(i, tile, reverse=False, first_block=0, more=None):
    if reverse:
        alive = more if more is not None else (lambda: True)
        tile(i, 1, True)

        @pl.when((i >= 1) & alive())
        def _():
            tile(i - 1, 1, False)

        rest = jnp.maximum(i - 1, 0)

        @pl.when((rest % 2 == 1) & alive())
        def _():
            tile(rest - 1, 1, False)

        pairs = rest // 2

        def wide(t):
            tile(2 * (pairs - 1 - t), 2, False)
            return t + 1

        lax.while_loop(lambda c: (c[0] < pairs) & c[1], lambda c: (wide(c[0]), alive()), (0, alive()))
        return

    pairs, odd = i // 2, i % 2

    @pl.when((first_block % 2 == 1) & (first_block < i))
    def _():
        tile(first_block, 1, False)

    def wide(t, carry):
        tile(2 * t, 2, False)
        return carry

    lax.fori_loop((first_block + 1) // 2, pairs, wide, 0)

    @pl.when(odd == 1)
    def _():
        tile(i - 1, 1, False)

    tile(i, 1, True)


def _walk_kv_ahead(i, lead, follow, kept, first_block=0):
    pairs, odd = i // 2, i % 2
    first_pair = (first_block + 1) // 2

    def keep(values):
        for ref, value in zip(kept, values):
            ref[...] = value

    @pl.when((first_block % 2 == 1) & (first_block < i))
    def _():
        follow(lead(first_block, 1), first_block, 1, False)

    @pl.when(pairs > first_pair)
    def _():
        keep(lead(2 * first_pair, 2))

    def wide(t, carry):
        ahead = lead(2 * jnp.minimum(t + 1, pairs - 1), 2)
        follow([ref[...] for ref in kept], 2 * t, 2, False)
        keep(ahead)
        return carry

    lax.fori_loop(first_pair, pairs, wide, 0)

    @pl.when(odd == 1)
    def _():
        follow(lead(i - 1, 1), i - 1, 1, False)

    follow(lead(i, 1), i, 1, True)


def _bf16_pieces(x):
    rnd = lambda a: lax.reduce_precision(a, exponent_bits=8, mantissa_bits=7)
    p0 = rnd(x)
    p1 = rnd(x - p0)
    p2 = rnd(x - p0 - p1)
    return [p0.astype(BF16), p1.astype(BF16), p2.astype(BF16)]


def _aug_lanes(cols):
    lp = cols[0].shape[0]
    vals = jnp.stack([c.astype(BF16) for c in cols], axis=-1)
    vals = vals.reshape(lp, N_PAIRS, 2, len(cols))[:, :, ::-1, :]
    vals = jnp.pad(vals, ((0, 0), (0, 0), (0, 0), (0, HEAD_DIM - len(cols))))
    return vals.reshape(lp, WIDTH)


N_AUG = 3


def _riding(exchange, n_in, n_out, n_scratch, grid):
    n = exchange.n if exchange else 0

    def split(refs):
        own_in, ex_in = refs[:n_in], refs[n_in:n_in + n]
        own_out, ex_out = refs[n_in + n:n_in + n + n_out], refs[n_in + n + n_out:n_in + 2 * n + n_out]
        rest = refs[n_in + 2 * n + n_out:]
        return own_in + own_out + rest[:n_scratch], (ex_in, ex_out, rest[n_scratch:])

    def at(step_of):
        here = pl.program_id(0) == step_of(grid[0])
        for axis in range(1, len(grid)):
            here = here & (pl.program_id(axis) == step_of(grid[axis]))
        return here

    def first_step(ex_refs):
        if exchange:
            @pl.when(at(lambda size: 0))
            def _():
                exchange.start(*ex_refs)

    def last_step(ex_refs):
        if exchange:
            @pl.when(at(lambda size: size - 1))
            def _():
                exchange.wait(*ex_refs)

    return split, first_step, last_step


def _fox_fwd(qkv, qaug, kaug, exchange=None):
    lp = qkv.shape[0]
    nq = lp // BLK
    split, first_step, last_step = _riding(exchange, 5, 2, 3, (N_PAIRS, nq))

    def body(*refs):
        (q_ref, k_ref, v_ref, qa_ref, ka_ref, o_ref, lse_ref, acc_ref, m_ref, s_ref), ex_refs = split(refs)
        first_step(ex_refs)
        i = pl.program_id(1)
        hs = range(2)
        masks = _head_masks()
        qs = q_ref[...] * ATT_SCALE
        qa = qa_ref[...]
        qh = [jnp.where(masks[hh], qs, qa) for hh in hs]
        acc_ref[...] = jnp.zeros_like(acc_ref)
        m_ref[...] = jnp.full_like(m_ref, -1e30)
        row = lax.broadcasted_iota(jnp.int32, (BLK, BLK), 0)
        col = lax.broadcasted_iota(jnp.int32, (BLK, BLK), 1)
        causal = col <= row

        def scores(j, nb):
            rows = _kv_rows(j, nb)
            kmasks = _head_masks(nb * BLK)
            k, ka = k_ref[rows, :], ka_ref[rows, :]
            return [_dot(qh[hh], jnp.where(kmasks[hh], k, ka), NT) for hh in hs]

        def absorb(s, j, nb, diag):
            v = v_ref[_kv_rows(j, nb), :]
            kmasks = _head_masks(nb * BLK)
            vh = [jnp.where(kmasks[hh], v, jnp.ones_like(v)) for hh in hs]
            if diag:
                s = [jnp.where(causal, s[hh], -1e30) for hh in hs]
            m_prev = [m_ref[hh] for hh in hs]
            m_new = [jnp.maximum(m_prev[hh], jnp.max(s[hh], axis=-1, keepdims=True)) for hh in hs]
            p = [jnp.exp(s[hh] - m_new[hh]).astype(BF16) for hh in hs]
            for hh in hs:
                acc_ref[hh] = jnp.exp(m_prev[hh] - m_new[hh]) * acc_ref[hh] + _dot(p[hh], vh[hh], NN)
                m_ref[hh] = m_new[hh]

        _walk_kv_ahead(i, scores, absorb, [s_ref.at[hh] for hh in hs], _first_block(qa, i))
        acc = [acc_ref[hh] for hh in hs]
        denom = [acc[0][:, HEAD_DIM:HEAD_DIM + 1], acc[1][:, 0:1]]
        o_ref[...] = jnp.where(masks[0], acc[0] / denom[0], acc[1] / denom[1])
        lse = jnp.where(masks[0], m_ref[0] + jnp.log(denom[0]), m_ref[1] + jnp.log(denom[1])).T
        lse_ref[0, 0, 0:1, :] = lse[0:1, :]
        lse_ref[0, 0, 1:2, :] = lse[HEAD_DIM:HEAD_DIM + 1, :]
        last_step(ex_refs)

    q_spec, k_spec, v_spec = _att_specs(0, lp)
    blk = pl.BlockSpec((BLK, LANES), lambda p, i: (i, p))
    col_full = pl.BlockSpec((lp, LANES), lambda p, i: (0, p))
    out = jax.ShapeDtypeStruct((lp, WIDTH), F32)
    ex = exchange
    return pl.pallas_call(
        body, name="fox_fwd",
        out_shape=(out, jax.ShapeDtypeStruct((N_PAIRS, nq, 2, BLK), F32)) + tuple(ex.out_shapes if ex else ()),
        grid=(N_PAIRS, nq),
        in_specs=[q_spec, k_spec, v_spec, blk, col_full] + (ex.any_specs if ex else []),
        out_specs=(blk, pl.BlockSpec((1, 1, 2, BLK), lambda p, i: (p, i, 0, 0))) + tuple(ex.any_specs if ex else ()),
        scratch_shapes=[pltpu.VMEM((2, BLK, LANES), F32), pltpu.VMEM((2, BLK, 1), F32),
                        pltpu.VMEM((2, BLK, 2 * BLK), F32)] + (ex.scratch if ex else []),
        compiler_params=_params(("arbitrary", "arbitrary") if ex else ("parallel", "parallel")),
    )(qkv, qkv, qkv, qaug, kaug, *(ex.arrays if ex else []))


FIRST_BLOCK_LANE = 9


def _first_block(qa, i):
    lane = lax.broadcasted_iota(jnp.int32, qa.shape, 1)
    first = jnp.max(jnp.where(lane == HEAD_DIM + FIRST_BLOCK_LANE, qa.astype(F32), 0.0)).astype(jnp.int32)
    return jnp.clip(first, 0, i)


def _fox_first_blocks(qkv, c):
    lp = qkv.shape[0]
    nq = lp // BLK

    def body(x_ref, o_ref):
        xv = x_ref[...].astype(F32)
        sq = (xv * xv).astype(BF16)
        col = lax.broadcasted_iota(jnp.int32, (2 * WIDTH, LANES), 0)
        lane = lax.broadcasted_iota(jnp.int32, (2 * WIDTH, LANES), 1)
        pick = (col // HEAD_DIM == lane).astype(BF16)
        o_ref[...] = _dot(sq, pick, NN)

    norms = pl.pallas_call(
        body, name="fox_norms",
        out_shape=jax.ShapeDtypeStruct((lp, LANES), F32),
        grid=(nq,),
        in_specs=[pl.BlockSpec((BLK, 2 * WIDTH), lambda i: (i, 0))],
        out_specs=pl.BlockSpec((BLK, LANES), lambda i: (i, 0)),
        compiler_params=_params(("parallel",)),
    )(qkv)
    a_max = 1.02 * jnp.sqrt(norms[:, :N_HEADS].reshape(nq, BLK, N_HEADS).max(axis=1))
    b_max = 1.02 * jnp.sqrt(norms[:, N_HEADS:2 * N_HEADS].reshape(nq, BLK, N_HEADS).max(axis=1))
    c_max = c.reshape(nq, BLK, N_HEADS).max(axis=1)
    c_min = c.reshape(nq, BLK, N_HEADS).min(axis=1)
    bound = (a_max[:, None] * (b_max[None, :] + b_max[:, None]) * ATT_SCALE + c_max[:, None] - c_min[None, :])
    alive = (bound > EXP_IS_ZERO) | jnp.isnan(bound)
    alive = alive.reshape(nq, nq, N_PAIRS, 2).any(axis=-1)
    first = jnp.argmax(alive, axis=1).astype(F32)
    return jnp.repeat(jnp.repeat(first, 2, axis=1), BLK, axis=0)


def _head_dots(a, b):
    lp = a.shape[0]

    def body(a_ref, b_ref, o_ref):
        prod = a_ref[...] * b_ref[...]
        col = lax.broadcasted_iota(jnp.int32, (WIDTH, LANES), 0)
        lane = lax.broadcasted_iota(jnp.int32, (WIDTH, LANES), 1)
        pick = (col // HEAD_DIM == lane).astype(BF16)
        hi = prod.astype(BF16)
        mid, lo = _split_bf16(prod - hi.astype(F32))
        o_ref[...] = _dot(hi, pick, NN) + _dot(mid, pick, NN) + _dot(lo, pick, NN)

    row = pl.BlockSpec((BLK, WIDTH), lambda i: (i, 0))
    return pl.pallas_call(
        body, name="head_dots",
        out_shape=jax.ShapeDtypeStruct((lp, LANES), F32),
        grid=(lp // BLK,),
        in_specs=[row, row],
        out_specs=pl.BlockSpec((BLK, LANES), lambda i: (i, 0)),
        compiler_params=_params(("parallel",)),
    )(a, b)


def _fox_bwd(qkv, qaug, kaug, doaug, d_o, exchange=None):
    lp = qkv.shape[0]
    nq = lp // BLK
    split, first_step, last_step = _riding(exchange, 7, 4, 2, (N_PAIRS, nq))

    def body(*refs):
        (q_ref, k_ref, v_ref, qa_ref, ka_ref, da_ref, do_ref,
         dq_ref, dk_ref, dv_ref, dc_ref, acc_ref, rs_ref), ex_refs = split(refs)
        first_step(ex_refs)
        i = pl.program_id(1)
        hs = range(2)
        masks = _head_masks()
        qs = q_ref[...] * ATT_SCALE
        qa = qa_ref[...]
        qm = [jnp.where(masks[hh], qs, 0) for hh in hs]
        qh = [jnp.where(masks[hh], qs, qa) for hh in hs]
        dov = do_ref[...].astype(BF16)
        doa = da_ref[...]
        dom = [jnp.where(masks[hh], dov, 0) for hh in hs]
        doh = [jnp.where(masks[hh], dov, doa) for hh in hs]
        row = lax.broadcasted_iota(jnp.int32, (BLK, BLK), 0)
        col = lax.broadcasted_iota(jnp.int32, (BLK, BLK), 1)
        causal = col <= row

        @pl.when(i == 0)
        def _():
            dk_ref[...] = jnp.zeros_like(dk_ref)
            dv_ref[...] = jnp.zeros_like(dv_ref)
            dc_ref[...] = jnp.zeros_like(dc_ref)

        acc_ref[...] = jnp.zeros_like(acc_ref)
        rs_ref[...] = jnp.zeros_like(rs_ref)

        def lead(j, nb):
            rows = _kv_rows(j, nb)
            kmasks = _head_masks(nb * BLK)
            k, ka, v = k_ref[rows, :], ka_ref[rows, :], v_ref[rows, :]
            ones = (lax.broadcasted_iota(jnp.int32, v.shape, 1) % HEAD_DIM < N_AUG).astype(BF16)
            logp = [_dot(qh[hh], jnp.where(kmasks[hh], k, ka), NT) for hh in hs]
            dp = [_dot(doh[hh], jnp.where(kmasks[hh], v, ones), NT) for hh in hs]
            return logp + dp

        def follow(lead_out, j, nb, diag):
            rows = _kv_rows(j, nb)
            k = k_ref[rows, :]
            logp, dp = lead_out[:2], lead_out[2:]
            p = [jnp.exp(logp[hh]) for hh in hs]
            if diag:
                p = [jnp.where(causal, p[hh], 0.0) for hh in hs]
            ds = [p[hh] * dp[hh] for hh in hs]
            dsb = [ds[hh].astype(BF16) for hh in hs]
            for hh in hs:
                acc_ref[hh] += _dot(dsb[hh], k, NN)
                col_sums = jnp.sum(ds[hh], axis=0, keepdims=True)
                for b in range(nb):
                    dc_ref[0, j + b, hh:hh + 1, :] -= col_sums[:, b * BLK:(b + 1) * BLK]
                rs_ref[hh] += jnp.sum(ds[hh], axis=-1, keepdims=True)
            dk_ref[rows, :] += _dot(dsb[0], qm[0], TN) + _dot(dsb[1], qm[1], TN)
            dv_ref[rows, :] += _dot(p[0].astype(BF16), dom[0], TN) + _dot(p[1].astype(BF16), dom[1], TN)

        _walk_kv(i, lambda j, nb, diag: follow(lead(j, nb), j, nb, diag), first_block=_first_block(qa, i))
        dq_ref[...] = (jnp.where(masks[0], acc_ref[0], acc_ref[1]) * ATT_SCALE).astype(BF16)
        row_sums = jnp.where(masks[0], rs_ref[0], rs_ref[1]).T
        dc_ref[0, i, 0:1, :] += row_sums[0:1, :]
        dc_ref[0, i, 1:2, :] += row_sums[HEAD_DIM:HEAD_DIM + 1, :]
        last_step(ex_refs)

    q_spec, k_spec, v_spec = _att_specs(0, lp)
    blk = pl.BlockSpec((BLK, LANES), lambda p, i: (i, p))
    col_full = pl.BlockSpec((lp, LANES), lambda p, i: (0, p))
    crow_spec = pl.BlockSpec((1, nq, 2, BLK), lambda p, i: (p, 0, 0, 0))
    ex = exchange
    return pl.pallas_call(
        body, name="fox_bwd",
        out_shape=(jax.ShapeDtypeStruct((lp, WIDTH), BF16), jax.ShapeDtypeStruct((lp, WIDTH), F32),
                   jax.ShapeDtypeStruct((lp, WIDTH), F32), jax.ShapeDtypeStruct((N_PAIRS, nq, 2, BLK), F32))
        + tuple(ex.out_shapes if ex else ()),
        grid=(N_PAIRS, nq),
        in_specs=[q_spec, k_spec, v_spec, blk, col_full, blk, blk] + (ex.any_specs if ex else []),
        out_specs=(blk, col_full, col_full, crow_spec) + tuple(ex.any_specs if ex else ()),
        scratch_shapes=[pltpu.VMEM((2, BLK, LANES), F32), pltpu.VMEM((2, BLK, 1), F32)] + (ex.scratch if ex else []),
        compiler_params=_params(("arbitrary", "arbitrary") if ex else ("parallel", "arbitrary")),
    )(qkv, qkv, qkv, qaug, kaug, doaug, d_o, *(ex.arrays if ex else []))


def _sb_scores(z):
    ell = jnp.minimum(z, 0.0) - jnp.log(1.0 + jnp.exp(-jnp.abs(z)))
    return ell, ell - z


def _stacked(tri):
    return jnp.concatenate([tri, tri], axis=0)


def _cumsum_dot(x, tri2):
    hi, lo = _split_bf16(x)
    return _dot(jnp.concatenate([hi, lo], axis=1), tri2, NN)


def _sb_units(qh, k, strict2, causal, nb, diag):
    hs, bs = range(2), range(nb)
    z = [_dot(qh[hh], k, NT) for hh in hs]
    sc = [[_sb_scores(z[hh][:, b * BLK:(b + 1) * BLK]) for b in bs] for hh in hs]
    ell = [[sc[hh][b][0] for b in bs] for hh in hs]
    kap = [[jnp.where(causal, sc[hh][b][1], 0.0) if diag else sc[hh][b][1] for b in bs] for hh in hs]
    later = [[_cumsum_dot(kap[hh][b], strict2) for b in bs] for hh in hs]
    return ell, kap, later


def _row_sum(x):
    return jnp.sum(x, axis=-1, keepdims=True)


def _join(blocks):
    joined = blocks[0] if len(blocks) == 1 else jnp.concatenate(blocks, axis=1)
    return joined.astype(BF16)


def _sb_fwd(qkv):
    lp = qkv.shape[0]
    nq = lp // BLK
    assert nq <= HEAD_DIM

    def body(q_ref, k_ref, v_ref, o_ref, lc_ref, acc_ref, car_ref):
        i = pl.program_id(1)
        masks = _head_masks()
        qs = q_ref[...] * ATT_SCALE
        qh = [jnp.where(mk, qs, 0).astype(BF16) for mk in masks]
        row = lax.broadcasted_iota(jnp.int32, (BLK, BLK), 0)
        col = lax.broadcasted_iota(jnp.int32, (BLK, BLK), 1)
        lane = lax.broadcasted_iota(jnp.int32, (BLK, LANES), 1)
        causal = col < row
        strict2 = _stacked((row > col).astype(BF16))
        acc_ref[...] = jnp.zeros_like(acc_ref)
        car_ref[...] = jnp.zeros_like(car_ref)
        lc_ref[...] = jnp.full_like(lc_ref, NOT_VISITED)

        def tile(j, nb, diag):
            hs, bs = range(2), range(nb)
            rows = _kv_rows(j, nb)
            k, v = k_ref[rows, :], v_ref[rows, :]
            ell, kap, later = _sb_units(qh, k, strict2, causal, nb, diag)
            car = [[None] * nb for _ in hs]
            for hh in hs:
                run = car_ref[hh]
                for b in reversed(bs):
                    car[hh][b] = run
                    run = run + _row_sum(kap[hh][b])
                car_ref[hh] = run
            if not diag:
                kept = lc_ref[...]
                for hh in hs:
                    for b in bs:
                        kept = jnp.where(lane == j + b + HEAD_DIM * hh, car[hh][b], kept)
                lc_ref[...] = kept
            a = [[jnp.exp(ell[hh][b] + later[hh][b] + car[hh][b]) for b in bs] for hh in hs]
            if diag:
                a = [[jnp.where(causal, a[hh][b], 0.0) for b in bs] for hh in hs]
            for hh in hs:
                acc_ref[hh] += _dot(_join(a[hh]), v, NN)

        _walk_kv(i, tile, reverse=True, more=lambda: jnp.max(car_ref[...]) > EXP_IS_ZERO)
        o_ref[...] = jnp.where(masks[0], acc_ref[0], acc_ref[1])

    q_spec, k_spec, v_spec = _att_specs(3 * N_PAIRS, lp)
    blk = pl.BlockSpec((BLK, LANES), lambda p, i: (i, p))
    out = jax.ShapeDtypeStruct((lp, WIDTH), F32)
    return pl.pallas_call(
        body, name="sb_fwd",
        out_shape=(out, out),
        grid=(N_PAIRS, nq),
        in_specs=[q_spec, k_spec, v_spec],
        out_specs=(blk, blk),
        scratch_shapes=[pltpu.VMEM((2, BLK, LANES), F32), pltpu.VMEM((2, BLK, 1), F32)],
        compiler_params=_params(("parallel", "parallel")),
    )(qkv, qkv, qkv)


def _sb_bwd(qkv, lcar, d_o):
    lp = qkv.shape[0]
    nq = lp // BLK

    def body(q_ref, k_ref, v_ref, lc_ref, do_ref, dq_ref, dk_ref, dv_ref, acc_ref, cg_ref):
        i = pl.program_id(1)
        masks = _head_masks()
        qs = q_ref[...] * ATT_SCALE
        qh = [jnp.where(mk, qs, 0).astype(BF16) for mk in masks]
        dov = do_ref[...]
        doh = [jnp.where(mk, dov, 0.0).astype(BF16) for mk in masks]
        lcv = lc_ref[...]
        lane_row = lax.broadcasted_iota(jnp.int32, (1, LANES), 1)
        alive = jnp.where(jnp.max(lcv, axis=0, keepdims=True) > EXP_IS_ZERO, 1.0, 0.0)
        n_alive = jnp.maximum(jnp.sum(jnp.where(lane_row < HEAD_DIM, alive, 0.0)),
                              jnp.sum(jnp.where(lane_row >= HEAD_DIM, alive, 0.0))).astype(jnp.int32)
        first_block = jnp.maximum(i - n_alive, 0)
        row = lax.broadcasted_iota(jnp.int32, (BLK, BLK), 0)
        col = lax.broadcasted_iota(jnp.int32, (BLK, BLK), 1)
        lane = lax.broadcasted_iota(jnp.int32, (BLK, LANES), 1)
        causal = col < row
        strict2 = _stacked((row > col).astype(BF16))
        before2 = _stacked((row < col).astype(BF16))

        @pl.when(i == 0)
        def _():
            dk_ref[...] = jnp.zeros_like(dk_ref)
            dv_ref[...] = jnp.zeros_like(dv_ref)

        acc_ref[...] = jnp.zeros_like(acc_ref)
        cg_ref[...] = jnp.zeros_like(cg_ref)

        def tile(j, nb, diag):
            hs, bs = range(2), range(nb)
            rows = _kv_rows(j, nb)
            k, v = k_ref[rows, :], v_ref[rows, :]
            if diag:
                car = [[0.0] for _ in hs]
            else:
                car = [[_row_sum(jnp.where(lane == j + b + HEAD_DIM * hh, lcv, 0.0)) for b in bs] for hh in hs]
            da = [_dot(doh[hh], v, NT) for hh in hs]
            ell, _, later = _sb_units(qh, k, strict2, causal, nb, diag)
            a = [[jnp.exp(ell[hh][b] + later[hh][b] + car[hh][b]) for b in bs] for hh in hs]
            if diag:
                a = [[jnp.where(causal, a[hh][b], 0.0) for b in bs] for hh in hs]
            g = [[da[hh][:, b * BLK:(b + 1) * BLK] * a[hh][b] for b in bs] for hh in hs]
            cg = [[_cumsum_dot(g[hh][b], before2) for b in bs] for hh in hs]
            before = [[None] * nb for _ in hs]
            for hh in hs:
                run = cg_ref[hh]
                for b in bs:
                    before[hh][b] = run
                    run = run + _row_sum(g[hh][b])
                cg_ref[hh] = run
            dz = [[g[hh][b] - jnp.exp(ell[hh][b]) * (g[hh][b] + cg[hh][b] + before[hh][b]) for b in bs] for hh in hs]
            if diag:
                dz = [[jnp.where(causal, dz[hh][b], 0.0) for b in bs] for hh in hs]
            dzb = [_join(dz[hh]) for hh in hs]
            ab = [_join(a[hh]) for hh in hs]
            for hh in hs:
                acc_ref[hh] += _dot(dzb[hh], k, NN)
            dk_ref[rows, :] += _dot(dzb[0], qh[0], TN) + _dot(dzb[1], qh[1], TN)
            dv_ref[rows, :] += _dot(ab[0], doh[0], TN) + _dot(ab[1], doh[1], TN)

        _walk_kv(i, tile, first_block=first_block)
        dq_ref[...] = (jnp.where(masks[0], acc_ref[0], acc_ref[1]) * ATT_SCALE).astype(BF16)

    q_spec, k_spec, v_spec = _att_specs(3 * N_PAIRS, lp)
    blk = pl.BlockSpec((BLK, LANES), lambda p, i: (i, p))
    col_full = pl.BlockSpec((lp, LANES), lambda p, i: (0, p))
    return pl.pallas_call(
        body, name="sb_bwd",
        out_shape=(jax.ShapeDtypeStruct((lp, WIDTH), BF16), jax.ShapeDtypeStruct((lp, WIDTH), F32),
                   jax.ShapeDtypeStruct((lp, WIDTH), F32)),
        grid=(N_PAIRS, nq),
        in_specs=[q_spec, k_spec, v_spec, blk, blk],
        out_specs=(blk, col_full, col_full),
        scratch_shapes=[pltpu.VMEM((2, BLK, LANES), F32), pltpu.VMEM((2, BLK, 1), F32)],
        compiler_params=_params(("parallel", "arbitrary")),
    )(qkv, qkv, qkv, lcar, d_o)


def _local_step(x, target, meta, gains, w_in, b_forget, w_fox, w_sb, w_out, w_up, conv_w, conv_b, w_down,
                ffn_block=None, late_weights=None, early_grads=None, w_in_grad=None):
    s, d = x.shape
    n_valid = N_META + s
    lp = -(-n_valid // BLK) * BLK
    pad = lp - n_valid
    nq = lp // BLK


    q_a, k_a, v_a, f_a, q_b, k_b, v_b, g_a, g_b = jnp.split(
        w_in, [512, 1024, 1536, 1544, 2056, 2568, 3080, 4104], axis=1)
    w_qkv = jnp.concatenate([q_a, k_a, v_a, q_b, k_b, v_b], axis=1)
    w_gf = jnp.concatenate([g_a, g_b, f_a, jnp.zeros((d, F_PAD - N_HEADS), BF16)], axis=1)
    b_pad = jnp.concatenate([b_forget.reshape(1, N_HEADS), jnp.zeros((1, LANES - N_HEADS), F32)], axis=1)
    g0, g1, g2, g3 = (gains[i:i + 1] for i in range(4))

    h0, xn1 = _first_norm_fwd(x, meta, g0, lp)
    qkv = _mm(xn1, w_qkv, "nn", BF16, "proj_qkv")
    gf = _mm(xn1, w_gf, "nn", F32, "proj_gates")
    fpre = gf[:, 2 * d:2 * d + LANES]
    c = _forget_fwd(fpre, b_pad)[:, :N_HEADS]
    c_pieces = _bf16_pieces(c)
    one = jnp.ones((lp, N_HEADS), BF16)
    kaug = _aug_lanes(3 * [one] + [-x for x in c_pieces] + 3 * [one])
    first_block = _fox_first_blocks(qkv, c)
    o_a, lse, *gathered = _fox_fwd(qkv, _aug_lanes(c_pieces + 3 * [one] + 3 * [0 * one] + [first_block]), kaug,
                                   late_weights[0] if late_weights else None)
    if late_weights:
        w_fox, w_sb, w_out, w_up, conv_w, w_down = late_weights[1](gathered)
    ffn_block = ffn_block or w_up.shape[1] // 2
    o_b, lcar = _sb_fwd(qkv)
    h1, gated, xn3 = _mix_fwd(o_a, o_b, gf, h0, w_fox, w_sb, w_out, g1, g2)
    up = _mm(xn3, w_up, "nn", F32, "ffn_up")
    act = _conv_gelu_fwd(up, conv_w, conv_b, ffn_block)
    ffn = _mm(act, w_down, "nn", F32, "ffn_down")
    dy, loss_acc, d_ffn, dg3 = _out_loss(h1, ffn, g3, target)
    loss = loss_acc[0, 0]

    d_act = _mm(d_ffn, w_down, "nt", F32, "ffn_down_dx")
    gw_down = _mm(act, d_ffn, "tn", BF16, "ffn_down_dw")
    d_up, g_conv_w, g_conv_b = _conv_gelu_bwd(up, d_act, conv_w, conv_b, ffn_block)
    d_xn3 = _mm(d_up, w_up, "nt", F32, "ffn_up_dx")
    gw_up = _mm(xn3, d_up, "tn", BF16, "ffn_up_dw")
    dh1, dg2 = _rmsnorm_bwd(h1, g2, d_xn3, dy, F32, "norm3_bwd")

    d_mixed, dg1 = _norm2_bwd(gated, w_out, g1, dh1)
    gw_out = _mm(gated, d_mixed, "tn", BF16, "out_dw")
    d_ya, d_yb, d_ga, d_gb = _gate_bwd(d_mixed, gf, o_a, o_b, w_fox, w_sb, w_out)
    d_oa = _mm(d_ya, w_fox, "nt", F32, "fox_o_dx")
    gw_fox = _mm(o_a, d_ya, "tn", BF16, "fox_o_dw")
    d_ob = _mm(d_yb, w_sb, "nt", F32, "sb_o_dx")
    gw_sb = _mm(o_b, d_yb, "tn", BF16, "sb_o_dw")
    neg_lse = [-x for x in _bf16_pieces(lse.transpose(0, 2, 1, 3).reshape(N_HEADS, lp).T)]
    neg_dsum = [-x for x in _bf16_pieces(_head_dots(d_oa, o_a)[:, :N_HEADS])]
    qaug = _aug_lanes(c_pieces + 3 * [one] + neg_lse + [first_block])
    early = {"w_o_fox": gw_fox, "w_o_sb": gw_sb, "w_out": gw_out, "w_up": gw_up, "conv_w": g_conv_w,
             "conv_b": g_conv_b, "w_down": gw_down}
    dq_a, dk_a, dv_a, dcrow, *early_parts = _fox_bwd(
        qkv, qaug, kaug, _aug_lanes(neg_dsum), d_oa, early_grads(early) if early_grads else None)
    dq_b, dk_b, dv_b = _sb_bwd(qkv, lcar, d_ob)
    dc = dcrow.transpose(0, 2, 1, 3).reshape(N_HEADS, lp).T
    dc = jnp.concatenate([dc, jnp.zeros((lp, LANES - N_HEADS), F32)], axis=1)
    df, db = _forget_bwd(dc, fpre, b_pad)
    lane = jnp.arange(LANES) < N_HEADS
    df = jnp.where(lane[None, :], df, 0.0)
    d_proj = jnp.concatenate(
        [dq_a, dk_a.astype(BF16), dv_a.astype(BF16), dq_b, dk_b.astype(BF16), dv_b.astype(BF16),
         d_ga, d_gb, df.astype(BF16), jnp.zeros((lp, F_PAD - LANES), BF16)], axis=1)
    w_in_p = jnp.concatenate([w_qkv, w_gf], axis=1)
    gw_in_p = _mm(xn1, d_proj, "tn", BF16, "proj_dw")
    qkv_parts = jnp.split(gw_in_p[:, :6 * WIDTH], 6, axis=1)
    gw_in = jnp.concatenate(
        qkv_parts[:3] + [gw_in_p[:, 6 * WIDTH + 2 * d:6 * WIDTH + 2 * d + N_HEADS]] + qkv_parts[3:]
        + [gw_in_p[:, 6 * WIDTH:6 * WIDTH + 2 * d]], axis=1)
    if w_in_grad:
        d_xn1, *w_in_parts = _mm(d_proj, w_in_p, "nt", F32, "proj_dx", w_in_grad(gw_in))
    else:
        d_xn1, w_in_parts = _mm(d_proj, w_in_p, "nt", F32, "proj_dx"), []
    grad_x, grad_meta, dg0 = _first_norm_bwd(h0, g0, d_xn1, dh1, s)
    grads = {
        "w_in_parts": w_in_parts,
        "meta_tokens": grad_meta,
        "norm_gains": jnp.concatenate([dg0, dg1, dg2, dg3], axis=0),
        "w_in": gw_in,
        "b_forget": db[:, :N_HEADS],
        "early_parts": early_parts,
        **early,
    }
    return loss, grad_x, grads


MESH_IDS = pl.DeviceIdType.MESH


def _window(ref, kind, idx, rows, cols):
    if kind == "slots":
        return ref.at[idx]
    if kind == "gate_value":
        half = N_DEV // 2
        idx = jnp.where(idx < half, 2 * idx, 2 * (idx - half) + 1)
        kind = "cols"
    if kind == "cols":
        return ref.at[:, pl.ds(pl.multiple_of(idx * cols, cols & -cols), cols)]
    return ref.at[pl.ds(pl.multiple_of(idx * rows, rows & -rows), rows), :]


def _gathered_shape(shape, kind):
    rows, cols = shape
    return {"slots": (N_DEV, rows, cols), "cols": (rows, N_DEV * cols), "gate_value": (rows, N_DEV * cols),
            "rows": (N_DEV * rows, cols)}[kind]


def _all_gather(shards, kinds):
    n = len(shards)

    def body(*refs):
        ins, outs = refs[:n], refs[n:2 * n]
        send_sems, recv_sems, local_sems = refs[2 * n:]
        x, y, c = lax.axis_index("x"), lax.axis_index("y"), lax.axis_index("c")
        me, sibling = (x, y, c), (x, y, 1 - c)
        chips = [(1 - x, y), (x, 1 - y), (1 - x, 1 - y)]

        def part(t, px, py, pc):
            return _window(outs[t], kinds[t], 4 * px + 2 * py + pc, *shards[t].shape)

        def copy(k, t, blk, to, src=None):
            return pltpu.make_async_remote_copy(
                src_ref=part(t, *blk) if src is None else src, dst_ref=part(t, *blk),
                send_sem=send_sems.at[k, t], recv_sem=recv_sems.at[k, t],
                device_id=to, device_id_type=MESH_IDS)

        mine = [pltpu.make_async_copy(ins[t], part(t, *me), local_sems.at[t]) for t in range(n)]
        for cp in mine:
            cp.start()
        first = [copy(0, t, me, sibling, src=ins[t]) for t in range(n)]
        first += [copy(1 + j, t, me, (*chip, c), src=ins[t]) for j, chip in enumerate(chips) for t in range(n)]
        for cp in first:
            cp.start()
        passed = []
        for j, chip in enumerate(chips):
            for t in range(n):
                copy(1 + j, t, (*chip, c), me).wait_recv()
                passed.append(copy(4 + j, t, (*chip, c), sibling))
                passed[-1].start()
        for t in range(n):
            copy(0, t, sibling, me).wait_recv()
        for j, chip in enumerate(chips):
            for t in range(n):
                copy(4 + j, t, (*chip, 1 - c), me).wait_recv()
        for cp in first + passed:
            cp.wait_send()
        for cp in mine:
            cp.wait()

    any_space = pl.BlockSpec(memory_space=pl.ANY)
    return pl.pallas_call(
        body, name="all_gather",
        out_shape=tuple(jax.ShapeDtypeStruct(_gathered_shape(a.shape, k), a.dtype) for a, k in zip(shards, kinds)),
        in_specs=[any_space] * n,
        out_specs=tuple([any_space] * n),
        scratch_shapes=[pltpu.SemaphoreType.DMA((7, n)), pltpu.SemaphoreType.DMA((7, n)),
                        pltpu.SemaphoreType.DMA((n,))],
    )(*shards)


class _Exchange:
    def __init__(self, arrays, kinds, shard_shapes, gather):
        self.arrays, self.kinds, self.shard_shapes, self.gather = list(arrays), list(kinds), list(shard_shapes), gather
        self.n = n = len(self.arrays)
        self.any_specs = [pl.BlockSpec(memory_space=pl.ANY)] * n
        if gather:
            shapes = [_gathered_shape(a.shape, k) for a, k in zip(self.arrays, kinds)]
        else:
            shapes = [(N_DEV,) + tuple(s) for s in shard_shapes]
        self.out_shapes = [jax.ShapeDtypeStruct(s, a.dtype) for s, a in zip(shapes, self.arrays)]
        self.scratch = [pltpu.SemaphoreType.DMA((N_DEV - 1, n)), pltpu.SemaphoreType.DMA((N_DEV - 1, n)),
                        pltpu.SemaphoreType.DMA((n,))]

    def copies(self, ins, outs, sems):
        send_sems, recv_sems, local_sems = sems
        x, y, c = lax.axis_index("x"), lax.axis_index("y"), lax.axis_index("c")
        my = 4 * x + 2 * y + c

        def src(t, receiver):
            if self.gather or self.kinds[t] == "all":
                return ins[t]
            return _window(ins[t], self.kinds[t], receiver, *self.shard_shapes[t])

        def dst(t, sender):
            if self.gather:
                return _window(outs[t], self.kinds[t], sender, *self.shard_shapes[t])
            return outs[t].at[sender]

        local = [pltpu.make_async_copy(src(t, my), dst(t, my), local_sems.at[t]) for t in range(self.n)]
        sends, arrivals = [], []
        for rel in range(1, N_DEV):
            px, py, pc = x ^ (rel >> 2), y ^ ((rel >> 1) & 1), c ^ (rel & 1)
            peer = 4 * px + 2 * py + pc
            for t in range(self.n):
                common = dict(send_sem=send_sems.at[rel - 1, t], recv_sem=recv_sems.at[rel - 1, t],
                              device_id=(px, py, pc), device_id_type=MESH_IDS)
                sends.append(pltpu.make_async_remote_copy(src_ref=src(t, peer), dst_ref=dst(t, my), **common))
                arrivals.append(pltpu.make_async_remote_copy(src_ref=src(t, my), dst_ref=dst(t, peer), **common))
        return local, sends, arrivals

    def start(self, ins, outs, sems):
        local, sends, _ = self.copies(ins, outs, sems)
        for cp in local + sends:
            cp.start()

    def wait(self, ins, outs, sems):
        local, sends, arrivals = self.copies(ins, outs, sems)
        for cp in arrivals:
            cp.wait_recv()
        for cp in sends:
            cp.wait_send()
        for cp in local:
            cp.wait()


def _exchange(grads, kinds, shard_shapes):
    ex = _Exchange(grads, kinds, shard_shapes, gather=False)
    n = ex.n

    def body(*refs):
        ins, outs, sems = refs[:n], refs[n:2 * n], refs[2 * n:]
        ex.start(ins, outs, sems)
        ex.wait(ins, outs, sems)

    return pl.pallas_call(
        body, name="grad_exchange",
        out_shape=tuple(ex.out_shapes),
        in_specs=ex.any_specs,
        out_specs=tuple(ex.any_specs),
        scratch_shapes=ex.scratch,
    )(*grads)


def _sum_adamw(parts, w, m, v, name):
    rows, cols = w.shape
    n, rows_p, cols_p = parts.shape
    tr = _tile(rows, BLK, SUBLANES) if rows > BLK else rows
    tp = tr if rows_p == rows else rows_p
    c1 = 1.0 - ADAM_B1 ** ADAM_STEP
    c2 = 1.0 - ADAM_B2 ** ADAM_STEP

    def body(p_ref, w_ref, m_ref, v_ref, g_ref, d_ref, nm_ref, nv_ref):
        gv = p_ref[0, 0:tr, 0:cols].astype(F32)
        for s in range(1, n):
            gv = gv + p_ref[s, 0:tr, 0:cols].astype(F32)
        g_ref[...] = gv
        nm = ADAM_B1 * m_ref[...] + (1.0 - ADAM_B1) * gv
        nv = ADAM_B2 * v_ref[...] + (1.0 - ADAM_B2) * (gv * gv)
        m_hat = nm / c1
        v_hat = nv / c2
        d_ref[...] = -ADAM_LR * (m_hat / (jnp.sqrt(v_hat) + ADAM_EPS) + ADAM_WD * w_ref[...])
        nm_ref[...] = nm
        nv_ref[...] = nv

    spec = pl.BlockSpec((tr, cols), lambda i: (i, 0))
    out = jax.ShapeDtypeStruct((rows, cols), F32)
    return pl.pallas_call(
        body, name=name,
        out_shape=(out, out, out, out),
        grid=(rows // tr,),
        in_specs=[pl.BlockSpec((n, tp, cols_p), lambda i: (0, i, 0)), spec, spec, spec],
        out_specs=(spec, spec, spec, spec),
        compiler_params=_params(("parallel",)),
    )(parts, w, m, v)


def _pad2(a, rows, cols):
    return jnp.pad(a, ((0, rows - a.shape[0]), (0, cols - a.shape[1])))


WEIGHTS = ["meta_tokens", "norm_gains", "w_in", "b_forget", "w_o_fox", "w_o_sb", "w_out", "w_up", "conv_w",
           "conv_b", "w_down"]


def kernel(x, meta_tokens, norm_gains, w_in, b_forget, w_o_fox, w_o_sb, w_out, w_up, conv_w, conv_b, w_down, loss_target, m_meta_tokens, m_norm_gains, m_w_in, m_b_forget, m_w_o_fox, m_w_o_sb, m_w_out, m_w_up, m_conv_w, m_conv_b, m_w_down, v_meta_tokens, v_norm_gains, v_w_in, v_b_forget, v_w_o_fox, v_w_o_sb, v_w_out, v_w_up, v_conv_w, v_conv_b, v_w_down):
    w = dict(meta_tokens=meta_tokens, norm_gains=norm_gains, w_in=w_in, b_forget=b_forget, w_o_fox=w_o_fox,
             w_o_sb=w_o_sb, w_out=w_out, w_up=w_up, conv_w=conv_w, conv_b=conv_b, w_down=w_down)
    mom = dict(meta_tokens=m_meta_tokens, norm_gains=m_norm_gains, w_in=m_w_in, b_forget=m_b_forget,
               w_o_fox=m_w_o_fox, w_o_sb=m_w_o_sb, w_out=m_w_out, w_up=m_w_up, conv_w=m_conv_w, conv_b=m_conv_b,
               w_down=m_w_down)
    vel = dict(meta_tokens=v_meta_tokens, norm_gains=v_norm_gains, w_in=v_w_in, b_forget=v_b_forget,
               w_o_fox=v_w_o_fox, w_o_sb=v_w_o_sb, w_out=v_w_out, w_up=v_w_up, conv_w=v_conv_w, conv_b=v_conv_b,
               w_down=v_w_down)
    w2 = {n: a.reshape(a.shape[-2:]) for n, a in w.items()}
    shard_shape = {n: a.shape for n, a in w2.items()}

    d = x.shape[-1]
    up_cols = shard_shape["w_up"][1]
    up_pad = -(-up_cols // LANES) * LANES
    half = N_DEV // 2
    pad_rows = lambda a: _pad2(a, SUBLANES, a.shape[1])

    shards = [
        ("w_in", "slots", w2["w_in"].astype(BF16)),
        ("meta_tokens", "cols", w2["meta_tokens"]),
        ("norm_gains", "cols", pad_rows(w2["norm_gains"])),
    ]
    later = [
        ("w_o_fox", "cols", w2["w_o_fox"].astype(BF16)),
        ("w_o_sb", "cols", w2["w_o_sb"].astype(BF16)),
        ("w_out", "rows", w2["w_out"].astype(BF16)),
        ("w_up", "gate_value", _pad2(w2["w_up"], d, up_pad).astype(BF16)),
        ("conv_w", "gate_value", _pad2(w2["conv_w"], SUBLANES, up_pad)),
        ("w_down", "rows", w2["w_down"].astype(BF16)),
    ]
    full = dict(zip([s[0] for s in shards], _all_gather([s[2] for s in shards], [s[1] for s in shards])))
    w_in_full = jnp.concatenate([full["w_in"][i] for i in range(N_DEV)], axis=1)
    conv_b_p = jnp.pad(w2["conv_b"].reshape(2, half, up_cols), ((0, 0), (0, 0), (0, up_pad - up_cols)))
    conv_b_p = conv_b_p.transpose(1, 0, 2)

    def finish_gather(gathered):
        w_fox, w_sb, w_out_full, w_up_p, conv_w_p, w_down_full = gathered
        w_down_p = jnp.pad(w_down_full.reshape(half, up_cols, d), ((0, 0), (0, up_pad - up_cols), (0, 0)))
        return w_fox, w_sb, w_out_full, w_up_p, conv_w_p[:3], w_down_p.reshape(half * up_pad, d)

    early_names = ["w_o_fox", "w_o_sb", "w_out", "w_up", "conv_w", "conv_b", "w_down"]

    def early_exchange(g):
        sends = {
            "w_o_fox": ("cols", g["w_o_fox"], shard_shape["w_o_fox"]),
            "w_o_sb": ("cols", g["w_o_sb"], shard_shape["w_o_sb"]),
            "w_out": ("rows", g["w_out"], shard_shape["w_out"]),
            "w_up": ("gate_value", g["w_up"], (d, up_pad)),
            "conv_w": ("gate_value", pad_rows(g["conv_w"]), (SUBLANES, up_pad)),
            "conv_b": ("all", pad_rows(g["conv_b"].reshape(half, 2, up_pad).transpose(1, 0, 2)[:, :, :up_cols]
                                       .reshape(1, -1)),
                       (SUBLANES, N_DEV * up_cols)),
            "w_down": ("rows", g["w_down"].reshape(half, up_pad, d)[:, :up_cols].reshape(half * up_cols, d),
                       shard_shape["w_down"]),
        }
        return _Exchange([sends[n][1] for n in early_names], [sends[n][0] for n in early_names],
                         [sends[n][2] for n in early_names], gather=False)

    def w_in_exchange(g):
        in_cols = shard_shape["w_in"][1]
        return _Exchange([jnp.stack([g[:, i * in_cols:(i + 1) * in_cols] for i in range(N_DEV)])], ["slots"],
                         [shard_shape["w_in"]], gather=False)

    late_weights = (_Exchange([s[2] for s in later], [s[1] for s in later], [s[2].shape for s in later], gather=True),
                    finish_gather)
    loss, grad_x, grads = _local_step(
        x[0], loss_target[0], full["meta_tokens"], full["norm_gains"][:4], w_in_full, w2["b_forget"],
        None, None, None, None, None, conv_b_p.reshape(1, N_DEV * up_pad), None, ffn_block=up_pad,
        late_weights=late_weights, early_grads=early_exchange, w_in_grad=w_in_exchange)
    loss = lax.psum(loss, ("x", "y", "c"))

    late_names = ["meta_tokens", "norm_gains", "b_forget"]
    sends = {
        "meta_tokens": ("cols", grads["meta_tokens"], (N_META, LANES)),
        "norm_gains": ("cols", pad_rows(grads["norm_gains"]), (SUBLANES, LANES)),
        "b_forget": ("all", _pad2(grads["b_forget"], SUBLANES, LANES), (SUBLANES, LANES)),
    }
    parts = dict(zip(late_names, _exchange([sends[n][1] for n in late_names], [sends[n][0] for n in late_names],
                                           [sends[n][2] for n in late_names])))
    parts.update(zip(early_names, grads["early_parts"]))
    parts["w_in"], = grads["w_in_parts"]

    grad, delta, new_m, new_v = {}, {}, {}, {}
    for n in WEIGHTS:
        shape = w[n].shape
        outs = _sum_adamw(parts[n], w2[n], mom[n].reshape(shard_shape[n]), vel[n].reshape(shard_shape[n]),
                          "adamw_" + n)
        grad[n], delta[n], new_m[n], new_v[n] = (o.reshape(shape) for o in outs)

    return (loss, grad_x[None], *[grad[n] for n in WEIGHTS], *[delta[n] for n in WEIGHTS],
            *[new_m[n] for n in WEIGHTS], *[new_v[n] for n in WEIGHTS])
```
